```python
import jax, jax.numpy as jnp
from jax import lax
import numpy as np

D_MODEL = 2048
BATCH = 8
SEQ = 4096
DEPTH = 1

GRID_W = 64
PLE_DIM = 256
D_ATTN = 1024
N_HEADS = 8
N_KV_HEADS = 2
HEAD_DIM = D_ATTN // N_HEADS
ROPE_THETA = 10000.0
Q_BLOCK = 128
D_SSM = 1024
SSM_GROUP = 16
N_SSM_GROUPS = D_SSM // SSM_GROUP
SSM_STATE = 64
DT_MIN = 0.001
DT_MAX = 0.1
D_MIX = D_ATTN + D_SSM
D_KV = N_KV_HEADS * HEAD_DIM
IN_SPLITS = (D_ATTN, D_KV, D_KV, D_ATTN, D_SSM, D_SSM)
D_IN = sum(IN_SPLITS)
EPS = 1e-6

kernel_name = "hybrid_gqa_axialrope_bis5_block"


def rms_norm(x, g):
    xf = x.astype(jnp.float32)
    y = xf * lax.rsqrt(jnp.mean(xf * xf, axis=-1, keepdims=True) + EPS)
    return (y * g.astype(jnp.float32)).astype(x.dtype)


def _rotate(x, ang):
    x1, x2 = jnp.split(x, 2, axis=-1)
    c = jnp.cos(ang)[None, :, None, :]
    s = jnp.sin(ang)[None, :, None, :]
    return jnp.concatenate([x1 * c - x2 * s, x2 * c + x1 * s], axis=-1)


def axial_rope(x, ang_row, ang_col):
    xf = x.astype(jnp.float32)
    half = x.shape[-1] // 2
    out = jnp.concatenate([_rotate(xf[..., :half], ang_row),
                           _rotate(xf[..., half:], ang_col)], axis=-1)
    return out.astype(x.dtype)


def grid_angles(L):
    rows_n = L // GRID_W
    rows = jnp.repeat(jnp.arange(rows_n), GRID_W).astype(jnp.float32)
    cols = jnp.tile(jnp.arange(GRID_W), rows_n).astype(jnp.float32)
    n_freq = HEAD_DIM // 4
    inv_freq = ROPE_THETA ** (-jnp.arange(n_freq, dtype=jnp.float32) / n_freq)
    return rows[:, None] * inv_freq[None, :], cols[:, None] * inv_freq[None, :]


def block_attention(q, k, v):
    bsz, L, H, Dh = q.shape
    kv = k.shape[2]
    rep = H // kv
    nb = L // Q_BLOCK
    scale = Dh ** -0.5
    qb = q.reshape(bsz, nb, Q_BLOCK, kv, rep, Dh).transpose(1, 0, 2, 3, 4, 5)

    def one_block(qblk):
        s = jnp.einsum('bqkrd,bskd->bkrqs', qblk, k).astype(jnp.float32) * scale
        pr = jax.nn.softmax(s, axis=-1).astype(v.dtype)
        return jnp.einsum('bkrqs,bskd->bqkrd', pr, v)

    o = lax.map(one_block, qb)
    return o.transpose(1, 0, 2, 3, 4, 5).reshape(bsz, L, H * Dh)


def _lin_combine(e1, e2):
    a1, b1 = e1
    a2, b2 = e2
    return a1 * a2, a2 * b1 + b2


def s5_bidirectional(u, a_re, a_im, log_dt, b_re, b_im, c_re, c_im, d):
    bsz, L, _ = u.shape
    ug = u.astype(jnp.float32).reshape(bsz, L, N_SSM_GROUPS, SSM_GROUP)
    uc = ug.astype(jnp.complex64)
    y = d.astype(jnp.float32).reshape(N_SSM_GROUPS, SSM_GROUP) * ug
    for direction in range(2):
        lam = lax.complex(jnp.minimum(a_re[direction].astype(jnp.float32), -1e-4),
                          a_im[direction].astype(jnp.float32))
        dt = jnp.exp(log_dt[direction].astype(jnp.float32))[:, None]
        lam_bar = jnp.exp(lam * dt)
        bmat = lax.complex(b_re[direction].astype(jnp.float32),
                           b_im[direction].astype(jnp.float32))
        b_bar = ((lam_bar - 1.0) / lam)[..., None] * bmat
        bu = jnp.einsum('blgh,gph->blgp', uc, b_bar)
        a = jnp.broadcast_to(lam_bar, bu.shape)
        _, xs = lax.associative_scan(_lin_combine, (a, bu), axis=1,
                                     reverse=(direction == 1))
        cmat = lax.complex(c_re[direction].astype(jnp.float32),
                           c_im[direction].astype(jnp.float32))
        y = y + jnp.real(jnp.einsum('blgp,ghp->blgh', xs, cmat))
    return y.reshape(bsz, L, D_SSM)


def _fwd_setup_inputs(seed: int = 0) -> dict:
    key = jax.random.key(seed)
    ks = jax.random.split(key, 24)
    f32 = jnp.float32
    G, P, H = N_SSM_GROUPS, SSM_STATE, SSM_GROUP
    nrm = lambda k, shape, s: jax.random.normal(k, shape, f32) * s
    x = jax.random.normal(ks[0], (BATCH, SEQ, D_MODEL), f32)
    p = jax.random.normal(ks[1], (DEPTH, BATCH, SEQ, PLE_DIM), f32)
    norm_mix = 1.0 + nrm(ks[2], (DEPTH, D_MODEL), 0.02)
    w_in = nrm(ks[3], (DEPTH, D_MODEL, D_IN), D_MODEL ** -0.5)
    q_norm = 1.0 + nrm(ks[4], (DEPTH, HEAD_DIM), 0.02)
    k_norm = 1.0 + nrm(ks[5], (DEPTH, HEAD_DIM), 0.02)
    ssm_a_re = -0.5 + nrm(ks[6], (DEPTH, 2, G, P), 0.01)
    ssm_a_im = (np.pi * jnp.arange(P, dtype=f32))[None, None, None, :] + nrm(ks[7], (DEPTH, 2, G, P), 0.01)
    ssm_log_dt = jax.random.uniform(ks[8], (DEPTH, 2, G), f32,
                                    minval=float(np.log(DT_MIN)), maxval=float(np.log(DT_MAX)))
    ssm_b_re = nrm(ks[9], (DEPTH, 2, G, P, H), (0.5 / H) ** 0.5)
    ssm_b_im = nrm(ks[10], (DEPTH, 2, G, P, H), (0.5 / H) ** 0.5)
    ssm_c_re = nrm(ks[11], (DEPTH, 2, G, H, P), (0.5 / P) ** 0.5)
    ssm_c_im = nrm(ks[12], (DEPTH, 2, G, H, P), (0.5 / P) ** 0.5)
    ssm_d = nrm(ks[13], (DEPTH, D_SSM), 1.0)
    w_glu = nrm(ks[14], (DEPTH, D_SSM, 2 * D_SSM), D_SSM ** -0.5)
    b_glu = nrm(ks[15], (DEPTH, 2 * D_SSM), 0.01)
    w_out = nrm(ks[16], (DEPTH, D_MIX, D_MODEL), D_MIX ** -0.5)
    norm_ple = 1.0 + nrm(ks[17], (DEPTH, D_MODEL), 0.02)
    w_ple_gate = nrm(ks[18], (DEPTH, D_MODEL, D_MODEL), D_MODEL ** -0.5)
    w_ple_proj = nrm(ks[19], (DEPTH, PLE_DIM, D_MODEL), PLE_DIM ** -0.5)
    norm_final = 1.0 + nrm(ks[20], (D_MODEL,), 0.02)
    return {"x": x, "p": p, "norm_mix": norm_mix, "w_in": w_in,
            "q_norm": q_norm, "k_norm": k_norm,
            "ssm_a_re": ssm_a_re, "ssm_a_im": ssm_a_im, "ssm_log_dt": ssm_log_dt,
            "ssm_b_re": ssm_b_re, "ssm_b_im": ssm_b_im,
            "ssm_c_re": ssm_c_re, "ssm_c_im": ssm_c_im, "ssm_d": ssm_d,
            "w_glu": w_glu, "b_glu": b_glu, "w_out": w_out,
            "norm_ple": norm_ple, "w_ple_gate": w_ple_gate, "w_ple_proj": w_ple_proj,
            "norm_final": norm_final}


def _fwd_reference(x, p, norm_mix, w_in, q_norm, k_norm, ssm_a_re, ssm_a_im, ssm_log_dt,
              ssm_b_re, ssm_b_im, ssm_c_re, ssm_c_im, ssm_d, w_glu, b_glu, w_out,
              norm_ple, w_ple_gate, w_ple_proj, norm_final):
    bsz, L, _ = x.shape
    ang_row, ang_col = grid_angles(L)
    split_idx = list(np.cumsum(IN_SPLITS)[:-1])
    h = x
    for i in range(DEPTH):
        hn = rms_norm(h, norm_mix[i])
        z = hn @ w_in[i]
        q, k, v, gate_a, u, gate_s = jnp.split(z, split_idx, axis=-1)

        q = q.reshape(bsz, L, N_HEADS, HEAD_DIM)
        k = k.reshape(bsz, L, N_KV_HEADS, HEAD_DIM)
        v = v.reshape(bsz, L, N_KV_HEADS, HEAD_DIM)
        q = axial_rope(rms_norm(q, q_norm[i]), ang_row, ang_col)
        k = axial_rope(rms_norm(k, k_norm[i]), ang_row, ang_col)
        y_attn = block_attention(q, k, v) * jax.nn.silu(gate_a)

        y_ssm = s5_bidirectional(u, ssm_a_re[i], ssm_a_im[i], ssm_log_dt[i],
                                 ssm_b_re[i], ssm_b_im[i], ssm_c_re[i], ssm_c_im[i],
                                 ssm_d[i])
        y_ssm = jax.nn.gelu(y_ssm).astype(x.dtype)
        glu = y_ssm @ w_glu[i] + b_glu[i]
        y_ssm = glu[..., :D_SSM] * jax.nn.sigmoid(glu[..., D_SSM:])
        y_ssm = y_ssm * jax.nn.silu(gate_s)

        h = h + jnp.concatenate([y_attn, y_ssm], axis=-1) @ w_out[i]

        gate = jax.nn.sigmoid(rms_norm(h, norm_ple[i]) @ w_ple_gate[i])
        h = h + gate * (p[i] @ w_ple_proj[i])
    return rms_norm(h, norm_final)


import jax as _jax
import jax.numpy as _jnp

TWIN_FORMAT = 'train_step'
FWD_PARAMS = ['x', 'p', 'norm_mix', 'w_in', 'q_norm', 'k_norm', 'ssm_a_re', 'ssm_a_im', 'ssm_log_dt', 'ssm_b_re', 'ssm_b_im', 'ssm_c_re', 'ssm_c_im', 'ssm_d', 'w_glu', 'b_glu', 'w_out', 'norm_ple', 'w_ple_gate', 'w_ple_proj', 'norm_final']
TWIN_WEIGHTS = ['norm_mix', 'w_in', 'q_norm', 'k_norm', 'ssm_a_re', 'ssm_a_im', 'ssm_log_dt', 'ssm_b_re', 'ssm_b_im', 'ssm_c_re', 'ssm_c_im', 'ssm_d', 'w_glu', 'b_glu', 'w_out', 'norm_ple', 'w_ple_gate', 'w_ple_proj', 'norm_final']
TWIN_DIFF_INPUT = 'x'
TWIN_INPUTS = ['x', 'p', 'norm_mix', 'w_in', 'q_norm', 'k_norm', 'ssm_a_re', 'ssm_a_im', 'ssm_log_dt', 'ssm_b_re', 'ssm_b_im', 'ssm_c_re', 'ssm_c_im', 'ssm_d', 'w_glu', 'b_glu', 'w_out', 'norm_ple', 'w_ple_gate', 'w_ple_proj', 'norm_final', 'loss_target', 'm_norm_mix', 'm_w_in', 'm_q_norm', 'm_k_norm', 'm_ssm_a_re', 'm_ssm_a_im', 'm_ssm_log_dt', 'm_ssm_b_re', 'm_ssm_b_im', 'm_ssm_c_re', 'm_ssm_c_im', 'm_ssm_d', 'm_w_glu', 'm_b_glu', 'm_w_out', 'm_norm_ple', 'm_w_ple_gate', 'm_w_ple_proj', 'm_norm_final', 'v_norm_mix', 'v_w_in', 'v_q_norm', 'v_k_norm', 'v_ssm_a_re', 'v_ssm_a_im', 'v_ssm_log_dt', 'v_ssm_b_re', 'v_ssm_b_im', 'v_ssm_c_re', 'v_ssm_c_im', 'v_ssm_d', 'v_w_glu', 'v_b_glu', 'v_w_out', 'v_norm_ple', 'v_w_ple_gate', 'v_w_ple_proj', 'v_norm_final']
TWIN_OUTPUTS = ['loss', 'grad_x', 'grad_norm_mix', 'grad_w_in', 'grad_q_norm', 'grad_k_norm', 'grad_ssm_a_re', 'grad_ssm_a_im', 'grad_ssm_log_dt', 'grad_ssm_b_re', 'grad_ssm_b_im', 'grad_ssm_c_re', 'grad_ssm_c_im', 'grad_ssm_d', 'grad_w_glu', 'grad_b_glu', 'grad_w_out', 'grad_norm_ple', 'grad_w_ple_gate', 'grad_w_ple_proj', 'grad_norm_final', 'delta_norm_mix', 'delta_w_in', 'delta_q_norm', 'delta_k_norm', 'delta_ssm_a_re', 'delta_ssm_a_im', 'delta_ssm_log_dt', 'delta_ssm_b_re', 'delta_ssm_b_im', 'delta_ssm_c_re', 'delta_ssm_c_im', 'delta_ssm_d', 'delta_w_glu', 'delta_b_glu', 'delta_w_out', 'delta_norm_ple', 'delta_w_ple_gate', 'delta_w_ple_proj', 'delta_norm_final', 'new_m_norm_mix', 'new_m_w_in', 'new_m_q_norm', 'new_m_k_norm', 'new_m_ssm_a_re', 'new_m_ssm_a_im', 'new_m_ssm_log_dt', 'new_m_ssm_b_re', 'new_m_ssm_b_im', 'new_m_ssm_c_re', 'new_m_ssm_c_im', 'new_m_ssm_d', 'new_m_w_glu', 'new_m_b_glu', 'new_m_w_out', 'new_m_norm_ple', 'new_m_w_ple_gate', 'new_m_w_ple_proj', 'new_m_norm_final', 'new_v_norm_mix', 'new_v_w_in', 'new_v_q_norm', 'new_v_k_norm', 'new_v_ssm_a_re', 'new_v_ssm_a_im', 'new_v_ssm_log_dt', 'new_v_ssm_b_re', 'new_v_ssm_b_im', 'new_v_ssm_c_re', 'new_v_ssm_c_im', 'new_v_ssm_d', 'new_v_w_glu', 'new_v_b_glu', 'new_v_w_out', 'new_v_norm_ple', 'new_v_w_ple_gate', 'new_v_w_ple_proj', 'new_v_norm_final']
TWIN_LEAF_KINDS = {'loss': 'loss', 'grad_x': 'grad_x', 'grad_norm_mix': 'grad_w', 'grad_w_in': 'grad_w', 'grad_q_norm': 'grad_w', 'grad_k_norm': 'grad_w', 'grad_ssm_a_re': 'grad_w', 'grad_ssm_a_im': 'grad_w', 'grad_ssm_log_dt': 'grad_w', 'grad_ssm_b_re': 'grad_w', 'grad_ssm_b_im': 'grad_w', 'grad_ssm_c_re': 'grad_w', 'grad_ssm_c_im': 'grad_w', 'grad_ssm_d': 'grad_w', 'grad_w_glu': 'grad_w', 'grad_b_glu': 'grad_w', 'grad_w_out': 'grad_w', 'grad_norm_ple': 'grad_w', 'grad_w_ple_gate': 'grad_w', 'grad_w_ple_proj': 'grad_w', 'grad_norm_final': 'grad_w', 'delta_norm_mix': 'delta_w', 'delta_w_in': 'delta_w', 'delta_q_norm': 'delta_w', 'delta_k_norm': 'delta_w', 'delta_ssm_a_re': 'delta_w', 'delta_ssm_a_im': 'delta_w', 'delta_ssm_log_dt': 'delta_w', 'delta_ssm_b_re': 'delta_w', 'delta_ssm_b_im': 'delta_w', 'delta_ssm_c_re': 'delta_w', 'delta_ssm_c_im': 'delta_w', 'delta_ssm_d': 'delta_w', 'delta_w_glu': 'delta_w', 'delta_b_glu': 'delta_w', 'delta_w_out': 'delta_w', 'delta_norm_ple': 'delta_w', 'delta_w_ple_gate': 'delta_w', 'delta_w_ple_proj': 'delta_w', 'delta_norm_final': 'delta_w', 'new_m_norm_mix': 'new_m', 'new_m_w_in': 'new_m', 'new_m_q_norm': 'new_m', 'new_m_k_norm': 'new_m', 'new_m_ssm_a_re': 'new_m', 'new_m_ssm_a_im': 'new_m', 'new_m_ssm_log_dt': 'new_m', 'new_m_ssm_b_re': 'new_m', 'new_m_ssm_b_im': 'new_m', 'new_m_ssm_c_re': 'new_m', 'new_m_ssm_c_im': 'new_m', 'new_m_ssm_d': 'new_m', 'new_m_w_glu': 'new_m', 'new_m_b_glu': 'new_m', 'new_m_w_out': 'new_m', 'new_m_norm_ple': 'new_m', 'new_m_w_ple_gate': 'new_m', 'new_m_w_ple_proj': 'new_m', 'new_m_norm_final': 'new_m', 'new_v_norm_mix': 'new_v', 'new_v_w_in': 'new_v', 'new_v_q_norm': 'new_v', 'new_v_k_norm': 'new_v', 'new_v_ssm_a_re': 'new_v', 'new_v_ssm_a_im': 'new_v', 'new_v_ssm_log_dt': 'new_v', 'new_v_ssm_b_re': 'new_v', 'new_v_ssm_b_im': 'new_v', 'new_v_ssm_c_re': 'new_v', 'new_v_ssm_c_im': 'new_v', 'new_v_ssm_d': 'new_v', 'new_v_w_glu': 'new_v', 'new_v_b_glu': 'new_v', 'new_v_w_out': 'new_v', 'new_v_norm_ple': 'new_v', 'new_v_w_ple_gate': 'new_v', 'new_v_w_ple_proj': 'new_v', 'new_v_norm_final': 'new_v'}


def _forward(args):
    return _fwd_reference(*[args[k] for k in FWD_PARAMS])


def _output_shape():
    def fwd():
        inp = _fwd_setup_inputs(0)
        return _fwd_reference(*[inp[k] for k in FWD_PARAMS])
    out = _jax.eval_shape(fwd)
    return out.shape, out.dtype

N_MICROBATCH = 1
ADAM_LR = 0.001
ADAM_B1 = 0.9
ADAM_B2 = 0.999
ADAM_EPS = 1e-08
ADAM_WD = 0.01
ADAM_STEP = 10
PER_EXAMPLE_BATCH_AXIS = {'x': 0, 'p': 1, 'loss_target': 0}
SHARED_INPUTS = []
_WEIGHT_DTYPES = {'norm_mix': _jnp.float32, 'w_in': _jnp.float32, 'q_norm': _jnp.float32, 'k_norm': _jnp.float32, 'ssm_a_re': _jnp.float32, 'ssm_a_im': _jnp.float32, 'ssm_log_dt': _jnp.float32, 'ssm_b_re': _jnp.float32, 'ssm_b_im': _jnp.float32, 'ssm_c_re': _jnp.float32, 'ssm_c_im': _jnp.float32, 'ssm_d': _jnp.float32, 'w_glu': _jnp.float32, 'b_glu': _jnp.float32, 'w_out': _jnp.float32, 'norm_ple': _jnp.float32, 'w_ple_gate': _jnp.float32, 'w_ple_proj': _jnp.float32, 'norm_final': _jnp.float32}
MOMENT_SCALE = {'norm_mix': 1.939103e-02, 'w_in': 1.267600e-02, 'q_norm': 1.390177e-02, 'k_norm': 1.425357e-02, 'ssm_a_re': 8.490417e-04, 'ssm_a_im': 8.441685e-04, 'ssm_log_dt': 6.465298e-01, 'ssm_b_re': 5.641902e-04, 'ssm_b_im': 5.646076e-04, 'ssm_c_re': 1.122055e-03, 'ssm_c_im': 1.144986e-03, 'ssm_d': 1.776158e-02, 'w_glu': 1.265199e-02, 'b_glu': 1.861813e-02, 'w_out': 1.244792e-02, 'norm_ple': 1.686064e-02, 'w_ple_gate': 1.638565e-02, 'w_ple_proj': 4.226239e-02, 'norm_final': 1.601217e+01}


def _to_microbatches(a, axis):
    t = _jnp.moveaxis(a, axis, 0)
    t = t.reshape((N_MICROBATCH, t.shape[0] // N_MICROBATCH) + t.shape[1:])
    return _jnp.moveaxis(t, 1, axis + 1)


def setup_inputs(seed: int = 0) -> dict:
    inp = _fwd_setup_inputs(seed)
    key = _jax.random.fold_in(_jax.random.key(seed), 7919)
    shape, _ = _output_shape()
    out = dict(inp)
    out["loss_target"] = _jax.random.normal(_jax.random.fold_in(key, 0), shape, _jnp.float32)
    for i, name in enumerate(TWIN_WEIGHTS):
        w = inp[name].astype(_jnp.float32)
        if MOMENT_SCALE is None:
            s = _jnp.sqrt(_jnp.mean(_jnp.square(w)) + 1e-30)
        else:
            s = MOMENT_SCALE[name]
        km, kv = _jax.random.split(_jax.random.fold_in(key, i + 1))
        out[name] = w
        out["m_" + name] = s * _jax.random.normal(km, w.shape, _jnp.float32)
        out["v_" + name] = (s * s) * _jax.random.uniform(kv, w.shape, _jnp.float32, 0.5, 1.5)
    if N_MICROBATCH > 1:
        for name, axis in PER_EXAMPLE_BATCH_AXIS.items():
            out[name] = _to_microbatches(out[name], axis)
    return {'x': out['x'], 'p': out['p'], 'norm_mix': out['norm_mix'], 'w_in': out['w_in'], 'q_norm': out['q_norm'], 'k_norm': out['k_norm'], 'ssm_a_re': out['ssm_a_re'], 'ssm_a_im': out['ssm_a_im'], 'ssm_log_dt': out['ssm_log_dt'], 'ssm_b_re': out['ssm_b_re'], 'ssm_b_im': out['ssm_b_im'], 'ssm_c_re': out['ssm_c_re'], 'ssm_c_im': out['ssm_c_im'], 'ssm_d': out['ssm_d'], 'w_glu': out['w_glu'], 'b_glu': out['b_glu'], 'w_out': out['w_out'], 'norm_ple': out['norm_ple'], 'w_ple_gate': out['w_ple_gate'], 'w_ple_proj': out['w_ple_proj'], 'norm_final': out['norm_final'], 'loss_target': out['loss_target'], 'm_norm_mix': out['m_norm_mix'], 'm_w_in': out['m_w_in'], 'm_q_norm': out['m_q_norm'], 'm_k_norm': out['m_k_norm'], 'm_ssm_a_re': out['m_ssm_a_re'], 'm_ssm_a_im': out['m_ssm_a_im'], 'm_ssm_log_dt': out['m_ssm_log_dt'], 'm_ssm_b_re': out['m_ssm_b_re'], 'm_ssm_b_im': out['m_ssm_b_im'], 'm_ssm_c_re': out['m_ssm_c_re'], 'm_ssm_c_im': out['m_ssm_c_im'], 'm_ssm_d': out['m_ssm_d'], 'm_w_glu': out['m_w_glu'], 'm_b_glu': out['m_b_glu'], 'm_w_out': out['m_w_out'], 'm_norm_ple': out['m_norm_ple'], 'm_w_ple_gate': out['m_w_ple_gate'], 'm_w_ple_proj': out['m_w_ple_proj'], 'm_norm_final': out['m_norm_final'], 'v_norm_mix': out['v_norm_mix'], 'v_w_in': out['v_w_in'], 'v_q_norm': out['v_q_norm'], 'v_k_norm': out['v_k_norm'], 'v_ssm_a_re': out['v_ssm_a_re'], 'v_ssm_a_im': out['v_ssm_a_im'], 'v_ssm_log_dt': out['v_ssm_log_dt'], 'v_ssm_b_re': out['v_ssm_b_re'], 'v_ssm_b_im': out['v_ssm_b_im'], 'v_ssm_c_re': out['v_ssm_c_re'], 'v_ssm_c_im': out['v_ssm_c_im'], 'v_ssm_d': out['v_ssm_d'], 'v_w_glu': out['v_w_glu'], 'v_b_glu': out['v_b_glu'], 'v_w_out': out['v_w_out'], 'v_norm_ple': out['v_norm_ple'], 'v_w_ple_gate': out['v_w_ple_gate'], 'v_w_ple_proj': out['v_w_ple_proj'], 'v_norm_final': out['v_norm_final']}


def _loss(weights, diff, rest, loss_target):
    with _jax.named_scope("forward"):
        args = {**rest, TWIN_DIFF_INPUT: diff, **{k: w.astype(_WEIGHT_DTYPES[k]) for k, w in weights.items()}}
        y = _forward(args)
    with _jax.named_scope("loss_head"):
        err = _jnp.square(y.astype(_jnp.float32) - loss_target)
        return 0.5 * _jnp.sum(_jnp.mean(err, axis=-1)) if err.ndim else 0.5 * err


def _adamw(w, g, m, v):
    m = ADAM_B1 * m + (1.0 - ADAM_B1) * g
    v = ADAM_B2 * v + (1.0 - ADAM_B2) * _jnp.square(g)
    m_hat = m / (1.0 - ADAM_B1 ** ADAM_STEP)
    v_hat = v / (1.0 - ADAM_B2 ** ADAM_STEP)
    delta = -ADAM_LR * (m_hat / (_jnp.sqrt(v_hat) + ADAM_EPS) + ADAM_WD * w)
    return delta, m, v


def reference(x, p, norm_mix, w_in, q_norm, k_norm, ssm_a_re, ssm_a_im, ssm_log_dt, ssm_b_re, ssm_b_im, ssm_c_re, ssm_c_im, ssm_d, w_glu, b_glu, w_out, norm_ple, w_ple_gate, w_ple_proj, norm_final, loss_target, m_norm_mix, m_w_in, m_q_norm, m_k_norm, m_ssm_a_re, m_ssm_a_im, m_ssm_log_dt, m_ssm_b_re, m_ssm_b_im, m_ssm_c_re, m_ssm_c_im, m_ssm_d, m_w_glu, m_b_glu, m_w_out, m_norm_ple, m_w_ple_gate, m_w_ple_proj, m_norm_final, v_norm_mix, v_w_in, v_q_norm, v_k_norm, v_ssm_a_re, v_ssm_a_im, v_ssm_log_dt, v_ssm_b_re, v_ssm_b_im, v_ssm_c_re, v_ssm_c_im, v_ssm_d, v_w_glu, v_b_glu, v_w_out, v_norm_ple, v_w_ple_gate, v_w_ple_proj, v_norm_final):
    given = dict(x=x, p=p, norm_mix=norm_mix, w_in=w_in, q_norm=q_norm, k_norm=k_norm, ssm_a_re=ssm_a_re, ssm_a_im=ssm_a_im, ssm_log_dt=ssm_log_dt, ssm_b_re=ssm_b_re, ssm_b_im=ssm_b_im, ssm_c_re=ssm_c_re, ssm_c_im=ssm_c_im, ssm_d=ssm_d, w_glu=w_glu, b_glu=b_glu, w_out=w_out, norm_ple=norm_ple, w_ple_gate=w_ple_gate, w_ple_proj=w_ple_proj, norm_final=norm_final, loss_target=loss_target, m_norm_mix=m_norm_mix, m_w_in=m_w_in, m_q_norm=m_q_norm, m_k_norm=m_k_norm, m_ssm_a_re=m_ssm_a_re, m_ssm_a_im=m_ssm_a_im, m_ssm_log_dt=m_ssm_log_dt, m_ssm_b_re=m_ssm_b_re, m_ssm_b_im=m_ssm_b_im, m_ssm_c_re=m_ssm_c_re, m_ssm_c_im=m_ssm_c_im, m_ssm_d=m_ssm_d, m_w_glu=m_w_glu, m_b_glu=m_b_glu, m_w_out=m_w_out, m_norm_ple=m_norm_ple, m_w_ple_gate=m_w_ple_gate, m_w_ple_proj=m_w_ple_proj, m_norm_final=m_norm_final, v_norm_mix=v_norm_mix, v_w_in=v_w_in, v_q_norm=v_q_norm, v_k_norm=v_k_norm, v_ssm_a_re=v_ssm_a_re, v_ssm_a_im=v_ssm_a_im, v_ssm_log_dt=v_ssm_log_dt, v_ssm_b_re=v_ssm_b_re, v_ssm_b_im=v_ssm_b_im, v_ssm_c_re=v_ssm_c_re, v_ssm_c_im=v_ssm_c_im, v_ssm_d=v_ssm_d, v_w_glu=v_w_glu, v_b_glu=v_b_glu, v_w_out=v_w_out, v_norm_ple=v_norm_ple, v_w_ple_gate=v_w_ple_gate, v_w_ple_proj=v_w_ple_proj, v_norm_final=v_norm_final)
    weights = {n: given[n] for n in TWIN_WEIGHTS}
    shared = {n: given[n] for n in SHARED_INPUTS}
    per_example = {n: given[n] for n in ['x', 'p']}
    grad_fn = _jax.value_and_grad(_loss, argnums=(0, 1))

    def one_microbatch(ex, loss_target):
        ex = dict(ex)
        diff = ex.pop(TWIN_DIFF_INPUT)
        return grad_fn(weights, diff, {**shared, **ex}, loss_target)

    if N_MICROBATCH == 1:
        loss, (grad_w, grad_x) = one_microbatch(per_example, given["loss_target"])
    else:
        def body(carry, xs):
            loss_sum, grad_sum = carry
            l_k, (gw_k, gx_k) = one_microbatch(xs[0], xs[1])
            with _jax.named_scope("update"):
                return (loss_sum + l_k, _jax.tree.map(_jnp.add, grad_sum, gw_k)), gx_k

        init = (_jnp.zeros((), _jnp.float32), _jax.tree.map(_jnp.zeros_like, weights))
        (loss, grad_w), grad_x = _jax.lax.scan(body, init, (per_example, given["loss_target"]))
    with _jax.named_scope("update"):
        delta_w, new_m, new_v = {}, {}, {}
        for n in TWIN_WEIGHTS:
            delta_w[n], new_m[n], new_v[n] = _adamw(weights[n], grad_w[n], given["m_" + n], given["v_" + n])
    return (loss, grad_x, *[grad_w[n] for n in TWIN_WEIGHTS], *[delta_w[n] for n in TWIN_WEIGHTS],
            *[new_m[n] for n in TWIN_WEIGHTS], *[new_v[n] for n in TWIN_WEIGHTS])
```

```python
import functools
import math

import numpy as np
import jax
import jax.numpy as jnp
from jax import lax
from jax.experimental import pallas as pl
from jax.experimental.pallas import tpu as pltpu

F32 = jnp.float32
BF16 = jnp.bfloat16

N_DEV = 8
EPS = 1e-6
GRID_W = 64
ROPE_THETA = 10000.0
HEAD_DIM = 128
N_HEADS = 8
N_KV = 2
REP = N_HEADS // N_KV
D_ATTN = N_HEADS * HEAD_DIM
D_KV = N_KV * HEAD_DIM
SSM_H = 16
SSM_P = 64
SLAB = 128
SLAB_G = SLAB // SSM_H
SLAB_S = SLAB_G * SSM_P
SEG = 8
LANES = 128
VMEM_LIMIT = 48 << 20

ADAM_LR = 0.001
ADAM_B1 = 0.9
ADAM_B2 = 0.999
ADAM_EPS = 1e-08
ADAM_WD = 0.01
ADAM_STEP = 10


def _pick(n, cands):
    for c in cands:
        if n % c == 0:
            return c
    return n


def _cparams(sem, vmem=VMEM_LIMIT):
    return pltpu.CompilerParams(dimension_semantics=sem, vmem_limit_bytes=vmem)


def _mm(a, b, mode, name, out_dtype=F32, add=None, bias=None):
    if mode == "nn":
        (M, K), (_, N) = a.shape, b.shape
    elif mode == "nt":
        (M, K), (N, _) = a.shape, b.shape
    else:
        (K, M), (_, N) = a.shape, b.shape
    tm = _pick(M, (1024, 512, 256))
    tn = _pick(N, (1024, 768, 512, 256))
    tk = _pick(K, (512, 256))
    nk = K // tk
    if mode == "nn":
        a_spec = pl.BlockSpec((tm, tk), lambda i, j, k: (i, k))
        b_spec = pl.BlockSpec((tk, tn), lambda i, j, k: (k, j))
        dims = (((1,), (0,)), ((), ()))
    elif mode == "nt":
        a_spec = pl.BlockSpec((tm, tk), lambda i, j, k: (i, k))
        b_spec = pl.BlockSpec((tn, tk), lambda i, j, k: (j, k))
        dims = (((1,), (1,)), ((), ()))
    else:
        a_spec = pl.BlockSpec((tk, tm), lambda i, j, k: (k, i))
        b_spec = pl.BlockSpec((tk, tn), lambda i, j, k: (k, j))
        dims = (((0,), (0,)), ((), ()))
    extras, extra_specs = [], []
    if add is not None:
        extras.append(add)
        extra_specs.append(pl.BlockSpec((tm, tn), lambda i, j, k: (i, j)))
    if bias is not None:
        extras.append(bias)
        extra_specs.append(pl.BlockSpec((1, tn), lambda i, j, k: (0, j)))

    def body(a_ref, b_ref, *rest):
        o_ref, acc_ref = rest[-2], rest[-1]
        k = pl.program_id(2)

        @pl.when(k == 0)
        def _():
            acc_ref[...] = jnp.zeros_like(acc_ref)

        acc_ref[...] += lax.dot_general(a_ref[...], b_ref[...], dims, preferred_element_type=F32)

        @pl.when(k == nk - 1)
        def _():
            out = acc_ref[...]
            idx = 0
            if add is not None:
                out = out + rest[idx][...]
                idx += 1
            if bias is not None:
                out = out + rest[idx][...]
            o_ref[...] = out.astype(out_dtype)

    return pl.pallas_call(
        body, name=name, grid=(M // tm, N // tn, nk),
        in_specs=[a_spec, b_spec] + extra_specs,
        out_specs=pl.BlockSpec((tm, tn), lambda i, j, k: (i, j)),
        out_shape=jax.ShapeDtypeStruct((M, N), out_dtype),
        scratch_shapes=[pltpu.VMEM((tm, tn), F32)],
        compiler_params=_cparams(("parallel", "parallel", "arbitrary")),
    )(a, b, *extras)


def _rspec(tm, w, cb=0):
    return pl.BlockSpec((tm, w), lambda i: (i, cb))


def _fspec(shape):
    nd = len(shape)
    return pl.BlockSpec(shape, lambda i: (0,) * nd)


def _rowcall(body, name, L, tm, ins, row_outs, acc_outs=()):
    out_shape = [jax.ShapeDtypeStruct((L, w), dt) for w, dt in row_outs]
    out_shape += [jax.ShapeDtypeStruct(s, F32) for s in acc_outs]
    out_specs = [_rspec(tm, w) for w, _ in row_outs] + [_fspec(s) for s in acc_outs]
    return pl.pallas_call(
        body, name=name, grid=(L // tm,),
        in_specs=[s for _, s in ins], out_specs=out_specs, out_shape=out_shape,
        compiler_params=_cparams(("arbitrary",)),
    )(*[a for a, _ in ins])


def _acc(ref, val):
    @pl.when(pl.program_id(0) == 0)
    def _():
        ref[...] = jnp.zeros_like(ref)
    ref[...] += val


def _colsum(v):
    return jnp.sum(v, axis=0, keepdims=True)


def _rms(xv):
    return lax.rsqrt(jnp.mean(xv * xv, axis=-1, keepdims=True) + EPS)


def _rms_bwd(dn, xhat, r, g):
    dng = dn * g
    return r * (dng - xhat * jnp.mean(dng * xhat, axis=-1, keepdims=True))


def _sigmoid(v):
    return jax.nn.sigmoid(v)


def _partner(v):
    w = v.shape[-1]
    lane = lax.broadcasted_iota(jnp.int32, v.shape, v.ndim - 1)
    first_half = (lane % 64) < 32
    return jnp.where(first_half, pltpu.roll(v, w - 32, axis=v.ndim - 1), pltpu.roll(v, 32, axis=v.ndim - 1))


def _norm_in(x, g, name):
    L, D = x.shape
    tm = _pick(L, (512, 256))

    def body(x_ref, g_ref, o_ref):
        xv = x_ref[...]
        o_ref[...] = (xv * _rms(xv) * g_ref[...]).astype(BF16)

    return _rowcall(body, name, L, tm, [(x, _rspec(tm, D)), (g, _fspec(g.shape))], [(D, BF16)])[0]


def _rope_tables(L):
    rows_n = L // GRID_W
    rows = jnp.repeat(jnp.arange(rows_n), GRID_W).astype(F32)
    cols = jnp.tile(jnp.arange(GRID_W), rows_n).astype(F32)
    n_freq = HEAD_DIM // 4
    inv_freq = ROPE_THETA ** (-jnp.arange(n_freq, dtype=F32) / n_freq)
    ar, ac = rows[:, None] * inv_freq[None, :], cols[:, None] * inv_freq[None, :]
    cos = jnp.concatenate([jnp.cos(ar), jnp.cos(ar), jnp.cos(ac), jnp.cos(ac)], axis=-1)
    sin = jnp.concatenate([-jnp.sin(ar), jnp.sin(ar), -jnp.sin(ac), jnp.sin(ac)], axis=-1)
    return cos, sin


def _qkv_prep(z, cos, sin, qn, kn):
    L = z.shape[0]
    tm = _pick(L, (512, 256))
    scale = HEAD_DIM ** -0.5
    kblk = 4 * D_ATTN // D_KV

    def body(q_ref, k_ref, v_ref, cos_ref, sin_ref, qn_ref, kn_ref, qo_ref, ko_ref, vo_ref, kt_ref):
        c, s = cos_ref[...], sin_ref[...]

        def head(xh, w):
            n = xh * _rms(xh) * w
            return n * c + _partner(n) * s

        for h in range(N_HEADS):
            sl = slice(h * HEAD_DIM, (h + 1) * HEAD_DIM)
            qo_ref[:, sl] = (head(q_ref[:, sl], qn_ref[...]) * scale).astype(BF16)
        for h in range(N_KV):
            sl = slice(h * HEAD_DIM, (h + 1) * HEAD_DIM)
            kr = head(k_ref[:, sl], kn_ref[...])
            ko_ref[:, sl] = kr.astype(BF16)
            kt_ref[sl, :] = kr.T.astype(BF16)
        vo_ref[...] = v_ref[...].astype(BF16)

    return pl.pallas_call(
        body, name="qkv_prep", grid=(L // tm,),
        in_specs=[_rspec(tm, D_ATTN, 0), _rspec(tm, D_KV, kblk), _rspec(tm, D_KV, kblk + 1),
                  _rspec(tm, HEAD_DIM), _rspec(tm, HEAD_DIM), _fspec(qn.shape), _fspec(kn.shape)],
        out_specs=[_rspec(tm, D_ATTN), _rspec(tm, D_KV), _rspec(tm, D_KV),
                   pl.BlockSpec((D_KV, tm), lambda i: (0, i))],
        out_shape=[jax.ShapeDtypeStruct((L, D_ATTN), BF16), jax.ShapeDtypeStruct((L, D_KV), BF16),
                   jax.ShapeDtypeStruct((L, D_KV), BF16), jax.ShapeDtypeStruct((D_KV, L), BF16)],
        compiler_params=_cparams(("arbitrary",)),
    )(z, z, z, cos, sin, qn, kn)


def _attn_fwd(q, k, v):
    L = q.shape[0]
    tq = _pick(L, (256, 128))

    def body(q_ref, k_ref, v_ref, o_ref):
        s = lax.dot_general(q_ref[...], k_ref[...], (((1,), (1,)), ((), ())), preferred_element_type=F32)
        e = jnp.exp(s - jnp.max(s, axis=-1, keepdims=True))
        l = jnp.sum(e, axis=-1, keepdims=True)
        o_ref[...] = jnp.dot(e.astype(BF16), v_ref[...], preferred_element_type=F32) / l

    return pl.pallas_call(
        body, name="attn_fwd", grid=(N_HEADS, L // tq),
        in_specs=[pl.BlockSpec((tq, HEAD_DIM), lambda h, i: (i, h)),
                  pl.BlockSpec((L, HEAD_DIM), lambda h, i: (0, h // REP)),
                  pl.BlockSpec((L, HEAD_DIM), lambda h, i: (0, h // REP))],
        out_specs=pl.BlockSpec((tq, HEAD_DIM), lambda h, i: (i, h)),
        out_shape=jax.ShapeDtypeStruct((L, D_ATTN), F32),
        compiler_params=_cparams(("arbitrary", "arbitrary")),
    )(q, k, v)


def _attn_bwd(q, k, v, kt, do):
    L = q.shape[0]
    tq = _pick(L, (256, 128))
    nt = (((1,), (1,)), ((), ()))

    def body(q_ref, do_ref, k_ref, v_ref, kt_ref, dq_ref, dk_ref, dv_ref):
        @pl.when((pl.program_id(1) == 0) & (pl.program_id(2) == 0))
        def _():
            dk_ref[...] = jnp.zeros_like(dk_ref)
            dv_ref[...] = jnp.zeros_like(dv_ref)

        qv, dov = q_ref[...], do_ref[...]
        st = lax.dot_general(k_ref[...], qv, nt, preferred_element_type=F32)
        e = jnp.exp(st - jnp.max(st, axis=0, keepdims=True))
        pt = e * (1.0 / jnp.sum(e, axis=0, keepdims=True))
        dpt = lax.dot_general(v_ref[...], dov, nt, preferred_element_type=F32)
        delta = jnp.sum(pt * dpt, axis=0, keepdims=True)
        dst = (pt * (dpt - delta)).astype(BF16)
        dv_ref[...] += jnp.dot(pt.astype(BF16), dov, preferred_element_type=F32)
        dk_ref[...] += jnp.dot(dst, qv, preferred_element_type=F32)
        dq_ref[...] = jnp.dot(kt_ref[...], dst, preferred_element_type=F32).T

    return pl.pallas_call(
        body, name="attn_bwd", grid=(N_KV, REP, L // tq),
        in_specs=[pl.BlockSpec((tq, HEAD_DIM), lambda g, r, i: (i, g * REP + r)),
                  pl.BlockSpec((tq, HEAD_DIM), lambda g, r, i: (i, g * REP + r)),
                  pl.BlockSpec((L, HEAD_DIM), lambda g, r, i: (0, g)),
                  pl.BlockSpec((L, HEAD_DIM), lambda g, r, i: (0, g)),
                  pl.BlockSpec((HEAD_DIM, L), lambda g, r, i: (g, 0))],
        out_specs=[pl.BlockSpec((tq, HEAD_DIM), lambda g, r, i: (i, g * REP + r)),
                   pl.BlockSpec((L, HEAD_DIM), lambda g, r, i: (0, g)),
                   pl.BlockSpec((L, HEAD_DIM), lambda g, r, i: (0, g))],
        out_shape=[jax.ShapeDtypeStruct((L, D_ATTN), F32), jax.ShapeDtypeStruct((L, D_KV), F32),
                   jax.ShapeDtypeStruct((L, D_KV), F32)],
        compiler_params=_cparams(("arbitrary", "arbitrary", "arbitrary")),
    )(q, do, k, v, kt)


def _seg_perm(a):
    L, C = a.shape
    return a.reshape(SEG, L // SEG, C).transpose(1, 0, 2).reshape(L, C)


def _seg_unperm(a):
    L, C = a.shape
    return a.reshape(L // SEG, SEG, C).transpose(1, 0, 2).reshape(L, C)


def _cmul(ar, ai, br, bi):
    return ar * br - ai * bi, ar * bi + ai * br


def _seg_scan(xr_ref, xi_ref, ar, ai, reverse, n_rows):
    shape = ar.shape
    zero = jnp.zeros(shape, F32)

    def row(r):
        rr = (n_rows - 1 - r) if reverse else r
        return pl.ds(pl.multiple_of(rr * SEG, SEG), SEG)

    def local(r, carry):
        cr, ci = carry
        sl = row(r)
        pr, pi = _cmul(ar, ai, cr, ci)
        nr, ni = pr + xr_ref[sl, :], pi + xi_ref[sl, :]
        xr_ref[sl, :] = nr
        xi_ref[sl, :] = ni
        return nr, ni

    er, ei = lax.fori_loop(0, n_rows, local, (zero, zero))

    pr, pi = ar, ai
    for _ in range(int(math.log2(n_rows))):
        pr, pi = _cmul(pr, pi, pr, pi)
    sub = lax.broadcasted_iota(jnp.int32, shape, 0)
    shift = (SEG - 1) if reverse else 1
    edge = (SEG - 1) if reverse else 0
    inr, ini = zero, zero
    for _ in range(SEG - 1):
        tr, ti = _cmul(pr, pi, inr, ini)
        inr = jnp.where(sub == edge, 0.0, pltpu.roll(tr + er, shift, axis=0))
        ini = jnp.where(sub == edge, 0.0, pltpu.roll(ti + ei, shift, axis=0))

    def fix(r, carry):
        wr, wi = carry
        sl = row(r)
        fr, fi = _cmul(wr, wi, inr, ini)
        xr_ref[sl, :] += fr
        xi_ref[sl, :] += fi
        return _cmul(wr, wi, ar, ai)

    lax.fori_loop(0, n_rows, fix, (ar, ai))
    return inr, ini


def _ssm_specs(L):
    u_spec = pl.BlockSpec((L, SLAB), lambda j: (0, j))
    wb_spec = pl.BlockSpec((2, 1, SLAB, SLAB_S), lambda j: (0, j, 0, 0))
    wc_spec = pl.BlockSpec((2, 1, SLAB_S, SLAB), lambda j: (0, j, 0, 0))
    lam_spec = pl.BlockSpec((2, 1, 1, SLAB_S), lambda j: (0, j, 0, 0))
    d_spec = pl.BlockSpec((1, SLAB), lambda j: (0, j))
    return u_spec, wb_spec, wc_spec, lam_spec, d_spec


def _ssm_fwd(u, wbr, wbi, wcr, wci, lbr, lbi, dskip):
    L, C = u.shape
    n_rows = L // SEG
    tc = _pick(L, (512, 256))
    u_spec, wb_spec, wc_spec, lam_spec, d_spec = _ssm_specs(L)

    def body(u_ref, wbr_ref, wbi_ref, wcr_ref, wci_ref, lbr_ref, lbi_ref, d_ref, y_ref, xr_ref, xi_ref):
        y_ref[...] = u_ref[...] * d_ref[...]
        for d in range(2):
            def inp(c, _):
                sl = pl.ds(pl.multiple_of(c * tc, tc), tc)
                ub = u_ref[sl, :].astype(BF16)
                xr_ref[sl, :] = jnp.dot(ub, wbr_ref[d, 0], preferred_element_type=F32)
                xi_ref[sl, :] = jnp.dot(ub, wbi_ref[d, 0], preferred_element_type=F32)
                return 0

            lax.fori_loop(0, L // tc, inp, 0)
            ar = jnp.broadcast_to(lbr_ref[d, 0], (SEG, SLAB_S))
            ai = jnp.broadcast_to(lbi_ref[d, 0], (SEG, SLAB_S))
            _seg_scan(xr_ref, xi_ref, ar, ai, d == 1, n_rows)

            def outp(c, _):
                sl = pl.ds(pl.multiple_of(c * tc, tc), tc)
                y_ref[sl, :] += (jnp.dot(xr_ref[sl, :].astype(BF16), wcr_ref[d, 0], preferred_element_type=F32)
                                 - jnp.dot(xi_ref[sl, :].astype(BF16), wci_ref[d, 0], preferred_element_type=F32))
                return 0

            lax.fori_loop(0, L // tc, outp, 0)

    return pl.pallas_call(
        body, name="ssm_fwd", grid=(C // SLAB,),
        in_specs=[u_spec, wb_spec, wb_spec, wc_spec, wc_spec, lam_spec, lam_spec, d_spec],
        out_specs=u_spec, out_shape=jax.ShapeDtypeStruct((L, C), F32),
        scratch_shapes=[pltpu.VMEM((L, SLAB_S), F32), pltpu.VMEM((L, SLAB_S), F32)],
        compiler_params=_cparams(("arbitrary",)),
    )(u, wbr, wbi, wcr, wci, lbr, lbi, dskip)


def _ssm_bwd(u, dy, wbr, wbi, wcr, wci, lbr, lbi, dskip):
    L, C = u.shape
    n_rows = L // SEG
    n_slab = C // SLAB
    tc = _pick(L, (512, 256))
    nt = (((1,), (1,)), ((), ()))
    u_spec, wb_spec, wc_spec, lam_spec, d_spec = _ssm_specs(L)
    wct_spec = pl.BlockSpec((2, 1, SLAB, SLAB_S), lambda j: (0, j, 0, 0))

    def body(u_ref, dy_ref, wbr_ref, wbi_ref, wcr_ref, wci_ref, lbr_ref, lbi_ref, d_ref,
             du_ref, dwbr_ref, dwbi_ref, dwcr_ref, dwci_ref, dlr_ref, dli_ref, dd_ref,
             xr_ref, xi_ref, gr_ref, gi_ref):
        du_ref[...] = dy_ref[...] * d_ref[...]
        dd_ref[...] = _colsum(dy_ref[...] * u_ref[...])
        zero_w = jnp.zeros((SLAB, SLAB_S), F32)
        for d in range(2):
            def inp(c, _):
                sl = pl.ds(pl.multiple_of(c * tc, tc), tc)
                ub = u_ref[sl, :].astype(BF16)
                dyb = dy_ref[sl, :].astype(BF16)
                xr_ref[sl, :] = jnp.dot(ub, wbr_ref[d, 0], preferred_element_type=F32)
                xi_ref[sl, :] = jnp.dot(ub, wbi_ref[d, 0], preferred_element_type=F32)
                gr_ref[sl, :] = lax.dot_general(dyb, wcr_ref[d, 0], nt, preferred_element_type=F32)
                gi_ref[sl, :] = -lax.dot_general(dyb, wci_ref[d, 0], nt, preferred_element_type=F32)
                return 0

            lax.fori_loop(0, L // tc, inp, 0)
            ar = jnp.broadcast_to(lbr_ref[d, 0], (SEG, SLAB_S))
            ai = jnp.broadcast_to(lbi_ref[d, 0], (SEG, SLAB_S))
            inr, ini = _seg_scan(xr_ref, xi_ref, ar, ai, d == 1, n_rows)
            _seg_scan(gr_ref, gi_ref, ar, -ai, d == 0, n_rows)

            def pole(r, carry):
                accr, acci, pr, pi = carry
                rr = (n_rows - 1 - r) if d == 1 else r
                sl = pl.ds(pl.multiple_of(rr * SEG, SEG), SEG)
                lr, li = gr_ref[sl, :], gi_ref[sl, :]
                accr = accr + lr * pr + li * pi
                acci = acci + li * pr - lr * pi
                return accr, acci, xr_ref[sl, :], xi_ref[sl, :]

            zero = jnp.zeros((SEG, SLAB_S), F32)
            accr, acci, _, _ = lax.fori_loop(0, n_rows, pole, (zero, zero, inr, ini))
            dlr_ref[d, 0] = _colsum(accr)
            dli_ref[d, 0] = _colsum(acci)

            dwbr_ref[d, 0] = zero_w
            dwbi_ref[d, 0] = zero_w
            dwcr_ref[d, 0] = zero_w
            dwci_ref[d, 0] = zero_w

            def outp(c, _):
                sl = pl.ds(pl.multiple_of(c * tc, tc), tc)
                lrb, lib = gr_ref[sl, :].astype(BF16), gi_ref[sl, :].astype(BF16)
                du_ref[sl, :] += (lax.dot_general(lrb, wbr_ref[d, 0], nt, preferred_element_type=F32)
                                  + lax.dot_general(lib, wbi_ref[d, 0], nt, preferred_element_type=F32))
                ut = u_ref[sl, :].T.astype(BF16)
                dyt = dy_ref[sl, :].T.astype(BF16)
                dwbr_ref[d, 0] += jnp.dot(ut, lrb, preferred_element_type=F32)
                dwbi_ref[d, 0] += jnp.dot(ut, lib, preferred_element_type=F32)
                dwcr_ref[d, 0] += jnp.dot(dyt, xr_ref[sl, :].astype(BF16), preferred_element_type=F32)
                dwci_ref[d, 0] -= jnp.dot(dyt, xi_ref[sl, :].astype(BF16), preferred_element_type=F32)
                return 0

            lax.fori_loop(0, L // tc, outp, 0)

    wshape = jax.ShapeDtypeStruct((2, n_slab, SLAB, SLAB_S), F32)
    lshape = jax.ShapeDtypeStruct((2, n_slab, 1, SLAB_S), F32)
    return pl.pallas_call(
        body, name="ssm_bwd", grid=(n_slab,),
        in_specs=[u_spec, u_spec, wb_spec, wb_spec, wc_spec, wc_spec, lam_spec, lam_spec, d_spec],
        out_specs=[u_spec, wct_spec, wct_spec, wct_spec, wct_spec, lam_spec, lam_spec, d_spec],
        out_shape=[jax.ShapeDtypeStruct((L, C), F32), wshape, wshape, wshape, wshape, lshape, lshape,
                   jax.ShapeDtypeStruct((1, C), F32)],
        scratch_shapes=[pltpu.VMEM((L, SLAB_S), F32)] * 4,
        compiler_params=_cparams(("arbitrary",), 60 << 20),
    )(u, dy, wbr, wbi, wcr, wci, lbr, lbi, dskip)


def _ssm_discretise(a_re, a_im, log_dt):
    def body(are_ref, aim_ref, ldt_ref, lbr_ref, lbi_ref, fr_ref, fi_ref):
        lr = jnp.minimum(are_ref[...], -1e-4)
        li = aim_ref[...]
        dt = jnp.exp(ldt_ref[...])
        mag = jnp.exp(lr * dt)
        lbr = mag * jnp.cos(li * dt)
        lbi = mag * jnp.sin(li * dt)
        den = lr * lr + li * li
        nr, ni = lbr - 1.0, lbi
        lbr_ref[...] = lbr
        lbi_ref[...] = lbi
        fr_ref[...] = (nr * lr + ni * li) / den
        fi_ref[...] = (ni * lr - nr * li) / den

    shp = jax.ShapeDtypeStruct(a_re.shape, F32)
    return pl.pallas_call(body, name="ssm_discretise", out_shape=[shp] * 4)(a_re, a_im, log_dt)


def _ssm_bbar(fr, fi, b_re, b_im):
    N, H = b_re.shape
    tr = _pick(N, (1024,))

    def body(fr_ref, fi_ref, br_ref, bi_ref, or_ref, oi_ref):
        f_r, f_i, b_r, b_i = fr_ref[...], fi_ref[...], br_ref[...], bi_ref[...]
        or_ref[...] = f_r * b_r - f_i * b_i
        oi_ref[...] = f_r * b_i + f_i * b_r

    fs, bs = _rspec(tr, 1), _rspec(tr, H)
    shp = jax.ShapeDtypeStruct(b_re.shape, F32)
    return pl.pallas_call(body, name="ssm_bbar", grid=(N // tr,), in_specs=[fs, fs, bs, bs], out_specs=[bs, bs],
                          out_shape=[shp] * 2, compiler_params=_cparams(("arbitrary",)))(fr, fi, b_re, b_im)


def _ssm_param_grads(a_re, a_im, log_dt, lbr, lbi, fr, fi, dlbr, dlbi, dfr, dfi):
    def body(are_ref, aim_ref, ldt_ref, lbr_ref, lbi_ref, fr_ref, fi_ref, glr_ref, gli_ref, gfr_ref, gfi_ref,
             dar_ref, dai_ref, dldt_ref):
        a_r = are_ref[...]
        lr = jnp.minimum(a_r, -1e-4)
        li = aim_ref[...]
        dt = jnp.exp(ldt_ref[...])
        den = lr * lr + li * li
        ir, ii = lr / den, -li / den
        gfr, gfi = gfr_ref[...], gfi_ref[...]
        f_r, f_i = fr_ref[...], fi_ref[...]
        tr, ti = _cmul(ir, -ii, gfr, gfi)
        glbr, glbi = glr_ref[...] + tr, gli_ref[...] + ti
        qr, qi = _cmul(f_r, f_i, ir, ii)
        dlr, dli = _cmul(-qr, qi, gfr, gfi)
        zr, zi = _cmul(lbr_ref[...], -lbi_ref[...], glbr, glbi)
        dlr = dlr + dt * zr
        dli = dli + dt * zi
        ddt = jnp.sum(lr * zr + li * zi, axis=-1, keepdims=True)
        dar_ref[...] = jnp.where(a_r < -1e-4, dlr, jnp.where(a_r == -1e-4, 0.5 * dlr, 0.0))
        dai_ref[...] = dli
        dldt_ref[...] = ddt * dt

    shp = jax.ShapeDtypeStruct(a_re.shape, F32)
    return pl.pallas_call(body, name="ssm_param_grads",
                          out_shape=[shp, shp, jax.ShapeDtypeStruct(log_dt.shape, F32)],
                          )(a_re, a_im, log_dt, lbr, lbi, fr, fi, dlbr, dlbi, dfr, dfi)


def _ssm_b_grads(fr, fi, b_re, b_im, dbbr, dbbi):
    N, H = b_re.shape
    tr = _pick(N, (1024,))

    def body(fr_ref, fi_ref, br_ref, bi_ref, gr_ref, gi_ref, dbr_ref, dbi_ref, dfr_ref, dfi_ref):
        f_r, f_i, b_r, b_i, g_r, g_i = (fr_ref[...], fi_ref[...], br_ref[...], bi_ref[...], gr_ref[...], gi_ref[...])
        dbr_ref[...] = f_r * g_r + f_i * g_i
        dbi_ref[...] = f_r * g_i - f_i * g_r
        dfr_ref[...] = jnp.sum(g_r * b_r + g_i * b_i, axis=-1, keepdims=True)
        dfi_ref[...] = jnp.sum(g_i * b_r - g_r * b_i, axis=-1, keepdims=True)

    fs, bs = _rspec(tr, 1), _rspec(tr, H)
    bshp = jax.ShapeDtypeStruct(b_re.shape, F32)
    fshp = jax.ShapeDtypeStruct(fr.shape, F32)
    return pl.pallas_call(body, name="ssm_b_grads", grid=(N // tr,), in_specs=[fs, fs, bs, bs, bs, bs],
                          out_specs=[bs, bs, fs, fs], out_shape=[bshp, bshp, fshp, fshp],
                          compiler_params=_cparams(("arbitrary",)))(fr, fi, b_re, b_im, dbbr, dbbi)


def _block_diag(w, n_slab):
    _, _, A, B = w.shape
    w5 = w.reshape(2, n_slab, SLAB_G, A, B)
    eye = jnp.eye(SLAB_G, dtype=w.dtype)
    out = w5[:, :, :, :, None, :] * eye[None, None, :, None, :, None]
    return out.reshape(2, n_slab, SLAB_G * A, SLAB_G * B)


def _block_diag_take(w, A, B):
    n_slab = w.shape[1]
    w6 = w.reshape(2, n_slab, SLAB_G, A, SLAB_G, B)
    idx = jnp.arange(SLAB_G)
    out = w6[:, :, idx, :, idx, :]
    return out.transpose(1, 2, 0, 3, 4).reshape(2, n_slab * SLAB_G, A, B)


def _peer(k, x, y, c):
    return (1 - x if k & 4 else x, 1 - y if k & 2 else y, 1 - c if k & 1 else c)


def _dev_index(pos):
    return 4 * pos[0] + 2 * pos[1] + pos[2]


def _all_gather(xs, name):
    n = len(xs)
    any_spec = pl.BlockSpec(memory_space=pl.ANY)

    def body(*refs):
        x_refs, out_refs = refs[:n], refs[n:2 * n]
        send_sems, recv_sems, local_sems = refs[2 * n:]
        x, y, c = lax.axis_index("x"), lax.axis_index("y"), lax.axis_index("c")
        me, sibling = (x, y, c), (x, y, 1 - c)
        chips = [(1 - x, y), (x, 1 - y), (1 - x, 1 - y)]

        def copy(a, k, block, to, src=None):
            dst = out_refs[a].at[_dev_index(block)]
            return pltpu.make_async_remote_copy(
                src_ref=dst if src is None else src, dst_ref=dst,
                send_sem=send_sems.at[a, k], recv_sem=recv_sems.at[a, k],
                device_id=to, device_id_type=pl.DeviceIdType.MESH)

        mine = [pltpu.make_async_copy(x_refs[a], out_refs[a].at[_dev_index(me)], local_sems.at[a]) for a in range(n)]
        for cp in mine:
            cp.start()
        first = []
        for a in range(n):
            first.append(copy(a, 0, me, sibling, src=x_refs[a]))
            first += [copy(a, 1 + j, me, (*chip, c), src=x_refs[a]) for j, chip in enumerate(chips)]
        for cp in first:
            cp.start()
        passed = []
        for j, chip in enumerate(chips):
            for a in range(n):
                copy(a, 1 + j, (*chip, c), me).wait_recv()
                fwd = copy(a, 4 + j, (*chip, c), sibling)
                fwd.start()
                passed.append(fwd)
        for a in range(n):
            copy(a, 0, sibling, me).wait_recv()
            for j, chip in enumerate(chips):
                copy(a, 4 + j, (*chip, 1 - c), me).wait_recv()
        for cp in first + passed:
            cp.wait_send()
        for cp in mine:
            cp.wait()

    return pl.pallas_call(
        body, name=name,
        out_shape=[jax.ShapeDtypeStruct((N_DEV,) + v.shape, v.dtype) for v in xs],
        in_specs=[any_spec] * n, out_specs=[any_spec] * n,
        scratch_shapes=[pltpu.SemaphoreType.DMA((n, 7)), pltpu.SemaphoreType.DMA((n, 7)),
                        pltpu.SemaphoreType.DMA((n,))],
    )(*xs)


def _all_to_all(xs, name):
    n = len(xs)
    any_spec = pl.BlockSpec(memory_space=pl.ANY)

    def body(*refs):
        x_refs, out_refs = refs[:n], refs[n:2 * n]
        send_sems, recv_sems, local_sems = refs[2 * n:]
        x, y, c = lax.axis_index("x"), lax.axis_index("y"), lax.axis_index("c")
        me = _dev_index((x, y, c))

        def copy(a, k):
            peer = _peer(k, x, y, c)
            return pltpu.make_async_remote_copy(
                src_ref=x_refs[a].at[_dev_index(peer)], dst_ref=out_refs[a].at[me],
                send_sem=send_sems.at[a, k - 1], recv_sem=recv_sems.at[a, k - 1],
                device_id=peer, device_id_type=pl.DeviceIdType.MESH)

        def arrival(a, k):
            land = out_refs[a].at[_dev_index(_peer(k, x, y, c))]
            return pltpu.make_async_remote_copy(
                src_ref=land, dst_ref=land, send_sem=send_sems.at[a, k - 1], recv_sem=recv_sems.at[a, k - 1],
                device_id=_peer(k, x, y, c), device_id_type=pl.DeviceIdType.MESH)

        mine = [pltpu.make_async_copy(x_refs[a].at[me], out_refs[a].at[me], local_sems.at[a]) for a in range(n)]
        for cp in mine:
            cp.start()
        sends = [copy(a, k) for k in range(1, N_DEV) for a in range(n)]
        for cp in sends:
            cp.start()
        for k in range(1, N_DEV):
            for a in range(n):
                arrival(a, k).wait_recv()
        for cp in sends:
            cp.wait_send()
        for cp in mine:
            cp.wait()

    return pl.pallas_call(
        body, name=name,
        out_shape=[jax.ShapeDtypeStruct(v.shape, v.dtype) for v in xs],
        in_specs=[any_spec] * n, out_specs=[any_spec] * n,
        scratch_shapes=[pltpu.SemaphoreType.DMA((n, 7)), pltpu.SemaphoreType.DMA((n, 7)),
                        pltpu.SemaphoreType.DMA((n,))],
    )(*xs)


def _sum_blocks(x, name):
    _, R, W = x.shape
    tr = _pick(R, (1664, 1024, 544, 512, 256, 8))

    def body(x_ref, o_ref):
        acc = x_ref[0].astype(F32)
        for d in range(1, N_DEV):
            acc = acc + x_ref[d].astype(F32)
        o_ref[...] = acc

    return pl.pallas_call(
        body, name=name, grid=(R // tr,),
        in_specs=[pl.BlockSpec((N_DEV, tr, W), lambda i: (0, i, 0))],
        out_specs=pl.BlockSpec((tr, W), lambda i: (i, 0)),
        out_shape=jax.ShapeDtypeStruct((R, W), F32),
        compiler_params=_cparams(("arbitrary",)),
    )(x)


def _adamw(w, g, m, v, name):
    R, W = w.shape
    tr = _pick(R, (512, 256, 128, 64, 32, 16, 8)) if R * W > (1 << 18) else R

    def body(w_ref, g_ref, m_ref, v_ref, d_ref, mo_ref, vo_ref):
        gv = g_ref[...]
        mn = ADAM_B1 * m_ref[...] + (1.0 - ADAM_B1) * gv
        vn = ADAM_B2 * v_ref[...] + (1.0 - ADAM_B2) * (gv * gv)
        m_hat = mn / (1.0 - ADAM_B1 ** ADAM_STEP)
        v_hat = vn / (1.0 - ADAM_B2 ** ADAM_STEP)
        d_ref[...] = -ADAM_LR * (m_hat / (jnp.sqrt(v_hat) + ADAM_EPS) + ADAM_WD * w_ref[...])
        mo_ref[...] = mn
        vo_ref[...] = vn

    spec = pl.BlockSpec((tr, W), lambda i: (i, 0))
    shp = jax.ShapeDtypeStruct((R, W), F32)
    return pl.pallas_call(
        body, name=name, grid=(R // tr,), in_specs=[spec] * 4, out_specs=[spec] * 3, out_shape=[shp] * 3,
        compiler_params=_cparams(("arbitrary",)),
    )(w, g, m, v)


def _pack_rows(parts, dtype, pad_rows_to=1):
    flat = jnp.concatenate([p.astype(dtype).reshape(-1) for p in parts])
    rows = -(-flat.shape[0] // LANES)
    rows = -(-rows // pad_rows_to) * pad_rows_to
    flat = jnp.pad(flat, (0, rows * LANES - flat.shape[0]))
    return flat.reshape(rows, LANES)


def _unpack_rows(packed, shapes):
    flat = packed.reshape(-1)
    out, off = [], 0
    for s in shapes:
        n = int(np.prod(s))
        out.append(flat[off:off + n].reshape(s))
        off += n
    return out


def _to_zp(w):
    a, kv = D_ATTN, D_KV
    return jnp.concatenate([w[:, :a], w[:, a + 2 * kv:], w[:, a:a + 2 * kv]], axis=1)


def _from_zp(w):
    a, kv = D_ATTN, D_KV
    n = w.shape[1]
    return jnp.concatenate([w[:, :a], w[:, n - 2 * kv:], w[:, a:n - 2 * kv]], axis=1)


def _gelu(v):
    c = math.sqrt(2.0 / math.pi)
    return 0.5 * v * (1.0 + jnp.tanh(c * (v + 0.044715 * v * v * v)))


def _gelu_grad(v):
    c = math.sqrt(2.0 / math.pi)
    t = jnp.tanh(c * (v + 0.044715 * v * v * v))
    return 0.5 * (1.0 + t) + 0.5 * v * (1.0 - t * t) * c * (1.0 + 3.0 * 0.044715 * v * v)


def kernel(x, p, norm_mix, w_in, q_norm, k_norm, ssm_a_re, ssm_a_im, ssm_log_dt, ssm_b_re, ssm_b_im, ssm_c_re, ssm_c_im, ssm_d, w_glu, b_glu, w_out, norm_ple, w_ple_gate, w_ple_proj, norm_final, loss_target, m_norm_mix, m_w_in, m_q_norm, m_k_norm, m_ssm_a_re, m_ssm_a_im, m_ssm_log_dt, m_ssm_b_re, m_ssm_b_im, m_ssm_c_re, m_ssm_c_im, m_ssm_d, m_w_glu, m_b_glu, m_w_out, m_norm_ple, m_w_ple_gate, m_w_ple_proj, m_norm_final, v_norm_mix, v_w_in, v_q_norm, v_k_norm, v_ssm_a_re, v_ssm_a_im, v_ssm_log_dt, v_ssm_b_re, v_ssm_b_im, v_ssm_c_re, v_ssm_c_im, v_ssm_d, v_w_glu, v_b_glu, v_w_out, v_norm_ple, v_w_ple_gate, v_w_ple_proj, v_norm_final):
    L, D = x.shape[1], x.shape[2]
    D_SSM = ssm_d.shape[1]
    G = D_SSM // SSM_H
    n_slab = D_SSM // SLAB
    D_IN = w_in.shape[2] * N_DEV
    PLE = p.shape[3]
    xs = x[0]
    ps = p[0, 0]
    tgt = loss_target[0]

    shard_shapes = [w_in.shape[1:], w_glu.shape[1:], w_out.shape[1:], w_ple_gate.shape[1:], w_ple_proj.shape[1:]]
    packed_w = _pack_rows([w_in[0], w_glu[0], w_out[0], w_ple_gate[0], w_ple_proj[0]], BF16, 16)
    (gathered,) = _all_gather([packed_w], "gather_weights")
    per_dev = [_unpack_rows(gathered[d], shard_shapes) for d in range(N_DEV)]
    win_p = _to_zp(jnp.concatenate([per_dev[d][0] for d in range(N_DEV)], axis=1))
    wglu = jnp.concatenate([per_dev[d][1] for d in range(N_DEV)], axis=1)
    wout = jnp.concatenate([per_dev[d][2] for d in range(N_DEV)], axis=0)
    wpg = jnp.concatenate([per_dev[d][3] for d in range(N_DEV)], axis=0)
    wpp = jnp.concatenate([per_dev[d][4] for d in range(N_DEV)], axis=1)

    a_re2, a_im2 = ssm_a_re[0].reshape(2 * G, SSM_P), ssm_a_im[0].reshape(2 * G, SSM_P)
    ldt2 = ssm_log_dt[0].reshape(2 * G, 1)
    b_re2, b_im2 = ssm_b_re[0].reshape(2 * G * SSM_P, SSM_H), ssm_b_im[0].reshape(2 * G * SSM_P, SSM_H)
    lbr, lbi, f_r, f_i = _ssm_discretise(a_re2, a_im2, ldt2)
    f_r1, f_i1 = f_r.reshape(-1, 1), f_i.reshape(-1, 1)
    bbr, bbi = _ssm_bbar(f_r1, f_i1, b_re2, b_im2)
    wbr = _block_diag(bbr.reshape(2, G, SSM_P, SSM_H).transpose(0, 1, 3, 2), n_slab).astype(BF16)
    wbi = _block_diag(bbi.reshape(2, G, SSM_P, SSM_H).transpose(0, 1, 3, 2), n_slab).astype(BF16)
    wcr = _block_diag(ssm_c_re[0].transpose(0, 1, 3, 2), n_slab).astype(BF16)
    wci = _block_diag(ssm_c_im[0].transpose(0, 1, 3, 2), n_slab).astype(BF16)
    lam_r = lbr.reshape(2, n_slab, 1, SLAB_S)
    lam_i = lbi.reshape(2, n_slab, 1, SLAB_S)

    cos, sin = _rope_tables(L)
    hn = _norm_in(xs, norm_mix, "norm_mix")
    z = _mm(hn, win_p, "nn", "in_proj")
    qr, kr, vb, kt = _qkv_prep(z, cos, sin, q_norm, k_norm)
    o = _attn_fwd(qr, kr, vb)
    u_off = 2 * D_ATTN
    u_perm = _seg_perm(z[:, u_off:u_off + D_SSM])
    ys_perm = _ssm_fwd(u_perm, wbr, wbi, wcr, wci, lam_r, lam_i, ssm_d)
    ys = _seg_unperm(ys_perm)

    tm = _pick(L, (256,))

    def gelu_body(y_ref, o_ref):
        o_ref[...] = _gelu(y_ref[...]).astype(BF16)

    (gy,) = _rowcall(gelu_body, "gelu", L, tm, [(ys, _rspec(tm, D_SSM))], [(D_SSM, BF16)])
    glu = _mm(gy, wglu, "nn", "glu_proj", bias=b_glu)

    def mix_body(o_ref, ga_ref, gla_ref, glb_ref, gs_ref, cat_ref):
        ga, gs = ga_ref[...], gs_ref[...]
        cat_ref[:, :D_ATTN] = (o_ref[...] * ga * _sigmoid(ga)).astype(BF16)
        cat_ref[:, D_ATTN:] = (gla_ref[...] * _sigmoid(glb_ref[...]) * gs * _sigmoid(gs)).astype(BF16)

    (cat,) = _rowcall(mix_body, "mix", L, tm,
                      [(o, _rspec(tm, D_ATTN)), (z, _rspec(tm, D_ATTN, 1)), (glu, _rspec(tm, D_SSM, 0)),
                       (glu, _rspec(tm, D_SSM, 1)), (z, _rspec(tm, D_SSM, 3))], [(D_ATTN + D_SSM, BF16)])
    h1 = _mm(cat, wout, "nn", "out_proj", add=xs)
    n2 = _norm_in(h1, norm_ple, "norm_ple")
    gpre = _mm(n2, wpg, "nn", "ple_gate")
    pb = ps.astype(BF16)
    pp = _mm(pb, wpp, "nn", "ple_proj")

    nf = norm_final.reshape(1, D)

    def tail_body(h1_ref, gp_ref, pp_ref, t_ref, g_ref, dh2_ref, dpp_ref, dsg_ref, loss_ref, dg_ref):
        gate = _sigmoid(gp_ref[...])
        ppv = pp_ref[...]
        h2 = h1_ref[...] + gate * ppv
        r = _rms(h2)
        hh = h2 * r
        err = hh * g_ref[...] - t_ref[...]
        _acc(loss_ref, jnp.broadcast_to(0.5 * jnp.sum(jnp.mean(err * err, axis=-1, keepdims=True)), loss_ref.shape))
        dy = err * (1.0 / D)
        _acc(dg_ref, _colsum(dy * hh))
        dh2 = _rms_bwd(dy, hh, r, g_ref[...])
        dh2_ref[...] = dh2
        dpp_ref[...] = (dh2 * gate).astype(BF16)
        dsg_ref[...] = (dh2 * ppv * gate * (1.0 - gate)).astype(BF16)

    dh2, dpp, dsg, loss_acc, d_nf = _rowcall(
        tail_body, "tail", L, tm,
        [(h1, _rspec(tm, D)), (gpre, _rspec(tm, D)), (pp, _rspec(tm, D)), (tgt, _rspec(tm, D)), (nf, _fspec(nf.shape))],
        [(D, F32), (D, BF16), (D, BF16)], [(1, LANES), (1, D)])
    loss = lax.psum(loss_acc[0, 0], ("x", "y", "c"))

    g_wpp = _mm(pb, dpp, "tn", "d_ple_proj")
    g_wpg = _mm(n2, dsg, "tn", "d_ple_gate")
    dn2 = _mm(dsg, wpg, "nt", "d_norm_ple_in")

    def ple_bwd_body(h1_ref, dn_ref, dh2_ref, g_ref, dh1_ref, dh1b_ref, dg_ref):
        h1v = h1_ref[...]
        r = _rms(h1v)
        hh = h1v * r
        dn = dn_ref[...]
        _acc(dg_ref, _colsum(dn * hh))
        dh1 = dh2_ref[...] + _rms_bwd(dn, hh, r, g_ref[...])
        dh1_ref[...] = dh1
        dh1b_ref[...] = dh1.astype(BF16)

    dh1, dh1b, d_nple = _rowcall(
        ple_bwd_body, "ple_bwd", L, tm,
        [(h1, _rspec(tm, D)), (dn2, _rspec(tm, D)), (dh2, _rspec(tm, D)), (norm_ple, _fspec(norm_ple.shape))],
        [(D, F32), (D, BF16)], [(1, D)])

    dcat = _mm(dh1b, wout, "nt", "d_cat")
    g_wout = _mm(cat, dh1b, "tn", "d_out_proj")

    def mix_bwd_body(dca_ref, dcs_ref, o_ref, ga_ref, gla_ref, glb_ref, gs_ref,
                     do_ref, dga_ref, dgs_ref, dglu_ref, db_ref):
        dca, dcs, ga, gs = dca_ref[...], dcs_ref[...], ga_ref[...], gs_ref[...]
        sa, ss, sb = _sigmoid(ga), _sigmoid(gs), _sigmoid(glb_ref[...])
        gla = gla_ref[...]
        do_ref[...] = (dca * ga * sa).astype(BF16)
        dga_ref[...] = (dca * o_ref[...] * sa * (1.0 + ga * (1.0 - sa))).astype(BF16)
        dgs_ref[...] = (dcs * gla * sb * ss * (1.0 + gs * (1.0 - ss))).astype(BF16)
        dy2 = dcs * gs * ss
        da, db = dy2 * sb, dy2 * gla * sb * (1.0 - sb)
        dglu_ref[:, :D_SSM] = da.astype(BF16)
        dglu_ref[:, D_SSM:] = db.astype(BF16)
        _acc(db_ref, jnp.concatenate([_colsum(da), _colsum(db)], axis=-1))

    do, dga, dgs, dglu, g_bglu = _rowcall(
        mix_bwd_body, "mix_bwd", L, tm,
        [(dcat, _rspec(tm, D_ATTN, 0)), (dcat, _rspec(tm, D_SSM, 1)), (o, _rspec(tm, D_ATTN)),
         (z, _rspec(tm, D_ATTN, 1)), (glu, _rspec(tm, D_SSM, 0)), (glu, _rspec(tm, D_SSM, 1)),
         (z, _rspec(tm, D_SSM, 3))],
        [(D_ATTN, BF16), (D_ATTN, BF16), (D_SSM, BF16), (2 * D_SSM, BF16)], [(1, 2 * D_SSM)])

    g_wglu = _mm(gy, dglu, "tn", "d_glu_proj")
    dgy = _mm(dglu, wglu, "nt", "d_gelu_out")

    def gelu_bwd_body(dg_ref, y_ref, o_ref):
        o_ref[...] = dg_ref[...] * _gelu_grad(y_ref[...])

    (dys,) = _rowcall(gelu_bwd_body, "gelu_bwd", L, tm,
                      [(dgy, _rspec(tm, D_SSM)), (ys, _rspec(tm, D_SSM))], [(D_SSM, F32)])
    (du_perm, dwbr, dwbi, dwcr, dwci, dlam_r, dlam_i, g_ssm_d) = _ssm_bwd(
        u_perm, _seg_perm(dys), wbr, wbi, wcr, wci, lam_r, lam_i, ssm_d)
    du = _seg_unperm(du_perm)

    dqs, dkr, dvv = _attn_bwd(qr, kr, vb, kt, do)

    scale = HEAD_DIM ** -0.5
    kblk = 4 * D_ATTN // D_KV
    tmq = _pick(L, (512, 256))

    def qkv_bwd_body(dq_ref, dk_ref, dv_ref, q_ref, k_ref, cos_ref, sin_ref, qn_ref, kn_ref,
                     dqo_ref, dko_ref, dvo_ref, dqn_ref, dkn_ref):
        c, s = cos_ref[...], sin_ref[...]

        def head(g, xh, w):
            dn = g * c + _partner(g * s)
            r = _rms(xh)
            xhat = xh * r
            return _rms_bwd(dn, xhat, r, w), _colsum(dn * xhat)

        dqn = jnp.zeros((1, HEAD_DIM), F32)
        for h in range(N_HEADS):
            sl = slice(h * HEAD_DIM, (h + 1) * HEAD_DIM)
            dx, dw = head(dq_ref[:, sl] * scale, q_ref[:, sl], qn_ref[...])
            dqo_ref[:, sl] = dx.astype(BF16)
            dqn = dqn + dw
        dkn = jnp.zeros((1, HEAD_DIM), F32)
        for h in range(N_KV):
            sl = slice(h * HEAD_DIM, (h + 1) * HEAD_DIM)
            dx, dw = head(dk_ref[:, sl], k_ref[:, sl], kn_ref[...])
            dko_ref[:, sl] = dx.astype(BF16)
            dkn = dkn + dw
        dvo_ref[...] = dv_ref[...].astype(BF16)
        _acc(dqn_ref, dqn)
        _acc(dkn_ref, dkn)

    dq, dk, dv, g_qn, g_kn = _rowcall(
        qkv_bwd_body, "qkv_bwd", L, tmq,
        [(dqs, _rspec(tmq, D_ATTN)), (dkr, _rspec(tmq, D_KV)), (dvv, _rspec(tmq, D_KV)),
         (z, _rspec(tmq, D_ATTN, 0)), (z, _rspec(tmq, D_KV, kblk)), (cos, _rspec(tmq, HEAD_DIM)),
         (sin, _rspec(tmq, HEAD_DIM)), (q_norm, _fspec(q_norm.shape)), (k_norm, _fspec(k_norm.shape))],
        [(D_ATTN, BF16), (D_KV, BF16), (D_KV, BF16)], [(1, HEAD_DIM), (1, HEAD_DIM)])

    dz = jnp.concatenate([dq, dga, du.astype(BF16), dgs, dk, dv], axis=1)
    g_win_p = _mm(hn, dz, "tn", "d_in_proj")
    dhn = _mm(dz, win_p, "nt", "d_norm_mix_in")

    def in_bwd_body(x_ref, dn_ref, dh1_ref, g_ref, dx_ref, dg_ref):
        xv = x_ref[...]
        r = _rms(xv)
        hh = xv * r
        dn = dn_ref[...]
        _acc(dg_ref, _colsum(dn * hh))
        dx_ref[...] = dh1_ref[...] + _rms_bwd(dn, hh, r, g_ref[...])

    grad_x, g_nmix = _rowcall(
        in_bwd_body, "in_bwd", L, tm,
        [(xs, _rspec(tm, D)), (dhn, _rspec(tm, D)), (dh1, _rspec(tm, D)), (norm_mix, _fspec(norm_mix.shape))],
        [(D, F32)], [(1, D)])

    g_bbr = _block_diag_take(dwbr, SSM_H, SSM_P).transpose(0, 1, 3, 2).reshape(2 * G * SSM_P, SSM_H)
    g_bbi = _block_diag_take(dwbi, SSM_H, SSM_P).transpose(0, 1, 3, 2).reshape(2 * G * SSM_P, SSM_H)
    g_cre = _block_diag_take(dwcr, SSM_H, SSM_P)
    g_cim = _block_diag_take(dwci, SSM_H, SSM_P)
    g_bre, g_bim, g_fr, g_fi = _ssm_b_grads(f_r1, f_i1, b_re2, b_im2, g_bbr, g_bbi)
    g_are, g_aim, g_ldt = _ssm_param_grads(
        a_re2, a_im2, ldt2, lbr, lbi, f_r, f_i, dlam_r.reshape(2 * G, SSM_P), dlam_i.reshape(2 * G, SSM_P),
        g_fr.reshape(2 * G, SSM_P), g_fi.reshape(2 * G, SSM_P))

    small_names = ["norm_mix", "q_norm", "k_norm", "ssm_a_re", "ssm_a_im", "ssm_log_dt", "ssm_b_re", "ssm_b_im",
                   "ssm_c_re", "ssm_c_im", "ssm_d", "b_glu", "norm_ple", "norm_final"]
    small_w = dict(norm_mix=norm_mix, q_norm=q_norm, k_norm=k_norm, ssm_a_re=ssm_a_re, ssm_a_im=ssm_a_im,
                   ssm_log_dt=ssm_log_dt, ssm_b_re=ssm_b_re, ssm_b_im=ssm_b_im, ssm_c_re=ssm_c_re, ssm_c_im=ssm_c_im,
                   ssm_d=ssm_d, b_glu=b_glu, norm_ple=norm_ple, norm_final=norm_final)
    small_m = dict(norm_mix=m_norm_mix, q_norm=m_q_norm, k_norm=m_k_norm, ssm_a_re=m_ssm_a_re, ssm_a_im=m_ssm_a_im,
                   ssm_log_dt=m_ssm_log_dt, ssm_b_re=m_ssm_b_re, ssm_b_im=m_ssm_b_im, ssm_c_re=m_ssm_c_re,
                   ssm_c_im=m_ssm_c_im, ssm_d=m_ssm_d, b_glu=m_b_glu, norm_ple=m_norm_ple, norm_final=m_norm_final)
    small_v = dict(norm_mix=v_norm_mix, q_norm=v_q_norm, k_norm=v_k_norm, ssm_a_re=v_ssm_a_re, ssm_a_im=v_ssm_a_im,
                   ssm_log_dt=v_ssm_log_dt, ssm_b_re=v_ssm_b_re, ssm_b_im=v_ssm_b_im, ssm_c_re=v_ssm_c_re,
                   ssm_c_im=v_ssm_c_im, ssm_d=v_ssm_d, b_glu=v_b_glu, norm_ple=v_norm_ple, norm_final=v_norm_final)
    small_g = dict(norm_mix=g_nmix, q_norm=g_qn, k_norm=g_kn, ssm_a_re=g_are, ssm_a_im=g_aim, ssm_log_dt=g_ldt,
                   ssm_b_re=g_bre, ssm_b_im=g_bim, ssm_c_re=g_cre, ssm_c_im=g_cim, ssm_d=g_ssm_d, b_glu=g_bglu,
                   norm_ple=d_nple, norm_final=d_nf)
    small_shapes = [small_w[k].shape for k in small_names]
    packed_sg = _pack_rows([small_g[k] for k in small_names], F32, N_DEV * 8)
    rs = packed_sg.shape[0] // N_DEV

    g_win = _from_zp(g_win_p)
    n_in = D_IN // N_DEV
    n_glu = g_wglu.shape[1] // N_DEV
    n_pp = D // N_DEV
    n_row = wout.shape[0] // N_DEV
    per_dest = []
    for d in range(N_DEV):
        per_dest.append(_pack_rows(
            [g_win[:, d * n_in:(d + 1) * n_in], g_wglu[:, d * n_glu:(d + 1) * n_glu],
             g_wout[d * n_row:(d + 1) * n_row], g_wpg[d * n_row:(d + 1) * n_row],
             g_wpp[:, d * n_pp:(d + 1) * n_pp]], BF16, 16))
    big_send = jnp.stack(per_dest)
    big_land, small_land = _all_to_all([big_send, packed_sg.reshape(N_DEV, rs, LANES)], "exchange_grads")
    big_sum = _sum_blocks(big_land, "sum_weight_grads")
    small_sum = _sum_blocks(small_land, "sum_small_grads")
    (small_all,) = _all_gather([small_sum], "gather_small_grads")
    small_grads = _unpack_rows(small_all.reshape(N_DEV * rs, LANES), small_shapes)
    gw_in, gw_glu, gw_out, gw_pg, gw_pp = _unpack_rows(big_sum, shard_shapes)

    big = [("w_in", w_in, gw_in, m_w_in, v_w_in), ("w_glu", w_glu, gw_glu, m_w_glu, v_w_glu),
           ("w_out", w_out, gw_out, m_w_out, v_w_out), ("w_ple_gate", w_ple_gate, gw_pg, m_w_ple_gate, v_w_ple_gate),
           ("w_ple_proj", w_ple_proj, gw_pp, m_w_ple_proj, v_w_ple_proj)]
    grads, deltas, new_ms, new_vs = {}, {}, {}, {}
    for name, w, g, m, v in big:
        shp = w.shape
        d2, m2, v2 = _adamw(w[0], g, m[0], v[0], "adamw_" + name)
        grads[name], deltas[name], new_ms[name], new_vs[name] = g.reshape(shp), d2.reshape(shp), m2.reshape(shp), v2.reshape(shp)
    sw = _pack_rows([small_w[k] for k in small_names], F32, N_DEV * 8)
    sm = _pack_rows([small_m[k] for k in small_names], F32, N_DEV * 8)
    sv = _pack_rows([small_v[k] for k in small_names], F32, N_DEV * 8)
    sd, smn, svn = _adamw(sw, small_all.reshape(N_DEV * rs, LANES), sm, sv, "adamw_small")
    for k, g, d_, m_, v_ in zip(small_names, small_grads, _unpack_rows(sd, small_shapes),
                                _unpack_rows(smn, small_shapes), _unpack_rows(svn, small_shapes)):
        grads[k], deltas[k], new_ms[k], new_vs[k] = g, d_, m_, v_

    order = ["norm_mix", "w_in", "q_norm", "k_norm", "ssm_a_re", "ssm_a_im", "ssm_log_dt", "ssm_b_re", "ssm_b_im",
             "ssm_c_re", "ssm_c_im", "ssm_d", "w_glu", "b_glu", "w_out", "norm_ple", "w_ple_gate", "w_ple_proj",
             "norm_final"]
    return (loss, grad_x[None], *[grads[k] for k in order], *[deltas[k] for k in order],
            *[new_ms[k] for k in order], *[new_vs[k] for k in order])
```

```python
import math

import numpy as np
import jax
import jax.numpy as jnp
from jax import lax
from jax.experimental import pallas as pl
from jax.experimental.pallas import tpu as pltpu

F32 = jnp.float32
BF16 = jnp.bfloat16

N_DEV = 8
EPS = 1e-6
GRID_W = 64
ROPE_THETA = 10000.0
HEAD_DIM = 128
N_HEADS = 8
N_KV = 2
REP = N_HEADS // N_KV
D_ATTN = N_HEADS * HEAD_DIM
D_KV = N_KV * HEAD_DIM
SSM_H = 16
SSM_P = 64
SLAB = 128
SLAB_G = SLAB // SSM_H
SLAB_S = SLAB_G * SSM_P
SEG = 8
LANES = 128
PG_ROWS = 72
VMEM_LIMIT = 48 << 20

ADAM_LR = 0.001
ADAM_B1 = 0.9
ADAM_B2 = 0.999
ADAM_EPS = 1e-08
ADAM_WD = 0.01
ADAM_STEP = 10


def _pick(n, cands):
    for c in cands:
        if n % c == 0:
            return c
    return n


def _cparams(sem, vmem=VMEM_LIMIT):
    return pltpu.CompilerParams(dimension_semantics=sem, vmem_limit_bytes=vmem)


def _mm(a, b, mode, name, out_dtype=F32, add=None, bias=None, a_blk=False, b_blk=False, out_blk=0):
    w = b.shape[2] if b_blk else out_blk
    if mode == "nn":
        M, K = a.shape
        N = b.shape[0] * w if b_blk else b.shape[1]
    elif mode == "nt":
        M = a.shape[1] if a_blk else a.shape[0]
        N = b.shape[1] if b_blk else b.shape[0]
        K = b.shape[0] * w if b_blk else b.shape[1]
    else:
        K, M = a.shape
        N = b.shape[0] * w if b_blk else b.shape[1]
    tm = _pick(M, (1024, 512, 256))
    tn = _pick(N, (1024, 768, 512, 256))
    tk = _pick(K, (512, 256))
    if mode == "nt" and b_blk:
        tk = w
    elif b_blk or out_blk:
        tn = w
    nk = K // tk
    if mode == "nn":
        a_spec = pl.BlockSpec((tm, tk), lambda i, j, k: (i, k))
        b_spec = (pl.BlockSpec((1, tk, tn), lambda i, j, k: (j, k, 0)) if b_blk
                  else pl.BlockSpec((tk, tn), lambda i, j, k: (k, j)))
        dims = (((1,), (0,)), ((), ()))
    elif mode == "nt":
        a_spec = (pl.BlockSpec((1, tm, tk), lambda i, j, k: (k, i, 0)) if a_blk
                  else pl.BlockSpec((tm, tk), lambda i, j, k: (i, k)))
        b_spec = (pl.BlockSpec((1, tn, tk), lambda i, j, k: (k, j, 0)) if b_blk
                  else pl.BlockSpec((tn, tk), lambda i, j, k: (j, k)))
        dims = (((1,), (1,)), ((), ()))
    else:
        a_spec = pl.BlockSpec((tk, tm), lambda i, j, k: (k, i))
        b_spec = (pl.BlockSpec((1, tk, tn), lambda i, j, k: (j, k, 0)) if b_blk
                  else pl.BlockSpec((tk, tn), lambda i, j, k: (k, j)))
        dims = (((0,), (0,)), ((), ()))
    if out_blk:
        out_spec = pl.BlockSpec((1, tm, tn), lambda i, j, k: (j, i, 0))
        out_shape = jax.ShapeDtypeStruct((N // tn, M, tn), out_dtype)
    else:
        out_spec = pl.BlockSpec((tm, tn), lambda i, j, k: (i, j))
        out_shape = jax.ShapeDtypeStruct((M, N), out_dtype)
    extras, extra_specs = [], []
    if add is not None:
        extras.append(add)
        extra_specs.append(pl.BlockSpec((tm, tn), lambda i, j, k: (i, j)))
    if bias is not None:
        extras.append(bias)
        extra_specs.append(pl.BlockSpec((1, tn), lambda i, j, k: (0, j)))

    def body(a_ref, b_ref, *rest):
        o_ref, acc_ref = rest[-2], rest[-1]
        k = pl.program_id(2)

        @pl.when(k == 0)
        def _():
            acc_ref[...] = jnp.zeros_like(acc_ref)

        av = a_ref[0] if a_blk else a_ref[...]
        bv = b_ref[0] if b_blk else b_ref[...]
        acc_ref[...] += lax.dot_general(av, bv, dims, preferred_element_type=F32)

        @pl.when(k == nk - 1)
        def _():
            out = acc_ref[...]
            idx = 0
            if add is not None:
                out = out + rest[idx][...]
                idx += 1
            if bias is not None:
                out = out + rest[idx][...]
            if out_blk:
                o_ref[0] = out.astype(out_dtype)
            else:
                o_ref[...] = out.astype(out_dtype)

    return pl.pallas_call(
        body, name=name, grid=(M // tm, N // tn, nk),
        in_specs=[a_spec, b_spec] + extra_specs, out_specs=out_spec, out_shape=out_shape,
        scratch_shapes=[pltpu.VMEM((tm, tn), F32)],
        compiler_params=_cparams(("parallel", "parallel", "arbitrary")),
    )(a, b, *extras)


def _rspec(tm, w, cb=0):
    return pl.BlockSpec((tm, w), lambda i: (i, cb))


def _fspec(shape):
    nd = len(shape)
    return pl.BlockSpec(shape, lambda i: (0,) * nd)


def _rowcall(body, name, L, tm, ins, row_outs, acc_outs=()):
    out_shape = [jax.ShapeDtypeStruct((L, w), dt) for w, dt in row_outs]
    out_shape += [jax.ShapeDtypeStruct(s, F32) for s in acc_outs]
    out_specs = [_rspec(tm, w) for w, _ in row_outs] + [_fspec(s) for s in acc_outs]
    return pl.pallas_call(
        body, name=name, grid=(L // tm,),
        in_specs=[s for _, s in ins], out_specs=out_specs, out_shape=out_shape,
        compiler_params=_cparams(("arbitrary",)),
    )(*[a for a, _ in ins])


def _acc(ref, val):
    @pl.when(pl.program_id(0) == 0)
    def _():
        ref[...] = jnp.zeros_like(ref)
    ref[...] += val


def _colsum(v):
    return jnp.sum(v, axis=0, keepdims=True)


def _rms(xv):
    return lax.rsqrt(jnp.mean(xv * xv, axis=-1, keepdims=True) + EPS)


def _rms_bwd(dn, xhat, r, g):
    dng = dn * g
    return r * (dng - xhat * jnp.mean(dng * xhat, axis=-1, keepdims=True))


def _sigmoid(v):
    return jax.nn.sigmoid(v)


def _partner(v):
    w = v.shape[-1]
    lane = lax.broadcasted_iota(jnp.int32, v.shape, v.ndim - 1)
    first_half = (lane % 64) < 32
    return jnp.where(first_half, pltpu.roll(v, w - 32, axis=v.ndim - 1), pltpu.roll(v, 32, axis=v.ndim - 1))


def _norm_in(x, g, name):
    L, D = x.shape
    tm = _pick(L, (512, 256))

    def body(x_ref, g_ref, o_ref):
        xv = x_ref[...]
        o_ref[...] = (xv * _rms(xv) * g_ref[...]).astype(BF16)

    return _rowcall(body, name, L, tm, [(x, _rspec(tm, D)), (g, _fspec(g.shape))], [(D, BF16)])[0]


def _rope_tables(L):
    t = np.arange(L)
    rows = (t // GRID_W).astype(np.float32)
    cols = (t % GRID_W).astype(np.float32)
    n_freq = HEAD_DIM // 4
    inv_freq = np.float32(ROPE_THETA) ** (-np.arange(n_freq, dtype=np.float32) / np.float32(n_freq))
    ar = (rows[:, None] * inv_freq[None, :]).astype(np.float32).astype(np.float64)
    ac = (cols[:, None] * inv_freq[None, :]).astype(np.float32).astype(np.float64)
    cos = np.concatenate([np.cos(ar), np.cos(ar), np.cos(ac), np.cos(ac)], axis=-1).astype(np.float32)
    sin = np.concatenate([-np.sin(ar), np.sin(ar), -np.sin(ac), np.sin(ac)], axis=-1).astype(np.float32)
    return jnp.asarray(cos), jnp.asarray(sin)


def _qkv_prep(z, cos, sin, qn, kn):
    L = z.shape[0]
    tm = _pick(L, (512, 256))
    scale = HEAD_DIM ** -0.5
    kblk = 4 * D_ATTN // D_KV

    def body(q_ref, k_ref, v_ref, cos_ref, sin_ref, qn_ref, kn_ref, qo_ref, ko_ref, vo_ref, kt_ref):
        c, s = cos_ref[...], sin_ref[...]

        def head(xh, w):
            n = xh * _rms(xh) * w
            return n * c + _partner(n) * s

        for h in range(N_HEADS):
            sl = slice(h * HEAD_DIM, (h + 1) * HEAD_DIM)
            qo_ref[:, sl] = (head(q_ref[:, sl], qn_ref[...]) * scale).astype(BF16)
        for h in range(N_KV):
            sl = slice(h * HEAD_DIM, (h + 1) * HEAD_DIM)
            kr = head(k_ref[:, sl], kn_ref[...])
            ko_ref[:, sl] = kr.astype(BF16)
            kt_ref[sl, :] = kr.T.astype(BF16)
        vo_ref[...] = v_ref[...].astype(BF16)

    return pl.pallas_call(
        body, name="qkv_prep", grid=(L // tm,),
        in_specs=[_rspec(tm, D_ATTN, 0), _rspec(tm, D_KV, kblk), _rspec(tm, D_KV, kblk + 1),
                  _rspec(tm, HEAD_DIM), _rspec(tm, HEAD_DIM), _fspec(qn.shape), _fspec(kn.shape)],
        out_specs=[_rspec(tm, D_ATTN), _rspec(tm, D_KV), _rspec(tm, D_KV),
                   pl.BlockSpec((D_KV, tm), lambda i: (0, i))],
        out_shape=[jax.ShapeDtypeStruct((L, D_ATTN), BF16), jax.ShapeDtypeStruct((L, D_KV), BF16),
                   jax.ShapeDtypeStruct((L, D_KV), BF16), jax.ShapeDtypeStruct((D_KV, L), BF16)],
        compiler_params=_cparams(("arbitrary",)),
    )(z, z, z, cos, sin, qn, kn)


def _attn_fwd(q, k, v):
    L = q.shape[0]
    tq = _pick(L, (256, 128))

    def body(q_ref, k_ref, v_ref, o_ref):
        s = lax.dot_general(q_ref[...], k_ref[...], (((1,), (1,)), ((), ())), preferred_element_type=F32)
        e = jnp.exp(s - jnp.max(s, axis=-1, keepdims=True))
        l = jnp.sum(e, axis=-1, keepdims=True)
        o_ref[...] = jnp.dot(e.astype(BF16), v_ref[...], preferred_element_type=F32) / l

    return pl.pallas_call(
        body, name="attn_fwd", grid=(N_HEADS, L // tq),
        in_specs=[pl.BlockSpec((tq, HEAD_DIM), lambda h, i: (i, h)),
                  pl.BlockSpec((L, HEAD_DIM), lambda h, i: (0, h // REP)),
                  pl.BlockSpec((L, HEAD_DIM), lambda h, i: (0, h // REP))],
        out_specs=pl.BlockSpec((tq, HEAD_DIM), lambda h, i: (i, h)),
        out_shape=jax.ShapeDtypeStruct((L, D_ATTN), F32),
        compiler_params=_cparams(("arbitrary", "arbitrary")),
    )(q, k, v)


def _attn_bwd(q, k, v, kt, do):
    L = q.shape[0]
    tq = _pick(L, (256, 128))
    nt = (((1,), (1,)), ((), ()))

    def body(q_ref, do_ref, k_ref, v_ref, kt_ref, dq_ref, dk_ref, dv_ref):
        @pl.when((pl.program_id(1) == 0) & (pl.program_id(2) == 0))
        def _():
            dk_ref[...] = jnp.zeros_like(dk_ref)
            dv_ref[...] = jnp.zeros_like(dv_ref)

        qv, dov = q_ref[...], do_ref[...]
        st = lax.dot_general(k_ref[...], qv, nt, preferred_element_type=F32)
        e = jnp.exp(st - jnp.max(st, axis=0, keepdims=True))
        pt = e * (1.0 / jnp.sum(e, axis=0, keepdims=True))
        dpt = lax.dot_general(v_ref[...], dov, nt, preferred_element_type=F32)
        delta = jnp.sum(pt * dpt, axis=0, keepdims=True)
        dst = (pt * (dpt - delta)).astype(BF16)
        dv_ref[...] += jnp.dot(pt.astype(BF16), dov, preferred_element_type=F32)
        dk_ref[...] += jnp.dot(dst, qv, preferred_element_type=F32)
        dq_ref[...] = jnp.dot(kt_ref[...], dst, preferred_element_type=F32).T

    return pl.pallas_call(
        body, name="attn_bwd", grid=(N_KV, REP, L // tq),
        in_specs=[pl.BlockSpec((tq, HEAD_DIM), lambda g, r, i: (i, g * REP + r)),
                  pl.BlockSpec((tq, HEAD_DIM), lambda g, r, i: (i, g * REP + r)),
                  pl.BlockSpec((L, HEAD_DIM), lambda g, r, i: (0, g)),
                  pl.BlockSpec((L, HEAD_DIM), lambda g, r, i: (0, g)),
                  pl.BlockSpec((HEAD_DIM, L), lambda g, r, i: (g, 0))],
        out_specs=[pl.BlockSpec((tq, HEAD_DIM), lambda g, r, i: (i, g * REP + r)),
                   pl.BlockSpec((L, HEAD_DIM), lambda g, r, i: (0, g)),
                   pl.BlockSpec((L, HEAD_DIM), lambda g, r, i: (0, g))],
        out_shape=[jax.ShapeDtypeStruct((L, D_ATTN), F32), jax.ShapeDtypeStruct((L, D_KV), F32),
                   jax.ShapeDtypeStruct((L, D_KV), F32)],
        compiler_params=_cparams(("arbitrary", "arbitrary", "arbitrary")),
    )(q, do, k, v, kt)


def _seg_perm(a):
    L, C = a.shape
    return a.reshape(SEG, L // SEG, C).transpose(1, 0, 2).reshape(L, C)


def _seg_unperm(a):
    L, C = a.shape
    return a.reshape(L // SEG, SEG, C).transpose(1, 0, 2).reshape(L, C)


def _cmul(ar, ai, br, bi):
    return ar * br - ai * bi, ar * bi + ai * br


def _seg_scan(xr_ref, xi_ref, ar, ai, reverse, n_rows):
    shape = ar.shape
    zero = jnp.zeros(shape, F32)

    def row(r):
        rr = (n_rows - 1 - r) if reverse else r
        return pl.ds(pl.multiple_of(rr * SEG, SEG), SEG)

    def local(r, carry):
        cr, ci = carry
        sl = row(r)
        pr, pi = _cmul(ar, ai, cr, ci)
        nr, ni = pr + xr_ref[sl, :], pi + xi_ref[sl, :]
        xr_ref[sl, :] = nr
        xi_ref[sl, :] = ni
        return nr, ni

    er, ei = lax.fori_loop(0, n_rows, local, (zero, zero))

    pr, pi = ar, ai
    for _ in range(int(math.log2(n_rows))):
        pr, pi = _cmul(pr, pi, pr, pi)
    sub = lax.broadcasted_iota(jnp.int32, shape, 0)
    shift = (SEG - 1) if reverse else 1
    edge = (SEG - 1) if reverse else 0
    inr, ini = zero, zero
    for _ in range(SEG - 1):
        tr, ti = _cmul(pr, pi, inr, ini)
        inr = jnp.where(sub == edge, 0.0, pltpu.roll(tr + er, shift, axis=0))
        ini = jnp.where(sub == edge, 0.0, pltpu.roll(ti + ei, shift, axis=0))

    def fix(r, carry):
        wr, wi = carry
        sl = row(r)
        fr, fi = _cmul(wr, wi, inr, ini)
        xr_ref[sl, :] += fr
        xi_ref[sl, :] += fi
        return _cmul(wr, wi, ar, ai)

    lax.fori_loop(0, n_rows, fix, (ar, ai))
    return inr, ini


def _discretise(a_re, a_im, ldt):
    lr = jnp.minimum(a_re, -1e-4)
    li = a_im
    dt = jnp.exp(ldt)
    mag = jnp.exp(lr * dt)
    lbr = mag * jnp.cos(li * dt)
    lbi = mag * jnp.sin(li * dt)
    den = lr * lr + li * li
    nr = lbr - 1.0
    fr = (nr * lr + lbi * li) / den
    fi = (lbi * lr - nr * li) / den
    return lr, li, dt, lbr, lbi, fr, fi


def _lane_row(v):
    return jnp.concatenate([v[g:g + 1, :] for g in range(v.shape[0])], axis=1)


def _ssm_fill_maps(d, prm, tmp_ref, maps):
    a_re_ref, a_im_ref, ldt_ref, bt_re_ref, bt_im_ref, c_re_ref, c_im_ref = prm
    _, _, _, lbr, lbi, fr, fi = _discretise(a_re_ref[d], a_im_ref[d], ldt_ref[d])

    def fill(dst, piece):
        tmp_ref[...] = jnp.zeros_like(tmp_ref)
        for g in range(SLAB_G):
            tmp_ref[g * SSM_H:(g + 1) * SSM_H, g * SSM_P:(g + 1) * SSM_P] = piece(g)
        dst[...] = tmp_ref[...].astype(BF16)

    wbr, wbi, wcr, wci = maps
    fill(wbr, lambda g: fr[g:g + 1] * bt_re_ref[d, g] - fi[g:g + 1] * bt_im_ref[d, g])
    fill(wbi, lambda g: fr[g:g + 1] * bt_im_ref[d, g] + fi[g:g + 1] * bt_re_ref[d, g])
    fill(wcr, lambda g: c_re_ref[d, g])
    fill(wci, lambda g: c_im_ref[d, g])
    return _lane_row(lbr), _lane_row(lbi)


def _ssm_param_specs():
    pole = pl.BlockSpec((2, SLAB_G, SSM_P), lambda j: (0, j, 0))
    step = pl.BlockSpec((2, SLAB_G, 1), lambda j: (0, j, 0))
    mat = pl.BlockSpec((2, SLAB_G, SSM_H, SSM_P), lambda j: (0, j, 0, 0))
    return [pole, pole, step, mat, mat, mat, mat]


_MAP_SCRATCH = [pltpu.VMEM((SLAB, SLAB_S), F32)] + [pltpu.VMEM((SLAB, SLAB_S), BF16)] * 4
_NT = (((1,), (1,)), ((), ()))


def _ssm_fwd(u, prm, dskip):
    L, C = u.shape
    n_rows = L // SEG
    tc = _pick(L, (512, 256))
    u_spec = pl.BlockSpec((L, SLAB), lambda j: (0, j))
    d_spec = pl.BlockSpec((1, SLAB), lambda j: (0, j))

    def body(u_ref, *rest):
        prm_refs, d_ref, y_ref = rest[:7], rest[7], rest[8]
        tmp_ref, maps, xr_ref, xi_ref = rest[9], rest[10:14], rest[14], rest[15]
        wbr, wbi, wcr, wci = maps
        y_ref[...] = u_ref[...] * d_ref[...]
        for d in range(2):
            lam_r, lam_i = _ssm_fill_maps(d, prm_refs, tmp_ref, maps)

            def inp(c, _):
                sl = pl.ds(pl.multiple_of(c * tc, tc), tc)
                ub = u_ref[sl, :].astype(BF16)
                xr_ref[sl, :] = jnp.dot(ub, wbr[...], preferred_element_type=F32)
                xi_ref[sl, :] = jnp.dot(ub, wbi[...], preferred_element_type=F32)
                return 0

            lax.fori_loop(0, L // tc, inp, 0)
            ar = jnp.broadcast_to(lam_r, (SEG, SLAB_S))
            ai = jnp.broadcast_to(lam_i, (SEG, SLAB_S))
            _seg_scan(xr_ref, xi_ref, ar, ai, d == 1, n_rows)

            def outp(c, _):
                sl = pl.ds(pl.multiple_of(c * tc, tc), tc)
                y_ref[sl, :] += (
                    lax.dot_general(xr_ref[sl, :].astype(BF16), wcr[...], _NT, preferred_element_type=F32)
                    - lax.dot_general(xi_ref[sl, :].astype(BF16), wci[...], _NT, preferred_element_type=F32))
                return 0

            lax.fori_loop(0, L // tc, outp, 0)

    return pl.pallas_call(
        body, name="ssm_fwd", grid=(C // SLAB,),
        in_specs=[u_spec] + _ssm_param_specs() + [d_spec],
        out_specs=u_spec, out_shape=jax.ShapeDtypeStruct((L, C), F32),
        scratch_shapes=_MAP_SCRATCH + [pltpu.VMEM((L, SLAB_S), F32)] * 2,
        compiler_params=_cparams(("arbitrary",)),
    )(u, *prm, dskip)


def _ssm_bwd(u, dy, prm, dskip):
    L, C = u.shape
    n_rows = L // SEG
    n_slab = C // SLAB
    tc = _pick(L, (512, 256))
    u_spec = pl.BlockSpec((L, SLAB), lambda j: (0, j))
    d_spec = pl.BlockSpec((1, SLAB), lambda j: (0, j))
    pg_spec = pl.BlockSpec((1, 2, PG_ROWS, SLAB_S), lambda j: (j, 0, 0, 0))

    def body(u_ref, dy_ref, *rest):
        prm_refs, d_ref, du_ref, pg_ref = rest[:7], rest[7], rest[8], rest[9]
        tmp_ref, maps, acc_ref = rest[10], rest[11:15], rest[15]
        xr_ref, xi_ref, gr_ref, gi_ref = rest[16:20]
        wbr, wbi, wcr, wci = maps
        du_ref[...] = dy_ref[...] * d_ref[...]
        pg_ref[...] = jnp.zeros_like(pg_ref)
        pg_ref[0, 0, 66:67, 0:SLAB] = _colsum(dy_ref[...] * u_ref[...])
        for d in range(2):
            lam_r, lam_i = _ssm_fill_maps(d, prm_refs, tmp_ref, maps)

            def inp(c, _):
                sl = pl.ds(pl.multiple_of(c * tc, tc), tc)
                ub = u_ref[sl, :].astype(BF16)
                dyb = dy_ref[sl, :].astype(BF16)
                xr_ref[sl, :] = jnp.dot(ub, wbr[...], preferred_element_type=F32)
                xi_ref[sl, :] = jnp.dot(ub, wbi[...], preferred_element_type=F32)
                gr_ref[sl, :] = jnp.dot(dyb, wcr[...], preferred_element_type=F32)
                gi_ref[sl, :] = -jnp.dot(dyb, wci[...], preferred_element_type=F32)
                return 0

            lax.fori_loop(0, L // tc, inp, 0)
            ar = jnp.broadcast_to(lam_r, (SEG, SLAB_S))
            ai = jnp.broadcast_to(lam_i, (SEG, SLAB_S))
            inr, ini = _seg_scan(xr_ref, xi_ref, ar, ai, d == 1, n_rows)
            _seg_scan(gr_ref, gi_ref, ar, -ai, d == 0, n_rows)

            def pole(r, carry):
                accr, acci, pr, pi = carry
                rr = (n_rows - 1 - r) if d == 1 else r
                sl = pl.ds(pl.multiple_of(rr * SEG, SEG), SEG)
                lr, li = gr_ref[sl, :], gi_ref[sl, :]
                accr = accr + lr * pr + li * pi
                acci = acci + li * pr - lr * pi
                return accr, acci, xr_ref[sl, :], xi_ref[sl, :]

            zero = jnp.zeros((SEG, SLAB_S), F32)
            accr, acci, _, _ = lax.fori_loop(0, n_rows, pole, (zero, zero, inr, ini))
            pg_ref[0, d, 64:65, :] = _colsum(accr)
            pg_ref[0, d, 65:66, :] = _colsum(acci)

            acc_ref[...] = jnp.zeros_like(acc_ref)

            def outp(c, _):
                sl = pl.ds(pl.multiple_of(c * tc, tc), tc)
                lrb, lib = gr_ref[sl, :].astype(BF16), gi_ref[sl, :].astype(BF16)
                du_ref[sl, :] += (lax.dot_general(lrb, wbr[...], _NT, preferred_element_type=F32)
                                  + lax.dot_general(lib, wbi[...], _NT, preferred_element_type=F32))
                ut = u_ref[sl, :].T.astype(BF16)
                dyt = dy_ref[sl, :].T.astype(BF16)
                acc_ref[0] += jnp.dot(ut, lrb, preferred_element_type=F32)
                acc_ref[1] += jnp.dot(ut, lib, preferred_element_type=F32)
                acc_ref[2] += jnp.dot(dyt, xr_ref[sl, :].astype(BF16), preferred_element_type=F32)
                acc_ref[3] -= jnp.dot(dyt, xi_ref[sl, :].astype(BF16), preferred_element_type=F32)
                return 0

            lax.fori_loop(0, L // tc, outp, 0)
            for m in range(4):
                for g in range(SLAB_G):
                    lanes = slice(g * SSM_P, (g + 1) * SSM_P)
                    pg_ref[0, d, m * SSM_H:(m + 1) * SSM_H, lanes] = acc_ref[m, g * SSM_H:(g + 1) * SSM_H, lanes]

    return pl.pallas_call(
        body, name="ssm_bwd", grid=(n_slab,),
        in_specs=[u_spec, u_spec] + _ssm_param_specs() + [d_spec],
        out_specs=[u_spec, pg_spec],
        out_shape=[jax.ShapeDtypeStruct((L, C), F32), jax.ShapeDtypeStruct((n_slab, 2, PG_ROWS, SLAB_S), F32)],
        scratch_shapes=_MAP_SCRATCH + [pltpu.VMEM((4, SLAB, SLAB_S), F32)] + [pltpu.VMEM((L, SLAB_S), F32)] * 4,
        compiler_params=_cparams(("arbitrary",), 60 << 20),
    )(u, dy, *prm, dskip)


def _ssm_param_grads(pg, prm):
    n_slab = pg.shape[0]
    G = n_slab * SLAB_G
    pg_spec = pl.BlockSpec((1, 2, PG_ROWS, SLAB_S), lambda j: (j, 0, 0, 0))
    pole, _, step, mat = _ssm_param_specs()[:4]

    def body(pg_ref, a_re_ref, a_im_ref, ldt_ref, bt_re_ref, bt_im_ref,
             dbr_ref, dbi_ref, dcr_ref, dci_ref, dar_ref, dai_ref, dldt_ref, dd_ref):
        dd_ref[...] = pg_ref[0, 0, 66:67, 0:SLAB]
        for d in range(2):
            a_r = a_re_ref[d]
            lr, li, dt, lbr, lbi, f_r, f_i = _discretise(a_r, a_im_ref[d], ldt_ref[d])
            gfr_rows, gfi_rows, glr_rows, gli_rows = [], [], [], []
            for g in range(SLAB_G):
                lanes = slice(g * SSM_P, (g + 1) * SSM_P)
                gbr, gbi = pg_ref[0, d, 0:SSM_H, lanes], pg_ref[0, d, SSM_H:2 * SSM_H, lanes]
                b_r, b_i = bt_re_ref[d, g], bt_im_ref[d, g]
                fr, fi = f_r[g:g + 1], f_i[g:g + 1]
                dbr_ref[d, g] = fr * gbr + fi * gbi
                dbi_ref[d, g] = fr * gbi - fi * gbr
                gfr_rows.append(_colsum(gbr * b_r + gbi * b_i))
                gfi_rows.append(_colsum(gbi * b_r - gbr * b_i))
                dcr_ref[d, g] = pg_ref[0, d, 2 * SSM_H:3 * SSM_H, lanes]
                dci_ref[d, g] = pg_ref[0, d, 3 * SSM_H:4 * SSM_H, lanes]
                glr_rows.append(pg_ref[0, d, 64:65, lanes])
                gli_rows.append(pg_ref[0, d, 65:66, lanes])
            gfr, gfi = jnp.concatenate(gfr_rows, axis=0), jnp.concatenate(gfi_rows, axis=0)
            glr, gli = jnp.concatenate(glr_rows, axis=0), jnp.concatenate(gli_rows, axis=0)
            den = lr * lr + li * li
            ir, ii = lr / den, -li / den
            tr, ti = _cmul(ir, -ii, gfr, gfi)
            glbr, glbi = glr + tr, gli + ti
            qr, qi = _cmul(f_r, f_i, ir, ii)
            dlr, dli = _cmul(-qr, qi, gfr, gfi)
            zr, zi = _cmul(lbr, -lbi, glbr, glbi)
            dlr = dlr + dt * zr
            dli = dli + dt * zi
            dar_ref[d] = jnp.where(a_r < -1e-4, dlr, jnp.where(a_r == -1e-4, 0.5 * dlr, 0.0))
            dai_ref[d] = dli
            dldt_ref[d] = jnp.sum(lr * zr + li * zi, axis=-1, keepdims=True) * dt

    a_re, a_im, ldt, bt_re, bt_im = prm[:5]
    mshape = jax.ShapeDtypeStruct(bt_re.shape, F32)
    pshape = jax.ShapeDtypeStruct(a_re.shape, F32)
    return pl.pallas_call(
        body, name="ssm_param_grads", grid=(n_slab,),
        in_specs=[pg_spec, pole, pole, step, mat, mat],
        out_specs=[mat, mat, mat, mat, pole, pole, step, pl.BlockSpec((1, SLAB), lambda j: (0, j))],
        out_shape=[mshape, mshape, mshape, mshape, pshape, pshape, jax.ShapeDtypeStruct(ldt.shape, F32),
                   jax.ShapeDtypeStruct((1, n_slab * SLAB), F32)],
        compiler_params=_cparams(("arbitrary",)),
    )(pg, a_re, a_im, ldt, bt_re, bt_im)


def _peer(k, x, y, c):
    return (1 - x if k & 4 else x, 1 - y if k & 2 else y, 1 - c if k & 1 else c)


def _dev_index(pos):
    return 4 * pos[0] + 2 * pos[1] + pos[2]


def _all_gather(xs, name):
    n = len(xs)
    any_spec = pl.BlockSpec(memory_space=pl.ANY)

    def body(*refs):
        x_refs, out_refs = refs[:n], refs[n:2 * n]
        send_sems, recv_sems, local_sems = refs[2 * n:]
        x, y, c = lax.axis_index("x"), lax.axis_index("y"), lax.axis_index("c")
        me, sibling = (x, y, c), (x, y, 1 - c)
        chips = [(1 - x, y), (x, 1 - y), (1 - x, 1 - y)]

        def copy(a, k, block, to, src=None):
            dst = out_refs[a].at[_dev_index(block)]
            return pltpu.make_async_remote_copy(
                src_ref=dst if src is None else src, dst_ref=dst,
                send_sem=send_sems.at[a, k], recv_sem=recv_sems.at[a, k],
                device_id=to, device_id_type=pl.DeviceIdType.MESH)

        mine = [pltpu.make_async_copy(x_refs[a], out_refs[a].at[_dev_index(me)], local_sems.at[a]) for a in range(n)]
        for cp in mine:
            cp.start()
        first = []
        for a in range(n):
            first.append(copy(a, 0, me, sibling, src=x_refs[a]))
            first += [copy(a, 1 + j, me, (*chip, c), src=x_refs[a]) for j, chip in enumerate(chips)]
        for cp in first:
            cp.start()
        passed = []
        for j, chip in enumerate(chips):
            for a in range(n):
                copy(a, 1 + j, (*chip, c), me).wait_recv()
                fwd = copy(a, 4 + j, (*chip, c), sibling)
                fwd.start()
                passed.append(fwd)
        for a in range(n):
            copy(a, 0, sibling, me).wait_recv()
            for j, chip in enumerate(chips):
                copy(a, 4 + j, (*chip, 1 - c), me).wait_recv()
        for cp in first + passed:
            cp.wait_send()
        for cp in mine:
            cp.wait()

    return pl.pallas_call(
        body, name=name,
        out_shape=[jax.ShapeDtypeStruct((N_DEV,) + v.shape, v.dtype) for v in xs],
        in_specs=[any_spec] * n, out_specs=[any_spec] * n,
        scratch_shapes=[pltpu.SemaphoreType.DMA((n, 7)), pltpu.SemaphoreType.DMA((n, 7)),
                        pltpu.SemaphoreType.DMA((n,))],
    )(*xs)


def _all_to_all(xs, name):
    n = len(xs)
    any_spec = pl.BlockSpec(memory_space=pl.ANY)

    def body(*refs):
        x_refs, out_refs = refs[:n], refs[n:2 * n]
        send_sems, recv_sems, local_sems = refs[2 * n:]
        x, y, c = lax.axis_index("x"), lax.axis_index("y"), lax.axis_index("c")
        me = _dev_index((x, y, c))

        def copy(a, k):
            peer = _peer(k, x, y, c)
            return pltpu.make_async_remote_copy(
                src_ref=x_refs[a].at[_dev_index(peer)], dst_ref=out_refs[a].at[me],
                send_sem=send_sems.at[a, k - 1], recv_sem=recv_sems.at[a, k - 1],
                device_id=peer, device_id_type=pl.DeviceIdType.MESH)

        def arrival(a, k):
            land = out_refs[a].at[_dev_index(_peer(k, x, y, c))]
            return pltpu.make_async_remote_copy(
                src_ref=land, dst_ref=land, send_sem=send_sems.at[a, k - 1], recv_sem=recv_sems.at[a, k - 1],
                device_id=_peer(k, x, y, c), device_id_type=pl.DeviceIdType.MESH)

        mine = [pltpu.make_async_copy(x_refs[a].at[me], out_refs[a].at[me], local_sems.at[a]) for a in range(n)]
        for cp in mine:
            cp.start()
        sends = [copy(a, k) for k in range(1, N_DEV) for a in range(n)]
        for cp in sends:
            cp.start()
        for k in range(1, N_DEV):
            for a in range(n):
                arrival(a, k).wait_recv()
        for cp in sends:
            cp.wait_send()
        for cp in mine:
            cp.wait()

    return pl.pallas_call(
        body, name=name,
        out_shape=[jax.ShapeDtypeStruct(v.shape, v.dtype) for v in xs],
        in_specs=[any_spec] * n, out_specs=[any_spec] * n,
        scratch_shapes=[pltpu.SemaphoreType.DMA((n, 7)), pltpu.SemaphoreType.DMA((n, 7)),
                        pltpu.SemaphoreType.DMA((n,))],
    )(*xs)


def _sum_blocks(x, name):
    _, R, W = x.shape

    def body(x_ref, o_ref):
        acc = x_ref[0].astype(F32)
        for d in range(1, N_DEV):
            acc = acc + x_ref[d].astype(F32)
        o_ref[...] = acc

    return pl.pallas_call(body, name=name, out_shape=jax.ShapeDtypeStruct((R, W), F32),
                          compiler_params=pltpu.CompilerParams(vmem_limit_bytes=VMEM_LIMIT))(x)


def _adam_update(w, g, m, v):
    mn = ADAM_B1 * m + (1.0 - ADAM_B1) * g
    vn = ADAM_B2 * v + (1.0 - ADAM_B2) * (g * g)
    m_hat = mn / (1.0 - ADAM_B1 ** ADAM_STEP)
    v_hat = vn / (1.0 - ADAM_B2 ** ADAM_STEP)
    return -ADAM_LR * (m_hat / (jnp.sqrt(v_hat) + ADAM_EPS) + ADAM_WD * w), mn, vn


def _row_tile(R, W, budget):
    padded = -(-W // LANES) * LANES * 4
    if R * padded <= budget:
        return R
    return _pick(R, [t for t in (2048, 1024, 512, 256, 128, 64, 32, 16, 8) if t * padded <= budget])


def _adamw(w, g, m, v, name):
    R, W = w.shape
    tr = _row_tile(R, W, 1 << 20)

    def body(w_ref, g_ref, m_ref, v_ref, d_ref, mo_ref, vo_ref):
        d_ref[...], mo_ref[...], vo_ref[...] = _adam_update(w_ref[...], g_ref[...], m_ref[...], v_ref[...])

    spec = pl.BlockSpec((tr, W), lambda i: (i, 0))
    shp = jax.ShapeDtypeStruct((R, W), F32)
    return pl.pallas_call(
        body, name=name, grid=(R // tr,), in_specs=[spec] * 4, out_specs=[spec] * 3, out_shape=[shp] * 3,
        compiler_params=_cparams(("arbitrary",)),
    )(w, g, m, v)


def _adamw_reduce(w, land, m, v, name):
    R, W = w.shape
    tr = _row_tile(R, W, 1 << 20)

    def body(w_ref, l_ref, m_ref, v_ref, g_ref, d_ref, mo_ref, vo_ref):
        g = l_ref[0].astype(F32)
        for d in range(1, N_DEV):
            g = g + l_ref[d].astype(F32)
        g_ref[...] = g
        d_ref[...], mo_ref[...], vo_ref[...] = _adam_update(w_ref[...], g, m_ref[...], v_ref[...])

    spec = pl.BlockSpec((tr, W), lambda i: (i, 0))
    lspec = pl.BlockSpec((N_DEV, tr, W), lambda i: (0, i, 0))
    shp = jax.ShapeDtypeStruct((R, W), F32)
    return pl.pallas_call(
        body, name=name, grid=(R // tr,), in_specs=[spec, lspec, spec, spec], out_specs=[spec] * 4,
        out_shape=[shp] * 4, compiler_params=_cparams(("arbitrary",)),
    )(w, land, m, v)


def _to_zp(w):
    a, kv = D_ATTN, D_KV
    return jnp.concatenate([w[:, :a], w[:, a + 2 * kv:], w[:, a:a + 2 * kv]], axis=1)


def _gelu(v):
    c = math.sqrt(2.0 / math.pi)
    return 0.5 * v * (1.0 + jnp.tanh(c * (v + 0.044715 * v * v * v)))


def _gelu_grad(v):
    c = math.sqrt(2.0 / math.pi)
    t = jnp.tanh(c * (v + 0.044715 * v * v * v))
    return 0.5 * (1.0 + t) + 0.5 * v * (1.0 - t * t) * c * (1.0 + 3.0 * 0.044715 * v * v)


def kernel(x, p, norm_mix, w_in, q_norm, k_norm, ssm_a_re, ssm_a_im, ssm_log_dt, ssm_b_re, ssm_b_im, ssm_c_re, ssm_c_im, ssm_d, w_glu, b_glu, w_out, norm_ple, w_ple_gate, w_ple_proj, norm_final, loss_target, m_norm_mix, m_w_in, m_q_norm, m_k_norm, m_ssm_a_re, m_ssm_a_im, m_ssm_log_dt, m_ssm_b_re, m_ssm_b_im, m_ssm_c_re, m_ssm_c_im, m_ssm_d, m_w_glu, m_b_glu, m_w_out, m_norm_ple, m_w_ple_gate, m_w_ple_proj, m_norm_final, v_norm_mix, v_w_in, v_q_norm, v_k_norm, v_ssm_a_re, v_ssm_a_im, v_ssm_log_dt, v_ssm_b_re, v_ssm_b_im, v_ssm_c_re, v_ssm_c_im, v_ssm_d, v_w_glu, v_b_glu, v_w_out, v_norm_ple, v_w_ple_gate, v_w_ple_proj, v_norm_final):
    L, D = x.shape[1], x.shape[2]
    D_SSM = ssm_d.shape[1]
    G = D_SSM // SSM_H
    n_slab = D_SSM // SLAB
    n_in = w_in.shape[2]
    D_IN = n_in * N_DEV
    n_pp = w_ple_proj.shape[2]
    n_glu = w_glu.shape[2]
    xs = x[0]
    ps = p[0, 0]
    tgt = loss_target[0]

    win3, wglu3, wout3, wpg3, wpp3 = _all_gather(
        [w_in[0].astype(BF16), w_glu[0].astype(BF16), w_out[0].astype(BF16), w_ple_gate[0].astype(BF16),
         w_ple_proj[0].astype(BF16)], "gather_weights")
    win_p = _to_zp(win3.transpose(1, 0, 2).reshape(D, D_IN))
    wout = wout3.reshape(-1, D)
    wpg = wpg3.reshape(-1, D)

    ssm_prm = (ssm_a_re[0], ssm_a_im[0], ssm_log_dt[0].reshape(2, G, 1),
               ssm_b_re[0].transpose(0, 1, 3, 2), ssm_b_im[0].transpose(0, 1, 3, 2), ssm_c_re[0], ssm_c_im[0])

    cos, sin = _rope_tables(L)
    hn = _norm_in(xs, norm_mix, "norm_mix")
    z = _mm(hn, win_p, "nn", "in_proj")
    qr, kr, vb, kt = _qkv_prep(z, cos, sin, q_norm, k_norm)
    o = _attn_fwd(qr, kr, vb)
    u_off = 2 * D_ATTN
    u_perm = _seg_perm(z[:, u_off:u_off + D_SSM])
    ys = _seg_unperm(_ssm_fwd(u_perm, ssm_prm, ssm_d))

    tm = _pick(L, (256,))

    def gelu_body(y_ref, o_ref):
        o_ref[...] = _gelu(y_ref[...]).astype(BF16)

    (gy,) = _rowcall(gelu_body, "gelu", L, tm, [(ys, _rspec(tm, D_SSM))], [(D_SSM, BF16)])
    glu = _mm(gy, wglu3, "nn", "glu_proj", bias=b_glu, b_blk=True)

    def mix_body(o_ref, ga_ref, gla_ref, glb_ref, gs_ref, cat_ref):
        ga, gs = ga_ref[...], gs_ref[...]
        cat_ref[:, :D_ATTN] = (o_ref[...] * ga * _sigmoid(ga)).astype(BF16)
        cat_ref[:, D_ATTN:] = (gla_ref[...] * _sigmoid(glb_ref[...]) * gs * _sigmoid(gs)).astype(BF16)

    (cat,) = _rowcall(mix_body, "mix", L, tm,
                      [(o, _rspec(tm, D_ATTN)), (z, _rspec(tm, D_ATTN, 1)), (glu, _rspec(tm, D_SSM, 0)),
                       (glu, _rspec(tm, D_SSM, 1)), (z, _rspec(tm, D_SSM, 3))], [(D_ATTN + D_SSM, BF16)])
    h1 = _mm(cat, wout, "nn", "out_proj", add=xs)
    n2 = _norm_in(h1, norm_ple, "norm_ple")
    gpre = _mm(n2, wpg, "nn", "ple_gate")
    pb = ps.astype(BF16)
    pp = _mm(pb, wpp3, "nn", "ple_proj", b_blk=True)

    nf = norm_final.reshape(1, D)

    def tail_body(h1_ref, gp_ref, pp_ref, t_ref, g_ref, dh2_ref, dpp_ref, dsg_ref, loss_ref, dg_ref):
        gate = _sigmoid(gp_ref[...])
        ppv = pp_ref[...]
        h2 = h1_ref[...] + gate * ppv
        r = _rms(h2)
        hh = h2 * r
        err = hh * g_ref[...] - t_ref[...]
        _acc(loss_ref, jnp.broadcast_to(0.5 * jnp.sum(jnp.mean(err * err, axis=-1, keepdims=True)), loss_ref.shape))
        dy = err * (1.0 / D)
        _acc(dg_ref, _colsum(dy * hh))
        dh2 = _rms_bwd(dy, hh, r, g_ref[...])
        dh2_ref[...] = dh2
        dpp_ref[...] = (dh2 * gate).astype(BF16)
        dsg_ref[...] = (dh2 * ppv * gate * (1.0 - gate)).astype(BF16)

    dh2, dpp, dsg, loss_acc, d_nf = _rowcall(
        tail_body, "tail", L, tm,
        [(h1, _rspec(tm, D)), (gpre, _rspec(tm, D)), (pp, _rspec(tm, D)), (tgt, _rspec(tm, D)), (nf, _fspec(nf.shape))],
        [(D, F32), (D, BF16), (D, BF16)], [(1, LANES), (1, D)])
    loss = lax.psum(loss_acc[0, 0], ("x", "y", "c"))

    g_wpp3 = _mm(pb, dpp, "tn", "d_ple_proj", out_dtype=BF16, out_blk=n_pp)
    g_wpg = _mm(n2, dsg, "tn", "d_ple_gate", out_dtype=BF16)
    dn2 = _mm(dsg, wpg, "nt", "d_norm_ple_in")

    def ple_bwd_body(h1_ref, dn_ref, dh2_ref, g_ref, dh1_ref, dh1b_ref, dg_ref):
        h1v = h1_ref[...]
        r = _rms(h1v)
        hh = h1v * r
        dn = dn_ref[...]
        _acc(dg_ref, _colsum(dn * hh))
        dh1 = dh2_ref[...] + _rms_bwd(dn, hh, r, g_ref[...])
        dh1_ref[...] = dh1
        dh1b_ref[...] = dh1.astype(BF16)

    dh1, dh1b, d_nple = _rowcall(
        ple_bwd_body, "ple_bwd", L, tm,
        [(h1, _rspec(tm, D)), (dn2, _rspec(tm, D)), (dh2, _rspec(tm, D)), (norm_ple, _fspec(norm_ple.shape))],
        [(D, F32), (D, BF16)], [(1, D)])

    dcat = _mm(dh1b, wout, "nt", "d_cat")
    g_wout = _mm(cat, dh1b, "tn", "d_out_proj", out_dtype=BF16)

    def mix_bwd_body(dca_ref, dcs_ref, o_ref, ga_ref, gla_ref, glb_ref, gs_ref,
                     do_ref, dga_ref, dgs_ref, dglu_ref, db_ref):
        dca, dcs, ga, gs = dca_ref[...], dcs_ref[...], ga_ref[...], gs_ref[...]
        sa, ss, sb = _sigmoid(ga), _sigmoid(gs), _sigmoid(glb_ref[...])
        gla = gla_ref[...]
        do_ref[...] = (dca * ga * sa).astype(BF16)
        dga_ref[...] = (dca * o_ref[...] * sa * (1.0 + ga * (1.0 - sa))).astype(BF16)
        dgs_ref[...] = (dcs * gla * sb * ss * (1.0 + gs * (1.0 - ss))).astype(BF16)
        dy2 = dcs * gs * ss
        da, db = dy2 * sb, dy2 * gla * sb * (1.0 - sb)
        dglu_ref[:, :D_SSM] = da.astype(BF16)
        dglu_ref[:, D_SSM:] = db.astype(BF16)
        _acc(db_ref, jnp.concatenate([_colsum(da), _colsum(db)], axis=-1))

    do, dga, dgs, dglu, g_bglu = _rowcall(
        mix_bwd_body, "mix_bwd", L, tm,
        [(dcat, _rspec(tm, D_ATTN, 0)), (dcat, _rspec(tm, D_SSM, 1)), (o, _rspec(tm, D_ATTN)),
         (z, _rspec(tm, D_ATTN, 1)), (glu, _rspec(tm, D_SSM, 0)), (glu, _rspec(tm, D_SSM, 1)),
         (z, _rspec(tm, D_SSM, 3))],
        [(D_ATTN, BF16), (D_ATTN, BF16), (D_SSM, BF16), (2 * D_SSM, BF16)], [(1, 2 * D_SSM)])

    g_wglu3 = _mm(gy, dglu, "tn", "d_glu_proj", out_dtype=BF16, out_blk=n_glu)
    dgy = _mm(dglu, wglu3, "nt", "d_gelu_out", b_blk=True)

    def gelu_bwd_body(dg_ref, y_ref, o_ref):
        o_ref[...] = dg_ref[...] * _gelu_grad(y_ref[...])

    (dys,) = _rowcall(gelu_bwd_body, "gelu_bwd", L, tm,
                      [(dgy, _rspec(tm, D_SSM)), (ys, _rspec(tm, D_SSM))], [(D_SSM, F32)])
    du_perm, pg = _ssm_bwd(u_perm, _seg_perm(dys), ssm_prm, ssm_d)
    du = _seg_unperm(du_perm)

    dqs, dkr, dvv = _attn_bwd(qr, kr, vb, kt, do)

    scale = HEAD_DIM ** -0.5
    kblk = 4 * D_ATTN // D_KV
    tmq = _pick(L, (512, 256))

    def qkv_bwd_body(dq_ref, dk_ref, dv_ref, q_ref, k_ref, cos_ref, sin_ref, qn_ref, kn_ref,
                     dqo_ref, dko_ref, dvo_ref, dqn_ref, dkn_ref):
        c, s = cos_ref[...], sin_ref[...]

        def head(g, xh, w):
            dn = g * c + _partner(g * s)
            r = _rms(xh)
            xhat = xh * r
            return _rms_bwd(dn, xhat, r, w), _colsum(dn * xhat)

        dqn = jnp.zeros((1, HEAD_DIM), F32)
        for h in range(N_HEADS):
            sl = slice(h * HEAD_DIM, (h + 1) * HEAD_DIM)
            dx, dw = head(dq_ref[:, sl] * scale, q_ref[:, sl], qn_ref[...])
            dqo_ref[:, sl] = dx.astype(BF16)
            dqn = dqn + dw
        dkn = jnp.zeros((1, HEAD_DIM), F32)
        for h in range(N_KV):
            sl = slice(h * HEAD_DIM, (h + 1) * HEAD_DIM)
            dx, dw = head(dk_ref[:, sl], k_ref[:, sl], kn_ref[...])
            dko_ref[:, sl] = dx.astype(BF16)
            dkn = dkn + dw
        dvo_ref[...] = dv_ref[...].astype(BF16)
        _acc(dqn_ref, dqn)
        _acc(dkn_ref, dkn)

    dq, dk, dv, g_qn, g_kn = _rowcall(
        qkv_bwd_body, "qkv_bwd", L, tmq,
        [(dqs, _rspec(tmq, D_ATTN)), (dkr, _rspec(tmq, D_KV)), (dvv, _rspec(tmq, D_KV)),
         (z, _rspec(tmq, D_ATTN, 0)), (z, _rspec(tmq, D_KV, kblk)), (cos, _rspec(tmq, HEAD_DIM)),
         (sin, _rspec(tmq, HEAD_DIM)), (q_norm, _fspec(q_norm.shape)), (k_norm, _fspec(k_norm.shape))],
        [(D_ATTN, BF16), (D_KV, BF16), (D_KV, BF16)], [(1, HEAD_DIM), (1, HEAD_DIM)])

    dz3 = (jnp.concatenate([dq, dk, dv, dga, du.astype(BF16), dgs], axis=1)
           .reshape(L, N_DEV, n_in).transpose(1, 0, 2))
    g_win3 = _mm(hn, dz3, "tn", "d_in_proj", out_dtype=BF16, b_blk=True, out_blk=n_in)
    dhn = _mm(dz3, win3, "nt", "d_norm_mix_in", a_blk=True, b_blk=True)

    def in_bwd_body(x_ref, dn_ref, dh1_ref, g_ref, dx_ref, dg_ref):
        xv = x_ref[...]
        r = _rms(xv)
        hh = xv * r
        dn = dn_ref[...]
        _acc(dg_ref, _colsum(dn * hh))
        dx_ref[...] = dh1_ref[...] + _rms_bwd(dn, hh, r, g_ref[...])

    grad_x, g_nmix = _rowcall(
        in_bwd_body, "in_bwd", L, tm,
        [(xs, _rspec(tm, D)), (dhn, _rspec(tm, D)), (dh1, _rspec(tm, D)), (norm_mix, _fspec(norm_mix.shape))],
        [(D, F32)], [(1, D)])

    tiny_parts = [g_nmix, g_bglu, d_nple, d_nf, g_qn, g_kn]
    tiny_flat = jnp.concatenate([t.reshape(-1) for t in tiny_parts])
    tiny_rows = N_DEV * 16
    tiny = jnp.pad(tiny_flat, (0, tiny_rows * LANES - tiny_flat.shape[0])).reshape(N_DEV, tiny_rows // N_DEV, LANES)
    pg_send = pg.reshape(N_DEV, (n_slab // N_DEV) * 2 * PG_ROWS, SLAB_S)
    land = _all_to_all(
        [g_win3, g_wglu3, g_wout.reshape(N_DEV, -1, D), g_wpg.reshape(N_DEV, -1, D), g_wpp3, pg_send, tiny],
        "exchange_grads")
    pg_sum = _sum_blocks(land[5], "sum_ssm_grads")
    tiny_sum = _sum_blocks(land[6], "sum_tiny_grads")
    pg_all, tiny_all = _all_gather([pg_sum, tiny_sum], "gather_small_grads")
    (g_bt_re, g_bt_im, g_c_re, g_c_im, g_a_re, g_a_im, g_ldt, g_skip) = _ssm_param_grads(
        pg_all.reshape(n_slab, 2, PG_ROWS, SLAB_S), ssm_prm)
    tiny_all = tiny_all.reshape(-1)
    tiny_grads, off = [], 0
    for t in tiny_parts:
        tiny_grads.append(tiny_all[off:off + t.size].reshape(t.shape))
        off += t.size
    r_nmix, r_bglu, r_nple, r_nf, r_qn, r_kn = tiny_grads

    grads, deltas, new_ms, new_vs = {}, {}, {}, {}
    big = [("w_in", w_in, land[0], m_w_in, v_w_in), ("w_glu", w_glu, land[1], m_w_glu, v_w_glu),
           ("w_out", w_out, land[2], m_w_out, v_w_out), ("w_ple_gate", w_ple_gate, land[3], m_w_ple_gate, v_w_ple_gate),
           ("w_ple_proj", w_ple_proj, land[4], m_w_ple_proj, v_w_ple_proj)]
    for name, w, ld, m, v in big:
        shp = w.shape
        outs = _adamw_reduce(w[0], ld, m[0], v[0], "adamw_" + name)
        grads[name], deltas[name], new_ms[name], new_vs[name] = [t.reshape(shp) for t in outs]
    small = [("norm_mix", norm_mix, r_nmix, m_norm_mix, v_norm_mix, (1, D)),
             ("q_norm", q_norm, r_qn, m_q_norm, v_q_norm, (1, HEAD_DIM)),
             ("k_norm", k_norm, r_kn, m_k_norm, v_k_norm, (1, HEAD_DIM)),
             ("ssm_a_re", ssm_a_re, g_a_re, m_ssm_a_re, v_ssm_a_re, (2 * G, SSM_P)),
             ("ssm_a_im", ssm_a_im, g_a_im, m_ssm_a_im, v_ssm_a_im, (2 * G, SSM_P)),
             ("ssm_log_dt", ssm_log_dt, g_ldt, m_ssm_log_dt, v_ssm_log_dt, (2, G)),
             ("ssm_b_re", ssm_b_re, g_bt_re.transpose(0, 1, 3, 2), m_ssm_b_re, v_ssm_b_re, (2 * G * SSM_P, SSM_H)),
             ("ssm_b_im", ssm_b_im, g_bt_im.transpose(0, 1, 3, 2), m_ssm_b_im, v_ssm_b_im, (2 * G * SSM_P, SSM_H)),
             ("ssm_c_re", ssm_c_re, g_c_re, m_ssm_c_re, v_ssm_c_re, (2 * G * SSM_H, SSM_P)),
             ("ssm_c_im", ssm_c_im, g_c_im, m_ssm_c_im, v_ssm_c_im, (2 * G * SSM_H, SSM_P)),
             ("ssm_d", ssm_d, g_skip, m_ssm_d, v_ssm_d, (1, D_SSM)),
             ("b_glu", b_glu, r_bglu, m_b_glu, v_b_glu, (1, 2 * D_SSM)),
             ("norm_ple", norm_ple, r_nple, m_norm_ple, v_norm_ple, (1, D)),
             ("norm_final", norm_final, r_nf, m_norm_final, v_norm_final, (1, D))]
    for name, w, g, m, v, s2 in small:
        shp = w.shape
        outs = _adamw(w.reshape(s2), g.reshape(s2), m.reshape(s2), v.reshape(s2), "adamw_" + name)
        grads[name] = g.reshape(shp)
        deltas[name], new_ms[name], new_vs[name] = [t.reshape(shp) for t in outs]

    order = ["norm_mix", "w_in", "q_norm", "k_norm", "ssm_a_re", "ssm_a_im", "ssm_log_dt", "ssm_b_re", "ssm_b_im",
             "ssm_c_re", "ssm_c_im", "ssm_d", "w_glu", "b_glu", "w_out", "norm_ple", "w_ple_gate", "w_ple_proj",
             "norm_final"]
    return (loss, grad_x[None], *[grads[k] for k in order], *[deltas[k] for k in order],
            *[new_ms[k] for k in order], *[new_vs[k] for k in order])
```

```python
import functools
import math

import numpy as np
import jax
import jax.numpy as jnp
from jax import lax
from jax.experimental import pallas as pl
from jax.experimental.pallas import tpu as pltpu

F32 = jnp.float32
BF16 = jnp.bfloat16

N_DEV = 8
EPS = 1e-6
GRID_W = 64
ROPE_THETA = 10000.0
HEAD_DIM = 128
N_HEADS = 8
N_KV = 2
REP = N_HEADS // N_KV
D_ATTN = N_HEADS * HEAD_DIM
D_KV = N_KV * HEAD_DIM
SSM_H = 16
SSM_P = 64
SLAB = 128
SLAB_G = SLAB // SSM_H
SLAB_S = SLAB_G * SSM_P
SEG = 8
LANES = 128
PG_ROWS = 72
VMEM_LIMIT = 48 << 20

ADAM_LR = 0.001
ADAM_B1 = 0.9
ADAM_B2 = 0.999
ADAM_EPS = 1e-08
ADAM_WD = 0.01
ADAM_STEP = 10


def _pick(n, cands):
    for c in cands:
        if n % c == 0:
            return c
    return n


def _cparams(sem, vmem=VMEM_LIMIT):
    return pltpu.CompilerParams(dimension_semantics=sem, vmem_limit_bytes=vmem)


class _Carry:
    def __init__(self, kind, xs):
        self.kind, self.xs, self.n = kind, list(xs), len(xs)
        lead = (N_DEV,) if kind == "gather" else ()
        self.out_shape = [jax.ShapeDtypeStruct(lead + v.shape, v.dtype) for v in xs]
        self.specs = [pl.BlockSpec(memory_space=pl.ANY)] * self.n
        self.scratch = [pltpu.SemaphoreType.DMA((self.n, N_DEV - 1)), pltpu.SemaphoreType.DMA((self.n, N_DEV - 1)),
                        pltpu.SemaphoreType.DMA((self.n,))]

    def _copies(self, x_refs, out_refs, sems):
        send_sems, recv_sems, local_sems = sems
        x, y, c = lax.axis_index("x"), lax.axis_index("y"), lax.axis_index("c")
        me = _dev_index((x, y, c))
        mine, sends, arrivals = [], [], []
        for a in range(self.n):
            src_mine = x_refs[a] if self.kind == "gather" else x_refs[a].at[me]
            mine.append(pltpu.make_async_copy(src_mine, out_refs[a].at[me], local_sems.at[a]))
            for k in range(1, N_DEV):
                peer = _peer(k, x, y, c)
                src = x_refs[a] if self.kind == "gather" else x_refs[a].at[_dev_index(peer)]
                sends.append(pltpu.make_async_remote_copy(
                    src_ref=src, dst_ref=out_refs[a].at[me], send_sem=send_sems.at[a, k - 1],
                    recv_sem=recv_sems.at[a, k - 1], device_id=peer, device_id_type=pl.DeviceIdType.MESH))
                land = out_refs[a].at[_dev_index(peer)]
                arrivals.append(pltpu.make_async_remote_copy(
                    src_ref=land, dst_ref=land, send_sem=send_sems.at[a, k - 1],
                    recv_sem=recv_sems.at[a, k - 1], device_id=peer, device_id_type=pl.DeviceIdType.MESH))
        return mine, sends, arrivals

    def start(self, x_refs, out_refs, sems):
        mine, sends, _ = self._copies(x_refs, out_refs, sems)
        for cp in mine + sends:
            cp.start()

    def wait(self, x_refs, out_refs, sems):
        mine, sends, arrivals = self._copies(x_refs, out_refs, sems)
        for cp in arrivals:
            cp.wait_recv()
        for cp in sends:
            cp.wait_send()
        for cp in mine:
            cp.wait()


def _grid_edges(grid):
    first = functools.reduce(lambda p, q: p & q, [pl.program_id(d) == 0 for d in range(len(grid))])
    last = functools.reduce(lambda p, q: p & q, [pl.program_id(d) == g - 1 for d, g in enumerate(grid)])
    return first, last


def _mm(a, b, mode, name, out_dtype=F32, add=None, bias=None, a_blk=False, b_blk=False, out_blk=0, carry=None):
    w = b.shape[2] if b_blk else out_blk
    if mode == "nn":
        M, K = a.shape
        N = b.shape[0] * w if b_blk else b.shape[1]
    elif mode == "nt":
        M = a.shape[1] if a_blk else a.shape[0]
        N = b.shape[1] if b_blk else b.shape[0]
        K = b.shape[0] * w if b_blk else b.shape[1]
    else:
        K, M = a.shape
        N = b.shape[0] * w if b_blk else b.shape[1]
    tm = _pick(M, (1024, 512, 256))
    tn = _pick(N, (1024, 768, 512, 256))
    tk = K if (mode != "tn" and K <= 2048) else _pick(K, (1024, 512, 256))
    if mode == "nt" and b_blk:
        tk = w
    elif b_blk or out_blk:
        tn = w
    nk = K // tk
    grid = (M // tm, N // tn, nk)
    if mode == "nn":
        a_spec = pl.BlockSpec((tm, tk), lambda i, j, k: (i, k))
        b_spec = (pl.BlockSpec((1, tk, tn), lambda i, j, k: (j, k, 0)) if b_blk
                  else pl.BlockSpec((tk, tn), lambda i, j, k: (k, j)))
        dims = (((1,), (0,)), ((), ()))
    elif mode == "nt":
        a_spec = (pl.BlockSpec((1, tm, tk), lambda i, j, k: (k, i, 0)) if a_blk
                  else pl.BlockSpec((tm, tk), lambda i, j, k: (i, k)))
        b_spec = (pl.BlockSpec((1, tn, tk), lambda i, j, k: (k, j, 0)) if b_blk
                  else pl.BlockSpec((tn, tk), lambda i, j, k: (j, k)))
        dims = (((1,), (1,)), ((), ()))
    else:
        a_spec = pl.BlockSpec((tk, tm), lambda i, j, k: (k, i))
        b_spec = (pl.BlockSpec((1, tk, tn), lambda i, j, k: (j, k, 0)) if b_blk
                  else pl.BlockSpec((tk, tn), lambda i, j, k: (k, j)))
        dims = (((0,), (0,)), ((), ()))
    if out_blk:
        out_spec = pl.BlockSpec((1, tm, tn), lambda i, j, k: (j, i, 0))
        out_shape = jax.ShapeDtypeStruct((N // tn, M, tn), out_dtype)
    else:
        out_spec = pl.BlockSpec((tm, tn), lambda i, j, k: (i, j))
        out_shape = jax.ShapeDtypeStruct((M, N), out_dtype)
    extras, extra_specs = [], []
    if add is not None:
        extras.append(add)
        extra_specs.append(pl.BlockSpec((tm, tn), lambda i, j, k: (i, j)))
    if bias is not None:
        extras.append(bias)
        extra_specs.append(pl.BlockSpec((1, tn), lambda i, j, k: (0, j)))

    n_ex = len(extras)
    nc = carry.n if carry else 0

    def body(a_ref, b_ref, *rest):
        ex_refs, cx = rest[:n_ex], rest[n_ex:n_ex + nc]
        o_ref, cout = rest[n_ex + nc], rest[n_ex + nc + 1:n_ex + 2 * nc + 1]
        tail = rest[n_ex + 2 * nc + 1:]
        sems = tail[:3] if carry else ()
        first, last = _grid_edges(grid)
        if carry:
            @pl.when(first)
            def _():
                carry.start(cx, cout, sems)

        av = a_ref[0] if a_blk else a_ref[...]
        bv = b_ref[0] if b_blk else b_ref[...]
        prod = lax.dot_general(av, bv, dims, preferred_element_type=F32)

        def finish(out):
            for r in ex_refs:
                out = out + r[...]
            if out_blk:
                o_ref[0] = out.astype(out_dtype)
            else:
                o_ref[...] = out.astype(out_dtype)

        if nk == 1:
            finish(prod)
        else:
            acc_ref = tail[-1]
            k = pl.program_id(2)

            @pl.when(k == 0)
            def _():
                acc_ref[...] = prod

            @pl.when(k > 0)
            def _():
                acc_ref[...] += prod

            @pl.when(k == nk - 1)
            def _():
                finish(acc_ref[...])

        if carry:
            @pl.when(last)
            def _():
                carry.wait(cx, cout, sems)

    scratch = (carry.scratch if carry else []) + ([pltpu.VMEM((tm, tn), F32)] if nk > 1 else [])
    outs = pl.pallas_call(
        body, name=name, grid=grid,
        in_specs=[a_spec, b_spec] + extra_specs + (carry.specs if carry else []),
        out_specs=[out_spec] + (carry.specs if carry else []),
        out_shape=[out_shape] + (carry.out_shape if carry else []),
        scratch_shapes=scratch,
        compiler_params=_cparams(("arbitrary", "arbitrary", "arbitrary")),
    )(a, b, *extras, *(carry.xs if carry else []))
    return (outs[0], outs[1:]) if carry else outs[0]


def _rspec(tm, w, cb=0):
    return pl.BlockSpec((tm, w), lambda i: (i, cb))


def _fspec(shape):
    nd = len(shape)
    return pl.BlockSpec(shape, lambda i: (0,) * nd)


def _rowcall(body, name, L, tm, ins, row_outs, acc_outs=()):
    out_shape = [jax.ShapeDtypeStruct((L, w), dt) for w, dt in row_outs]
    out_shape += [jax.ShapeDtypeStruct(s, F32) for s in acc_outs]
    out_specs = [_rspec(tm, w) for w, _ in row_outs] + [_fspec(s) for s in acc_outs]
    return pl.pallas_call(
        body, name=name, grid=(L // tm,),
        in_specs=[s for _, s in ins], out_specs=out_specs, out_shape=out_shape,
        compiler_params=_cparams(("arbitrary",)),
    )(*[a for a, _ in ins])


def _acc(ref, val):
    @pl.when(pl.program_id(0) == 0)
    def _():
        ref[...] = jnp.zeros_like(ref)
    ref[...] += val


def _colsum(v):
    return jnp.sum(v, axis=0, keepdims=True)


def _rms(xv):
    return lax.rsqrt(jnp.mean(xv * xv, axis=-1, keepdims=True) + EPS)


def _rms_bwd(dn, xhat, r, g):
    dng = dn * g
    return r * (dng - xhat * jnp.mean(dng * xhat, axis=-1, keepdims=True))


def _sigmoid(v):
    return jax.nn.sigmoid(v)


def _partner(v):
    w = v.shape[-1]
    lane = lax.broadcasted_iota(jnp.int32, v.shape, v.ndim - 1)
    first_half = (lane % 64) < 32
    return jnp.where(first_half, pltpu.roll(v, w - 32, axis=v.ndim - 1), pltpu.roll(v, 32, axis=v.ndim - 1))


def _norm_in(x, g, name):
    L, D = x.shape
    tm = _pick(L, (512, 256))

    def body(x_ref, g_ref, o_ref):
        xv = x_ref[...]
        o_ref[...] = (xv * _rms(xv) * g_ref[...]).astype(BF16)

    return _rowcall(body, name, L, tm, [(x, _rspec(tm, D)), (g, _fspec(g.shape))], [(D, BF16)])[0]


def _rope_tables(L):
    t = np.arange(L)
    rows = (t // GRID_W).astype(np.float32)
    cols = (t % GRID_W).astype(np.float32)
    n_freq = HEAD_DIM // 4
    inv_freq = np.float32(ROPE_THETA) ** (-np.arange(n_freq, dtype=np.float32) / np.float32(n_freq))
    ar = (rows[:, None] * inv_freq[None, :]).astype(np.float32).astype(np.float64)
    ac = (cols[:, None] * inv_freq[None, :]).astype(np.float32).astype(np.float64)
    cos = np.concatenate([np.cos(ar), np.cos(ar), np.cos(ac), np.cos(ac)], axis=-1).astype(np.float32)
    sin = np.concatenate([-np.sin(ar), np.sin(ar), -np.sin(ac), np.sin(ac)], axis=-1).astype(np.float32)
    return jnp.asarray(cos), jnp.asarray(sin)


def _qkv_prep(z, cos, sin, qn, kn):
    L = z.shape[0]
    tm = _pick(L, (512, 256))
    scale = HEAD_DIM ** -0.5
    kblk = 4 * D_ATTN // D_KV

    def body(q_ref, k_ref, v_ref, cos_ref, sin_ref, qn_ref, kn_ref, qo_ref, ko_ref, vo_ref, kt_ref):
        c, s = cos_ref[...], sin_ref[...]

        def head(xh, w):
            n = xh * _rms(xh) * w
            return n * c + _partner(n) * s

        for h in range(N_HEADS):
            sl = slice(h * HEAD_DIM, (h + 1) * HEAD_DIM)
            qo_ref[:, sl] = (head(q_ref[:, sl], qn_ref[...]) * scale).astype(BF16)
        for h in range(N_KV):
            sl = slice(h * HEAD_DIM, (h + 1) * HEAD_DIM)
            kr = head(k_ref[:, sl], kn_ref[...])
            ko_ref[:, sl] = kr.astype(BF16)
            kt_ref[sl, :] = kr.T.astype(BF16)
        vo_ref[...] = v_ref[...].astype(BF16)

    return pl.pallas_call(
        body, name="qkv_prep", grid=(L // tm,),
        in_specs=[_rspec(tm, D_ATTN, 0), _rspec(tm, D_KV, kblk), _rspec(tm, D_KV, kblk + 1),
                  _rspec(tm, HEAD_DIM), _rspec(tm, HEAD_DIM), _fspec(qn.shape), _fspec(kn.shape)],
        out_specs=[_rspec(tm, D_ATTN), _rspec(tm, D_KV), _rspec(tm, D_KV),
                   pl.BlockSpec((D_KV, tm), lambda i: (0, i))],
        out_shape=[jax.ShapeDtypeStruct((L, D_ATTN), BF16), jax.ShapeDtypeStruct((L, D_KV), BF16),
                   jax.ShapeDtypeStruct((L, D_KV), BF16), jax.ShapeDtypeStruct((D_KV, L), BF16)],
        compiler_params=_cparams(("arbitrary",)),
    )(z, z, z, cos, sin, qn, kn)


def _attn_fwd(q, k, v, carry):
    L = q.shape[0]
    tq = _pick(L, (256, 128))
    grid = (N_HEADS, L // tq)
    nc = carry.n

    def body(q_ref, k_ref, v_ref, *rest):
        cx, o_ref, cout, sems = rest[:nc], rest[nc], rest[nc + 1:2 * nc + 1], rest[2 * nc + 1:]
        first, last = _grid_edges(grid)

        @pl.when(first)
        def _():
            carry.start(cx, cout, sems)

        s = lax.dot_general(q_ref[...], k_ref[...], (((1,), (1,)), ((), ())), preferred_element_type=F32)
        e = jnp.exp(s - jnp.max(s, axis=-1, keepdims=True))
        l = jnp.sum(e, axis=-1, keepdims=True)
        o_ref[...] = jnp.dot(e.astype(BF16), v_ref[...], preferred_element_type=F32) / l

        @pl.when(last)
        def _():
            carry.wait(cx, cout, sems)

    outs = pl.pallas_call(
        body, name="attn_fwd", grid=grid,
        in_specs=[pl.BlockSpec((tq, HEAD_DIM), lambda h, i: (i, h)),
                  pl.BlockSpec((L, HEAD_DIM), lambda h, i: (0, h // REP)),
                  pl.BlockSpec((L, HEAD_DIM), lambda h, i: (0, h // REP))] + carry.specs,
        out_specs=[pl.BlockSpec((tq, HEAD_DIM), lambda h, i: (i, h))] + carry.specs,
        out_shape=[jax.ShapeDtypeStruct((L, D_ATTN), F32)] + carry.out_shape,
        scratch_shapes=carry.scratch,
        compiler_params=_cparams(("arbitrary", "arbitrary")),
    )(q, k, v, *carry.xs)
    return outs[0], outs[1:]


def _attn_bwd(q, k, v, kt, do, carry):
    L = q.shape[0]
    tq = _pick(L, (256, 128))
    nt = (((1,), (1,)), ((), ()))
    grid = (N_KV, REP, L // tq)
    nc = carry.n

    def body(q_ref, do_ref, k_ref, v_ref, kt_ref, *rest):
        cx, (dq_ref, dk_ref, dv_ref) = rest[:nc], rest[nc:nc + 3]
        cout, sems = rest[nc + 3:2 * nc + 3], rest[2 * nc + 3:]
        first, last = _grid_edges(grid)

        @pl.when(first)
        def _():
            carry.start(cx, cout, sems)

        @pl.when((pl.program_id(1) == 0) & (pl.program_id(2) == 0))
        def _():
            dk_ref[...] = jnp.zeros_like(dk_ref)
            dv_ref[...] = jnp.zeros_like(dv_ref)

        qv, dov = q_ref[...], do_ref[...]
        st = lax.dot_general(k_ref[...], qv, nt, preferred_element_type=F32)
        e = jnp.exp(st - jnp.max(st, axis=0, keepdims=True))
        pt = e * (1.0 / jnp.sum(e, axis=0, keepdims=True))
        dpt = lax.dot_general(v_ref[...], dov, nt, preferred_element_type=F32)
        delta = jnp.sum(pt * dpt, axis=0, keepdims=True)
        dst = (pt * (dpt - delta)).astype(BF16)
        dv_ref[...] += jnp.dot(pt.astype(BF16), dov, preferred_element_type=F32)
        dk_ref[...] += jnp.dot(dst, qv, preferred_element_type=F32)
        dq_ref[...] = jnp.dot(kt_ref[...], dst, preferred_element_type=F32).T

        @pl.when(last)
        def _():
            carry.wait(cx, cout, sems)

    outs = pl.pallas_call(
        body, name="attn_bwd", grid=grid,
        in_specs=[pl.BlockSpec((tq, HEAD_DIM), lambda g, r, i: (i, g * REP + r)),
                  pl.BlockSpec((tq, HEAD_DIM), lambda g, r, i: (i, g * REP + r)),
                  pl.BlockSpec((L, HEAD_DIM), lambda g, r, i: (0, g)),
                  pl.BlockSpec((L, HEAD_DIM), lambda g, r, i: (0, g)),
                  pl.BlockSpec((HEAD_DIM, L), lambda g, r, i: (g, 0))] + carry.specs,
        out_specs=[pl.BlockSpec((tq, HEAD_DIM), lambda g, r, i: (i, g * REP + r)),
                   pl.BlockSpec((L, HEAD_DIM), lambda g, r, i: (0, g)),
                   pl.BlockSpec((L, HEAD_DIM), lambda g, r, i: (0, g))] + carry.specs,
        out_shape=[jax.ShapeDtypeStruct((L, D_ATTN), F32), jax.ShapeDtypeStruct((L, D_KV), F32),
                   jax.ShapeDtypeStruct((L, D_KV), F32)] + carry.out_shape,
        scratch_shapes=carry.scratch,
        compiler_params=_cparams(("arbitrary", "arbitrary", "arbitrary")),
    )(q, do, k, v, kt, *carry.xs)
    return outs[0], outs[1], outs[2], outs[3:]


def _seg_perm(a):
    L, C = a.shape
    return a.reshape(SEG, L // SEG, C).transpose(1, 0, 2).reshape(L, C)


def _seg_unperm(a):
    L, C = a.shape
    return a.reshape(L // SEG, SEG, C).transpose(1, 0, 2).reshape(L, C)


def _cmul(ar, ai, br, bi):
    return ar * br - ai * bi, ar * bi + ai * br


def _seg_scan(xr_ref, xi_ref, ar, ai, reverse, n_rows):
    shape = ar.shape
    zero = jnp.zeros(shape, F32)

    def row(r):
        rr = (n_rows - 1 - r) if reverse else r
        return pl.ds(pl.multiple_of(rr * SEG, SEG), SEG)

    def local(r, carry):
        cr, ci = carry
        sl = row(r)
        pr, pi = _cmul(ar, ai, cr, ci)
        nr, ni = pr + xr_ref[sl, :], pi + xi_ref[sl, :]
        xr_ref[sl, :] = nr
        xi_ref[sl, :] = ni
        return nr, ni

    er, ei = lax.fori_loop(0, n_rows, local, (zero, zero))

    pr, pi = ar, ai
    for _ in range(int(math.log2(n_rows))):
        pr, pi = _cmul(pr, pi, pr, pi)
    sub = lax.broadcasted_iota(jnp.int32, shape, 0)
    shift = (SEG - 1) if reverse else 1
    edge = (SEG - 1) if reverse else 0
    inr, ini = zero, zero
    for _ in range(SEG - 1):
        tr, ti = _cmul(pr, pi, inr, ini)
        inr = jnp.where(sub == edge, 0.0, pltpu.roll(tr + er, shift, axis=0))
        ini = jnp.where(sub == edge, 0.0, pltpu.roll(ti + ei, shift, axis=0))

    def fix(r, carry):
        wr, wi = carry
        sl = row(r)
        fr, fi = _cmul(wr, wi, inr, ini)
        xr_ref[sl, :] += fr
        xi_ref[sl, :] += fi
        return _cmul(wr, wi, ar, ai)

    lax.fori_loop(0, n_rows, fix, (ar, ai))
    return inr, ini


def _discretise(a_re, a_im, ldt):
    lr = jnp.minimum(a_re, -1e-4)
    li = a_im
    dt = jnp.exp(ldt)
    mag = jnp.exp(lr * dt)
    lbr = mag * jnp.cos(li * dt)
    lbi = mag * jnp.sin(li * dt)
    den = lr * lr + li * li
    nr = lbr - 1.0
    fr = (nr * lr + lbi * li) / den
    fi = (lbi * lr - nr * li) / den
    return lr, li, dt, lbr, lbi, fr, fi


def _lane_row(v):
    return jnp.concatenate([v[g:g + 1, :] for g in range(v.shape[0])], axis=1)


def _ssm_fill_maps(d, prm, tmp_ref, maps):
    a_re_ref, a_im_ref, ldt_ref, bt_re_ref, bt_im_ref, c_re_ref, c_im_ref = prm
    _, _, _, lbr, lbi, fr, fi = _discretise(a_re_ref[d], a_im_ref[d], ldt_ref[d])

    def fill(dst, piece):
        tmp_ref[...] = jnp.zeros_like(tmp_ref)
        for g in range(SLAB_G):
            tmp_ref[g * SSM_H:(g + 1) * SSM_H, g * SSM_P:(g + 1) * SSM_P] = piece(g)
        dst[...] = tmp_ref[...].astype(BF16)

    wbr, wbi, wcr, wci = maps
    fill(wbr, lambda g: fr[g:g + 1] * bt_re_ref[d, g] - fi[g:g + 1] * bt_im_ref[d, g])
    fill(wbi, lambda g: fr[g:g + 1] * bt_im_ref[d, g] + fi[g:g + 1] * bt_re_ref[d, g])
    fill(wcr, lambda g: c_re_ref[d, g])
    fill(wci, lambda g: c_im_ref[d, g])
    return _lane_row(lbr), _lane_row(lbi)


def _ssm_param_specs():
    pole = pl.BlockSpec((2, SLAB_G, SSM_P), lambda j: (0, j, 0))
    step = pl.BlockSpec((2, SLAB_G, 1), lambda j: (0, j, 0))
    mat = pl.BlockSpec((2, SLAB_G, SSM_H, SSM_P), lambda j: (0, j, 0, 0))
    return [pole, pole, step, mat, mat, mat, mat]


_MAP_SCRATCH = [pltpu.VMEM((SLAB, SLAB_S), F32)] + [pltpu.VMEM((SLAB, SLAB_S), BF16)] * 4
_NT = (((1,), (1,)), ((), ()))


def _ssm_fwd(u, prm, dskip):
    L, C = u.shape
    n_rows = L // SEG
    tc = _pick(L, (512, 256))
    u_spec = pl.BlockSpec((L, SLAB), lambda j: (0, j))
    d_spec = pl.BlockSpec((1, SLAB), lambda j: (0, j))

    def body(u_ref, *rest):
        prm_refs, d_ref, y_ref = rest[:7], rest[7], rest[8]
        tmp_ref, maps, xr_ref, xi_ref = rest[9], rest[10:14], rest[14], rest[15]
        wbr, wbi, wcr, wci = maps
        y_ref[...] = u_ref[...] * d_ref[...]
        for d in range(2):
            lam_r, lam_i = _ssm_fill_maps(d, prm_refs, tmp_ref, maps)

            def inp(c, _):
                sl = pl.ds(pl.multiple_of(c * tc, tc), tc)
                ub = u_ref[sl, :].astype(BF16)
                xr_ref[sl, :] = jnp.dot(ub, wbr[...], preferred_element_type=F32)
                xi_ref[sl, :] = jnp.dot(ub, wbi[...], preferred_element_type=F32)
                return 0

            lax.fori_loop(0, L // tc, inp, 0)
            ar = jnp.broadcast_to(lam_r, (SEG, SLAB_S))
            ai = jnp.broadcast_to(lam_i, (SEG, SLAB_S))
            _seg_scan(xr_ref, xi_ref, ar, ai, d == 1, n_rows)

            def outp(c, _):
                sl = pl.ds(pl.multiple_of(c * tc, tc), tc)
                y_ref[sl, :] += (
                    lax.dot_general(xr_ref[sl, :].astype(BF16), wcr[...], _NT, preferred_element_type=F32)
                    - lax.dot_general(xi_ref[sl, :].astype(BF16), wci[...], _NT, preferred_element_type=F32))
                return 0

            lax.fori_loop(0, L // tc, outp, 0)

    return pl.pallas_call(
        body, name="ssm_fwd", grid=(C // SLAB,),
        in_specs=[u_spec] + _ssm_param_specs() + [d_spec],
        out_specs=u_spec, out_shape=jax.ShapeDtypeStruct((L, C), F32),
        scratch_shapes=_MAP_SCRATCH + [pltpu.VMEM((L, SLAB_S), F32)] * 2,
        compiler_params=_cparams(("arbitrary",)),
    )(u, *prm, dskip)


def _ssm_bwd(u, dy, prm, dskip):
    L, C = u.shape
    n_rows = L // SEG
    n_slab = C // SLAB
    tc = _pick(L, (512, 256))
    u_spec = pl.BlockSpec((L, SLAB), lambda j: (0, j))
    d_spec = pl.BlockSpec((1, SLAB), lambda j: (0, j))
    pg_spec = pl.BlockSpec((1, 2, PG_ROWS, SLAB_S), lambda j: (j, 0, 0, 0))

    def body(u_ref, dy_ref, *rest):
        prm_refs, d_ref, du_ref, pg_ref = rest[:7], rest[7], rest[8], rest[9]
        tmp_ref, maps, acc_ref = rest[10], rest[11:15], rest[15]
        xr_ref, xi_ref, gr_ref, gi_ref = rest[16:20]
        wbr, wbi, wcr, wci = maps
        du_ref[...] = dy_ref[...] * d_ref[...]
        pg_ref[...] = jnp.zeros_like(pg_ref)
        pg_ref[0, 0, 66:67, 0:SLAB] = _colsum(dy_ref[...] * u_ref[...])
        for d in range(2):
            lam_r, lam_i = _ssm_fill_maps(d, prm_refs, tmp_ref, maps)

            def inp(c, _):
                sl = pl.ds(pl.multiple_of(c * tc, tc), tc)
                ub = u_ref[sl, :].astype(BF16)
                dyb = dy_ref[sl, :].astype(BF16)
                xr_ref[sl, :] = jnp.dot(ub, wbr[...], preferred_element_type=F32)
                xi_ref[sl, :] = jnp.dot(ub, wbi[...], preferred_element_type=F32)
                gr_ref[sl, :] = jnp.dot(dyb, wcr[...], preferred_element_type=F32)
                gi_ref[sl, :] = -jnp.dot(dyb, wci[...], preferred_element_type=F32)
                return 0

            lax.fori_loop(0, L // tc, inp, 0)
            ar = jnp.broadcast_to(lam_r, (SEG, SLAB_S))
            ai = jnp.broadcast_to(lam_i, (SEG, SLAB_S))
            inr, ini = _seg_scan(xr_ref, xi_ref, ar, ai, d == 1, n_rows)
            _seg_scan(gr_ref, gi_ref, ar, -ai, d == 0, n_rows)

            def pole(r, carry):
                accr, acci, pr, pi = carry
                rr = (n_rows - 1 - r) if d == 1 else r
                sl = pl.ds(pl.multiple_of(rr * SEG, SEG), SEG)
                lr, li = gr_ref[sl, :], gi_ref[sl, :]
                accr = accr + lr * pr + li * pi
                acci = acci + li * pr - lr * pi
                return accr, acci, xr_ref[sl, :], xi_ref[sl, :]

            zero = jnp.zeros((SEG, SLAB_S), F32)
            accr, acci, _, _ = lax.fori_loop(0, n_rows, pole, (zero, zero, inr, ini))
            pg_ref[0, d, 64:65, :] = _colsum(accr)
            pg_ref[0, d, 65:66, :] = _colsum(acci)

            acc_ref[...] = jnp.zeros_like(acc_ref)

            def outp(c, _):
                sl = pl.ds(pl.multiple_of(c * tc, tc), tc)
                lrb, lib = gr_ref[sl, :].astype(BF16), gi_ref[sl, :].astype(BF16)
                du_ref[sl, :] += (lax.dot_general(lrb, wbr[...], _NT, preferred_element_type=F32)
                                  + lax.dot_general(lib, wbi[...], _NT, preferred_element_type=F32))
                ut = u_ref[sl, :].T.astype(BF16)
                dyt = dy_ref[sl, :].T.astype(BF16)
                acc_ref[0] += jnp.dot(ut, lrb, preferred_element_type=F32)
                acc_ref[1] += jnp.dot(ut, lib, preferred_element_type=F32)
                acc_ref[2] += jnp.dot(dyt, xr_ref[sl, :].astype(BF16), preferred_element_type=F32)
                acc_ref[3] -= jnp.dot(dyt, xi_ref[sl, :].astype(BF16), preferred_element_type=F32)
                return 0

            lax.fori_loop(0, L // tc, outp, 0)
            for m in range(4):
                for g in range(SLAB_G):
                    lanes = slice(g * SSM_P, (g + 1) * SSM_P)
                    pg_ref[0, d, m * SSM_H:(m + 1) * SSM_H, lanes] = acc_ref[m, g * SSM_H:(g + 1) * SSM_H, lanes]

    return pl.pallas_call(
        body, name="ssm_bwd", grid=(n_slab,),
        in_specs=[u_spec, u_spec] + _ssm_param_specs() + [d_spec],
        out_specs=[u_spec, pg_spec],
        out_shape=[jax.ShapeDtypeStruct((L, C), F32), jax.ShapeDtypeStruct((n_slab, 2, PG_ROWS, SLAB_S), F32)],
        scratch_shapes=_MAP_SCRATCH + [pltpu.VMEM((4, SLAB, SLAB_S), F32)] + [pltpu.VMEM((L, SLAB_S), F32)] * 4,
        compiler_params=_cparams(("arbitrary",), 60 << 20),
    )(u, dy, *prm, dskip)


def _ssm_param_grads(pg, prm):
    n_slab = pg.shape[0]
    G = n_slab * SLAB_G
    pg_spec = pl.BlockSpec((1, 2, PG_ROWS, SLAB_S), lambda j: (j, 0, 0, 0))
    pole, _, step, mat = _ssm_param_specs()[:4]

    def body(pg_ref, a_re_ref, a_im_ref, ldt_ref, bt_re_ref, bt_im_ref,
             dbr_ref, dbi_ref, dcr_ref, dci_ref, dar_ref, dai_ref, dldt_ref, dd_ref):
        dd_ref[...] = pg_ref[0, 0, 66:67, 0:SLAB]
        for d in range(2):
            a_r = a_re_ref[d]
            lr, li, dt, lbr, lbi, f_r, f_i = _discretise(a_r, a_im_ref[d], ldt_ref[d])
            gfr_rows, gfi_rows, glr_rows, gli_rows = [], [], [], []
            for g in range(SLAB_G):
                lanes = slice(g * SSM_P, (g + 1) * SSM_P)
                gbr, gbi = pg_ref[0, d, 0:SSM_H, lanes], pg_ref[0, d, SSM_H:2 * SSM_H, lanes]
                b_r, b_i = bt_re_ref[d, g], bt_im_ref[d, g]
                fr, fi = f_r[g:g + 1], f_i[g:g + 1]
                dbr_ref[d, g] = fr * gbr + fi * gbi
                dbi_ref[d, g] = fr * gbi - fi * gbr
                gfr_rows.append(_colsum(gbr * b_r + gbi * b_i))
                gfi_rows.append(_colsum(gbi * b_r - gbr * b_i))
                dcr_ref[d, g] = pg_ref[0, d, 2 * SSM_H:3 * SSM_H, lanes]
                dci_ref[d, g] = pg_ref[0, d, 3 * SSM_H:4 * SSM_H, lanes]
                glr_rows.append(pg_ref[0, d, 64:65, lanes])
                gli_rows.append(pg_ref[0, d, 65:66, lanes])
            gfr, gfi = jnp.concatenate(gfr_rows, axis=0), jnp.concatenate(gfi_rows, axis=0)
            glr, gli = jnp.concatenate(glr_rows, axis=0), jnp.concatenate(gli_rows, axis=0)
            den = lr * lr + li * li
            ir, ii = lr / den, -li / den
            tr, ti = _cmul(ir, -ii, gfr, gfi)
            glbr, glbi = glr + tr, gli + ti
            qr, qi = _cmul(f_r, f_i, ir, ii)
            dlr, dli = _cmul(-qr, qi, gfr, gfi)
            zr, zi = _cmul(lbr, -lbi, glbr, glbi)
            dlr = dlr + dt * zr
            dli = dli + dt * zi
            dar_ref[d] = jnp.where(a_r < -1e-4, dlr, jnp.where(a_r == -1e-4, 0.5 * dlr, 0.0))
            dai_ref[d] = dli
            dldt_ref[d] = jnp.sum(lr * zr + li * zi, axis=-1, keepdims=True) * dt

    a_re, a_im, ldt, bt_re, bt_im = prm[:5]
    mshape = jax.ShapeDtypeStruct(bt_re.shape, F32)
    pshape = jax.ShapeDtypeStruct(a_re.shape, F32)
    return pl.pallas_call(
        body, name="ssm_param_grads", grid=(n_slab,),
        in_specs=[pg_spec, pole, pole, step, mat, mat],
        out_specs=[mat, mat, mat, mat, pole, pole, step, pl.BlockSpec((1, SLAB), lambda j: (0, j))],
        out_shape=[mshape, mshape, mshape, mshape, pshape, pshape, jax.ShapeDtypeStruct(ldt.shape, F32),
                   jax.ShapeDtypeStruct((1, n_slab * SLAB), F32)],
        compiler_params=_cparams(("arbitrary",)),
    )(pg, a_re, a_im, ldt, bt_re, bt_im)


def _peer(k, x, y, c):
    return (1 - x if k & 4 else x, 1 - y if k & 2 else y, 1 - c if k & 1 else c)


def _dev_index(pos):
    return 4 * pos[0] + 2 * pos[1] + pos[2]


def _all_gather(xs, name):
    n = len(xs)
    any_spec = pl.BlockSpec(memory_space=pl.ANY)

    def body(*refs):
        x_refs, out_refs = refs[:n], refs[n:2 * n]
        send_sems, recv_sems, local_sems = refs[2 * n:]
        x, y, c = lax.axis_index("x"), lax.axis_index("y"), lax.axis_index("c")
        me, sibling = (x, y, c), (x, y, 1 - c)
        chips = [(1 - x, y), (x, 1 - y), (1 - x, 1 - y)]

        def copy(a, k, block, to, src=None):
            dst = out_refs[a].at[_dev_index(block)]
            return pltpu.make_async_remote_copy(
                src_ref=dst if src is None else src, dst_ref=dst,
                send_sem=send_sems.at[a, k], recv_sem=recv_sems.at[a, k],
                device_id=to, device_id_type=pl.DeviceIdType.MESH)

        mine = [pltpu.make_async_copy(x_refs[a], out_refs[a].at[_dev_index(me)], local_sems.at[a]) for a in range(n)]
        for cp in mine:
            cp.start()
        first = []
        for a in range(n):
            first.append(copy(a, 0, me, sibling, src=x_refs[a]))
            first += [copy(a, 1 + j, me, (*chip, c), src=x_refs[a]) for j, chip in enumerate(chips)]
        for cp in first:
            cp.start()
        passed = []
        for j, chip in enumerate(chips):
            for a in range(n):
                copy(a, 1 + j, (*chip, c), me).wait_recv()
                fwd = copy(a, 4 + j, (*chip, c), sibling)
                fwd.start()
                passed.append(fwd)
        for a in range(n):
            copy(a, 0, sibling, me).wait_recv()
            for j, chip in enumerate(chips):
                copy(a, 4 + j, (*chip, 1 - c), me).wait_recv()
        for cp in first + passed:
            cp.wait_send()
        for cp in mine:
            cp.wait()

    return pl.pallas_call(
        body, name=name,
        out_shape=[jax.ShapeDtypeStruct((N_DEV,) + v.shape, v.dtype) for v in xs],
        in_specs=[any_spec] * n, out_specs=[any_spec] * n,
        scratch_shapes=[pltpu.SemaphoreType.DMA((n, 7)), pltpu.SemaphoreType.DMA((n, 7)),
                        pltpu.SemaphoreType.DMA((n,))],
    )(*xs)


def _sum_blocks(x, name):
    _, R, W = x.shape

    def body(x_ref, o_ref):
        acc = x_ref[0].astype(F32)
        for d in range(1, N_DEV):
            acc = acc + x_ref[d].astype(F32)
        o_ref[...] = acc

    return pl.pallas_call(body, name=name, out_shape=jax.ShapeDtypeStruct((R, W), F32),
                          compiler_params=pltpu.CompilerParams(vmem_limit_bytes=VMEM_LIMIT))(x)


def _adam_update(w, g, m, v):
    mn = ADAM_B1 * m + (1.0 - ADAM_B1) * g
    vn = ADAM_B2 * v + (1.0 - ADAM_B2) * (g * g)
    m_hat = mn / (1.0 - ADAM_B1 ** ADAM_STEP)
    v_hat = vn / (1.0 - ADAM_B2 ** ADAM_STEP)
    return -ADAM_LR * (m_hat / (jnp.sqrt(v_hat) + ADAM_EPS) + ADAM_WD * w), mn, vn


def _row_tile(R, W, budget):
    padded = -(-W // LANES) * LANES * 4
    if R * padded <= budget:
        return R
    return _pick(R, [t for t in (2048, 1024, 512, 256, 128, 64, 32, 16, 8) if t * padded <= budget])


def _adamw(w, g, m, v, name):
    R, W = w.shape
    tr = _row_tile(R, W, 1 << 20)

    def body(w_ref, g_ref, m_ref, v_ref, d_ref, mo_ref, vo_ref):
        d_ref[...], mo_ref[...], vo_ref[...] = _adam_update(w_ref[...], g_ref[...], m_ref[...], v_ref[...])

    spec = pl.BlockSpec((tr, W), lambda i: (i, 0))
    shp = jax.ShapeDtypeStruct((R, W), F32)
    return pl.pallas_call(
        body, name=name, grid=(R // tr,), in_specs=[spec] * 4, out_specs=[spec] * 3, out_shape=[shp] * 3,
        compiler_params=_cparams(("arbitrary",)),
    )(w, g, m, v)


def _adamw_reduce(w, land, m, v, name):
    R, W = w.shape
    tr = _row_tile(R, W, 1 << 20)

    def body(w_ref, l_ref, m_ref, v_ref, g_ref, d_ref, mo_ref, vo_ref):
        g = l_ref[0].astype(F32)
        for d in range(1, N_DEV):
            g = g + l_ref[d].astype(F32)
        g_ref[...] = g
        d_ref[...], mo_ref[...], vo_ref[...] = _adam_update(w_ref[...], g, m_ref[...], v_ref[...])

    spec = pl.BlockSpec((tr, W), lambda i: (i, 0))
    lspec = pl.BlockSpec((N_DEV, tr, W), lambda i: (0, i, 0))
    shp = jax.ShapeDtypeStruct((R, W), F32)
    return pl.pallas_call(
        body, name=name, grid=(R // tr,), in_specs=[spec, lspec, spec, spec], out_specs=[spec] * 4,
        out_shape=[shp] * 4, compiler_params=_cparams(("arbitrary",)),
    )(w, land, m, v)


def _to_zp(w):
    a, kv = D_ATTN, D_KV
    return jnp.concatenate([w[:, :a], w[:, a + 2 * kv:], w[:, a:a + 2 * kv]], axis=1)


def _gelu(v):
    c = math.sqrt(2.0 / math.pi)
    return 0.5 * v * (1.0 + jnp.tanh(c * (v + 0.044715 * v * v * v)))


def _gelu_grad(v):
    c = math.sqrt(2.0 / math.pi)
    t = jnp.tanh(c * (v + 0.044715 * v * v * v))
    return 0.5 * (1.0 + t) + 0.5 * v * (1.0 - t * t) * c * (1.0 + 3.0 * 0.044715 * v * v)


def kernel(x, p, norm_mix, w_in, q_norm, k_norm, ssm_a_re, ssm_a_im, ssm_log_dt, ssm_b_re, ssm_b_im, ssm_c_re, ssm_c_im, ssm_d, w_glu, b_glu, w_out, norm_ple, w_ple_gate, w_ple_proj, norm_final, loss_target, m_norm_mix, m_w_in, m_q_norm, m_k_norm, m_ssm_a_re, m_ssm_a_im, m_ssm_log_dt, m_ssm_b_re, m_ssm_b_im, m_ssm_c_re, m_ssm_c_im, m_ssm_d, m_w_glu, m_b_glu, m_w_out, m_norm_ple, m_w_ple_gate, m_w_ple_proj, m_norm_final, v_norm_mix, v_w_in, v_q_norm, v_k_norm, v_ssm_a_re, v_ssm_a_im, v_ssm_log_dt, v_ssm_b_re, v_ssm_b_im, v_ssm_c_re, v_ssm_c_im, v_ssm_d, v_w_glu, v_b_glu, v_w_out, v_norm_ple, v_w_ple_gate, v_w_ple_proj, v_norm_final):
    L, D = x.shape[1], x.shape[2]
    D_SSM = ssm_d.shape[1]
    G = D_SSM // SSM_H
    n_slab = D_SSM // SLAB
    n_in = w_in.shape[2]
    D_IN = n_in * N_DEV
    n_pp = w_ple_proj.shape[2]
    n_glu = w_glu.shape[2]
    xs = x[0]
    ps = p[0, 0]
    tgt = loss_target[0]

    (win3,) = _all_gather([w_in[0].astype(BF16)], "gather_w_in")
    win_p = _to_zp(win3.transpose(1, 0, 2).reshape(D, D_IN))
    later_weights = _Carry("gather", [w_glu[0].astype(BF16), w_out[0].astype(BF16), w_ple_gate[0].astype(BF16),
                                      w_ple_proj[0].astype(BF16)])

    ssm_prm = (ssm_a_re[0], ssm_a_im[0], ssm_log_dt[0].reshape(2, G, 1),
               ssm_b_re[0].transpose(0, 1, 3, 2), ssm_b_im[0].transpose(0, 1, 3, 2), ssm_c_re[0], ssm_c_im[0])

    cos, sin = _rope_tables(L)
    hn = _norm_in(xs, norm_mix, "norm_mix")
    z = _mm(hn, win_p, "nn", "in_proj")
    qr, kr, vb, kt = _qkv_prep(z, cos, sin, q_norm, k_norm)
    o, (wglu3, wout3, wpg3, wpp3) = _attn_fwd(qr, kr, vb, later_weights)
    wout = wout3.reshape(-1, D)
    wpg = wpg3.reshape(-1, D)
    u_off = 2 * D_ATTN
    u_perm = _seg_perm(z[:, u_off:u_off + D_SSM])
    ys = _seg_unperm(_ssm_fwd(u_perm, ssm_prm, ssm_d))

    tm = _pick(L, (256,))

    def gelu_body(y_ref, o_ref):
        o_ref[...] = _gelu(y_ref[...]).astype(BF16)

    (gy,) = _rowcall(gelu_body, "gelu", L, tm, [(ys, _rspec(tm, D_SSM))], [(D_SSM, BF16)])
    glu = _mm(gy, wglu3, "nn", "glu_proj", bias=b_glu, b_blk=True)

    def mix_body(o_ref, ga_ref, gla_ref, glb_ref, gs_ref, cat_ref):
        ga, gs = ga_ref[...], gs_ref[...]
        cat_ref[:, :D_ATTN] = (o_ref[...] * ga * _sigmoid(ga)).astype(BF16)
        cat_ref[:, D_ATTN:] = (gla_ref[...] * _sigmoid(glb_ref[...]) * gs * _sigmoid(gs)).astype(BF16)

    (cat,) = _rowcall(mix_body, "mix", L, tm,
                      [(o, _rspec(tm, D_ATTN)), (z, _rspec(tm, D_ATTN, 1)), (glu, _rspec(tm, D_SSM, 0)),
                       (glu, _rspec(tm, D_SSM, 1)), (z, _rspec(tm, D_SSM, 3))], [(D_ATTN + D_SSM, BF16)])
    h1 = _mm(cat, wout, "nn", "out_proj", add=xs)
    n2 = _norm_in(h1, norm_ple, "norm_ple")
    gpre = _mm(n2, wpg, "nn", "ple_gate")
    pb = ps.astype(BF16)
    pp = _mm(pb, wpp3, "nn", "ple_proj", b_blk=True)

    nf = norm_final.reshape(1, D)

    def tail_body(h1_ref, gp_ref, pp_ref, t_ref, g_ref, dh2_ref, dpp_ref, dsg_ref, loss_ref, dg_ref):
        gate = _sigmoid(gp_ref[...])
        ppv = pp_ref[...]
        h2 = h1_ref[...] + gate * ppv
        r = _rms(h2)
        hh = h2 * r
        err = hh * g_ref[...] - t_ref[...]
        _acc(loss_ref, jnp.broadcast_to(0.5 * jnp.sum(jnp.mean(err * err, axis=-1, keepdims=True)), loss_ref.shape))
        dy = err * (1.0 / D)
        _acc(dg_ref, _colsum(dy * hh))
        dh2 = _rms_bwd(dy, hh, r, g_ref[...])
        dh2_ref[...] = dh2
        dpp_ref[...] = (dh2 * gate).astype(BF16)
        dsg_ref[...] = (dh2 * ppv * gate * (1.0 - gate)).astype(BF16)

    dh2, dpp, dsg, loss_acc, d_nf = _rowcall(
        tail_body, "tail", L, tm,
        [(h1, _rspec(tm, D)), (gpre, _rspec(tm, D)), (pp, _rspec(tm, D)), (tgt, _rspec(tm, D)), (nf, _fspec(nf.shape))],
        [(D, F32), (D, BF16), (D, BF16)], [(1, LANES), (1, D)])
    loss = lax.psum(loss_acc[0, 0], ("x", "y", "c"))

    g_wpp3 = _mm(pb, dpp, "tn", "d_ple_proj", out_dtype=BF16, out_blk=n_pp)
    g_wpg = _mm(n2, dsg, "tn", "d_ple_gate", out_dtype=BF16)
    dn2 = _mm(dsg, wpg, "nt", "d_norm_ple_in")

    def ple_bwd_body(h1_ref, dn_ref, dh2_ref, g_ref, dh1_ref, dh1b_ref, dg_ref):
        h1v = h1_ref[...]
        r = _rms(h1v)
        hh = h1v * r
        dn = dn_ref[...]
        _acc(dg_ref, _colsum(dn * hh))
        dh1 = dh2_ref[...] + _rms_bwd(dn, hh, r, g_ref[...])
        dh1_ref[...] = dh1
        dh1b_ref[...] = dh1.astype(BF16)

    dh1, dh1b, d_nple = _rowcall(
        ple_bwd_body, "ple_bwd", L, tm,
        [(h1, _rspec(tm, D)), (dn2, _rspec(tm, D)), (dh2, _rspec(tm, D)), (norm_ple, _fspec(norm_ple.shape))],
        [(D, F32), (D, BF16)], [(1, D)])

    dcat = _mm(dh1b, wout, "nt", "d_cat")
    g_wout = _mm(cat, dh1b, "tn", "d_out_proj", out_dtype=BF16)

    def mix_bwd_body(dca_ref, dcs_ref, o_ref, ga_ref, gla_ref, glb_ref, gs_ref,
                     do_ref, dga_ref, dgs_ref, dglu_ref, db_ref):
        dca, dcs, ga, gs = dca_ref[...], dcs_ref[...], ga_ref[...], gs_ref[...]
        sa, ss, sb = _sigmoid(ga), _sigmoid(gs), _sigmoid(glb_ref[...])
        gla = gla_ref[...]
        do_ref[...] = (dca * ga * sa).astype(BF16)
        dga_ref[...] = (dca * o_ref[...] * sa * (1.0 + ga * (1.0 - sa))).astype(BF16)
        dgs_ref[...] = (dcs * gla * sb * ss * (1.0 + gs * (1.0 - ss))).astype(BF16)
        dy2 = dcs * gs * ss
        da, db = dy2 * sb, dy2 * gla * sb * (1.0 - sb)
        dglu_ref[:, :D_SSM] = da.astype(BF16)
        dglu_ref[:, D_SSM:] = db.astype(BF16)
        _acc(db_ref, jnp.concatenate([_colsum(da), _colsum(db)], axis=-1))

    do, dga, dgs, dglu, g_bglu = _rowcall(
        mix_bwd_body, "mix_bwd", L, tm,
        [(dcat, _rspec(tm, D_ATTN, 0)), (dcat, _rspec(tm, D_SSM, 1)), (o, _rspec(tm, D_ATTN)),
         (z, _rspec(tm, D_ATTN, 1)), (glu, _rspec(tm, D_SSM, 0)), (glu, _rspec(tm, D_SSM, 1)),
         (z, _rspec(tm, D_SSM, 3))],
        [(D_ATTN, BF16), (D_ATTN, BF16), (D_SSM, BF16), (2 * D_SSM, BF16)], [(1, 2 * D_SSM)])

    g_wglu3 = _mm(gy, dglu, "tn", "d_glu_proj", out_dtype=BF16, out_blk=n_glu)
    dgy = _mm(dglu, wglu3, "nt", "d_gelu_out", b_blk=True)

    def gelu_bwd_body(dg_ref, y_ref, o_ref):
        o_ref[...] = dg_ref[...] * _gelu_grad(y_ref[...])

    (dys,) = _rowcall(gelu_bwd_body, "gelu_bwd", L, tm,
                      [(dgy, _rspec(tm, D_SSM)), (ys, _rspec(tm, D_SSM))], [(D_SSM, F32)])
    du_perm, pg = _ssm_bwd(u_perm, _seg_perm(dys), ssm_prm, ssm_d)
    du = _seg_unperm(du_perm)

    dqs, dkr, dvv, (l_wglu, l_wout, l_wpg, l_wpp) = _attn_bwd(
        qr, kr, vb, kt, do,
        _Carry("a2a", [g_wglu3, g_wout.reshape(N_DEV, -1, D), g_wpg.reshape(N_DEV, -1, D), g_wpp3]))

    scale = HEAD_DIM ** -0.5
    kblk = 4 * D_ATTN // D_KV
    tmq = _pick(L, (512, 256))

    def qkv_bwd_body(dq_ref, dk_ref, dv_ref, q_ref, k_ref, cos_ref, sin_ref, qn_ref, kn_ref,
                     dqo_ref, dko_ref, dvo_ref, dqn_ref, dkn_ref):
        c, s = cos_ref[...], sin_ref[...]

        def head(g, xh, w):
            dn = g * c + _partner(g * s)
            r = _rms(xh)
            xhat = xh * r
            return _rms_bwd(dn, xhat, r, w), _colsum(dn * xhat)

        dqn = jnp.zeros((1, HEAD_DIM), F32)
        for h in range(N_HEADS):
            sl = slice(h * HEAD_DIM, (h + 1) * HEAD_DIM)
            dx, dw = head(dq_ref[:, sl] * scale, q_ref[:, sl], qn_ref[...])
            dqo_ref[:, sl] = dx.astype(BF16)
            dqn = dqn + dw
        dkn = jnp.zeros((1, HEAD_DIM), F32)
        for h in range(N_KV):
            sl = slice(h * HEAD_DIM, (h + 1) * HEAD_DIM)
            dx, dw = head(dk_ref[:, sl], k_ref[:, sl], kn_ref[...])
            dko_ref[:, sl] = dx.astype(BF16)
            dkn = dkn + dw
        dvo_ref[...] = dv_ref[...].astype(BF16)
        _acc(dqn_ref, dqn)
        _acc(dkn_ref, dkn)

    dq, dk, dv, g_qn, g_kn = _rowcall(
        qkv_bwd_body, "qkv_bwd", L, tmq,
        [(dqs, _rspec(tmq, D_ATTN)), (dkr, _rspec(tmq, D_KV)), (dvv, _rspec(tmq, D_KV)),
         (z, _rspec(tmq, D_ATTN, 0)), (z, _rspec(tmq, D_KV, kblk)), (cos, _rspec(tmq, HEAD_DIM)),
         (sin, _rspec(tmq, HEAD_DIM)), (q_norm, _fspec(q_norm.shape)), (k_norm, _fspec(k_norm.shape))],
        [(D_ATTN, BF16), (D_KV, BF16), (D_KV, BF16)], [(1, HEAD_DIM), (1, HEAD_DIM)])

    dz3 = (jnp.concatenate([dq, dk, dv, dga, du.astype(BF16), dgs], axis=1)
           .reshape(L, N_DEV, n_in).transpose(1, 0, 2))
    g_win3 = _mm(hn, dz3, "tn", "d_in_proj", out_dtype=BF16, b_blk=True, out_blk=n_in)
    pg_send = pg.reshape(N_DEV, (n_slab // N_DEV) * 2 * PG_ROWS, SLAB_S)
    dhn, (l_win, l_pg) = _mm(dz3, win3, "nt", "d_norm_mix_in", a_blk=True, b_blk=True,
                             carry=_Carry("a2a", [g_win3, pg_send]))

    def in_bwd_body(x_ref, dn_ref, dh1_ref, g_ref, dx_ref, dg_ref):
        xv = x_ref[...]
        r = _rms(xv)
        hh = xv * r
        dn = dn_ref[...]
        _acc(dg_ref, _colsum(dn * hh))
        dx_ref[...] = dh1_ref[...] + _rms_bwd(dn, hh, r, g_ref[...])

    grad_x, g_nmix = _rowcall(
        in_bwd_body, "in_bwd", L, tm,
        [(xs, _rspec(tm, D)), (dhn, _rspec(tm, D)), (dh1, _rspec(tm, D)), (norm_mix, _fspec(norm_mix.shape))],
        [(D, F32)], [(1, D)])

    tiny_parts = [g_nmix, g_bglu, d_nple, d_nf, g_qn, g_kn]
    tiny_flat = jnp.concatenate([t.reshape(-1) for t in tiny_parts])
    tiny_rows = -(-tiny_flat.shape[0] // (8 * LANES)) * 8
    tiny = jnp.pad(tiny_flat, (0, tiny_rows * LANES - tiny_flat.shape[0])).reshape(tiny_rows, LANES)
    pg_sum = _sum_blocks(l_pg, "sum_ssm_grads")
    pg_all, tiny_all = _all_gather([pg_sum, tiny], "gather_small_grads")
    (g_bt_re, g_bt_im, g_c_re, g_c_im, g_a_re, g_a_im, g_ldt, g_skip) = _ssm_param_grads(
        pg_all.reshape(n_slab, 2, PG_ROWS, SLAB_S), ssm_prm)
    tiny_sum = _sum_blocks(tiny_all, "sum_tiny_grads").reshape(-1)
    tiny_grads, off = [], 0
    for t in tiny_parts:
        tiny_grads.append(tiny_sum[off:off + t.size].reshape(t.shape))
        off += t.size
    r_nmix, r_bglu, r_nple, r_nf, r_qn, r_kn = tiny_grads

    grads, deltas, new_ms, new_vs = {}, {}, {}, {}
    big = [("w_in", w_in, l_win, m_w_in, v_w_in), ("w_glu", w_glu, l_wglu, m_w_glu, v_w_glu),
           ("w_out", w_out, l_wout, m_w_out, v_w_out), ("w_ple_gate", w_ple_gate, l_wpg, m_w_ple_gate, v_w_ple_gate),
           ("w_ple_proj", w_ple_proj, l_wpp, m_w_ple_proj, v_w_ple_proj)]
    for name, w, ld, m, v in big:
        shp = w.shape
        outs = _adamw_reduce(w[0], ld, m[0], v[0], "adamw_" + name)
        grads[name], deltas[name], new_ms[name], new_vs[name] = [t.reshape(shp) for t in outs]
    small = [("norm_mix", norm_mix, r_nmix, m_norm_mix, v_norm_mix, (1, D)),
             ("q_norm", q_norm, r_qn, m_q_norm, v_q_norm, (1, HEAD_DIM)),
             ("k_norm", k_norm, r_kn, m_k_norm, v_k_norm, (1, HEAD_DIM)),
             ("ssm_a_re", ssm_a_re, g_a_re, m_ssm_a_re, v_ssm_a_re, (2 * G, SSM_P)),
             ("ssm_a_im", ssm_a_im, g_a_im, m_ssm_a_im, v_ssm_a_im, (2 * G, SSM_P)),
             ("ssm_log_dt", ssm_log_dt, g_ldt, m_ssm_log_dt, v_ssm_log_dt, (2, G)),
             ("ssm_b_re", ssm_b_re, g_bt_re.transpose(0, 1, 3, 2), m_ssm_b_re, v_ssm_b_re, (2 * G * SSM_P, SSM_H)),
             ("ssm_b_im", ssm_b_im, g_bt_im.transpose(0, 1, 3, 2), m_ssm_b_im, v_ssm_b_im, (2 * G * SSM_P, SSM_H)),
             ("ssm_c_re", ssm_c_re, g_c_re, m_ssm_c_re, v_ssm_c_re, (2 * G * SSM_H, SSM_P)),
             ("ssm_c_im", ssm_c_im, g_c_im, m_ssm_c_im, v_ssm_c_im, (2 * G * SSM_H, SSM_P)),
             ("ssm_d", ssm_d, g_skip, m_ssm_d, v_ssm_d, (1, D_SSM)),
             ("b_glu", b_glu, r_bglu, m_b_glu, v_b_glu, (1, 2 * D_SSM)),
             ("norm_ple", norm_ple, r_nple, m_norm_ple, v_norm_ple, (1, D)),
             ("norm_final", norm_final, r_nf, m_norm_final, v_norm_final, (1, D))]
    for name, w, g, m, v, s2 in small:
        shp = w.shape
        outs = _adamw(w.reshape(s2), g.reshape(s2), m.reshape(s2), v.reshape(s2), "adamw_" + name)
        grads[name] = g.reshape(shp)
        deltas[name], new_ms[name], new_vs[name] = [t.reshape(shp) for t in outs]

    order = ["norm_mix", "w_in", "q_norm", "k_norm", "ssm_a_re", "ssm_a_im", "ssm_log_dt", "ssm_b_re", "ssm_b_im",
             "ssm_c_re", "ssm_c_im", "ssm_d", "w_glu", "b_glu", "w_out", "norm_ple", "w_ple_gate", "w_ple_proj",
             "norm_final"]
    return (loss, grad_x[None], *[grads[k] for k in order], *[deltas[k] for k in order],
            *[new_ms[k] for k in order], *[new_vs[k] for k in order])
```

```python
import functools
import math

import numpy as np
import jax
import jax.numpy as jnp
from jax import lax
from jax.experimental import pallas as pl
from jax.experimental.pallas import tpu as pltpu

F32 = jnp.float32
BF16 = jnp.bfloat16

N_DEV = 8
EPS = 1e-6
GRID_W = 64
ROPE_THETA = 10000.0
HEAD_DIM = 128
N_HEADS = 8
N_KV = 2
REP = N_HEADS // N_KV
D_ATTN = N_HEADS * HEAD_DIM
D_KV = N_KV * HEAD_DIM
SSM_H = 16
SSM_P = 64
SLAB = 128
SLAB_G = SLAB // SSM_H
SLAB_S = SLAB_G * SSM_P
SEG = 8
LANES = 128
PG_ROWS = 72
VMEM_LIMIT = 48 << 20

ADAM_LR = 0.001
ADAM_B1 = 0.9
ADAM_B2 = 0.999
ADAM_EPS = 1e-08
ADAM_WD = 0.01
ADAM_STEP = 10


def _pick(n, cands):
    for c in cands:
        if n % c == 0:
            return c
    return n


def _cparams(sem, vmem=VMEM_LIMIT):
    return pltpu.CompilerParams(dimension_semantics=sem, vmem_limit_bytes=vmem)


class _Carry:
    def __init__(self, kind, xs):
        self.kind, self.xs, self.n = kind, list(xs), len(xs)
        lead = (N_DEV,) if kind == "gather" else ()
        self.out_shape = [jax.ShapeDtypeStruct(lead + v.shape, v.dtype) for v in xs]
        self.specs = [pl.BlockSpec(memory_space=pl.ANY)] * self.n
        self.scratch = [pltpu.SemaphoreType.DMA((self.n, N_DEV - 1)), pltpu.SemaphoreType.DMA((self.n, N_DEV - 1)),
                        pltpu.SemaphoreType.DMA((self.n,))]

    def _copies(self, x_refs, out_refs, sems):
        send_sems, recv_sems, local_sems = sems
        x, y, c = lax.axis_index("x"), lax.axis_index("y"), lax.axis_index("c")
        me = _dev_index((x, y, c))
        mine, sends, arrivals = [], [], []
        for a in range(self.n):
            src_mine = x_refs[a] if self.kind == "gather" else x_refs[a].at[me]
            mine.append(pltpu.make_async_copy(src_mine, out_refs[a].at[me], local_sems.at[a]))
            for k in range(1, N_DEV):
                peer = _peer(k, x, y, c)
                src = x_refs[a] if self.kind == "gather" else x_refs[a].at[_dev_index(peer)]
                sends.append(pltpu.make_async_remote_copy(
                    src_ref=src, dst_ref=out_refs[a].at[me], send_sem=send_sems.at[a, k - 1],
                    recv_sem=recv_sems.at[a, k - 1], device_id=peer, device_id_type=pl.DeviceIdType.MESH))
                land = out_refs[a].at[_dev_index(peer)]
                arrivals.append(pltpu.make_async_remote_copy(
                    src_ref=land, dst_ref=land, send_sem=send_sems.at[a, k - 1],
                    recv_sem=recv_sems.at[a, k - 1], device_id=peer, device_id_type=pl.DeviceIdType.MESH))
        return mine, sends, arrivals

    def start(self, x_refs, out_refs, sems):
        mine, sends, _ = self._copies(x_refs, out_refs, sems)
        for cp in mine + sends:
            cp.start()

    def wait(self, x_refs, out_refs, sems):
        mine, sends, arrivals = self._copies(x_refs, out_refs, sems)
        for cp in arrivals:
            cp.wait_recv()
        for cp in sends:
            cp.wait_send()
        for cp in mine:
            cp.wait()


def _grid_edges(grid):
    first = functools.reduce(lambda p, q: p & q, [pl.program_id(d) == 0 for d in range(len(grid))])
    last = functools.reduce(lambda p, q: p & q, [pl.program_id(d) == g - 1 for d, g in enumerate(grid)])
    return first, last


def _mm(a, b, mode, name, out_dtype=F32, add=None, bias=None, a_blk=False, b_blk=False, out_blk=0, carry=None):
    w = b.shape[2] if b_blk else out_blk
    if mode == "nn":
        M, K = a.shape
        N = b.shape[0] * w if b_blk else b.shape[1]
    elif mode == "nt":
        M = a.shape[1] if a_blk else a.shape[0]
        N = b.shape[1] if b_blk else b.shape[0]
        K = b.shape[0] * w if b_blk else b.shape[1]
    else:
        K, M = a.shape
        N = b.shape[0] * w if b_blk else b.shape[1]
    tm = _pick(M, (1024, 512, 256))
    tn = _pick(N, (1024, 768, 512, 256))
    tk = K if (mode != "tn" and K <= 2048) else _pick(K, (1024, 512, 256))
    if mode == "nt" and b_blk:
        tk = w
    elif b_blk or out_blk:
        tn = w
    nk = K // tk
    grid = (M // tm, N // tn, nk)
    if mode == "nn":
        a_spec = pl.BlockSpec((tm, tk), lambda i, j, k: (i, k))
        b_spec = (pl.BlockSpec((1, tk, tn), lambda i, j, k: (j, k, 0)) if b_blk
                  else pl.BlockSpec((tk, tn), lambda i, j, k: (k, j)))
        dims = (((1,), (0,)), ((), ()))
    elif mode == "nt":
        a_spec = (pl.BlockSpec((1, tm, tk), lambda i, j, k: (k, i, 0)) if a_blk
                  else pl.BlockSpec((tm, tk), lambda i, j, k: (i, k)))
        b_spec = (pl.BlockSpec((1, tn, tk), lambda i, j, k: (k, j, 0)) if b_blk
                  else pl.BlockSpec((tn, tk), lambda i, j, k: (j, k)))
        dims = (((1,), (1,)), ((), ()))
    else:
        a_spec = pl.BlockSpec((tk, tm), lambda i, j, k: (k, i))
        b_spec = (pl.BlockSpec((1, tk, tn), lambda i, j, k: (j, k, 0)) if b_blk
                  else pl.BlockSpec((tk, tn), lambda i, j, k: (k, j)))
        dims = (((0,), (0,)), ((), ()))
    if out_blk:
        out_spec = pl.BlockSpec((1, tm, tn), lambda i, j, k: (j, i, 0))
        out_shape = jax.ShapeDtypeStruct((N // tn, M, tn), out_dtype)
    else:
        out_spec = pl.BlockSpec((tm, tn), lambda i, j, k: (i, j))
        out_shape = jax.ShapeDtypeStruct((M, N), out_dtype)
    extras, extra_specs = [], []
    if add is not None:
        extras.append(add)
        extra_specs.append(pl.BlockSpec((tm, tn), lambda i, j, k: (i, j)))
    if bias is not None:
        extras.append(bias)
        extra_specs.append(pl.BlockSpec((1, tn), lambda i, j, k: (0, j)))

    n_ex = len(extras)
    nc = carry.n if carry else 0

    def body(a_ref, b_ref, *rest):
        ex_refs, cx = rest[:n_ex], rest[n_ex:n_ex + nc]
        o_ref, cout = rest[n_ex + nc], rest[n_ex + nc + 1:n_ex + 2 * nc + 1]
        tail = rest[n_ex + 2 * nc + 1:]
        sems = tail[:3] if carry else ()
        first, last = _grid_edges(grid)
        if carry:
            @pl.when(first)
            def _():
                carry.start(cx, cout, sems)

        def product():
            av = a_ref[0] if a_blk else a_ref[...]
            bv = b_ref[0] if b_blk else b_ref[...]
            return lax.dot_general(av, bv, dims, preferred_element_type=F32)

        def finish(out):
            for r in ex_refs:
                out = out + r[...]
            if out_blk:
                o_ref[0] = out.astype(out_dtype)
            else:
                o_ref[...] = out.astype(out_dtype)

        if nk == 1:
            finish(product())
        else:
            acc_ref = tail[-1]
            k = pl.program_id(2)

            @pl.when(k == 0)
            def _():
                acc_ref[...] = jnp.zeros_like(acc_ref)

            acc_ref[...] += product()

            @pl.when(k == nk - 1)
            def _():
                finish(acc_ref[...])

        if carry:
            @pl.when(last)
            def _():
                carry.wait(cx, cout, sems)

    scratch = (carry.scratch if carry else []) + ([pltpu.VMEM((tm, tn), F32)] if nk > 1 else [])
    outs = pl.pallas_call(
        body, name=name, grid=grid,
        in_specs=[a_spec, b_spec] + extra_specs + (carry.specs if carry else []),
        out_specs=[out_spec] + (carry.specs if carry else []),
        out_shape=[out_shape] + (carry.out_shape if carry else []),
        scratch_shapes=scratch,
        compiler_params=_cparams(("arbitrary", "arbitrary", "arbitrary")),
    )(a, b, *extras, *(carry.xs if carry else []))
    return (outs[0], outs[1:]) if carry else outs[0]


def _rspec(tm, w, cb=0):
    return pl.BlockSpec((tm, w), lambda i: (i, cb))


def _fspec(shape):
    nd = len(shape)
    return pl.BlockSpec(shape, lambda i: (0,) * nd)


def _rowcall(body, name, L, tm, ins, row_outs, acc_outs=()):
    out_shape = [jax.ShapeDtypeStruct((L, w), dt) for w, dt in row_outs]
    out_shape += [jax.ShapeDtypeStruct(s, F32) for s in acc_outs]
    out_specs = [_rspec(tm, w) for w, _ in row_outs] + [_fspec(s) for s in acc_outs]
    return pl.pallas_call(
        body, name=name, grid=(L // tm,),
        in_specs=[s for _, s in ins], out_specs=out_specs, out_shape=out_shape,
        compiler_params=_cparams(("arbitrary",)),
    )(*[a for a, _ in ins])


def _acc(ref, val):
    @pl.when(pl.program_id(0) == 0)
    def _():
        ref[...] = jnp.zeros_like(ref)
    ref[...] += val


def _colsum(v):
    return jnp.sum(v, axis=0, keepdims=True)


def _rms(xv):
    return lax.rsqrt(jnp.mean(xv * xv, axis=-1, keepdims=True) + EPS)


def _rms_bwd(dn, xhat, r, g):
    dng = dn * g
    return r * (dng - xhat * jnp.mean(dng * xhat, axis=-1, keepdims=True))


def _sigmoid(v):
    return jax.nn.sigmoid(v)


def _partner(v):
    w = v.shape[-1]
    lane = lax.broadcasted_iota(jnp.int32, v.shape, v.ndim - 1)
    first_half = (lane % 64) < 32
    return jnp.where(first_half, pltpu.roll(v, w - 32, axis=v.ndim - 1), pltpu.roll(v, 32, axis=v.ndim - 1))


def _norm_in(x, g, name):
    L, D = x.shape
    tm = _pick(L, (512, 256))

    def body(x_ref, g_ref, o_ref):
        xv = x_ref[...]
        o_ref[...] = (xv * _rms(xv) * g_ref[...]).astype(BF16)

    return _rowcall(body, name, L, tm, [(x, _rspec(tm, D)), (g, _fspec(g.shape))], [(D, BF16)])[0]


def _rope_tables(L):
    t = np.arange(L)
    rows = (t // GRID_W).astype(np.float32)
    cols = (t % GRID_W).astype(np.float32)
    n_freq = HEAD_DIM // 4
    inv_freq = np.float32(ROPE_THETA) ** (-np.arange(n_freq, dtype=np.float32) / np.float32(n_freq))
    ar = (rows[:, None] * inv_freq[None, :]).astype(np.float32).astype(np.float64)
    ac = (cols[:, None] * inv_freq[None, :]).astype(np.float32).astype(np.float64)
    cos = np.concatenate([np.cos(ar), np.cos(ar), np.cos(ac), np.cos(ac)], axis=-1).astype(np.float32)
    sin = np.concatenate([-np.sin(ar), np.sin(ar), -np.sin(ac), np.sin(ac)], axis=-1).astype(np.float32)
    return jnp.asarray(cos), jnp.asarray(sin)


def _qkv_prep(z, cos, sin, qn, kn):
    L = z.shape[0]
    tm = _pick(L, (512, 256))
    scale = HEAD_DIM ** -0.5
    kblk = 4 * D_ATTN // D_KV

    def body(q_ref, k_ref, v_ref, cos_ref, sin_ref, qn_ref, kn_ref, qo_ref, ko_ref, vo_ref, kt_ref):
        c, s = cos_ref[...], sin_ref[...]

        def head(xh, w):
            n = xh * _rms(xh) * w
            return n * c + _partner(n) * s

        for h in range(N_HEADS):
            sl = slice(h * HEAD_DIM, (h + 1) * HEAD_DIM)
            qo_ref[:, sl] = (head(q_ref[:, sl], qn_ref[...]) * scale).astype(BF16)
        for h in range(N_KV):
            sl = slice(h * HEAD_DIM, (h + 1) * HEAD_DIM)
            kr = head(k_ref[:, sl], kn_ref[...])
            ko_ref[:, sl] = kr.astype(BF16)
            kt_ref[sl, :] = kr.T.astype(BF16)
        vo_ref[...] = v_ref[...].astype(BF16)

    return pl.pallas_call(
        body, name="qkv_prep", grid=(L // tm,),
        in_specs=[_rspec(tm, D_ATTN, 0), _rspec(tm, D_KV, kblk), _rspec(tm, D_KV, kblk + 1),
                  _rspec(tm, HEAD_DIM), _rspec(tm, HEAD_DIM), _fspec(qn.shape), _fspec(kn.shape)],
        out_specs=[_rspec(tm, D_ATTN), _rspec(tm, D_KV), _rspec(tm, D_KV),
                   pl.BlockSpec((D_KV, tm), lambda i: (0, i))],
        out_shape=[jax.ShapeDtypeStruct((L, D_ATTN), BF16), jax.ShapeDtypeStruct((L, D_KV), BF16),
                   jax.ShapeDtypeStruct((L, D_KV), BF16), jax.ShapeDtypeStruct((D_KV, L), BF16)],
        compiler_params=_cparams(("arbitrary",)),
    )(z, z, z, cos, sin, qn, kn)


def _col_to_row(col):
    n = col.shape[0]
    eye = lax.broadcasted_iota(jnp.int32, (n, n), 0) == lax.broadcasted_iota(jnp.int32, (n, n), 1)
    return jnp.sum(jnp.where(eye, col, 0.0), axis=0, keepdims=True)


def _attn_fwd(q, k, v, carry=None):
    L = q.shape[0]
    tq = _pick(L, (256, 128))
    grid = (N_HEADS, L // tq)
    nc = carry.n if carry else 0

    def body(q_ref, k_ref, v_ref, *rest):
        cx, (o_ref, lse_ref) = rest[:nc], rest[nc:nc + 2]
        cout, sems = rest[nc + 2:2 * nc + 2], rest[2 * nc + 2:]
        first, last = _grid_edges(grid)
        if carry:
            @pl.when(first)
            def _():
                carry.start(cx, cout, sems)

        s = lax.dot_general(q_ref[...], k_ref[...], (((1,), (1,)), ((), ())), preferred_element_type=F32)
        m = jnp.max(s, axis=-1, keepdims=True)
        e = jnp.exp(s - m)
        l = jnp.sum(e, axis=-1, keepdims=True)
        o_ref[...] = jnp.dot(e.astype(BF16), v_ref[...], preferred_element_type=F32) / l
        lse_ref[0] = _col_to_row(m + jnp.log(l))

        if carry:
            @pl.when(last)
            def _():
                carry.wait(cx, cout, sems)

    outs = pl.pallas_call(
        body, name="attn_fwd", grid=grid,
        in_specs=[pl.BlockSpec((tq, HEAD_DIM), lambda h, i: (i, h)),
                  pl.BlockSpec((L, HEAD_DIM), lambda h, i: (0, h // REP)),
                  pl.BlockSpec((L, HEAD_DIM), lambda h, i: (0, h // REP))] + (carry.specs if carry else []),
        out_specs=[pl.BlockSpec((tq, HEAD_DIM), lambda h, i: (i, h)),
                   pl.BlockSpec((1, 1, tq), lambda h, i: (h, 0, i))] + (carry.specs if carry else []),
        out_shape=[jax.ShapeDtypeStruct((L, D_ATTN), F32), jax.ShapeDtypeStruct((N_HEADS, 1, L), F32)]
        + (carry.out_shape if carry else []),
        scratch_shapes=carry.scratch if carry else [],
        compiler_params=_cparams(("arbitrary", "arbitrary")),
    )(q, k, v, *(carry.xs if carry else []))
    return outs[0], outs[1], outs[2:]


def _attn_bwd(q, k, v, kt, do, o, lse, carry=None):
    L = q.shape[0]
    tq = _pick(L, (256, 128))
    kc = _pick(L, (256, 128))
    nt = (((1,), (1,)), ((), ()))
    grid = (N_KV, REP, L // tq)
    nc = carry.n if carry else 0

    def body(q_ref, do_ref, o_ref, lse_ref, k_ref, v_ref, kt_ref, *rest):
        cx, (dq_ref, dk_ref, dv_ref) = rest[:nc], rest[nc:nc + 3]
        cout, sems = rest[nc + 3:2 * nc + 3], rest[2 * nc + 3:]
        first, last = _grid_edges(grid)
        if carry:
            @pl.when(first)
            def _():
                carry.start(cx, cout, sems)

        @pl.when((pl.program_id(1) == 0) & (pl.program_id(2) == 0))
        def _():
            dk_ref[...] = jnp.zeros_like(dk_ref)
            dv_ref[...] = jnp.zeros_like(dv_ref)

        qv, dov = q_ref[...], do_ref[...]
        lse_row = lse_ref[0]
        delta = _col_to_row(jnp.sum(dov.astype(F32) * o_ref[...], axis=-1, keepdims=True))
        dqt = jnp.zeros((HEAD_DIM, tq), F32)
        for c in range(L // kc):
            sl = slice(c * kc, (c + 1) * kc)
            st = lax.dot_general(k_ref[sl, :], qv, nt, preferred_element_type=F32)
            pt = jnp.exp(st - lse_row)
            dpt = lax.dot_general(v_ref[sl, :], dov, nt, preferred_element_type=F32)
            dst = (pt * (dpt - delta)).astype(BF16)
            dv_ref[sl, :] += jnp.dot(pt.astype(BF16), dov, preferred_element_type=F32)
            dk_ref[sl, :] += jnp.dot(dst, qv, preferred_element_type=F32)
            dqt = dqt + jnp.dot(kt_ref[:, sl], dst, preferred_element_type=F32)
        dq_ref[...] = dqt.T

        if carry:
            @pl.when(last)
            def _():
                carry.wait(cx, cout, sems)

    head = lambda g, r, i: (i, g * REP + r)
    outs = pl.pallas_call(
        body, name="attn_bwd", grid=grid,
        in_specs=[pl.BlockSpec((tq, HEAD_DIM), head), pl.BlockSpec((tq, HEAD_DIM), head),
                  pl.BlockSpec((tq, HEAD_DIM), head),
                  pl.BlockSpec((1, 1, tq), lambda g, r, i: (g * REP + r, 0, i)),
                  pl.BlockSpec((L, HEAD_DIM), lambda g, r, i: (0, g)),
                  pl.BlockSpec((L, HEAD_DIM), lambda g, r, i: (0, g)),
                  pl.BlockSpec((HEAD_DIM, L), lambda g, r, i: (g, 0))] + (carry.specs if carry else []),
        out_specs=[pl.BlockSpec((tq, HEAD_DIM), head),
                   pl.BlockSpec((L, HEAD_DIM), lambda g, r, i: (0, g)),
                   pl.BlockSpec((L, HEAD_DIM), lambda g, r, i: (0, g))] + (carry.specs if carry else []),
        out_shape=[jax.ShapeDtypeStruct((L, D_ATTN), F32), jax.ShapeDtypeStruct((L, D_KV), F32),
                   jax.ShapeDtypeStruct((L, D_KV), F32)] + (carry.out_shape if carry else []),
        scratch_shapes=carry.scratch if carry else [],
        compiler_params=_cparams(("arbitrary", "arbitrary", "arbitrary")),
    )(q, do, o, lse, k, v, kt, *(carry.xs if carry else []))
    return outs[0], outs[1], outs[2], outs[3:]


def _seg_perm(a):
    L, C = a.shape
    return a.reshape(SEG, L // SEG, C).transpose(1, 0, 2).reshape(L, C)


def _seg_unperm(a):
    L, C = a.shape
    return a.reshape(L // SEG, SEG, C).transpose(1, 0, 2).reshape(L, C)


def _cmul(ar, ai, br, bi):
    return ar * br - ai * bi, ar * bi + ai * br


def _rows8(rr):
    return pl.ds(pl.multiple_of(rr * SEG, SEG), SEG)


def _seg_scan(xr_ref, xi_ref, ar, ai, reverse, n_rows, visit=None, visit_init=()):
    shape = ar.shape
    zero = jnp.zeros(shape, F32)

    def index(r):
        return (n_rows - 1 - r) if reverse else r

    def ends(r, carry):
        cr, ci = carry
        sl = _rows8(index(r))
        pr, pi = _cmul(ar, ai, cr, ci)
        return pr + xr_ref[sl, :], pi + xi_ref[sl, :]

    er, ei = lax.fori_loop(0, n_rows, ends, (zero, zero))

    pr, pi = ar, ai
    for _ in range(int(math.log2(n_rows))):
        pr, pi = _cmul(pr, pi, pr, pi)
    sub = lax.broadcasted_iota(jnp.int32, shape, 0)
    shift = (SEG - 1) if reverse else 1
    edge = (SEG - 1) if reverse else 0
    inr, ini = zero, zero
    for _ in range(SEG - 1):
        tr, ti = _cmul(pr, pi, inr, ini)
        inr = jnp.where(sub == edge, 0.0, pltpu.roll(tr + er, shift, axis=0))
        ini = jnp.where(sub == edge, 0.0, pltpu.roll(ti + ei, shift, axis=0))

    def step(rr, carry, last):
        cr, ci = carry[:2]
        sl = _rows8(rr)
        pr, pi = _cmul(ar, ai, cr, ci)
        nr, ni = pr + xr_ref[sl, :], pi + xi_ref[sl, :]
        xr_ref[sl, :] = nr
        xi_ref[sl, :] = ni
        acc = visit(rr, nr, ni, carry[2:], last) if visit else ()
        return (nr, ni, *acc)

    carry = lax.fori_loop(0, n_rows - 1, lambda r, c: step(index(r), c, False), (inr, ini, *visit_init))
    carry = step(index(n_rows - 1), carry, True)
    return inr, ini, carry[2:]


def _discretise(a_re, a_im, ldt):
    lr = jnp.minimum(a_re, -1e-4)
    li = a_im
    dt = jnp.exp(ldt)
    mag = jnp.exp(lr * dt)
    lbr = mag * jnp.cos(li * dt)
    lbi = mag * jnp.sin(li * dt)
    den = lr * lr + li * li
    nr = lbr - 1.0
    fr = (nr * lr + lbi * li) / den
    fi = (lbi * lr - nr * li) / den
    return lr, li, dt, lbr, lbi, fr, fi


def _lane_row(v):
    return jnp.concatenate([v[g:g + 1, :] for g in range(v.shape[0])], axis=1)


def _ssm_fill_maps(d, prm, tmp_ref, maps):
    a_re_ref, a_im_ref, ldt_ref, bt_re_ref, bt_im_ref, c_re_ref, c_im_ref = prm
    _, _, _, lbr, lbi, fr, fi = _discretise(a_re_ref[d], a_im_ref[d], ldt_ref[d])

    def fill(dst, piece):
        tmp_ref[...] = jnp.zeros_like(tmp_ref)
        for g in range(SLAB_G):
            tmp_ref[g * SSM_H:(g + 1) * SSM_H, g * SSM_P:(g + 1) * SSM_P] = piece(g)
        dst[...] = tmp_ref[...].astype(BF16)

    wbr, wbi, wcr, wci = maps
    fill(wbr, lambda g: fr[g:g + 1] * bt_re_ref[d, g] - fi[g:g + 1] * bt_im_ref[d, g])
    fill(wbi, lambda g: fr[g:g + 1] * bt_im_ref[d, g] + fi[g:g + 1] * bt_re_ref[d, g])
    fill(wcr, lambda g: c_re_ref[d, g])
    fill(wci, lambda g: c_im_ref[d, g])
    return _lane_row(lbr), _lane_row(lbi)


def _ssm_param_specs():
    pole = pl.BlockSpec((2, SLAB_G, SSM_P), lambda j: (0, j, 0))
    step = pl.BlockSpec((2, SLAB_G, 1), lambda j: (0, j, 0))
    mat = pl.BlockSpec((2, SLAB_G, SSM_H, SSM_P), lambda j: (0, j, 0, 0))
    return [pole, pole, step, mat, mat, mat, mat]


_MAP_SCRATCH = [pltpu.VMEM((SLAB, SLAB_S), F32)] + [pltpu.VMEM((SLAB, SLAB_S), BF16)] * 4
_NT = (((1,), (1,)), ((), ()))


def _ssm_fwd(u, prm, dskip):
    L, C = u.shape
    n_rows = L // SEG
    tc = _pick(L, (512, 256))
    u_spec = pl.BlockSpec((L, SLAB), lambda j: (0, j))
    d_spec = pl.BlockSpec((1, SLAB), lambda j: (0, j))

    def body(u_ref, *rest):
        prm_refs, d_ref, y_ref = rest[:7], rest[7], rest[8]
        tmp_ref, maps, xr_ref, xi_ref = rest[9], rest[10:14], rest[14], rest[15]
        wbr, wbi, wcr, wci = maps
        y_ref[...] = u_ref[...] * d_ref[...]
        for d in range(2):
            lam_r, lam_i = _ssm_fill_maps(d, prm_refs, tmp_ref, maps)

            def inp(c, _):
                sl = pl.ds(pl.multiple_of(c * tc, tc), tc)
                ub = u_ref[sl, :].astype(BF16)
                xr_ref[sl, :] = jnp.dot(ub, wbr[...], preferred_element_type=F32)
                xi_ref[sl, :] = jnp.dot(ub, wbi[...], preferred_element_type=F32)
                return 0

            lax.fori_loop(0, L // tc, inp, 0)
            ar = jnp.broadcast_to(lam_r, (SEG, SLAB_S))
            ai = jnp.broadcast_to(lam_i, (SEG, SLAB_S))
            _seg_scan(xr_ref, xi_ref, ar, ai, d == 1, n_rows)

            def outp(c, _):
                sl = pl.ds(pl.multiple_of(c * tc, tc), tc)
                y_ref[sl, :] += (
                    lax.dot_general(xr_ref[sl, :].astype(BF16), wcr[...], _NT, preferred_element_type=F32)
                    - lax.dot_general(xi_ref[sl, :].astype(BF16), wci[...], _NT, preferred_element_type=F32))
                return 0

            lax.fori_loop(0, L // tc, outp, 0)

    return pl.pallas_call(
        body, name="ssm_fwd", grid=(C // SLAB,),
        in_specs=[u_spec] + _ssm_param_specs() + [d_spec],
        out_specs=u_spec, out_shape=jax.ShapeDtypeStruct((L, C), F32),
        scratch_shapes=_MAP_SCRATCH + [pltpu.VMEM((L, SLAB_S), F32)] * 2,
        compiler_params=_cparams(("arbitrary",)),
    )(u, *prm, dskip)


def _ssm_bwd(u, dy, prm, dskip):
    L, C = u.shape
    n_rows = L // SEG
    n_slab = C // SLAB
    tc = _pick(L, (512, 256))
    u_spec = pl.BlockSpec((L, SLAB), lambda j: (0, j))
    d_spec = pl.BlockSpec((1, SLAB), lambda j: (0, j))
    pg_spec = pl.BlockSpec((1, 2, PG_ROWS, SLAB_S), lambda j: (j, 0, 0, 0))

    def body(u_ref, dy_ref, *rest):
        prm_refs, d_ref, du_ref, pg_ref = rest[:7], rest[7], rest[8], rest[9]
        tmp_ref, maps, acc_ref = rest[10], rest[11:15], rest[15]
        xr_ref, xi_ref, gr_ref, gi_ref = rest[16:20]
        wbr, wbi, wcr, wci = maps
        du_ref[...] = dy_ref[...] * d_ref[...]
        pg_ref[...] = jnp.zeros_like(pg_ref)
        pg_ref[0, 0, 66:67, 0:SLAB] = _colsum(dy_ref[...] * u_ref[...])
        for d in range(2):
            lam_r, lam_i = _ssm_fill_maps(d, prm_refs, tmp_ref, maps)

            def inp(c, _):
                sl = pl.ds(pl.multiple_of(c * tc, tc), tc)
                ub = u_ref[sl, :].astype(BF16)
                dyb = dy_ref[sl, :].astype(BF16)
                xr_ref[sl, :] = jnp.dot(ub, wbr[...], preferred_element_type=F32)
                xi_ref[sl, :] = jnp.dot(ub, wbi[...], preferred_element_type=F32)
                gr_ref[sl, :] = jnp.dot(dyb, wcr[...], preferred_element_type=F32)
                gi_ref[sl, :] = -jnp.dot(dyb, wci[...], preferred_element_type=F32)
                return 0

            lax.fori_loop(0, L // tc, inp, 0)
            ar = jnp.broadcast_to(lam_r, (SEG, SLAB_S))
            ai = jnp.broadcast_to(lam_i, (SEG, SLAB_S))
            inr, ini, _ = _seg_scan(xr_ref, xi_ref, ar, ai, d == 1, n_rows)

            def pole(rr, lr, li, acc, last):
                if last:
                    pr, pi = inr, ini
                else:
                    nb = _rows8(rr + 1 if d == 1 else rr - 1)
                    pr, pi = xr_ref[nb, :], xi_ref[nb, :]
                return acc[0] + lr * pr + li * pi, acc[1] + li * pr - lr * pi

            zero = jnp.zeros((SEG, SLAB_S), F32)
            _, _, (accr, acci) = _seg_scan(gr_ref, gi_ref, ar, -ai, d == 0, n_rows, pole, (zero, zero))
            pg_ref[0, d, 64:65, :] = _colsum(accr)
            pg_ref[0, d, 65:66, :] = _colsum(acci)

            acc_ref[...] = jnp.zeros_like(acc_ref)

            def outp(c, _):
                sl = pl.ds(pl.multiple_of(c * tc, tc), tc)
                lrb, lib = gr_ref[sl, :].astype(BF16), gi_ref[sl, :].astype(BF16)
                du_ref[sl, :] += (lax.dot_general(lrb, wbr[...], _NT, preferred_element_type=F32)
                                  + lax.dot_general(lib, wbi[...], _NT, preferred_element_type=F32))
                ut = u_ref[sl, :].T.astype(BF16)
                dyt = dy_ref[sl, :].T.astype(BF16)
                acc_ref[0] += jnp.dot(ut, lrb, preferred_element_type=F32)
                acc_ref[1] += jnp.dot(ut, lib, preferred_element_type=F32)
                acc_ref[2] += jnp.dot(dyt, xr_ref[sl, :].astype(BF16), preferred_element_type=F32)
                acc_ref[3] -= jnp.dot(dyt, xi_ref[sl, :].astype(BF16), preferred_element_type=F32)
                return 0

            lax.fori_loop(0, L // tc, outp, 0)
            for m in range(4):
                for g in range(SLAB_G):
                    lanes = slice(g * SSM_P, (g + 1) * SSM_P)
                    pg_ref[0, d, m * SSM_H:(m + 1) * SSM_H, lanes] = acc_ref[m, g * SSM_H:(g + 1) * SSM_H, lanes]

    return pl.pallas_call(
        body, name="ssm_bwd", grid=(n_slab,),
        in_specs=[u_spec, u_spec] + _ssm_param_specs() + [d_spec],
        out_specs=[u_spec, pg_spec],
        out_shape=[jax.ShapeDtypeStruct((L, C), F32), jax.ShapeDtypeStruct((n_slab, 2, PG_ROWS, SLAB_S), F32)],
        scratch_shapes=_MAP_SCRATCH + [pltpu.VMEM((4, SLAB, SLAB_S), F32)] + [pltpu.VMEM((L, SLAB_S), F32)] * 4,
        compiler_params=_cparams(("arbitrary",), 60 << 20),
    )(u, dy, *prm, dskip)


def _ssm_param_grads(pg, prm):
    n_slab = pg.shape[0]
    G = n_slab * SLAB_G
    pg_spec = pl.BlockSpec((1, 2, PG_ROWS, SLAB_S), lambda j: (j, 0, 0, 0))
    pole, _, step, mat = _ssm_param_specs()[:4]

    def body(pg_ref, a_re_ref, a_im_ref, ldt_ref, bt_re_ref, bt_im_ref,
             dbr_ref, dbi_ref, dcr_ref, dci_ref, dar_ref, dai_ref, dldt_ref, dd_ref):
        dd_ref[...] = pg_ref[0, 0, 66:67, 0:SLAB]
        for d in range(2):
            a_r = a_re_ref[d]
            lr, li, dt, lbr, lbi, f_r, f_i = _discretise(a_r, a_im_ref[d], ldt_ref[d])
            gfr_rows, gfi_rows, glr_rows, gli_rows = [], [], [], []
            for g in range(SLAB_G):
                lanes = slice(g * SSM_P, (g + 1) * SSM_P)
                gbr, gbi = pg_ref[0, d, 0:SSM_H, lanes], pg_ref[0, d, SSM_H:2 * SSM_H, lanes]
                b_r, b_i = bt_re_ref[d, g], bt_im_ref[d, g]
                fr, fi = f_r[g:g + 1], f_i[g:g + 1]
                dbr_ref[d, g] = fr * gbr + fi * gbi
                dbi_ref[d, g] = fr * gbi - fi * gbr
                gfr_rows.append(_colsum(gbr * b_r + gbi * b_i))
                gfi_rows.append(_colsum(gbi * b_r - gbr * b_i))
                dcr_ref[d, g] = pg_ref[0, d, 2 * SSM_H:3 * SSM_H, lanes]
                dci_ref[d, g] = pg_ref[0, d, 3 * SSM_H:4 * SSM_H, lanes]
                glr_rows.append(pg_ref[0, d, 64:65, lanes])
                gli_rows.append(pg_ref[0, d, 65:66, lanes])
            gfr, gfi = jnp.concatenate(gfr_rows, axis=0), jnp.concatenate(gfi_rows, axis=0)
            glr, gli = jnp.concatenate(glr_rows, axis=0), jnp.concatenate(gli_rows, axis=0)
            den = lr * lr + li * li
            ir, ii = lr / den, -li / den
            tr, ti = _cmul(ir, -ii, gfr, gfi)
            glbr, glbi = glr + tr, gli + ti
            qr, qi = _cmul(f_r, f_i, ir, ii)
            dlr, dli = _cmul(-qr, qi, gfr, gfi)
            zr, zi = _cmul(lbr, -lbi, glbr, glbi)
            dlr = dlr + dt * zr
            dli = dli + dt * zi
            dar_ref[d] = jnp.where(a_r < -1e-4, dlr, jnp.where(a_r == -1e-4, 0.5 * dlr, 0.0))
            dai_ref[d] = dli
            dldt_ref[d] = jnp.sum(lr * zr + li * zi, axis=-1, keepdims=True) * dt

    a_re, a_im, ldt, bt_re, bt_im = prm[:5]
    mshape = jax.ShapeDtypeStruct(bt_re.shape, F32)
    pshape = jax.ShapeDtypeStruct(a_re.shape, F32)
    return pl.pallas_call(
        body, name="ssm_param_grads", grid=(n_slab,),
        in_specs=[pg_spec, pole, pole, step, mat, mat],
        out_specs=[mat, mat, mat, mat, pole, pole, step, pl.BlockSpec((1, SLAB), lambda j: (0, j))],
        out_shape=[mshape, mshape, mshape, mshape, pshape, pshape, jax.ShapeDtypeStruct(ldt.shape, F32),
                   jax.ShapeDtypeStruct((1, n_slab * SLAB), F32)],
        compiler_params=_cparams(("arbitrary",)),
    )(pg, a_re, a_im, ldt, bt_re, bt_im)


def _peer(k, x, y, c):
    return (1 - x if k & 4 else x, 1 - y if k & 2 else y, 1 - c if k & 1 else c)


def _dev_index(pos):
    return 4 * pos[0] + 2 * pos[1] + pos[2]


def _all_gather(xs, name):
    n = len(xs)
    any_spec = pl.BlockSpec(memory_space=pl.ANY)

    def body(*refs):
        x_refs, out_refs = refs[:n], refs[n:2 * n]
        send_sems, recv_sems, local_sems = refs[2 * n:]
        x, y, c = lax.axis_index("x"), lax.axis_index("y"), lax.axis_index("c")
        me, sibling = (x, y, c), (x, y, 1 - c)
        chips = [(1 - x, y), (x, 1 - y), (1 - x, 1 - y)]

        def copy(a, k, block, to, src=None):
            dst = out_refs[a].at[_dev_index(block)]
            return pltpu.make_async_remote_copy(
                src_ref=dst if src is None else src, dst_ref=dst,
                send_sem=send_sems.at[a, k], recv_sem=recv_sems.at[a, k],
                device_id=to, device_id_type=pl.DeviceIdType.MESH)

        mine = [pltpu.make_async_copy(x_refs[a], out_refs[a].at[_dev_index(me)], local_sems.at[a]) for a in range(n)]
        for cp in mine:
            cp.start()
        first = []
        for a in range(n):
            first.append(copy(a, 0, me, sibling, src=x_refs[a]))
            first += [copy(a, 1 + j, me, (*chip, c), src=x_refs[a]) for j, chip in enumerate(chips)]
        for cp in first:
            cp.start()
        passed = []
        for j, chip in enumerate(chips):
            for a in range(n):
                copy(a, 1 + j, (*chip, c), me).wait_recv()
                fwd = copy(a, 4 + j, (*chip, c), sibling)
                fwd.start()
                passed.append(fwd)
        for a in range(n):
            copy(a, 0, sibling, me).wait_recv()
            for j, chip in enumerate(chips):
                copy(a, 4 + j, (*chip, 1 - c), me).wait_recv()
        for cp in first + passed:
            cp.wait_send()
        for cp in mine:
            cp.wait()

    return pl.pallas_call(
        body, name=name,
        out_shape=[jax.ShapeDtypeStruct((N_DEV,) + v.shape, v.dtype) for v in xs],
        in_specs=[any_spec] * n, out_specs=[any_spec] * n,
        scratch_shapes=[pltpu.SemaphoreType.DMA((n, 7)), pltpu.SemaphoreType.DMA((n, 7)),
                        pltpu.SemaphoreType.DMA((n,))],
    )(*xs)


def _sum_blocks(x, name):
    _, R, W = x.shape

    def body(x_ref, o_ref):
        acc = x_ref[0].astype(F32)
        for d in range(1, N_DEV):
            acc = acc + x_ref[d].astype(F32)
        o_ref[...] = acc

    return pl.pallas_call(body, name=name, out_shape=jax.ShapeDtypeStruct((R, W), F32),
                          compiler_params=pltpu.CompilerParams(vmem_limit_bytes=VMEM_LIMIT))(x)


def _adam_update(w, g, m, v):
    mn = ADAM_B1 * m + (1.0 - ADAM_B1) * g
    vn = ADAM_B2 * v + (1.0 - ADAM_B2) * (g * g)
    m_hat = mn / (1.0 - ADAM_B1 ** ADAM_STEP)
    v_hat = vn / (1.0 - ADAM_B2 ** ADAM_STEP)
    return -ADAM_LR * (m_hat / (jnp.sqrt(v_hat) + ADAM_EPS) + ADAM_WD * w), mn, vn


def _row_tile(R, W, budget):
    padded = -(-W // LANES) * LANES * 4
    if R * padded <= budget:
        return R
    return _pick(R, [t for t in (2048, 1024, 512, 256, 128, 64, 32, 16, 8) if t * padded <= budget])


def _adamw(w, g, m, v, name):
    R, W = w.shape
    tr = _row_tile(R, W, 1 << 20)

    def body(w_ref, g_ref, m_ref, v_ref, d_ref, mo_ref, vo_ref):
        d_ref[...], mo_ref[...], vo_ref[...] = _adam_update(w_ref[...], g_ref[...], m_ref[...], v_ref[...])

    spec = pl.BlockSpec((tr, W), lambda i: (i, 0))
    shp = jax.ShapeDtypeStruct((R, W), F32)
    return pl.pallas_call(
        body, name=name, grid=(R // tr,), in_specs=[spec] * 4, out_specs=[spec] * 3, out_shape=[shp] * 3,
        compiler_params=_cparams(("arbitrary",)),
    )(w, g, m, v)


def _adamw_reduce(w, land, m, v, name):
    R, W = w.shape
    tr = _row_tile(R, W, 1 << 20)

    def body(w_ref, l_ref, m_ref, v_ref, g_ref, d_ref, mo_ref, vo_ref):
        g = l_ref[0].astype(F32)
        for d in range(1, N_DEV):
            g = g + l_ref[d].astype(F32)
        g_ref[...] = g
        d_ref[...], mo_ref[...], vo_ref[...] = _adam_update(w_ref[...], g, m_ref[...], v_ref[...])

    spec = pl.BlockSpec((tr, W), lambda i: (i, 0))
    lspec = pl.BlockSpec((N_DEV, tr, W), lambda i: (0, i, 0))
    shp = jax.ShapeDtypeStruct((R, W), F32)
    return pl.pallas_call(
        body, name=name, grid=(R // tr,), in_specs=[spec, lspec, spec, spec], out_specs=[spec] * 4,
        out_shape=[shp] * 4, compiler_params=_cparams(("arbitrary",)),
    )(w, land, m, v)


def _to_zp(w):
    a, kv = D_ATTN, D_KV
    return jnp.concatenate([w[:, :a], w[:, a + 2 * kv:], w[:, a:a + 2 * kv]], axis=1)


def _gelu(v):
    c = math.sqrt(2.0 / math.pi)
    return 0.5 * v * (1.0 + jnp.tanh(c * (v + 0.044715 * v * v * v)))


def _gelu_grad(v):
    c = math.sqrt(2.0 / math.pi)
    t = jnp.tanh(c * (v + 0.044715 * v * v * v))
    return 0.5 * (1.0 + t) + 0.5 * v * (1.0 - t * t) * c * (1.0 + 3.0 * 0.044715 * v * v)


def kernel(x, p, norm_mix, w_in, q_norm, k_norm, ssm_a_re, ssm_a_im, ssm_log_dt, ssm_b_re, ssm_b_im, ssm_c_re, ssm_c_im, ssm_d, w_glu, b_glu, w_out, norm_ple, w_ple_gate, w_ple_proj, norm_final, loss_target, m_norm_mix, m_w_in, m_q_norm, m_k_norm, m_ssm_a_re, m_ssm_a_im, m_ssm_log_dt, m_ssm_b_re, m_ssm_b_im, m_ssm_c_re, m_ssm_c_im, m_ssm_d, m_w_glu, m_b_glu, m_w_out, m_norm_ple, m_w_ple_gate, m_w_ple_proj, m_norm_final, v_norm_mix, v_w_in, v_q_norm, v_k_norm, v_ssm_a_re, v_ssm_a_im, v_ssm_log_dt, v_ssm_b_re, v_ssm_b_im, v_ssm_c_re, v_ssm_c_im, v_ssm_d, v_w_glu, v_b_glu, v_w_out, v_norm_ple, v_w_ple_gate, v_w_ple_proj, v_norm_final):
    L, D = x.shape[1], x.shape[2]
    D_SSM = ssm_d.shape[1]
    G = D_SSM // SSM_H
    n_slab = D_SSM // SLAB
    n_in = w_in.shape[2]
    D_IN = n_in * N_DEV
    n_pp = w_ple_proj.shape[2]
    n_glu = w_glu.shape[2]
    xs = x[0]
    ps = p[0, 0]
    tgt = loss_target[0]

    (win3,) = _all_gather([w_in[0].astype(BF16)], "gather_w_in")
    win_p = _to_zp(win3.transpose(1, 0, 2).reshape(D, D_IN))
    later_weights = _Carry("gather", [w_glu[0].astype(BF16), w_out[0].astype(BF16), w_ple_gate[0].astype(BF16),
                                      w_ple_proj[0].astype(BF16)])

    ssm_prm = (ssm_a_re[0], ssm_a_im[0], ssm_log_dt[0].reshape(2, G, 1),
               ssm_b_re[0].transpose(0, 1, 3, 2), ssm_b_im[0].transpose(0, 1, 3, 2), ssm_c_re[0], ssm_c_im[0])

    cos, sin = _rope_tables(L)
    hn = _norm_in(xs, norm_mix, "norm_mix")
    z = _mm(hn, win_p, "nn", "in_proj")
    qr, kr, vb, kt = _qkv_prep(z, cos, sin, q_norm, k_norm)
    o, lse, (wglu3, wout3, wpg3, wpp3) = _attn_fwd(qr, kr, vb, later_weights)
    wout = wout3.reshape(-1, D)
    wpg = wpg3.reshape(-1, D)
    u_off = 2 * D_ATTN
    u_perm = _seg_perm(z[:, u_off:u_off + D_SSM])
    ys = _seg_unperm(_ssm_fwd(u_perm, ssm_prm, ssm_d))

    tm = _pick(L, (256,))

    def gelu_body(y_ref, o_ref):
        o_ref[...] = _gelu(y_ref[...]).astype(BF16)

    (gy,) = _rowcall(gelu_body, "gelu", L, tm, [(ys, _rspec(tm, D_SSM))], [(D_SSM, BF16)])
    glu = _mm(gy, wglu3, "nn", "glu_proj", bias=b_glu, b_blk=True)

    def mix_body(o_ref, ga_ref, gla_ref, glb_ref, gs_ref, cat_ref):
        ga, gs = ga_ref[...], gs_ref[...]
        cat_ref[:, :D_ATTN] = (o_ref[...] * ga * _sigmoid(ga)).astype(BF16)
        cat_ref[:, D_ATTN:] = (gla_ref[...] * _sigmoid(glb_ref[...]) * gs * _sigmoid(gs)).astype(BF16)

    (cat,) = _rowcall(mix_body, "mix", L, tm,
                      [(o, _rspec(tm, D_ATTN)), (z, _rspec(tm, D_ATTN, 1)), (glu, _rspec(tm, D_SSM, 0)),
                       (glu, _rspec(tm, D_SSM, 1)), (z, _rspec(tm, D_SSM, 3))], [(D_ATTN + D_SSM, BF16)])
    h1 = _mm(cat, wout, "nn", "out_proj", add=xs)
    n2 = _norm_in(h1, norm_ple, "norm_ple")
    gpre = _mm(n2, wpg, "nn", "ple_gate")
    pb = ps.astype(BF16)
    pp = _mm(pb, wpp3, "nn", "ple_proj", b_blk=True)

    nf = norm_final.reshape(1, D)

    def tail_body(h1_ref, gp_ref, pp_ref, t_ref, g_ref, dh2_ref, dpp_ref, dsg_ref, loss_ref, dg_ref):
        gate = _sigmoid(gp_ref[...])
        ppv = pp_ref[...]
        h2 = h1_ref[...] + gate * ppv
        r = _rms(h2)
        hh = h2 * r
        err = hh * g_ref[...] - t_ref[...]
        _acc(loss_ref, jnp.broadcast_to(0.5 * jnp.sum(jnp.mean(err * err, axis=-1, keepdims=True)), loss_ref.shape))
        dy = err * (1.0 / D)
        _acc(dg_ref, _colsum(dy * hh))
        dh2 = _rms_bwd(dy, hh, r, g_ref[...])
        dh2_ref[...] = dh2
        dpp_ref[...] = (dh2 * gate).astype(BF16)
        dsg_ref[...] = (dh2 * ppv * gate * (1.0 - gate)).astype(BF16)

    dh2, dpp, dsg, loss_acc, d_nf = _rowcall(
        tail_body, "tail", L, tm,
        [(h1, _rspec(tm, D)), (gpre, _rspec(tm, D)), (pp, _rspec(tm, D)), (tgt, _rspec(tm, D)), (nf, _fspec(nf.shape))],
        [(D, F32), (D, BF16), (D, BF16)], [(1, LANES), (1, D)])
    loss = lax.psum(loss_acc[0, 0], ("x", "y", "c"))

    g_wpp3 = _mm(pb, dpp, "tn", "d_ple_proj", out_dtype=BF16, out_blk=n_pp)
    g_wpg = _mm(n2, dsg, "tn", "d_ple_gate", out_dtype=BF16)
    dn2 = _mm(dsg, wpg, "nt", "d_norm_ple_in")

    def ple_bwd_body(h1_ref, dn_ref, dh2_ref, g_ref, dh1_ref, dh1b_ref, dg_ref):
        h1v = h1_ref[...]
        r = _rms(h1v)
        hh = h1v * r
        dn = dn_ref[...]
        _acc(dg_ref, _colsum(dn * hh))
        dh1 = dh2_ref[...] + _rms_bwd(dn, hh, r, g_ref[...])
        dh1_ref[...] = dh1
        dh1b_ref[...] = dh1.astype(BF16)

    dh1, dh1b, d_nple = _rowcall(
        ple_bwd_body, "ple_bwd", L, tm,
        [(h1, _rspec(tm, D)), (dn2, _rspec(tm, D)), (dh2, _rspec(tm, D)), (norm_ple, _fspec(norm_ple.shape))],
        [(D, F32), (D, BF16)], [(1, D)])

    dcat = _mm(dh1b, wout, "nt", "d_cat")
    g_wout = _mm(cat, dh1b, "tn", "d_out_proj", out_dtype=BF16)

    def mix_bwd_body(dca_ref, dcs_ref, o_ref, ga_ref, gla_ref, glb_ref, gs_ref,
                     do_ref, dga_ref, dgs_ref, dglu_ref, db_ref):
        dca, dcs, ga, gs = dca_ref[...], dcs_ref[...], ga_ref[...], gs_ref[...]
        sa, ss, sb = _sigmoid(ga), _sigmoid(gs), _sigmoid(glb_ref[...])
        gla = gla_ref[...]
        do_ref[...] = (dca * ga * sa).astype(BF16)
        dga_ref[...] = (dca * o_ref[...] * sa * (1.0 + ga * (1.0 - sa))).astype(BF16)
        dgs_ref[...] = (dcs * gla * sb * ss * (1.0 + gs * (1.0 - ss))).astype(BF16)
        dy2 = dcs * gs * ss
        da, db = dy2 * sb, dy2 * gla * sb * (1.0 - sb)
        dglu_ref[:, :D_SSM] = da.astype(BF16)
        dglu_ref[:, D_SSM:] = db.astype(BF16)
        _acc(db_ref, jnp.concatenate([_colsum(da), _colsum(db)], axis=-1))

    do, dga, dgs, dglu, g_bglu = _rowcall(
        mix_bwd_body, "mix_bwd", L, tm,
        [(dcat, _rspec(tm, D_ATTN, 0)), (dcat, _rspec(tm, D_SSM, 1)), (o, _rspec(tm, D_ATTN)),
         (z, _rspec(tm, D_ATTN, 1)), (glu, _rspec(tm, D_SSM, 0)), (glu, _rspec(tm, D_SSM, 1)),
         (z, _rspec(tm, D_SSM, 3))],
        [(D_ATTN, BF16), (D_ATTN, BF16), (D_SSM, BF16), (2 * D_SSM, BF16)], [(1, 2 * D_SSM)])

    g_wglu3 = _mm(gy, dglu, "tn", "d_glu_proj", out_dtype=BF16, out_blk=n_glu)
    dgy = _mm(dglu, wglu3, "nt", "d_gelu_out", b_blk=True)

    def gelu_bwd_body(dg_ref, y_ref, o_ref):
        o_ref[...] = dg_ref[...] * _gelu_grad(y_ref[...])

    (dys,) = _rowcall(gelu_bwd_body, "gelu_bwd", L, tm,
                      [(dgy, _rspec(tm, D_SSM)), (ys, _rspec(tm, D_SSM))], [(D_SSM, F32)])
    du_perm, pg = _ssm_bwd(u_perm, _seg_perm(dys), ssm_prm, ssm_d)
    du = _seg_unperm(du_perm)

    dqs, dkr, dvv, (l_wglu, l_wout, l_wpg, l_wpp) = _attn_bwd(
        qr, kr, vb, kt, do, o, lse,
        _Carry("a2a", [g_wglu3, g_wout.reshape(N_DEV, -1, D), g_wpg.reshape(N_DEV, -1, D), g_wpp3]))

    scale = HEAD_DIM ** -0.5
    kblk = 4 * D_ATTN // D_KV
    tmq = _pick(L, (512, 256))

    def qkv_bwd_body(dq_ref, dk_ref, dv_ref, q_ref, k_ref, cos_ref, sin_ref, qn_ref, kn_ref,
                     dqo_ref, dko_ref, dvo_ref, dqn_ref, dkn_ref):
        c, s = cos_ref[...], sin_ref[...]

        def head(g, xh, w):
            dn = g * c + _partner(g * s)
            r = _rms(xh)
            xhat = xh * r
            return _rms_bwd(dn, xhat, r, w), _colsum(dn * xhat)

        dqn = jnp.zeros((1, HEAD_DIM), F32)
        for h in range(N_HEADS):
            sl = slice(h * HEAD_DIM, (h + 1) * HEAD_DIM)
            dx, dw = head(dq_ref[:, sl] * scale, q_ref[:, sl], qn_ref[...])
            dqo_ref[:, sl] = dx.astype(BF16)
            dqn = dqn + dw
        dkn = jnp.zeros((1, HEAD_DIM), F32)
        for h in range(N_KV):
            sl = slice(h * HEAD_DIM, (h + 1) * HEAD_DIM)
            dx, dw = head(dk_ref[:, sl], k_ref[:, sl], kn_ref[...])
            dko_ref[:, sl] = dx.astype(BF16)
            dkn = dkn + dw
        dvo_ref[...] = dv_ref[...].astype(BF16)
        _acc(dqn_ref, dqn)
        _acc(dkn_ref, dkn)

    dq, dk, dv, g_qn, g_kn = _rowcall(
        qkv_bwd_body, "qkv_bwd", L, tmq,
        [(dqs, _rspec(tmq, D_ATTN)), (dkr, _rspec(tmq, D_KV)), (dvv, _rspec(tmq, D_KV)),
         (z, _rspec(tmq, D_ATTN, 0)), (z, _rspec(tmq, D_KV, kblk)), (cos, _rspec(tmq, HEAD_DIM)),
         (sin, _rspec(tmq, HEAD_DIM)), (q_norm, _fspec(q_norm.shape)), (k_norm, _fspec(k_norm.shape))],
        [(D_ATTN, BF16), (D_KV, BF16), (D_KV, BF16)], [(1, HEAD_DIM), (1, HEAD_DIM)])

    dz3 = (jnp.concatenate([dq, dk, dv, dga, du.astype(BF16), dgs], axis=1)
           .reshape(L, N_DEV, n_in).transpose(1, 0, 2))
    g_win3 = _mm(hn, dz3, "tn", "d_in_proj", out_dtype=BF16, b_blk=True, out_blk=n_in)
    pg_send = pg.reshape(N_DEV, (n_slab // N_DEV) * 2 * PG_ROWS, SLAB_S)
    dhn, (l_win, l_pg) = _mm(dz3, win3, "nt", "d_norm_mix_in", a_blk=True, b_blk=True,
                             carry=_Carry("a2a", [g_win3, pg_send]))

    def in_bwd_body(x_ref, dn_ref, dh1_ref, g_ref, dx_ref, dg_ref):
        xv = x_ref[...]
        r = _rms(xv)
        hh = xv * r
        dn = dn_ref[...]
        _acc(dg_ref, _colsum(dn * hh))
        dx_ref[...] = dh1_ref[...] + _rms_bwd(dn, hh, r, g_ref[...])

    grad_x, g_nmix = _rowcall(
        in_bwd_body, "in_bwd", L, tm,
        [(xs, _rspec(tm, D)), (dhn, _rspec(tm, D)), (dh1, _rspec(tm, D)), (norm_mix, _fspec(norm_mix.shape))],
        [(D, F32)], [(1, D)])

    tiny_parts = [g_nmix, g_bglu, d_nple, d_nf, g_qn, g_kn]
    tiny_flat = jnp.concatenate([t.reshape(-1) for t in tiny_parts])
    tiny_rows = -(-tiny_flat.shape[0] // (8 * LANES)) * 8
    tiny = jnp.pad(tiny_flat, (0, tiny_rows * LANES - tiny_flat.shape[0])).reshape(tiny_rows, LANES)
    pg_sum = _sum_blocks(l_pg, "sum_ssm_grads")
    pg_all, tiny_all = _all_gather([pg_sum, tiny], "gather_small_grads")
    (g_bt_re, g_bt_im, g_c_re, g_c_im, g_a_re, g_a_im, g_ldt, g_skip) = _ssm_param_grads(
        pg_all.reshape(n_slab, 2, PG_ROWS, SLAB_S), ssm_prm)
    tiny_sum = _sum_blocks(tiny_all, "sum_tiny_grads").reshape(-1)
    tiny_grads, off = [], 0
    for t in tiny_parts:
        tiny_grads.append(tiny_sum[off:off + t.size].reshape(t.shape))
        off += t.size
    r_nmix, r_bglu, r_nple, r_nf, r_qn, r_kn = tiny_grads

    grads, deltas, new_ms, new_vs = {}, {}, {}, {}
    big = [("w_in", w_in, l_win, m_w_in, v_w_in), ("w_glu", w_glu, l_wglu, m_w_glu, v_w_glu),
           ("w_out", w_out, l_wout, m_w_out, v_w_out), ("w_ple_gate", w_ple_gate, l_wpg, m_w_ple_gate, v_w_ple_gate),
           ("w_ple_proj", w_ple_proj, l_wpp, m_w_ple_proj, v_w_ple_proj)]
    for name, w, ld, m, v in big:
        shp = w.shape
        outs = _adamw_reduce(w[0], ld, m[0], v[0], "adamw_" + name)
        grads[name], deltas[name], new_ms[name], new_vs[name] = [t.reshape(shp) for t in outs]
    small = [("norm_mix", norm_mix, r_nmix, m_norm_mix, v_norm_mix, (1, D)),
             ("q_norm", q_norm, r_qn, m_q_norm, v_q_norm, (1, HEAD_DIM)),
             ("k_norm", k_norm, r_kn, m_k_norm, v_k_norm, (1, HEAD_DIM)),
             ("ssm_a_re", ssm_a_re, g_a_re, m_ssm_a_re, v_ssm_a_re, (2 * G, SSM_P)),
             ("ssm_a_im", ssm_a_im, g_a_im, m_ssm_a_im, v_ssm_a_im, (2 * G, SSM_P)),
             ("ssm_log_dt", ssm_log_dt, g_ldt, m_ssm_log_dt, v_ssm_log_dt, (2, G)),
             ("ssm_b_re", ssm_b_re, g_bt_re.transpose(0, 1, 3, 2), m_ssm_b_re, v_ssm_b_re, (2 * G * SSM_P, SSM_H)),
             ("ssm_b_im", ssm_b_im, g_bt_im.transpose(0, 1, 3, 2), m_ssm_b_im, v_ssm_b_im, (2 * G * SSM_P, SSM_H)),
             ("ssm_c_re", ssm_c_re, g_c_re, m_ssm_c_re, v_ssm_c_re, (2 * G * SSM_H, SSM_P)),
             ("ssm_c_im", ssm_c_im, g_c_im, m_ssm_c_im, v_ssm_c_im, (2 * G * SSM_H, SSM_P)),
             ("ssm_d", ssm_d, g_skip, m_ssm_d, v_ssm_d, (1, D_SSM)),
             ("b_glu", b_glu, r_bglu, m_b_glu, v_b_glu, (1, 2 * D_SSM)),
             ("norm_ple", norm_ple, r_nple, m_norm_ple, v_norm_ple, (1, D)),
             ("norm_final", norm_final, r_nf, m_norm_final, v_norm_final, (1, D))]
    for name, w, g, m, v, s2 in small:
        shp = w.shape
        outs = _adamw(w.reshape(s2), g.reshape(s2), m.reshape(s2), v.reshape(s2), "adamw_" + name)
        grads[name] = g.reshape(shp)
        deltas[name], new_ms[name], new_vs[name] = [t.reshape(shp) for t in outs]

    order = ["norm_mix", "w_in", "q_norm", "k_norm", "ssm_a_re", "ssm_a_im", "ssm_log_dt", "ssm_b_re", "ssm_b_im",
             "ssm_c_re", "ssm_c_im", "ssm_d", "w_glu", "b_glu", "w_out", "norm_ple", "w_ple_gate", "w_ple_proj",
             "norm_final"]
    return (loss, grad_x[None], *[grads[k] for k in order], *[deltas[k] for k in order],
            *[new_ms[k] for k in order], *[new_vs[k] for k in order])
```

```python
import functools
import math

import numpy as np
import jax
import jax.numpy as jnp
from jax import lax
from jax.experimental import pallas as pl
from jax.experimental.pallas import tpu as pltpu

F32 = jnp.float32
BF16 = jnp.bfloat16

N_DEV = 8
EPS = 1e-6
GRID_W = 64
ROPE_THETA = 10000.0
HEAD_DIM = 128
N_HEADS = 8
N_KV = 2
REP = N_HEADS // N_KV
D_ATTN = N_HEADS * HEAD_DIM
D_KV = N_KV * HEAD_DIM
SSM_H = 16
SSM_P = 64
SLAB = 128
SLAB_G = SLAB // SSM_H
SLAB_S = SLAB_G * SSM_P
SEG = 8
LANES = 128
PG_ROWS = 72
VMEM_LIMIT = 48 << 20

ADAM_LR = 0.001
ADAM_B1 = 0.9
ADAM_B2 = 0.999
ADAM_EPS = 1e-08
ADAM_WD = 0.01
ADAM_STEP = 10


def _pick(n, cands):
    for c in cands:
        if n % c == 0:
            return c
    return n


def _cparams(sem, vmem=VMEM_LIMIT):
    return pltpu.CompilerParams(dimension_semantics=sem, vmem_limit_bytes=vmem)


class _Carry:
    def __init__(self, kind, xs):
        self.kind, self.xs, self.n = kind, list(xs), len(xs)
        lead = (N_DEV,) if kind == "gather" else ()
        self.out_shape = [jax.ShapeDtypeStruct(lead + v.shape, v.dtype) for v in xs]
        self.specs = [pl.BlockSpec(memory_space=pl.ANY)] * self.n
        self.scratch = [pltpu.SemaphoreType.DMA((self.n, N_DEV - 1)), pltpu.SemaphoreType.DMA((self.n, N_DEV - 1)),
                        pltpu.SemaphoreType.DMA((self.n,))]

    def _copies(self, x_refs, out_refs, sems):
        send_sems, recv_sems, local_sems = sems
        x, y, c = lax.axis_index("x"), lax.axis_index("y"), lax.axis_index("c")
        me = _dev_index((x, y, c))
        mine, sends, arrivals = [], [], []
        for a in range(self.n):
            src_mine = x_refs[a] if self.kind == "gather" else x_refs[a].at[me]
            mine.append(pltpu.make_async_copy(src_mine, out_refs[a].at[me], local_sems.at[a]))
            for k in range(1, N_DEV):
                peer = _peer(k, x, y, c)
                src = x_refs[a] if self.kind == "gather" else x_refs[a].at[_dev_index(peer)]
                sends.append(pltpu.make_async_remote_copy(
                    src_ref=src, dst_ref=out_refs[a].at[me], send_sem=send_sems.at[a, k - 1],
                    recv_sem=recv_sems.at[a, k - 1], device_id=peer, device_id_type=pl.DeviceIdType.MESH))
                land = out_refs[a].at[_dev_index(peer)]
                arrivals.append(pltpu.make_async_remote_copy(
                    src_ref=land, dst_ref=land, send_sem=send_sems.at[a, k - 1],
                    recv_sem=recv_sems.at[a, k - 1], device_id=peer, device_id_type=pl.DeviceIdType.MESH))
        return mine, sends, arrivals

    def start(self, x_refs, out_refs, sems):
        mine, sends, _ = self._copies(x_refs, out_refs, sems)
        for cp in mine + sends:
            cp.start()

    def wait(self, x_refs, out_refs, sems):
        mine, sends, arrivals = self._copies(x_refs, out_refs, sems)
        for cp in arrivals:
            cp.wait_recv()
        for cp in sends:
            cp.wait_send()
        for cp in mine:
            cp.wait()


def _grid_edges(grid):
    first = functools.reduce(lambda p, q: p & q, [pl.program_id(d) == 0 for d in range(len(grid))])
    last = functools.reduce(lambda p, q: p & q, [pl.program_id(d) == g - 1 for d, g in enumerate(grid)])
    return first, last


def _mm(a, b, mode, name, out_dtype=F32, add=None, bias=None, a_blk=False, b_blk=False, out_blk=0, carry=None,
        n_tiles=None):
    w = b.shape[2] if b_blk else out_blk
    if mode == "nn":
        M, K = a.shape
        N = b.shape[0] * w if b_blk else b.shape[1]
    elif mode == "nt":
        M = a.shape[1] if a_blk else a.shape[0]
        N = b.shape[1] if b_blk else b.shape[0]
        K = b.shape[0] * w if b_blk else b.shape[1]
    else:
        K, M = a.shape
        N = b.shape[0] * w if b_blk else b.shape[1]
    tm = _pick(M, (1024, 768, 512, 256))
    tn = _pick(N, (1024, 768, 512, 256))
    tk = K if (mode != "tn" and K <= 2048) else _pick(K, (1024, 768, 512, 256))
    perm = lambda j: j
    if n_tiles:
        tn, perm = n_tiles
    if mode == "nt" and b_blk:
        tk = w
    elif b_blk or out_blk:
        tn = w
    nk = K // tk
    grid = (M // tm, N // tn, nk)
    if mode == "nn":
        a_spec = pl.BlockSpec((tm, tk), lambda i, j, k: (i, k))
        b_spec = (pl.BlockSpec((1, tk, tn), lambda i, j, k: (j, k, 0)) if b_blk
                  else pl.BlockSpec((tk, tn), lambda i, j, k: (k, j)))
        dims = (((1,), (0,)), ((), ()))
    elif mode == "nt":
        a_spec = (pl.BlockSpec((1, tm, tk), lambda i, j, k: (k, i, 0)) if a_blk
                  else pl.BlockSpec((tm, tk), lambda i, j, k: (i, k)))
        b_spec = (pl.BlockSpec((1, tn, tk), lambda i, j, k: (k, j, 0)) if b_blk
                  else pl.BlockSpec((tn, tk), lambda i, j, k: (perm(j), k)))
        dims = (((1,), (1,)), ((), ()))
    else:
        a_spec = pl.BlockSpec((tk, tm), lambda i, j, k: (k, i))
        b_spec = (pl.BlockSpec((1, tk, tn), lambda i, j, k: (j, k, 0)) if b_blk
                  else pl.BlockSpec((tk, tn), lambda i, j, k: (k, j)))
        dims = (((0,), (0,)), ((), ()))
    if out_blk:
        out_spec = pl.BlockSpec((1, tm, tn), lambda i, j, k: (j, i, 0))
        out_shape = jax.ShapeDtypeStruct((N // tn, M, tn), out_dtype)
    else:
        out_spec = pl.BlockSpec((tm, tn), lambda i, j, k: (i, j))
        out_shape = jax.ShapeDtypeStruct((M, N), out_dtype)
    extras, extra_specs = [], []
    if add is not None:
        extras.append(add)
        extra_specs.append(pl.BlockSpec((tm, tn), lambda i, j, k: (i, j)))
    if bias is not None:
        extras.append(bias)
        extra_specs.append(pl.BlockSpec((1, tn), lambda i, j, k: (0, j)))

    n_ex = len(extras)
    nc = carry.n if carry else 0

    def body(a_ref, b_ref, *rest):
        ex_refs, cx = rest[:n_ex], rest[n_ex:n_ex + nc]
        o_ref, cout = rest[n_ex + nc], rest[n_ex + nc + 1:n_ex + 2 * nc + 1]
        tail = rest[n_ex + 2 * nc + 1:]
        sems = tail[:3] if carry else ()
        first, last = _grid_edges(grid)
        if carry:
            @pl.when(first)
            def _():
                carry.start(cx, cout, sems)

        def product():
            av = a_ref[0] if a_blk else a_ref[...]
            bv = b_ref[0] if b_blk else b_ref[...]
            return lax.dot_general(av, bv, dims, preferred_element_type=F32)

        def finish(out):
            for r in ex_refs:
                out = out + r[...]
            if out_blk:
                o_ref[0] = out.astype(out_dtype)
            else:
                o_ref[...] = out.astype(out_dtype)

        if nk == 1:
            finish(product())
        else:
            acc_ref = tail[-1]
            k = pl.program_id(2)

            @pl.when(k == 0)
            def _():
                acc_ref[...] = jnp.zeros_like(acc_ref)

            acc_ref[...] += product()

            @pl.when(k == nk - 1)
            def _():
                finish(acc_ref[...])

        if carry:
            @pl.when(last)
            def _():
                carry.wait(cx, cout, sems)

    scratch = (carry.scratch if carry else []) + ([pltpu.VMEM((tm, tn), F32)] if nk > 1 else [])
    outs = pl.pallas_call(
        body, name=name, grid=grid,
        in_specs=[a_spec, b_spec] + extra_specs + (carry.specs if carry else []),
        out_specs=[out_spec] + (carry.specs if carry else []),
        out_shape=[out_shape] + (carry.out_shape if carry else []),
        scratch_shapes=scratch,
        compiler_params=_cparams(("arbitrary", "arbitrary", "arbitrary")),
    )(a, b, *extras, *(carry.xs if carry else []))
    return (outs[0], outs[1:]) if carry else outs[0]


def _rspec(tm, w, cb=0):
    return pl.BlockSpec((tm, w), lambda i: (i, cb))


def _fspec(shape):
    nd = len(shape)
    return pl.BlockSpec(shape, lambda i: (0,) * nd)


def _rowcall(body, name, L, tm, ins, row_outs, acc_outs=()):
    out_shape = [jax.ShapeDtypeStruct((L, w), dt) for w, dt in row_outs]
    out_shape += [jax.ShapeDtypeStruct(s, F32) for s in acc_outs]
    out_specs = [_rspec(tm, w) for w, _ in row_outs] + [_fspec(s) for s in acc_outs]
    return pl.pallas_call(
        body, name=name, grid=(L // tm,),
        in_specs=[s for _, s in ins], out_specs=out_specs, out_shape=out_shape,
        compiler_params=_cparams(("arbitrary",)),
    )(*[a for a, _ in ins])


def _acc(ref, val):
    @pl.when(pl.program_id(0) == 0)
    def _():
        ref[...] = jnp.zeros_like(ref)
    ref[...] += val


def _colsum(v):
    return jnp.sum(v, axis=0, keepdims=True)


def _rms(xv):
    return lax.rsqrt(jnp.mean(xv * xv, axis=-1, keepdims=True) + EPS)


def _rms_bwd(dn, xhat, r, g):
    dng = dn * g
    return r * (dng - xhat * jnp.mean(dng * xhat, axis=-1, keepdims=True))


def _sigmoid(v):
    return jax.nn.sigmoid(v)


def _partner(v):
    w = v.shape[-1]
    lane = lax.broadcasted_iota(jnp.int32, v.shape, v.ndim - 1)
    first_half = (lane % 64) < 32
    return jnp.where(first_half, pltpu.roll(v, w - 32, axis=v.ndim - 1), pltpu.roll(v, 32, axis=v.ndim - 1))


def _norm_in(x, g, name):
    L, D = x.shape
    tm = _pick(L, (512, 256))

    def body(x_ref, g_ref, o_ref):
        xv = x_ref[...]
        o_ref[...] = (xv * _rms(xv) * g_ref[...]).astype(BF16)

    return _rowcall(body, name, L, tm, [(x, _rspec(tm, D)), (g, _fspec(g.shape))], [(D, BF16)])[0]


def _rope_tables(L):
    t = np.arange(L)
    rows = (t // GRID_W).astype(np.float32)
    cols = (t % GRID_W).astype(np.float32)
    n_freq = HEAD_DIM // 4
    inv_freq = np.float32(ROPE_THETA) ** (-np.arange(n_freq, dtype=np.float32) / np.float32(n_freq))
    ar = (rows[:, None] * inv_freq[None, :]).astype(np.float32).astype(np.float64)
    ac = (cols[:, None] * inv_freq[None, :]).astype(np.float32).astype(np.float64)
    cos = np.concatenate([np.cos(ar), np.cos(ar), np.cos(ac), np.cos(ac)], axis=-1).astype(np.float32)
    sin = np.concatenate([-np.sin(ar), np.sin(ar), -np.sin(ac), np.sin(ac)], axis=-1).astype(np.float32)
    return jnp.asarray(cos), jnp.asarray(sin)


def _qkv_prep(z, cos, sin, qn, kn):
    L = z.shape[0]
    tm = _pick(L, (512, 256))
    scale = HEAD_DIM ** -0.5
    kblk = 4 * D_ATTN // D_KV

    def body(q_ref, k_ref, v_ref, cos_ref, sin_ref, qn_ref, kn_ref, qo_ref, ko_ref, vo_ref, kt_ref):
        c, s = cos_ref[...], sin_ref[...]

        def head(xh, w):
            n = xh * _rms(xh) * w
            return n * c + _partner(n) * s

        for h in range(N_HEADS):
            sl = slice(h * HEAD_DIM, (h + 1) * HEAD_DIM)
            qo_ref[:, sl] = (head(q_ref[:, sl], qn_ref[...]) * scale).astype(BF16)
        for h in range(N_KV):
            sl = slice(h * HEAD_DIM, (h + 1) * HEAD_DIM)
            kr = head(k_ref[:, sl], kn_ref[...])
            ko_ref[:, sl] = kr.astype(BF16)
            kt_ref[sl, :] = kr.T.astype(BF16)
        vo_ref[...] = v_ref[...].astype(BF16)

    return pl.pallas_call(
        body, name="qkv_prep", grid=(L // tm,),
        in_specs=[_rspec(tm, D_ATTN, 0), _rspec(tm, D_KV, kblk), _rspec(tm, D_KV, kblk + 1),
                  _rspec(tm, HEAD_DIM), _rspec(tm, HEAD_DIM), _fspec(qn.shape), _fspec(kn.shape)],
        out_specs=[_rspec(tm, D_ATTN), _rspec(tm, D_KV), _rspec(tm, D_KV),
                   pl.BlockSpec((D_KV, tm), lambda i: (0, i))],
        out_shape=[jax.ShapeDtypeStruct((L, D_ATTN), BF16), jax.ShapeDtypeStruct((L, D_KV), BF16),
                   jax.ShapeDtypeStruct((L, D_KV), BF16), jax.ShapeDtypeStruct((D_KV, L), BF16)],
        compiler_params=_cparams(("arbitrary",)),
    )(z, z, z, cos, sin, qn, kn)


def _col_to_row(col):
    n = col.shape[0]
    eye = lax.broadcasted_iota(jnp.int32, (n, n), 0) == lax.broadcasted_iota(jnp.int32, (n, n), 1)
    return jnp.sum(jnp.where(eye, col, 0.0), axis=0, keepdims=True)


def _attn_fwd(q, k, v, carry=None):
    L = q.shape[0]
    tq = _pick(L, (256, 128))
    grid = (N_HEADS, L // tq)
    nc = carry.n if carry else 0

    def body(q_ref, k_ref, v_ref, *rest):
        cx, (o_ref, lse_ref) = rest[:nc], rest[nc:nc + 2]
        cout, sems = rest[nc + 2:2 * nc + 2], rest[2 * nc + 2:]
        first, last = _grid_edges(grid)
        if carry:
            @pl.when(first)
            def _():
                carry.start(cx, cout, sems)

        s = lax.dot_general(q_ref[...], k_ref[...], (((1,), (1,)), ((), ())), preferred_element_type=F32)
        m = jnp.max(s, axis=-1, keepdims=True)
        e = jnp.exp(s - m)
        l = jnp.sum(e, axis=-1, keepdims=True)
        o_ref[...] = jnp.dot(e.astype(BF16), v_ref[...], preferred_element_type=F32) / l
        lse_ref[0] = _col_to_row(m + jnp.log(l))

        if carry:
            @pl.when(last)
            def _():
                carry.wait(cx, cout, sems)

    outs = pl.pallas_call(
        body, name="attn_fwd", grid=grid,
        in_specs=[pl.BlockSpec((tq, HEAD_DIM), lambda h, i: (i, h)),
                  pl.BlockSpec((L, HEAD_DIM), lambda h, i: (0, h // REP)),
                  pl.BlockSpec((L, HEAD_DIM), lambda h, i: (0, h // REP))] + (carry.specs if carry else []),
        out_specs=[pl.BlockSpec((tq, HEAD_DIM), lambda h, i: (i, h)),
                   pl.BlockSpec((1, 1, tq), lambda h, i: (h, 0, i))] + (carry.specs if carry else []),
        out_shape=[jax.ShapeDtypeStruct((L, D_ATTN), F32), jax.ShapeDtypeStruct((N_HEADS, 1, L), F32)]
        + (carry.out_shape if carry else []),
        scratch_shapes=carry.scratch if carry else [],
        compiler_params=_cparams(("arbitrary", "arbitrary")),
    )(q, k, v, *(carry.xs if carry else []))
    return outs[0], outs[1], outs[2:]


def _attn_bwd(q, k, v, kt, do, o, lse, carry=None):
    L = q.shape[0]
    tq = _pick(L, (256, 128))
    kc = _pick(L, (256, 128))
    nt = (((1,), (1,)), ((), ()))
    grid = (N_KV, REP, L // tq)
    nc = carry.n if carry else 0

    def body(q_ref, do_ref, o_ref, lse_ref, k_ref, v_ref, kt_ref, *rest):
        cx, (dq_ref, dk_ref, dv_ref) = rest[:nc], rest[nc:nc + 3]
        cout, sems = rest[nc + 3:2 * nc + 3], rest[2 * nc + 3:]
        first, last = _grid_edges(grid)
        if carry:
            @pl.when(first)
            def _():
                carry.start(cx, cout, sems)

        @pl.when((pl.program_id(1) == 0) & (pl.program_id(2) == 0))
        def _():
            dk_ref[...] = jnp.zeros_like(dk_ref)
            dv_ref[...] = jnp.zeros_like(dv_ref)

        qv, dov = q_ref[...], do_ref[...]
        lse_row = lse_ref[0]
        delta = _col_to_row(jnp.sum(dov.astype(F32) * o_ref[...], axis=-1, keepdims=True))
        dqt = jnp.zeros((HEAD_DIM, tq), F32)
        for c in range(L // kc):
            sl = slice(c * kc, (c + 1) * kc)
            st = lax.dot_general(k_ref[sl, :], qv, nt, preferred_element_type=F32)
            pt = jnp.exp(st - lse_row)
            dpt = lax.dot_general(v_ref[sl, :], dov, nt, preferred_element_type=F32)
            dst = (pt * (dpt - delta)).astype(BF16)
            dv_ref[sl, :] += jnp.dot(pt.astype(BF16), dov, preferred_element_type=F32)
            dk_ref[sl, :] += jnp.dot(dst, qv, preferred_element_type=F32)
            dqt = dqt + jnp.dot(kt_ref[:, sl], dst, preferred_element_type=F32)
        dq_ref[...] = dqt.T

        if carry:
            @pl.when(last)
            def _():
                carry.wait(cx, cout, sems)

    head = lambda g, r, i: (i, g * REP + r)
    outs = pl.pallas_call(
        body, name="attn_bwd", grid=grid,
        in_specs=[pl.BlockSpec((tq, HEAD_DIM), head), pl.BlockSpec((tq, HEAD_DIM), head),
                  pl.BlockSpec((tq, HEAD_DIM), head),
                  pl.BlockSpec((1, 1, tq), lambda g, r, i: (g * REP + r, 0, i)),
                  pl.BlockSpec((L, HEAD_DIM), lambda g, r, i: (0, g)),
                  pl.BlockSpec((L, HEAD_DIM), lambda g, r, i: (0, g)),
                  pl.BlockSpec((HEAD_DIM, L), lambda g, r, i: (g, 0))] + (carry.specs if carry else []),
        out_specs=[pl.BlockSpec((tq, HEAD_DIM), head),
                   pl.BlockSpec((L, HEAD_DIM), lambda g, r, i: (0, g)),
                   pl.BlockSpec((L, HEAD_DIM), lambda g, r, i: (0, g))] + (carry.specs if carry else []),
        out_shape=[jax.ShapeDtypeStruct((L, D_ATTN), F32), jax.ShapeDtypeStruct((L, D_KV), F32),
                   jax.ShapeDtypeStruct((L, D_KV), F32)] + (carry.out_shape if carry else []),
        scratch_shapes=carry.scratch if carry else [],
        compiler_params=_cparams(("arbitrary", "arbitrary", "arbitrary")),
    )(q, do, o, lse, k, v, kt, *(carry.xs if carry else []))
    return outs[0], outs[1], outs[2], outs[3:]


def _seg_perm(a):
    L, C = a.shape
    return a.reshape(SEG, L // SEG, C).transpose(1, 0, 2).reshape(L, C)


def _seg_unperm(a):
    L, C = a.shape
    return a.reshape(L // SEG, SEG, C).transpose(1, 0, 2).reshape(L, C)


def _cmul(ar, ai, br, bi):
    return ar * br - ai * bi, ar * bi + ai * br


def _rows8(rr):
    return pl.ds(pl.multiple_of(rr * SEG, SEG), SEG)


def _seg_scan(xr_ref, xi_ref, ar, ai, reverse, n_rows, visit=None, visit_init=()):
    shape = ar.shape
    zero = jnp.zeros(shape, F32)

    def index(r):
        return (n_rows - 1 - r) if reverse else r

    def ends(r, carry):
        cr, ci = carry
        sl = _rows8(index(r))
        pr, pi = _cmul(ar, ai, cr, ci)
        return pr + xr_ref[sl, :], pi + xi_ref[sl, :]

    er, ei = lax.fori_loop(0, n_rows, ends, (zero, zero))

    pr, pi = ar, ai
    for _ in range(int(math.log2(n_rows))):
        pr, pi = _cmul(pr, pi, pr, pi)
    sub = lax.broadcasted_iota(jnp.int32, shape, 0)
    shift = (SEG - 1) if reverse else 1
    edge = (SEG - 1) if reverse else 0
    inr, ini = zero, zero
    for _ in range(SEG - 1):
        tr, ti = _cmul(pr, pi, inr, ini)
        inr = jnp.where(sub == edge, 0.0, pltpu.roll(tr + er, shift, axis=0))
        ini = jnp.where(sub == edge, 0.0, pltpu.roll(ti + ei, shift, axis=0))

    def step(rr, carry, last):
        cr, ci = carry[:2]
        sl = _rows8(rr)
        pr, pi = _cmul(ar, ai, cr, ci)
        nr, ni = pr + xr_ref[sl, :], pi + xi_ref[sl, :]
        xr_ref[sl, :] = nr
        xi_ref[sl, :] = ni
        acc = visit(rr, nr, ni, carry[2:], last) if visit else ()
        return (nr, ni, *acc)

    carry = lax.fori_loop(0, n_rows - 1, lambda r, c: step(index(r), c, False), (inr, ini, *visit_init))
    carry = step(index(n_rows - 1), carry, True)
    return inr, ini, carry[2:]


def _discretise(a_re, a_im, ldt):
    lr = jnp.minimum(a_re, -1e-4)
    li = a_im
    dt = jnp.exp(ldt)
    mag = jnp.exp(lr * dt)
    lbr = mag * jnp.cos(li * dt)
    lbi = mag * jnp.sin(li * dt)
    den = lr * lr + li * li
    nr = lbr - 1.0
    fr = (nr * lr + lbi * li) / den
    fi = (lbi * lr - nr * li) / den
    return lr, li, dt, lbr, lbi, fr, fi


def _lane_row(v):
    return jnp.concatenate([v[g:g + 1, :] for g in range(v.shape[0])], axis=1)


def _ssm_fill_maps(d, prm, tmp_ref, maps):
    a_re_ref, a_im_ref, ldt_ref, bt_re_ref, bt_im_ref, c_re_ref, c_im_ref = prm
    _, _, _, lbr, lbi, fr, fi = _discretise(a_re_ref[d], a_im_ref[d], ldt_ref[d])

    def fill(dst, piece):
        tmp_ref[...] = jnp.zeros_like(tmp_ref)
        for g in range(SLAB_G):
            tmp_ref[g * SSM_H:(g + 1) * SSM_H, g * SSM_P:(g + 1) * SSM_P] = piece(g)
        dst[...] = tmp_ref[...].astype(BF16)

    wbr, wbi, wcr, wci = maps
    fill(wbr, lambda g: fr[g:g + 1] * bt_re_ref[d, g] - fi[g:g + 1] * bt_im_ref[d, g])
    fill(wbi, lambda g: fr[g:g + 1] * bt_im_ref[d, g] + fi[g:g + 1] * bt_re_ref[d, g])
    fill(wcr, lambda g: c_re_ref[d, g])
    fill(wci, lambda g: c_im_ref[d, g])
    return _lane_row(lbr), _lane_row(lbi)


def _ssm_param_specs():
    pole = pl.BlockSpec((2, SLAB_G, SSM_P), lambda j: (0, j, 0))
    step = pl.BlockSpec((2, SLAB_G, 1), lambda j: (0, j, 0))
    mat = pl.BlockSpec((2, SLAB_G, SSM_H, SSM_P), lambda j: (0, j, 0, 0))
    return [pole, pole, step, mat, mat, mat, mat]


_MAP_SCRATCH = [pltpu.VMEM((SLAB, SLAB_S), F32)] + [pltpu.VMEM((SLAB, SLAB_S), BF16)] * 4
_NT = (((1,), (1,)), ((), ()))


def _ssm_fwd(u, prm, dskip):
    L, C = u.shape
    n_rows = L // SEG
    tc = _pick(L, (512, 256))
    u_spec = pl.BlockSpec((L, SLAB), lambda j: (0, j))
    d_spec = pl.BlockSpec((1, SLAB), lambda j: (0, j))

    def body(u_ref, *rest):
        prm_refs, d_ref, y_ref = rest[:7], rest[7], rest[8]
        tmp_ref, maps, xr_ref, xi_ref = rest[9], rest[10:14], rest[14], rest[15]
        wbr, wbi, wcr, wci = maps
        y_ref[...] = u_ref[...] * d_ref[...]
        for d in range(2):
            lam_r, lam_i = _ssm_fill_maps(d, prm_refs, tmp_ref, maps)

            def inp(c, _):
                sl = pl.ds(pl.multiple_of(c * tc, tc), tc)
                ub = u_ref[sl, :].astype(BF16)
                xr_ref[sl, :] = jnp.dot(ub, wbr[...], preferred_element_type=F32)
                xi_ref[sl, :] = jnp.dot(ub, wbi[...], preferred_element_type=F32)
                return 0

            lax.fori_loop(0, L // tc, inp, 0)
            ar = jnp.broadcast_to(lam_r, (SEG, SLAB_S))
            ai = jnp.broadcast_to(lam_i, (SEG, SLAB_S))
            _seg_scan(xr_ref, xi_ref, ar, ai, d == 1, n_rows)

            def outp(c, _):
                sl = pl.ds(pl.multiple_of(c * tc, tc), tc)
                y_ref[sl, :] += (
                    lax.dot_general(xr_ref[sl, :].astype(BF16), wcr[...], _NT, preferred_element_type=F32)
                    - lax.dot_general(xi_ref[sl, :].astype(BF16), wci[...], _NT, preferred_element_type=F32))
                return 0

            lax.fori_loop(0, L // tc, outp, 0)

    return pl.pallas_call(
        body, name="ssm_fwd", grid=(C // SLAB,),
        in_specs=[u_spec] + _ssm_param_specs() + [d_spec],
        out_specs=u_spec, out_shape=jax.ShapeDtypeStruct((L, C), F32),
        scratch_shapes=_MAP_SCRATCH + [pltpu.VMEM((L, SLAB_S), F32)] * 2,
        compiler_params=_cparams(("arbitrary",)),
    )(u, *prm, dskip)


def _ssm_bwd(u, dy, prm, dskip):
    L, C = u.shape
    n_rows = L // SEG
    n_slab = C // SLAB
    tc = _pick(L, (512, 256))
    u_spec = pl.BlockSpec((L, SLAB), lambda j: (0, j))
    d_spec = pl.BlockSpec((1, SLAB), lambda j: (0, j))
    pg_spec = pl.BlockSpec((1, 2, PG_ROWS, SLAB_S), lambda j: (j, 0, 0, 0))

    def body(u_ref, dy_ref, *rest):
        prm_refs, d_ref, du_ref, pg_ref = rest[:7], rest[7], rest[8], rest[9]
        tmp_ref, maps, acc_ref = rest[10], rest[11:15], rest[15]
        xr_ref, xi_ref, gr_ref, gi_ref = rest[16:20]
        wbr, wbi, wcr, wci = maps
        du_ref[...] = dy_ref[...] * d_ref[...]
        pg_ref[...] = jnp.zeros_like(pg_ref)
        pg_ref[0, 0, 66:67, 0:SLAB] = _colsum(dy_ref[...] * u_ref[...])
        for d in range(2):
            lam_r, lam_i = _ssm_fill_maps(d, prm_refs, tmp_ref, maps)

            def inp(c, _):
                sl = pl.ds(pl.multiple_of(c * tc, tc), tc)
                ub = u_ref[sl, :].astype(BF16)
                dyb = dy_ref[sl, :].astype(BF16)
                xr_ref[sl, :] = jnp.dot(ub, wbr[...], preferred_element_type=F32)
                xi_ref[sl, :] = jnp.dot(ub, wbi[...], preferred_element_type=F32)
                gr_ref[sl, :] = jnp.dot(dyb, wcr[...], preferred_element_type=F32)
                gi_ref[sl, :] = -jnp.dot(dyb, wci[...], preferred_element_type=F32)
                return 0

            lax.fori_loop(0, L // tc, inp, 0)
            ar = jnp.broadcast_to(lam_r, (SEG, SLAB_S))
            ai = jnp.broadcast_to(lam_i, (SEG, SLAB_S))
            inr, ini, _ = _seg_scan(xr_ref, xi_ref, ar, ai, d == 1, n_rows)

            def pole(rr, lr, li, acc, last):
                if last:
                    pr, pi = inr, ini
                else:
                    nb = _rows8(rr + 1 if d == 1 else rr - 1)
                    pr, pi = xr_ref[nb, :], xi_ref[nb, :]
                return acc[0] + lr * pr + li * pi, acc[1] + li * pr - lr * pi

            zero = jnp.zeros((SEG, SLAB_S), F32)
            _, _, (accr, acci) = _seg_scan(gr_ref, gi_ref, ar, -ai, d == 0, n_rows, pole, (zero, zero))
            pg_ref[0, d, 64:65, :] = _colsum(accr)
            pg_ref[0, d, 65:66, :] = _colsum(acci)

            acc_ref[...] = jnp.zeros_like(acc_ref)

            def outp(c, _):
                sl = pl.ds(pl.multiple_of(c * tc, tc), tc)
                lrb, lib = gr_ref[sl, :].astype(BF16), gi_ref[sl, :].astype(BF16)
                du_ref[sl, :] += (lax.dot_general(lrb, wbr[...], _NT, preferred_element_type=F32)
                                  + lax.dot_general(lib, wbi[...], _NT, preferred_element_type=F32))
                ut = u_ref[sl, :].T.astype(BF16)
                dyt = dy_ref[sl, :].T.astype(BF16)
                acc_ref[0] += jnp.dot(ut, lrb, preferred_element_type=F32)
                acc_ref[1] += jnp.dot(ut, lib, preferred_element_type=F32)
                acc_ref[2] += jnp.dot(dyt, xr_ref[sl, :].astype(BF16), preferred_element_type=F32)
                acc_ref[3] -= jnp.dot(dyt, xi_ref[sl, :].astype(BF16), preferred_element_type=F32)
                return 0

            lax.fori_loop(0, L // tc, outp, 0)
            for m in range(4):
                for g in range(SLAB_G):
                    lanes = slice(g * SSM_P, (g + 1) * SSM_P)
                    pg_ref[0, d, m * SSM_H:(m + 1) * SSM_H, lanes] = acc_ref[m, g * SSM_H:(g + 1) * SSM_H, lanes]

    return pl.pallas_call(
        body, name="ssm_bwd", grid=(n_slab,),
        in_specs=[u_spec, u_spec] + _ssm_param_specs() + [d_spec],
        out_specs=[u_spec, pg_spec],
        out_shape=[jax.ShapeDtypeStruct((L, C), F32), jax.ShapeDtypeStruct((n_slab, 2, PG_ROWS, SLAB_S), F32)],
        scratch_shapes=_MAP_SCRATCH + [pltpu.VMEM((4, SLAB, SLAB_S), F32)] + [pltpu.VMEM((L, SLAB_S), F32)] * 4,
        compiler_params=_cparams(("arbitrary",), 60 << 20),
    )(u, dy, *prm, dskip)


def _ssm_param_grads(pg, prm):
    n_slab = pg.shape[0]
    G = n_slab * SLAB_G
    pg_spec = pl.BlockSpec((1, 2, PG_ROWS, SLAB_S), lambda j: (j, 0, 0, 0))
    pole, _, step, mat = _ssm_param_specs()[:4]

    def body(pg_ref, a_re_ref, a_im_ref, ldt_ref, bt_re_ref, bt_im_ref,
             dbr_ref, dbi_ref, dcr_ref, dci_ref, dar_ref, dai_ref, dldt_ref, dd_ref):
        dd_ref[...] = pg_ref[0, 0, 66:67, 0:SLAB]
        for d in range(2):
            a_r = a_re_ref[d]
            lr, li, dt, lbr, lbi, f_r, f_i = _discretise(a_r, a_im_ref[d], ldt_ref[d])
            gfr_rows, gfi_rows, glr_rows, gli_rows = [], [], [], []
            for g in range(SLAB_G):
                lanes = slice(g * SSM_P, (g + 1) * SSM_P)
                gbr, gbi = pg_ref[0, d, 0:SSM_H, lanes], pg_ref[0, d, SSM_H:2 * SSM_H, lanes]
                b_r, b_i = bt_re_ref[d, g], bt_im_ref[d, g]
                fr, fi = f_r[g:g + 1], f_i[g:g + 1]
                dbr_ref[d, g] = fr * gbr + fi * gbi
                dbi_ref[d, g] = fr * gbi - fi * gbr
                gfr_rows.append(_colsum(gbr * b_r + gbi * b_i))
                gfi_rows.append(_colsum(gbi * b_r - gbr * b_i))
                dcr_ref[d, g] = pg_ref[0, d, 2 * SSM_H:3 * SSM_H, lanes]
                dci_ref[d, g] = pg_ref[0, d, 3 * SSM_H:4 * SSM_H, lanes]
                glr_rows.append(pg_ref[0, d, 64:65, lanes])
                gli_rows.append(pg_ref[0, d, 65:66, lanes])
            gfr, gfi = jnp.concatenate(gfr_rows, axis=0), jnp.concatenate(gfi_rows, axis=0)
            glr, gli = jnp.concatenate(glr_rows, axis=0), jnp.concatenate(gli_rows, axis=0)
            den = lr * lr + li * li
            ir, ii = lr / den, -li / den
            tr, ti = _cmul(ir, -ii, gfr, gfi)
            glbr, glbi = glr + tr, gli + ti
            qr, qi = _cmul(f_r, f_i, ir, ii)
            dlr, dli = _cmul(-qr, qi, gfr, gfi)
            zr, zi = _cmul(lbr, -lbi, glbr, glbi)
            dlr = dlr + dt * zr
            dli = dli + dt * zi
            dar_ref[d] = jnp.where(a_r < -1e-4, dlr, jnp.where(a_r == -1e-4, 0.5 * dlr, 0.0))
            dai_ref[d] = dli
            dldt_ref[d] = jnp.sum(lr * zr + li * zi, axis=-1, keepdims=True) * dt

    a_re, a_im, ldt, bt_re, bt_im = prm[:5]
    mshape = jax.ShapeDtypeStruct(bt_re.shape, F32)
    pshape = jax.ShapeDtypeStruct(a_re.shape, F32)
    return pl.pallas_call(
        body, name="ssm_param_grads", grid=(n_slab,),
        in_specs=[pg_spec, pole, pole, step, mat, mat],
        out_specs=[mat, mat, mat, mat, pole, pole, step, pl.BlockSpec((1, SLAB), lambda j: (0, j))],
        out_shape=[mshape, mshape, mshape, mshape, pshape, pshape, jax.ShapeDtypeStruct(ldt.shape, F32),
                   jax.ShapeDtypeStruct((1, n_slab * SLAB), F32)],
        compiler_params=_cparams(("arbitrary",)),
    )(pg, a_re, a_im, ldt, bt_re, bt_im)


def _peer(k, x, y, c):
    return (1 - x if k & 4 else x, 1 - y if k & 2 else y, 1 - c if k & 1 else c)


def _dev_index(pos):
    return 4 * pos[0] + 2 * pos[1] + pos[2]


def _all_gather(xs, name):
    n = len(xs)
    any_spec = pl.BlockSpec(memory_space=pl.ANY)

    def body(*refs):
        x_refs, out_refs = refs[:n], refs[n:2 * n]
        send_sems, recv_sems, local_sems = refs[2 * n:]
        x, y, c = lax.axis_index("x"), lax.axis_index("y"), lax.axis_index("c")
        me, sibling = (x, y, c), (x, y, 1 - c)
        chips = [(1 - x, y), (x, 1 - y), (1 - x, 1 - y)]

        def copy(a, k, block, to, src=None):
            dst = out_refs[a].at[_dev_index(block)]
            return pltpu.make_async_remote_copy(
                src_ref=dst if src is None else src, dst_ref=dst,
                send_sem=send_sems.at[a, k], recv_sem=recv_sems.at[a, k],
                device_id=to, device_id_type=pl.DeviceIdType.MESH)

        mine = [pltpu.make_async_copy(x_refs[a], out_refs[a].at[_dev_index(me)], local_sems.at[a]) for a in range(n)]
        for cp in mine:
            cp.start()
        first = []
        for a in range(n):
            first.append(copy(a, 0, me, sibling, src=x_refs[a]))
            first += [copy(a, 1 + j, me, (*chip, c), src=x_refs[a]) for j, chip in enumerate(chips)]
        for cp in first:
            cp.start()
        passed = []
        for j, chip in enumerate(chips):
            for a in range(n):
                copy(a, 1 + j, (*chip, c), me).wait_recv()
                fwd = copy(a, 4 + j, (*chip, c), sibling)
                fwd.start()
                passed.append(fwd)
        for a in range(n):
            copy(a, 0, sibling, me).wait_recv()
            for j, chip in enumerate(chips):
                copy(a, 4 + j, (*chip, 1 - c), me).wait_recv()
        for cp in first + passed:
            cp.wait_send()
        for cp in mine:
            cp.wait()

    return pl.pallas_call(
        body, name=name,
        out_shape=[jax.ShapeDtypeStruct((N_DEV,) + v.shape, v.dtype) for v in xs],
        in_specs=[any_spec] * n, out_specs=[any_spec] * n,
        scratch_shapes=[pltpu.SemaphoreType.DMA((n, 7)), pltpu.SemaphoreType.DMA((n, 7)),
                        pltpu.SemaphoreType.DMA((n,))],
    )(*xs)


def _sum_blocks(x, name):
    _, R, W = x.shape

    def body(x_ref, o_ref):
        acc = x_ref[0].astype(F32)
        for d in range(1, N_DEV):
            acc = acc + x_ref[d].astype(F32)
        o_ref[...] = acc

    return pl.pallas_call(body, name=name, out_shape=jax.ShapeDtypeStruct((R, W), F32),
                          compiler_params=pltpu.CompilerParams(vmem_limit_bytes=VMEM_LIMIT))(x)


def _adam_update(w, g, m, v):
    mn = ADAM_B1 * m + (1.0 - ADAM_B1) * g
    vn = ADAM_B2 * v + (1.0 - ADAM_B2) * (g * g)
    m_hat = mn / (1.0 - ADAM_B1 ** ADAM_STEP)
    v_hat = vn / (1.0 - ADAM_B2 ** ADAM_STEP)
    return -ADAM_LR * (m_hat / (jnp.sqrt(v_hat) + ADAM_EPS) + ADAM_WD * w), mn, vn


def _row_tile(R, W, budget):
    padded = -(-W // LANES) * LANES * 4
    if R * padded <= budget:
        return R
    return _pick(R, [t for t in (2048, 1024, 512, 256, 128, 64, 32, 16, 8) if t * padded <= budget])


def _adamw(w, g, m, v, name):
    R, W = w.shape
    tr = _row_tile(R, W, 1 << 20)

    def body(w_ref, g_ref, m_ref, v_ref, d_ref, mo_ref, vo_ref):
        d_ref[...], mo_ref[...], vo_ref[...] = _adam_update(w_ref[...], g_ref[...], m_ref[...], v_ref[...])

    spec = pl.BlockSpec((tr, W), lambda i: (i, 0))
    shp = jax.ShapeDtypeStruct((R, W), F32)
    return pl.pallas_call(
        body, name=name, grid=(R // tr,), in_specs=[spec] * 4, out_specs=[spec] * 3, out_shape=[shp] * 3,
        compiler_params=_cparams(("arbitrary",)),
    )(w, g, m, v)


def _adamw_reduce(w, land, m, v, name):
    R, W = w.shape
    tr = _row_tile(R, W, 1 << 20)

    def body(w_ref, l_ref, m_ref, v_ref, g_ref, d_ref, mo_ref, vo_ref):
        g = l_ref[0].astype(F32)
        for d in range(1, N_DEV):
            g = g + l_ref[d].astype(F32)
        g_ref[...] = g
        d_ref[...], mo_ref[...], vo_ref[...] = _adam_update(w_ref[...], g, m_ref[...], v_ref[...])

    spec = pl.BlockSpec((tr, W), lambda i: (i, 0))
    lspec = pl.BlockSpec((N_DEV, tr, W), lambda i: (0, i, 0))
    shp = jax.ShapeDtypeStruct((R, W), F32)
    return pl.pallas_call(
        body, name=name, grid=(R // tr,), in_specs=[spec, lspec, spec, spec], out_specs=[spec] * 4,
        out_shape=[shp] * 4, compiler_params=_cparams(("arbitrary",)),
    )(w, land, m, v)


def _gelu(v):
    c = math.sqrt(2.0 / math.pi)
    return 0.5 * v * (1.0 + jnp.tanh(c * (v + 0.044715 * v * v * v)))


def _gelu_grad(v):
    c = math.sqrt(2.0 / math.pi)
    t = jnp.tanh(c * (v + 0.044715 * v * v * v))
    return 0.5 * (1.0 + t) + 0.5 * v * (1.0 - t * t) * c * (1.0 + 3.0 * 0.044715 * v * v)


def kernel(x, p, norm_mix, w_in, q_norm, k_norm, ssm_a_re, ssm_a_im, ssm_log_dt, ssm_b_re, ssm_b_im, ssm_c_re, ssm_c_im, ssm_d, w_glu, b_glu, w_out, norm_ple, w_ple_gate, w_ple_proj, norm_final, loss_target, m_norm_mix, m_w_in, m_q_norm, m_k_norm, m_ssm_a_re, m_ssm_a_im, m_ssm_log_dt, m_ssm_b_re, m_ssm_b_im, m_ssm_c_re, m_ssm_c_im, m_ssm_d, m_w_glu, m_b_glu, m_w_out, m_norm_ple, m_w_ple_gate, m_w_ple_proj, m_norm_final, v_norm_mix, v_w_in, v_q_norm, v_k_norm, v_ssm_a_re, v_ssm_a_im, v_ssm_log_dt, v_ssm_b_re, v_ssm_b_im, v_ssm_c_re, v_ssm_c_im, v_ssm_d, v_w_glu, v_b_glu, v_w_out, v_norm_ple, v_w_ple_gate, v_w_ple_proj, v_norm_final):
    L, D = x.shape[1], x.shape[2]
    D_SSM = ssm_d.shape[1]
    G = D_SSM // SSM_H
    n_slab = D_SSM // SLAB
    n_in = w_in.shape[2]
    D_IN = n_in * N_DEV
    n_pp = w_ple_proj.shape[2]
    n_glu = w_glu.shape[2]
    xs = x[0]
    ps = p[0, 0]
    tgt = loss_target[0]

    (win_t3,) = _all_gather([w_in[0].T.astype(BF16)], "gather_w_in")
    win_t = win_t3.reshape(D_IN, D)
    later_weights = _Carry("gather", [w_glu[0].astype(BF16), w_out[0].astype(BF16), w_ple_gate[0].astype(BF16),
                                      w_ple_proj[0].astype(BF16)])

    ssm_prm = (ssm_a_re[0], ssm_a_im[0], ssm_log_dt[0].reshape(2, G, 1),
               ssm_b_re[0].transpose(0, 1, 3, 2), ssm_b_im[0].transpose(0, 1, 3, 2), ssm_c_re[0], ssm_c_im[0])

    cos, sin = _rope_tables(L)
    hn = _norm_in(xs, norm_mix, "norm_mix")
    ZT = 512
    zp_tile = lambda j: jnp.where(j < 2, j, jnp.where(j < D_IN // ZT - 1, j + 1, 2))
    z = _mm(hn, win_t, "nt", "in_proj", n_tiles=(ZT, zp_tile))
    qr, kr, vb, kt = _qkv_prep(z, cos, sin, q_norm, k_norm)
    o, lse, (wglu3, wout3, wpg3, wpp3) = _attn_fwd(qr, kr, vb, later_weights)
    wout = wout3.reshape(-1, D)
    wpg = wpg3.reshape(-1, D)
    u_off = 2 * D_ATTN
    u_perm = _seg_perm(z[:, u_off:u_off + D_SSM])
    ys = _seg_unperm(_ssm_fwd(u_perm, ssm_prm, ssm_d))

    tm = _pick(L, (256,))

    def gelu_body(y_ref, o_ref):
        o_ref[...] = _gelu(y_ref[...]).astype(BF16)

    (gy,) = _rowcall(gelu_body, "gelu", L, tm, [(ys, _rspec(tm, D_SSM))], [(D_SSM, BF16)])
    glu = _mm(gy, wglu3, "nn", "glu_proj", bias=b_glu, b_blk=True)

    def mix_body(o_ref, ga_ref, gla_ref, glb_ref, gs_ref, cat_ref):
        ga, gs = ga_ref[...], gs_ref[...]
        cat_ref[:, :D_ATTN] = (o_ref[...] * ga * _sigmoid(ga)).astype(BF16)
        cat_ref[:, D_ATTN:] = (gla_ref[...] * _sigmoid(glb_ref[...]) * gs * _sigmoid(gs)).astype(BF16)

    (cat,) = _rowcall(mix_body, "mix", L, tm,
                      [(o, _rspec(tm, D_ATTN)), (z, _rspec(tm, D_ATTN, 1)), (glu, _rspec(tm, D_SSM, 0)),
                       (glu, _rspec(tm, D_SSM, 1)), (z, _rspec(tm, D_SSM, 3))], [(D_ATTN + D_SSM, BF16)])
    h1 = _mm(cat, wout, "nn", "out_proj", add=xs)
    n2 = _norm_in(h1, norm_ple, "norm_ple")
    gpre = _mm(n2, wpg, "nn", "ple_gate")
    pb = ps.astype(BF16)
    pp = _mm(pb, wpp3, "nn", "ple_proj", b_blk=True)

    nf = norm_final.reshape(1, D)

    def tail_body(h1_ref, gp_ref, pp_ref, t_ref, g_ref, dh2_ref, dpp_ref, dsg_ref, loss_ref, dg_ref):
        gate = _sigmoid(gp_ref[...])
        ppv = pp_ref[...]
        h2 = h1_ref[...] + gate * ppv
        r = _rms(h2)
        hh = h2 * r
        err = hh * g_ref[...] - t_ref[...]
        _acc(loss_ref, jnp.broadcast_to(0.5 * jnp.sum(jnp.mean(err * err, axis=-1, keepdims=True)), loss_ref.shape))
        dy = err * (1.0 / D)
        _acc(dg_ref, _colsum(dy * hh))
        dh2 = _rms_bwd(dy, hh, r, g_ref[...])
        dh2_ref[...] = dh2
        dpp_ref[...] = (dh2 * gate).astype(BF16)
        dsg_ref[...] = (dh2 * ppv * gate * (1.0 - gate)).astype(BF16)

    dh2, dpp, dsg, loss_acc, d_nf = _rowcall(
        tail_body, "tail", L, tm,
        [(h1, _rspec(tm, D)), (gpre, _rspec(tm, D)), (pp, _rspec(tm, D)), (tgt, _rspec(tm, D)), (nf, _fspec(nf.shape))],
        [(D, F32), (D, BF16), (D, BF16)], [(1, LANES), (1, D)])
    loss = lax.psum(loss_acc[0, 0], ("x", "y", "c"))

    g_wpp3 = _mm(pb, dpp, "tn", "d_ple_proj", out_dtype=BF16, out_blk=n_pp)
    g_wpg = _mm(n2, dsg, "tn", "d_ple_gate", out_dtype=BF16)
    dn2 = _mm(dsg, wpg, "nt", "d_norm_ple_in")

    def ple_bwd_body(h1_ref, dn_ref, dh2_ref, g_ref, dh1_ref, dh1b_ref, dg_ref):
        h1v = h1_ref[...]
        r = _rms(h1v)
        hh = h1v * r
        dn = dn_ref[...]
        _acc(dg_ref, _colsum(dn * hh))
        dh1 = dh2_ref[...] + _rms_bwd(dn, hh, r, g_ref[...])
        dh1_ref[...] = dh1
        dh1b_ref[...] = dh1.astype(BF16)

    dh1, dh1b, d_nple = _rowcall(
        ple_bwd_body, "ple_bwd", L, tm,
        [(h1, _rspec(tm, D)), (dn2, _rspec(tm, D)), (dh2, _rspec(tm, D)), (norm_ple, _fspec(norm_ple.shape))],
        [(D, F32), (D, BF16)], [(1, D)])

    dcat = _mm(dh1b, wout, "nt", "d_cat")
    g_wout = _mm(cat, dh1b, "tn", "d_out_proj", out_dtype=BF16)

    def mix_bwd_body(dca_ref, dcs_ref, o_ref, ga_ref, gla_ref, glb_ref, gs_ref,
                     do_ref, dga_ref, dgs_ref, dglu_ref, db_ref):
        dca, dcs, ga, gs = dca_ref[...], dcs_ref[...], ga_ref[...], gs_ref[...]
        sa, ss, sb = _sigmoid(ga), _sigmoid(gs), _sigmoid(glb_ref[...])
        gla = gla_ref[...]
        do_ref[...] = (dca * ga * sa).astype(BF16)
        dga_ref[...] = (dca * o_ref[...] * sa * (1.0 + ga * (1.0 - sa))).astype(BF16)
        dgs_ref[...] = (dcs * gla * sb * ss * (1.0 + gs * (1.0 - ss))).astype(BF16)
        dy2 = dcs * gs * ss
        da, db = dy2 * sb, dy2 * gla * sb * (1.0 - sb)
        dglu_ref[:, :D_SSM] = da.astype(BF16)
        dglu_ref[:, D_SSM:] = db.astype(BF16)
        _acc(db_ref, jnp.concatenate([_colsum(da), _colsum(db)], axis=-1))

    do, dga, dgs, dglu, g_bglu = _rowcall(
        mix_bwd_body, "mix_bwd", L, tm,
        [(dcat, _rspec(tm, D_ATTN, 0)), (dcat, _rspec(tm, D_SSM, 1)), (o, _rspec(tm, D_ATTN)),
         (z, _rspec(tm, D_ATTN, 1)), (glu, _rspec(tm, D_SSM, 0)), (glu, _rspec(tm, D_SSM, 1)),
         (z, _rspec(tm, D_SSM, 3))],
        [(D_ATTN, BF16), (D_ATTN, BF16), (D_SSM, BF16), (2 * D_SSM, BF16)], [(1, 2 * D_SSM)])

    g_wglu3 = _mm(gy, dglu, "tn", "d_glu_proj", out_dtype=BF16, out_blk=n_glu)
    dgy = _mm(dglu, wglu3, "nt", "d_gelu_out", b_blk=True)

    def gelu_bwd_body(dg_ref, y_ref, o_ref):
        o_ref[...] = dg_ref[...] * _gelu_grad(y_ref[...])

    (dys,) = _rowcall(gelu_bwd_body, "gelu_bwd", L, tm,
                      [(dgy, _rspec(tm, D_SSM)), (ys, _rspec(tm, D_SSM))], [(D_SSM, F32)])
    du_perm, pg = _ssm_bwd(u_perm, _seg_perm(dys), ssm_prm, ssm_d)
    du = _seg_unperm(du_perm)

    dqs, dkr, dvv, (l_wglu, l_wout, l_wpg, l_wpp) = _attn_bwd(
        qr, kr, vb, kt, do, o, lse,
        _Carry("a2a", [g_wglu3, g_wout.reshape(N_DEV, -1, D), g_wpg.reshape(N_DEV, -1, D), g_wpp3]))

    scale = HEAD_DIM ** -0.5
    kblk = 4 * D_ATTN // D_KV
    tmq = _pick(L, (512, 256))

    def qkv_bwd_body(dq_ref, dk_ref, dv_ref, q_ref, k_ref, cos_ref, sin_ref, qn_ref, kn_ref,
                     dqo_ref, dko_ref, dvo_ref, dqn_ref, dkn_ref):
        c, s = cos_ref[...], sin_ref[...]

        def head(g, xh, w):
            dn = g * c + _partner(g * s)
            r = _rms(xh)
            xhat = xh * r
            return _rms_bwd(dn, xhat, r, w), _colsum(dn * xhat)

        dqn = jnp.zeros((1, HEAD_DIM), F32)
        for h in range(N_HEADS):
            sl = slice(h * HEAD_DIM, (h + 1) * HEAD_DIM)
            dx, dw = head(dq_ref[:, sl] * scale, q_ref[:, sl], qn_ref[...])
            dqo_ref[:, sl] = dx.astype(BF16)
            dqn = dqn + dw
        dkn = jnp.zeros((1, HEAD_DIM), F32)
        for h in range(N_KV):
            sl = slice(h * HEAD_DIM, (h + 1) * HEAD_DIM)
            dx, dw = head(dk_ref[:, sl], k_ref[:, sl], kn_ref[...])
            dko_ref[:, sl] = dx.astype(BF16)
            dkn = dkn + dw
        dvo_ref[...] = dv_ref[...].astype(BF16)
        _acc(dqn_ref, dqn)
        _acc(dkn_ref, dkn)

    dq, dk, dv, g_qn, g_kn = _rowcall(
        qkv_bwd_body, "qkv_bwd", L, tmq,
        [(dqs, _rspec(tmq, D_ATTN)), (dkr, _rspec(tmq, D_KV)), (dvv, _rspec(tmq, D_KV)),
         (z, _rspec(tmq, D_ATTN, 0)), (z, _rspec(tmq, D_KV, kblk)), (cos, _rspec(tmq, HEAD_DIM)),
         (sin, _rspec(tmq, HEAD_DIM)), (q_norm, _fspec(q_norm.shape)), (k_norm, _fspec(k_norm.shape))],
        [(D_ATTN, BF16), (D_KV, BF16), (D_KV, BF16)], [(1, HEAD_DIM), (1, HEAD_DIM)])

    dz = jnp.concatenate([dq, dk, dv, dga, du.astype(BF16), dgs], axis=1)
    g_win_t = _mm(dz, hn, "tn", "d_in_proj", out_dtype=BF16)
    pg_send = pg.reshape(N_DEV, (n_slab // N_DEV) * 2 * PG_ROWS, SLAB_S)
    dhn, (l_win_t, l_pg) = _mm(dz, win_t, "nn", "d_norm_mix_in",
                               carry=_Carry("a2a", [g_win_t.reshape(N_DEV, n_in, D), pg_send]))

    def in_bwd_body(x_ref, dn_ref, dh1_ref, g_ref, dx_ref, dg_ref):
        xv = x_ref[...]
        r = _rms(xv)
        hh = xv * r
        dn = dn_ref[...]
        _acc(dg_ref, _colsum(dn * hh))
        dx_ref[...] = dh1_ref[...] + _rms_bwd(dn, hh, r, g_ref[...])

    grad_x, g_nmix = _rowcall(
        in_bwd_body, "in_bwd", L, tm,
        [(xs, _rspec(tm, D)), (dhn, _rspec(tm, D)), (dh1, _rspec(tm, D)), (norm_mix, _fspec(norm_mix.shape))],
        [(D, F32)], [(1, D)])

    tiny_parts = [g_nmix, g_bglu, d_nple, d_nf, g_qn, g_kn]
    tiny_flat = jnp.concatenate([t.reshape(-1) for t in tiny_parts])
    tiny_rows = -(-tiny_flat.shape[0] // (8 * LANES)) * 8
    tiny = jnp.pad(tiny_flat, (0, tiny_rows * LANES - tiny_flat.shape[0])).reshape(tiny_rows, LANES)
    pg_sum = _sum_blocks(l_pg, "sum_ssm_grads")
    pg_all, tiny_all = _all_gather([pg_sum, tiny], "gather_small_grads")
    (g_bt_re, g_bt_im, g_c_re, g_c_im, g_a_re, g_a_im, g_ldt, g_skip) = _ssm_param_grads(
        pg_all.reshape(n_slab, 2, PG_ROWS, SLAB_S), ssm_prm)
    tiny_sum = _sum_blocks(tiny_all, "sum_tiny_grads").reshape(-1)
    tiny_grads, off = [], 0
    for t in tiny_parts:
        tiny_grads.append(tiny_sum[off:off + t.size].reshape(t.shape))
        off += t.size
    r_nmix, r_bglu, r_nple, r_nf, r_qn, r_kn = tiny_grads

    grads, deltas, new_ms, new_vs = {}, {}, {}, {}
    outs = _adamw_reduce(w_in[0].T, l_win_t, m_w_in[0].T, v_w_in[0].T, "adamw_w_in")
    grads["w_in"], deltas["w_in"], new_ms["w_in"], new_vs["w_in"] = [t.T[None] for t in outs]
    big = [("w_glu", w_glu, l_wglu, m_w_glu, v_w_glu),
           ("w_out", w_out, l_wout, m_w_out, v_w_out), ("w_ple_gate", w_ple_gate, l_wpg, m_w_ple_gate, v_w_ple_gate),
           ("w_ple_proj", w_ple_proj, l_wpp, m_w_ple_proj, v_w_ple_proj)]
    for name, w, ld, m, v in big:
        shp = w.shape
        outs = _adamw_reduce(w[0], ld, m[0], v[0], "adamw_" + name)
        grads[name], deltas[name], new_ms[name], new_vs[name] = [t.reshape(shp) for t in outs]
    bt2 = (2 * G * SSM_H, SSM_P)
    for name, w, g, m, v in (("ssm_b_re", ssm_b_re, g_bt_re, m_ssm_b_re, v_ssm_b_re),
                             ("ssm_b_im", ssm_b_im, g_bt_im, m_ssm_b_im, v_ssm_b_im)):
        to2 = lambda t: t[0].transpose(0, 1, 3, 2).reshape(bt2)
        back = lambda t: t.reshape(2, G, SSM_H, SSM_P).transpose(0, 1, 3, 2)[None]
        outs = _adamw(to2(w), g.reshape(bt2), to2(m), to2(v), "adamw_" + name)
        grads[name] = back(g)
        deltas[name], new_ms[name], new_vs[name] = [back(t) for t in outs]
    small = [("norm_mix", norm_mix, r_nmix, m_norm_mix, v_norm_mix, (1, D)),
             ("q_norm", q_norm, r_qn, m_q_norm, v_q_norm, (1, HEAD_DIM)),
             ("k_norm", k_norm, r_kn, m_k_norm, v_k_norm, (1, HEAD_DIM)),
             ("ssm_a_re", ssm_a_re, g_a_re, m_ssm_a_re, v_ssm_a_re, (2 * G, SSM_P)),
             ("ssm_a_im", ssm_a_im, g_a_im, m_ssm_a_im, v_ssm_a_im, (2 * G, SSM_P)),
             ("ssm_log_dt", ssm_log_dt, g_ldt, m_ssm_log_dt, v_ssm_log_dt, (2, G)),
             ("ssm_c_re", ssm_c_re, g_c_re, m_ssm_c_re, v_ssm_c_re, (2 * G * SSM_H, SSM_P)),
             ("ssm_c_im", ssm_c_im, g_c_im, m_ssm_c_im, v_ssm_c_im, (2 * G * SSM_H, SSM_P)),
             ("ssm_d", ssm_d, g_skip, m_ssm_d, v_ssm_d, (1, D_SSM)),
             ("b_glu", b_glu, r_bglu, m_b_glu, v_b_glu, (1, 2 * D_SSM)),
             ("norm_ple", norm_ple, r_nple, m_norm_ple, v_norm_ple, (1, D)),
             ("norm_final", norm_final, r_nf, m_norm_final, v_norm_final, (1, D))]
    for name, w, g, m, v, s2 in small:
        shp = w.shape
        outs = _adamw(w.reshape(s2), g.reshape(s2), m.reshape(s2), v.reshape(s2), "adamw_" + name)
        grads[name] = g.reshape(shp)
        deltas[name], new_ms[name], new_vs[name] = [t.reshape(shp) for t in outs]

    order = ["norm_mix", "w_in", "q_norm", "k_norm", "ssm_a_re", "ssm_a_im", "ssm_log_dt", "ssm_b_re", "ssm_b_im",
             "ssm_c_re", "ssm_c_im", "ssm_d", "w_glu", "b_glu", "w_out", "norm_ple", "w_ple_gate", "w_ple_proj",
             "norm_final"]
    return (loss, grad_x[None], *[grads[k] for k in order], *[deltas[k] for k in order],
            *[new_ms[k] for k in order], *[new_vs[k] for k in order])
```

```python
import functools
import math

import numpy as np
import jax
import jax.numpy as jnp
from jax import lax
from jax.experimental import pallas as pl
from jax.experimental.pallas import tpu as pltpu

F32 = jnp.float32
BF16 = jnp.bfloat16

N_DEV = 8
EPS = 1e-6
GRID_W = 64
ROPE_THETA = 10000.0
HEAD_DIM = 128
N_HEADS = 8
N_KV = 2
REP = N_HEADS // N_KV
D_ATTN = N_HEADS * HEAD_DIM
D_KV = N_KV * HEAD_DIM
SSM_H = 16
SSM_P = 64
SLAB = 128
SLAB_G = SLAB // SSM_H
SLAB_S = SLAB_G * SSM_P
SEG = 8
LANES = 128
PG_ROWS = 72
VMEM_LIMIT = 48 << 20

ADAM_LR = 0.001
ADAM_B1 = 0.9
ADAM_B2 = 0.999
ADAM_EPS = 1e-08
ADAM_WD = 0.01
ADAM_STEP = 10


def _pick(n, cands):
    for c in cands:
        if n % c == 0:
            return c
    return n


def _cparams(sem, vmem=VMEM_LIMIT):
    return pltpu.CompilerParams(dimension_semantics=sem, vmem_limit_bytes=vmem)


class _Carry:
    def __init__(self, kind, xs):
        self.kind, self.xs, self.n = kind, list(xs), len(xs)
        lead = (N_DEV,) if kind == "gather" else ()
        self.out_shape = [jax.ShapeDtypeStruct(lead + v.shape, v.dtype) for v in xs]
        self.specs = [pl.BlockSpec(memory_space=pl.ANY)] * self.n
        self.scratch = [pltpu.SemaphoreType.DMA((self.n, N_DEV - 1)), pltpu.SemaphoreType.DMA((self.n, N_DEV - 1)),
                        pltpu.SemaphoreType.DMA((self.n,))]

    def _copies(self, x_refs, out_refs, sems):
        send_sems, recv_sems, local_sems = sems
        x, y, c = lax.axis_index("x"), lax.axis_index("y"), lax.axis_index("c")
        me = _dev_index((x, y, c))
        mine, sends, arrivals = [], [], []
        for a in range(self.n):
            src_mine = x_refs[a] if self.kind == "gather" else x_refs[a].at[me]
            mine.append(pltpu.make_async_copy(src_mine, out_refs[a].at[me], local_sems.at[a]))
            for k in range(1, N_DEV):
                peer = _peer(k, x, y, c)
                src = x_refs[a] if self.kind == "gather" else x_refs[a].at[_dev_index(peer)]
                sends.append(pltpu.make_async_remote_copy(
                    src_ref=src, dst_ref=out_refs[a].at[me], send_sem=send_sems.at[a, k - 1],
                    recv_sem=recv_sems.at[a, k - 1], device_id=peer, device_id_type=pl.DeviceIdType.MESH))
                land = out_refs[a].at[_dev_index(peer)]
                arrivals.append(pltpu.make_async_remote_copy(
                    src_ref=land, dst_ref=land, send_sem=send_sems.at[a, k - 1],
                    recv_sem=recv_sems.at[a, k - 1], device_id=peer, device_id_type=pl.DeviceIdType.MESH))
        return mine, sends, arrivals

    def start(self, x_refs, out_refs, sems):
        mine, sends, _ = self._copies(x_refs, out_refs, sems)
        for cp in mine + sends:
            cp.start()

    def wait(self, x_refs, out_refs, sems):
        mine, sends, arrivals = self._copies(x_refs, out_refs, sems)
        for cp in arrivals:
            cp.wait_recv()
        for cp in sends:
            cp.wait_send()
        for cp in mine:
            cp.wait()


def _grid_edges(grid):
    first = functools.reduce(lambda p, q: p & q, [pl.program_id(d) == 0 for d in range(len(grid))])
    last = functools.reduce(lambda p, q: p & q, [pl.program_id(d) == g - 1 for d, g in enumerate(grid)])
    return first, last


def _mm(a, b, mode, name, out_dtype=F32, add=None, bias=None, a_blk=False, b_blk=False, out_blk=0, carry=None,
        n_tiles=None):
    w = b.shape[2] if b_blk else out_blk
    if mode == "nn":
        M, K = a.shape
        N = b.shape[0] * w if b_blk else b.shape[1]
    elif mode == "nt":
        M = a.shape[1] if a_blk else a.shape[0]
        N = b.shape[1] if b_blk else b.shape[0]
        K = b.shape[0] * w if b_blk else b.shape[1]
    else:
        K, M = a.shape
        N = b.shape[0] * w if b_blk else b.shape[1]
    tm = _pick(M, (1024, 768, 512, 256))
    tn = _pick(N, (1024, 768, 512, 256))
    tk = K if (mode != "tn" and K <= 2048) else _pick(K, (1024, 768, 512, 256))
    perm = lambda j: j
    if n_tiles:
        tn, perm = n_tiles
    if mode == "nt" and b_blk:
        tk = w
    elif b_blk or out_blk:
        tn = w
    nk = K // tk
    grid = (M // tm, N // tn, nk)
    if mode == "nn":
        a_spec = pl.BlockSpec((tm, tk), lambda i, j, k: (i, k))
        b_spec = (pl.BlockSpec((1, tk, tn), lambda i, j, k: (j, k, 0)) if b_blk
                  else pl.BlockSpec((tk, tn), lambda i, j, k: (k, j)))
        dims = (((1,), (0,)), ((), ()))
    elif mode == "nt":
        a_spec = (pl.BlockSpec((1, tm, tk), lambda i, j, k: (k, i, 0)) if a_blk
                  else pl.BlockSpec((tm, tk), lambda i, j, k: (i, k)))
        b_spec = (pl.BlockSpec((1, tn, tk), lambda i, j, k: (k, j, 0)) if b_blk
                  else pl.BlockSpec((tn, tk), lambda i, j, k: (perm(j), k)))
        dims = (((1,), (1,)), ((), ()))
    else:
        a_spec = pl.BlockSpec((tk, tm), lambda i, j, k: (k, i))
        b_spec = (pl.BlockSpec((1, tk, tn), lambda i, j, k: (j, k, 0)) if b_blk
                  else pl.BlockSpec((tk, tn), lambda i, j, k: (k, j)))
        dims = (((0,), (0,)), ((), ()))
    if out_blk:
        out_spec = pl.BlockSpec((1, tm, tn), lambda i, j, k: (j, i, 0))
        out_shape = jax.ShapeDtypeStruct((N // tn, M, tn), out_dtype)
    else:
        out_spec = pl.BlockSpec((tm, tn), lambda i, j, k: (i, j))
        out_shape = jax.ShapeDtypeStruct((M, N), out_dtype)
    extras, extra_specs = [], []
    if add is not None:
        extras.append(add)
        extra_specs.append(pl.BlockSpec((tm, tn), lambda i, j, k: (i, j)))
    if bias is not None:
        extras.append(bias)
        extra_specs.append(pl.BlockSpec((1, tn), lambda i, j, k: (0, j)))

    n_ex = len(extras)
    nc = carry.n if carry else 0

    def body(a_ref, b_ref, *rest):
        ex_refs, cx = rest[:n_ex], rest[n_ex:n_ex + nc]
        o_ref, cout = rest[n_ex + nc], rest[n_ex + nc + 1:n_ex + 2 * nc + 1]
        tail = rest[n_ex + 2 * nc + 1:]
        sems = tail[:3] if carry else ()
        first, last = _grid_edges(grid)
        if carry:
            @pl.when(first)
            def _():
                carry.start(cx, cout, sems)

        def product():
            av = a_ref[0] if a_blk else a_ref[...]
            bv = b_ref[0] if b_blk else b_ref[...]
            return lax.dot_general(av, bv, dims, preferred_element_type=F32)

        def finish(out):
            for r in ex_refs:
                out = out + r[...]
            if out_blk:
                o_ref[0] = out.astype(out_dtype)
            else:
                o_ref[...] = out.astype(out_dtype)

        if nk == 1:
            finish(product())
        else:
            acc_ref = tail[-1]
            k = pl.program_id(2)

            @pl.when(k == 0)
            def _():
                acc_ref[...] = jnp.zeros_like(acc_ref)

            acc_ref[...] += product()

            @pl.when(k == nk - 1)
            def _():
                finish(acc_ref[...])

        if carry:
            @pl.when(last)
            def _():
                carry.wait(cx, cout, sems)

    scratch = (carry.scratch if carry else []) + ([pltpu.VMEM((tm, tn), F32)] if nk > 1 else [])
    outs = pl.pallas_call(
        body, name=name, grid=grid,
        in_specs=[a_spec, b_spec] + extra_specs + (carry.specs if carry else []),
        out_specs=[out_spec] + (carry.specs if carry else []),
        out_shape=[out_shape] + (carry.out_shape if carry else []),
        scratch_shapes=scratch,
        compiler_params=_cparams(("arbitrary", "arbitrary", "arbitrary")),
    )(a, b, *extras, *(carry.xs if carry else []))
    return (outs[0], outs[1:]) if carry else outs[0]


def _rspec(tm, w, cb=0):
    return pl.BlockSpec((tm, w), lambda i: (i, cb))


def _fspec(shape):
    nd = len(shape)
    return pl.BlockSpec(shape, lambda i: (0,) * nd)


def _rowcall(body, name, L, tm, ins, row_outs, acc_outs=()):
    out_shape = [jax.ShapeDtypeStruct((L, w), dt) for w, dt in row_outs]
    out_shape += [jax.ShapeDtypeStruct(s, F32) for s in acc_outs]
    out_specs = [_rspec(tm, w) for w, _ in row_outs] + [_fspec(s) for s in acc_outs]
    return pl.pallas_call(
        body, name=name, grid=(L // tm,),
        in_specs=[s for _, s in ins], out_specs=out_specs, out_shape=out_shape,
        compiler_params=_cparams(("arbitrary",)),
    )(*[a for a, _ in ins])


def _acc(ref, val):
    @pl.when(pl.program_id(0) == 0)
    def _():
        ref[...] = jnp.zeros_like(ref)
    ref[...] += val


def _colsum(v):
    return jnp.sum(v, axis=0, keepdims=True)


def _rms(xv):
    return lax.rsqrt(jnp.mean(xv * xv, axis=-1, keepdims=True) + EPS)


def _rms_bwd(dn, xhat, r, g):
    dng = dn * g
    return r * (dng - xhat * jnp.mean(dng * xhat, axis=-1, keepdims=True))


def _sigmoid(v):
    return jax.nn.sigmoid(v)


def _f32(ref):
    return ref[...].astype(F32)


def _partner(v):
    w = v.shape[-1]
    lane = lax.broadcasted_iota(jnp.int32, v.shape, v.ndim - 1)
    first_half = (lane % 64) < 32
    return jnp.where(first_half, pltpu.roll(v, w - 32, axis=v.ndim - 1), pltpu.roll(v, 32, axis=v.ndim - 1))


def _norm_in(x, g, name):
    L, D = x.shape
    tm = _pick(L, (512, 256))

    def body(x_ref, g_ref, o_ref):
        xv = x_ref[...]
        o_ref[...] = (xv * _rms(xv) * g_ref[...]).astype(BF16)

    return _rowcall(body, name, L, tm, [(x, _rspec(tm, D)), (g, _fspec(g.shape))], [(D, BF16)])[0]


def _rope_tables(L):
    t = np.arange(L)
    rows = (t // GRID_W).astype(np.float32)
    cols = (t % GRID_W).astype(np.float32)
    n_freq = HEAD_DIM // 4
    inv_freq = np.float32(ROPE_THETA) ** (-np.arange(n_freq, dtype=np.float32) / np.float32(n_freq))
    ar = (rows[:, None] * inv_freq[None, :]).astype(np.float32).astype(np.float64)
    ac = (cols[:, None] * inv_freq[None, :]).astype(np.float32).astype(np.float64)
    cos = np.concatenate([np.cos(ar), np.cos(ar), np.cos(ac), np.cos(ac)], axis=-1).astype(np.float32)
    sin = np.concatenate([-np.sin(ar), np.sin(ar), -np.sin(ac), np.sin(ac)], axis=-1).astype(np.float32)
    return jnp.asarray(cos), jnp.asarray(sin)


def _qkv_prep(z, cos, sin, qn, kn):
    L = z.shape[0]
    tm = _pick(L, (512, 256))
    scale = HEAD_DIM ** -0.5
    kblk = 4 * D_ATTN // D_KV

    def body(q_ref, k_ref, v_ref, cos_ref, sin_ref, qn_ref, kn_ref, qo_ref, ko_ref, vo_ref, kt_ref):
        c, s = cos_ref[...], sin_ref[...]

        def head(xh, w):
            n = xh * _rms(xh) * w
            return n * c + _partner(n) * s

        for h in range(N_HEADS):
            sl = slice(h * HEAD_DIM, (h + 1) * HEAD_DIM)
            qo_ref[:, sl] = (head(q_ref[:, sl].astype(F32), qn_ref[...]) * scale).astype(BF16)
        for h in range(N_KV):
            sl = slice(h * HEAD_DIM, (h + 1) * HEAD_DIM)
            kr = head(k_ref[:, sl].astype(F32), kn_ref[...])
            ko_ref[:, sl] = kr.astype(BF16)
            kt_ref[sl, :] = kr.T.astype(BF16)
        vo_ref[...] = v_ref[...].astype(BF16)

    return pl.pallas_call(
        body, name="qkv_prep", grid=(L // tm,),
        in_specs=[_rspec(tm, D_ATTN, 0), _rspec(tm, D_KV, kblk), _rspec(tm, D_KV, kblk + 1),
                  _rspec(tm, HEAD_DIM), _rspec(tm, HEAD_DIM), _fspec(qn.shape), _fspec(kn.shape)],
        out_specs=[_rspec(tm, D_ATTN), _rspec(tm, D_KV), _rspec(tm, D_KV),
                   pl.BlockSpec((D_KV, tm), lambda i: (0, i))],
        out_shape=[jax.ShapeDtypeStruct((L, D_ATTN), BF16), jax.ShapeDtypeStruct((L, D_KV), BF16),
                   jax.ShapeDtypeStruct((L, D_KV), BF16), jax.ShapeDtypeStruct((D_KV, L), BF16)],
        compiler_params=_cparams(("arbitrary",)),
    )(z, z, z, cos, sin, qn, kn)


def _col_to_row(col):
    n = col.shape[0]
    eye = lax.broadcasted_iota(jnp.int32, (n, n), 0) == lax.broadcasted_iota(jnp.int32, (n, n), 1)
    return jnp.sum(jnp.where(eye, col, 0.0), axis=0, keepdims=True)


def _attn_fwd(q, k, v, carry=None):
    L = q.shape[0]
    tq = _pick(L, (256, 128))
    grid = (N_HEADS, L // tq)
    nc = carry.n if carry else 0

    def body(q_ref, k_ref, v_ref, *rest):
        cx, (o_ref, lse_ref) = rest[:nc], rest[nc:nc + 2]
        cout, sems = rest[nc + 2:2 * nc + 2], rest[2 * nc + 2:]
        first, last = _grid_edges(grid)
        if carry:
            @pl.when(first)
            def _():
                carry.start(cx, cout, sems)

        s = lax.dot_general(q_ref[...], k_ref[...], (((1,), (1,)), ((), ())), preferred_element_type=F32)
        m = jnp.max(s, axis=-1, keepdims=True)
        e = jnp.exp(s - m)
        l = jnp.sum(e, axis=-1, keepdims=True)
        o_ref[...] = jnp.dot(e.astype(BF16), v_ref[...], preferred_element_type=F32) / l
        lse_ref[0] = _col_to_row(m + jnp.log(l))

        if carry:
            @pl.when(last)
            def _():
                carry.wait(cx, cout, sems)

    outs = pl.pallas_call(
        body, name="attn_fwd", grid=grid,
        in_specs=[pl.BlockSpec((tq, HEAD_DIM), lambda h, i: (i, h)),
                  pl.BlockSpec((L, HEAD_DIM), lambda h, i: (0, h // REP)),
                  pl.BlockSpec((L, HEAD_DIM), lambda h, i: (0, h // REP))] + (carry.specs if carry else []),
        out_specs=[pl.BlockSpec((tq, HEAD_DIM), lambda h, i: (i, h)),
                   pl.BlockSpec((1, 1, tq), lambda h, i: (h, 0, i))] + (carry.specs if carry else []),
        out_shape=[jax.ShapeDtypeStruct((L, D_ATTN), F32), jax.ShapeDtypeStruct((N_HEADS, 1, L), F32)]
        + (carry.out_shape if carry else []),
        scratch_shapes=carry.scratch if carry else [],
        compiler_params=_cparams(("arbitrary", "arbitrary")),
    )(q, k, v, *(carry.xs if carry else []))
    return outs[0], outs[1], outs[2:]


def _attn_bwd(q, k, v, kt, do, o, lse, carry=None):
    L = q.shape[0]
    tq = _pick(L, (256, 128))
    kc = _pick(L, (256, 128))
    nt = (((1,), (1,)), ((), ()))
    grid = (N_KV, REP, L // tq)
    nc = carry.n if carry else 0

    def body(q_ref, do_ref, o_ref, lse_ref, k_ref, v_ref, kt_ref, *rest):
        cx, (dq_ref, dk_ref, dv_ref) = rest[:nc], rest[nc:nc + 3]
        cout, sems = rest[nc + 3:2 * nc + 3], rest[2 * nc + 3:]
        first, last = _grid_edges(grid)
        if carry:
            @pl.when(first)
            def _():
                carry.start(cx, cout, sems)

        @pl.when((pl.program_id(1) == 0) & (pl.program_id(2) == 0))
        def _():
            dk_ref[...] = jnp.zeros_like(dk_ref)
            dv_ref[...] = jnp.zeros_like(dv_ref)

        qv, dov = q_ref[...], do_ref[...]
        lse_row = lse_ref[0]
        delta = _col_to_row(jnp.sum(dov.astype(F32) * o_ref[...], axis=-1, keepdims=True))
        dqt = jnp.zeros((HEAD_DIM, tq), F32)
        for c in range(L // kc):
            sl = slice(c * kc, (c + 1) * kc)
            st = lax.dot_general(k_ref[sl, :], qv, nt, preferred_element_type=F32)
            pt = jnp.exp(st - lse_row)
            dpt = lax.dot_general(v_ref[sl, :], dov, nt, preferred_element_type=F32)
            dst = (pt * (dpt - delta)).astype(BF16)
            dv_ref[sl, :] += jnp.dot(pt.astype(BF16), dov, preferred_element_type=F32)
            dk_ref[sl, :] += jnp.dot(dst, qv, preferred_element_type=F32)
            dqt = dqt + jnp.dot(kt_ref[:, sl], dst, preferred_element_type=F32)
        dq_ref[...] = dqt.T

        if carry:
            @pl.when(last)
            def _():
                carry.wait(cx, cout, sems)

    head = lambda g, r, i: (i, g * REP + r)
    outs = pl.pallas_call(
        body, name="attn_bwd", grid=grid,
        in_specs=[pl.BlockSpec((tq, HEAD_DIM), head), pl.BlockSpec((tq, HEAD_DIM), head),
                  pl.BlockSpec((tq, HEAD_DIM), head),
                  pl.BlockSpec((1, 1, tq), lambda g, r, i: (g * REP + r, 0, i)),
                  pl.BlockSpec((L, HEAD_DIM), lambda g, r, i: (0, g)),
                  pl.BlockSpec((L, HEAD_DIM), lambda g, r, i: (0, g)),
                  pl.BlockSpec((HEAD_DIM, L), lambda g, r, i: (g, 0))] + (carry.specs if carry else []),
        out_specs=[pl.BlockSpec((tq, HEAD_DIM), head),
                   pl.BlockSpec((L, HEAD_DIM), lambda g, r, i: (0, g)),
                   pl.BlockSpec((L, HEAD_DIM), lambda g, r, i: (0, g))] + (carry.specs if carry else []),
        out_shape=[jax.ShapeDtypeStruct((L, D_ATTN), F32), jax.ShapeDtypeStruct((L, D_KV), F32),
                   jax.ShapeDtypeStruct((L, D_KV), F32)] + (carry.out_shape if carry else []),
        scratch_shapes=carry.scratch if carry else [],
        compiler_params=_cparams(("arbitrary", "arbitrary", "arbitrary")),
    )(q, do, o, lse, k, v, kt, *(carry.xs if carry else []))
    return outs[0], outs[1], outs[2], outs[3:]


def _seg_perm(a):
    L, C = a.shape
    return a.reshape(SEG, L // SEG, C).transpose(1, 0, 2).reshape(L, C)


def _seg_unperm(a):
    L, C = a.shape
    return a.reshape(L // SEG, SEG, C).transpose(1, 0, 2).reshape(L, C)


def _cmul(ar, ai, br, bi):
    return ar * br - ai * bi, ar * bi + ai * br


def _rows8(rr):
    return pl.ds(pl.multiple_of(rr * SEG, SEG), SEG)


def _seg_scan(xr_ref, xi_ref, ar, ai, reverse, n_rows, visit=None, visit_init=()):
    shape = ar.shape
    zero = jnp.zeros(shape, F32)

    def index(r):
        return (n_rows - 1 - r) if reverse else r

    def ends(r, carry):
        cr, ci = carry
        sl = _rows8(index(r))
        pr, pi = _cmul(ar, ai, cr, ci)
        return pr + xr_ref[sl, :], pi + xi_ref[sl, :]

    er, ei = lax.fori_loop(0, n_rows, ends, (zero, zero))

    pr, pi = ar, ai
    for _ in range(int(math.log2(n_rows))):
        pr, pi = _cmul(pr, pi, pr, pi)
    sub = lax.broadcasted_iota(jnp.int32, shape, 0)
    shift = (SEG - 1) if reverse else 1
    edge = (SEG - 1) if reverse else 0
    inr, ini = zero, zero
    for _ in range(SEG - 1):
        tr, ti = _cmul(pr, pi, inr, ini)
        inr = jnp.where(sub == edge, 0.0, pltpu.roll(tr + er, shift, axis=0))
        ini = jnp.where(sub == edge, 0.0, pltpu.roll(ti + ei, shift, axis=0))

    def step(rr, carry, last):
        cr, ci = carry[:2]
        sl = _rows8(rr)
        pr, pi = _cmul(ar, ai, cr, ci)
        nr, ni = pr + xr_ref[sl, :], pi + xi_ref[sl, :]
        xr_ref[sl, :] = nr
        xi_ref[sl, :] = ni
        acc = visit(rr, nr, ni, carry[2:], last) if visit else ()
        return (nr, ni, *acc)

    carry = lax.fori_loop(0, n_rows - 1, lambda r, c: step(index(r), c, False), (inr, ini, *visit_init))
    carry = step(index(n_rows - 1), carry, True)
    return inr, ini, carry[2:]


def _discretise(a_re, a_im, ldt):
    lr = jnp.minimum(a_re, -1e-4)
    li = a_im
    dt = jnp.exp(ldt)
    mag = jnp.exp(lr * dt)
    lbr = mag * jnp.cos(li * dt)
    lbi = mag * jnp.sin(li * dt)
    den = lr * lr + li * li
    nr = lbr - 1.0
    fr = (nr * lr + lbi * li) / den
    fi = (lbi * lr - nr * li) / den
    return lr, li, dt, lbr, lbi, fr, fi


def _lane_row(v):
    return jnp.concatenate([v[g:g + 1, :] for g in range(v.shape[0])], axis=1)


def _ssm_fill_maps(d, prm, tmp_ref, maps):
    a_re_ref, a_im_ref, ldt_ref, bt_re_ref, bt_im_ref, c_re_ref, c_im_ref = prm
    _, _, _, lbr, lbi, fr, fi = _discretise(a_re_ref[d], a_im_ref[d], ldt_ref[d])

    def fill(dst, piece):
        tmp_ref[...] = jnp.zeros_like(tmp_ref)
        for g in range(SLAB_G):
            tmp_ref[g * SSM_H:(g + 1) * SSM_H, g * SSM_P:(g + 1) * SSM_P] = piece(g)
        dst[...] = tmp_ref[...].astype(BF16)

    wbr, wbi, wcr, wci = maps
    fill(wbr, lambda g: fr[g:g + 1] * bt_re_ref[d, g] - fi[g:g + 1] * bt_im_ref[d, g])
    fill(wbi, lambda g: fr[g:g + 1] * bt_im_ref[d, g] + fi[g:g + 1] * bt_re_ref[d, g])
    fill(wcr, lambda g: c_re_ref[d, g])
    fill(wci, lambda g: c_im_ref[d, g])
    return _lane_row(lbr), _lane_row(lbi)


def _ssm_param_specs():
    pole = pl.BlockSpec((2, SLAB_G, SSM_P), lambda j: (0, j, 0))
    step = pl.BlockSpec((2, SLAB_G, 1), lambda j: (0, j, 0))
    mat = pl.BlockSpec((2, SLAB_G, SSM_H, SSM_P), lambda j: (0, j, 0, 0))
    return [pole, pole, step, mat, mat, mat, mat]


_MAP_SCRATCH = [pltpu.VMEM((SLAB, SLAB_S), F32)] + [pltpu.VMEM((SLAB, SLAB_S), BF16)] * 4
_NT = (((1,), (1,)), ((), ()))


def _ssm_fwd(u, prm, dskip):
    L, C = u.shape
    n_rows = L // SEG
    tc = _pick(L, (512, 256))
    u_spec = pl.BlockSpec((L, SLAB), lambda j: (0, j))
    d_spec = pl.BlockSpec((1, SLAB), lambda j: (0, j))

    def body(u_ref, *rest):
        prm_refs, d_ref, y_ref = rest[:7], rest[7], rest[8]
        tmp_ref, maps, xr_ref, xi_ref = rest[9], rest[10:14], rest[14], rest[15]
        wbr, wbi, wcr, wci = maps
        y_ref[...] = u_ref[...] * d_ref[...]
        for d in range(2):
            lam_r, lam_i = _ssm_fill_maps(d, prm_refs, tmp_ref, maps)

            def inp(c, _):
                sl = pl.ds(pl.multiple_of(c * tc, tc), tc)
                ub = u_ref[sl, :].astype(BF16)
                xr_ref[sl, :] = jnp.dot(ub, wbr[...], preferred_element_type=F32)
                xi_ref[sl, :] = jnp.dot(ub, wbi[...], preferred_element_type=F32)
                return 0

            lax.fori_loop(0, L // tc, inp, 0)
            ar = jnp.broadcast_to(lam_r, (SEG, SLAB_S))
            ai = jnp.broadcast_to(lam_i, (SEG, SLAB_S))
            _seg_scan(xr_ref, xi_ref, ar, ai, d == 1, n_rows)

            def outp(c, _):
                sl = pl.ds(pl.multiple_of(c * tc, tc), tc)
                y_ref[sl, :] += (
                    lax.dot_general(xr_ref[sl, :].astype(BF16), wcr[...], _NT, preferred_element_type=F32)
                    - lax.dot_general(xi_ref[sl, :].astype(BF16), wci[...], _NT, preferred_element_type=F32))
                return 0

            lax.fori_loop(0, L // tc, outp, 0)

    return pl.pallas_call(
        body, name="ssm_fwd", grid=(C // SLAB,),
        in_specs=[u_spec] + _ssm_param_specs() + [d_spec],
        out_specs=u_spec, out_shape=jax.ShapeDtypeStruct((L, C), F32),
        scratch_shapes=_MAP_SCRATCH + [pltpu.VMEM((L, SLAB_S), F32)] * 2,
        compiler_params=_cparams(("arbitrary",)),
    )(u, *prm, dskip)


def _ssm_bwd(u, dy, prm, dskip):
    L, C = u.shape
    n_rows = L // SEG
    n_slab = C // SLAB
    tc = _pick(L, (512, 256))
    u_spec = pl.BlockSpec((L, SLAB), lambda j: (0, j))
    d_spec = pl.BlockSpec((1, SLAB), lambda j: (0, j))
    pg_spec = pl.BlockSpec((1, 2, PG_ROWS, SLAB_S), lambda j: (j, 0, 0, 0))

    def body(u_ref, dy_ref, *rest):
        prm_refs, d_ref, du_ref, pg_ref = rest[:7], rest[7], rest[8], rest[9]
        tmp_ref, maps, acc_ref = rest[10], rest[11:15], rest[15]
        xr_ref, xi_ref, gr_ref, gi_ref = rest[16:20]
        wbr, wbi, wcr, wci = maps
        du_ref[...] = dy_ref[...] * d_ref[...]
        pg_ref[...] = jnp.zeros_like(pg_ref)
        pg_ref[0, 0, 66:67, 0:SLAB] = _colsum(dy_ref[...] * u_ref[...])
        for d in range(2):
            lam_r, lam_i = _ssm_fill_maps(d, prm_refs, tmp_ref, maps)

            def inp(c, _):
                sl = pl.ds(pl.multiple_of(c * tc, tc), tc)
                ub = u_ref[sl, :].astype(BF16)
                dyb = dy_ref[sl, :].astype(BF16)
                xr_ref[sl, :] = jnp.dot(ub, wbr[...], preferred_element_type=F32)
                xi_ref[sl, :] = jnp.dot(ub, wbi[...], preferred_element_type=F32)
                gr_ref[sl, :] = jnp.dot(dyb, wcr[...], preferred_element_type=F32)
                gi_ref[sl, :] = -jnp.dot(dyb, wci[...], preferred_element_type=F32)
                return 0

            lax.fori_loop(0, L // tc, inp, 0)
            ar = jnp.broadcast_to(lam_r, (SEG, SLAB_S))
            ai = jnp.broadcast_to(lam_i, (SEG, SLAB_S))
            inr, ini, _ = _seg_scan(xr_ref, xi_ref, ar, ai, d == 1, n_rows)

            def pole(rr, lr, li, acc, last):
                if last:
                    pr, pi = inr, ini
                else:
                    nb = _rows8(rr + 1 if d == 1 else rr - 1)
                    pr, pi = xr_ref[nb, :], xi_ref[nb, :]
                return acc[0] + lr * pr + li * pi, acc[1] + li * pr - lr * pi

            zero = jnp.zeros((SEG, SLAB_S), F32)
            _, _, (accr, acci) = _seg_scan(gr_ref, gi_ref, ar, -ai, d == 0, n_rows, pole, (zero, zero))
            pg_ref[0, d, 64:65, :] = _colsum(accr)
            pg_ref[0, d, 65:66, :] = _colsum(acci)

            acc_ref[...] = jnp.zeros_like(acc_ref)

            def outp(c, _):
                sl = pl.ds(pl.multiple_of(c * tc, tc), tc)
                lrb, lib = gr_ref[sl, :].astype(BF16), gi_ref[sl, :].astype(BF16)
                du_ref[sl, :] += (lax.dot_general(lrb, wbr[...], _NT, preferred_element_type=F32)
                                  + lax.dot_general(lib, wbi[...], _NT, preferred_element_type=F32))
                ut = u_ref[sl, :].T.astype(BF16)
                dyt = dy_ref[sl, :].T.astype(BF16)
                acc_ref[0] += jnp.dot(ut, lrb, preferred_element_type=F32)
                acc_ref[1] += jnp.dot(ut, lib, preferred_element_type=F32)
                acc_ref[2] += jnp.dot(dyt, xr_ref[sl, :].astype(BF16), preferred_element_type=F32)
                acc_ref[3] -= jnp.dot(dyt, xi_ref[sl, :].astype(BF16), preferred_element_type=F32)
                return 0

            lax.fori_loop(0, L // tc, outp, 0)
            for m in range(4):
                for g in range(SLAB_G):
                    lanes = slice(g * SSM_P, (g + 1) * SSM_P)
                    pg_ref[0, d, m * SSM_H:(m + 1) * SSM_H, lanes] = acc_ref[m, g * SSM_H:(g + 1) * SSM_H, lanes]

    return pl.pallas_call(
        body, name="ssm_bwd", grid=(n_slab,),
        in_specs=[u_spec, u_spec] + _ssm_param_specs() + [d_spec],
        out_specs=[u_spec, pg_spec],
        out_shape=[jax.ShapeDtypeStruct((L, C), F32), jax.ShapeDtypeStruct((n_slab, 2, PG_ROWS, SLAB_S), F32)],
        scratch_shapes=_MAP_SCRATCH + [pltpu.VMEM((4, SLAB, SLAB_S), F32)] + [pltpu.VMEM((L, SLAB_S), F32)] * 4,
        compiler_params=_cparams(("arbitrary",), 60 << 20),
    )(u, dy, *prm, dskip)


def _ssm_param_grads(pg, prm):
    n_slab = pg.shape[0]
    G = n_slab * SLAB_G
    pg_spec = pl.BlockSpec((1, 2, PG_ROWS, SLAB_S), lambda j: (j, 0, 0, 0))
    pole, _, step, mat = _ssm_param_specs()[:4]

    def body(pg_ref, a_re_ref, a_im_ref, ldt_ref, bt_re_ref, bt_im_ref,
             dbr_ref, dbi_ref, dcr_ref, dci_ref, dar_ref, dai_ref, dldt_ref, dd_ref):
        dd_ref[...] = pg_ref[0, 0, 66:67, 0:SLAB]
        for d in range(2):
            a_r = a_re_ref[d]
            lr, li, dt, lbr, lbi, f_r, f_i = _discretise(a_r, a_im_ref[d], ldt_ref[d])
            gfr_rows, gfi_rows, glr_rows, gli_rows = [], [], [], []
            for g in range(SLAB_G):
                lanes = slice(g * SSM_P, (g + 1) * SSM_P)
                gbr, gbi = pg_ref[0, d, 0:SSM_H, lanes], pg_ref[0, d, SSM_H:2 * SSM_H, lanes]
                b_r, b_i = bt_re_ref[d, g], bt_im_ref[d, g]
                fr, fi = f_r[g:g + 1], f_i[g:g + 1]
                dbr_ref[d, g] = fr * gbr + fi * gbi
                dbi_ref[d, g] = fr * gbi - fi * gbr
                gfr_rows.append(_colsum(gbr * b_r + gbi * b_i))
                gfi_rows.append(_colsum(gbi * b_r - gbr * b_i))
                dcr_ref[d, g] = pg_ref[0, d, 2 * SSM_H:3 * SSM_H, lanes]
                dci_ref[d, g] = pg_ref[0, d, 3 * SSM_H:4 * SSM_H, lanes]
                glr_rows.append(pg_ref[0, d, 64:65, lanes])
                gli_rows.append(pg_ref[0, d, 65:66, lanes])
            gfr, gfi = jnp.concatenate(gfr_rows, axis=0), jnp.concatenate(gfi_rows, axis=0)
            glr, gli = jnp.concatenate(glr_rows, axis=0), jnp.concatenate(gli_rows, axis=0)
            den = lr * lr + li * li
            ir, ii = lr / den, -li / den
            tr, ti = _cmul(ir, -ii, gfr, gfi)
            glbr, glbi = glr + tr, gli + ti
            qr, qi = _cmul(f_r, f_i, ir, ii)
            dlr, dli = _cmul(-qr, qi, gfr, gfi)
            zr, zi = _cmul(lbr, -lbi, glbr, glbi)
            dlr = dlr + dt * zr
            dli = dli + dt * zi
            dar_ref[d] = jnp.where(a_r < -1e-4, dlr, jnp.where(a_r == -1e-4, 0.5 * dlr, 0.0))
            dai_ref[d] = dli
            dldt_ref[d] = jnp.sum(lr * zr + li * zi, axis=-1, keepdims=True) * dt

    a_re, a_im, ldt, bt_re, bt_im = prm[:5]
    mshape = jax.ShapeDtypeStruct(bt_re.shape, F32)
    pshape = jax.ShapeDtypeStruct(a_re.shape, F32)
    return pl.pallas_call(
        body, name="ssm_param_grads", grid=(n_slab,),
        in_specs=[pg_spec, pole, pole, step, mat, mat],
        out_specs=[mat, mat, mat, mat, pole, pole, step, pl.BlockSpec((1, SLAB), lambda j: (0, j))],
        out_shape=[mshape, mshape, mshape, mshape, pshape, pshape, jax.ShapeDtypeStruct(ldt.shape, F32),
                   jax.ShapeDtypeStruct((1, n_slab * SLAB), F32)],
        compiler_params=_cparams(("arbitrary",)),
    )(pg, a_re, a_im, ldt, bt_re, bt_im)


def _peer(k, x, y, c):
    return (1 - x if k & 4 else x, 1 - y if k & 2 else y, 1 - c if k & 1 else c)


def _dev_index(pos):
    return 4 * pos[0] + 2 * pos[1] + pos[2]


def _all_gather(xs, name):
    n = len(xs)
    any_spec = pl.BlockSpec(memory_space=pl.ANY)

    def body(*refs):
        x_refs, out_refs = refs[:n], refs[n:2 * n]
        send_sems, recv_sems, local_sems = refs[2 * n:]
        x, y, c = lax.axis_index("x"), lax.axis_index("y"), lax.axis_index("c")
        me, sibling = (x, y, c), (x, y, 1 - c)
        chips = [(1 - x, y), (x, 1 - y), (1 - x, 1 - y)]

        def copy(a, k, block, to, src=None):
            dst = out_refs[a].at[_dev_index(block)]
            return pltpu.make_async_remote_copy(
                src_ref=dst if src is None else src, dst_ref=dst,
                send_sem=send_sems.at[a, k], recv_sem=recv_sems.at[a, k],
                device_id=to, device_id_type=pl.DeviceIdType.MESH)

        mine = [pltpu.make_async_copy(x_refs[a], out_refs[a].at[_dev_index(me)], local_sems.at[a]) for a in range(n)]
        for cp in mine:
            cp.start()
        first = []
        for a in range(n):
            first.append(copy(a, 0, me, sibling, src=x_refs[a]))
            first += [copy(a, 1 + j, me, (*chip, c), src=x_refs[a]) for j, chip in enumerate(chips)]
        for cp in first:
            cp.start()
        passed = []
        for j, chip in enumerate(chips):
            for a in range(n):
                copy(a, 1 + j, (*chip, c), me).wait_recv()
                fwd = copy(a, 4 + j, (*chip, c), sibling)
                fwd.start()
                passed.append(fwd)
        for a in range(n):
            copy(a, 0, sibling, me).wait_recv()
            for j, chip in enumerate(chips):
                copy(a, 4 + j, (*chip, 1 - c), me).wait_recv()
        for cp in first + passed:
            cp.wait_send()
        for cp in mine:
            cp.wait()

    return pl.pallas_call(
        body, name=name,
        out_shape=[jax.ShapeDtypeStruct((N_DEV,) + v.shape, v.dtype) for v in xs],
        in_specs=[any_spec] * n, out_specs=[any_spec] * n,
        scratch_shapes=[pltpu.SemaphoreType.DMA((n, 7)), pltpu.SemaphoreType.DMA((n, 7)),
                        pltpu.SemaphoreType.DMA((n,))],
    )(*xs)


def _sum_blocks(x, name):
    _, R, W = x.shape

    def body(x_ref, o_ref):
        acc = x_ref[0].astype(F32)
        for d in range(1, N_DEV):
            acc = acc + x_ref[d].astype(F32)
        o_ref[...] = acc

    return pl.pallas_call(body, name=name, out_shape=jax.ShapeDtypeStruct((R, W), F32),
                          compiler_params=pltpu.CompilerParams(vmem_limit_bytes=VMEM_LIMIT))(x)


def _adam_update(w, g, m, v):
    mn = ADAM_B1 * m + (1.0 - ADAM_B1) * g
    vn = ADAM_B2 * v + (1.0 - ADAM_B2) * (g * g)
    m_hat = mn / (1.0 - ADAM_B1 ** ADAM_STEP)
    v_hat = vn / (1.0 - ADAM_B2 ** ADAM_STEP)
    return -ADAM_LR * (m_hat / (jnp.sqrt(v_hat) + ADAM_EPS) + ADAM_WD * w), mn, vn


def _row_tile(R, W, budget):
    padded = -(-W // LANES) * LANES * 4
    if R * padded <= budget:
        return R
    return _pick(R, [t for t in (2048, 1024, 512, 256, 128, 64, 32, 16, 8) if t * padded <= budget])


def _adamw(w, g, m, v, name):
    R, W = w.shape
    tr = _row_tile(R, W, 1 << 20)

    def body(w_ref, g_ref, m_ref, v_ref, d_ref, mo_ref, vo_ref):
        d_ref[...], mo_ref[...], vo_ref[...] = _adam_update(w_ref[...], g_ref[...], m_ref[...], v_ref[...])

    spec = pl.BlockSpec((tr, W), lambda i: (i, 0))
    shp = jax.ShapeDtypeStruct((R, W), F32)
    return pl.pallas_call(
        body, name=name, grid=(R // tr,), in_specs=[spec] * 4, out_specs=[spec] * 3, out_shape=[shp] * 3,
        compiler_params=_cparams(("arbitrary",)),
    )(w, g, m, v)


def _adamw_reduce(w, land, m, v, name):
    R, W = w.shape
    tr = _row_tile(R, W, 1 << 20)

    def body(w_ref, l_ref, m_ref, v_ref, g_ref, d_ref, mo_ref, vo_ref):
        g = l_ref[0].astype(F32)
        for d in range(1, N_DEV):
            g = g + l_ref[d].astype(F32)
        g_ref[...] = g
        d_ref[...], mo_ref[...], vo_ref[...] = _adam_update(w_ref[...], g, m_ref[...], v_ref[...])

    spec = pl.BlockSpec((tr, W), lambda i: (i, 0))
    lspec = pl.BlockSpec((N_DEV, tr, W), lambda i: (0, i, 0))
    shp = jax.ShapeDtypeStruct((R, W), F32)
    return pl.pallas_call(
        body, name=name, grid=(R // tr,), in_specs=[spec, lspec, spec, spec], out_specs=[spec] * 4,
        out_shape=[shp] * 4, compiler_params=_cparams(("arbitrary",)),
    )(w, land, m, v)


def _gelu(v):
    c = math.sqrt(2.0 / math.pi)
    return 0.5 * v * (1.0 + jnp.tanh(c * (v + 0.044715 * v * v * v)))


def _gelu_grad(v):
    c = math.sqrt(2.0 / math.pi)
    t = jnp.tanh(c * (v + 0.044715 * v * v * v))
    return 0.5 * (1.0 + t) + 0.5 * v * (1.0 - t * t) * c * (1.0 + 3.0 * 0.044715 * v * v)


def kernel(x, p, norm_mix, w_in, q_norm, k_norm, ssm_a_re, ssm_a_im, ssm_log_dt, ssm_b_re, ssm_b_im, ssm_c_re, ssm_c_im, ssm_d, w_glu, b_glu, w_out, norm_ple, w_ple_gate, w_ple_proj, norm_final, loss_target, m_norm_mix, m_w_in, m_q_norm, m_k_norm, m_ssm_a_re, m_ssm_a_im, m_ssm_log_dt, m_ssm_b_re, m_ssm_b_im, m_ssm_c_re, m_ssm_c_im, m_ssm_d, m_w_glu, m_b_glu, m_w_out, m_norm_ple, m_w_ple_gate, m_w_ple_proj, m_norm_final, v_norm_mix, v_w_in, v_q_norm, v_k_norm, v_ssm_a_re, v_ssm_a_im, v_ssm_log_dt, v_ssm_b_re, v_ssm_b_im, v_ssm_c_re, v_ssm_c_im, v_ssm_d, v_w_glu, v_b_glu, v_w_out, v_norm_ple, v_w_ple_gate, v_w_ple_proj, v_norm_final):
    L, D = x.shape[1], x.shape[2]
    D_SSM = ssm_d.shape[1]
    G = D_SSM // SSM_H
    n_slab = D_SSM // SLAB
    n_in = w_in.shape[2]
    D_IN = n_in * N_DEV
    n_pp = w_ple_proj.shape[2]
    n_glu = w_glu.shape[2]
    xs = x[0]
    ps = p[0, 0]
    tgt = loss_target[0]

    (win_t3,) = _all_gather([w_in[0].T.astype(BF16)], "gather_w_in")
    win_t = win_t3.reshape(D_IN, D)
    later_weights = _Carry("gather", [w_glu[0].astype(BF16), w_out[0].astype(BF16), w_ple_gate[0].astype(BF16),
                                      w_ple_proj[0].astype(BF16)])

    ssm_prm = (ssm_a_re[0], ssm_a_im[0], ssm_log_dt[0].reshape(2, G, 1),
               ssm_b_re[0].transpose(0, 1, 3, 2), ssm_b_im[0].transpose(0, 1, 3, 2), ssm_c_re[0], ssm_c_im[0])

    cos, sin = _rope_tables(L)
    hn = _norm_in(xs, norm_mix, "norm_mix")
    ZT = 512
    zp_tile = lambda j: jnp.where(j < 2, j, jnp.where(j < D_IN // ZT - 1, j + 1, 2))
    z = _mm(hn, win_t, "nt", "in_proj", out_dtype=BF16, n_tiles=(ZT, zp_tile))
    qr, kr, vb, kt = _qkv_prep(z, cos, sin, q_norm, k_norm)
    o, lse, (wglu3, wout3, wpg3, wpp3) = _attn_fwd(qr, kr, vb, later_weights)
    wout = wout3.reshape(-1, D)
    wpg = wpg3.reshape(-1, D)
    u_off = 2 * D_ATTN
    u_perm = _seg_perm(z[:, u_off:u_off + D_SSM]).astype(F32)
    ys = _seg_unperm(_ssm_fwd(u_perm, ssm_prm, ssm_d))

    tm = _pick(L, (256,))

    def gelu_body(y_ref, o_ref):
        o_ref[...] = _gelu(y_ref[...]).astype(BF16)

    (gy,) = _rowcall(gelu_body, "gelu", L, tm, [(ys, _rspec(tm, D_SSM))], [(D_SSM, BF16)])
    glu = _mm(gy, wglu3, "nn", "glu_proj", out_dtype=BF16, bias=b_glu, b_blk=True)

    def mix_body(o_ref, ga_ref, gla_ref, glb_ref, gs_ref, cat_ref):
        ga, gs = _f32(ga_ref), _f32(gs_ref)
        cat_ref[:, :D_ATTN] = (o_ref[...] * ga * _sigmoid(ga)).astype(BF16)
        cat_ref[:, D_ATTN:] = (_f32(gla_ref) * _sigmoid(_f32(glb_ref)) * gs * _sigmoid(gs)).astype(BF16)

    (cat,) = _rowcall(mix_body, "mix", L, tm,
                      [(o, _rspec(tm, D_ATTN)), (z, _rspec(tm, D_ATTN, 1)), (glu, _rspec(tm, D_SSM, 0)),
                       (glu, _rspec(tm, D_SSM, 1)), (z, _rspec(tm, D_SSM, 3))], [(D_ATTN + D_SSM, BF16)])
    h1 = _mm(cat, wout, "nn", "out_proj", add=xs)
    n2 = _norm_in(h1, norm_ple, "norm_ple")
    gpre = _mm(n2, wpg, "nn", "ple_gate", out_dtype=BF16)
    pb = ps.astype(BF16)
    pp = _mm(pb, wpp3, "nn", "ple_proj", out_dtype=BF16, b_blk=True)

    nf = norm_final.reshape(1, D)

    def tail_body(h1_ref, gp_ref, pp_ref, t_ref, g_ref, dh2_ref, dpp_ref, dsg_ref, loss_ref, dg_ref):
        gate = _sigmoid(_f32(gp_ref))
        ppv = _f32(pp_ref)
        h2 = h1_ref[...] + gate * ppv
        r = _rms(h2)
        hh = h2 * r
        err = hh * g_ref[...] - t_ref[...]
        _acc(loss_ref, jnp.broadcast_to(0.5 * jnp.sum(jnp.mean(err * err, axis=-1, keepdims=True)), loss_ref.shape))
        dy = err * (1.0 / D)
        _acc(dg_ref, _colsum(dy * hh))
        dh2 = _rms_bwd(dy, hh, r, g_ref[...])
        dh2_ref[...] = dh2
        dpp_ref[...] = (dh2 * gate).astype(BF16)
        dsg_ref[...] = (dh2 * ppv * gate * (1.0 - gate)).astype(BF16)

    dh2, dpp, dsg, loss_acc, d_nf = _rowcall(
        tail_body, "tail", L, tm,
        [(h1, _rspec(tm, D)), (gpre, _rspec(tm, D)), (pp, _rspec(tm, D)), (tgt, _rspec(tm, D)), (nf, _fspec(nf.shape))],
        [(D, F32), (D, BF16), (D, BF16)], [(1, LANES), (1, D)])
    loss = lax.psum(loss_acc[0, 0], ("x", "y", "c"))

    g_wpp3 = _mm(pb, dpp, "tn", "d_ple_proj", out_dtype=BF16, out_blk=n_pp)
    g_wpg = _mm(n2, dsg, "tn", "d_ple_gate", out_dtype=BF16)
    dn2 = _mm(dsg, wpg, "nt", "d_norm_ple_in", out_dtype=BF16)

    def ple_bwd_body(h1_ref, dn_ref, dh2_ref, g_ref, dh1_ref, dh1b_ref, dg_ref):
        h1v = h1_ref[...]
        r = _rms(h1v)
        hh = h1v * r
        dn = _f32(dn_ref)
        _acc(dg_ref, _colsum(dn * hh))
        dh1 = dh2_ref[...] + _rms_bwd(dn, hh, r, g_ref[...])
        dh1_ref[...] = dh1
        dh1b_ref[...] = dh1.astype(BF16)

    dh1, dh1b, d_nple = _rowcall(
        ple_bwd_body, "ple_bwd", L, tm,
        [(h1, _rspec(tm, D)), (dn2, _rspec(tm, D)), (dh2, _rspec(tm, D)), (norm_ple, _fspec(norm_ple.shape))],
        [(D, F32), (D, BF16)], [(1, D)])

    dcat = _mm(dh1b, wout, "nt", "d_cat", out_dtype=BF16)
    g_wout = _mm(cat, dh1b, "tn", "d_out_proj", out_dtype=BF16)

    def mix_bwd_body(dca_ref, dcs_ref, o_ref, ga_ref, gla_ref, glb_ref, gs_ref,
                     do_ref, dga_ref, dgs_ref, dglu_ref, db_ref):
        dca, dcs, ga, gs = _f32(dca_ref), _f32(dcs_ref), _f32(ga_ref), _f32(gs_ref)
        sa, ss, sb = _sigmoid(ga), _sigmoid(gs), _sigmoid(_f32(glb_ref))
        gla = _f32(gla_ref)
        do_ref[...] = (dca * ga * sa).astype(BF16)
        dga_ref[...] = (dca * o_ref[...] * sa * (1.0 + ga * (1.0 - sa))).astype(BF16)
        dgs_ref[...] = (dcs * gla * sb * ss * (1.0 + gs * (1.0 - ss))).astype(BF16)
        dy2 = dcs * gs * ss
        da, db = dy2 * sb, dy2 * gla * sb * (1.0 - sb)
        dglu_ref[:, :D_SSM] = da.astype(BF16)
        dglu_ref[:, D_SSM:] = db.astype(BF16)
        _acc(db_ref, jnp.concatenate([_colsum(da), _colsum(db)], axis=-1))

    do, dga, dgs, dglu, g_bglu = _rowcall(
        mix_bwd_body, "mix_bwd", L, tm,
        [(dcat, _rspec(tm, D_ATTN, 0)), (dcat, _rspec(tm, D_SSM, 1)), (o, _rspec(tm, D_ATTN)),
         (z, _rspec(tm, D_ATTN, 1)), (glu, _rspec(tm, D_SSM, 0)), (glu, _rspec(tm, D_SSM, 1)),
         (z, _rspec(tm, D_SSM, 3))],
        [(D_ATTN, BF16), (D_ATTN, BF16), (D_SSM, BF16), (2 * D_SSM, BF16)], [(1, 2 * D_SSM)])

    g_wglu3 = _mm(gy, dglu, "tn", "d_glu_proj", out_dtype=BF16, out_blk=n_glu)
    dgy = _mm(dglu, wglu3, "nt", "d_gelu_out", out_dtype=BF16, b_blk=True)

    def gelu_bwd_body(dg_ref, y_ref, o_ref):
        o_ref[...] = _f32(dg_ref) * _gelu_grad(y_ref[...])

    (dys,) = _rowcall(gelu_bwd_body, "gelu_bwd", L, tm,
                      [(dgy, _rspec(tm, D_SSM)), (ys, _rspec(tm, D_SSM))], [(D_SSM, F32)])
    du_perm, pg = _ssm_bwd(u_perm, _seg_perm(dys), ssm_prm, ssm_d)
    du = _seg_unperm(du_perm)

    dqs, dkr, dvv, (l_wglu, l_wout, l_wpg, l_wpp) = _attn_bwd(
        qr, kr, vb, kt, do, o, lse,
        _Carry("a2a", [g_wglu3, g_wout.reshape(N_DEV, -1, D), g_wpg.reshape(N_DEV, -1, D), g_wpp3]))

    scale = HEAD_DIM ** -0.5
    kblk = 4 * D_ATTN // D_KV
    tmq = _pick(L, (512, 256))

    def qkv_bwd_body(dq_ref, dk_ref, dv_ref, q_ref, k_ref, cos_ref, sin_ref, qn_ref, kn_ref,
                     dqo_ref, dko_ref, dvo_ref, dqn_ref, dkn_ref):
        c, s = cos_ref[...], sin_ref[...]

        def head(g, xh, w):
            dn = g * c + _partner(g * s)
            r = _rms(xh)
            xhat = xh * r
            return _rms_bwd(dn, xhat, r, w), _colsum(dn * xhat)

        dqn = jnp.zeros((1, HEAD_DIM), F32)
        for h in range(N_HEADS):
            sl = slice(h * HEAD_DIM, (h + 1) * HEAD_DIM)
            dx, dw = head(dq_ref[:, sl] * scale, q_ref[:, sl].astype(F32), qn_ref[...])
            dqo_ref[:, sl] = dx.astype(BF16)
            dqn = dqn + dw
        dkn = jnp.zeros((1, HEAD_DIM), F32)
        for h in range(N_KV):
            sl = slice(h * HEAD_DIM, (h + 1) * HEAD_DIM)
            dx, dw = head(dk_ref[:, sl], k_ref[:, sl].astype(F32), kn_ref[...])
            dko_ref[:, sl] = dx.astype(BF16)
            dkn = dkn + dw
        dvo_ref[...] = dv_ref[...].astype(BF16)
        _acc(dqn_ref, dqn)
        _acc(dkn_ref, dkn)

    dq, dk, dv, g_qn, g_kn = _rowcall(
        qkv_bwd_body, "qkv_bwd", L, tmq,
        [(dqs, _rspec(tmq, D_ATTN)), (dkr, _rspec(tmq, D_KV)), (dvv, _rspec(tmq, D_KV)),
         (z, _rspec(tmq, D_ATTN, 0)), (z, _rspec(tmq, D_KV, kblk)), (cos, _rspec(tmq, HEAD_DIM)),
         (sin, _rspec(tmq, HEAD_DIM)), (q_norm, _fspec(q_norm.shape)), (k_norm, _fspec(k_norm.shape))],
        [(D_ATTN, BF16), (D_KV, BF16), (D_KV, BF16)], [(1, HEAD_DIM), (1, HEAD_DIM)])

    dz = jnp.concatenate([dq, dk, dv, dga, du.astype(BF16), dgs], axis=1)
    g_win_t = _mm(dz, hn, "tn", "d_in_proj", out_dtype=BF16)
    pg_send = pg.reshape(N_DEV, (n_slab // N_DEV) * 2 * PG_ROWS, SLAB_S)
    dhn, (l_win_t, l_pg) = _mm(dz, win_t, "nn", "d_norm_mix_in", out_dtype=BF16,
                               carry=_Carry("a2a", [g_win_t.reshape(N_DEV, n_in, D), pg_send]))

    def in_bwd_body(x_ref, dn_ref, dh1_ref, g_ref, dx_ref, dg_ref):
        xv = x_ref[...]
        r = _rms(xv)
        hh = xv * r
        dn = _f32(dn_ref)
        _acc(dg_ref, _colsum(dn * hh))
        dx_ref[...] = dh1_ref[...] + _rms_bwd(dn, hh, r, g_ref[...])

    grad_x, g_nmix = _rowcall(
        in_bwd_body, "in_bwd", L, tm,
        [(xs, _rspec(tm, D)), (dhn, _rspec(tm, D)), (dh1, _rspec(tm, D)), (norm_mix, _fspec(norm_mix.shape))],
        [(D, F32)], [(1, D)])

    tiny_parts = [g_nmix, g_bglu, d_nple, d_nf, g_qn, g_kn]
    tiny_flat = jnp.concatenate([t.reshape(-1) for t in tiny_parts])
    tiny_rows = -(-tiny_flat.shape[0] // (8 * LANES)) * 8
    tiny = jnp.pad(tiny_flat, (0, tiny_rows * LANES - tiny_flat.shape[0])).reshape(tiny_rows, LANES)
    pg_sum = _sum_blocks(l_pg, "sum_ssm_grads")
    pg_all, tiny_all = _all_gather([pg_sum, tiny], "gather_small_grads")
    (g_bt_re, g_bt_im, g_c_re, g_c_im, g_a_re, g_a_im, g_ldt, g_skip) = _ssm_param_grads(
        pg_all.reshape(n_slab, 2, PG_ROWS, SLAB_S), ssm_prm)
    tiny_sum = _sum_blocks(tiny_all, "sum_tiny_grads").reshape(-1)
    tiny_grads, off = [], 0
    for t in tiny_parts:
        tiny_grads.append(tiny_sum[off:off + t.size].reshape(t.shape))
        off += t.size
    r_nmix, r_bglu, r_nple, r_nf, r_qn, r_kn = tiny_grads

    grads, deltas, new_ms, new_vs = {}, {}, {}, {}
    outs = _adamw_reduce(w_in[0].T, l_win_t, m_w_in[0].T, v_w_in[0].T, "adamw_w_in")
    grads["w_in"], deltas["w_in"], new_ms["w_in"], new_vs["w_in"] = [t.T[None] for t in outs]
    big = [("w_glu", w_glu, l_wglu, m_w_glu, v_w_glu),
           ("w_out", w_out, l_wout, m_w_out, v_w_out), ("w_ple_gate", w_ple_gate, l_wpg, m_w_ple_gate, v_w_ple_gate),
           ("w_ple_proj", w_ple_proj, l_wpp, m_w_ple_proj, v_w_ple_proj)]
    for name, w, ld, m, v in big:
        shp = w.shape
        outs = _adamw_reduce(w[0], ld, m[0], v[0], "adamw_" + name)
        grads[name], deltas[name], new_ms[name], new_vs[name] = [t.reshape(shp) for t in outs]
    bt2 = (2 * G * SSM_H, SSM_P)
    for name, w, g, m, v in (("ssm_b_re", ssm_b_re, g_bt_re, m_ssm_b_re, v_ssm_b_re),
                             ("ssm_b_im", ssm_b_im, g_bt_im, m_ssm_b_im, v_ssm_b_im)):
        to2 = lambda t: t[0].transpose(0, 1, 3, 2).reshape(bt2)
        back = lambda t: t.reshape(2, G, SSM_H, SSM_P).transpose(0, 1, 3, 2)[None]
        outs = _adamw(to2(w), g.reshape(bt2), to2(m), to2(v), "adamw_" + name)
        grads[name] = back(g)
        deltas[name], new_ms[name], new_vs[name] = [back(t) for t in outs]
    small = [("norm_mix", norm_mix, r_nmix, m_norm_mix, v_norm_mix, (1, D)),
             ("q_norm", q_norm, r_qn, m_q_norm, v_q_norm, (1, HEAD_DIM)),
             ("k_norm", k_norm, r_kn, m_k_norm, v_k_norm, (1, HEAD_DIM)),
             ("ssm_a_re", ssm_a_re, g_a_re, m_ssm_a_re, v_ssm_a_re, (2 * G, SSM_P)),
             ("ssm_a_im", ssm_a_im, g_a_im, m_ssm_a_im, v_ssm_a_im, (2 * G, SSM_P)),
             ("ssm_log_dt", ssm_log_dt, g_ldt, m_ssm_log_dt, v_ssm_log_dt, (2, G)),
             ("ssm_c_re", ssm_c_re, g_c_re, m_ssm_c_re, v_ssm_c_re, (2 * G * SSM_H, SSM_P)),
             ("ssm_c_im", ssm_c_im, g_c_im, m_ssm_c_im, v_ssm_c_im, (2 * G * SSM_H, SSM_P)),
             ("ssm_d", ssm_d, g_skip, m_ssm_d, v_ssm_d, (1, D_SSM)),
             ("b_glu", b_glu, r_bglu, m_b_glu, v_b_glu, (1, 2 * D_SSM)),
             ("norm_ple", norm_ple, r_nple, m_norm_ple, v_norm_ple, (1, D)),
             ("norm_final", norm_final, r_nf, m_norm_final, v_norm_final, (1, D))]
    for name, w, g, m, v, s2 in small:
        shp = w.shape
        outs = _adamw(w.reshape(s2), g.reshape(s2), m.reshape(s2), v.reshape(s2), "adamw_" + name)
        grads[name] = g.reshape(shp)
        deltas[name], new_ms[name], new_vs[name] = [t.reshape(shp) for t in outs]

    order = ["norm_mix", "w_in", "q_norm", "k_norm", "ssm_a_re", "ssm_a_im", "ssm_log_dt", "ssm_b_re", "ssm_b_im",
             "ssm_c_re", "ssm_c_im", "ssm_d", "w_glu", "b_glu", "w_out", "norm_ple", "w_ple_gate", "w_ple_proj",
             "norm_final"]
    return (loss, grad_x[None], *[grads[k] for k in order], *[deltas[k] for k in order],
            *[new_ms[k] for k in order], *[new_vs[k] for k in order])
```

```python
import functools
import math

import numpy as np
import jax
import jax.numpy as jnp
from jax import lax
from jax.experimental import pallas as pl
from jax.experimental.pallas import tpu as pltpu

F32 = jnp.float32
BF16 = jnp.bfloat16

N_DEV = 8
EPS = 1e-6
GRID_W = 64
ROPE_THETA = 10000.0
HEAD_DIM = 128
N_HEADS = 8
N_KV = 2
REP = N_HEADS // N_KV
D_ATTN = N_HEADS * HEAD_DIM
D_KV = N_KV * HEAD_DIM
SSM_H = 16
SSM_P = 64
SLAB = 128
SLAB_G = SLAB // SSM_H
SLAB_S = SLAB_G * SSM_P
SEG = 8
LANES = 128
PG_ROWS = 72
VMEM_LIMIT = 48 << 20

ADAM_LR = 0.001
ADAM_B1 = 0.9
ADAM_B2 = 0.999
ADAM_EPS = 1e-08
ADAM_WD = 0.01
ADAM_STEP = 10


def _pick(n, cands):
    for c in cands:
        if n % c == 0:
            return c
    return n


def _cparams(sem, vmem=VMEM_LIMIT):
    return pltpu.CompilerParams(dimension_semantics=sem, vmem_limit_bytes=vmem)


class _Carry:
    def __init__(self, kind, xs):
        self.kind, self.xs, self.n = kind, list(xs), len(xs)
        lead = (N_DEV,) if kind == "gather" else ()
        self.out_shape = [jax.ShapeDtypeStruct(lead + v.shape, v.dtype) for v in xs]
        self.specs = [pl.BlockSpec(memory_space=pl.ANY)] * self.n
        self.scratch = [pltpu.SemaphoreType.DMA((self.n, N_DEV - 1)), pltpu.SemaphoreType.DMA((self.n, N_DEV - 1)),
                        pltpu.SemaphoreType.DMA((self.n,))]

    def _copies(self, x_refs, out_refs, sems):
        send_sems, recv_sems, local_sems = sems
        x, y, c = lax.axis_index("x"), lax.axis_index("y"), lax.axis_index("c")
        me = _dev_index((x, y, c))
        mine, sends, arrivals = [], [], []
        for a in range(self.n):
            src_mine = x_refs[a] if self.kind == "gather" else x_refs[a].at[me]
            mine.append(pltpu.make_async_copy(src_mine, out_refs[a].at[me], local_sems.at[a]))
            for k in range(1, N_DEV):
                peer = _peer(k, x, y, c)
                src = x_refs[a] if self.kind == "gather" else x_refs[a].at[_dev_index(peer)]
                sends.append(pltpu.make_async_remote_copy(
                    src_ref=src, dst_ref=out_refs[a].at[me], send_sem=send_sems.at[a, k - 1],
                    recv_sem=recv_sems.at[a, k - 1], device_id=peer, device_id_type=pl.DeviceIdType.MESH))
                land = out_refs[a].at[_dev_index(peer)]
                arrivals.append(pltpu.make_async_remote_copy(
                    src_ref=land, dst_ref=land, send_sem=send_sems.at[a, k - 1],
                    recv_sem=recv_sems.at[a, k - 1], device_id=peer, device_id_type=pl.DeviceIdType.MESH))
        return mine, sends, arrivals

    def start(self, x_refs, out_refs, sems):
        mine, sends, _ = self._copies(x_refs, out_refs, sems)
        for cp in mine + sends:
            cp.start()

    def wait(self, x_refs, out_refs, sems):
        mine, sends, arrivals = self._copies(x_refs, out_refs, sems)
        for cp in arrivals:
            cp.wait_recv()
        for cp in sends:
            cp.wait_send()
        for cp in mine:
            cp.wait()


def _grid_edges(grid):
    first = functools.reduce(lambda p, q: p & q, [pl.program_id(d) == 0 for d in range(len(grid))])
    last = functools.reduce(lambda p, q: p & q, [pl.program_id(d) == g - 1 for d, g in enumerate(grid)])
    return first, last


def _mm(a, b, mode, name, out_dtype=F32, add=None, bias=None, a_blk=False, b_blk=False, out_blk=0, carry=None,
        n_tiles=None):
    w = b.shape[2] if b_blk else out_blk
    if mode == "nn":
        M, K = a.shape
        N = b.shape[0] * w if b_blk else b.shape[1]
    elif mode == "nt":
        M = a.shape[1] if a_blk else a.shape[0]
        N = b.shape[1] if b_blk else b.shape[0]
        K = b.shape[0] * w if b_blk else b.shape[1]
    else:
        K, M = a.shape
        N = b.shape[0] * w if b_blk else b.shape[1]
    tm = _pick(M, (1024, 768, 512, 256))
    tn = _pick(N, (1024, 768, 512, 256))
    tk = K if (mode != "tn" and K <= 2048) else _pick(K, (1024, 768, 512, 256))
    perm = lambda j: j
    if n_tiles:
        tn, perm = n_tiles
    if mode == "nt" and b_blk:
        tk = w
    elif b_blk or out_blk:
        tn = w
    nk = K // tk
    grid = (M // tm, N // tn, nk)
    if mode == "nn":
        a_spec = pl.BlockSpec((tm, tk), lambda i, j, k: (i, k))
        b_spec = (pl.BlockSpec((1, tk, tn), lambda i, j, k: (j, k, 0)) if b_blk
                  else pl.BlockSpec((tk, tn), lambda i, j, k: (k, j)))
        dims = (((1,), (0,)), ((), ()))
    elif mode == "nt":
        a_spec = (pl.BlockSpec((1, tm, tk), lambda i, j, k: (k, i, 0)) if a_blk
                  else pl.BlockSpec((tm, tk), lambda i, j, k: (i, k)))
        b_spec = (pl.BlockSpec((1, tn, tk), lambda i, j, k: (k, j, 0)) if b_blk
                  else pl.BlockSpec((tn, tk), lambda i, j, k: (perm(j), k)))
        dims = (((1,), (1,)), ((), ()))
    else:
        a_spec = pl.BlockSpec((tk, tm), lambda i, j, k: (k, i))
        b_spec = (pl.BlockSpec((1, tk, tn), lambda i, j, k: (j, k, 0)) if b_blk
                  else pl.BlockSpec((tk, tn), lambda i, j, k: (k, j)))
        dims = (((0,), (0,)), ((), ()))
    if out_blk:
        out_spec = pl.BlockSpec((1, tm, tn), lambda i, j, k: (j, i, 0))
        out_shape = jax.ShapeDtypeStruct((N // tn, M, tn), out_dtype)
    else:
        out_spec = pl.BlockSpec((tm, tn), lambda i, j, k: (i, j))
        out_shape = jax.ShapeDtypeStruct((M, N), out_dtype)
    extras, extra_specs = [], []
    if add is not None:
        extras.append(add)
        extra_specs.append(pl.BlockSpec((tm, tn), lambda i, j, k: (i, j)))
    if bias is not None:
        extras.append(bias)
        extra_specs.append(pl.BlockSpec((1, tn), lambda i, j, k: (0, j)))

    n_ex = len(extras)
    nc = carry.n if carry else 0

    def body(a_ref, b_ref, *rest):
        ex_refs, cx = rest[:n_ex], rest[n_ex:n_ex + nc]
        o_ref, cout = rest[n_ex + nc], rest[n_ex + nc + 1:n_ex + 2 * nc + 1]
        tail = rest[n_ex + 2 * nc + 1:]
        sems = tail[:3] if carry else ()
        first, last = _grid_edges(grid)
        if carry:
            @pl.when(first)
            def _():
                carry.start(cx, cout, sems)

        def product():
            av = a_ref[0] if a_blk else a_ref[...]
            bv = b_ref[0] if b_blk else b_ref[...]
            return lax.dot_general(av, bv, dims, preferred_element_type=F32)

        def finish(out):
            for r in ex_refs:
                out = out + r[...]
            if out_blk:
                o_ref[0] = out.astype(out_dtype)
            else:
                o_ref[...] = out.astype(out_dtype)

        if nk == 1:
            finish(product())
        else:
            acc_ref = tail[-1]
            k = pl.program_id(2)

            @pl.when(k == 0)
            def _():
                acc_ref[...] = jnp.zeros_like(acc_ref)

            acc_ref[...] += product()

            @pl.when(k == nk - 1)
            def _():
                finish(acc_ref[...])

        if carry:
            @pl.when(last)
            def _():
                carry.wait(cx, cout, sems)

    scratch = (carry.scratch if carry else []) + ([pltpu.VMEM((tm, tn), F32)] if nk > 1 else [])
    outs = pl.pallas_call(
        body, name=name, grid=grid,
        in_specs=[a_spec, b_spec] + extra_specs + (carry.specs if carry else []),
        out_specs=[out_spec] + (carry.specs if carry else []),
        out_shape=[out_shape] + (carry.out_shape if carry else []),
        scratch_shapes=scratch,
        compiler_params=_cparams(("arbitrary", "arbitrary", "arbitrary")),
    )(a, b, *extras, *(carry.xs if carry else []))
    return (outs[0], outs[1:]) if carry else outs[0]


def _rspec(tm, w, cb=0):
    return pl.BlockSpec((tm, w), lambda i: (i, cb))


def _fspec(shape):
    nd = len(shape)
    return pl.BlockSpec(shape, lambda i: (0,) * nd)


def _rowcall(body, name, L, tm, ins, row_outs, acc_outs=()):
    out_shape = [jax.ShapeDtypeStruct((L, w), dt) for w, dt in row_outs]
    out_shape += [jax.ShapeDtypeStruct(s, F32) for s in acc_outs]
    out_specs = [_rspec(tm, w) for w, _ in row_outs] + [_fspec(s) for s in acc_outs]
    return pl.pallas_call(
        body, name=name, grid=(L // tm,),
        in_specs=[s for _, s in ins], out_specs=out_specs, out_shape=out_shape,
        compiler_params=_cparams(("arbitrary",)),
    )(*[a for a, _ in ins])


def _acc(ref, val):
    @pl.when(pl.program_id(0) == 0)
    def _():
        ref[...] = jnp.zeros_like(ref)
    ref[...] += val


def _colsum(v):
    return jnp.sum(v, axis=0, keepdims=True)


def _rms(xv):
    return lax.rsqrt(jnp.mean(xv * xv, axis=-1, keepdims=True) + EPS)


def _rms_bwd(dn, xhat, r, g):
    dng = dn * g
    return r * (dng - xhat * jnp.mean(dng * xhat, axis=-1, keepdims=True))


def _sigmoid(v):
    return jax.nn.sigmoid(v)


def _f32(ref):
    return ref[...].astype(F32)


def _partner(v):
    w = v.shape[-1]
    lane = lax.broadcasted_iota(jnp.int32, v.shape, v.ndim - 1)
    first_half = (lane % 64) < 32
    return jnp.where(first_half, pltpu.roll(v, w - 32, axis=v.ndim - 1), pltpu.roll(v, 32, axis=v.ndim - 1))


def _norm_in(x, g, name):
    L, D = x.shape
    tm = _pick(L, (512, 256))

    def body(x_ref, g_ref, o_ref):
        xv = x_ref[...]
        o_ref[...] = (xv * _rms(xv) * g_ref[...]).astype(BF16)

    return _rowcall(body, name, L, tm, [(x, _rspec(tm, D)), (g, _fspec(g.shape))], [(D, BF16)])[0]


def _rope_tables(L):
    t = np.arange(L)
    rows = (t // GRID_W).astype(np.float32)
    cols = (t % GRID_W).astype(np.float32)
    n_freq = HEAD_DIM // 4
    inv_freq = np.float32(ROPE_THETA) ** (-np.arange(n_freq, dtype=np.float32) / np.float32(n_freq))
    ar = (rows[:, None] * inv_freq[None, :]).astype(np.float32).astype(np.float64)
    ac = (cols[:, None] * inv_freq[None, :]).astype(np.float32).astype(np.float64)
    cos = np.concatenate([np.cos(ar), np.cos(ar), np.cos(ac), np.cos(ac)], axis=-1).astype(np.float32)
    sin = np.concatenate([-np.sin(ar), np.sin(ar), -np.sin(ac), np.sin(ac)], axis=-1).astype(np.float32)
    return jnp.asarray(cos), jnp.asarray(sin)


def _qkv_prep(z, cos, sin, qn, kn):
    L = z.shape[0]
    tm = _pick(L, (512, 256))
    scale = HEAD_DIM ** -0.5
    kblk = 4 * D_ATTN // D_KV

    def body(q_ref, k_ref, v_ref, cos_ref, sin_ref, qn_ref, kn_ref, qo_ref, ko_ref, vo_ref, kt_ref):
        c, s = cos_ref[...], sin_ref[...]

        def head(xh, w):
            n = xh * _rms(xh) * w
            return n * c + _partner(n) * s

        for h in range(N_HEADS):
            sl = slice(h * HEAD_DIM, (h + 1) * HEAD_DIM)
            qo_ref[:, sl] = (head(q_ref[:, sl].astype(F32), qn_ref[...]) * scale).astype(BF16)
        for h in range(N_KV):
            sl = slice(h * HEAD_DIM, (h + 1) * HEAD_DIM)
            kr = head(k_ref[:, sl].astype(F32), kn_ref[...])
            ko_ref[:, sl] = kr.astype(BF16)
            kt_ref[sl, :] = kr.T.astype(BF16)
        vo_ref[...] = v_ref[...].astype(BF16)

    return pl.pallas_call(
        body, name="qkv_prep", grid=(L // tm,),
        in_specs=[_rspec(tm, D_ATTN, 0), _rspec(tm, D_KV, kblk), _rspec(tm, D_KV, kblk + 1),
                  _rspec(tm, HEAD_DIM), _rspec(tm, HEAD_DIM), _fspec(qn.shape), _fspec(kn.shape)],
        out_specs=[_rspec(tm, D_ATTN), _rspec(tm, D_KV), _rspec(tm, D_KV),
                   pl.BlockSpec((D_KV, tm), lambda i: (0, i))],
        out_shape=[jax.ShapeDtypeStruct((L, D_ATTN), BF16), jax.ShapeDtypeStruct((L, D_KV), BF16),
                   jax.ShapeDtypeStruct((L, D_KV), BF16), jax.ShapeDtypeStruct((D_KV, L), BF16)],
        compiler_params=_cparams(("arbitrary",)),
    )(z, z, z, cos, sin, qn, kn)


def _col_to_row(col):
    n = col.shape[0]
    eye = lax.broadcasted_iota(jnp.int32, (n, n), 0) == lax.broadcasted_iota(jnp.int32, (n, n), 1)
    return jnp.sum(jnp.where(eye, col, 0.0), axis=0, keepdims=True)


def _attn_fwd(q, k, v, carry=None):
    L = q.shape[0]
    tq = _pick(L, (256, 128))
    grid = (N_HEADS, L // tq)
    nc = carry.n if carry else 0

    def body(q_ref, k_ref, v_ref, *rest):
        cx, (o_ref, lse_ref) = rest[:nc], rest[nc:nc + 2]
        cout, sems = rest[nc + 2:2 * nc + 2], rest[2 * nc + 2:]
        first, last = _grid_edges(grid)
        if carry:
            @pl.when(first)
            def _():
                carry.start(cx, cout, sems)

        s = lax.dot_general(q_ref[...], k_ref[...], (((1,), (1,)), ((), ())), preferred_element_type=F32)
        m = jnp.max(s, axis=-1, keepdims=True)
        e = jnp.exp(s - m)
        l = jnp.sum(e, axis=-1, keepdims=True)
        o_ref[...] = jnp.dot(e.astype(BF16), v_ref[...], preferred_element_type=F32) / l
        lse_ref[0] = _col_to_row(m + jnp.log(l))

        if carry:
            @pl.when(last)
            def _():
                carry.wait(cx, cout, sems)

    outs = pl.pallas_call(
        body, name="attn_fwd", grid=grid,
        in_specs=[pl.BlockSpec((tq, HEAD_DIM), lambda h, i: (i, h)),
                  pl.BlockSpec((L, HEAD_DIM), lambda h, i: (0, h // REP)),
                  pl.BlockSpec((L, HEAD_DIM), lambda h, i: (0, h // REP))] + (carry.specs if carry else []),
        out_specs=[pl.BlockSpec((tq, HEAD_DIM), lambda h, i: (i, h)),
                   pl.BlockSpec((1, 1, tq), lambda h, i: (h, 0, i))] + (carry.specs if carry else []),
        out_shape=[jax.ShapeDtypeStruct((L, D_ATTN), F32), jax.ShapeDtypeStruct((N_HEADS, 1, L), F32)]
        + (carry.out_shape if carry else []),
        scratch_shapes=carry.scratch if carry else [],
        compiler_params=_cparams(("arbitrary", "arbitrary")),
    )(q, k, v, *(carry.xs if carry else []))
    return outs[0], outs[1], outs[2:]


def _attn_bwd(q, k, v, kt, do, o, lse, carry=None):
    L = q.shape[0]
    tq = _pick(L, (256, 128))
    kc = _pick(L, (512, 256, 128))
    nt = (((1,), (1,)), ((), ()))
    grid = (N_KV, REP, L // tq)
    nc = carry.n if carry else 0

    def body(q_ref, do_ref, o_ref, lse_ref, k_ref, v_ref, kt_ref, *rest):
        cx, (dq_ref, dk_ref, dv_ref) = rest[:nc], rest[nc:nc + 3]
        cout, sems = rest[nc + 3:2 * nc + 3], rest[2 * nc + 3:]
        first, last = _grid_edges(grid)
        if carry:
            @pl.when(first)
            def _():
                carry.start(cx, cout, sems)

        @pl.when((pl.program_id(1) == 0) & (pl.program_id(2) == 0))
        def _():
            dk_ref[...] = jnp.zeros_like(dk_ref)
            dv_ref[...] = jnp.zeros_like(dv_ref)

        qv, dov = q_ref[...], do_ref[...]
        lse_row = lse_ref[0]
        delta = _col_to_row(jnp.sum(dov.astype(F32) * o_ref[...], axis=-1, keepdims=True))
        dqt = jnp.zeros((HEAD_DIM, tq), F32)
        for c in range(L // kc):
            sl = slice(c * kc, (c + 1) * kc)
            st = lax.dot_general(k_ref[sl, :], qv, nt, preferred_element_type=F32)
            pt = jnp.exp(st - lse_row)
            dpt = lax.dot_general(v_ref[sl, :], dov, nt, preferred_element_type=F32)
            dst = (pt * (dpt - delta)).astype(BF16)
            dv_ref[sl, :] += jnp.dot(pt.astype(BF16), dov, preferred_element_type=F32)
            dk_ref[sl, :] += jnp.dot(dst, qv, preferred_element_type=F32)
            dqt = dqt + jnp.dot(kt_ref[:, sl], dst, preferred_element_type=F32)
        dq_ref[...] = dqt.T

        if carry:
            @pl.when(last)
            def _():
                carry.wait(cx, cout, sems)

    head = lambda g, r, i: (i, g * REP + r)
    outs = pl.pallas_call(
        body, name="attn_bwd", grid=grid,
        in_specs=[pl.BlockSpec((tq, HEAD_DIM), head), pl.BlockSpec((tq, HEAD_DIM), head),
                  pl.BlockSpec((tq, HEAD_DIM), head),
                  pl.BlockSpec((1, 1, tq), lambda g, r, i: (g * REP + r, 0, i)),
                  pl.BlockSpec((L, HEAD_DIM), lambda g, r, i: (0, g)),
                  pl.BlockSpec((L, HEAD_DIM), lambda g, r, i: (0, g)),
                  pl.BlockSpec((HEAD_DIM, L), lambda g, r, i: (g, 0))] + (carry.specs if carry else []),
        out_specs=[pl.BlockSpec((tq, HEAD_DIM), head),
                   pl.BlockSpec((L, HEAD_DIM), lambda g, r, i: (0, g)),
                   pl.BlockSpec((L, HEAD_DIM), lambda g, r, i: (0, g))] + (carry.specs if carry else []),
        out_shape=[jax.ShapeDtypeStruct((L, D_ATTN), F32), jax.ShapeDtypeStruct((L, D_KV), F32),
                   jax.ShapeDtypeStruct((L, D_KV), F32)] + (carry.out_shape if carry else []),
        scratch_shapes=carry.scratch if carry else [],
        compiler_params=_cparams(("arbitrary", "arbitrary", "arbitrary")),
    )(q, do, o, lse, k, v, kt, *(carry.xs if carry else []))
    return outs[0], outs[1], outs[2], outs[3:]


def _seg_perm(a):
    L, C = a.shape
    return a.reshape(SEG, L // SEG, C).transpose(1, 0, 2).reshape(L, C)


def _seg_unperm(a):
    L, C = a.shape
    return a.reshape(L // SEG, SEG, C).transpose(1, 0, 2).reshape(L, C)


def _cmul(ar, ai, br, bi):
    return ar * br - ai * bi, ar * bi + ai * br


def _rows8(rr):
    return pl.ds(pl.multiple_of(rr * SEG, SEG), SEG)


def _seg_scan(xr_ref, xi_ref, ar, ai, reverse, n_rows, visit=None, visit_init=()):
    shape = ar.shape
    zero = jnp.zeros(shape, F32)

    def index(r):
        return (n_rows - 1 - r) if reverse else r

    def ends(r, carry):
        cr, ci = carry
        sl = _rows8(index(r))
        pr, pi = _cmul(ar, ai, cr, ci)
        return pr + xr_ref[sl, :], pi + xi_ref[sl, :]

    er, ei = lax.fori_loop(0, n_rows, ends, (zero, zero))

    pr, pi = ar, ai
    for _ in range(int(math.log2(n_rows))):
        pr, pi = _cmul(pr, pi, pr, pi)
    sub = lax.broadcasted_iota(jnp.int32, shape, 0)
    shift = (SEG - 1) if reverse else 1
    edge = (SEG - 1) if reverse else 0
    inr, ini = zero, zero
    for _ in range(SEG - 1):
        tr, ti = _cmul(pr, pi, inr, ini)
        inr = jnp.where(sub == edge, 0.0, pltpu.roll(tr + er, shift, axis=0))
        ini = jnp.where(sub == edge, 0.0, pltpu.roll(ti + ei, shift, axis=0))

    def step(rr, carry, last):
        cr, ci = carry[:2]
        sl = _rows8(rr)
        pr, pi = _cmul(ar, ai, cr, ci)
        nr, ni = pr + xr_ref[sl, :], pi + xi_ref[sl, :]
        xr_ref[sl, :] = nr
        xi_ref[sl, :] = ni
        acc = visit(rr, nr, ni, carry[2:], last) if visit else ()
        return (nr, ni, *acc)

    carry = lax.fori_loop(0, n_rows - 1, lambda r, c: step(index(r), c, False), (inr, ini, *visit_init))
    carry = step(index(n_rows - 1), carry, True)
    return inr, ini, carry[2:]


def _discretise(a_re, a_im, ldt):
    lr = jnp.minimum(a_re, -1e-4)
    li = a_im
    dt = jnp.exp(ldt)
    mag = jnp.exp(lr * dt)
    lbr = mag * jnp.cos(li * dt)
    lbi = mag * jnp.sin(li * dt)
    den = lr * lr + li * li
    nr = lbr - 1.0
    fr = (nr * lr + lbi * li) / den
    fi = (lbi * lr - nr * li) / den
    return lr, li, dt, lbr, lbi, fr, fi


def _lane_row(v):
    return jnp.concatenate([v[g:g + 1, :] for g in range(v.shape[0])], axis=1)


def _ssm_fill_maps(d, prm, tmp_ref, maps):
    a_re_ref, a_im_ref, ldt_ref, bt_re_ref, bt_im_ref, c_re_ref, c_im_ref = prm
    _, _, _, lbr, lbi, fr, fi = _discretise(a_re_ref[d], a_im_ref[d], ldt_ref[d])

    def fill(dst, piece):
        tmp_ref[...] = jnp.zeros_like(tmp_ref)
        for g in range(SLAB_G):
            tmp_ref[g * SSM_H:(g + 1) * SSM_H, g * SSM_P:(g + 1) * SSM_P] = piece(g)
        dst[...] = tmp_ref[...].astype(BF16)

    wbr, wbi, wcr, wci = maps
    fill(wbr, lambda g: fr[g:g + 1] * bt_re_ref[d, g] - fi[g:g + 1] * bt_im_ref[d, g])
    fill(wbi, lambda g: fr[g:g + 1] * bt_im_ref[d, g] + fi[g:g + 1] * bt_re_ref[d, g])
    fill(wcr, lambda g: c_re_ref[d, g])
    fill(wci, lambda g: c_im_ref[d, g])
    return _lane_row(lbr), _lane_row(lbi)


def _ssm_param_specs():
    pole = pl.BlockSpec((2, SLAB_G, SSM_P), lambda j: (0, j, 0))
    step = pl.BlockSpec((2, SLAB_G, 1), lambda j: (0, j, 0))
    mat = pl.BlockSpec((2, SLAB_G, SSM_H, SSM_P), lambda j: (0, j, 0, 0))
    return [pole, pole, step, mat, mat, mat, mat]


_MAP_SCRATCH = [pltpu.VMEM((SLAB, SLAB_S), F32)] + [pltpu.VMEM((SLAB, SLAB_S), BF16)] * 4
_NT = (((1,), (1,)), ((), ()))


def _ssm_fwd(u, prm, dskip):
    L, C = u.shape
    n_rows = L // SEG
    tc = _pick(L, (1024, 512, 256))
    u_spec = pl.BlockSpec((L, SLAB), lambda j: (0, j))
    d_spec = pl.BlockSpec((1, SLAB), lambda j: (0, j))

    def body(u_ref, *rest):
        prm_refs, d_ref, y_ref = rest[:7], rest[7], rest[8]
        tmp_ref, maps, xr_ref, xi_ref = rest[9], rest[10:14], rest[14], rest[15]
        wbr, wbi, wcr, wci = maps
        y_ref[...] = u_ref[...] * d_ref[...]
        for d in range(2):
            lam_r, lam_i = _ssm_fill_maps(d, prm_refs, tmp_ref, maps)

            def inp(c, _):
                sl = pl.ds(pl.multiple_of(c * tc, tc), tc)
                ub = u_ref[sl, :].astype(BF16)
                xr_ref[sl, :] = jnp.dot(ub, wbr[...], preferred_element_type=F32)
                xi_ref[sl, :] = jnp.dot(ub, wbi[...], preferred_element_type=F32)
                return 0

            lax.fori_loop(0, L // tc, inp, 0)
            ar = jnp.broadcast_to(lam_r, (SEG, SLAB_S))
            ai = jnp.broadcast_to(lam_i, (SEG, SLAB_S))
            _seg_scan(xr_ref, xi_ref, ar, ai, d == 1, n_rows)

            def outp(c, _):
                sl = pl.ds(pl.multiple_of(c * tc, tc), tc)
                y_ref[sl, :] += (
                    lax.dot_general(xr_ref[sl, :].astype(BF16), wcr[...], _NT, preferred_element_type=F32)
                    - lax.dot_general(xi_ref[sl, :].astype(BF16), wci[...], _NT, preferred_element_type=F32))
                return 0

            lax.fori_loop(0, L // tc, outp, 0)

    return pl.pallas_call(
        body, name="ssm_fwd", grid=(C // SLAB,),
        in_specs=[u_spec] + _ssm_param_specs() + [d_spec],
        out_specs=u_spec, out_shape=jax.ShapeDtypeStruct((L, C), F32),
        scratch_shapes=_MAP_SCRATCH + [pltpu.VMEM((L, SLAB_S), F32)] * 2,
        compiler_params=_cparams(("arbitrary",)),
    )(u, *prm, dskip)


def _ssm_bwd(u, dy, prm, dskip):
    L, C = u.shape
    n_rows = L // SEG
    n_slab = C // SLAB
    tc = _pick(L, (1024, 512, 256))
    u_spec = pl.BlockSpec((L, SLAB), lambda j: (0, j))
    d_spec = pl.BlockSpec((1, SLAB), lambda j: (0, j))
    pg_spec = pl.BlockSpec((1, 2, PG_ROWS, SLAB_S), lambda j: (j, 0, 0, 0))

    def body(u_ref, dy_ref, *rest):
        prm_refs, d_ref, du_ref, pg_ref = rest[:7], rest[7], rest[8], rest[9]
        tmp_ref, maps, acc_ref = rest[10], rest[11:15], rest[15]
        xr_ref, xi_ref, gr_ref, gi_ref = rest[16:20]
        wbr, wbi, wcr, wci = maps
        du_ref[...] = dy_ref[...] * d_ref[...]
        pg_ref[...] = jnp.zeros_like(pg_ref)
        pg_ref[0, 0, 66:67, 0:SLAB] = _colsum(dy_ref[...] * u_ref[...])
        for d in range(2):
            lam_r, lam_i = _ssm_fill_maps(d, prm_refs, tmp_ref, maps)

            def inp(c, _):
                sl = pl.ds(pl.multiple_of(c * tc, tc), tc)
                ub = u_ref[sl, :].astype(BF16)
                dyb = dy_ref[sl, :].astype(BF16)
                xr_ref[sl, :] = jnp.dot(ub, wbr[...], preferred_element_type=F32)
                xi_ref[sl, :] = jnp.dot(ub, wbi[...], preferred_element_type=F32)
                gr_ref[sl, :] = jnp.dot(dyb, wcr[...], preferred_element_type=F32)
                gi_ref[sl, :] = -jnp.dot(dyb, wci[...], preferred_element_type=F32)
                return 0

            lax.fori_loop(0, L // tc, inp, 0)
            ar = jnp.broadcast_to(lam_r, (SEG, SLAB_S))
            ai = jnp.broadcast_to(lam_i, (SEG, SLAB_S))
            inr, ini, _ = _seg_scan(xr_ref, xi_ref, ar, ai, d == 1, n_rows)

            def pole(rr, lr, li, acc, last):
                if last:
                    pr, pi = inr, ini
                else:
                    nb = _rows8(rr + 1 if d == 1 else rr - 1)
                    pr, pi = xr_ref[nb, :], xi_ref[nb, :]
                return acc[0] + lr * pr + li * pi, acc[1] + li * pr - lr * pi

            zero = jnp.zeros((SEG, SLAB_S), F32)
            _, _, (accr, acci) = _seg_scan(gr_ref, gi_ref, ar, -ai, d == 0, n_rows, pole, (zero, zero))
            pg_ref[0, d, 64:65, :] = _colsum(accr)
            pg_ref[0, d, 65:66, :] = _colsum(acci)

            acc_ref[...] = jnp.zeros_like(acc_ref)

            def outp(c, _):
                sl = pl.ds(pl.multiple_of(c * tc, tc), tc)
                lrb, lib = gr_ref[sl, :].astype(BF16), gi_ref[sl, :].astype(BF16)
                du_ref[sl, :] += (lax.dot_general(lrb, wbr[...], _NT, preferred_element_type=F32)
                                  + lax.dot_general(lib, wbi[...], _NT, preferred_element_type=F32))
                ut = u_ref[sl, :].T.astype(BF16)
                dyt = dy_ref[sl, :].T.astype(BF16)
                acc_ref[0] += jnp.dot(ut, lrb, preferred_element_type=F32)
                acc_ref[1] += jnp.dot(ut, lib, preferred_element_type=F32)
                acc_ref[2] += jnp.dot(dyt, xr_ref[sl, :].astype(BF16), preferred_element_type=F32)
                acc_ref[3] -= jnp.dot(dyt, xi_ref[sl, :].astype(BF16), preferred_element_type=F32)
                return 0

            lax.fori_loop(0, L // tc, outp, 0)
            for m in range(4):
                for g in range(SLAB_G):
                    lanes = slice(g * SSM_P, (g + 1) * SSM_P)
                    pg_ref[0, d, m * SSM_H:(m + 1) * SSM_H, lanes] = acc_ref[m, g * SSM_H:(g + 1) * SSM_H, lanes]

    return pl.pallas_call(
        body, name="ssm_bwd", grid=(n_slab,),
        in_specs=[u_spec, u_spec] + _ssm_param_specs() + [d_spec],
        out_specs=[u_spec, pg_spec],
        out_shape=[jax.ShapeDtypeStruct((L, C), F32), jax.ShapeDtypeStruct((n_slab, 2, PG_ROWS, SLAB_S), F32)],
        scratch_shapes=_MAP_SCRATCH + [pltpu.VMEM((4, SLAB, SLAB_S), F32)] + [pltpu.VMEM((L, SLAB_S), F32)] * 4,
        compiler_params=_cparams(("arbitrary",), 60 << 20),
    )(u, dy, *prm, dskip)


def _ssm_param_grads(pg, prm):
    n_slab = pg.shape[0]
    G = n_slab * SLAB_G
    pg_spec = pl.BlockSpec((1, 2, PG_ROWS, SLAB_S), lambda j: (j, 0, 0, 0))
    pole, _, step, mat = _ssm_param_specs()[:4]

    def body(pg_ref, a_re_ref, a_im_ref, ldt_ref, bt_re_ref, bt_im_ref,
             dbr_ref, dbi_ref, dcr_ref, dci_ref, dar_ref, dai_ref, dldt_ref, dd_ref):
        dd_ref[...] = pg_ref[0, 0, 66:67, 0:SLAB]
        for d in range(2):
            a_r = a_re_ref[d]
            lr, li, dt, lbr, lbi, f_r, f_i = _discretise(a_r, a_im_ref[d], ldt_ref[d])
            gfr_rows, gfi_rows, glr_rows, gli_rows = [], [], [], []
            for g in range(SLAB_G):
                lanes = slice(g * SSM_P, (g + 1) * SSM_P)
                gbr, gbi = pg_ref[0, d, 0:SSM_H, lanes], pg_ref[0, d, SSM_H:2 * SSM_H, lanes]
                b_r, b_i = bt_re_ref[d, g], bt_im_ref[d, g]
                fr, fi = f_r[g:g + 1], f_i[g:g + 1]
                dbr_ref[d, g] = fr * gbr + fi * gbi
                dbi_ref[d, g] = fr * gbi - fi * gbr
                gfr_rows.append(_colsum(gbr * b_r + gbi * b_i))
                gfi_rows.append(_colsum(gbi * b_r - gbr * b_i))
                dcr_ref[d, g] = pg_ref[0, d, 2 * SSM_H:3 * SSM_H, lanes]
                dci_ref[d, g] = pg_ref[0, d, 3 * SSM_H:4 * SSM_H, lanes]
                glr_rows.append(pg_ref[0, d, 64:65, lanes])
                gli_rows.append(pg_ref[0, d, 65:66, lanes])
            gfr, gfi = jnp.concatenate(gfr_rows, axis=0), jnp.concatenate(gfi_rows, axis=0)
            glr, gli = jnp.concatenate(glr_rows, axis=0), jnp.concatenate(gli_rows, axis=0)
            den = lr * lr + li * li
            ir, ii = lr / den, -li / den
            tr, ti = _cmul(ir, -ii, gfr, gfi)
            glbr, glbi = glr + tr, gli + ti
            qr, qi = _cmul(f_r, f_i, ir, ii)
            dlr, dli = _cmul(-qr, qi, gfr, gfi)
            zr, zi = _cmul(lbr, -lbi, glbr, glbi)
            dlr = dlr + dt * zr
            dli = dli + dt * zi
            dar_ref[d] = jnp.where(a_r < -1e-4, dlr, jnp.where(a_r == -1e-4, 0.5 * dlr, 0.0))
            dai_ref[d] = dli
            dldt_ref[d] = jnp.sum(lr * zr + li * zi, axis=-1, keepdims=True) * dt

    a_re, a_im, ldt, bt_re, bt_im = prm[:5]
    mshape = jax.ShapeDtypeStruct(bt_re.shape, F32)
    pshape = jax.ShapeDtypeStruct(a_re.shape, F32)
    return pl.pallas_call(
        body, name="ssm_param_grads", grid=(n_slab,),
        in_specs=[pg_spec, pole, pole, step, mat, mat],
        out_specs=[mat, mat, mat, mat, pole, pole, step, pl.BlockSpec((1, SLAB), lambda j: (0, j))],
        out_shape=[mshape, mshape, mshape, mshape, pshape, pshape, jax.ShapeDtypeStruct(ldt.shape, F32),
                   jax.ShapeDtypeStruct((1, n_slab * SLAB), F32)],
        compiler_params=_cparams(("arbitrary",)),
    )(pg, a_re, a_im, ldt, bt_re, bt_im)


def _peer(k, x, y, c):
    return (1 - x if k & 4 else x, 1 - y if k & 2 else y, 1 - c if k & 1 else c)


def _dev_index(pos):
    return 4 * pos[0] + 2 * pos[1] + pos[2]


def _all_gather(xs, name):
    n = len(xs)
    any_spec = pl.BlockSpec(memory_space=pl.ANY)

    def body(*refs):
        x_refs, out_refs = refs[:n], refs[n:2 * n]
        send_sems, recv_sems, local_sems = refs[2 * n:]
        x, y, c = lax.axis_index("x"), lax.axis_index("y"), lax.axis_index("c")
        me, sibling = (x, y, c), (x, y, 1 - c)
        chips = [(1 - x, y), (x, 1 - y), (1 - x, 1 - y)]

        def copy(a, k, block, to, src=None):
            dst = out_refs[a].at[_dev_index(block)]
            return pltpu.make_async_remote_copy(
                src_ref=dst if src is None else src, dst_ref=dst,
                send_sem=send_sems.at[a, k], recv_sem=recv_sems.at[a, k],
                device_id=to, device_id_type=pl.DeviceIdType.MESH)

        mine = [pltpu.make_async_copy(x_refs[a], out_refs[a].at[_dev_index(me)], local_sems.at[a]) for a in range(n)]
        for cp in mine:
            cp.start()
        first = []
        for a in range(n):
            first.append(copy(a, 0, me, sibling, src=x_refs[a]))
            first += [copy(a, 1 + j, me, (*chip, c), src=x_refs[a]) for j, chip in enumerate(chips)]
        for cp in first:
            cp.start()
        passed = []
        for j, chip in enumerate(chips):
            for a in range(n):
                copy(a, 1 + j, (*chip, c), me).wait_recv()
                fwd = copy(a, 4 + j, (*chip, c), sibling)
                fwd.start()
                passed.append(fwd)
        for a in range(n):
            copy(a, 0, sibling, me).wait_recv()
            for j, chip in enumerate(chips):
                copy(a, 4 + j, (*chip, 1 - c), me).wait_recv()
        for cp in first + passed:
            cp.wait_send()
        for cp in mine:
            cp.wait()

    return pl.pallas_call(
        body, name=name,
        out_shape=[jax.ShapeDtypeStruct((N_DEV,) + v.shape, v.dtype) for v in xs],
        in_specs=[any_spec] * n, out_specs=[any_spec] * n,
        scratch_shapes=[pltpu.SemaphoreType.DMA((n, 7)), pltpu.SemaphoreType.DMA((n, 7)),
                        pltpu.SemaphoreType.DMA((n,))],
    )(*xs)


def _sum_blocks(x, name):
    _, R, W = x.shape

    def body(x_ref, o_ref):
        acc = x_ref[0].astype(F32)
        for d in range(1, N_DEV):
            acc = acc + x_ref[d].astype(F32)
        o_ref[...] = acc

    return pl.pallas_call(body, name=name, out_shape=jax.ShapeDtypeStruct((R, W), F32),
                          compiler_params=pltpu.CompilerParams(vmem_limit_bytes=VMEM_LIMIT))(x)


def _adam_update(w, g, m, v):
    mn = ADAM_B1 * m + (1.0 - ADAM_B1) * g
    vn = ADAM_B2 * v + (1.0 - ADAM_B2) * (g * g)
    m_hat = mn / (1.0 - ADAM_B1 ** ADAM_STEP)
    v_hat = vn / (1.0 - ADAM_B2 ** ADAM_STEP)
    return -ADAM_LR * (m_hat / (jnp.sqrt(v_hat) + ADAM_EPS) + ADAM_WD * w), mn, vn


def _row_tile(R, W, budget):
    padded = -(-W // LANES) * LANES * 4
    if R * padded <= budget:
        return R
    return _pick(R, [t for t in (2048, 1024, 512, 256, 128, 64, 32, 16, 8) if t * padded <= budget])


def _adamw(w, g, m, v, name):
    R, W = w.shape
    tr = _row_tile(R, W, 1 << 20)

    def body(w_ref, g_ref, m_ref, v_ref, d_ref, mo_ref, vo_ref):
        d_ref[...], mo_ref[...], vo_ref[...] = _adam_update(w_ref[...], g_ref[...], m_ref[...], v_ref[...])

    spec = pl.BlockSpec((tr, W), lambda i: (i, 0))
    shp = jax.ShapeDtypeStruct((R, W), F32)
    return pl.pallas_call(
        body, name=name, grid=(R // tr,), in_specs=[spec] * 4, out_specs=[spec] * 3, out_shape=[shp] * 3,
        compiler_params=_cparams(("arbitrary",)),
    )(w, g, m, v)


def _adamw_reduce(w, land, m, v, name):
    R, W = w.shape
    tr = _row_tile(R, W, 1 << 20)

    def body(w_ref, l_ref, m_ref, v_ref, g_ref, d_ref, mo_ref, vo_ref):
        g = l_ref[0].astype(F32)
        for d in range(1, N_DEV):
            g = g + l_ref[d].astype(F32)
        g_ref[...] = g
        d_ref[...], mo_ref[...], vo_ref[...] = _adam_update(w_ref[...], g, m_ref[...], v_ref[...])

    spec = pl.BlockSpec((tr, W), lambda i: (i, 0))
    lspec = pl.BlockSpec((N_DEV, tr, W), lambda i: (0, i, 0))
    shp = jax.ShapeDtypeStruct((R, W), F32)
    return pl.pallas_call(
        body, name=name, grid=(R // tr,), in_specs=[spec, lspec, spec, spec], out_specs=[spec] * 4,
        out_shape=[shp] * 4, compiler_params=_cparams(("arbitrary",)),
    )(w, land, m, v)


def _gelu(v):
    c = math.sqrt(2.0 / math.pi)
    return 0.5 * v * (1.0 + jnp.tanh(c * (v + 0.044715 * v * v * v)))


def _gelu_grad(v):
    c = math.sqrt(2.0 / math.pi)
    t = jnp.tanh(c * (v + 0.044715 * v * v * v))
    return 0.5 * (1.0 + t) + 0.5 * v * (1.0 - t * t) * c * (1.0 + 3.0 * 0.044715 * v * v)


def kernel(x, p, norm_mix, w_in, q_norm, k_norm, ssm_a_re, ssm_a_im, ssm_log_dt, ssm_b_re, ssm_b_im, ssm_c_re, ssm_c_im, ssm_d, w_glu, b_glu, w_out, norm_ple, w_ple_gate, w_ple_proj, norm_final, loss_target, m_norm_mix, m_w_in, m_q_norm, m_k_norm, m_ssm_a_re, m_ssm_a_im, m_ssm_log_dt, m_ssm_b_re, m_ssm_b_im, m_ssm_c_re, m_ssm_c_im, m_ssm_d, m_w_glu, m_b_glu, m_w_out, m_norm_ple, m_w_ple_gate, m_w_ple_proj, m_norm_final, v_norm_mix, v_w_in, v_q_norm, v_k_norm, v_ssm_a_re, v_ssm_a_im, v_ssm_log_dt, v_ssm_b_re, v_ssm_b_im, v_ssm_c_re, v_ssm_c_im, v_ssm_d, v_w_glu, v_b_glu, v_w_out, v_norm_ple, v_w_ple_gate, v_w_ple_proj, v_norm_final):
    L, D = x.shape[1], x.shape[2]
    D_SSM = ssm_d.shape[1]
    G = D_SSM // SSM_H
    n_slab = D_SSM // SLAB
    n_in = w_in.shape[2]
    D_IN = n_in * N_DEV
    n_pp = w_ple_proj.shape[2]
    n_glu = w_glu.shape[2]
    xs = x[0]
    ps = p[0, 0]
    tgt = loss_target[0]

    (win_t3,) = _all_gather([w_in[0].T.astype(BF16)], "gather_w_in")
    win_t = win_t3.reshape(D_IN, D)
    later_weights = _Carry("gather", [w_glu[0].astype(BF16), w_out[0].astype(BF16), w_ple_gate[0].astype(BF16),
                                      w_ple_proj[0].astype(BF16)])

    ssm_prm = (ssm_a_re[0], ssm_a_im[0], ssm_log_dt[0].reshape(2, G, 1),
               ssm_b_re[0].transpose(0, 1, 3, 2), ssm_b_im[0].transpose(0, 1, 3, 2), ssm_c_re[0], ssm_c_im[0])

    cos, sin = _rope_tables(L)
    hn = _norm_in(xs, norm_mix, "norm_mix")
    ZT = 512
    zp_tile = lambda j: jnp.where(j < 2, j, jnp.where(j < D_IN // ZT - 1, j + 1, 2))
    z = _mm(hn, win_t, "nt", "in_proj", out_dtype=BF16, n_tiles=(ZT, zp_tile))
    qr, kr, vb, kt = _qkv_prep(z, cos, sin, q_norm, k_norm)
    o, lse, (wglu3, wout3, wpg3, wpp3) = _attn_fwd(qr, kr, vb, later_weights)
    wout = wout3.reshape(-1, D)
    wpg = wpg3.reshape(-1, D)
    u_off = 2 * D_ATTN
    u_perm = _seg_perm(z[:, u_off:u_off + D_SSM]).astype(F32)
    ys = _seg_unperm(_ssm_fwd(u_perm, ssm_prm, ssm_d))

    tm = _pick(L, (256,))

    def gelu_body(y_ref, o_ref):
        o_ref[...] = _gelu(y_ref[...]).astype(BF16)

    (gy,) = _rowcall(gelu_body, "gelu", L, tm, [(ys, _rspec(tm, D_SSM))], [(D_SSM, BF16)])
    glu = _mm(gy, wglu3, "nn", "glu_proj", out_dtype=BF16, bias=b_glu, b_blk=True)

    def mix_body(o_ref, ga_ref, gla_ref, glb_ref, gs_ref, cat_ref):
        ga, gs = _f32(ga_ref), _f32(gs_ref)
        cat_ref[:, :D_ATTN] = (o_ref[...] * ga * _sigmoid(ga)).astype(BF16)
        cat_ref[:, D_ATTN:] = (_f32(gla_ref) * _sigmoid(_f32(glb_ref)) * gs * _sigmoid(gs)).astype(BF16)

    (cat,) = _rowcall(mix_body, "mix", L, tm,
                      [(o, _rspec(tm, D_ATTN)), (z, _rspec(tm, D_ATTN, 1)), (glu, _rspec(tm, D_SSM, 0)),
                       (glu, _rspec(tm, D_SSM, 1)), (z, _rspec(tm, D_SSM, 3))], [(D_ATTN + D_SSM, BF16)])
    h1 = _mm(cat, wout, "nn", "out_proj", add=xs)
    n2 = _norm_in(h1, norm_ple, "norm_ple")
    gpre = _mm(n2, wpg, "nn", "ple_gate", out_dtype=BF16)
    pb = ps.astype(BF16)
    pp = _mm(pb, wpp3, "nn", "ple_proj", out_dtype=BF16, b_blk=True)

    nf = norm_final.reshape(1, D)

    def tail_body(h1_ref, gp_ref, pp_ref, t_ref, g_ref, dh2_ref, dpp_ref, dsg_ref, loss_ref, dg_ref):
        gate = _sigmoid(_f32(gp_ref))
        ppv = _f32(pp_ref)
        h2 = h1_ref[...] + gate * ppv
        r = _rms(h2)
        hh = h2 * r
        err = hh * g_ref[...] - t_ref[...]
        _acc(loss_ref, jnp.broadcast_to(0.5 * jnp.sum(jnp.mean(err * err, axis=-1, keepdims=True)), loss_ref.shape))
        dy = err * (1.0 / D)
        _acc(dg_ref, _colsum(dy * hh))
        dh2 = _rms_bwd(dy, hh, r, g_ref[...])
        dh2_ref[...] = dh2
        dpp_ref[...] = (dh2 * gate).astype(BF16)
        dsg_ref[...] = (dh2 * ppv * gate * (1.0 - gate)).astype(BF16)

    dh2, dpp, dsg, loss_acc, d_nf = _rowcall(
        tail_body, "tail", L, tm,
        [(h1, _rspec(tm, D)), (gpre, _rspec(tm, D)), (pp, _rspec(tm, D)), (tgt, _rspec(tm, D)), (nf, _fspec(nf.shape))],
        [(D, F32), (D, BF16), (D, BF16)], [(1, LANES), (1, D)])
    loss = lax.psum(loss_acc[0, 0], ("x", "y", "c"))

    g_wpp3 = _mm(pb, dpp, "tn", "d_ple_proj", out_dtype=BF16, out_blk=n_pp)
    g_wpg = _mm(n2, dsg, "tn", "d_ple_gate", out_dtype=BF16)
    dn2 = _mm(dsg, wpg, "nt", "d_norm_ple_in", out_dtype=BF16)

    def ple_bwd_body(h1_ref, dn_ref, dh2_ref, g_ref, dh1_ref, dh1b_ref, dg_ref):
        h1v = h1_ref[...]
        r = _rms(h1v)
        hh = h1v * r
        dn = _f32(dn_ref)
        _acc(dg_ref, _colsum(dn * hh))
        dh1 = dh2_ref[...] + _rms_bwd(dn, hh, r, g_ref[...])
        dh1_ref[...] = dh1
        dh1b_ref[...] = dh1.astype(BF16)

    dh1, dh1b, d_nple = _rowcall(
        ple_bwd_body, "ple_bwd", L, tm,
        [(h1, _rspec(tm, D)), (dn2, _rspec(tm, D)), (dh2, _rspec(tm, D)), (norm_ple, _fspec(norm_ple.shape))],
        [(D, F32), (D, BF16)], [(1, D)])

    dcat = _mm(dh1b, wout, "nt", "d_cat", out_dtype=BF16)
    g_wout = _mm(cat, dh1b, "tn", "d_out_proj", out_dtype=BF16)

    def mix_bwd_body(dca_ref, dcs_ref, o_ref, ga_ref, gla_ref, glb_ref, gs_ref,
                     do_ref, dga_ref, dgs_ref, dglu_ref, db_ref):
        dca, dcs, ga, gs = _f32(dca_ref), _f32(dcs_ref), _f32(ga_ref), _f32(gs_ref)
        sa, ss, sb = _sigmoid(ga), _sigmoid(gs), _sigmoid(_f32(glb_ref))
        gla = _f32(gla_ref)
        do_ref[...] = (dca * ga * sa).astype(BF16)
        dga_ref[...] = (dca * o_ref[...] * sa * (1.0 + ga * (1.0 - sa))).astype(BF16)
        dgs_ref[...] = (dcs * gla * sb * ss * (1.0 + gs * (1.0 - ss))).astype(BF16)
        dy2 = dcs * gs * ss
        da, db = dy2 * sb, dy2 * gla * sb * (1.0 - sb)
        dglu_ref[:, :D_SSM] = da.astype(BF16)
        dglu_ref[:, D_SSM:] = db.astype(BF16)
        _acc(db_ref, jnp.concatenate([_colsum(da), _colsum(db)], axis=-1))

    do, dga, dgs, dglu, g_bglu = _rowcall(
        mix_bwd_body, "mix_bwd", L, tm,
        [(dcat, _rspec(tm, D_ATTN, 0)), (dcat, _rspec(tm, D_SSM, 1)), (o, _rspec(tm, D_ATTN)),
         (z, _rspec(tm, D_ATTN, 1)), (glu, _rspec(tm, D_SSM, 0)), (glu, _rspec(tm, D_SSM, 1)),
         (z, _rspec(tm, D_SSM, 3))],
        [(D_ATTN, BF16), (D_ATTN, BF16), (D_SSM, BF16), (2 * D_SSM, BF16)], [(1, 2 * D_SSM)])

    g_wglu3 = _mm(gy, dglu, "tn", "d_glu_proj", out_dtype=BF16, out_blk=n_glu)
    dgy = _mm(dglu, wglu3, "nt", "d_gelu_out", out_dtype=BF16, b_blk=True)

    def gelu_bwd_body(dg_ref, y_ref, o_ref):
        o_ref[...] = _f32(dg_ref) * _gelu_grad(y_ref[...])

    (dys,) = _rowcall(gelu_bwd_body, "gelu_bwd", L, tm,
                      [(dgy, _rspec(tm, D_SSM)), (ys, _rspec(tm, D_SSM))], [(D_SSM, F32)])
    du_perm, pg = _ssm_bwd(u_perm, _seg_perm(dys), ssm_prm, ssm_d)
    du = _seg_unperm(du_perm)

    dqs, dkr, dvv, (l_wglu, l_wout, l_wpg, l_wpp) = _attn_bwd(
        qr, kr, vb, kt, do, o, lse,
        _Carry("a2a", [g_wglu3, g_wout.reshape(N_DEV, -1, D), g_wpg.reshape(N_DEV, -1, D), g_wpp3]))

    scale = HEAD_DIM ** -0.5
    kblk = 4 * D_ATTN // D_KV
    tmq = _pick(L, (512, 256))

    def qkv_bwd_body(dq_ref, dk_ref, dv_ref, q_ref, k_ref, cos_ref, sin_ref, qn_ref, kn_ref,
                     dqo_ref, dko_ref, dvo_ref, dqn_ref, dkn_ref):
        c, s = cos_ref[...], sin_ref[...]

        def head(g, xh, w):
            dn = g * c + _partner(g * s)
            r = _rms(xh)
            xhat = xh * r
            return _rms_bwd(dn, xhat, r, w), _colsum(dn * xhat)

        dqn = jnp.zeros((1, HEAD_DIM), F32)
        for h in range(N_HEADS):
            sl = slice(h * HEAD_DIM, (h + 1) * HEAD_DIM)
            dx, dw = head(dq_ref[:, sl] * scale, q_ref[:, sl].astype(F32), qn_ref[...])
            dqo_ref[:, sl] = dx.astype(BF16)
            dqn = dqn + dw
        dkn = jnp.zeros((1, HEAD_DIM), F32)
        for h in range(N_KV):
            sl = slice(h * HEAD_DIM, (h + 1) * HEAD_DIM)
            dx, dw = head(dk_ref[:, sl], k_ref[:, sl].astype(F32), kn_ref[...])
            dko_ref[:, sl] = dx.astype(BF16)
            dkn = dkn + dw
        dvo_ref[...] = dv_ref[...].astype(BF16)
        _acc(dqn_ref, dqn)
        _acc(dkn_ref, dkn)

    dq, dk, dv, g_qn, g_kn = _rowcall(
        qkv_bwd_body, "qkv_bwd", L, tmq,
        [(dqs, _rspec(tmq, D_ATTN)), (dkr, _rspec(tmq, D_KV)), (dvv, _rspec(tmq, D_KV)),
         (z, _rspec(tmq, D_ATTN, 0)), (z, _rspec(tmq, D_KV, kblk)), (cos, _rspec(tmq, HEAD_DIM)),
         (sin, _rspec(tmq, HEAD_DIM)), (q_norm, _fspec(q_norm.shape)), (k_norm, _fspec(k_norm.shape))],
        [(D_ATTN, BF16), (D_KV, BF16), (D_KV, BF16)], [(1, HEAD_DIM), (1, HEAD_DIM)])

    dz = jnp.concatenate([dq, dk, dv, dga, du.astype(BF16), dgs], axis=1)
    g_win_t = _mm(dz, hn, "tn", "d_in_proj", out_dtype=BF16)
    pg_send = pg.reshape(N_DEV, (n_slab // N_DEV) * 2 * PG_ROWS, SLAB_S)
    dhn, (l_win_t, l_pg) = _mm(dz, win_t, "nn", "d_norm_mix_in", out_dtype=BF16,
                               carry=_Carry("a2a", [g_win_t.reshape(N_DEV, n_in, D), pg_send]))

    def in_bwd_body(x_ref, dn_ref, dh1_ref, g_ref, dx_ref, dg_ref):
        xv = x_ref[...]
        r = _rms(xv)
        hh = xv * r
        dn = _f32(dn_ref)
        _acc(dg_ref, _colsum(dn * hh))
        dx_ref[...] = dh1_ref[...] + _rms_bwd(dn, hh, r, g_ref[...])

    grad_x, g_nmix = _rowcall(
        in_bwd_body, "in_bwd", L, tm,
        [(xs, _rspec(tm, D)), (dhn, _rspec(tm, D)), (dh1, _rspec(tm, D)), (norm_mix, _fspec(norm_mix.shape))],
        [(D, F32)], [(1, D)])

    tiny_parts = [g_nmix, g_bglu, d_nple, d_nf, g_qn, g_kn]
    tiny_flat = jnp.concatenate([t.reshape(-1) for t in tiny_parts])
    tiny_rows = -(-tiny_flat.shape[0] // (8 * LANES)) * 8
    tiny = jnp.pad(tiny_flat, (0, tiny_rows * LANES - tiny_flat.shape[0])).reshape(tiny_rows, LANES)
    pg_sum = _sum_blocks(l_pg, "sum_ssm_grads")
    pg_all, tiny_all = _all_gather([pg_sum, tiny], "gather_small_grads")
    (g_bt_re, g_bt_im, g_c_re, g_c_im, g_a_re, g_a_im, g_ldt, g_skip) = _ssm_param_grads(
        pg_all.reshape(n_slab, 2, PG_ROWS, SLAB_S), ssm_prm)
    tiny_sum = _sum_blocks(tiny_all, "sum_tiny_grads").reshape(-1)
    tiny_grads, off = [], 0
    for t in tiny_parts:
        tiny_grads.append(tiny_sum[off:off + t.size].reshape(t.shape))
        off += t.size
    r_nmix, r_bglu, r_nple, r_nf, r_qn, r_kn = tiny_grads

    grads, deltas, new_ms, new_vs = {}, {}, {}, {}
    outs = _adamw_reduce(w_in[0].T, l_win_t, m_w_in[0].T, v_w_in[0].T, "adamw_w_in")
    grads["w_in"], deltas["w_in"], new_ms["w_in"], new_vs["w_in"] = [t.T[None] for t in outs]
    big = [("w_glu", w_glu, l_wglu, m_w_glu, v_w_glu),
           ("w_out", w_out, l_wout, m_w_out, v_w_out), ("w_ple_gate", w_ple_gate, l_wpg, m_w_ple_gate, v_w_ple_gate),
           ("w_ple_proj", w_ple_proj, l_wpp, m_w_ple_proj, v_w_ple_proj)]
    for name, w, ld, m, v in big:
        shp = w.shape
        outs = _adamw_reduce(w[0], ld, m[0], v[0], "adamw_" + name)
        grads[name], deltas[name], new_ms[name], new_vs[name] = [t.reshape(shp) for t in outs]
    bt2 = (2 * G * SSM_H, SSM_P)
    for name, w, g, m, v in (("ssm_b_re", ssm_b_re, g_bt_re, m_ssm_b_re, v_ssm_b_re),
                             ("ssm_b_im", ssm_b_im, g_bt_im, m_ssm_b_im, v_ssm_b_im)):
        to2 = lambda t: t[0].transpose(0, 1, 3, 2).reshape(bt2)
        back = lambda t: t.reshape(2, G, SSM_H, SSM_P).transpose(0, 1, 3, 2)[None]
        outs = _adamw(to2(w), g.reshape(bt2), to2(m), to2(v), "adamw_" + name)
        grads[name] = back(g)
        deltas[name], new_ms[name], new_vs[name] = [back(t) for t in outs]
    small = [("norm_mix", norm_mix, r_nmix, m_norm_mix, v_norm_mix, (1, D)),
             ("q_norm", q_norm, r_qn, m_q_norm, v_q_norm, (1, HEAD_DIM)),
             ("k_norm", k_norm, r_kn, m_k_norm, v_k_norm, (1, HEAD_DIM)),
             ("ssm_a_re", ssm_a_re, g_a_re, m_ssm_a_re, v_ssm_a_re, (2 * G, SSM_P)),
             ("ssm_a_im", ssm_a_im, g_a_im, m_ssm_a_im, v_ssm_a_im, (2 * G, SSM_P)),
             ("ssm_log_dt", ssm_log_dt, g_ldt, m_ssm_log_dt, v_ssm_log_dt, (2, G)),
             ("ssm_c_re", ssm_c_re, g_c_re, m_ssm_c_re, v_ssm_c_re, (2 * G * SSM_H, SSM_P)),
             ("ssm_c_im", ssm_c_im, g_c_im, m_ssm_c_im, v_ssm_c_im, (2 * G * SSM_H, SSM_P)),
             ("ssm_d", ssm_d, g_skip, m_ssm_d, v_ssm_d, (1, D_SSM)),
             ("b_glu", b_glu, r_bglu, m_b_glu, v_b_glu, (1, 2 * D_SSM)),
             ("norm_ple", norm_ple, r_nple, m_norm_ple, v_norm_ple, (1, D)),
             ("norm_final", norm_final, r_nf, m_norm_final, v_norm_final, (1, D))]
    for name, w, g, m, v, s2 in small:
        shp = w.shape
        outs = _adamw(w.reshape(s2), g.reshape(s2), m.reshape(s2), v.reshape(s2), "adamw_" + name)
        grads[name] = g.reshape(shp)
        deltas[name], new_ms[name], new_vs[name] = [t.reshape(shp) for t in outs]

    order = ["norm_mix", "w_in", "q_norm", "k_norm", "ssm_a_re", "ssm_a_im", "ssm_log_dt", "ssm_b_re", "ssm_b_im",
             "ssm_c_re", "ssm_c_im", "ssm_d", "w_glu", "b_glu", "w_out", "norm_ple", "w_ple_gate", "w_ple_proj",
             "norm_final"]
    return (loss, grad_x[None], *[grads[k] for k in order], *[deltas[k] for k in order],
            *[new_ms[k] for k in order], *[new_vs[k] for k in order])
```

```python
import functools
import math

import numpy as np
import jax
import jax.numpy as jnp
from jax import lax
from jax.experimental import pallas as pl
from jax.experimental.pallas import tpu as pltpu

F32 = jnp.float32
BF16 = jnp.bfloat16

N_DEV = 8
N_CHIPS = 4
EPS = 1e-6
GRID_W = 64
ROPE_THETA = 10000.0
HEAD_DIM = 128
N_HEADS = 8
N_KV = 2
REP = N_HEADS // N_KV
D_ATTN = N_HEADS * HEAD_DIM
D_KV = N_KV * HEAD_DIM
SSM_H = 16
SSM_P = 64
SLAB = 128
SLAB_G = SLAB // SSM_H
SLAB_S = SLAB_G * SSM_P
SEG = 8
LANES = 128
PG_ROWS = 72
VMEM_LIMIT = 48 << 20

ADAM_LR = 0.001
ADAM_B1 = 0.9
ADAM_B2 = 0.999
ADAM_EPS = 1e-08
ADAM_WD = 0.01
ADAM_STEP = 10


def _pick(n, cands):
    for c in cands:
        if n % c == 0:
            return c
    return n


def _cparams(sem, vmem=VMEM_LIMIT):
    return pltpu.CompilerParams(dimension_semantics=sem, vmem_limit_bytes=vmem)


class _Carry:
    def __init__(self, kind, xs):
        self.kind, self.xs, self.n = kind, list(xs), len(xs)
        self.ks = (2, 4, 6) if kind == "a2a_chips" else tuple(range(1, N_DEV))
        self.index = _chip_index if kind == "a2a_chips" else _dev_index
        lead = (N_DEV,) if kind == "gather" else ()
        self.out_shape = [jax.ShapeDtypeStruct(lead + v.shape, v.dtype) for v in xs]
        self.specs = [pl.BlockSpec(memory_space=pl.ANY)] * self.n
        self.scratch = [pltpu.SemaphoreType.DMA((self.n, len(self.ks))), pltpu.SemaphoreType.DMA((self.n, len(self.ks))),
                        pltpu.SemaphoreType.DMA((self.n,))]

    def _copies(self, x_refs, out_refs, sems):
        send_sems, recv_sems, local_sems = sems
        x, y, c = lax.axis_index("x"), lax.axis_index("y"), lax.axis_index("c")
        me = self.index((x, y, c))
        mine, sends, arrivals = [], [], []
        for a in range(self.n):
            src_mine = x_refs[a] if self.kind == "gather" else x_refs[a].at[me]
            mine.append(pltpu.make_async_copy(src_mine, out_refs[a].at[me], local_sems.at[a]))
            for s, k in enumerate(self.ks):
                peer = _peer(k, x, y, c)
                src = x_refs[a] if self.kind == "gather" else x_refs[a].at[self.index(peer)]
                sends.append(pltpu.make_async_remote_copy(
                    src_ref=src, dst_ref=out_refs[a].at[me], send_sem=send_sems.at[a, s],
                    recv_sem=recv_sems.at[a, s], device_id=peer, device_id_type=pl.DeviceIdType.MESH))
                land = out_refs[a].at[self.index(peer)]
                arrivals.append(pltpu.make_async_remote_copy(
                    src_ref=land, dst_ref=land, send_sem=send_sems.at[a, s],
                    recv_sem=recv_sems.at[a, s], device_id=peer, device_id_type=pl.DeviceIdType.MESH))
        return mine, sends, arrivals

    def start(self, x_refs, out_refs, sems):
        mine, sends, _ = self._copies(x_refs, out_refs, sems)
        for cp in mine + sends:
            cp.start()

    def wait(self, x_refs, out_refs, sems):
        mine, sends, arrivals = self._copies(x_refs, out_refs, sems)
        for cp in arrivals:
            cp.wait_recv()
        for cp in sends:
            cp.wait_send()
        for cp in mine:
            cp.wait()


def _grid_edges(grid):
    first = functools.reduce(lambda p, q: p & q, [pl.program_id(d) == 0 for d in range(len(grid))])
    last = functools.reduce(lambda p, q: p & q, [pl.program_id(d) == g - 1 for d, g in enumerate(grid)])
    return first, last


def _mm(a, b, mode, name, out_dtype=F32, add=None, bias=None, a_blk=False, b_blk=False, out_blk=0, carry=None,
        n_tiles=None):
    w = b.shape[2] if b_blk else out_blk
    if mode == "nn":
        M, K = a.shape
        N = b.shape[0] * w if b_blk else b.shape[1]
    elif mode == "nt":
        M = a.shape[1] if a_blk else a.shape[0]
        N = b.shape[1] if b_blk else b.shape[0]
        K = b.shape[0] * w if b_blk else b.shape[1]
    else:
        K, M = a.shape
        N = b.shape[0] * w if b_blk else b.shape[1]
    tm = _pick(M, (1024, 768, 512, 256))
    tn = _pick(N, (1024, 768, 512, 256))
    tk = K if (mode != "tn" and K <= 2048) else _pick(K, (1024, 768, 512, 256))
    perm = lambda j: j
    if n_tiles:
        tn, perm = n_tiles
    if mode == "nt" and b_blk:
        tk = w
    elif b_blk or out_blk:
        tn = w
    nk = K // tk
    grid = (M // tm, N // tn, nk)
    if mode == "nn":
        a_spec = pl.BlockSpec((tm, tk), lambda i, j, k: (i, k))
        b_spec = (pl.BlockSpec((1, tk, tn), lambda i, j, k: (j, k, 0)) if b_blk
                  else pl.BlockSpec((tk, tn), lambda i, j, k: (k, j)))
        dims = (((1,), (0,)), ((), ()))
    elif mode == "nt":
        a_spec = (pl.BlockSpec((1, tm, tk), lambda i, j, k: (k, i, 0)) if a_blk
                  else pl.BlockSpec((tm, tk), lambda i, j, k: (i, k)))
        b_spec = (pl.BlockSpec((1, tn, tk), lambda i, j, k: (k, j, 0)) if b_blk
                  else pl.BlockSpec((tn, tk), lambda i, j, k: (perm(j), k)))
        dims = (((1,), (1,)), ((), ()))
    else:
        a_spec = pl.BlockSpec((tk, tm), lambda i, j, k: (k, i))
        b_spec = (pl.BlockSpec((1, tk, tn), lambda i, j, k: (j, k, 0)) if b_blk
                  else pl.BlockSpec((tk, tn), lambda i, j, k: (k, j)))
        dims = (((0,), (0,)), ((), ()))
    if out_blk:
        out_spec = pl.BlockSpec((1, tm, tn), lambda i, j, k: (j, i, 0))
        out_shape = jax.ShapeDtypeStruct((N // tn, M, tn), out_dtype)
    else:
        out_spec = pl.BlockSpec((tm, tn), lambda i, j, k: (i, j))
        out_shape = jax.ShapeDtypeStruct((M, N), out_dtype)
    extras, extra_specs = [], []
    if add is not None:
        extras.append(add)
        extra_specs.append(pl.BlockSpec((tm, tn), lambda i, j, k: (i, j)))
    if bias is not None:
        extras.append(bias)
        extra_specs.append(pl.BlockSpec((1, tn), lambda i, j, k: (0, j)))

    n_ex = len(extras)
    nc = carry.n if carry else 0

    def body(a_ref, b_ref, *rest):
        ex_refs, cx = rest[:n_ex], rest[n_ex:n_ex + nc]
        o_ref, cout = rest[n_ex + nc], rest[n_ex + nc + 1:n_ex + 2 * nc + 1]
        tail = rest[n_ex + 2 * nc + 1:]
        sems = tail[:3] if carry else ()
        first, last = _grid_edges(grid)
        if carry:
            @pl.when(first)
            def _():
                carry.start(cx, cout, sems)

        def product():
            av = a_ref[0] if a_blk else a_ref[...]
            bv = b_ref[0] if b_blk else b_ref[...]
            return lax.dot_general(av, bv, dims, preferred_element_type=F32)

        def finish(out):
            for r in ex_refs:
                out = out + r[...]
            if out_blk:
                o_ref[0] = out.astype(out_dtype)
            else:
                o_ref[...] = out.astype(out_dtype)

        if nk == 1:
            finish(product())
        else:
            acc_ref = tail[-1]
            k = pl.program_id(2)

            @pl.when(k == 0)
            def _():
                acc_ref[...] = jnp.zeros_like(acc_ref)

            acc_ref[...] += product()

            @pl.when(k == nk - 1)
            def _():
                finish(acc_ref[...])

        if carry:
            @pl.when(last)
            def _():
                carry.wait(cx, cout, sems)

    scratch = (carry.scratch if carry else []) + ([pltpu.VMEM((tm, tn), F32)] if nk > 1 else [])
    outs = pl.pallas_call(
        body, name=name, grid=grid,
        in_specs=[a_spec, b_spec] + extra_specs + (carry.specs if carry else []),
        out_specs=[out_spec] + (carry.specs if carry else []),
        out_shape=[out_shape] + (carry.out_shape if carry else []),
        scratch_shapes=scratch,
        compiler_params=_cparams(("arbitrary", "arbitrary", "arbitrary")),
    )(a, b, *extras, *(carry.xs if carry else []))
    return (outs[0], outs[1:]) if carry else outs[0]


def _rspec(tm, w, cb=0):
    return pl.BlockSpec((tm, w), lambda i: (i, cb))


def _fspec(shape):
    nd = len(shape)
    return pl.BlockSpec(shape, lambda i: (0,) * nd)


def _rowcall(body, name, L, tm, ins, row_outs, acc_outs=()):
    out_shape = [jax.ShapeDtypeStruct((L, w), dt) for w, dt in row_outs]
    out_shape += [jax.ShapeDtypeStruct(s, F32) for s in acc_outs]
    out_specs = [_rspec(tm, w) for w, _ in row_outs] + [_fspec(s) for s in acc_outs]
    return pl.pallas_call(
        body, name=name, grid=(L // tm,),
        in_specs=[s for _, s in ins], out_specs=out_specs, out_shape=out_shape,
        compiler_params=_cparams(("arbitrary",)),
    )(*[a for a, _ in ins])


def _acc(ref, val):
    @pl.when(pl.program_id(0) == 0)
    def _():
        ref[...] = jnp.zeros_like(ref)
    ref[...] += val


def _colsum(v):
    return jnp.sum(v, axis=0, keepdims=True)


def _rms(xv):
    return lax.rsqrt(jnp.mean(xv * xv, axis=-1, keepdims=True) + EPS)


def _rms_bwd(dn, xhat, r, g):
    dng = dn * g
    return r * (dng - xhat * jnp.mean(dng * xhat, axis=-1, keepdims=True))


def _sigmoid(v):
    return jax.nn.sigmoid(v)


def _f32(ref):
    return ref[...].astype(F32)


def _partner(v):
    w = v.shape[-1]
    lane = lax.broadcasted_iota(jnp.int32, v.shape, v.ndim - 1)
    first_half = (lane % 64) < 32
    return jnp.where(first_half, pltpu.roll(v, w - 32, axis=v.ndim - 1), pltpu.roll(v, 32, axis=v.ndim - 1))


def _norm_in(x, g, name):
    L, D = x.shape
    tm = _pick(L, (512, 256))

    def body(x_ref, g_ref, o_ref):
        xv = x_ref[...]
        o_ref[...] = (xv * _rms(xv) * g_ref[...]).astype(BF16)

    return _rowcall(body, name, L, tm, [(x, _rspec(tm, D)), (g, _fspec(g.shape))], [(D, BF16)])[0]


def _rope_tables(L):
    t = np.arange(L)
    rows = (t // GRID_W).astype(np.float32)
    cols = (t % GRID_W).astype(np.float32)
    n_freq = HEAD_DIM // 4
    inv_freq = np.float32(ROPE_THETA) ** (-np.arange(n_freq, dtype=np.float32) / np.float32(n_freq))
    ar = (rows[:, None] * inv_freq[None, :]).astype(np.float32).astype(np.float64)
    ac = (cols[:, None] * inv_freq[None, :]).astype(np.float32).astype(np.float64)
    cos = np.concatenate([np.cos(ar), np.cos(ar), np.cos(ac), np.cos(ac)], axis=-1).astype(np.float32)
    sin = np.concatenate([-np.sin(ar), np.sin(ar), -np.sin(ac), np.sin(ac)], axis=-1).astype(np.float32)
    return jnp.asarray(cos), jnp.asarray(sin)


def _qkv_prep(z, cos, sin, qn, kn):
    L = z.shape[0]
    tm = _pick(L, (512, 256))
    scale = HEAD_DIM ** -0.5
    kblk = 4 * D_ATTN // D_KV

    def body(q_ref, k_ref, v_ref, cos_ref, sin_ref, qn_ref, kn_ref, qo_ref, ko_ref, vo_ref, kt_ref):
        c, s = cos_ref[...], sin_ref[...]

        def head(xh, w):
            n = xh * _rms(xh) * w
            return n * c + _partner(n) * s

        for h in range(N_HEADS):
            sl = slice(h * HEAD_DIM, (h + 1) * HEAD_DIM)
            qo_ref[:, sl] = (head(q_ref[:, sl].astype(F32), qn_ref[...]) * scale).astype(BF16)
        for h in range(N_KV):
            sl = slice(h * HEAD_DIM, (h + 1) * HEAD_DIM)
            kr = head(k_ref[:, sl].astype(F32), kn_ref[...])
            ko_ref[:, sl] = kr.astype(BF16)
            kt_ref[sl, :] = kr.T.astype(BF16)
        vo_ref[...] = v_ref[...].astype(BF16)

    return pl.pallas_call(
        body, name="qkv_prep", grid=(L // tm,),
        in_specs=[_rspec(tm, D_ATTN, 0), _rspec(tm, D_KV, kblk), _rspec(tm, D_KV, kblk + 1),
                  _rspec(tm, HEAD_DIM), _rspec(tm, HEAD_DIM), _fspec(qn.shape), _fspec(kn.shape)],
        out_specs=[_rspec(tm, D_ATTN), _rspec(tm, D_KV), _rspec(tm, D_KV),
                   pl.BlockSpec((D_KV, tm), lambda i: (0, i))],
        out_shape=[jax.ShapeDtypeStruct((L, D_ATTN), BF16), jax.ShapeDtypeStruct((L, D_KV), BF16),
                   jax.ShapeDtypeStruct((L, D_KV), BF16), jax.ShapeDtypeStruct((D_KV, L), BF16)],
        compiler_params=_cparams(("arbitrary",)),
    )(z, z, z, cos, sin, qn, kn)


def _col_to_row(col):
    n = col.shape[0]
    eye = lax.broadcasted_iota(jnp.int32, (n, n), 0) == lax.broadcasted_iota(jnp.int32, (n, n), 1)
    return jnp.sum(jnp.where(eye, col, 0.0), axis=0, keepdims=True)


def _attn_fwd(q, k, v, carry=None):
    L = q.shape[0]
    tq = _pick(L, (256, 128))
    grid = (N_HEADS, L // tq)
    nc = carry.n if carry else 0

    def body(q_ref, k_ref, v_ref, *rest):
        cx, (o_ref, lse_ref) = rest[:nc], rest[nc:nc + 2]
        cout, sems = rest[nc + 2:2 * nc + 2], rest[2 * nc + 2:]
        first, last = _grid_edges(grid)
        if carry:
            @pl.when(first)
            def _():
                carry.start(cx, cout, sems)

        s = lax.dot_general(q_ref[...], k_ref[...], (((1,), (1,)), ((), ())), preferred_element_type=F32)
        m = jnp.max(s, axis=-1, keepdims=True)
        e = jnp.exp(s - m)
        l = jnp.sum(e, axis=-1, keepdims=True)
        o_ref[...] = jnp.dot(e.astype(BF16), v_ref[...], preferred_element_type=F32) / l
        lse_ref[0] = _col_to_row(m + jnp.log(l))

        if carry:
            @pl.when(last)
            def _():
                carry.wait(cx, cout, sems)

    outs = pl.pallas_call(
        body, name="attn_fwd", grid=grid,
        in_specs=[pl.BlockSpec((tq, HEAD_DIM), lambda h, i: (i, h)),
                  pl.BlockSpec((L, HEAD_DIM), lambda h, i: (0, h // REP)),
                  pl.BlockSpec((L, HEAD_DIM), lambda h, i: (0, h // REP))] + (carry.specs if carry else []),
        out_specs=[pl.BlockSpec((tq, HEAD_DIM), lambda h, i: (i, h)),
                   pl.BlockSpec((1, 1, tq), lambda h, i: (h, 0, i))] + (carry.specs if carry else []),
        out_shape=[jax.ShapeDtypeStruct((L, D_ATTN), F32), jax.ShapeDtypeStruct((N_HEADS, 1, L), F32)]
        + (carry.out_shape if carry else []),
        scratch_shapes=carry.scratch if carry else [],
        compiler_params=_cparams(("arbitrary", "arbitrary")),
    )(q, k, v, *(carry.xs if carry else []))
    return outs[0], outs[1], outs[2:]


def _attn_bwd(q, k, v, kt, do, o, lse, carry=None):
    L = q.shape[0]
    tq = _pick(L, (256, 128))
    kc = _pick(L, (512, 256, 128))
    nt = (((1,), (1,)), ((), ()))
    grid = (N_KV, REP, L // tq)
    nc = carry.n if carry else 0

    def body(q_ref, do_ref, o_ref, lse_ref, k_ref, v_ref, kt_ref, *rest):
        cx, (dq_ref, dk_ref, dv_ref) = rest[:nc], rest[nc:nc + 3]
        cout, sems = rest[nc + 3:2 * nc + 3], rest[2 * nc + 3:]
        first, last = _grid_edges(grid)
        if carry:
            @pl.when(first)
            def _():
                carry.start(cx, cout, sems)

        @pl.when((pl.program_id(1) == 0) & (pl.program_id(2) == 0))
        def _():
            dk_ref[...] = jnp.zeros_like(dk_ref)
            dv_ref[...] = jnp.zeros_like(dv_ref)

        qv, dov = q_ref[...], do_ref[...]
        lse_row = lse_ref[0]
        delta = _col_to_row(jnp.sum(dov.astype(F32) * o_ref[...], axis=-1, keepdims=True))
        dqt = jnp.zeros((HEAD_DIM, tq), F32)
        for c in range(L // kc):
            sl = slice(c * kc, (c + 1) * kc)
            st = lax.dot_general(k_ref[sl, :], qv, nt, preferred_element_type=F32)
            pt = jnp.exp(st - lse_row)
            dpt = lax.dot_general(v_ref[sl, :], dov, nt, preferred_element_type=F32)
            dst = (pt * (dpt - delta)).astype(BF16)
            dv_ref[sl, :] += jnp.dot(pt.astype(BF16), dov, preferred_element_type=F32)
            dk_ref[sl, :] += jnp.dot(dst, qv, preferred_element_type=F32)
            dqt = dqt + jnp.dot(kt_ref[:, sl], dst, preferred_element_type=F32)
        dq_ref[...] = dqt.T

        if carry:
            @pl.when(last)
            def _():
                carry.wait(cx, cout, sems)

    head = lambda g, r, i: (i, g * REP + r)
    outs = pl.pallas_call(
        body, name="attn_bwd", grid=grid,
        in_specs=[pl.BlockSpec((tq, HEAD_DIM), head), pl.BlockSpec((tq, HEAD_DIM), head),
                  pl.BlockSpec((tq, HEAD_DIM), head),
                  pl.BlockSpec((1, 1, tq), lambda g, r, i: (g * REP + r, 0, i)),
                  pl.BlockSpec((L, HEAD_DIM), lambda g, r, i: (0, g)),
                  pl.BlockSpec((L, HEAD_DIM), lambda g, r, i: (0, g)),
                  pl.BlockSpec((HEAD_DIM, L), lambda g, r, i: (g, 0))] + (carry.specs if carry else []),
        out_specs=[pl.BlockSpec((tq, HEAD_DIM), head),
                   pl.BlockSpec((L, HEAD_DIM), lambda g, r, i: (0, g)),
                   pl.BlockSpec((L, HEAD_DIM), lambda g, r, i: (0, g))] + (carry.specs if carry else []),
        out_shape=[jax.ShapeDtypeStruct((L, D_ATTN), F32), jax.ShapeDtypeStruct((L, D_KV), F32),
                   jax.ShapeDtypeStruct((L, D_KV), F32)] + (carry.out_shape if carry else []),
        scratch_shapes=carry.scratch if carry else [],
        compiler_params=_cparams(("arbitrary", "arbitrary", "arbitrary")),
    )(q, do, o, lse, k, v, kt, *(carry.xs if carry else []))
    return outs[0], outs[1], outs[2], outs[3:]


def _seg_perm(a):
    L, C = a.shape
    return a.reshape(SEG, L // SEG, C).transpose(1, 0, 2).reshape(L, C)


def _seg_unperm(a):
    L, C = a.shape
    return a.reshape(L // SEG, SEG, C).transpose(1, 0, 2).reshape(L, C)


def _cmul(ar, ai, br, bi):
    return ar * br - ai * bi, ar * bi + ai * br


def _rows8(rr):
    return pl.ds(pl.multiple_of(rr * SEG, SEG), SEG)


def _seg_scan(xr_ref, xi_ref, ar, ai, reverse, n_rows, visit=None, visit_init=()):
    shape = ar.shape
    zero = jnp.zeros(shape, F32)

    def index(r):
        return (n_rows - 1 - r) if reverse else r

    def ends(r, carry):
        cr, ci = carry
        sl = _rows8(index(r))
        pr, pi = _cmul(ar, ai, cr, ci)
        return pr + xr_ref[sl, :], pi + xi_ref[sl, :]

    er, ei = lax.fori_loop(0, n_rows, ends, (zero, zero))

    pr, pi = ar, ai
    for _ in range(int(math.log2(n_rows))):
        pr, pi = _cmul(pr, pi, pr, pi)
    sub = lax.broadcasted_iota(jnp.int32, shape, 0)
    shift = (SEG - 1) if reverse else 1
    edge = (SEG - 1) if reverse else 0
    inr, ini = zero, zero
    for _ in range(SEG - 1):
        tr, ti = _cmul(pr, pi, inr, ini)
        inr = jnp.where(sub == edge, 0.0, pltpu.roll(tr + er, shift, axis=0))
        ini = jnp.where(sub == edge, 0.0, pltpu.roll(ti + ei, shift, axis=0))

    def step(rr, carry, last):
        cr, ci = carry[:2]
        sl = _rows8(rr)
        pr, pi = _cmul(ar, ai, cr, ci)
        nr, ni = pr + xr_ref[sl, :], pi + xi_ref[sl, :]
        xr_ref[sl, :] = nr
        xi_ref[sl, :] = ni
        acc = visit(rr, nr, ni, carry[2:], last) if visit else ()
        return (nr, ni, *acc)

    carry = lax.fori_loop(0, n_rows - 1, lambda r, c: step(index(r), c, False), (inr, ini, *visit_init))
    carry = step(index(n_rows - 1), carry, True)
    return inr, ini, carry[2:]


def _discretise(a_re, a_im, ldt):
    lr = jnp.minimum(a_re, -1e-4)
    li = a_im
    dt = jnp.exp(ldt)
    mag = jnp.exp(lr * dt)
    lbr = mag * jnp.cos(li * dt)
    lbi = mag * jnp.sin(li * dt)
    den = lr * lr + li * li
    nr = lbr - 1.0
    fr = (nr * lr + lbi * li) / den
    fi = (lbi * lr - nr * li) / den
    return lr, li, dt, lbr, lbi, fr, fi


def _lane_row(v):
    return jnp.concatenate([v[g:g + 1, :] for g in range(v.shape[0])], axis=1)


def _ssm_fill_maps(d, prm, tmp_ref, maps):
    a_re_ref, a_im_ref, ldt_ref, bt_re_ref, bt_im_ref, c_re_ref, c_im_ref = prm
    _, _, _, lbr, lbi, fr, fi = _discretise(a_re_ref[d], a_im_ref[d], ldt_ref[d])

    def fill(dst, piece):
        tmp_ref[...] = jnp.zeros_like(tmp_ref)
        for g in range(SLAB_G):
            tmp_ref[g * SSM_H:(g + 1) * SSM_H, g * SSM_P:(g + 1) * SSM_P] = piece(g)
        dst[...] = tmp_ref[...].astype(BF16)

    wbr, wbi, wcr, wci = maps
    fill(wbr, lambda g: fr[g:g + 1] * bt_re_ref[d, g] - fi[g:g + 1] * bt_im_ref[d, g])
    fill(wbi, lambda g: fr[g:g + 1] * bt_im_ref[d, g] + fi[g:g + 1] * bt_re_ref[d, g])
    fill(wcr, lambda g: c_re_ref[d, g])
    fill(wci, lambda g: c_im_ref[d, g])
    return _lane_row(lbr), _lane_row(lbi)


def _ssm_param_specs():
    pole = pl.BlockSpec((2, SLAB_G, SSM_P), lambda j: (0, j, 0))
    step = pl.BlockSpec((2, SLAB_G, 1), lambda j: (0, j, 0))
    mat = pl.BlockSpec((2, SLAB_G, SSM_H, SSM_P), lambda j: (0, j, 0, 0))
    return [pole, pole, step, mat, mat, mat, mat]


_MAP_SCRATCH = [pltpu.VMEM((SLAB, SLAB_S), F32)] + [pltpu.VMEM((SLAB, SLAB_S), BF16)] * 4
_NT = (((1,), (1,)), ((), ()))


def _ssm_fwd(u, prm, dskip):
    L, C = u.shape
    n_rows = L // SEG
    tc = _pick(L, (1024, 512, 256))
    u_spec = pl.BlockSpec((L, SLAB), lambda j: (0, j))
    d_spec = pl.BlockSpec((1, SLAB), lambda j: (0, j))

    def body(u_ref, *rest):
        prm_refs, d_ref, y_ref = rest[:7], rest[7], rest[8]
        tmp_ref, maps, xr_ref, xi_ref = rest[9], rest[10:14], rest[14], rest[15]
        wbr, wbi, wcr, wci = maps
        y_ref[...] = u_ref[...] * d_ref[...]
        for d in range(2):
            lam_r, lam_i = _ssm_fill_maps(d, prm_refs, tmp_ref, maps)

            def inp(c, _):
                sl = pl.ds(pl.multiple_of(c * tc, tc), tc)
                ub = u_ref[sl, :].astype(BF16)
                xr_ref[sl, :] = jnp.dot(ub, wbr[...], preferred_element_type=F32)
                xi_ref[sl, :] = jnp.dot(ub, wbi[...], preferred_element_type=F32)
                return 0

            lax.fori_loop(0, L // tc, inp, 0)
            ar = jnp.broadcast_to(lam_r, (SEG, SLAB_S))
            ai = jnp.broadcast_to(lam_i, (SEG, SLAB_S))
            _seg_scan(xr_ref, xi_ref, ar, ai, d == 1, n_rows)

            def outp(c, _):
                sl = pl.ds(pl.multiple_of(c * tc, tc), tc)
                y_ref[sl, :] += (
                    lax.dot_general(xr_ref[sl, :].astype(BF16), wcr[...], _NT, preferred_element_type=F32)
                    - lax.dot_general(xi_ref[sl, :].astype(BF16), wci[...], _NT, preferred_element_type=F32))
                return 0

            lax.fori_loop(0, L // tc, outp, 0)

    return pl.pallas_call(
        body, name="ssm_fwd", grid=(C // SLAB,),
        in_specs=[u_spec] + _ssm_param_specs() + [d_spec],
        out_specs=u_spec, out_shape=jax.ShapeDtypeStruct((L, C), F32),
        scratch_shapes=_MAP_SCRATCH + [pltpu.VMEM((L, SLAB_S), F32)] * 2,
        compiler_params=_cparams(("arbitrary",)),
    )(u, *prm, dskip)


def _ssm_bwd(u, dy, prm, dskip):
    L, C = u.shape
    n_rows = L // SEG
    n_slab = C // SLAB
    tc = _pick(L, (1024, 512, 256))
    u_spec = pl.BlockSpec((L, SLAB), lambda j: (0, j))
    d_spec = pl.BlockSpec((1, SLAB), lambda j: (0, j))
    pg_spec = pl.BlockSpec((1, 2, PG_ROWS, SLAB_S), lambda j: (j, 0, 0, 0))

    def body(u_ref, dy_ref, *rest):
        prm_refs, d_ref, du_ref, pg_ref = rest[:7], rest[7], rest[8], rest[9]
        tmp_ref, maps, acc_ref = rest[10], rest[11:15], rest[15]
        xr_ref, xi_ref, gr_ref, gi_ref = rest[16:20]
        wbr, wbi, wcr, wci = maps
        du_ref[...] = dy_ref[...] * d_ref[...]
        pg_ref[...] = jnp.zeros_like(pg_ref)
        pg_ref[0, 0, 66:67, 0:SLAB] = _colsum(dy_ref[...] * u_ref[...])
        for d in range(2):
            lam_r, lam_i = _ssm_fill_maps(d, prm_refs, tmp_ref, maps)

            def inp(c, _):
                sl = pl.ds(pl.multiple_of(c * tc, tc), tc)
                ub = u_ref[sl, :].astype(BF16)
                dyb = dy_ref[sl, :].astype(BF16)
                xr_ref[sl, :] = jnp.dot(ub, wbr[...], preferred_element_type=F32)
                xi_ref[sl, :] = jnp.dot(ub, wbi[...], preferred_element_type=F32)
                gr_ref[sl, :] = jnp.dot(dyb, wcr[...], preferred_element_type=F32)
                gi_ref[sl, :] = -jnp.dot(dyb, wci[...], preferred_element_type=F32)
                return 0

            lax.fori_loop(0, L // tc, inp, 0)
            ar = jnp.broadcast_to(lam_r, (SEG, SLAB_S))
            ai = jnp.broadcast_to(lam_i, (SEG, SLAB_S))
            inr, ini, _ = _seg_scan(xr_ref, xi_ref, ar, ai, d == 1, n_rows)

            def pole(rr, lr, li, acc, last):
                if last:
                    pr, pi = inr, ini
                else:
                    nb = _rows8(rr + 1 if d == 1 else rr - 1)
                    pr, pi = xr_ref[nb, :], xi_ref[nb, :]
                return acc[0] + lr * pr + li * pi, acc[1] + li * pr - lr * pi

            zero = jnp.zeros((SEG, SLAB_S), F32)
            _, _, (accr, acci) = _seg_scan(gr_ref, gi_ref, ar, -ai, d == 0, n_rows, pole, (zero, zero))
            pg_ref[0, d, 64:65, :] = _colsum(accr)
            pg_ref[0, d, 65:66, :] = _colsum(acci)

            acc_ref[...] = jnp.zeros_like(acc_ref)

            def outp(c, _):
                sl = pl.ds(pl.multiple_of(c * tc, tc), tc)
                lrb, lib = gr_ref[sl, :].astype(BF16), gi_ref[sl, :].astype(BF16)
                du_ref[sl, :] += (lax.dot_general(lrb, wbr[...], _NT, preferred_element_type=F32)
                                  + lax.dot_general(lib, wbi[...], _NT, preferred_element_type=F32))
                ut = u_ref[sl, :].T.astype(BF16)
                dyt = dy_ref[sl, :].T.astype(BF16)
                acc_ref[0] += jnp.dot(ut, lrb, preferred_element_type=F32)
                acc_ref[1] += jnp.dot(ut, lib, preferred_element_type=F32)
                acc_ref[2] += jnp.dot(dyt, xr_ref[sl, :].astype(BF16), preferred_element_type=F32)
                acc_ref[3] -= jnp.dot(dyt, xi_ref[sl, :].astype(BF16), preferred_element_type=F32)
                return 0

            lax.fori_loop(0, L // tc, outp, 0)
            for m in range(4):
                for g in range(SLAB_G):
                    lanes = slice(g * SSM_P, (g + 1) * SSM_P)
                    pg_ref[0, d, m * SSM_H:(m + 1) * SSM_H, lanes] = acc_ref[m, g * SSM_H:(g + 1) * SSM_H, lanes]

    return pl.pallas_call(
        body, name="ssm_bwd", grid=(n_slab,),
        in_specs=[u_spec, u_spec] + _ssm_param_specs() + [d_spec],
        out_specs=[u_spec, pg_spec],
        out_shape=[jax.ShapeDtypeStruct((L, C), F32), jax.ShapeDtypeStruct((n_slab, 2, PG_ROWS, SLAB_S), F32)],
        scratch_shapes=_MAP_SCRATCH + [pltpu.VMEM((4, SLAB, SLAB_S), F32)] + [pltpu.VMEM((L, SLAB_S), F32)] * 4,
        compiler_params=_cparams(("arbitrary",), 60 << 20),
    )(u, dy, *prm, dskip)


def _ssm_param_grads(pg, prm):
    n_slab = pg.shape[0]
    G = n_slab * SLAB_G
    pg_spec = pl.BlockSpec((1, 2, PG_ROWS, SLAB_S), lambda j: (j, 0, 0, 0))
    pole, _, step, mat = _ssm_param_specs()[:4]

    def body(pg_ref, a_re_ref, a_im_ref, ldt_ref, bt_re_ref, bt_im_ref,
             dbr_ref, dbi_ref, dcr_ref, dci_ref, dar_ref, dai_ref, dldt_ref, dd_ref):
        dd_ref[...] = pg_ref[0, 0, 66:67, 0:SLAB]
        for d in range(2):
            a_r = a_re_ref[d]
            lr, li, dt, lbr, lbi, f_r, f_i = _discretise(a_r, a_im_ref[d], ldt_ref[d])
            gfr_rows, gfi_rows, glr_rows, gli_rows = [], [], [], []
            for g in range(SLAB_G):
                lanes = slice(g * SSM_P, (g + 1) * SSM_P)
                gbr, gbi = pg_ref[0, d, 0:SSM_H, lanes], pg_ref[0, d, SSM_H:2 * SSM_H, lanes]
                b_r, b_i = bt_re_ref[d, g], bt_im_ref[d, g]
                fr, fi = f_r[g:g + 1], f_i[g:g + 1]
                dbr_ref[d, g] = fr * gbr + fi * gbi
                dbi_ref[d, g] = fr * gbi - fi * gbr
                gfr_rows.append(_colsum(gbr * b_r + gbi * b_i))
                gfi_rows.append(_colsum(gbi * b_r - gbr * b_i))
                dcr_ref[d, g] = pg_ref[0, d, 2 * SSM_H:3 * SSM_H, lanes]
                dci_ref[d, g] = pg_ref[0, d, 3 * SSM_H:4 * SSM_H, lanes]
                glr_rows.append(pg_ref[0, d, 64:65, lanes])
                gli_rows.append(pg_ref[0, d, 65:66, lanes])
            gfr, gfi = jnp.concatenate(gfr_rows, axis=0), jnp.concatenate(gfi_rows, axis=0)
            glr, gli = jnp.concatenate(glr_rows, axis=0), jnp.concatenate(gli_rows, axis=0)
            den = lr * lr + li * li
            ir, ii = lr / den, -li / den
            tr, ti = _cmul(ir, -ii, gfr, gfi)
            glbr, glbi = glr + tr, gli + ti
            qr, qi = _cmul(f_r, f_i, ir, ii)
            dlr, dli = _cmul(-qr, qi, gfr, gfi)
            zr, zi = _cmul(lbr, -lbi, glbr, glbi)
            dlr = dlr + dt * zr
            dli = dli + dt * zi
            dar_ref[d] = jnp.where(a_r < -1e-4, dlr, jnp.where(a_r == -1e-4, 0.5 * dlr, 0.0))
            dai_ref[d] = dli
            dldt_ref[d] = jnp.sum(lr * zr + li * zi, axis=-1, keepdims=True) * dt

    a_re, a_im, ldt, bt_re, bt_im = prm[:5]
    mshape = jax.ShapeDtypeStruct(bt_re.shape, F32)
    pshape = jax.ShapeDtypeStruct(a_re.shape, F32)
    return pl.pallas_call(
        body, name="ssm_param_grads", grid=(n_slab,),
        in_specs=[pg_spec, pole, pole, step, mat, mat],
        out_specs=[mat, mat, mat, mat, pole, pole, step, pl.BlockSpec((1, SLAB), lambda j: (0, j))],
        out_shape=[mshape, mshape, mshape, mshape, pshape, pshape, jax.ShapeDtypeStruct(ldt.shape, F32),
                   jax.ShapeDtypeStruct((1, n_slab * SLAB), F32)],
        compiler_params=_cparams(("arbitrary",)),
    )(pg, a_re, a_im, ldt, bt_re, bt_im)


def _peer(k, x, y, c):
    return (1 - x if k & 4 else x, 1 - y if k & 2 else y, 1 - c if k & 1 else c)


def _dev_index(pos):
    return 4 * pos[0] + 2 * pos[1] + pos[2]


def _chip_index(pos):
    return 2 * pos[0] + pos[1]


def _sibling_swap(x, name):
    any_spec = pl.BlockSpec(memory_space=pl.ANY)

    def body(x_ref, out_ref, send_sems, recv_sems):
        x_, y_, c_ = lax.axis_index("x"), lax.axis_index("y"), lax.axis_index("c")
        copies = [pltpu.make_async_remote_copy(
            src_ref=x_ref.at[2 * chip + (1 - c_)], dst_ref=out_ref.at[chip], send_sem=send_sems.at[chip],
            recv_sem=recv_sems.at[chip], device_id=(x_, y_, 1 - c_), device_id_type=pl.DeviceIdType.MESH)
            for chip in range(N_CHIPS)]
        for cp in copies:
            cp.start()
        for cp in copies:
            cp.wait()

    return pl.pallas_call(
        body, name=name, out_shape=jax.ShapeDtypeStruct((N_CHIPS,) + x.shape[1:], x.dtype),
        in_specs=[any_spec], out_specs=any_spec,
        scratch_shapes=[pltpu.SemaphoreType.DMA((N_CHIPS,)), pltpu.SemaphoreType.DMA((N_CHIPS,))],
    )(x)


def _pair_sum(a, b, name):
    n, R, W = a.shape
    tr = _row_tile(R, W, 1 << 20)

    def body(a_ref, b_ref, o_ref):
        o_ref[...] = (a_ref[...].astype(F32) + b_ref[...].astype(F32)).astype(BF16)

    spec = pl.BlockSpec((1, tr, W), lambda i, j: (i, j, 0))
    return pl.pallas_call(
        body, name=name, grid=(n, R // tr), in_specs=[spec, spec], out_specs=spec,
        out_shape=jax.ShapeDtypeStruct((n, R, W), BF16), compiler_params=_cparams(("arbitrary", "arbitrary")),
    )(a, b)


def _all_gather(xs, name):
    n = len(xs)
    any_spec = pl.BlockSpec(memory_space=pl.ANY)

    def body(*refs):
        x_refs, out_refs = refs[:n], refs[n:2 * n]
        send_sems, recv_sems, local_sems = refs[2 * n:]
        x, y, c = lax.axis_index("x"), lax.axis_index("y"), lax.axis_index("c")
        me, sibling = (x, y, c), (x, y, 1 - c)
        chips = [(1 - x, y), (x, 1 - y), (1 - x, 1 - y)]

        def copy(a, k, block, to, src=None):
            dst = out_refs[a].at[_dev_index(block)]
            return pltpu.make_async_remote_copy(
                src_ref=dst if src is None else src, dst_ref=dst,
                send_sem=send_sems.at[a, k], recv_sem=recv_sems.at[a, k],
                device_id=to, device_id_type=pl.DeviceIdType.MESH)

        mine = [pltpu.make_async_copy(x_refs[a], out_refs[a].at[_dev_index(me)], local_sems.at[a]) for a in range(n)]
        for cp in mine:
            cp.start()
        first = []
        for a in range(n):
            first.append(copy(a, 0, me, sibling, src=x_refs[a]))
            first += [copy(a, 1 + j, me, (*chip, c), src=x_refs[a]) for j, chip in enumerate(chips)]
        for cp in first:
            cp.start()
        passed = []
        for j, chip in enumerate(chips):
            for a in range(n):
                copy(a, 1 + j, (*chip, c), me).wait_recv()
                fwd = copy(a, 4 + j, (*chip, c), sibling)
                fwd.start()
                passed.append(fwd)
        for a in range(n):
            copy(a, 0, sibling, me).wait_recv()
            for j, chip in enumerate(chips):
                copy(a, 4 + j, (*chip, 1 - c), me).wait_recv()
        for cp in first + passed:
            cp.wait_send()
        for cp in mine:
            cp.wait()

    return pl.pallas_call(
        body, name=name,
        out_shape=[jax.ShapeDtypeStruct((N_DEV,) + v.shape, v.dtype) for v in xs],
        in_specs=[any_spec] * n, out_specs=[any_spec] * n,
        scratch_shapes=[pltpu.SemaphoreType.DMA((n, 7)), pltpu.SemaphoreType.DMA((n, 7)),
                        pltpu.SemaphoreType.DMA((n,))],
    )(*xs)


def _sum_blocks(x, name):
    _, R, W = x.shape

    def body(x_ref, o_ref):
        acc = x_ref[0].astype(F32)
        for d in range(1, N_DEV):
            acc = acc + x_ref[d].astype(F32)
        o_ref[...] = acc

    return pl.pallas_call(body, name=name, out_shape=jax.ShapeDtypeStruct((R, W), F32),
                          compiler_params=pltpu.CompilerParams(vmem_limit_bytes=VMEM_LIMIT))(x)


def _adam_update(w, g, m, v):
    mn = ADAM_B1 * m + (1.0 - ADAM_B1) * g
    vn = ADAM_B2 * v + (1.0 - ADAM_B2) * (g * g)
    m_hat = mn / (1.0 - ADAM_B1 ** ADAM_STEP)
    v_hat = vn / (1.0 - ADAM_B2 ** ADAM_STEP)
    return -ADAM_LR * (m_hat / (jnp.sqrt(v_hat) + ADAM_EPS) + ADAM_WD * w), mn, vn


def _row_tile(R, W, budget):
    padded = -(-W // LANES) * LANES * 4
    if R * padded <= budget:
        return R
    return _pick(R, [t for t in (2048, 1024, 512, 256, 128, 64, 32, 16, 8) if t * padded <= budget])


def _adamw(w, g, m, v, name):
    R, W = w.shape
    tr = _row_tile(R, W, 1 << 20)

    def body(w_ref, g_ref, m_ref, v_ref, d_ref, mo_ref, vo_ref):
        d_ref[...], mo_ref[...], vo_ref[...] = _adam_update(w_ref[...], g_ref[...], m_ref[...], v_ref[...])

    spec = pl.BlockSpec((tr, W), lambda i: (i, 0))
    shp = jax.ShapeDtypeStruct((R, W), F32)
    return pl.pallas_call(
        body, name=name, grid=(R // tr,), in_specs=[spec] * 4, out_specs=[spec] * 3, out_shape=[shp] * 3,
        compiler_params=_cparams(("arbitrary",)),
    )(w, g, m, v)


def _adamw_reduce(w, land, m, v, name):
    R, W = w.shape
    n = land.shape[0]
    tr = _row_tile(R, W, 1 << 20)

    def body(w_ref, l_ref, m_ref, v_ref, g_ref, d_ref, mo_ref, vo_ref):
        g = l_ref[0].astype(F32)
        for d in range(1, n):
            g = g + l_ref[d].astype(F32)
        g_ref[...] = g
        d_ref[...], mo_ref[...], vo_ref[...] = _adam_update(w_ref[...], g, m_ref[...], v_ref[...])

    spec = pl.BlockSpec((tr, W), lambda i: (i, 0))
    lspec = pl.BlockSpec((n, tr, W), lambda i: (0, i, 0))
    shp = jax.ShapeDtypeStruct((R, W), F32)
    return pl.pallas_call(
        body, name=name, grid=(R // tr,), in_specs=[spec, lspec, spec, spec], out_specs=[spec] * 4,
        out_shape=[shp] * 4, compiler_params=_cparams(("arbitrary",)),
    )(w, land, m, v)


def _gelu(v):
    c = math.sqrt(2.0 / math.pi)
    return 0.5 * v * (1.0 + jnp.tanh(c * (v + 0.044715 * v * v * v)))


def _gelu_grad(v):
    c = math.sqrt(2.0 / math.pi)
    t = jnp.tanh(c * (v + 0.044715 * v * v * v))
    return 0.5 * (1.0 + t) + 0.5 * v * (1.0 - t * t) * c * (1.0 + 3.0 * 0.044715 * v * v)


def kernel(x, p, norm_mix, w_in, q_norm, k_norm, ssm_a_re, ssm_a_im, ssm_log_dt, ssm_b_re, ssm_b_im, ssm_c_re, ssm_c_im, ssm_d, w_glu, b_glu, w_out, norm_ple, w_ple_gate, w_ple_proj, norm_final, loss_target, m_norm_mix, m_w_in, m_q_norm, m_k_norm, m_ssm_a_re, m_ssm_a_im, m_ssm_log_dt, m_ssm_b_re, m_ssm_b_im, m_ssm_c_re, m_ssm_c_im, m_ssm_d, m_w_glu, m_b_glu, m_w_out, m_norm_ple, m_w_ple_gate, m_w_ple_proj, m_norm_final, v_norm_mix, v_w_in, v_q_norm, v_k_norm, v_ssm_a_re, v_ssm_a_im, v_ssm_log_dt, v_ssm_b_re, v_ssm_b_im, v_ssm_c_re, v_ssm_c_im, v_ssm_d, v_w_glu, v_b_glu, v_w_out, v_norm_ple, v_w_ple_gate, v_w_ple_proj, v_norm_final):
    L, D = x.shape[1], x.shape[2]
    D_SSM = ssm_d.shape[1]
    G = D_SSM // SSM_H
    n_slab = D_SSM // SLAB
    n_in = w_in.shape[2]
    D_IN = n_in * N_DEV
    n_pp = w_ple_proj.shape[2]
    n_glu = w_glu.shape[2]
    xs = x[0]
    ps = p[0, 0]
    tgt = loss_target[0]

    (win_t3,) = _all_gather([w_in[0].T.astype(BF16)], "gather_w_in")
    win_t = win_t3.reshape(D_IN, D)
    later_weights = _Carry("gather", [w_glu[0].astype(BF16), w_out[0].astype(BF16), w_ple_gate[0].astype(BF16),
                                      w_ple_proj[0].astype(BF16)])

    ssm_prm = (ssm_a_re[0], ssm_a_im[0], ssm_log_dt[0].reshape(2, G, 1),
               ssm_b_re[0].transpose(0, 1, 3, 2), ssm_b_im[0].transpose(0, 1, 3, 2), ssm_c_re[0], ssm_c_im[0])

    cos, sin = _rope_tables(L)
    hn = _norm_in(xs, norm_mix, "norm_mix")
    ZT = 512
    zp_tile = lambda j: jnp.where(j < 2, j, jnp.where(j < D_IN // ZT - 1, j + 1, 2))
    z = _mm(hn, win_t, "nt", "in_proj", out_dtype=BF16, n_tiles=(ZT, zp_tile))
    qr, kr, vb, kt = _qkv_prep(z, cos, sin, q_norm, k_norm)
    o, lse, (wglu3, wout3, wpg3, wpp3) = _attn_fwd(qr, kr, vb, later_weights)
    wout = wout3.reshape(-1, D)
    wpg = wpg3.reshape(-1, D)
    u_off = 2 * D_ATTN
    u_perm = _seg_perm(z[:, u_off:u_off + D_SSM]).astype(F32)
    ys = _seg_unperm(_ssm_fwd(u_perm, ssm_prm, ssm_d))

    tm = _pick(L, (256,))

    def gelu_body(y_ref, o_ref):
        o_ref[...] = _gelu(y_ref[...]).astype(BF16)

    (gy,) = _rowcall(gelu_body, "gelu", L, tm, [(ys, _rspec(tm, D_SSM))], [(D_SSM, BF16)])
    glu = _mm(gy, wglu3, "nn", "glu_proj", out_dtype=BF16, bias=b_glu, b_blk=True)

    def mix_body(o_ref, ga_ref, gla_ref, glb_ref, gs_ref, cat_ref):
        ga, gs = _f32(ga_ref), _f32(gs_ref)
        cat_ref[:, :D_ATTN] = (o_ref[...] * ga * _sigmoid(ga)).astype(BF16)
        cat_ref[:, D_ATTN:] = (_f32(gla_ref) * _sigmoid(_f32(glb_ref)) * gs * _sigmoid(gs)).astype(BF16)

    (cat,) = _rowcall(mix_body, "mix", L, tm,
                      [(o, _rspec(tm, D_ATTN)), (z, _rspec(tm, D_ATTN, 1)), (glu, _rspec(tm, D_SSM, 0)),
                       (glu, _rspec(tm, D_SSM, 1)), (z, _rspec(tm, D_SSM, 3))], [(D_ATTN + D_SSM, BF16)])
    h1 = _mm(cat, wout, "nn", "out_proj", add=xs)
    n2 = _norm_in(h1, norm_ple, "norm_ple")
    gpre = _mm(n2, wpg, "nn", "ple_gate", out_dtype=BF16)
    pb = ps.astype(BF16)
    pp = _mm(pb, wpp3, "nn", "ple_proj", out_dtype=BF16, b_blk=True)

    nf = norm_final.reshape(1, D)

    def tail_body(h1_ref, gp_ref, pp_ref, t_ref, g_ref, dh2_ref, dpp_ref, dsg_ref, loss_ref, dg_ref):
        gate = _sigmoid(_f32(gp_ref))
        ppv = _f32(pp_ref)
        h2 = h1_ref[...] + gate * ppv
        r = _rms(h2)
        hh = h2 * r
        err = hh * g_ref[...] - t_ref[...]
        _acc(loss_ref, jnp.broadcast_to(0.5 * jnp.sum(jnp.mean(err * err, axis=-1, keepdims=True)), loss_ref.shape))
        dy = err * (1.0 / D)
        _acc(dg_ref, _colsum(dy * hh))
        dh2 = _rms_bwd(dy, hh, r, g_ref[...])
        dh2_ref[...] = dh2
        dpp_ref[...] = (dh2 * gate).astype(BF16)
        dsg_ref[...] = (dh2 * ppv * gate * (1.0 - gate)).astype(BF16)

    dh2, dpp, dsg, loss_acc, d_nf = _rowcall(
        tail_body, "tail", L, tm,
        [(h1, _rspec(tm, D)), (gpre, _rspec(tm, D)), (pp, _rspec(tm, D)), (tgt, _rspec(tm, D)), (nf, _fspec(nf.shape))],
        [(D, F32), (D, BF16), (D, BF16)], [(1, LANES), (1, D)])
    loss = lax.psum(loss_acc[0, 0], ("x", "y", "c"))

    g_wpp3 = _mm(pb, dpp, "tn", "d_ple_proj", out_dtype=BF16, out_blk=n_pp)
    g_wpg = _mm(n2, dsg, "tn", "d_ple_gate", out_dtype=BF16)
    dn2 = _mm(dsg, wpg, "nt", "d_norm_ple_in", out_dtype=BF16)

    def ple_bwd_body(h1_ref, dn_ref, dh2_ref, g_ref, dh1_ref, dh1b_ref, dg_ref):
        h1v = h1_ref[...]
        r = _rms(h1v)
        hh = h1v * r
        dn = _f32(dn_ref)
        _acc(dg_ref, _colsum(dn * hh))
        dh1 = dh2_ref[...] + _rms_bwd(dn, hh, r, g_ref[...])
        dh1_ref[...] = dh1
        dh1b_ref[...] = dh1.astype(BF16)

    dh1, dh1b, d_nple = _rowcall(
        ple_bwd_body, "ple_bwd", L, tm,
        [(h1, _rspec(tm, D)), (dn2, _rspec(tm, D)), (dh2, _rspec(tm, D)), (norm_ple, _fspec(norm_ple.shape))],
        [(D, F32), (D, BF16)], [(1, D)])

    dcat = _mm(dh1b, wout, "nt", "d_cat", out_dtype=BF16)
    g_wout = _mm(cat, dh1b, "tn", "d_out_proj", out_dtype=BF16)

    def mix_bwd_body(dca_ref, dcs_ref, o_ref, ga_ref, gla_ref, glb_ref, gs_ref,
                     do_ref, dga_ref, dgs_ref, dglu_ref, db_ref):
        dca, dcs, ga, gs = _f32(dca_ref), _f32(dcs_ref), _f32(ga_ref), _f32(gs_ref)
        sa, ss, sb = _sigmoid(ga), _sigmoid(gs), _sigmoid(_f32(glb_ref))
        gla = _f32(gla_ref)
        do_ref[...] = (dca * ga * sa).astype(BF16)
        dga_ref[...] = (dca * o_ref[...] * sa * (1.0 + ga * (1.0 - sa))).astype(BF16)
        dgs_ref[...] = (dcs * gla * sb * ss * (1.0 + gs * (1.0 - ss))).astype(BF16)
        dy2 = dcs * gs * ss
        da, db = dy2 * sb, dy2 * gla * sb * (1.0 - sb)
        dglu_ref[:, :D_SSM] = da.astype(BF16)
        dglu_ref[:, D_SSM:] = db.astype(BF16)
        _acc(db_ref, jnp.concatenate([_colsum(da), _colsum(db)], axis=-1))

    do, dga, dgs, dglu, g_bglu = _rowcall(
        mix_bwd_body, "mix_bwd", L, tm,
        [(dcat, _rspec(tm, D_ATTN, 0)), (dcat, _rspec(tm, D_SSM, 1)), (o, _rspec(tm, D_ATTN)),
         (z, _rspec(tm, D_ATTN, 1)), (glu, _rspec(tm, D_SSM, 0)), (glu, _rspec(tm, D_SSM, 1)),
         (z, _rspec(tm, D_SSM, 3))],
        [(D_ATTN, BF16), (D_ATTN, BF16), (D_SSM, BF16), (2 * D_SSM, BF16)], [(1, 2 * D_SSM)])

    g_wglu3 = _mm(gy, dglu, "tn", "d_glu_proj", out_dtype=BF16, out_blk=n_glu)
    dgy = _mm(dglu, wglu3, "nt", "d_gelu_out", out_dtype=BF16, b_blk=True)

    def gelu_bwd_body(dg_ref, y_ref, o_ref):
        o_ref[...] = _f32(dg_ref) * _gelu_grad(y_ref[...])

    (dys,) = _rowcall(gelu_bwd_body, "gelu_bwd", L, tm,
                      [(dgy, _rspec(tm, D_SSM)), (ys, _rspec(tm, D_SSM))], [(D_SSM, F32)])
    du_perm, pg = _ssm_bwd(u_perm, _seg_perm(dys), ssm_prm, ssm_d)
    du = _seg_unperm(du_perm)

    pg_send = pg.reshape(N_DEV, (n_slab // N_DEV) * 2 * PG_ROWS, SLAB_S)
    dqs, dkr, dvv, (l_wglu, l_wout, l_wpg, l_wpp, l_pg) = _attn_bwd(
        qr, kr, vb, kt, do, o, lse,
        _Carry("a2a", [g_wglu3, g_wout.reshape(N_DEV, -1, D), g_wpg.reshape(N_DEV, -1, D), g_wpp3, pg_send]))

    scale = HEAD_DIM ** -0.5
    kblk = 4 * D_ATTN // D_KV
    tmq = _pick(L, (512, 256))

    def qkv_bwd_body(dq_ref, dk_ref, dv_ref, q_ref, k_ref, cos_ref, sin_ref, qn_ref, kn_ref,
                     dqo_ref, dko_ref, dvo_ref, dqn_ref, dkn_ref):
        c, s = cos_ref[...], sin_ref[...]

        def head(g, xh, w):
            dn = g * c + _partner(g * s)
            r = _rms(xh)
            xhat = xh * r
            return _rms_bwd(dn, xhat, r, w), _colsum(dn * xhat)

        dqn = jnp.zeros((1, HEAD_DIM), F32)
        for h in range(N_HEADS):
            sl = slice(h * HEAD_DIM, (h + 1) * HEAD_DIM)
            dx, dw = head(dq_ref[:, sl] * scale, q_ref[:, sl].astype(F32), qn_ref[...])
            dqo_ref[:, sl] = dx.astype(BF16)
            dqn = dqn + dw
        dkn = jnp.zeros((1, HEAD_DIM), F32)
        for h in range(N_KV):
            sl = slice(h * HEAD_DIM, (h + 1) * HEAD_DIM)
            dx, dw = head(dk_ref[:, sl], k_ref[:, sl].astype(F32), kn_ref[...])
            dko_ref[:, sl] = dx.astype(BF16)
            dkn = dkn + dw
        dvo_ref[...] = dv_ref[...].astype(BF16)
        _acc(dqn_ref, dqn)
        _acc(dkn_ref, dkn)

    dq, dk, dv, g_qn, g_kn = _rowcall(
        qkv_bwd_body, "qkv_bwd", L, tmq,
        [(dqs, _rspec(tmq, D_ATTN)), (dkr, _rspec(tmq, D_KV)), (dvv, _rspec(tmq, D_KV)),
         (z, _rspec(tmq, D_ATTN, 0)), (z, _rspec(tmq, D_KV, kblk)), (cos, _rspec(tmq, HEAD_DIM)),
         (sin, _rspec(tmq, HEAD_DIM)), (q_norm, _fspec(q_norm.shape)), (k_norm, _fspec(k_norm.shape))],
        [(D_ATTN, BF16), (D_KV, BF16), (D_KV, BF16)], [(1, HEAD_DIM), (1, HEAD_DIM)])

    dz = jnp.concatenate([dq, dk, dv, dga, du.astype(BF16), dgs], axis=1)
    g_win_t = _mm(dz, hn, "tn", "d_in_proj", out_dtype=BF16)
    g_win8 = g_win_t.reshape(N_DEV, n_in, D)
    from_sibling = _sibling_swap(g_win8, "swap_d_w_in")
    own = lax.dynamic_index_in_dim(g_win8.reshape(N_CHIPS, 2, n_in, D), lax.axis_index("c"), axis=1, keepdims=False)
    pair = _pair_sum(own, from_sibling, "pair_sum_d_w_in")
    dhn, (l_win_t,) = _mm(dz, win_t, "nn", "d_norm_mix_in", out_dtype=BF16,
                          carry=_Carry("a2a_chips", [pair]))

    def in_bwd_body(x_ref, dn_ref, dh1_ref, g_ref, dx_ref, dg_ref):
        xv = x_ref[...]
        r = _rms(xv)
        hh = xv * r
        dn = _f32(dn_ref)
        _acc(dg_ref, _colsum(dn * hh))
        dx_ref[...] = dh1_ref[...] + _rms_bwd(dn, hh, r, g_ref[...])

    grad_x, g_nmix = _rowcall(
        in_bwd_body, "in_bwd", L, tm,
        [(xs, _rspec(tm, D)), (dhn, _rspec(tm, D)), (dh1, _rspec(tm, D)), (norm_mix, _fspec(norm_mix.shape))],
        [(D, F32)], [(1, D)])

    tiny_parts = [g_nmix, g_bglu, d_nple, d_nf, g_qn, g_kn]
    tiny_flat = jnp.concatenate([t.reshape(-1) for t in tiny_parts])
    tiny_rows = -(-tiny_flat.shape[0] // (8 * LANES)) * 8
    tiny = jnp.pad(tiny_flat, (0, tiny_rows * LANES - tiny_flat.shape[0])).reshape(tiny_rows, LANES)
    pg_sum = _sum_blocks(l_pg, "sum_ssm_grads")
    pg_all, tiny_all = _all_gather([pg_sum, tiny], "gather_small_grads")
    (g_bt_re, g_bt_im, g_c_re, g_c_im, g_a_re, g_a_im, g_ldt, g_skip) = _ssm_param_grads(
        pg_all.reshape(n_slab, 2, PG_ROWS, SLAB_S), ssm_prm)
    tiny_sum = _sum_blocks(tiny_all, "sum_tiny_grads").reshape(-1)
    tiny_grads, off = [], 0
    for t in tiny_parts:
        tiny_grads.append(tiny_sum[off:off + t.size].reshape(t.shape))
        off += t.size
    r_nmix, r_bglu, r_nple, r_nf, r_qn, r_kn = tiny_grads

    grads, deltas, new_ms, new_vs = {}, {}, {}, {}
    outs = _adamw_reduce(w_in[0].T, l_win_t, m_w_in[0].T, v_w_in[0].T, "adamw_w_in")
    grads["w_in"], deltas["w_in"], new_ms["w_in"], new_vs["w_in"] = [t.T[None] for t in outs]
    big = [("w_glu", w_glu, l_wglu, m_w_glu, v_w_glu),
           ("w_out", w_out, l_wout, m_w_out, v_w_out), ("w_ple_gate", w_ple_gate, l_wpg, m_w_ple_gate, v_w_ple_gate),
           ("w_ple_proj", w_ple_proj, l_wpp, m_w_ple_proj, v_w_ple_proj)]
    for name, w, ld, m, v in big:
        shp = w.shape
        outs = _adamw_reduce(w[0], ld, m[0], v[0], "adamw_" + name)
        grads[name], deltas[name], new_ms[name], new_vs[name] = [t.reshape(shp) for t in outs]
    bt2 = (2 * G * SSM_H, SSM_P)
    for name, w, g, m, v in (("ssm_b_re", ssm_b_re, g_bt_re, m_ssm_b_re, v_ssm_b_re),
                             ("ssm_b_im", ssm_b_im, g_bt_im, m_ssm_b_im, v_ssm_b_im)):
        to2 = lambda t: t[0].transpose(0, 1, 3, 2).reshape(bt2)
        back = lambda t: t.reshape(2, G, SSM_H, SSM_P).transpose(0, 1, 3, 2)[None]
        outs = _adamw(to2(w), g.reshape(bt2), to2(m), to2(v), "adamw_" + name)
        grads[name] = back(g)
        deltas[name], new_ms[name], new_vs[name] = [back(t) for t in outs]
    small = [("norm_mix", norm_mix, r_nmix, m_norm_mix, v_norm_mix, (1, D)),
             ("q_norm", q_norm, r_qn, m_q_norm, v_q_norm, (1, HEAD_DIM)),
             ("k_norm", k_norm, r_kn, m_k_norm, v_k_norm, (1, HEAD_DIM)),
             ("ssm_a_re", ssm_a_re, g_a_re, m_ssm_a_re, v_ssm_a_re, (2 * G, SSM_P)),
             ("ssm_a_im", ssm_a_im, g_a_im, m_ssm_a_im, v_ssm_a_im, (2 * G, SSM_P)),
             ("ssm_log_dt", ssm_log_dt, g_ldt, m_ssm_log_dt, v_ssm_log_dt, (2, G)),
             ("ssm_c_re", ssm_c_re, g_c_re, m_ssm_c_re, v_ssm_c_re, (2 * G * SSM_H, SSM_P)),
             ("ssm_c_im", ssm_c_im, g_c_im, m_ssm_c_im, v_ssm_c_im, (2 * G * SSM_H, SSM_P)),
             ("ssm_d", ssm_d, g_skip, m_ssm_d, v_ssm_d, (1, D_SSM)),
             ("b_glu", b_glu, r_bglu, m_b_glu, v_b_glu, (1, 2 * D_SSM)),
             ("norm_ple", norm_ple, r_nple, m_norm_ple, v_norm_ple, (1, D)),
             ("norm_final", norm_final, r_nf, m_norm_final, v_norm_final, (1, D))]
    for name, w, g, m, v, s2 in small:
        shp = w.shape
        outs = _adamw(w.reshape(s2), g.reshape(s2), m.reshape(s2), v.reshape(s2), "adamw_" + name)
        grads[name] = g.reshape(shp)
        deltas[name], new_ms[name], new_vs[name] = [t.reshape(shp) for t in outs]

    order = ["norm_mix", "w_in", "q_norm", "k_norm", "ssm_a_re", "ssm_a_im", "ssm_log_dt", "ssm_b_re", "ssm_b_im",
             "ssm_c_re", "ssm_c_im", "ssm_d", "w_glu", "b_glu", "w_out", "norm_ple", "w_ple_gate", "w_ple_proj",
             "norm_final"]
    return (loss, grad_x[None], *[grads[k] for k in order], *[deltas[k] for k in order],
            *[new_ms[k] for k in order], *[new_vs[k] for k in order])
```

```python
import functools
import math

import numpy as np
import jax
import jax.numpy as jnp
from jax import lax
from jax.experimental import pallas as pl
from jax.experimental.pallas import tpu as pltpu

F32 = jnp.float32
BF16 = jnp.bfloat16

N_DEV = 8
N_CHIPS = 4
EPS = 1e-6
GRID_W = 64
ROPE_THETA = 10000.0
HEAD_DIM = 128
N_HEADS = 8
N_KV = 2
REP = N_HEADS // N_KV
D_ATTN = N_HEADS * HEAD_DIM
D_KV = N_KV * HEAD_DIM
SSM_H = 16
SSM_P = 64
SLAB = 128
SLAB_G = SLAB // SSM_H
SLAB_S = SLAB_G * SSM_P
SEG = 8
CHAINS = 2
SCAN_UNROLL = 4
LANES = 128
PG_ROWS = 72
VMEM_LIMIT = 48 << 20

ADAM_LR = 0.001
ADAM_B1 = 0.9
ADAM_B2 = 0.999
ADAM_EPS = 1e-08
ADAM_WD = 0.01
ADAM_STEP = 10


def _pick(n, cands):
    for c in cands:
        if n % c == 0:
            return c
    return n


def _cparams(sem, vmem=VMEM_LIMIT):
    return pltpu.CompilerParams(dimension_semantics=sem, vmem_limit_bytes=vmem)


class _Carry:
    def __init__(self, kind, xs):
        self.kind, self.xs, self.n = kind, list(xs), len(xs)
        self.ks = (2, 4, 6) if kind == "a2a_chips" else tuple(range(1, N_DEV))
        self.index = _chip_index if kind == "a2a_chips" else _dev_index
        lead = (N_DEV,) if kind == "gather" else ()
        self.out_shape = [jax.ShapeDtypeStruct(lead + v.shape, v.dtype) for v in xs]
        self.specs = [pl.BlockSpec(memory_space=pl.ANY)] * self.n
        self.scratch = [pltpu.SemaphoreType.DMA((self.n, len(self.ks))), pltpu.SemaphoreType.DMA((self.n, len(self.ks))),
                        pltpu.SemaphoreType.DMA((self.n,))]

    def _copies(self, x_refs, out_refs, sems):
        send_sems, recv_sems, local_sems = sems
        x, y, c = lax.axis_index("x"), lax.axis_index("y"), lax.axis_index("c")
        me = self.index((x, y, c))
        mine, sends, arrivals = [], [], []
        for a in range(self.n):
            src_mine = x_refs[a] if self.kind == "gather" else x_refs[a].at[me]
            mine.append(pltpu.make_async_copy(src_mine, out_refs[a].at[me], local_sems.at[a]))
            for s, k in enumerate(self.ks):
                peer = _peer(k, x, y, c)
                src = x_refs[a] if self.kind == "gather" else x_refs[a].at[self.index(peer)]
                sends.append(pltpu.make_async_remote_copy(
                    src_ref=src, dst_ref=out_refs[a].at[me], send_sem=send_sems.at[a, s],
                    recv_sem=recv_sems.at[a, s], device_id=peer, device_id_type=pl.DeviceIdType.MESH))
                land = out_refs[a].at[self.index(peer)]
                arrivals.append(pltpu.make_async_remote_copy(
                    src_ref=land, dst_ref=land, send_sem=send_sems.at[a, s],
                    recv_sem=recv_sems.at[a, s], device_id=peer, device_id_type=pl.DeviceIdType.MESH))
        return mine, sends, arrivals

    def start(self, x_refs, out_refs, sems):
        mine, sends, _ = self._copies(x_refs, out_refs, sems)
        for cp in mine + sends:
            cp.start()

    def wait(self, x_refs, out_refs, sems):
        mine, sends, arrivals = self._copies(x_refs, out_refs, sems)
        for cp in arrivals:
            cp.wait_recv()
        for cp in sends:
            cp.wait_send()
        for cp in mine:
            cp.wait()


def _grid_edges(grid):
    first = functools.reduce(lambda p, q: p & q, [pl.program_id(d) == 0 for d in range(len(grid))])
    last = functools.reduce(lambda p, q: p & q, [pl.program_id(d) == g - 1 for d, g in enumerate(grid)])
    return first, last


def _mm(a, b, mode, name, out_dtype=F32, add=None, bias=None, a_blk=False, b_blk=False, out_blk=0, carry=None,
        n_tiles=None):
    w = b.shape[2] if b_blk else out_blk
    if mode == "nn":
        M, K = a.shape
        N = b.shape[0] * w if b_blk else b.shape[1]
    elif mode == "nt":
        M = a.shape[1] if a_blk else a.shape[0]
        N = b.shape[1] if b_blk else b.shape[0]
        K = b.shape[0] * w if b_blk else b.shape[1]
    else:
        K, M = a.shape
        N = b.shape[0] * w if b_blk else b.shape[1]
    tm = _pick(M, (1024, 768, 512, 256))
    tn = _pick(N, (1024, 768, 512, 256))
    tk = K if (mode != "tn" and K <= 2048) else _pick(K, (1024, 768, 512, 256))
    perm = lambda j: j
    if n_tiles:
        tn, perm = n_tiles
    if mode == "nt" and b_blk:
        tk = w
    elif b_blk or out_blk:
        tn = w
    nk = K // tk
    grid = (M // tm, N // tn, nk)
    if mode == "nn":
        a_spec = pl.BlockSpec((tm, tk), lambda i, j, k: (i, k))
        b_spec = (pl.BlockSpec((1, tk, tn), lambda i, j, k: (j, k, 0)) if b_blk
                  else pl.BlockSpec((tk, tn), lambda i, j, k: (k, j)))
        dims = (((1,), (0,)), ((), ()))
    elif mode == "nt":
        a_spec = (pl.BlockSpec((1, tm, tk), lambda i, j, k: (k, i, 0)) if a_blk
                  else pl.BlockSpec((tm, tk), lambda i, j, k: (i, k)))
        b_spec = (pl.BlockSpec((1, tn, tk), lambda i, j, k: (k, j, 0)) if b_blk
                  else pl.BlockSpec((tn, tk), lambda i, j, k: (perm(j), k)))
        dims = (((1,), (1,)), ((), ()))
    else:
        a_spec = pl.BlockSpec((tk, tm), lambda i, j, k: (k, i))
        b_spec = (pl.BlockSpec((1, tk, tn), lambda i, j, k: (j, k, 0)) if b_blk
                  else pl.BlockSpec((tk, tn), lambda i, j, k: (k, j)))
        dims = (((0,), (0,)), ((), ()))
    if out_blk:
        out_spec = pl.BlockSpec((1, tm, tn), lambda i, j, k: (j, i, 0))
        out_shape = jax.ShapeDtypeStruct((N // tn, M, tn), out_dtype)
    else:
        out_spec = pl.BlockSpec((tm, tn), lambda i, j, k: (i, j))
        out_shape = jax.ShapeDtypeStruct((M, N), out_dtype)
    extras, extra_specs = [], []
    if add is not None:
        extras.append(add)
        extra_specs.append(pl.BlockSpec((tm, tn), lambda i, j, k: (i, j)))
    if bias is not None:
        extras.append(bias)
        extra_specs.append(pl.BlockSpec((1, tn), lambda i, j, k: (0, j)))

    n_ex = len(extras)
    nc = carry.n if carry else 0

    def body(a_ref, b_ref, *rest):
        ex_refs, cx = rest[:n_ex], rest[n_ex:n_ex + nc]
        o_ref, cout = rest[n_ex + nc], rest[n_ex + nc + 1:n_ex + 2 * nc + 1]
        tail = rest[n_ex + 2 * nc + 1:]
        sems = tail[:3] if carry else ()
        first, last = _grid_edges(grid)
        if carry:
            @pl.when(first)
            def _():
                carry.start(cx, cout, sems)

        def product():
            av = a_ref[0] if a_blk else a_ref[...]
            bv = b_ref[0] if b_blk else b_ref[...]
            return lax.dot_general(av, bv, dims, preferred_element_type=F32)

        def finish(out):
            for r in ex_refs:
                out = out + r[...]
            if out_blk:
                o_ref[0] = out.astype(out_dtype)
            else:
                o_ref[...] = out.astype(out_dtype)

        if nk == 1:
            finish(product())
        else:
            acc_ref = tail[-1]
            k = pl.program_id(2)

            @pl.when(k == 0)
            def _():
                acc_ref[...] = jnp.zeros_like(acc_ref)

            acc_ref[...] += product()

            @pl.when(k == nk - 1)
            def _():
                finish(acc_ref[...])

        if carry:
            @pl.when(last)
            def _():
                carry.wait(cx, cout, sems)

    scratch = (carry.scratch if carry else []) + ([pltpu.VMEM((tm, tn), F32)] if nk > 1 else [])
    outs = pl.pallas_call(
        body, name=name, grid=grid,
        in_specs=[a_spec, b_spec] + extra_specs + (carry.specs if carry else []),
        out_specs=[out_spec] + (carry.specs if carry else []),
        out_shape=[out_shape] + (carry.out_shape if carry else []),
        scratch_shapes=scratch,
        compiler_params=_cparams(("arbitrary", "arbitrary", "arbitrary")),
    )(a, b, *extras, *(carry.xs if carry else []))
    return (outs[0], outs[1:]) if carry else outs[0]


def _rspec(tm, w, cb=0):
    return pl.BlockSpec((tm, w), lambda i: (i, cb))


def _fspec(shape):
    nd = len(shape)
    return pl.BlockSpec(shape, lambda i: (0,) * nd)


def _rowcall(body, name, L, tm, ins, row_outs, acc_outs=()):
    out_shape = [jax.ShapeDtypeStruct((L, w), dt) for w, dt in row_outs]
    out_shape += [jax.ShapeDtypeStruct(s, F32) for s in acc_outs]
    out_specs = [_rspec(tm, w) for w, _ in row_outs] + [_fspec(s) for s in acc_outs]
    return pl.pallas_call(
        body, name=name, grid=(L // tm,),
        in_specs=[s for _, s in ins], out_specs=out_specs, out_shape=out_shape,
        compiler_params=_cparams(("arbitrary",)),
    )(*[a for a, _ in ins])


def _acc(ref, val):
    @pl.when(pl.program_id(0) == 0)
    def _():
        ref[...] = jnp.zeros_like(ref)
    ref[...] += val


def _colsum(v):
    return jnp.sum(v, axis=0, keepdims=True)


def _rms(xv):
    return lax.rsqrt(jnp.mean(xv * xv, axis=-1, keepdims=True) + EPS)


def _rms_bwd(dn, xhat, r, g):
    dng = dn * g
    return r * (dng - xhat * jnp.mean(dng * xhat, axis=-1, keepdims=True))


def _sigmoid(v):
    return jax.nn.sigmoid(v)


def _f32(ref):
    return ref[...].astype(F32)


def _partner(v):
    w = v.shape[-1]
    lane = lax.broadcasted_iota(jnp.int32, v.shape, v.ndim - 1)
    first_half = (lane % 64) < 32
    return jnp.where(first_half, pltpu.roll(v, w - 32, axis=v.ndim - 1), pltpu.roll(v, 32, axis=v.ndim - 1))


def _norm_in(x, g, name):
    L, D = x.shape
    tm = _pick(L, (512, 256))

    def body(x_ref, g_ref, o_ref):
        xv = x_ref[...]
        o_ref[...] = (xv * _rms(xv) * g_ref[...]).astype(BF16)

    return _rowcall(body, name, L, tm, [(x, _rspec(tm, D)), (g, _fspec(g.shape))], [(D, BF16)])[0]


def _rope_tables(L):
    t = np.arange(L)
    rows = (t // GRID_W).astype(np.float32)
    cols = (t % GRID_W).astype(np.float32)
    n_freq = HEAD_DIM // 4
    inv_freq = np.float32(ROPE_THETA) ** (-np.arange(n_freq, dtype=np.float32) / np.float32(n_freq))
    ar = (rows[:, None] * inv_freq[None, :]).astype(np.float32).astype(np.float64)
    ac = (cols[:, None] * inv_freq[None, :]).astype(np.float32).astype(np.float64)
    cos = np.concatenate([np.cos(ar), np.cos(ar), np.cos(ac), np.cos(ac)], axis=-1).astype(np.float32)
    sin = np.concatenate([-np.sin(ar), np.sin(ar), -np.sin(ac), np.sin(ac)], axis=-1).astype(np.float32)
    return jnp.asarray(cos), jnp.asarray(sin)


def _qkv_prep(z, cos, sin, qn, kn):
    L = z.shape[0]
    tm = _pick(L, (512, 256))
    scale = HEAD_DIM ** -0.5
    kblk = 4 * D_ATTN // D_KV

    def body(q_ref, k_ref, v_ref, cos_ref, sin_ref, qn_ref, kn_ref, qo_ref, ko_ref, vo_ref, kt_ref):
        c, s = cos_ref[...], sin_ref[...]

        def head(xh, w):
            n = xh * _rms(xh) * w
            return n * c + _partner(n) * s

        for h in range(N_HEADS):
            sl = slice(h * HEAD_DIM, (h + 1) * HEAD_DIM)
            qo_ref[:, sl] = (head(q_ref[:, sl].astype(F32), qn_ref[...]) * scale).astype(BF16)
        for h in range(N_KV):
            sl = slice(h * HEAD_DIM, (h + 1) * HEAD_DIM)
            kr = head(k_ref[:, sl].astype(F32), kn_ref[...])
            ko_ref[:, sl] = kr.astype(BF16)
            kt_ref[sl, :] = kr.T.astype(BF16)
        vo_ref[...] = v_ref[...].astype(BF16)

    return pl.pallas_call(
        body, name="qkv_prep", grid=(L // tm,),
        in_specs=[_rspec(tm, D_ATTN, 0), _rspec(tm, D_KV, kblk), _rspec(tm, D_KV, kblk + 1),
                  _rspec(tm, HEAD_DIM), _rspec(tm, HEAD_DIM), _fspec(qn.shape), _fspec(kn.shape)],
        out_specs=[_rspec(tm, D_ATTN), _rspec(tm, D_KV), _rspec(tm, D_KV),
                   pl.BlockSpec((D_KV, tm), lambda i: (0, i))],
        out_shape=[jax.ShapeDtypeStruct((L, D_ATTN), BF16), jax.ShapeDtypeStruct((L, D_KV), BF16),
                   jax.ShapeDtypeStruct((L, D_KV), BF16), jax.ShapeDtypeStruct((D_KV, L), BF16)],
        compiler_params=_cparams(("arbitrary",)),
    )(z, z, z, cos, sin, qn, kn)


def _col_to_row(col):
    n = col.shape[0]
    eye = lax.broadcasted_iota(jnp.int32, (n, n), 0) == lax.broadcasted_iota(jnp.int32, (n, n), 1)
    return jnp.sum(jnp.where(eye, col, 0.0), axis=0, keepdims=True)


def _attn_fwd(q, k, v, carry=None):
    L = q.shape[0]
    tq = _pick(L, (256, 128))
    grid = (N_HEADS, L // tq)
    nc = carry.n if carry else 0

    def body(q_ref, k_ref, v_ref, *rest):
        cx, (o_ref, lse_ref) = rest[:nc], rest[nc:nc + 2]
        cout, sems = rest[nc + 2:2 * nc + 2], rest[2 * nc + 2:]
        first, last = _grid_edges(grid)
        if carry:
            @pl.when(first)
            def _():
                carry.start(cx, cout, sems)

        s = lax.dot_general(q_ref[...], k_ref[...], (((1,), (1,)), ((), ())), preferred_element_type=F32)
        m = jnp.max(s, axis=-1, keepdims=True)
        e = jnp.exp(s - m)
        l = jnp.sum(e, axis=-1, keepdims=True)
        o_ref[...] = jnp.dot(e.astype(BF16), v_ref[...], preferred_element_type=F32) / l
        lse_ref[0] = _col_to_row(m + jnp.log(l))

        if carry:
            @pl.when(last)
            def _():
                carry.wait(cx, cout, sems)

    outs = pl.pallas_call(
        body, name="attn_fwd", grid=grid,
        in_specs=[pl.BlockSpec((tq, HEAD_DIM), lambda h, i: (i, h)),
                  pl.BlockSpec((L, HEAD_DIM), lambda h, i: (0, h // REP)),
                  pl.BlockSpec((L, HEAD_DIM), lambda h, i: (0, h // REP))] + (carry.specs if carry else []),
        out_specs=[pl.BlockSpec((tq, HEAD_DIM), lambda h, i: (i, h)),
                   pl.BlockSpec((1, 1, tq), lambda h, i: (h, 0, i))] + (carry.specs if carry else []),
        out_shape=[jax.ShapeDtypeStruct((L, D_ATTN), F32), jax.ShapeDtypeStruct((N_HEADS, 1, L), F32)]
        + (carry.out_shape if carry else []),
        scratch_shapes=carry.scratch if carry else [],
        compiler_params=_cparams(("arbitrary", "arbitrary")),
    )(q, k, v, *(carry.xs if carry else []))
    return outs[0], outs[1], outs[2:]


def _attn_bwd(q, k, v, kt, do, o, lse, carry=None):
    L = q.shape[0]
    tq = _pick(L, (256, 128))
    kc = _pick(L, (512, 256, 128))
    nt = (((1,), (1,)), ((), ()))
    grid = (N_KV, REP, L // tq)
    nc = carry.n if carry else 0

    def body(q_ref, do_ref, o_ref, lse_ref, k_ref, v_ref, kt_ref, *rest):
        cx, (dq_ref, dk_ref, dv_ref) = rest[:nc], rest[nc:nc + 3]
        cout, sems = rest[nc + 3:2 * nc + 3], rest[2 * nc + 3:]
        first, last = _grid_edges(grid)
        if carry:
            @pl.when(first)
            def _():
                carry.start(cx, cout, sems)

        @pl.when((pl.program_id(1) == 0) & (pl.program_id(2) == 0))
        def _():
            dk_ref[...] = jnp.zeros_like(dk_ref)
            dv_ref[...] = jnp.zeros_like(dv_ref)

        qv, dov = q_ref[...], do_ref[...]
        lse_row = lse_ref[0]
        delta = _col_to_row(jnp.sum(dov.astype(F32) * o_ref[...], axis=-1, keepdims=True))
        dqt = jnp.zeros((HEAD_DIM, tq), F32)
        for c in range(L // kc):
            sl = slice(c * kc, (c + 1) * kc)
            st = lax.dot_general(k_ref[sl, :], qv, nt, preferred_element_type=F32)
            pt = jnp.exp(st - lse_row)
            dpt = lax.dot_general(v_ref[sl, :], dov, nt, preferred_element_type=F32)
            dst = (pt * (dpt - delta)).astype(BF16)
            dv_ref[sl, :] += jnp.dot(pt.astype(BF16), dov, preferred_element_type=F32)
            dk_ref[sl, :] += jnp.dot(dst, qv, preferred_element_type=F32)
            dqt = dqt + jnp.dot(kt_ref[:, sl], dst, preferred_element_type=F32)
        dq_ref[...] = dqt.T

        if carry:
            @pl.when(last)
            def _():
                carry.wait(cx, cout, sems)

    head = lambda g, r, i: (i, g * REP + r)
    outs = pl.pallas_call(
        body, name="attn_bwd", grid=grid,
        in_specs=[pl.BlockSpec((tq, HEAD_DIM), head), pl.BlockSpec((tq, HEAD_DIM), head),
                  pl.BlockSpec((tq, HEAD_DIM), head),
                  pl.BlockSpec((1, 1, tq), lambda g, r, i: (g * REP + r, 0, i)),
                  pl.BlockSpec((L, HEAD_DIM), lambda g, r, i: (0, g)),
                  pl.BlockSpec((L, HEAD_DIM), lambda g, r, i: (0, g)),
                  pl.BlockSpec((HEAD_DIM, L), lambda g, r, i: (g, 0))] + (carry.specs if carry else []),
        out_specs=[pl.BlockSpec((tq, HEAD_DIM), head),
                   pl.BlockSpec((L, HEAD_DIM), lambda g, r, i: (0, g)),
                   pl.BlockSpec((L, HEAD_DIM), lambda g, r, i: (0, g))] + (carry.specs if carry else []),
        out_shape=[jax.ShapeDtypeStruct((L, D_ATTN), F32), jax.ShapeDtypeStruct((L, D_KV), F32),
                   jax.ShapeDtypeStruct((L, D_KV), F32)] + (carry.out_shape if carry else []),
        scratch_shapes=carry.scratch if carry else [],
        compiler_params=_cparams(("arbitrary", "arbitrary", "arbitrary")),
    )(q, do, o, lse, k, v, kt, *(carry.xs if carry else []))
    return outs[0], outs[1], outs[2], outs[3:]


def _seg_perm(a):
    L, C = a.shape
    return a.reshape(SEG, L // SEG, C).transpose(1, 0, 2).reshape(L, C)


def _seg_unperm(a):
    L, C = a.shape
    return a.reshape(L // SEG, SEG, C).transpose(1, 0, 2).reshape(L, C)


def _cmul(ar, ai, br, bi):
    return ar * br - ai * bi, ar * bi + ai * br


def _rows8(rr):
    if isinstance(rr, int):
        return pl.ds(rr * SEG, SEG)
    return pl.ds(pl.multiple_of(rr * SEG, SEG), SEG)


def _seg_scan(xr_ref, xi_ref, ar, ai, reverse, n_rows, visit=None, visit_init=(), entering=None):
    shape = ar.shape
    zero = jnp.zeros(shape, F32)
    rc = n_rows // CHAINS

    def index(q):
        return (n_rows - 1 - q) if reverse else q

    if entering is None:
        def ends(q, carry):
            out = []
            for j in range(CHAINS):
                sl = _rows8(index(j * rc + q))
                pr, pi = _cmul(ar, ai, carry[2 * j], carry[2 * j + 1])
                out += [pr + xr_ref[sl, :], pi + xi_ref[sl, :]]
            return tuple(out)

        def ends_block(qb, carry):
            for t in range(SCAN_UNROLL):
                carry = ends(qb * SCAN_UNROLL + t, carry)
            return carry

        e = lax.fori_loop(0, rc // SCAN_UNROLL, ends_block, (zero,) * (2 * CHAINS))

        pr, pi = ar, ai
        for _ in range(int(math.log2(rc))):
            pr, pi = _cmul(pr, pi, pr, pi)
        sub = lax.broadcasted_iota(jnp.int32, shape, 0)
        shift = (SEG - 1) if reverse else 1
        edge = (SEG - 1) if reverse else 0
        entering = [(zero, zero)] * CHAINS
        for _ in range(SEG):
            tr, ti = _cmul(pr, pi, *entering[CHAINS - 1])
            cur = (jnp.where(sub == edge, 0.0, pltpu.roll(tr + e[2 * CHAINS - 2], shift, axis=0)),
                   jnp.where(sub == edge, 0.0, pltpu.roll(ti + e[2 * CHAINS - 1], shift, axis=0)))
            entering = [cur]
            for j in range(1, CHAINS):
                tr, ti = _cmul(pr, pi, *cur)
                cur = (tr + e[2 * j - 2], ti + e[2 * j - 1])
                entering.append(cur)

    def step(q, carry, last):
        out, acc = [], carry[2 * CHAINS:]
        for j in range(CHAINS):
            rr = index(j * rc + q)
            sl = _rows8(rr)
            pr, pi = _cmul(ar, ai, carry[2 * j], carry[2 * j + 1])
            nr, ni = pr + xr_ref[sl, :], pi + xi_ref[sl, :]
            xr_ref[sl, :] = nr
            xi_ref[sl, :] = ni
            if visit:
                acc = visit(rr, nr, ni, acc, last and j == CHAINS - 1)
            out += [nr, ni]
        return (*out, *acc)

    def step_block(qb, carry):
        for t in range(SCAN_UNROLL):
            carry = step(qb * SCAN_UNROLL + t, carry, False)
        return carry

    start = tuple(v for pair in entering for v in pair)
    n_blocks = (rc - 1) // SCAN_UNROLL
    carry = lax.fori_loop(0, n_blocks, step_block, (*start, *visit_init))
    for q in range(n_blocks * SCAN_UNROLL, rc - 1):
        carry = step(q, carry, False)
    carry = step(rc - 1, carry, True)
    return entering, carry[2 * CHAINS:]


def _discretise(a_re, a_im, ldt):
    lr = jnp.minimum(a_re, -1e-4)
    li = a_im
    dt = jnp.exp(ldt)
    mag = jnp.exp(lr * dt)
    lbr = mag * jnp.cos(li * dt)
    lbi = mag * jnp.sin(li * dt)
    den = lr * lr + li * li
    nr = lbr - 1.0
    fr = (nr * lr + lbi * li) / den
    fi = (lbi * lr - nr * li) / den
    return lr, li, dt, lbr, lbi, fr, fi


def _lane_row(v):
    return jnp.concatenate([v[g:g + 1, :] for g in range(v.shape[0])], axis=1)


def _ssm_fill_maps(d, prm, tmp_ref, maps):
    a_re_ref, a_im_ref, ldt_ref, bt_re_ref, bt_im_ref, c_re_ref, c_im_ref = prm
    _, _, _, lbr, lbi, fr, fi = _discretise(a_re_ref[d], a_im_ref[d], ldt_ref[d])

    def fill(dst, piece):
        tmp_ref[...] = jnp.zeros_like(tmp_ref)
        for g in range(SLAB_G):
            tmp_ref[g * SSM_H:(g + 1) * SSM_H, g * SSM_P:(g + 1) * SSM_P] = piece(g)
        dst[...] = tmp_ref[...].astype(BF16)

    wbr, wbi, wcr, wci = maps
    fill(wbr, lambda g: fr[g:g + 1] * bt_re_ref[d, g] - fi[g:g + 1] * bt_im_ref[d, g])
    fill(wbi, lambda g: fr[g:g + 1] * bt_im_ref[d, g] + fi[g:g + 1] * bt_re_ref[d, g])
    fill(wcr, lambda g: c_re_ref[d, g])
    fill(wci, lambda g: c_im_ref[d, g])
    return _lane_row(lbr), _lane_row(lbi)


def _ssm_param_specs():
    pole = pl.BlockSpec((2, SLAB_G, SSM_P), lambda j: (0, j, 0))
    step = pl.BlockSpec((2, SLAB_G, 1), lambda j: (0, j, 0))
    mat = pl.BlockSpec((2, SLAB_G, SSM_H, SSM_P), lambda j: (0, j, 0, 0))
    return [pole, pole, step, mat, mat, mat, mat]


_MAP_SCRATCH = [pltpu.VMEM((SLAB, SLAB_S), F32)] + [pltpu.VMEM((SLAB, SLAB_S), BF16)] * 4
_ENT_SPEC = pl.BlockSpec((1, 2, 2 * CHAINS, SEG, SLAB_S), lambda j: (j, 0, 0, 0, 0))
_NT = (((1,), (1,)), ((), ()))


def _ssm_fwd(u, prm, dskip):
    L, C = u.shape
    n_rows = L // SEG
    tc = _pick(L, (1024, 512, 256))
    u_spec = pl.BlockSpec((L, SLAB), lambda j: (0, j))
    d_spec = pl.BlockSpec((1, SLAB), lambda j: (0, j))

    def body(u_ref, *rest):
        prm_refs, d_ref, y_ref, ent_ref = rest[:7], rest[7], rest[8], rest[9]
        tmp_ref, maps, xr_ref, xi_ref = rest[10], rest[11:15], rest[15], rest[16]
        wbr, wbi, wcr, wci = maps
        y_ref[...] = u_ref[...] * d_ref[...]
        for d in range(2):
            lam_r, lam_i = _ssm_fill_maps(d, prm_refs, tmp_ref, maps)

            def inp(c, _):
                sl = pl.ds(pl.multiple_of(c * tc, tc), tc)
                ub = u_ref[sl, :].astype(BF16)
                xr_ref[sl, :] = jnp.dot(ub, wbr[...], preferred_element_type=F32)
                xi_ref[sl, :] = jnp.dot(ub, wbi[...], preferred_element_type=F32)
                return 0

            lax.fori_loop(0, L // tc, inp, 0)
            ar = jnp.broadcast_to(lam_r, (SEG, SLAB_S))
            ai = jnp.broadcast_to(lam_i, (SEG, SLAB_S))
            entering, _ = _seg_scan(xr_ref, xi_ref, ar, ai, d == 1, n_rows)
            for j, (er, ei) in enumerate(entering):
                ent_ref[0, d, 2 * j] = er
                ent_ref[0, d, 2 * j + 1] = ei

            def outp(c, _):
                sl = pl.ds(pl.multiple_of(c * tc, tc), tc)
                y_ref[sl, :] += (
                    lax.dot_general(xr_ref[sl, :].astype(BF16), wcr[...], _NT, preferred_element_type=F32)
                    - lax.dot_general(xi_ref[sl, :].astype(BF16), wci[...], _NT, preferred_element_type=F32))
                return 0

            lax.fori_loop(0, L // tc, outp, 0)

    return pl.pallas_call(
        body, name="ssm_fwd", grid=(C // SLAB,),
        in_specs=[u_spec] + _ssm_param_specs() + [d_spec],
        out_specs=[u_spec, _ENT_SPEC],
        out_shape=[jax.ShapeDtypeStruct((L, C), F32),
                   jax.ShapeDtypeStruct((C // SLAB, 2, 2 * CHAINS, SEG, SLAB_S), F32)],
        scratch_shapes=_MAP_SCRATCH + [pltpu.VMEM((L, SLAB_S), F32)] * 2,
        compiler_params=_cparams(("arbitrary",)),
    )(u, *prm, dskip)


def _ssm_bwd(u, dy, ent, prm, dskip):
    L, C = u.shape
    n_rows = L // SEG
    n_slab = C // SLAB
    tc = _pick(L, (1024, 512, 256))
    u_spec = pl.BlockSpec((L, SLAB), lambda j: (0, j))
    d_spec = pl.BlockSpec((1, SLAB), lambda j: (0, j))
    pg_spec = pl.BlockSpec((1, 2, PG_ROWS, SLAB_S), lambda j: (j, 0, 0, 0))

    def body(u_ref, dy_ref, ent_ref, *rest):
        prm_refs, d_ref, du_ref, pg_ref = rest[:7], rest[7], rest[8], rest[9]
        tmp_ref, maps, acc_ref = rest[10], rest[11:15], rest[15]
        xr_ref, xi_ref, gr_ref, gi_ref = rest[16:20]
        wbr, wbi, wcr, wci = maps
        du_ref[...] = dy_ref[...] * d_ref[...]
        pg_ref[...] = jnp.zeros_like(pg_ref)
        pg_ref[0, 0, 66:67, 0:SLAB] = _colsum(dy_ref[...] * u_ref[...])
        for d in range(2):
            lam_r, lam_i = _ssm_fill_maps(d, prm_refs, tmp_ref, maps)

            def inp(c, _):
                sl = pl.ds(pl.multiple_of(c * tc, tc), tc)
                ub = u_ref[sl, :].astype(BF16)
                dyb = dy_ref[sl, :].astype(BF16)
                xr_ref[sl, :] = jnp.dot(ub, wbr[...], preferred_element_type=F32)
                xi_ref[sl, :] = jnp.dot(ub, wbi[...], preferred_element_type=F32)
                gr_ref[sl, :] = jnp.dot(dyb, wcr[...], preferred_element_type=F32)
                gi_ref[sl, :] = -jnp.dot(dyb, wci[...], preferred_element_type=F32)
                return 0

            lax.fori_loop(0, L // tc, inp, 0)
            ar = jnp.broadcast_to(lam_r, (SEG, SLAB_S))
            ai = jnp.broadcast_to(lam_i, (SEG, SLAB_S))
            entering = [(ent_ref[0, d, 2 * j], ent_ref[0, d, 2 * j + 1]) for j in range(CHAINS)]
            _seg_scan(xr_ref, xi_ref, ar, ai, d == 1, n_rows, entering=entering)

            def pole(rr, lr, li, acc, last):
                if last:
                    pr, pi = entering[0]
                else:
                    nb = _rows8(rr + 1 if d == 1 else rr - 1)
                    pr, pi = xr_ref[nb, :], xi_ref[nb, :]
                return acc[0] + lr * pr + li * pi, acc[1] + li * pr - lr * pi

            zero = jnp.zeros((SEG, SLAB_S), F32)
            _, (accr, acci) = _seg_scan(gr_ref, gi_ref, ar, -ai, d == 0, n_rows, pole, (zero, zero))
            pg_ref[0, d, 64:65, :] = _colsum(accr)
            pg_ref[0, d, 65:66, :] = _colsum(acci)

            acc_ref[...] = jnp.zeros_like(acc_ref)

            def outp(c, _):
                sl = pl.ds(pl.multiple_of(c * tc, tc), tc)
                lrb, lib = gr_ref[sl, :].astype(BF16), gi_ref[sl, :].astype(BF16)
                du_ref[sl, :] += (lax.dot_general(lrb, wbr[...], _NT, preferred_element_type=F32)
                                  + lax.dot_general(lib, wbi[...], _NT, preferred_element_type=F32))
                ut = u_ref[sl, :].T.astype(BF16)
                dyt = dy_ref[sl, :].T.astype(BF16)
                acc_ref[0] += jnp.dot(ut, lrb, preferred_element_type=F32)
                acc_ref[1] += jnp.dot(ut, lib, preferred_element_type=F32)
                acc_ref[2] += jnp.dot(dyt, xr_ref[sl, :].astype(BF16), preferred_element_type=F32)
                acc_ref[3] -= jnp.dot(dyt, xi_ref[sl, :].astype(BF16), preferred_element_type=F32)
                return 0

            lax.fori_loop(0, L // tc, outp, 0)
            for m in range(4):
                for g in range(SLAB_G):
                    lanes = slice(g * SSM_P, (g + 1) * SSM_P)
                    pg_ref[0, d, m * SSM_H:(m + 1) * SSM_H, lanes] = acc_ref[m, g * SSM_H:(g + 1) * SSM_H, lanes]

    return pl.pallas_call(
        body, name="ssm_bwd", grid=(n_slab,),
        in_specs=[u_spec, u_spec, _ENT_SPEC] + _ssm_param_specs() + [d_spec],
        out_specs=[u_spec, pg_spec],
        out_shape=[jax.ShapeDtypeStruct((L, C), F32), jax.ShapeDtypeStruct((n_slab, 2, PG_ROWS, SLAB_S), F32)],
        scratch_shapes=_MAP_SCRATCH + [pltpu.VMEM((4, SLAB, SLAB_S), F32)] + [pltpu.VMEM((L, SLAB_S), F32)] * 4,
        compiler_params=_cparams(("arbitrary",), 60 << 20),
    )(u, dy, ent, *prm, dskip)


def _ssm_param_grads(pg, prm):
    n_slab = pg.shape[0]
    G = n_slab * SLAB_G
    pg_spec = pl.BlockSpec((1, 2, PG_ROWS, SLAB_S), lambda j: (j, 0, 0, 0))
    pole, _, step, mat = _ssm_param_specs()[:4]

    def body(pg_ref, a_re_ref, a_im_ref, ldt_ref, bt_re_ref, bt_im_ref,
             dbr_ref, dbi_ref, dcr_ref, dci_ref, dar_ref, dai_ref, dldt_ref, dd_ref):
        dd_ref[...] = pg_ref[0, 0, 66:67, 0:SLAB]
        for d in range(2):
            a_r = a_re_ref[d]
            lr, li, dt, lbr, lbi, f_r, f_i = _discretise(a_r, a_im_ref[d], ldt_ref[d])
            gfr_rows, gfi_rows, glr_rows, gli_rows = [], [], [], []
            for g in range(SLAB_G):
                lanes = slice(g * SSM_P, (g + 1) * SSM_P)
                gbr, gbi = pg_ref[0, d, 0:SSM_H, lanes], pg_ref[0, d, SSM_H:2 * SSM_H, lanes]
                b_r, b_i = bt_re_ref[d, g], bt_im_ref[d, g]
                fr, fi = f_r[g:g + 1], f_i[g:g + 1]
                dbr_ref[d, g] = fr * gbr + fi * gbi
                dbi_ref[d, g] = fr * gbi - fi * gbr
                gfr_rows.append(_colsum(gbr * b_r + gbi * b_i))
                gfi_rows.append(_colsum(gbi * b_r - gbr * b_i))
                dcr_ref[d, g] = pg_ref[0, d, 2 * SSM_H:3 * SSM_H, lanes]
                dci_ref[d, g] = pg_ref[0, d, 3 * SSM_H:4 * SSM_H, lanes]
                glr_rows.append(pg_ref[0, d, 64:65, lanes])
                gli_rows.append(pg_ref[0, d, 65:66, lanes])
            gfr, gfi = jnp.concatenate(gfr_rows, axis=0), jnp.concatenate(gfi_rows, axis=0)
            glr, gli = jnp.concatenate(glr_rows, axis=0), jnp.concatenate(gli_rows, axis=0)
            den = lr * lr + li * li
            ir, ii = lr / den, -li / den
            tr, ti = _cmul(ir, -ii, gfr, gfi)
            glbr, glbi = glr + tr, gli + ti
            qr, qi = _cmul(f_r, f_i, ir, ii)
            dlr, dli = _cmul(-qr, qi, gfr, gfi)
            zr, zi = _cmul(lbr, -lbi, glbr, glbi)
            dlr = dlr + dt * zr
            dli = dli + dt * zi
            dar_ref[d] = jnp.where(a_r < -1e-4, dlr, jnp.where(a_r == -1e-4, 0.5 * dlr, 0.0))
            dai_ref[d] = dli
            dldt_ref[d] = jnp.sum(lr * zr + li * zi, axis=-1, keepdims=True) * dt

    a_re, a_im, ldt, bt_re, bt_im = prm[:5]
    mshape = jax.ShapeDtypeStruct(bt_re.shape, F32)
    pshape = jax.ShapeDtypeStruct(a_re.shape, F32)
    return pl.pallas_call(
        body, name="ssm_param_grads", grid=(n_slab,),
        in_specs=[pg_spec, pole, pole, step, mat, mat],
        out_specs=[mat, mat, mat, mat, pole, pole, step, pl.BlockSpec((1, SLAB), lambda j: (0, j))],
        out_shape=[mshape, mshape, mshape, mshape, pshape, pshape, jax.ShapeDtypeStruct(ldt.shape, F32),
                   jax.ShapeDtypeStruct((1, n_slab * SLAB), F32)],
        compiler_params=_cparams(("arbitrary",)),
    )(pg, a_re, a_im, ldt, bt_re, bt_im)


def _peer(k, x, y, c):
    return (1 - x if k & 4 else x, 1 - y if k & 2 else y, 1 - c if k & 1 else c)


def _dev_index(pos):
    return 4 * pos[0] + 2 * pos[1] + pos[2]


def _chip_index(pos):
    return 2 * pos[0] + pos[1]


def _sibling_swap(x, name):
    any_spec = pl.BlockSpec(memory_space=pl.ANY)

    def body(x_ref, out_ref, send_sems, recv_sems):
        x_, y_, c_ = lax.axis_index("x"), lax.axis_index("y"), lax.axis_index("c")
        copies = [pltpu.make_async_remote_copy(
            src_ref=x_ref.at[2 * chip + (1 - c_)], dst_ref=out_ref.at[chip], send_sem=send_sems.at[chip],
            recv_sem=recv_sems.at[chip], device_id=(x_, y_, 1 - c_), device_id_type=pl.DeviceIdType.MESH)
            for chip in range(N_CHIPS)]
        for cp in copies:
            cp.start()
        for cp in copies:
            cp.wait()

    return pl.pallas_call(
        body, name=name, out_shape=jax.ShapeDtypeStruct((N_CHIPS,) + x.shape[1:], x.dtype),
        in_specs=[any_spec], out_specs=any_spec,
        scratch_shapes=[pltpu.SemaphoreType.DMA((N_CHIPS,)), pltpu.SemaphoreType.DMA((N_CHIPS,))],
    )(x)


def _pair_sum(a, b, name):
    n, R, W = a.shape
    tr = _row_tile(R, W, 5 << 20)

    def body(a_ref, b_ref, o_ref):
        o_ref[...] = (a_ref[...].astype(F32) + b_ref[...].astype(F32)).astype(BF16)

    spec = pl.BlockSpec((1, tr, W), lambda i, j: (i, j, 0))
    return pl.pallas_call(
        body, name=name, grid=(n, R // tr), in_specs=[spec, spec], out_specs=spec,
        out_shape=jax.ShapeDtypeStruct((n, R, W), BF16), compiler_params=_cparams(("arbitrary", "arbitrary")),
    )(a, b)


def _all_gather(xs, name):
    n = len(xs)
    any_spec = pl.BlockSpec(memory_space=pl.ANY)

    def body(*refs):
        x_refs, out_refs = refs[:n], refs[n:2 * n]
        send_sems, recv_sems, local_sems = refs[2 * n:]
        x, y, c = lax.axis_index("x"), lax.axis_index("y"), lax.axis_index("c")
        me, sibling = (x, y, c), (x, y, 1 - c)
        chips = [(1 - x, y), (x, 1 - y), (1 - x, 1 - y)]

        def copy(a, k, block, to, src=None):
            dst = out_refs[a].at[_dev_index(block)]
            return pltpu.make_async_remote_copy(
                src_ref=dst if src is None else src, dst_ref=dst,
                send_sem=send_sems.at[a, k], recv_sem=recv_sems.at[a, k],
                device_id=to, device_id_type=pl.DeviceIdType.MESH)

        mine = [pltpu.make_async_copy(x_refs[a], out_refs[a].at[_dev_index(me)], local_sems.at[a]) for a in range(n)]
        for cp in mine:
            cp.start()
        first = []
        for a in range(n):
            first.append(copy(a, 0, me, sibling, src=x_refs[a]))
            first += [copy(a, 1 + j, me, (*chip, c), src=x_refs[a]) for j, chip in enumerate(chips)]
        for cp in first:
            cp.start()
        passed = []
        for j, chip in enumerate(chips):
            for a in range(n):
                copy(a, 1 + j, (*chip, c), me).wait_recv()
                fwd = copy(a, 4 + j, (*chip, c), sibling)
                fwd.start()
                passed.append(fwd)
        for a in range(n):
            copy(a, 0, sibling, me).wait_recv()
            for j, chip in enumerate(chips):
                copy(a, 4 + j, (*chip, 1 - c), me).wait_recv()
        for cp in first + passed:
            cp.wait_send()
        for cp in mine:
            cp.wait()

    return pl.pallas_call(
        body, name=name,
        out_shape=[jax.ShapeDtypeStruct((N_DEV,) + v.shape, v.dtype) for v in xs],
        in_specs=[any_spec] * n, out_specs=[any_spec] * n,
        scratch_shapes=[pltpu.SemaphoreType.DMA((n, 7)), pltpu.SemaphoreType.DMA((n, 7)),
                        pltpu.SemaphoreType.DMA((n,))],
    )(*xs)


def _sum_blocks(x, name):
    _, R, W = x.shape

    def body(x_ref, o_ref):
        acc = x_ref[0].astype(F32)
        for d in range(1, N_DEV):
            acc = acc + x_ref[d].astype(F32)
        o_ref[...] = acc

    return pl.pallas_call(body, name=name, out_shape=jax.ShapeDtypeStruct((R, W), F32),
                          compiler_params=pltpu.CompilerParams(vmem_limit_bytes=VMEM_LIMIT))(x)


def _adam_update(w, g, m, v):
    mn = ADAM_B1 * m + (1.0 - ADAM_B1) * g
    vn = ADAM_B2 * v + (1.0 - ADAM_B2) * (g * g)
    m_hat = mn / (1.0 - ADAM_B1 ** ADAM_STEP)
    v_hat = vn / (1.0 - ADAM_B2 ** ADAM_STEP)
    return -ADAM_LR * (m_hat / (jnp.sqrt(v_hat) + ADAM_EPS) + ADAM_WD * w), mn, vn


def _row_tile(R, W, budget):
    padded = -(-W // LANES) * LANES * 4
    if R * padded <= budget:
        return R
    return _pick(R, [t for t in (2048, 1024, 512, 256, 128, 64, 32, 16, 8) if t * padded <= budget])


def _adamw(w, g, m, v, name):
    R, W = w.shape
    tr = _row_tile(R, W, 1 << 20)

    def body(w_ref, g_ref, m_ref, v_ref, d_ref, mo_ref, vo_ref):
        d_ref[...], mo_ref[...], vo_ref[...] = _adam_update(w_ref[...], g_ref[...], m_ref[...], v_ref[...])

    spec = pl.BlockSpec((tr, W), lambda i: (i, 0))
    shp = jax.ShapeDtypeStruct((R, W), F32)
    return pl.pallas_call(
        body, name=name, grid=(R // tr,), in_specs=[spec] * 4, out_specs=[spec] * 3, out_shape=[shp] * 3,
        compiler_params=_cparams(("arbitrary",)),
    )(w, g, m, v)


def _adamw_reduce(w, land, m, v, name):
    R, W = w.shape
    n = land.shape[0]
    tr = _row_tile(R, W, 1 << 20)

    def body(w_ref, l_ref, m_ref, v_ref, g_ref, d_ref, mo_ref, vo_ref):
        g = l_ref[0].astype(F32)
        for d in range(1, n):
            g = g + l_ref[d].astype(F32)
        g_ref[...] = g
        d_ref[...], mo_ref[...], vo_ref[...] = _adam_update(w_ref[...], g, m_ref[...], v_ref[...])

    spec = pl.BlockSpec((tr, W), lambda i: (i, 0))
    lspec = pl.BlockSpec((n, tr, W), lambda i: (0, i, 0))
    shp = jax.ShapeDtypeStruct((R, W), F32)
    return pl.pallas_call(
        body, name=name, grid=(R // tr,), in_specs=[spec, lspec, spec, spec], out_specs=[spec] * 4,
        out_shape=[shp] * 4, compiler_params=_cparams(("arbitrary",)),
    )(w, land, m, v)


def _gelu(v):
    c = math.sqrt(2.0 / math.pi)
    return 0.5 * v * (1.0 + jnp.tanh(c * (v + 0.044715 * v * v * v)))


def _gelu_grad(v):
    c = math.sqrt(2.0 / math.pi)
    t = jnp.tanh(c * (v + 0.044715 * v * v * v))
    return 0.5 * (1.0 + t) + 0.5 * v * (1.0 - t * t) * c * (1.0 + 3.0 * 0.044715 * v * v)


def kernel(x, p, norm_mix, w_in, q_norm, k_norm, ssm_a_re, ssm_a_im, ssm_log_dt, ssm_b_re, ssm_b_im, ssm_c_re, ssm_c_im, ssm_d, w_glu, b_glu, w_out, norm_ple, w_ple_gate, w_ple_proj, norm_final, loss_target, m_norm_mix, m_w_in, m_q_norm, m_k_norm, m_ssm_a_re, m_ssm_a_im, m_ssm_log_dt, m_ssm_b_re, m_ssm_b_im, m_ssm_c_re, m_ssm_c_im, m_ssm_d, m_w_glu, m_b_glu, m_w_out, m_norm_ple, m_w_ple_gate, m_w_ple_proj, m_norm_final, v_norm_mix, v_w_in, v_q_norm, v_k_norm, v_ssm_a_re, v_ssm_a_im, v_ssm_log_dt, v_ssm_b_re, v_ssm_b_im, v_ssm_c_re, v_ssm_c_im, v_ssm_d, v_w_glu, v_b_glu, v_w_out, v_norm_ple, v_w_ple_gate, v_w_ple_proj, v_norm_final):
    L, D = x.shape[1], x.shape[2]
    D_SSM = ssm_d.shape[1]
    G = D_SSM // SSM_H
    n_slab = D_SSM // SLAB
    n_in = w_in.shape[2]
    D_IN = n_in * N_DEV
    n_pp = w_ple_proj.shape[2]
    n_glu = w_glu.shape[2]
    xs = x[0]
    ps = p[0, 0]
    tgt = loss_target[0]

    (win_t3,) = _all_gather([w_in[0].T.astype(BF16)], "gather_w_in")
    win_t = win_t3.reshape(D_IN, D)
    later_weights = _Carry("gather", [w_glu[0].astype(BF16), w_out[0].astype(BF16), w_ple_gate[0].astype(BF16),
                                      w_ple_proj[0].astype(BF16)])

    ssm_prm = (ssm_a_re[0], ssm_a_im[0], ssm_log_dt[0].reshape(2, G, 1),
               ssm_b_re[0].transpose(0, 1, 3, 2), ssm_b_im[0].transpose(0, 1, 3, 2), ssm_c_re[0], ssm_c_im[0])

    cos, sin = _rope_tables(L)
    hn = _norm_in(xs, norm_mix, "norm_mix")
    ZT = 512
    zp_tile = lambda j: jnp.where(j < 2, j, jnp.where(j < D_IN // ZT - 1, j + 1, 2))
    z = _mm(hn, win_t, "nt", "in_proj", out_dtype=BF16, n_tiles=(ZT, zp_tile))
    qr, kr, vb, kt = _qkv_prep(z, cos, sin, q_norm, k_norm)
    o, lse, (wglu3, wout3, wpg3, wpp3) = _attn_fwd(qr, kr, vb, later_weights)
    wout = wout3.reshape(-1, D)
    wpg = wpg3.reshape(-1, D)
    u_off = 2 * D_ATTN
    u_perm = _seg_perm(z[:, u_off:u_off + D_SSM]).astype(F32)
    ys_perm, ssm_ent = _ssm_fwd(u_perm, ssm_prm, ssm_d)
    ys = _seg_unperm(ys_perm)

    tm = _pick(L, (256,))

    def gelu_body(y_ref, o_ref):
        o_ref[...] = _gelu(y_ref[...]).astype(BF16)

    (gy,) = _rowcall(gelu_body, "gelu", L, tm, [(ys, _rspec(tm, D_SSM))], [(D_SSM, BF16)])
    glu = _mm(gy, wglu3, "nn", "glu_proj", out_dtype=BF16, bias=b_glu, b_blk=True)

    def mix_body(o_ref, ga_ref, gla_ref, glb_ref, gs_ref, cat_ref):
        ga, gs = _f32(ga_ref), _f32(gs_ref)
        cat_ref[:, :D_ATTN] = (o_ref[...] * ga * _sigmoid(ga)).astype(BF16)
        cat_ref[:, D_ATTN:] = (_f32(gla_ref) * _sigmoid(_f32(glb_ref)) * gs * _sigmoid(gs)).astype(BF16)

    (cat,) = _rowcall(mix_body, "mix", L, tm,
                      [(o, _rspec(tm, D_ATTN)), (z, _rspec(tm, D_ATTN, 1)), (glu, _rspec(tm, D_SSM, 0)),
                       (glu, _rspec(tm, D_SSM, 1)), (z, _rspec(tm, D_SSM, 3))], [(D_ATTN + D_SSM, BF16)])
    h1 = _mm(cat, wout, "nn", "out_proj", add=xs)
    n2 = _norm_in(h1, norm_ple, "norm_ple")
    gpre = _mm(n2, wpg, "nn", "ple_gate", out_dtype=BF16)
    pb = ps.astype(BF16)
    pp = _mm(pb, wpp3, "nn", "ple_proj", out_dtype=BF16, b_blk=True)

    nf = norm_final.reshape(1, D)

    def tail_body(h1_ref, gp_ref, pp_ref, t_ref, g_ref, dh2_ref, dpp_ref, dsg_ref, loss_ref, dg_ref):
        gate = _sigmoid(_f32(gp_ref))
        ppv = _f32(pp_ref)
        h2 = h1_ref[...] + gate * ppv
        r = _rms(h2)
        hh = h2 * r
        err = hh * g_ref[...] - t_ref[...]
        _acc(loss_ref, jnp.broadcast_to(0.5 * jnp.sum(jnp.mean(err * err, axis=-1, keepdims=True)), loss_ref.shape))
        dy = err * (1.0 / D)
        _acc(dg_ref, _colsum(dy * hh))
        dh2 = _rms_bwd(dy, hh, r, g_ref[...])
        dh2_ref[...] = dh2
        dpp_ref[...] = (dh2 * gate).astype(BF16)
        dsg_ref[...] = (dh2 * ppv * gate * (1.0 - gate)).astype(BF16)

    dh2, dpp, dsg, loss_acc, d_nf = _rowcall(
        tail_body, "tail", L, tm,
        [(h1, _rspec(tm, D)), (gpre, _rspec(tm, D)), (pp, _rspec(tm, D)), (tgt, _rspec(tm, D)), (nf, _fspec(nf.shape))],
        [(D, F32), (D, BF16), (D, BF16)], [(1, LANES), (1, D)])
    loss = lax.psum(loss_acc[0, 0], ("x", "y", "c"))

    g_wpp3 = _mm(pb, dpp, "tn", "d_ple_proj", out_dtype=BF16, out_blk=n_pp)
    g_wpg = _mm(n2, dsg, "tn", "d_ple_gate", out_dtype=BF16)
    dn2 = _mm(dsg, wpg, "nt", "d_norm_ple_in", out_dtype=BF16)

    def ple_bwd_body(h1_ref, dn_ref, dh2_ref, g_ref, dh1_ref, dh1b_ref, dg_ref):
        h1v = h1_ref[...]
        r = _rms(h1v)
        hh = h1v * r
        dn = _f32(dn_ref)
        _acc(dg_ref, _colsum(dn * hh))
        dh1 = dh2_ref[...] + _rms_bwd(dn, hh, r, g_ref[...])
        dh1_ref[...] = dh1
        dh1b_ref[...] = dh1.astype(BF16)

    dh1, dh1b, d_nple = _rowcall(
        ple_bwd_body, "ple_bwd", L, tm,
        [(h1, _rspec(tm, D)), (dn2, _rspec(tm, D)), (dh2, _rspec(tm, D)), (norm_ple, _fspec(norm_ple.shape))],
        [(D, F32), (D, BF16)], [(1, D)])

    dcat = _mm(dh1b, wout, "nt", "d_cat", out_dtype=BF16)
    g_wout = _mm(cat, dh1b, "tn", "d_out_proj", out_dtype=BF16)

    def mix_bwd_body(dca_ref, dcs_ref, o_ref, ga_ref, gla_ref, glb_ref, gs_ref,
                     do_ref, dga_ref, dgs_ref, dglu_ref, db_ref):
        dca, dcs, ga, gs = _f32(dca_ref), _f32(dcs_ref), _f32(ga_ref), _f32(gs_ref)
        sa, ss, sb = _sigmoid(ga), _sigmoid(gs), _sigmoid(_f32(glb_ref))
        gla = _f32(gla_ref)
        do_ref[...] = (dca * ga * sa).astype(BF16)
        dga_ref[...] = (dca * o_ref[...] * sa * (1.0 + ga * (1.0 - sa))).astype(BF16)
        dgs_ref[...] = (dcs * gla * sb * ss * (1.0 + gs * (1.0 - ss))).astype(BF16)
        dy2 = dcs * gs * ss
        da, db = dy2 * sb, dy2 * gla * sb * (1.0 - sb)
        dglu_ref[:, :D_SSM] = da.astype(BF16)
        dglu_ref[:, D_SSM:] = db.astype(BF16)
        _acc(db_ref, jnp.concatenate([_colsum(da), _colsum(db)], axis=-1))

    do, dga, dgs, dglu, g_bglu = _rowcall(
        mix_bwd_body, "mix_bwd", L, tm,
        [(dcat, _rspec(tm, D_ATTN, 0)), (dcat, _rspec(tm, D_SSM, 1)), (o, _rspec(tm, D_ATTN)),
         (z, _rspec(tm, D_ATTN, 1)), (glu, _rspec(tm, D_SSM, 0)), (glu, _rspec(tm, D_SSM, 1)),
         (z, _rspec(tm, D_SSM, 3))],
        [(D_ATTN, BF16), (D_ATTN, BF16), (D_SSM, BF16), (2 * D_SSM, BF16)], [(1, 2 * D_SSM)])

    g_wglu3 = _mm(gy, dglu, "tn", "d_glu_proj", out_dtype=BF16, out_blk=n_glu)
    dgy = _mm(dglu, wglu3, "nt", "d_gelu_out", out_dtype=BF16, b_blk=True)

    def gelu_bwd_body(dg_ref, y_ref, o_ref):
        o_ref[...] = _f32(dg_ref) * _gelu_grad(y_ref[...])

    (dys,) = _rowcall(gelu_bwd_body, "gelu_bwd", L, tm,
                      [(dgy, _rspec(tm, D_SSM)), (ys, _rspec(tm, D_SSM))], [(D_SSM, F32)])
    du_perm, pg = _ssm_bwd(u_perm, _seg_perm(dys), ssm_ent, ssm_prm, ssm_d)
    du = _seg_unperm(du_perm)

    pg_send = pg.reshape(N_DEV, (n_slab // N_DEV) * 2 * PG_ROWS, SLAB_S)
    dqs, dkr, dvv, (l_wglu, l_wout, l_wpg, l_wpp, l_pg) = _attn_bwd(
        qr, kr, vb, kt, do, o, lse,
        _Carry("a2a", [g_wglu3, g_wout.reshape(N_DEV, -1, D), g_wpg.reshape(N_DEV, -1, D), g_wpp3, pg_send]))

    scale = HEAD_DIM ** -0.5
    kblk = 4 * D_ATTN // D_KV
    tmq = _pick(L, (512, 256))

    def qkv_bwd_body(dq_ref, dk_ref, dv_ref, q_ref, k_ref, cos_ref, sin_ref, qn_ref, kn_ref,
                     dqo_ref, dko_ref, dvo_ref, dqn_ref, dkn_ref):
        c, s = cos_ref[...], sin_ref[...]

        def head(g, xh, w):
            dn = g * c + _partner(g * s)
            r = _rms(xh)
            xhat = xh * r
            return _rms_bwd(dn, xhat, r, w), _colsum(dn * xhat)

        dqn = jnp.zeros((1, HEAD_DIM), F32)
        for h in range(N_HEADS):
            sl = slice(h * HEAD_DIM, (h + 1) * HEAD_DIM)
            dx, dw = head(dq_ref[:, sl] * scale, q_ref[:, sl].astype(F32), qn_ref[...])
            dqo_ref[:, sl] = dx.astype(BF16)
            dqn = dqn + dw
        dkn = jnp.zeros((1, HEAD_DIM), F32)
        for h in range(N_KV):
            sl = slice(h * HEAD_DIM, (h + 1) * HEAD_DIM)
            dx, dw = head(dk_ref[:, sl], k_ref[:, sl].astype(F32), kn_ref[...])
            dko_ref[:, sl] = dx.astype(BF16)
            dkn = dkn + dw
        dvo_ref[...] = dv_ref[...].astype(BF16)
        _acc(dqn_ref, dqn)
        _acc(dkn_ref, dkn)

    dq, dk, dv, g_qn, g_kn = _rowcall(
        qkv_bwd_body, "qkv_bwd", L, tmq,
        [(dqs, _rspec(tmq, D_ATTN)), (dkr, _rspec(tmq, D_KV)), (dvv, _rspec(tmq, D_KV)),
         (z, _rspec(tmq, D_ATTN, 0)), (z, _rspec(tmq, D_KV, kblk)), (cos, _rspec(tmq, HEAD_DIM)),
         (sin, _rspec(tmq, HEAD_DIM)), (q_norm, _fspec(q_norm.shape)), (k_norm, _fspec(k_norm.shape))],
        [(D_ATTN, BF16), (D_KV, BF16), (D_KV, BF16)], [(1, HEAD_DIM), (1, HEAD_DIM)])

    dz = jnp.concatenate([dq, dk, dv, dga, du.astype(BF16), dgs], axis=1)
    g_win_t = _mm(dz, hn, "tn", "d_in_proj", out_dtype=BF16)
    g_win8 = g_win_t.reshape(N_DEV, n_in, D)
    from_sibling = _sibling_swap(g_win8, "swap_d_w_in")
    own = lax.dynamic_index_in_dim(g_win8.reshape(N_CHIPS, 2, n_in, D), lax.axis_index("c"), axis=1, keepdims=False)
    pair = _pair_sum(own, from_sibling, "pair_sum_d_w_in")
    dhn, (l_win_t,) = _mm(dz, win_t, "nn", "d_norm_mix_in", out_dtype=BF16,
                          carry=_Carry("a2a_chips", [pair]))

    def in_bwd_body(x_ref, dn_ref, dh1_ref, g_ref, dx_ref, dg_ref):
        xv = x_ref[...]
        r = _rms(xv)
        hh = xv * r
        dn = _f32(dn_ref)
        _acc(dg_ref, _colsum(dn * hh))
        dx_ref[...] = dh1_ref[...] + _rms_bwd(dn, hh, r, g_ref[...])

    grad_x, g_nmix = _rowcall(
        in_bwd_body, "in_bwd", L, tm,
        [(xs, _rspec(tm, D)), (dhn, _rspec(tm, D)), (dh1, _rspec(tm, D)), (norm_mix, _fspec(norm_mix.shape))],
        [(D, F32)], [(1, D)])

    tiny_parts = [g_nmix, g_bglu, d_nple, d_nf, g_qn, g_kn]
    tiny_flat = jnp.concatenate([t.reshape(-1) for t in tiny_parts])
    tiny_rows = -(-tiny_flat.shape[0] // (8 * LANES)) * 8
    tiny = jnp.pad(tiny_flat, (0, tiny_rows * LANES - tiny_flat.shape[0])).reshape(tiny_rows, LANES)
    pg_sum = _sum_blocks(l_pg, "sum_ssm_grads")
    pg_all, tiny_all = _all_gather([pg_sum, tiny], "gather_small_grads")
    (g_bt_re, g_bt_im, g_c_re, g_c_im, g_a_re, g_a_im, g_ldt, g_skip) = _ssm_param_grads(
        pg_all.reshape(n_slab, 2, PG_ROWS, SLAB_S), ssm_prm)
    tiny_sum = _sum_blocks(tiny_all, "sum_tiny_grads").reshape(-1)
    tiny_grads, off = [], 0
    for t in tiny_parts:
        tiny_grads.append(tiny_sum[off:off + t.size].reshape(t.shape))
        off += t.size
    r_nmix, r_bglu, r_nple, r_nf, r_qn, r_kn = tiny_grads

    grads, deltas, new_ms, new_vs = {}, {}, {}, {}
    outs = _adamw_reduce(w_in[0].T, l_win_t, m_w_in[0].T, v_w_in[0].T, "adamw_w_in")
    grads["w_in"], deltas["w_in"], new_ms["w_in"], new_vs["w_in"] = [t.T[None] for t in outs]
    big = [("w_glu", w_glu, l_wglu, m_w_glu, v_w_glu),
           ("w_out", w_out, l_wout, m_w_out, v_w_out), ("w_ple_gate", w_ple_gate, l_wpg, m_w_ple_gate, v_w_ple_gate),
           ("w_ple_proj", w_ple_proj, l_wpp, m_w_ple_proj, v_w_ple_proj)]
    for name, w, ld, m, v in big:
        shp = w.shape
        outs = _adamw_reduce(w[0], ld, m[0], v[0], "adamw_" + name)
        grads[name], deltas[name], new_ms[name], new_vs[name] = [t.reshape(shp) for t in outs]
    bt2 = (2 * G * SSM_H, SSM_P)
    for name, w, g, m, v in (("ssm_b_re", ssm_b_re, g_bt_re, m_ssm_b_re, v_ssm_b_re),
                             ("ssm_b_im", ssm_b_im, g_bt_im, m_ssm_b_im, v_ssm_b_im)):
        to2 = lambda t: t[0].transpose(0, 1, 3, 2).reshape(bt2)
        back = lambda t: t.reshape(2, G, SSM_H, SSM_P).transpose(0, 1, 3, 2)[None]
        outs = _adamw(to2(w), g.reshape(bt2), to2(m), to2(v), "adamw_" + name)
        grads[name] = back(g)
        deltas[name], new_ms[name], new_vs[name] = [back(t) for t in outs]
    small = [("norm_mix", norm_mix, r_nmix, m_norm_mix, v_norm_mix, (1, D)),
             ("q_norm", q_norm, r_qn, m_q_norm, v_q_norm, (1, HEAD_DIM)),
             ("k_norm", k_norm, r_kn, m_k_norm, v_k_norm, (1, HEAD_DIM)),
             ("ssm_a_re", ssm_a_re, g_a_re, m_ssm_a_re, v_ssm_a_re, (2 * G, SSM_P)),
             ("ssm_a_im", ssm_a_im, g_a_im, m_ssm_a_im, v_ssm_a_im, (2 * G, SSM_P)),
             ("ssm_log_dt", ssm_log_dt, g_ldt, m_ssm_log_dt, v_ssm_log_dt, (2, G)),
             ("ssm_c_re", ssm_c_re, g_c_re, m_ssm_c_re, v_ssm_c_re, (2 * G * SSM_H, SSM_P)),
             ("ssm_c_im", ssm_c_im, g_c_im, m_ssm_c_im, v_ssm_c_im, (2 * G * SSM_H, SSM_P)),
             ("ssm_d", ssm_d, g_skip, m_ssm_d, v_ssm_d, (1, D_SSM)),
             ("b_glu", b_glu, r_bglu, m_b_glu, v_b_glu, (1, 2 * D_SSM)),
             ("norm_ple", norm_ple, r_nple, m_norm_ple, v_norm_ple, (1, D)),
             ("norm_final", norm_final, r_nf, m_norm_final, v_norm_final, (1, D))]
    for name, w, g, m, v, s2 in small:
        shp = w.shape
        outs = _adamw(w.reshape(s2), g.reshape(s2), m.reshape(s2), v.reshape(s2), "adamw_" + name)
        grads[name] = g.reshape(shp)
        deltas[name], new_ms[name], new_vs[name] = [t.reshape(shp) for t in outs]

    order = ["norm_mix", "w_in", "q_norm", "k_norm", "ssm_a_re", "ssm_a_im", "ssm_log_dt", "ssm_b_re", "ssm_b_im",
             "ssm_c_re", "ssm_c_im", "ssm_d", "w_glu", "b_glu", "w_out", "norm_ple", "w_ple_gate", "w_ple_proj",
             "norm_final"]
    return (loss, grad_x[None], *[grads[k] for k in order], *[deltas[k] for k in order],
            *[new_ms[k] for k in order], *[new_vs[k] for k in order])
```

```python
import functools
import math

import numpy as np
import jax
import jax.numpy as jnp
from jax import lax
from jax.experimental import pallas as pl
from jax.experimental.pallas import tpu as pltpu

F32 = jnp.float32
BF16 = jnp.bfloat16

N_DEV = 8
N_CHIPS = 4
EPS = 1e-6
GRID_W = 64
ROPE_THETA = 10000.0
HEAD_DIM = 128
N_HEADS = 8
N_KV = 2
REP = N_HEADS // N_KV
D_ATTN = N_HEADS * HEAD_DIM
D_KV = N_KV * HEAD_DIM
SSM_H = 16
SSM_P = 64
SLAB = 128
SLAB_G = SLAB // SSM_H
SLAB_S = SLAB_G * SSM_P
SEG = 8
CHAINS = 2
SCAN_UNROLL = 4
LANES = 128
PG_ROWS = 72
VMEM_LIMIT = 48 << 20

ADAM_LR = 0.001
ADAM_B1 = 0.9
ADAM_B2 = 0.999
ADAM_EPS = 1e-08
ADAM_WD = 0.01
ADAM_STEP = 10


def _pick(n, cands):
    for c in cands:
        if n % c == 0:
            return c
    return n


def _cparams(sem, vmem=VMEM_LIMIT):
    return pltpu.CompilerParams(dimension_semantics=sem, vmem_limit_bytes=vmem)


class _Carry:
    def __init__(self, kind, xs):
        self.kind, self.xs, self.n = kind, list(xs), len(xs)
        self.ks = (2, 4, 6) if kind == "a2a_chips" else tuple(range(1, N_DEV))
        self.index = _chip_index if kind == "a2a_chips" else _dev_index
        lead = (N_DEV,) if kind == "gather" else ()
        self.out_shape = [jax.ShapeDtypeStruct(lead + v.shape, v.dtype) for v in xs]
        self.specs = [pl.BlockSpec(memory_space=pl.ANY)] * self.n
        self.scratch = [pltpu.SemaphoreType.DMA((self.n, len(self.ks))), pltpu.SemaphoreType.DMA((self.n, len(self.ks))),
                        pltpu.SemaphoreType.DMA((self.n,))]

    def _copies(self, x_refs, out_refs, sems):
        send_sems, recv_sems, local_sems = sems
        x, y, c = lax.axis_index("x"), lax.axis_index("y"), lax.axis_index("c")
        me = self.index((x, y, c))
        mine, sends, arrivals = [], [], []
        for a in range(self.n):
            src_mine = x_refs[a] if self.kind == "gather" else x_refs[a].at[me]
            mine.append(pltpu.make_async_copy(src_mine, out_refs[a].at[me], local_sems.at[a]))
            for s, k in enumerate(self.ks):
                peer = _peer(k, x, y, c)
                src = x_refs[a] if self.kind == "gather" else x_refs[a].at[self.index(peer)]
                sends.append(pltpu.make_async_remote_copy(
                    src_ref=src, dst_ref=out_refs[a].at[me], send_sem=send_sems.at[a, s],
                    recv_sem=recv_sems.at[a, s], device_id=peer, device_id_type=pl.DeviceIdType.MESH))
                land = out_refs[a].at[self.index(peer)]
                arrivals.append(pltpu.make_async_remote_copy(
                    src_ref=land, dst_ref=land, send_sem=send_sems.at[a, s],
                    recv_sem=recv_sems.at[a, s], device_id=peer, device_id_type=pl.DeviceIdType.MESH))
        return mine, sends, arrivals

    def start(self, x_refs, out_refs, sems):
        mine, sends, _ = self._copies(x_refs, out_refs, sems)
        for cp in mine + sends:
            cp.start()

    def wait(self, x_refs, out_refs, sems):
        mine, sends, arrivals = self._copies(x_refs, out_refs, sems)
        for cp in arrivals:
            cp.wait_recv()
        for cp in sends:
            cp.wait_send()
        for cp in mine:
            cp.wait()


def _grid_edges(grid):
    first = functools.reduce(lambda p, q: p & q, [pl.program_id(d) == 0 for d in range(len(grid))])
    last = functools.reduce(lambda p, q: p & q, [pl.program_id(d) == g - 1 for d, g in enumerate(grid)])
    return first, last


def _mm(a, b, mode, name, out_dtype=F32, add=None, bias=None, a_blk=False, b_blk=False, out_blk=0, carry=None,
        n_tiles=None, post=None):
    w = b.shape[2] if b_blk else out_blk
    if mode == "nn":
        M, K = a.shape
        N = b.shape[0] * w if b_blk else b.shape[1]
    elif mode == "nt":
        M = a.shape[1] if a_blk else a.shape[0]
        N = b.shape[1] if b_blk else b.shape[0]
        K = b.shape[0] * w if b_blk else b.shape[1]
    else:
        K, M = a.shape
        N = b.shape[0] * w if b_blk else b.shape[1]
    tm = _pick(M, (1024, 768, 512, 256))
    tn = _pick(N, (1024, 768, 512, 256))
    if mode == "tn" and N <= 2048:
        tn = N
    tk = K if (mode != "tn" and K <= 2048) else _pick(K, (1024, 768, 512, 256))
    perm = lambda j: j
    if n_tiles:
        tn, perm = n_tiles
    if mode == "nt" and b_blk:
        tk = w
    elif b_blk or out_blk:
        tn = w
    nk = K // tk
    grid = (M // tm, N // tn, nk)
    if mode == "nn":
        a_spec = pl.BlockSpec((tm, tk), lambda i, j, k: (i, k))
        b_spec = (pl.BlockSpec((1, tk, tn), lambda i, j, k: (j, k, 0)) if b_blk
                  else pl.BlockSpec((tk, tn), lambda i, j, k: (k, j)))
        dims = (((1,), (0,)), ((), ()))
    elif mode == "nt":
        a_spec = (pl.BlockSpec((1, tm, tk), lambda i, j, k: (k, i, 0)) if a_blk
                  else pl.BlockSpec((tm, tk), lambda i, j, k: (i, k)))
        b_spec = (pl.BlockSpec((1, tn, tk), lambda i, j, k: (k, j, 0)) if b_blk
                  else pl.BlockSpec((tn, tk), lambda i, j, k: (perm(j), k)))
        dims = (((1,), (1,)), ((), ()))
    else:
        a_spec = pl.BlockSpec((tk, tm), lambda i, j, k: (k, i))
        b_spec = (pl.BlockSpec((1, tk, tn), lambda i, j, k: (j, k, 0)) if b_blk
                  else pl.BlockSpec((tk, tn), lambda i, j, k: (k, j)))
        dims = (((0,), (0,)), ((), ()))
    if out_blk:
        out_spec = pl.BlockSpec((1, tm, tn), lambda i, j, k: (j, i, 0))
        out_shape = jax.ShapeDtypeStruct((N // tn, M, tn), out_dtype)
    else:
        out_spec = pl.BlockSpec((tm, tn), lambda i, j, k: (i, j))
        out_shape = jax.ShapeDtypeStruct((M, N), out_dtype)
    extras, extra_specs, combine = [], [], []
    if add is not None:
        extras.append(add)
        extra_specs.append(pl.BlockSpec((tm, tn), lambda i, j, k: (i, j)))
        combine.append(lambda out, t: out + t)
    if bias is not None:
        extras.append(bias)
        extra_specs.append(pl.BlockSpec((1, tn), lambda i, j, k: (0, j)))
        combine.append(lambda out, t: out + t)
    if post is not None:
        extras.append(post[1])
        extra_specs.append(pl.BlockSpec((tm, tn), lambda i, j, k: (i, j)))
        combine.append(post[0])

    n_ex = len(extras)
    nc = carry.n if carry else 0

    def body(a_ref, b_ref, *rest):
        ex_refs, cx = rest[:n_ex], rest[n_ex:n_ex + nc]
        o_ref, cout = rest[n_ex + nc], rest[n_ex + nc + 1:n_ex + 2 * nc + 1]
        tail = rest[n_ex + 2 * nc + 1:]
        sems = tail[:3] if carry else ()
        first, last = _grid_edges(grid)
        if carry:
            @pl.when(first)
            def _():
                carry.start(cx, cout, sems)

        def product():
            av = a_ref[0] if a_blk else a_ref[...]
            bv = b_ref[0] if b_blk else b_ref[...]
            return lax.dot_general(av, bv, dims, preferred_element_type=F32)

        def finish(out):
            for r, fn in zip(ex_refs, combine):
                out = fn(out, r[...])
            if out_blk:
                o_ref[0] = out.astype(out_dtype)
            else:
                o_ref[...] = out.astype(out_dtype)

        if nk == 1:
            finish(product())
        else:
            acc_ref = tail[-1]
            k = pl.program_id(2)

            @pl.when(k == 0)
            def _():
                acc_ref[...] = jnp.zeros_like(acc_ref)

            acc_ref[...] += product()

            @pl.when(k == nk - 1)
            def _():
                finish(acc_ref[...])

        if carry:
            @pl.when(last)
            def _():
                carry.wait(cx, cout, sems)

    scratch = (carry.scratch if carry else []) + ([pltpu.VMEM((tm, tn), F32)] if nk > 1 else [])
    outs = pl.pallas_call(
        body, name=name, grid=grid,
        in_specs=[a_spec, b_spec] + extra_specs + (carry.specs if carry else []),
        out_specs=[out_spec] + (carry.specs if carry else []),
        out_shape=[out_shape] + (carry.out_shape if carry else []),
        scratch_shapes=scratch,
        compiler_params=_cparams(("arbitrary", "arbitrary", "arbitrary")),
    )(a, b, *extras, *(carry.xs if carry else []))
    return (outs[0], outs[1:]) if carry else outs[0]


def _rspec(tm, w, cb=0):
    return pl.BlockSpec((tm, w), lambda i: (i, cb))


def _fspec(shape):
    nd = len(shape)
    return pl.BlockSpec(shape, lambda i: (0,) * nd)


def _rowcall(body, name, L, tm, ins, row_outs, acc_outs=()):
    out_shape = [jax.ShapeDtypeStruct((L, w), dt) for w, dt in row_outs]
    out_shape += [jax.ShapeDtypeStruct(s, F32) for s in acc_outs]
    out_specs = [_rspec(tm, w) for w, _ in row_outs] + [_fspec(s) for s in acc_outs]
    return pl.pallas_call(
        body, name=name, grid=(L // tm,),
        in_specs=[s for _, s in ins], out_specs=out_specs, out_shape=out_shape,
        compiler_params=_cparams(("arbitrary",)),
    )(*[a for a, _ in ins])


def _acc(ref, val):
    @pl.when(pl.program_id(0) == 0)
    def _():
        ref[...] = jnp.zeros_like(ref)
    ref[...] += val


def _colsum(v):
    return jnp.sum(v, axis=0, keepdims=True)


def _rms(xv):
    return lax.rsqrt(jnp.mean(xv * xv, axis=-1, keepdims=True) + EPS)


def _rms_bwd(dn, xhat, r, g):
    dng = dn * g
    return r * (dng - xhat * jnp.mean(dng * xhat, axis=-1, keepdims=True))


def _sigmoid(v):
    return jax.nn.sigmoid(v)


def _f32(ref):
    return ref[...].astype(F32)


def _partner(v):
    w = v.shape[-1]
    lane = lax.broadcasted_iota(jnp.int32, v.shape, v.ndim - 1)
    first_half = (lane % 64) < 32
    return jnp.where(first_half, pltpu.roll(v, w - 32, axis=v.ndim - 1), pltpu.roll(v, 32, axis=v.ndim - 1))


def _norm_in(x, g, name):
    L, D = x.shape
    tm = _pick(L, (512, 256))

    def body(x_ref, g_ref, o_ref):
        xv = x_ref[...]
        o_ref[...] = (xv * _rms(xv) * g_ref[...]).astype(BF16)

    return _rowcall(body, name, L, tm, [(x, _rspec(tm, D)), (g, _fspec(g.shape))], [(D, BF16)])[0]


def _rope_tables(L):
    t = np.arange(L)
    rows = (t // GRID_W).astype(np.float32)
    cols = (t % GRID_W).astype(np.float32)
    n_freq = HEAD_DIM // 4
    inv_freq = np.float32(ROPE_THETA) ** (-np.arange(n_freq, dtype=np.float32) / np.float32(n_freq))
    ar = (rows[:, None] * inv_freq[None, :]).astype(np.float32).astype(np.float64)
    ac = (cols[:, None] * inv_freq[None, :]).astype(np.float32).astype(np.float64)
    cos = np.concatenate([np.cos(ar), np.cos(ar), np.cos(ac), np.cos(ac)], axis=-1).astype(np.float32)
    sin = np.concatenate([-np.sin(ar), np.sin(ar), -np.sin(ac), np.sin(ac)], axis=-1).astype(np.float32)
    return jnp.asarray(cos), jnp.asarray(sin)


def _qkv_prep(z, cos, sin, qn, kn):
    L = z.shape[0]
    tm = _pick(L, (512, 256))
    scale = HEAD_DIM ** -0.5
    kblk = 4 * D_ATTN // D_KV

    def body(q_ref, k_ref, v_ref, cos_ref, sin_ref, qn_ref, kn_ref, qo_ref, ko_ref, vo_ref, kt_ref):
        c, s = cos_ref[...], sin_ref[...]

        def head(xh, w):
            n = xh * _rms(xh) * w
            return n * c + _partner(n) * s

        for h in range(N_HEADS):
            sl = slice(h * HEAD_DIM, (h + 1) * HEAD_DIM)
            qo_ref[:, sl] = (head(q_ref[:, sl].astype(F32), qn_ref[...]) * scale).astype(BF16)
        for h in range(N_KV):
            sl = slice(h * HEAD_DIM, (h + 1) * HEAD_DIM)
            kr = head(k_ref[:, sl].astype(F32), kn_ref[...])
            ko_ref[:, sl] = kr.astype(BF16)
            kt_ref[sl, :] = kr.T.astype(BF16)
        vo_ref[...] = v_ref[...].astype(BF16)

    return pl.pallas_call(
        body, name="qkv_prep", grid=(L // tm,),
        in_specs=[_rspec(tm, D_ATTN, 0), _rspec(tm, D_KV, kblk), _rspec(tm, D_KV, kblk + 1),
                  _rspec(tm, HEAD_DIM), _rspec(tm, HEAD_DIM), _fspec(qn.shape), _fspec(kn.shape)],
        out_specs=[_rspec(tm, D_ATTN), _rspec(tm, D_KV), _rspec(tm, D_KV),
                   pl.BlockSpec((D_KV, tm), lambda i: (0, i))],
        out_shape=[jax.ShapeDtypeStruct((L, D_ATTN), BF16), jax.ShapeDtypeStruct((L, D_KV), BF16),
                   jax.ShapeDtypeStruct((L, D_KV), BF16), jax.ShapeDtypeStruct((D_KV, L), BF16)],
        compiler_params=_cparams(("arbitrary",)),
    )(z, z, z, cos, sin, qn, kn)


def _col_to_row(col):
    n = col.shape[0]
    eye = lax.broadcasted_iota(jnp.int32, (n, n), 0) == lax.broadcasted_iota(jnp.int32, (n, n), 1)
    return jnp.sum(jnp.where(eye, col, 0.0), axis=0, keepdims=True)


def _attn_fwd(q, k, v, carry=None):
    L = q.shape[0]
    tq = _pick(L, (256, 128))
    grid = (N_HEADS, L // tq)
    nc = carry.n if carry else 0

    def body(q_ref, k_ref, v_ref, *rest):
        cx, (o_ref, lse_ref) = rest[:nc], rest[nc:nc + 2]
        cout, sems = rest[nc + 2:2 * nc + 2], rest[2 * nc + 2:]
        first, last = _grid_edges(grid)
        if carry:
            @pl.when(first)
            def _():
                carry.start(cx, cout, sems)

        s = lax.dot_general(q_ref[...], k_ref[...], (((1,), (1,)), ((), ())), preferred_element_type=F32)
        m = jnp.max(s, axis=-1, keepdims=True)
        e = jnp.exp(s - m)
        l = jnp.sum(e, axis=-1, keepdims=True)
        o_ref[...] = jnp.dot(e.astype(BF16), v_ref[...], preferred_element_type=F32) / l
        lse_ref[0] = _col_to_row(m + jnp.log(l))

        if carry:
            @pl.when(last)
            def _():
                carry.wait(cx, cout, sems)

    outs = pl.pallas_call(
        body, name="attn_fwd", grid=grid,
        in_specs=[pl.BlockSpec((tq, HEAD_DIM), lambda h, i: (i, h)),
                  pl.BlockSpec((L, HEAD_DIM), lambda h, i: (0, h // REP)),
                  pl.BlockSpec((L, HEAD_DIM), lambda h, i: (0, h // REP))] + (carry.specs if carry else []),
        out_specs=[pl.BlockSpec((tq, HEAD_DIM), lambda h, i: (i, h)),
                   pl.BlockSpec((1, 1, tq), lambda h, i: (h, 0, i))] + (carry.specs if carry else []),
        out_shape=[jax.ShapeDtypeStruct((L, D_ATTN), F32), jax.ShapeDtypeStruct((N_HEADS, 1, L), F32)]
        + (carry.out_shape if carry else []),
        scratch_shapes=carry.scratch if carry else [],
        compiler_params=_cparams(("arbitrary", "arbitrary")),
    )(q, k, v, *(carry.xs if carry else []))
    return outs[0], outs[1], outs[2:]


def _attn_bwd(q, k, v, kt, do, o, lse, carry=None):
    L = q.shape[0]
    tq = _pick(L, (256, 128))
    kc = _pick(L, (512, 256, 128))
    nt = (((1,), (1,)), ((), ()))
    grid = (N_KV, REP, L // tq)
    nc = carry.n if carry else 0

    def body(q_ref, do_ref, o_ref, lse_ref, k_ref, v_ref, kt_ref, *rest):
        cx, (dq_ref, dk_ref, dv_ref) = rest[:nc], rest[nc:nc + 3]
        cout, sems = rest[nc + 3:2 * nc + 3], rest[2 * nc + 3:]
        first, last = _grid_edges(grid)
        if carry:
            @pl.when(first)
            def _():
                carry.start(cx, cout, sems)

        @pl.when((pl.program_id(1) == 0) & (pl.program_id(2) == 0))
        def _():
            dk_ref[...] = jnp.zeros_like(dk_ref)
            dv_ref[...] = jnp.zeros_like(dv_ref)

        qv, dov = q_ref[...], do_ref[...]
        lse_row = lse_ref[0]
        delta = _col_to_row(jnp.sum(dov.astype(F32) * o_ref[...], axis=-1, keepdims=True))
        dqt = jnp.zeros((HEAD_DIM, tq), F32)
        for c in range(L // kc):
            sl = slice(c * kc, (c + 1) * kc)
            st = lax.dot_general(k_ref[sl, :], qv, nt, preferred_element_type=F32)
            pt = jnp.exp(st - lse_row)
            dpt = lax.dot_general(v_ref[sl, :], dov, nt, preferred_element_type=F32)
            dst = (pt * (dpt - delta)).astype(BF16)
            dv_ref[sl, :] += jnp.dot(pt.astype(BF16), dov, preferred_element_type=F32)
            dk_ref[sl, :] += jnp.dot(dst, qv, preferred_element_type=F32)
            dqt = dqt + jnp.dot(kt_ref[:, sl], dst, preferred_element_type=F32)
        dq_ref[...] = dqt.T

        if carry:
            @pl.when(last)
            def _():
                carry.wait(cx, cout, sems)

    head = lambda g, r, i: (i, g * REP + r)
    outs = pl.pallas_call(
        body, name="attn_bwd", grid=grid,
        in_specs=[pl.BlockSpec((tq, HEAD_DIM), head), pl.BlockSpec((tq, HEAD_DIM), head),
                  pl.BlockSpec((tq, HEAD_DIM), head),
                  pl.BlockSpec((1, 1, tq), lambda g, r, i: (g * REP + r, 0, i)),
                  pl.BlockSpec((L, HEAD_DIM), lambda g, r, i: (0, g)),
                  pl.BlockSpec((L, HEAD_DIM), lambda g, r, i: (0, g)),
                  pl.BlockSpec((HEAD_DIM, L), lambda g, r, i: (g, 0))] + (carry.specs if carry else []),
        out_specs=[pl.BlockSpec((tq, HEAD_DIM), head),
                   pl.BlockSpec((L, HEAD_DIM), lambda g, r, i: (0, g)),
                   pl.BlockSpec((L, HEAD_DIM), lambda g, r, i: (0, g))] + (carry.specs if carry else []),
        out_shape=[jax.ShapeDtypeStruct((L, D_ATTN), F32), jax.ShapeDtypeStruct((L, D_KV), F32),
                   jax.ShapeDtypeStruct((L, D_KV), F32)] + (carry.out_shape if carry else []),
        scratch_shapes=carry.scratch if carry else [],
        compiler_params=_cparams(("arbitrary", "arbitrary", "arbitrary")),
    )(q, do, o, lse, k, v, kt, *(carry.xs if carry else []))
    return outs[0], outs[1], outs[2], outs[3:]


def _seg_perm(a):
    L, C = a.shape
    return a.reshape(SEG, L // SEG, C).transpose(1, 0, 2).reshape(L, C)


def _seg_unperm(a):
    L, C = a.shape
    return a.reshape(L // SEG, SEG, C).transpose(1, 0, 2).reshape(L, C)


def _cmul(ar, ai, br, bi):
    return ar * br - ai * bi, ar * bi + ai * br


def _rows8(rr):
    if isinstance(rr, int):
        return pl.ds(rr * SEG, SEG)
    return pl.ds(pl.multiple_of(rr * SEG, SEG), SEG)


def _seg_scan(xr_ref, xi_ref, ar, ai, reverse, n_rows, visit=None, visit_init=(), entering=None):
    shape = ar.shape
    zero = jnp.zeros(shape, F32)
    rc = n_rows // CHAINS

    def index(q):
        return (n_rows - 1 - q) if reverse else q

    if entering is None:
        def ends(q, carry):
            out = []
            for j in range(CHAINS):
                sl = _rows8(index(j * rc + q))
                pr, pi = _cmul(ar, ai, carry[2 * j], carry[2 * j + 1])
                out += [pr + xr_ref[sl, :], pi + xi_ref[sl, :]]
            return tuple(out)

        def ends_block(qb, carry):
            for t in range(SCAN_UNROLL):
                carry = ends(qb * SCAN_UNROLL + t, carry)
            return carry

        e = lax.fori_loop(0, rc // SCAN_UNROLL, ends_block, (zero,) * (2 * CHAINS))

        pr, pi = ar, ai
        for _ in range(int(math.log2(rc))):
            pr, pi = _cmul(pr, pi, pr, pi)
        sub = lax.broadcasted_iota(jnp.int32, shape, 0)
        shift = (SEG - 1) if reverse else 1
        edge = (SEG - 1) if reverse else 0
        entering = [(zero, zero)] * CHAINS
        for _ in range(SEG):
            tr, ti = _cmul(pr, pi, *entering[CHAINS - 1])
            cur = (jnp.where(sub == edge, 0.0, pltpu.roll(tr + e[2 * CHAINS - 2], shift, axis=0)),
                   jnp.where(sub == edge, 0.0, pltpu.roll(ti + e[2 * CHAINS - 1], shift, axis=0)))
            entering = [cur]
            for j in range(1, CHAINS):
                tr, ti = _cmul(pr, pi, *cur)
                cur = (tr + e[2 * j - 2], ti + e[2 * j - 1])
                entering.append(cur)

    def step(q, carry, last):
        out, acc = [], carry[2 * CHAINS:]
        for j in range(CHAINS):
            rr = index(j * rc + q)
            sl = _rows8(rr)
            pr, pi = _cmul(ar, ai, carry[2 * j], carry[2 * j + 1])
            nr, ni = pr + xr_ref[sl, :], pi + xi_ref[sl, :]
            xr_ref[sl, :] = nr
            xi_ref[sl, :] = ni
            if visit:
                acc = visit(rr, nr, ni, acc, last and j == CHAINS - 1)
            out += [nr, ni]
        return (*out, *acc)

    def step_block(qb, carry):
        for t in range(SCAN_UNROLL):
            carry = step(qb * SCAN_UNROLL + t, carry, False)
        return carry

    start = tuple(v for pair in entering for v in pair)
    n_blocks = (rc - 1) // SCAN_UNROLL
    carry = lax.fori_loop(0, n_blocks, step_block, (*start, *visit_init))
    for q in range(n_blocks * SCAN_UNROLL, rc - 1):
        carry = step(q, carry, False)
    carry = step(rc - 1, carry, True)
    return entering, carry[2 * CHAINS:]


def _discretise(a_re, a_im, ldt):
    lr = jnp.minimum(a_re, -1e-4)
    li = a_im
    dt = jnp.exp(ldt)
    mag = jnp.exp(lr * dt)
    lbr = mag * jnp.cos(li * dt)
    lbi = mag * jnp.sin(li * dt)
    den = lr * lr + li * li
    nr = lbr - 1.0
    fr = (nr * lr + lbi * li) / den
    fi = (lbi * lr - nr * li) / den
    return lr, li, dt, lbr, lbi, fr, fi


def _lane_row(v):
    return jnp.concatenate([v[g:g + 1, :] for g in range(v.shape[0])], axis=1)


def _ssm_fill_maps(d, prm, tmp_ref, maps):
    a_re_ref, a_im_ref, ldt_ref, bt_re_ref, bt_im_ref, c_re_ref, c_im_ref = prm
    _, _, _, lbr, lbi, fr, fi = _discretise(a_re_ref[d], a_im_ref[d], ldt_ref[d])

    def fill(dst, piece):
        tmp_ref[...] = jnp.zeros_like(tmp_ref)
        for g in range(SLAB_G):
            tmp_ref[g * SSM_H:(g + 1) * SSM_H, g * SSM_P:(g + 1) * SSM_P] = piece(g)
        dst[...] = tmp_ref[...].astype(BF16)

    wbr, wbi, wcr, wci = maps
    fill(wbr, lambda g: fr[g:g + 1] * bt_re_ref[d, g] - fi[g:g + 1] * bt_im_ref[d, g])
    fill(wbi, lambda g: fr[g:g + 1] * bt_im_ref[d, g] + fi[g:g + 1] * bt_re_ref[d, g])
    fill(wcr, lambda g: c_re_ref[d, g])
    fill(wci, lambda g: c_im_ref[d, g])
    return _lane_row(lbr), _lane_row(lbi)


def _ssm_param_specs():
    pole = pl.BlockSpec((2, SLAB_G, SSM_P), lambda j: (0, j, 0))
    step = pl.BlockSpec((2, SLAB_G, 1), lambda j: (0, j, 0))
    mat = pl.BlockSpec((2, SLAB_G, SSM_H, SSM_P), lambda j: (0, j, 0, 0))
    return [pole, pole, step, mat, mat, mat, mat]


_MAP_SCRATCH = [pltpu.VMEM((SLAB, SLAB_S), F32)] + [pltpu.VMEM((SLAB, SLAB_S), BF16)] * 4
_ENT_SPEC = pl.BlockSpec((1, 2, 2 * CHAINS, SEG, SLAB_S), lambda j: (j, 0, 0, 0, 0))
_NT = (((1,), (1,)), ((), ()))


def _ssm_fwd(u, prm, dskip):
    L, C = u.shape
    n_rows = L // SEG
    tc = _pick(L, (1024, 512, 256))
    u_spec = pl.BlockSpec((L, SLAB), lambda j: (0, j))
    d_spec = pl.BlockSpec((1, SLAB), lambda j: (0, j))

    def body(u_ref, *rest):
        prm_refs, d_ref, y_ref, ent_ref = rest[:7], rest[7], rest[8], rest[9]
        tmp_ref, maps, xr_ref, xi_ref = rest[10], rest[11:15], rest[15], rest[16]
        wbr, wbi, wcr, wci = maps
        y_ref[...] = u_ref[...] * d_ref[...]
        for d in range(2):
            lam_r, lam_i = _ssm_fill_maps(d, prm_refs, tmp_ref, maps)

            def inp(c, _):
                sl = pl.ds(pl.multiple_of(c * tc, tc), tc)
                ub = u_ref[sl, :].astype(BF16)
                xr_ref[sl, :] = jnp.dot(ub, wbr[...], preferred_element_type=F32)
                xi_ref[sl, :] = jnp.dot(ub, wbi[...], preferred_element_type=F32)
                return 0

            lax.fori_loop(0, L // tc, inp, 0)
            ar = jnp.broadcast_to(lam_r, (SEG, SLAB_S))
            ai = jnp.broadcast_to(lam_i, (SEG, SLAB_S))
            entering, _ = _seg_scan(xr_ref, xi_ref, ar, ai, d == 1, n_rows)
            for j, (er, ei) in enumerate(entering):
                ent_ref[0, d, 2 * j] = er
                ent_ref[0, d, 2 * j + 1] = ei

            def outp(c, _):
                sl = pl.ds(pl.multiple_of(c * tc, tc), tc)
                y_ref[sl, :] += (
                    lax.dot_general(xr_ref[sl, :].astype(BF16), wcr[...], _NT, preferred_element_type=F32)
                    - lax.dot_general(xi_ref[sl, :].astype(BF16), wci[...], _NT, preferred_element_type=F32))
                return 0

            lax.fori_loop(0, L // tc, outp, 0)

    return pl.pallas_call(
        body, name="ssm_fwd", grid=(C // SLAB,),
        in_specs=[u_spec] + _ssm_param_specs() + [d_spec],
        out_specs=[u_spec, _ENT_SPEC],
        out_shape=[jax.ShapeDtypeStruct((L, C), F32),
                   jax.ShapeDtypeStruct((C // SLAB, 2, 2 * CHAINS, SEG, SLAB_S), F32)],
        scratch_shapes=_MAP_SCRATCH + [pltpu.VMEM((L, SLAB_S), F32)] * 2,
        compiler_params=_cparams(("arbitrary",)),
    )(u, *prm, dskip)


def _ssm_bwd(u, dy, ent, prm, dskip):
    L, C = u.shape
    n_rows = L // SEG
    n_slab = C // SLAB
    tc = _pick(L, (1024, 512, 256))
    u_spec = pl.BlockSpec((L, SLAB), lambda j: (0, j))
    d_spec = pl.BlockSpec((1, SLAB), lambda j: (0, j))
    pg_spec = pl.BlockSpec((1, 2, PG_ROWS, SLAB_S), lambda j: (j, 0, 0, 0))

    def body(u_ref, dy_ref, ent_ref, *rest):
        prm_refs, d_ref, du_ref, pg_ref = rest[:7], rest[7], rest[8], rest[9]
        tmp_ref, maps, acc_ref = rest[10], rest[11:15], rest[15]
        xr_ref, xi_ref, gr_ref, gi_ref = rest[16:20]
        wbr, wbi, wcr, wci = maps
        du_ref[...] = dy_ref[...] * d_ref[...]
        pg_ref[...] = jnp.zeros_like(pg_ref)
        pg_ref[0, 0, 66:67, 0:SLAB] = _colsum(dy_ref[...] * u_ref[...])
        for d in range(2):
            lam_r, lam_i = _ssm_fill_maps(d, prm_refs, tmp_ref, maps)

            def inp(c, _):
                sl = pl.ds(pl.multiple_of(c * tc, tc), tc)
                ub = u_ref[sl, :].astype(BF16)
                dyb = dy_ref[sl, :].astype(BF16)
                xr_ref[sl, :] = jnp.dot(ub, wbr[...], preferred_element_type=F32)
                xi_ref[sl, :] = jnp.dot(ub, wbi[...], preferred_element_type=F32)
                gr_ref[sl, :] = jnp.dot(dyb, wcr[...], preferred_element_type=F32)
                gi_ref[sl, :] = -jnp.dot(dyb, wci[...], preferred_element_type=F32)
                return 0

            lax.fori_loop(0, L // tc, inp, 0)
            ar = jnp.broadcast_to(lam_r, (SEG, SLAB_S))
            ai = jnp.broadcast_to(lam_i, (SEG, SLAB_S))
            entering = [(ent_ref[0, d, 2 * j], ent_ref[0, d, 2 * j + 1]) for j in range(CHAINS)]
            _seg_scan(xr_ref, xi_ref, ar, ai, d == 1, n_rows, entering=entering)

            def pole(rr, lr, li, acc, last):
                if last:
                    pr, pi = entering[0]
                else:
                    nb = _rows8(rr + 1 if d == 1 else rr - 1)
                    pr, pi = xr_ref[nb, :], xi_ref[nb, :]
                return acc[0] + lr * pr + li * pi, acc[1] + li * pr - lr * pi

            zero = jnp.zeros((SEG, SLAB_S), F32)
            _, (accr, acci) = _seg_scan(gr_ref, gi_ref, ar, -ai, d == 0, n_rows, pole, (zero, zero))
            pg_ref[0, d, 64:65, :] = _colsum(accr)
            pg_ref[0, d, 65:66, :] = _colsum(acci)

            acc_ref[...] = jnp.zeros_like(acc_ref)

            def outp(c, _):
                sl = pl.ds(pl.multiple_of(c * tc, tc), tc)
                lrb, lib = gr_ref[sl, :].astype(BF16), gi_ref[sl, :].astype(BF16)
                du_ref[sl, :] += (lax.dot_general(lrb, wbr[...], _NT, preferred_element_type=F32)
                                  + lax.dot_general(lib, wbi[...], _NT, preferred_element_type=F32))
                ut = u_ref[sl, :].T.astype(BF16)
                dyt = dy_ref[sl, :].T.astype(BF16)
                acc_ref[0] += jnp.dot(ut, lrb, preferred_element_type=F32)
                acc_ref[1] += jnp.dot(ut, lib, preferred_element_type=F32)
                acc_ref[2] += jnp.dot(dyt, xr_ref[sl, :].astype(BF16), preferred_element_type=F32)
                acc_ref[3] -= jnp.dot(dyt, xi_ref[sl, :].astype(BF16), preferred_element_type=F32)
                return 0

            lax.fori_loop(0, L // tc, outp, 0)
            for m in range(4):
                for g in range(SLAB_G):
                    lanes = slice(g * SSM_P, (g + 1) * SSM_P)
                    pg_ref[0, d, m * SSM_H:(m + 1) * SSM_H, lanes] = acc_ref[m, g * SSM_H:(g + 1) * SSM_H, lanes]

    return pl.pallas_call(
        body, name="ssm_bwd", grid=(n_slab,),
        in_specs=[u_spec, u_spec, _ENT_SPEC] + _ssm_param_specs() + [d_spec],
        out_specs=[u_spec, pg_spec],
        out_shape=[jax.ShapeDtypeStruct((L, C), F32), jax.ShapeDtypeStruct((n_slab, 2, PG_ROWS, SLAB_S), F32)],
        scratch_shapes=_MAP_SCRATCH + [pltpu.VMEM((4, SLAB, SLAB_S), F32)] + [pltpu.VMEM((L, SLAB_S), F32)] * 4,
        compiler_params=_cparams(("arbitrary",), 60 << 20),
    )(u, dy, ent, *prm, dskip)


def _ssm_param_grads(pg, prm):
    n_slab = pg.shape[0]
    G = n_slab * SLAB_G
    pg_spec = pl.BlockSpec((1, 2, PG_ROWS, SLAB_S), lambda j: (j, 0, 0, 0))
    pole, _, step, mat = _ssm_param_specs()[:4]

    def body(pg_ref, a_re_ref, a_im_ref, ldt_ref, bt_re_ref, bt_im_ref,
             dbr_ref, dbi_ref, dcr_ref, dci_ref, dar_ref, dai_ref, dldt_ref, dd_ref):
        dd_ref[...] = pg_ref[0, 0, 66:67, 0:SLAB]
        for d in range(2):
            a_r = a_re_ref[d]
            lr, li, dt, lbr, lbi, f_r, f_i = _discretise(a_r, a_im_ref[d], ldt_ref[d])
            gfr_rows, gfi_rows, glr_rows, gli_rows = [], [], [], []
            for g in range(SLAB_G):
                lanes = slice(g * SSM_P, (g + 1) * SSM_P)
                gbr, gbi = pg_ref[0, d, 0:SSM_H, lanes], pg_ref[0, d, SSM_H:2 * SSM_H, lanes]
                b_r, b_i = bt_re_ref[d, g], bt_im_ref[d, g]
                fr, fi = f_r[g:g + 1], f_i[g:g + 1]
                dbr_ref[d, g] = fr * gbr + fi * gbi
                dbi_ref[d, g] = fr * gbi - fi * gbr
                gfr_rows.append(_colsum(gbr * b_r + gbi * b_i))
                gfi_rows.append(_colsum(gbi * b_r - gbr * b_i))
                dcr_ref[d, g] = pg_ref[0, d, 2 * SSM_H:3 * SSM_H, lanes]
                dci_ref[d, g] = pg_ref[0, d, 3 * SSM_H:4 * SSM_H, lanes]
                glr_rows.append(pg_ref[0, d, 64:65, lanes])
                gli_rows.append(pg_ref[0, d, 65:66, lanes])
            gfr, gfi = jnp.concatenate(gfr_rows, axis=0), jnp.concatenate(gfi_rows, axis=0)
            glr, gli = jnp.concatenate(glr_rows, axis=0), jnp.concatenate(gli_rows, axis=0)
            den = lr * lr + li * li
            ir, ii = lr / den, -li / den
            tr, ti = _cmul(ir, -ii, gfr, gfi)
            glbr, glbi = glr + tr, gli + ti
            qr, qi = _cmul(f_r, f_i, ir, ii)
            dlr, dli = _cmul(-qr, qi, gfr, gfi)
            zr, zi = _cmul(lbr, -lbi, glbr, glbi)
            dlr = dlr + dt * zr
            dli = dli + dt * zi
            dar_ref[d] = jnp.where(a_r < -1e-4, dlr, jnp.where(a_r == -1e-4, 0.5 * dlr, 0.0))
            dai_ref[d] = dli
            dldt_ref[d] = jnp.sum(lr * zr + li * zi, axis=-1, keepdims=True) * dt

    a_re, a_im, ldt, bt_re, bt_im = prm[:5]
    mshape = jax.ShapeDtypeStruct(bt_re.shape, F32)
    pshape = jax.ShapeDtypeStruct(a_re.shape, F32)
    return pl.pallas_call(
        body, name="ssm_param_grads", grid=(n_slab,),
        in_specs=[pg_spec, pole, pole, step, mat, mat],
        out_specs=[mat, mat, mat, mat, pole, pole, step, pl.BlockSpec((1, SLAB), lambda j: (0, j))],
        out_shape=[mshape, mshape, mshape, mshape, pshape, pshape, jax.ShapeDtypeStruct(ldt.shape, F32),
                   jax.ShapeDtypeStruct((1, n_slab * SLAB), F32)],
        compiler_params=_cparams(("arbitrary",)),
    )(pg, a_re, a_im, ldt, bt_re, bt_im)


def _peer(k, x, y, c):
    return (1 - x if k & 4 else x, 1 - y if k & 2 else y, 1 - c if k & 1 else c)


def _dev_index(pos):
    return 4 * pos[0] + 2 * pos[1] + pos[2]


def _chip_index(pos):
    return 2 * pos[0] + pos[1]


def _sibling_swap(x, name):
    any_spec = pl.BlockSpec(memory_space=pl.ANY)

    def body(x_ref, out_ref, send_sems, recv_sems):
        x_, y_, c_ = lax.axis_index("x"), lax.axis_index("y"), lax.axis_index("c")
        copies = [pltpu.make_async_remote_copy(
            src_ref=x_ref.at[2 * chip + (1 - c_)], dst_ref=out_ref.at[chip], send_sem=send_sems.at[chip],
            recv_sem=recv_sems.at[chip], device_id=(x_, y_, 1 - c_), device_id_type=pl.DeviceIdType.MESH)
            for chip in range(N_CHIPS)]
        for cp in copies:
            cp.start()
        for cp in copies:
            cp.wait()

    return pl.pallas_call(
        body, name=name, out_shape=jax.ShapeDtypeStruct((N_CHIPS,) + x.shape[1:], x.dtype),
        in_specs=[any_spec], out_specs=any_spec,
        scratch_shapes=[pltpu.SemaphoreType.DMA((N_CHIPS,)), pltpu.SemaphoreType.DMA((N_CHIPS,))],
    )(x)


def _pair_sum(a, b, name):
    n, R, W = a.shape
    tr = _row_tile(R, W, 5 << 20)

    def body(a_ref, b_ref, o_ref):
        o_ref[...] = (a_ref[...].astype(F32) + b_ref[...].astype(F32)).astype(BF16)

    spec = pl.BlockSpec((1, tr, W), lambda i, j: (i, j, 0))
    return pl.pallas_call(
        body, name=name, grid=(n, R // tr), in_specs=[spec, spec], out_specs=spec,
        out_shape=jax.ShapeDtypeStruct((n, R, W), BF16), compiler_params=_cparams(("arbitrary", "arbitrary")),
    )(a, b)


def _all_gather(xs, name):
    n = len(xs)
    any_spec = pl.BlockSpec(memory_space=pl.ANY)

    def body(*refs):
        x_refs, out_refs = refs[:n], refs[n:2 * n]
        send_sems, recv_sems, local_sems = refs[2 * n:]
        x, y, c = lax.axis_index("x"), lax.axis_index("y"), lax.axis_index("c")
        me, sibling = (x, y, c), (x, y, 1 - c)
        chips = [(1 - x, y), (x, 1 - y), (1 - x, 1 - y)]

        def copy(a, k, block, to, src=None):
            dst = out_refs[a].at[_dev_index(block)]
            return pltpu.make_async_remote_copy(
                src_ref=dst if src is None else src, dst_ref=dst,
                send_sem=send_sems.at[a, k], recv_sem=recv_sems.at[a, k],
                device_id=to, device_id_type=pl.DeviceIdType.MESH)

        mine = [pltpu.make_async_copy(x_refs[a], out_refs[a].at[_dev_index(me)], local_sems.at[a]) for a in range(n)]
        for cp in mine:
            cp.start()
        first = []
        for a in range(n):
            first.append(copy(a, 0, me, sibling, src=x_refs[a]))
            first += [copy(a, 1 + j, me, (*chip, c), src=x_refs[a]) for j, chip in enumerate(chips)]
        for cp in first:
            cp.start()
        passed = []
        for j, chip in enumerate(chips):
            for a in range(n):
                copy(a, 1 + j, (*chip, c), me).wait_recv()
                fwd = copy(a, 4 + j, (*chip, c), sibling)
                fwd.start()
                passed.append(fwd)
        for a in range(n):
            copy(a, 0, sibling, me).wait_recv()
            for j, chip in enumerate(chips):
                copy(a, 4 + j, (*chip, 1 - c), me).wait_recv()
        for cp in first + passed:
            cp.wait_send()
        for cp in mine:
            cp.wait()

    return pl.pallas_call(
        body, name=name,
        out_shape=[jax.ShapeDtypeStruct((N_DEV,) + v.shape, v.dtype) for v in xs],
        in_specs=[any_spec] * n, out_specs=[any_spec] * n,
        scratch_shapes=[pltpu.SemaphoreType.DMA((n, 7)), pltpu.SemaphoreType.DMA((n, 7)),
                        pltpu.SemaphoreType.DMA((n,))],
    )(*xs)


def _sum_blocks(x, name):
    _, R, W = x.shape

    def body(x_ref, o_ref):
        acc = x_ref[0].astype(F32)
        for d in range(1, N_DEV):
            acc = acc + x_ref[d].astype(F32)
        o_ref[...] = acc

    return pl.pallas_call(body, name=name, out_shape=jax.ShapeDtypeStruct((R, W), F32),
                          compiler_params=pltpu.CompilerParams(vmem_limit_bytes=VMEM_LIMIT))(x)


def _adam_update(w, g, m, v):
    mn = ADAM_B1 * m + (1.0 - ADAM_B1) * g
    vn = ADAM_B2 * v + (1.0 - ADAM_B2) * (g * g)
    m_hat = mn / (1.0 - ADAM_B1 ** ADAM_STEP)
    v_hat = vn / (1.0 - ADAM_B2 ** ADAM_STEP)
    return -ADAM_LR * (m_hat / (jnp.sqrt(v_hat) + ADAM_EPS) + ADAM_WD * w), mn, vn


def _row_tile(R, W, budget):
    padded = -(-W // LANES) * LANES * 4
    if R * padded <= budget:
        return R
    return _pick(R, [t for t in (2048, 1024, 512, 256, 128, 64, 32, 16, 8) if t * padded <= budget])


def _adamw(w, g, m, v, name):
    R, W = w.shape
    tr = _row_tile(R, W, 1 << 20)

    def body(w_ref, g_ref, m_ref, v_ref, d_ref, mo_ref, vo_ref):
        d_ref[...], mo_ref[...], vo_ref[...] = _adam_update(w_ref[...], g_ref[...], m_ref[...], v_ref[...])

    spec = pl.BlockSpec((tr, W), lambda i: (i, 0))
    shp = jax.ShapeDtypeStruct((R, W), F32)
    return pl.pallas_call(
        body, name=name, grid=(R // tr,), in_specs=[spec] * 4, out_specs=[spec] * 3, out_shape=[shp] * 3,
        compiler_params=_cparams(("arbitrary",)),
    )(w, g, m, v)


def _adamw_reduce(w, land, m, v, name):
    R, W = w.shape
    n = land.shape[0]
    tr = _row_tile(R, W, 1 << 20)

    def body(w_ref, l_ref, m_ref, v_ref, g_ref, d_ref, mo_ref, vo_ref):
        g = l_ref[0].astype(F32)
        for d in range(1, n):
            g = g + l_ref[d].astype(F32)
        g_ref[...] = g
        d_ref[...], mo_ref[...], vo_ref[...] = _adam_update(w_ref[...], g, m_ref[...], v_ref[...])

    spec = pl.BlockSpec((tr, W), lambda i: (i, 0))
    lspec = pl.BlockSpec((n, tr, W), lambda i: (0, i, 0))
    shp = jax.ShapeDtypeStruct((R, W), F32)
    return pl.pallas_call(
        body, name=name, grid=(R // tr,), in_specs=[spec, lspec, spec, spec], out_specs=[spec] * 4,
        out_shape=[shp] * 4, compiler_params=_cparams(("arbitrary",)),
    )(w, land, m, v)


def _gelu(v):
    c = math.sqrt(2.0 / math.pi)
    return 0.5 * v * (1.0 + jnp.tanh(c * (v + 0.044715 * v * v * v)))


def _gelu_grad(v):
    c = math.sqrt(2.0 / math.pi)
    t = jnp.tanh(c * (v + 0.044715 * v * v * v))
    return 0.5 * (1.0 + t) + 0.5 * v * (1.0 - t * t) * c * (1.0 + 3.0 * 0.044715 * v * v)


def kernel(x, p, norm_mix, w_in, q_norm, k_norm, ssm_a_re, ssm_a_im, ssm_log_dt, ssm_b_re, ssm_b_im, ssm_c_re, ssm_c_im, ssm_d, w_glu, b_glu, w_out, norm_ple, w_ple_gate, w_ple_proj, norm_final, loss_target, m_norm_mix, m_w_in, m_q_norm, m_k_norm, m_ssm_a_re, m_ssm_a_im, m_ssm_log_dt, m_ssm_b_re, m_ssm_b_im, m_ssm_c_re, m_ssm_c_im, m_ssm_d, m_w_glu, m_b_glu, m_w_out, m_norm_ple, m_w_ple_gate, m_w_ple_proj, m_norm_final, v_norm_mix, v_w_in, v_q_norm, v_k_norm, v_ssm_a_re, v_ssm_a_im, v_ssm_log_dt, v_ssm_b_re, v_ssm_b_im, v_ssm_c_re, v_ssm_c_im, v_ssm_d, v_w_glu, v_b_glu, v_w_out, v_norm_ple, v_w_ple_gate, v_w_ple_proj, v_norm_final):
    L, D = x.shape[1], x.shape[2]
    D_SSM = ssm_d.shape[1]
    G = D_SSM // SSM_H
    n_slab = D_SSM // SLAB
    n_in = w_in.shape[2]
    D_IN = n_in * N_DEV
    n_pp = w_ple_proj.shape[2]
    n_glu = w_glu.shape[2]
    xs = x[0]
    ps = p[0, 0]
    tgt = loss_target[0]

    (win_t3,) = _all_gather([w_in[0].T.astype(BF16)], "gather_w_in")
    win_t = win_t3.reshape(D_IN, D)
    later_weights = _Carry("gather", [w_glu[0].astype(BF16), w_out[0].astype(BF16), w_ple_gate[0].astype(BF16),
                                      w_ple_proj[0].astype(BF16)])

    ssm_prm = (ssm_a_re[0], ssm_a_im[0], ssm_log_dt[0].reshape(2, G, 1),
               ssm_b_re[0].transpose(0, 1, 3, 2), ssm_b_im[0].transpose(0, 1, 3, 2), ssm_c_re[0], ssm_c_im[0])

    cos, sin = _rope_tables(L)
    hn = _norm_in(xs, norm_mix, "norm_mix")
    ZT = 512
    zp_tile = lambda j: jnp.where(j < 2, j, jnp.where(j < D_IN // ZT - 1, j + 1, 2))
    z = _mm(hn, win_t, "nt", "in_proj", out_dtype=BF16, n_tiles=(ZT, zp_tile))
    qr, kr, vb, kt = _qkv_prep(z, cos, sin, q_norm, k_norm)
    o, lse, (wglu3, wout3, wpg3, wpp3) = _attn_fwd(qr, kr, vb, later_weights)
    wout = wout3.reshape(-1, D)
    wpg = wpg3.reshape(-1, D)
    u_off = 2 * D_ATTN
    u_perm = _seg_perm(z[:, u_off:u_off + D_SSM]).astype(F32)
    ys_perm, ssm_ent = _ssm_fwd(u_perm, ssm_prm, ssm_d)
    ys = _seg_unperm(ys_perm)

    tm = _pick(L, (256,))

    def gelu_body(y_ref, o_ref):
        o_ref[...] = _gelu(y_ref[...]).astype(BF16)

    (gy,) = _rowcall(gelu_body, "gelu", L, tm, [(ys, _rspec(tm, D_SSM))], [(D_SSM, BF16)])
    glu = _mm(gy, wglu3, "nn", "glu_proj", out_dtype=BF16, bias=b_glu, b_blk=True)

    def mix_body(o_ref, ga_ref, gla_ref, glb_ref, gs_ref, cat_ref):
        ga, gs = _f32(ga_ref), _f32(gs_ref)
        cat_ref[:, :D_ATTN] = (o_ref[...] * ga * _sigmoid(ga)).astype(BF16)
        cat_ref[:, D_ATTN:] = (_f32(gla_ref) * _sigmoid(_f32(glb_ref)) * gs * _sigmoid(gs)).astype(BF16)

    (cat,) = _rowcall(mix_body, "mix", L, tm,
                      [(o, _rspec(tm, D_ATTN)), (z, _rspec(tm, D_ATTN, 1)), (glu, _rspec(tm, D_SSM, 0)),
                       (glu, _rspec(tm, D_SSM, 1)), (z, _rspec(tm, D_SSM, 3))], [(D_ATTN + D_SSM, BF16)])
    h1 = _mm(cat, wout, "nn", "out_proj", add=xs)
    n2 = _norm_in(h1, norm_ple, "norm_ple")
    gpre = _mm(n2, wpg, "nn", "ple_gate", out_dtype=BF16)
    pb = ps.astype(BF16)
    pp = _mm(pb, wpp3, "nn", "ple_proj", out_dtype=BF16, b_blk=True)

    nf = norm_final.reshape(1, D)

    def tail_body(h1_ref, gp_ref, pp_ref, t_ref, g_ref, dh2_ref, dpp_ref, dsg_ref, loss_ref, dg_ref):
        gate = _sigmoid(_f32(gp_ref))
        ppv = _f32(pp_ref)
        h2 = h1_ref[...] + gate * ppv
        r = _rms(h2)
        hh = h2 * r
        err = hh * g_ref[...] - t_ref[...]
        _acc(loss_ref, jnp.broadcast_to(0.5 * jnp.sum(jnp.mean(err * err, axis=-1, keepdims=True)), loss_ref.shape))
        dy = err * (1.0 / D)
        _acc(dg_ref, _colsum(dy * hh))
        dh2 = _rms_bwd(dy, hh, r, g_ref[...])
        dh2_ref[...] = dh2
        dpp_ref[...] = (dh2 * gate).astype(BF16)
        dsg_ref[...] = (dh2 * ppv * gate * (1.0 - gate)).astype(BF16)

    dh2, dpp, dsg, loss_acc, d_nf = _rowcall(
        tail_body, "tail", L, tm,
        [(h1, _rspec(tm, D)), (gpre, _rspec(tm, D)), (pp, _rspec(tm, D)), (tgt, _rspec(tm, D)), (nf, _fspec(nf.shape))],
        [(D, F32), (D, BF16), (D, BF16)], [(1, LANES), (1, D)])
    loss = lax.psum(loss_acc[0, 0], ("x", "y", "c"))

    g_wpp3 = _mm(pb, dpp, "tn", "d_ple_proj", out_dtype=BF16, out_blk=n_pp)
    g_wpg = _mm(n2, dsg, "tn", "d_ple_gate", out_dtype=BF16)
    dn2 = _mm(dsg, wpg, "nt", "d_norm_ple_in", out_dtype=BF16)

    def ple_bwd_body(h1_ref, dn_ref, dh2_ref, g_ref, dh1_ref, dh1b_ref, dg_ref):
        h1v = h1_ref[...]
        r = _rms(h1v)
        hh = h1v * r
        dn = _f32(dn_ref)
        _acc(dg_ref, _colsum(dn * hh))
        dh1 = dh2_ref[...] + _rms_bwd(dn, hh, r, g_ref[...])
        dh1_ref[...] = dh1
        dh1b_ref[...] = dh1.astype(BF16)

    dh1, dh1b, d_nple = _rowcall(
        ple_bwd_body, "ple_bwd", L, tm,
        [(h1, _rspec(tm, D)), (dn2, _rspec(tm, D)), (dh2, _rspec(tm, D)), (norm_ple, _fspec(norm_ple.shape))],
        [(D, F32), (D, BF16)], [(1, D)])

    dcat = _mm(dh1b, wout, "nt", "d_cat", out_dtype=BF16)
    g_wout = _mm(cat, dh1b, "tn", "d_out_proj", out_dtype=BF16)

    def mix_bwd_body(dca_ref, dcs_ref, o_ref, ga_ref, gla_ref, glb_ref, gs_ref,
                     do_ref, dga_ref, dgs_ref, dglu_ref, db_ref):
        dca, dcs, ga, gs = _f32(dca_ref), _f32(dcs_ref), _f32(ga_ref), _f32(gs_ref)
        sa, ss, sb = _sigmoid(ga), _sigmoid(gs), _sigmoid(_f32(glb_ref))
        gla = _f32(gla_ref)
        do_ref[...] = (dca * ga * sa).astype(BF16)
        dga_ref[...] = (dca * o_ref[...] * sa * (1.0 + ga * (1.0 - sa))).astype(BF16)
        dgs_ref[...] = (dcs * gla * sb * ss * (1.0 + gs * (1.0 - ss))).astype(BF16)
        dy2 = dcs * gs * ss
        da, db = dy2 * sb, dy2 * gla * sb * (1.0 - sb)
        dglu_ref[:, :D_SSM] = da.astype(BF16)
        dglu_ref[:, D_SSM:] = db.astype(BF16)
        _acc(db_ref, jnp.concatenate([_colsum(da), _colsum(db)], axis=-1))

    do, dga, dgs, dglu, g_bglu = _rowcall(
        mix_bwd_body, "mix_bwd", L, tm,
        [(dcat, _rspec(tm, D_ATTN, 0)), (dcat, _rspec(tm, D_SSM, 1)), (o, _rspec(tm, D_ATTN)),
         (z, _rspec(tm, D_ATTN, 1)), (glu, _rspec(tm, D_SSM, 0)), (glu, _rspec(tm, D_SSM, 1)),
         (z, _rspec(tm, D_SSM, 3))],
        [(D_ATTN, BF16), (D_ATTN, BF16), (D_SSM, BF16), (2 * D_SSM, BF16)], [(1, 2 * D_SSM)])

    g_wglu3 = _mm(gy, dglu, "tn", "d_glu_proj", out_dtype=BF16, out_blk=n_glu)
    dys = _mm(dglu, wglu3, "nt", "d_ssm_out", b_blk=True,
              post=(lambda out, y: out * _gelu_grad(y), ys))
    du_perm, pg = _ssm_bwd(u_perm, _seg_perm(dys), ssm_ent, ssm_prm, ssm_d)
    du = _seg_unperm(du_perm)

    pg_send = pg.reshape(N_DEV, (n_slab // N_DEV) * 2 * PG_ROWS, SLAB_S)
    dqs, dkr, dvv, (l_wglu, l_wout, l_wpg, l_wpp, l_pg) = _attn_bwd(
        qr, kr, vb, kt, do, o, lse,
        _Carry("a2a", [g_wglu3, g_wout.reshape(N_DEV, -1, D), g_wpg.reshape(N_DEV, -1, D), g_wpp3, pg_send]))

    scale = HEAD_DIM ** -0.5
    kblk = 4 * D_ATTN // D_KV

    a0, k0, v0, u0, s0 = D_ATTN + 2 * D_KV, D_ATTN, D_ATTN + D_KV, 2 * D_ATTN + 2 * D_KV, 2 * D_ATTN + 2 * D_KV + D_SSM

    def qkv_bwd_body(dq_ref, dk_ref, dv_ref, q_ref, k_ref, cos_ref, sin_ref, qn_ref, kn_ref, dga_ref, du_ref, dgs_ref,
                     dz_ref, dqn_ref, dkn_ref):
        c, s = cos_ref[...], sin_ref[...]
        dz_ref[:, a0:a0 + D_ATTN] = dga_ref[...]
        dz_ref[:, u0:u0 + D_SSM] = du_ref[...].astype(BF16)
        dz_ref[:, s0:s0 + D_SSM] = dgs_ref[...]

        def head(g, xh, w):
            dn = g * c + _partner(g * s)
            r = _rms(xh)
            xhat = xh * r
            return _rms_bwd(dn, xhat, r, w), _colsum(dn * xhat)

        dqn = jnp.zeros((1, HEAD_DIM), F32)
        for h in range(N_HEADS):
            sl = slice(h * HEAD_DIM, (h + 1) * HEAD_DIM)
            dx, dw = head(dq_ref[:, sl] * scale, q_ref[:, sl].astype(F32), qn_ref[...])
            dz_ref[:, sl] = dx.astype(BF16)
            dqn = dqn + dw
        dkn = jnp.zeros((1, HEAD_DIM), F32)
        for h in range(N_KV):
            sl = slice(h * HEAD_DIM, (h + 1) * HEAD_DIM)
            dx, dw = head(dk_ref[:, sl], k_ref[:, sl].astype(F32), kn_ref[...])
            dz_ref[:, k0 + h * HEAD_DIM:k0 + (h + 1) * HEAD_DIM] = dx.astype(BF16)
            dkn = dkn + dw
        dz_ref[:, v0:v0 + D_KV] = dv_ref[...].astype(BF16)
        _acc(dqn_ref, dqn)
        _acc(dkn_ref, dkn)

    dz, g_qn, g_kn = _rowcall(
        qkv_bwd_body, "qkv_bwd", L, tm,
        [(dqs, _rspec(tm, D_ATTN)), (dkr, _rspec(tm, D_KV)), (dvv, _rspec(tm, D_KV)),
         (z, _rspec(tm, D_ATTN, 0)), (z, _rspec(tm, D_KV, kblk)), (cos, _rspec(tm, HEAD_DIM)),
         (sin, _rspec(tm, HEAD_DIM)), (q_norm, _fspec(q_norm.shape)), (k_norm, _fspec(k_norm.shape)),
         (dga, _rspec(tm, D_ATTN)), (du, _rspec(tm, D_SSM)), (dgs, _rspec(tm, D_SSM))],
        [(D_IN, BF16)], [(1, HEAD_DIM), (1, HEAD_DIM)])

    g_win_t = _mm(dz, hn, "tn", "d_in_proj", out_dtype=BF16)
    g_win8 = g_win_t.reshape(N_DEV, n_in, D)
    from_sibling = _sibling_swap(g_win8, "swap_d_w_in")
    own = lax.dynamic_index_in_dim(g_win8.reshape(N_CHIPS, 2, n_in, D), lax.axis_index("c"), axis=1, keepdims=False)
    pair = _pair_sum(own, from_sibling, "pair_sum_d_w_in")
    dhn, (l_win_t,) = _mm(dz, win_t, "nn", "d_norm_mix_in", out_dtype=BF16,
                          carry=_Carry("a2a_chips", [pair]))

    def in_bwd_body(x_ref, dn_ref, dh1_ref, g_ref, dx_ref, dg_ref):
        xv = x_ref[...]
        r = _rms(xv)
        hh = xv * r
        dn = _f32(dn_ref)
        _acc(dg_ref, _colsum(dn * hh))
        dx_ref[...] = dh1_ref[...] + _rms_bwd(dn, hh, r, g_ref[...])

    grad_x, g_nmix = _rowcall(
        in_bwd_body, "in_bwd", L, tm,
        [(xs, _rspec(tm, D)), (dhn, _rspec(tm, D)), (dh1, _rspec(tm, D)), (norm_mix, _fspec(norm_mix.shape))],
        [(D, F32)], [(1, D)])

    tiny_parts = [g_nmix, g_bglu, d_nple, d_nf, g_qn, g_kn]
    tiny_flat = jnp.concatenate([t.reshape(-1) for t in tiny_parts])
    tiny_rows = -(-tiny_flat.shape[0] // (8 * LANES)) * 8
    tiny = jnp.pad(tiny_flat, (0, tiny_rows * LANES - tiny_flat.shape[0])).reshape(tiny_rows, LANES)
    pg_sum = _sum_blocks(l_pg, "sum_ssm_grads")
    pg_all, tiny_all = _all_gather([pg_sum, tiny], "gather_small_grads")
    (g_bt_re, g_bt_im, g_c_re, g_c_im, g_a_re, g_a_im, g_ldt, g_skip) = _ssm_param_grads(
        pg_all.reshape(n_slab, 2, PG_ROWS, SLAB_S), ssm_prm)
    tiny_sum = _sum_blocks(tiny_all, "sum_tiny_grads").reshape(-1)
    tiny_grads, off = [], 0
    for t in tiny_parts:
        tiny_grads.append(tiny_sum[off:off + t.size].reshape(t.shape))
        off += t.size
    r_nmix, r_bglu, r_nple, r_nf, r_qn, r_kn = tiny_grads

    grads, deltas, new_ms, new_vs = {}, {}, {}, {}
    outs = _adamw_reduce(w_in[0].T, l_win_t, m_w_in[0].T, v_w_in[0].T, "adamw_w_in")
    grads["w_in"], deltas["w_in"], new_ms["w_in"], new_vs["w_in"] = [t.T[None] for t in outs]
    big = [("w_glu", w_glu, l_wglu, m_w_glu, v_w_glu),
           ("w_out", w_out, l_wout, m_w_out, v_w_out), ("w_ple_gate", w_ple_gate, l_wpg, m_w_ple_gate, v_w_ple_gate),
           ("w_ple_proj", w_ple_proj, l_wpp, m_w_ple_proj, v_w_ple_proj)]
    for name, w, ld, m, v in big:
        shp = w.shape
        outs = _adamw_reduce(w[0], ld, m[0], v[0], "adamw_" + name)
        grads[name], deltas[name], new_ms[name], new_vs[name] = [t.reshape(shp) for t in outs]
    bt2 = (2 * G * SSM_H, SSM_P)
    for name, w, g, m, v in (("ssm_b_re", ssm_b_re, g_bt_re, m_ssm_b_re, v_ssm_b_re),
                             ("ssm_b_im", ssm_b_im, g_bt_im, m_ssm_b_im, v_ssm_b_im)):
        to2 = lambda t: t[0].transpose(0, 1, 3, 2).reshape(bt2)
        back = lambda t: t.reshape(2, G, SSM_H, SSM_P).transpose(0, 1, 3, 2)[None]
        outs = _adamw(to2(w), g.reshape(bt2), to2(m), to2(v), "adamw_" + name)
        grads[name] = back(g)
        deltas[name], new_ms[name], new_vs[name] = [back(t) for t in outs]
    small = [("norm_mix", norm_mix, r_nmix, m_norm_mix, v_norm_mix, (1, D)),
             ("q_norm", q_norm, r_qn, m_q_norm, v_q_norm, (1, HEAD_DIM)),
             ("k_norm", k_norm, r_kn, m_k_norm, v_k_norm, (1, HEAD_DIM)),
             ("ssm_a_re", ssm_a_re, g_a_re, m_ssm_a_re, v_ssm_a_re, (2 * G, SSM_P)),
             ("ssm_a_im", ssm_a_im, g_a_im, m_ssm_a_im, v_ssm_a_im, (2 * G, SSM_P)),
             ("ssm_log_dt", ssm_log_dt, g_ldt, m_ssm_log_dt, v_ssm_log_dt, (2, G)),
             ("ssm_c_re", ssm_c_re, g_c_re, m_ssm_c_re, v_ssm_c_re, (2 * G * SSM_H, SSM_P)),
             ("ssm_c_im", ssm_c_im, g_c_im, m_ssm_c_im, v_ssm_c_im, (2 * G * SSM_H, SSM_P)),
             ("ssm_d", ssm_d, g_skip, m_ssm_d, v_ssm_d, (1, D_SSM)),
             ("b_glu", b_glu, r_bglu, m_b_glu, v_b_glu, (1, 2 * D_SSM)),
             ("norm_ple", norm_ple, r_nple, m_norm_ple, v_norm_ple, (1, D)),
             ("norm_final", norm_final, r_nf, m_norm_final, v_norm_final, (1, D))]
    for name, w, g, m, v, s2 in small:
        shp = w.shape
        outs = _adamw(w.reshape(s2), g.reshape(s2), m.reshape(s2), v.reshape(s2), "adamw_" + name)
        grads[name] = g.reshape(shp)
        deltas[name], new_ms[name], new_vs[name] = [t.reshape(shp) for t in outs]

    order = ["norm_mix", "w_in", "q_norm", "k_norm", "ssm_a_re", "ssm_a_im", "ssm_log_dt", "ssm_b_re", "ssm_b_im",
             "ssm_c_re", "ssm_c_im", "ssm_d", "w_glu", "b_glu", "w_out", "norm_ple", "w_ple_gate", "w_ple_proj",
             "norm_final"]
    return (loss, grad_x[None], *[grads[k] for k in order], *[deltas[k] for k in order],
            *[new_ms[k] for k in order], *[new_vs[k] for k in order])
```

```python
import functools
import math

import numpy as np
import jax
import jax.numpy as jnp
from jax import lax
from jax.experimental import pallas as pl
from jax.experimental.pallas import tpu as pltpu

F32 = jnp.float32
BF16 = jnp.bfloat16

N_DEV = 8
N_CHIPS = 4
EPS = 1e-6
GRID_W = 64
ROPE_THETA = 10000.0
HEAD_DIM = 128
N_HEADS = 8
N_KV = 2
REP = N_HEADS // N_KV
D_ATTN = N_HEADS * HEAD_DIM
D_KV = N_KV * HEAD_DIM
SSM_H = 16
SSM_P = 64
SLAB = 128
SLAB_G = SLAB // SSM_H
SLAB_S = SLAB_G * SSM_P
SEG = 8
CHAINS = 2
SCAN_UNROLL = 4
LANES = 128
PG_ROWS = 72
VMEM_LIMIT = 48 << 20

ADAM_LR = 0.001
ADAM_B1 = 0.9
ADAM_B2 = 0.999
ADAM_EPS = 1e-08
ADAM_WD = 0.01
ADAM_STEP = 10


def _pick(n, cands):
    for c in cands:
        if n % c == 0:
            return c
    return n


def _cparams(sem, vmem=VMEM_LIMIT):
    return pltpu.CompilerParams(dimension_semantics=sem, vmem_limit_bytes=vmem)


class _Carry:
    def __init__(self, kind, xs):
        self.kind, self.xs, self.n = kind, list(xs), len(xs)
        self.ks = (2, 4, 6) if kind == "a2a_chips" else tuple(range(1, N_DEV))
        self.index = _chip_index if kind == "a2a_chips" else _dev_index
        lead = (N_DEV,) if kind == "gather" else ()
        self.out_shape = [jax.ShapeDtypeStruct(lead + v.shape, v.dtype) for v in xs]
        self.specs = [pl.BlockSpec(memory_space=pl.ANY)] * self.n
        self.scratch = [pltpu.SemaphoreType.DMA((self.n, len(self.ks))), pltpu.SemaphoreType.DMA((self.n, len(self.ks))),
                        pltpu.SemaphoreType.DMA((self.n,))]

    def _copies(self, x_refs, out_refs, sems):
        send_sems, recv_sems, local_sems = sems
        x, y, c = lax.axis_index("x"), lax.axis_index("y"), lax.axis_index("c")
        me = self.index((x, y, c))
        mine, sends, arrivals = [], [], []
        for a in range(self.n):
            src_mine = x_refs[a] if self.kind == "gather" else x_refs[a].at[me]
            mine.append(pltpu.make_async_copy(src_mine, out_refs[a].at[me], local_sems.at[a]))
            for s, k in enumerate(self.ks):
                peer = _peer(k, x, y, c)
                src = x_refs[a] if self.kind == "gather" else x_refs[a].at[self.index(peer)]
                sends.append(pltpu.make_async_remote_copy(
                    src_ref=src, dst_ref=out_refs[a].at[me], send_sem=send_sems.at[a, s],
                    recv_sem=recv_sems.at[a, s], device_id=peer, device_id_type=pl.DeviceIdType.MESH))
                land = out_refs[a].at[self.index(peer)]
                arrivals.append(pltpu.make_async_remote_copy(
                    src_ref=land, dst_ref=land, send_sem=send_sems.at[a, s],
                    recv_sem=recv_sems.at[a, s], device_id=peer, device_id_type=pl.DeviceIdType.MESH))
        return mine, sends, arrivals

    def start(self, x_refs, out_refs, sems):
        mine, sends, _ = self._copies(x_refs, out_refs, sems)
        for cp in mine + sends:
            cp.start()

    def wait(self, x_refs, out_refs, sems):
        mine, sends, arrivals = self._copies(x_refs, out_refs, sems)
        for cp in arrivals:
            cp.wait_recv()
        for cp in sends:
            cp.wait_send()
        for cp in mine:
            cp.wait()


def _grid_edges(grid):
    first = functools.reduce(lambda p, q: p & q, [pl.program_id(d) == 0 for d in range(len(grid))])
    last = functools.reduce(lambda p, q: p & q, [pl.program_id(d) == g - 1 for d, g in enumerate(grid)])
    return first, last


def _mm(a, b, mode, name, out_dtype=F32, add=None, bias=None, a_blk=False, b_blk=False, out_blk=0, carry=None,
        n_tiles=None, post=None):
    w = b.shape[2] if b_blk else out_blk
    if mode == "nn":
        M, K = a.shape
        N = b.shape[0] * w if b_blk else b.shape[1]
    elif mode == "nt":
        M = a.shape[1] if a_blk else a.shape[0]
        N = b.shape[1] if b_blk else b.shape[0]
        K = b.shape[0] * w if b_blk else b.shape[1]
    else:
        K, M = a.shape
        N = b.shape[0] * w if b_blk else b.shape[1]
    tm = _pick(M, (1024, 768, 512, 256))
    tn = _pick(N, (1024, 768, 512, 256))
    if mode == "tn" and N <= 2048:
        tn = N
    tk = K if (mode != "tn" and K <= 2048) else _pick(K, (1024, 768, 512, 256))
    if mode == "nn" and K > 2048 and N <= 2048:
        tn, tk = N, _pick(K, (1536, 1024, 768, 512, 256))
    perm = lambda j: j
    if n_tiles:
        tn, perm = n_tiles
        tm = _pick(M, (2048, 1024, 512, 256))
    if mode == "nt" and b_blk:
        tk = w
    elif b_blk or out_blk:
        tn = w
    nk = K // tk
    grid = (M // tm, N // tn, nk)
    if mode == "nn":
        a_spec = pl.BlockSpec((tm, tk), lambda i, j, k: (i, k))
        b_spec = (pl.BlockSpec((1, tk, tn), lambda i, j, k: (j, k, 0)) if b_blk
                  else pl.BlockSpec((tk, tn), lambda i, j, k: (k, j)))
        dims = (((1,), (0,)), ((), ()))
    elif mode == "nt":
        a_spec = (pl.BlockSpec((1, tm, tk), lambda i, j, k: (k, i, 0)) if a_blk
                  else pl.BlockSpec((tm, tk), lambda i, j, k: (i, k)))
        b_spec = (pl.BlockSpec((1, tn, tk), lambda i, j, k: (k, j, 0)) if b_blk
                  else pl.BlockSpec((tn, tk), lambda i, j, k: (perm(j), k)))
        dims = (((1,), (1,)), ((), ()))
    else:
        a_spec = pl.BlockSpec((tk, tm), lambda i, j, k: (k, i))
        b_spec = (pl.BlockSpec((1, tk, tn), lambda i, j, k: (j, k, 0)) if b_blk
                  else pl.BlockSpec((tk, tn), lambda i, j, k: (k, j)))
        dims = (((0,), (0,)), ((), ()))
    if out_blk:
        out_spec = pl.BlockSpec((1, tm, tn), lambda i, j, k: (j, i, 0))
        out_shape = jax.ShapeDtypeStruct((N // tn, M, tn), out_dtype)
    else:
        out_spec = pl.BlockSpec((tm, tn), lambda i, j, k: (i, j))
        out_shape = jax.ShapeDtypeStruct((M, N), out_dtype)
    extras, extra_specs, combine = [], [], []
    if add is not None:
        extras.append(add)
        extra_specs.append(pl.BlockSpec((tm, tn), lambda i, j, k: (i, j)))
        combine.append(lambda out, t: out + t)
    if bias is not None:
        extras.append(bias)
        extra_specs.append(pl.BlockSpec((1, tn), lambda i, j, k: (0, j)))
        combine.append(lambda out, t: out + t)
    if post is not None:
        extras.append(post[1])
        extra_specs.append(pl.BlockSpec((tm, tn), lambda i, j, k: (i, j)))
        combine.append(post[0])

    n_ex = len(extras)
    nc = carry.n if carry else 0

    def body(a_ref, b_ref, *rest):
        ex_refs, cx = rest[:n_ex], rest[n_ex:n_ex + nc]
        o_ref, cout = rest[n_ex + nc], rest[n_ex + nc + 1:n_ex + 2 * nc + 1]
        tail = rest[n_ex + 2 * nc + 1:]
        sems = tail[:3] if carry else ()
        first, last = _grid_edges(grid)
        if carry:
            @pl.when(first)
            def _():
                carry.start(cx, cout, sems)

        def product():
            av = a_ref[0] if a_blk else a_ref[...]
            bv = b_ref[0] if b_blk else b_ref[...]
            return lax.dot_general(av, bv, dims, preferred_element_type=F32)

        def finish(out):
            for r, fn in zip(ex_refs, combine):
                out = fn(out, r[...])
            if out_blk:
                o_ref[0] = out.astype(out_dtype)
            else:
                o_ref[...] = out.astype(out_dtype)

        if nk == 1:
            finish(product())
        else:
            acc_ref = tail[-1]
            k = pl.program_id(2)

            @pl.when(k == 0)
            def _():
                acc_ref[...] = jnp.zeros_like(acc_ref)

            acc_ref[...] += product()

            @pl.when(k == nk - 1)
            def _():
                finish(acc_ref[...])

        if carry:
            @pl.when(last)
            def _():
                carry.wait(cx, cout, sems)

    scratch = (carry.scratch if carry else []) + ([pltpu.VMEM((tm, tn), F32)] if nk > 1 else [])
    outs = pl.pallas_call(
        body, name=name, grid=grid,
        in_specs=[a_spec, b_spec] + extra_specs + (carry.specs if carry else []),
        out_specs=[out_spec] + (carry.specs if carry else []),
        out_shape=[out_shape] + (carry.out_shape if carry else []),
        scratch_shapes=scratch,
        compiler_params=_cparams(("arbitrary", "arbitrary", "arbitrary")),
    )(a, b, *extras, *(carry.xs if carry else []))
    return (outs[0], outs[1:]) if carry else outs[0]


def _rspec(tm, w, cb=0):
    return pl.BlockSpec((tm, w), lambda i: (i, cb))


def _fspec(shape):
    nd = len(shape)
    return pl.BlockSpec(shape, lambda i: (0,) * nd)


def _rowcall(body, name, L, tm, ins, row_outs, acc_outs=()):
    out_shape = [jax.ShapeDtypeStruct((L, w), dt) for w, dt in row_outs]
    out_shape += [jax.ShapeDtypeStruct(s, F32) for s in acc_outs]
    out_specs = [_rspec(tm, w) for w, _ in row_outs] + [_fspec(s) for s in acc_outs]
    return pl.pallas_call(
        body, name=name, grid=(L // tm,),
        in_specs=[s for _, s in ins], out_specs=out_specs, out_shape=out_shape,
        compiler_params=_cparams(("arbitrary",)),
    )(*[a for a, _ in ins])


def _acc(ref, val):
    @pl.when(pl.program_id(0) == 0)
    def _():
        ref[...] = jnp.zeros_like(ref)
    ref[...] += val


def _colsum(v):
    return jnp.sum(v, axis=0, keepdims=True)


def _rms(xv):
    return lax.rsqrt(jnp.mean(xv * xv, axis=-1, keepdims=True) + EPS)


def _rms_bwd(dn, xhat, r, g):
    dng = dn * g
    return r * (dng - xhat * jnp.mean(dng * xhat, axis=-1, keepdims=True))


def _sigmoid(v):
    return jax.nn.sigmoid(v)


def _f32(ref):
    return ref[...].astype(F32)


def _partner(v):
    w = v.shape[-1]
    lane = lax.broadcasted_iota(jnp.int32, v.shape, v.ndim - 1)
    first_half = (lane % 64) < 32
    return jnp.where(first_half, pltpu.roll(v, w - 32, axis=v.ndim - 1), pltpu.roll(v, 32, axis=v.ndim - 1))


def _norm_in(x, g, name):
    L, D = x.shape
    tm = _pick(L, (512, 256))

    def body(x_ref, g_ref, o_ref):
        xv = x_ref[...]
        o_ref[...] = (xv * _rms(xv) * g_ref[...]).astype(BF16)

    return _rowcall(body, name, L, tm, [(x, _rspec(tm, D)), (g, _fspec(g.shape))], [(D, BF16)])[0]


def _rope_tables(L):
    t = np.arange(L)
    rows = (t // GRID_W).astype(np.float32)
    cols = (t % GRID_W).astype(np.float32)
    n_freq = HEAD_DIM // 4
    inv_freq = np.float32(ROPE_THETA) ** (-np.arange(n_freq, dtype=np.float32) / np.float32(n_freq))
    ar = (rows[:, None] * inv_freq[None, :]).astype(np.float32).astype(np.float64)
    ac = (cols[:, None] * inv_freq[None, :]).astype(np.float32).astype(np.float64)
    cos = np.concatenate([np.cos(ar), np.cos(ar), np.cos(ac), np.cos(ac)], axis=-1).astype(np.float32)
    sin = np.concatenate([-np.sin(ar), np.sin(ar), -np.sin(ac), np.sin(ac)], axis=-1).astype(np.float32)
    return jnp.asarray(cos), jnp.asarray(sin)


def _qkv_prep(z, cos, sin, qn, kn):
    L = z.shape[0]
    tm = _pick(L, (512, 256))
    scale = HEAD_DIM ** -0.5
    kblk = 4 * D_ATTN // D_KV

    def body(q_ref, k_ref, v_ref, cos_ref, sin_ref, qn_ref, kn_ref, qo_ref, ko_ref, vo_ref, kt_ref):
        c, s = cos_ref[...], sin_ref[...]

        def head(xh, w):
            n = xh * _rms(xh) * w
            return n * c + _partner(n) * s

        for h in range(N_HEADS):
            sl = slice(h * HEAD_DIM, (h + 1) * HEAD_DIM)
            qo_ref[:, sl] = (head(q_ref[:, sl].astype(F32), qn_ref[...]) * scale).astype(BF16)
        for h in range(N_KV):
            sl = slice(h * HEAD_DIM, (h + 1) * HEAD_DIM)
            kr = head(k_ref[:, sl].astype(F32), kn_ref[...])
            ko_ref[:, sl] = kr.astype(BF16)
            kt_ref[sl, :] = kr.T.astype(BF16)
        vo_ref[...] = v_ref[...].astype(BF16)

    return pl.pallas_call(
        body, name="qkv_prep", grid=(L // tm,),
        in_specs=[_rspec(tm, D_ATTN, 0), _rspec(tm, D_KV, kblk), _rspec(tm, D_KV, kblk + 1),
                  _rspec(tm, HEAD_DIM), _rspec(tm, HEAD_DIM), _fspec(qn.shape), _fspec(kn.shape)],
        out_specs=[_rspec(tm, D_ATTN), _rspec(tm, D_KV), _rspec(tm, D_KV),
                   pl.BlockSpec((D_KV, tm), lambda i: (0, i))],
        out_shape=[jax.ShapeDtypeStruct((L, D_ATTN), BF16), jax.ShapeDtypeStruct((L, D_KV), BF16),
                   jax.ShapeDtypeStruct((L, D_KV), BF16), jax.ShapeDtypeStruct((D_KV, L), BF16)],
        compiler_params=_cparams(("arbitrary",)),
    )(z, z, z, cos, sin, qn, kn)


def _col_to_row(col):
    n = col.shape[0]
    eye = lax.broadcasted_iota(jnp.int32, (n, n), 0) == lax.broadcasted_iota(jnp.int32, (n, n), 1)
    return jnp.sum(jnp.where(eye, col, 0.0), axis=0, keepdims=True)


def _attn_fwd(q, k, v, carry=None):
    L = q.shape[0]
    tq = _pick(L, (256, 128))
    grid = (N_HEADS, L // tq)
    nc = carry.n if carry else 0

    def body(q_ref, k_ref, v_ref, *rest):
        cx, (o_ref, lse_ref) = rest[:nc], rest[nc:nc + 2]
        cout, sems = rest[nc + 2:2 * nc + 2], rest[2 * nc + 2:]
        first, last = _grid_edges(grid)
        if carry:
            @pl.when(first)
            def _():
                carry.start(cx, cout, sems)

        s = lax.dot_general(q_ref[...], k_ref[...], (((1,), (1,)), ((), ())), preferred_element_type=F32)
        m = jnp.max(s, axis=-1, keepdims=True)
        e = jnp.exp(s - m)
        l = jnp.sum(e, axis=-1, keepdims=True)
        o_ref[...] = jnp.dot(e.astype(BF16), v_ref[...], preferred_element_type=F32) / l
        lse_ref[0] = _col_to_row(m + jnp.log(l))

        if carry:
            @pl.when(last)
            def _():
                carry.wait(cx, cout, sems)

    outs = pl.pallas_call(
        body, name="attn_fwd", grid=grid,
        in_specs=[pl.BlockSpec((tq, HEAD_DIM), lambda h, i: (i, h)),
                  pl.BlockSpec((L, HEAD_DIM), lambda h, i: (0, h // REP)),
                  pl.BlockSpec((L, HEAD_DIM), lambda h, i: (0, h // REP))] + (carry.specs if carry else []),
        out_specs=[pl.BlockSpec((tq, HEAD_DIM), lambda h, i: (i, h)),
                   pl.BlockSpec((1, 1, tq), lambda h, i: (h, 0, i))] + (carry.specs if carry else []),
        out_shape=[jax.ShapeDtypeStruct((L, D_ATTN), F32), jax.ShapeDtypeStruct((N_HEADS, 1, L), F32)]
        + (carry.out_shape if carry else []),
        scratch_shapes=carry.scratch if carry else [],
        compiler_params=_cparams(("arbitrary", "arbitrary")),
    )(q, k, v, *(carry.xs if carry else []))
    return outs[0], outs[1], outs[2:]


def _attn_bwd(q, k, v, kt, do, o, lse, carry=None):
    L = q.shape[0]
    tq = _pick(L, (256, 128))
    kc = _pick(L, (512, 256, 128))
    nt = (((1,), (1,)), ((), ()))
    grid = (N_KV, REP, L // tq)
    nc = carry.n if carry else 0

    def body(q_ref, do_ref, o_ref, lse_ref, k_ref, v_ref, kt_ref, *rest):
        cx, (dq_ref, dk_ref, dv_ref) = rest[:nc], rest[nc:nc + 3]
        cout, sems = rest[nc + 3:2 * nc + 3], rest[2 * nc + 3:]
        first, last = _grid_edges(grid)
        if carry:
            @pl.when(first)
            def _():
                carry.start(cx, cout, sems)

        @pl.when((pl.program_id(1) == 0) & (pl.program_id(2) == 0))
        def _():
            dk_ref[...] = jnp.zeros_like(dk_ref)
            dv_ref[...] = jnp.zeros_like(dv_ref)

        qv, dov = q_ref[...], do_ref[...]
        lse_row = lse_ref[0]
        delta = _col_to_row(jnp.sum(dov.astype(F32) * o_ref[...], axis=-1, keepdims=True))
        dqt = jnp.zeros((HEAD_DIM, tq), F32)
        for c in range(L // kc):
            sl = slice(c * kc, (c + 1) * kc)
            st = lax.dot_general(k_ref[sl, :], qv, nt, preferred_element_type=F32)
            pt = jnp.exp(st - lse_row)
            dpt = lax.dot_general(v_ref[sl, :], dov, nt, preferred_element_type=F32)
            dst = (pt * (dpt - delta)).astype(BF16)
            dv_ref[sl, :] += jnp.dot(pt.astype(BF16), dov, preferred_element_type=F32)
            dk_ref[sl, :] += jnp.dot(dst, qv, preferred_element_type=F32)
            dqt = dqt + jnp.dot(kt_ref[:, sl], dst, preferred_element_type=F32)
        dq_ref[...] = dqt.T

        if carry:
            @pl.when(last)
            def _():
                carry.wait(cx, cout, sems)

    head = lambda g, r, i: (i, g * REP + r)
    outs = pl.pallas_call(
        body, name="attn_bwd", grid=grid,
        in_specs=[pl.BlockSpec((tq, HEAD_DIM), head), pl.BlockSpec((tq, HEAD_DIM), head),
                  pl.BlockSpec((tq, HEAD_DIM), head),
                  pl.BlockSpec((1, 1, tq), lambda g, r, i: (g * REP + r, 0, i)),
                  pl.BlockSpec((L, HEAD_DIM), lambda g, r, i: (0, g)),
                  pl.BlockSpec((L, HEAD_DIM), lambda g, r, i: (0, g)),
                  pl.BlockSpec((HEAD_DIM, L), lambda g, r, i: (g, 0))] + (carry.specs if carry else []),
        out_specs=[pl.BlockSpec((tq, HEAD_DIM), head),
                   pl.BlockSpec((L, HEAD_DIM), lambda g, r, i: (0, g)),
                   pl.BlockSpec((L, HEAD_DIM), lambda g, r, i: (0, g))] + (carry.specs if carry else []),
        out_shape=[jax.ShapeDtypeStruct((L, D_ATTN), F32), jax.ShapeDtypeStruct((L, D_KV), F32),
                   jax.ShapeDtypeStruct((L, D_KV), F32)] + (carry.out_shape if carry else []),
        scratch_shapes=carry.scratch if carry else [],
        compiler_params=_cparams(("arbitrary", "arbitrary", "arbitrary")),
    )(q, do, o, lse, k, v, kt, *(carry.xs if carry else []))
    return outs[0], outs[1], outs[2], outs[3:]


def _seg_perm(a):
    L, C = a.shape
    return a.reshape(SEG, L // SEG, C).transpose(1, 0, 2).reshape(L, C)


def _seg_unperm(a):
    L, C = a.shape
    return a.reshape(L // SEG, SEG, C).transpose(1, 0, 2).reshape(L, C)


def _cmul(ar, ai, br, bi):
    return ar * br - ai * bi, ar * bi + ai * br


def _rows8(rr):
    if isinstance(rr, int):
        return pl.ds(rr * SEG, SEG)
    return pl.ds(pl.multiple_of(rr * SEG, SEG), SEG)


def _seg_scan(xr_ref, xi_ref, ar, ai, reverse, n_rows, visit=None, visit_init=(), entering=None):
    shape = ar.shape
    zero = jnp.zeros(shape, F32)
    rc = n_rows // CHAINS

    def index(q):
        return (n_rows - 1 - q) if reverse else q

    if entering is None:
        def ends(q, carry):
            out = []
            for j in range(CHAINS):
                sl = _rows8(index(j * rc + q))
                pr, pi = _cmul(ar, ai, carry[2 * j], carry[2 * j + 1])
                out += [pr + xr_ref[sl, :], pi + xi_ref[sl, :]]
            return tuple(out)

        def ends_block(qb, carry):
            for t in range(SCAN_UNROLL):
                carry = ends(qb * SCAN_UNROLL + t, carry)
            return carry

        e = lax.fori_loop(0, rc // SCAN_UNROLL, ends_block, (zero,) * (2 * CHAINS))

        pr, pi = ar, ai
        for _ in range(int(math.log2(rc))):
            pr, pi = _cmul(pr, pi, pr, pi)
        sub = lax.broadcasted_iota(jnp.int32, shape, 0)
        shift = (SEG - 1) if reverse else 1
        edge = (SEG - 1) if reverse else 0
        entering = [(zero, zero)] * CHAINS
        for _ in range(SEG):
            tr, ti = _cmul(pr, pi, *entering[CHAINS - 1])
            cur = (jnp.where(sub == edge, 0.0, pltpu.roll(tr + e[2 * CHAINS - 2], shift, axis=0)),
                   jnp.where(sub == edge, 0.0, pltpu.roll(ti + e[2 * CHAINS - 1], shift, axis=0)))
            entering = [cur]
            for j in range(1, CHAINS):
                tr, ti = _cmul(pr, pi, *cur)
                cur = (tr + e[2 * j - 2], ti + e[2 * j - 1])
                entering.append(cur)

    def step(q, carry, last):
        out, acc = [], carry[2 * CHAINS:]
        for j in range(CHAINS):
            rr = index(j * rc + q)
            sl = _rows8(rr)
            pr, pi = _cmul(ar, ai, carry[2 * j], carry[2 * j + 1])
            nr, ni = pr + xr_ref[sl, :], pi + xi_ref[sl, :]
            xr_ref[sl, :] = nr
            xi_ref[sl, :] = ni
            if visit:
                acc = visit(rr, nr, ni, acc, last and j == CHAINS - 1)
            out += [nr, ni]
        return (*out, *acc)

    def step_block(qb, carry):
        for t in range(SCAN_UNROLL):
            carry = step(qb * SCAN_UNROLL + t, carry, False)
        return carry

    start = tuple(v for pair in entering for v in pair)
    n_blocks = (rc - 1) // SCAN_UNROLL
    carry = lax.fori_loop(0, n_blocks, step_block, (*start, *visit_init))
    for q in range(n_blocks * SCAN_UNROLL, rc - 1):
        carry = step(q, carry, False)
    carry = step(rc - 1, carry, True)
    return entering, carry[2 * CHAINS:]


def _discretise(a_re, a_im, ldt):
    lr = jnp.minimum(a_re, -1e-4)
    li = a_im
    dt = jnp.exp(ldt)
    mag = jnp.exp(lr * dt)
    lbr = mag * jnp.cos(li * dt)
    lbi = mag * jnp.sin(li * dt)
    den = lr * lr + li * li
    nr = lbr - 1.0
    fr = (nr * lr + lbi * li) / den
    fi = (lbi * lr - nr * li) / den
    return lr, li, dt, lbr, lbi, fr, fi


def _lane_row(v):
    return jnp.concatenate([v[g:g + 1, :] for g in range(v.shape[0])], axis=1)


def _ssm_fill_maps(d, prm, tmp_ref, maps):
    a_re_ref, a_im_ref, ldt_ref, bt_re_ref, bt_im_ref, c_re_ref, c_im_ref = prm
    _, _, _, lbr, lbi, fr, fi = _discretise(a_re_ref[d], a_im_ref[d], ldt_ref[d])

    def fill(dst, piece):
        tmp_ref[...] = jnp.zeros_like(tmp_ref)
        for g in range(SLAB_G):
            tmp_ref[g * SSM_H:(g + 1) * SSM_H, g * SSM_P:(g + 1) * SSM_P] = piece(g)
        dst[...] = tmp_ref[...].astype(BF16)

    wbr, wbi, wcr, wci = maps
    fill(wbr, lambda g: fr[g:g + 1] * bt_re_ref[d, g] - fi[g:g + 1] * bt_im_ref[d, g])
    fill(wbi, lambda g: fr[g:g + 1] * bt_im_ref[d, g] + fi[g:g + 1] * bt_re_ref[d, g])
    fill(wcr, lambda g: c_re_ref[d, g])
    fill(wci, lambda g: c_im_ref[d, g])
    return _lane_row(lbr), _lane_row(lbi)


def _ssm_param_specs():
    pole = pl.BlockSpec((2, SLAB_G, SSM_P), lambda j: (0, j, 0))
    step = pl.BlockSpec((2, SLAB_G, 1), lambda j: (0, j, 0))
    mat = pl.BlockSpec((2, SLAB_G, SSM_H, SSM_P), lambda j: (0, j, 0, 0))
    return [pole, pole, step, mat, mat, mat, mat]


_MAP_SCRATCH = [pltpu.VMEM((SLAB, SLAB_S), F32)] + [pltpu.VMEM((SLAB, SLAB_S), BF16)] * 4
_ENT_SPEC = pl.BlockSpec((1, 2, 2 * CHAINS, SEG, SLAB_S), lambda j: (j, 0, 0, 0, 0))
_NT = (((1,), (1,)), ((), ()))


def _ssm_fwd(u, prm, dskip):
    L, C = u.shape
    n_rows = L // SEG
    tc = _pick(L, (1024, 512, 256))
    u_spec = pl.BlockSpec((L, SLAB), lambda j: (0, j))
    d_spec = pl.BlockSpec((1, SLAB), lambda j: (0, j))

    def body(u_ref, *rest):
        prm_refs, d_ref, y_ref, ent_ref = rest[:7], rest[7], rest[8], rest[9]
        tmp_ref, maps, xr_ref, xi_ref = rest[10], rest[11:15], rest[15], rest[16]
        wbr, wbi, wcr, wci = maps
        y_ref[...] = u_ref[...] * d_ref[...]
        for d in range(2):
            lam_r, lam_i = _ssm_fill_maps(d, prm_refs, tmp_ref, maps)

            def inp(c, _):
                sl = pl.ds(pl.multiple_of(c * tc, tc), tc)
                ub = u_ref[sl, :].astype(BF16)
                xr_ref[sl, :] = jnp.dot(ub, wbr[...], preferred_element_type=F32)
                xi_ref[sl, :] = jnp.dot(ub, wbi[...], preferred_element_type=F32)
                return 0

            lax.fori_loop(0, L // tc, inp, 0)
            ar = jnp.broadcast_to(lam_r, (SEG, SLAB_S))
            ai = jnp.broadcast_to(lam_i, (SEG, SLAB_S))
            entering, _ = _seg_scan(xr_ref, xi_ref, ar, ai, d == 1, n_rows)
            for j, (er, ei) in enumerate(entering):
                ent_ref[0, d, 2 * j] = er
                ent_ref[0, d, 2 * j + 1] = ei

            def outp(c, _):
                sl = pl.ds(pl.multiple_of(c * tc, tc), tc)
                y_ref[sl, :] += (
                    lax.dot_general(xr_ref[sl, :].astype(BF16), wcr[...], _NT, preferred_element_type=F32)
                    - lax.dot_general(xi_ref[sl, :].astype(BF16), wci[...], _NT, preferred_element_type=F32))
                return 0

            lax.fori_loop(0, L // tc, outp, 0)

    return pl.pallas_call(
        body, name="ssm_fwd", grid=(C // SLAB,),
        in_specs=[u_spec] + _ssm_param_specs() + [d_spec],
        out_specs=[u_spec, _ENT_SPEC],
        out_shape=[jax.ShapeDtypeStruct((L, C), F32),
                   jax.ShapeDtypeStruct((C // SLAB, 2, 2 * CHAINS, SEG, SLAB_S), F32)],
        scratch_shapes=_MAP_SCRATCH + [pltpu.VMEM((L, SLAB_S), F32)] * 2,
        compiler_params=_cparams(("arbitrary",)),
    )(u, *prm, dskip)


def _ssm_bwd(u, dy, ent, prm, dskip):
    L, C = u.shape
    n_rows = L // SEG
    n_slab = C // SLAB
    tc = _pick(L, (1024, 512, 256))
    u_spec = pl.BlockSpec((L, SLAB), lambda j: (0, j))
    d_spec = pl.BlockSpec((1, SLAB), lambda j: (0, j))
    pg_spec = pl.BlockSpec((1, 2, PG_ROWS, SLAB_S), lambda j: (j, 0, 0, 0))

    def body(u_ref, dy_ref, ent_ref, *rest):
        prm_refs, d_ref, du_ref, pg_ref = rest[:7], rest[7], rest[8], rest[9]
        tmp_ref, maps, acc_ref = rest[10], rest[11:15], rest[15]
        xr_ref, xi_ref, gr_ref, gi_ref = rest[16:20]
        wbr, wbi, wcr, wci = maps
        du_ref[...] = dy_ref[...] * d_ref[...]
        pg_ref[...] = jnp.zeros_like(pg_ref)
        pg_ref[0, 0, 66:67, 0:SLAB] = _colsum(dy_ref[...] * u_ref[...])
        for d in range(2):
            lam_r, lam_i = _ssm_fill_maps(d, prm_refs, tmp_ref, maps)

            def inp(c, _):
                sl = pl.ds(pl.multiple_of(c * tc, tc), tc)
                ub = u_ref[sl, :].astype(BF16)
                dyb = dy_ref[sl, :].astype(BF16)
                xr_ref[sl, :] = jnp.dot(ub, wbr[...], preferred_element_type=F32)
                xi_ref[sl, :] = jnp.dot(ub, wbi[...], preferred_element_type=F32)
                gr_ref[sl, :] = jnp.dot(dyb, wcr[...], preferred_element_type=F32)
                gi_ref[sl, :] = -jnp.dot(dyb, wci[...], preferred_element_type=F32)
                return 0

            lax.fori_loop(0, L // tc, inp, 0)
            ar = jnp.broadcast_to(lam_r, (SEG, SLAB_S))
            ai = jnp.broadcast_to(lam_i, (SEG, SLAB_S))
            entering = [(ent_ref[0, d, 2 * j], ent_ref[0, d, 2 * j + 1]) for j in range(CHAINS)]
            _seg_scan(xr_ref, xi_ref, ar, ai, d == 1, n_rows, entering=entering)

            def pole(rr, lr, li, acc, last):
                if last:
                    pr, pi = entering[0]
                else:
                    nb = _rows8(rr + 1 if d == 1 else rr - 1)
                    pr, pi = xr_ref[nb, :], xi_ref[nb, :]
                return acc[0] + lr * pr + li * pi, acc[1] + li * pr - lr * pi

            zero = jnp.zeros((SEG, SLAB_S), F32)
            _, (accr, acci) = _seg_scan(gr_ref, gi_ref, ar, -ai, d == 0, n_rows, pole, (zero, zero))
            pg_ref[0, d, 64:65, :] = _colsum(accr)
            pg_ref[0, d, 65:66, :] = _colsum(acci)

            acc_ref[...] = jnp.zeros_like(acc_ref)

            def outp(c, _):
                sl = pl.ds(pl.multiple_of(c * tc, tc), tc)
                lrb, lib = gr_ref[sl, :].astype(BF16), gi_ref[sl, :].astype(BF16)
                du_ref[sl, :] += (lax.dot_general(lrb, wbr[...], _NT, preferred_element_type=F32)
                                  + lax.dot_general(lib, wbi[...], _NT, preferred_element_type=F32))
                ut = u_ref[sl, :].astype(F32).T.astype(BF16)
                dyt = dy_ref[sl, :].T.astype(BF16)
                acc_ref[0] += jnp.dot(ut, lrb, preferred_element_type=F32)
                acc_ref[1] += jnp.dot(ut, lib, preferred_element_type=F32)
                acc_ref[2] += jnp.dot(dyt, xr_ref[sl, :].astype(BF16), preferred_element_type=F32)
                acc_ref[3] -= jnp.dot(dyt, xi_ref[sl, :].astype(BF16), preferred_element_type=F32)
                return 0

            lax.fori_loop(0, L // tc, outp, 0)
            for m in range(4):
                for g in range(SLAB_G):
                    lanes = slice(g * SSM_P, (g + 1) * SSM_P)
                    pg_ref[0, d, m * SSM_H:(m + 1) * SSM_H, lanes] = acc_ref[m, g * SSM_H:(g + 1) * SSM_H, lanes]

    return pl.pallas_call(
        body, name="ssm_bwd", grid=(n_slab,),
        in_specs=[u_spec, u_spec, _ENT_SPEC] + _ssm_param_specs() + [d_spec],
        out_specs=[u_spec, pg_spec],
        out_shape=[jax.ShapeDtypeStruct((L, C), F32), jax.ShapeDtypeStruct((n_slab, 2, PG_ROWS, SLAB_S), F32)],
        scratch_shapes=_MAP_SCRATCH + [pltpu.VMEM((4, SLAB, SLAB_S), F32)] + [pltpu.VMEM((L, SLAB_S), F32)] * 4,
        compiler_params=_cparams(("arbitrary",), 60 << 20),
    )(u, dy, ent, *prm, dskip)


def _ssm_param_grads(pg, prm):
    n_slab = pg.shape[0]
    G = n_slab * SLAB_G
    pg_spec = pl.BlockSpec((1, 2, PG_ROWS, SLAB_S), lambda j: (j, 0, 0, 0))
    pole, _, step, mat = _ssm_param_specs()[:4]

    def body(pg_ref, a_re_ref, a_im_ref, ldt_ref, bt_re_ref, bt_im_ref,
             dbr_ref, dbi_ref, dcr_ref, dci_ref, dar_ref, dai_ref, dldt_ref, dd_ref):
        dd_ref[...] = pg_ref[0, 0, 66:67, 0:SLAB]
        for d in range(2):
            a_r = a_re_ref[d]
            lr, li, dt, lbr, lbi, f_r, f_i = _discretise(a_r, a_im_ref[d], ldt_ref[d])
            gfr_rows, gfi_rows, glr_rows, gli_rows = [], [], [], []
            for g in range(SLAB_G):
                lanes = slice(g * SSM_P, (g + 1) * SSM_P)
                gbr, gbi = pg_ref[0, d, 0:SSM_H, lanes], pg_ref[0, d, SSM_H:2 * SSM_H, lanes]
                b_r, b_i = bt_re_ref[d, g], bt_im_ref[d, g]
                fr, fi = f_r[g:g + 1], f_i[g:g + 1]
                dbr_ref[d, g] = fr * gbr + fi * gbi
                dbi_ref[d, g] = fr * gbi - fi * gbr
                gfr_rows.append(_colsum(gbr * b_r + gbi * b_i))
                gfi_rows.append(_colsum(gbi * b_r - gbr * b_i))
                dcr_ref[d, g] = pg_ref[0, d, 2 * SSM_H:3 * SSM_H, lanes]
                dci_ref[d, g] = pg_ref[0, d, 3 * SSM_H:4 * SSM_H, lanes]
                glr_rows.append(pg_ref[0, d, 64:65, lanes])
                gli_rows.append(pg_ref[0, d, 65:66, lanes])
            gfr, gfi = jnp.concatenate(gfr_rows, axis=0), jnp.concatenate(gfi_rows, axis=0)
            glr, gli = jnp.concatenate(glr_rows, axis=0), jnp.concatenate(gli_rows, axis=0)
            den = lr * lr + li * li
            ir, ii = lr / den, -li / den
            tr, ti = _cmul(ir, -ii, gfr, gfi)
            glbr, glbi = glr + tr, gli + ti
            qr, qi = _cmul(f_r, f_i, ir, ii)
            dlr, dli = _cmul(-qr, qi, gfr, gfi)
            zr, zi = _cmul(lbr, -lbi, glbr, glbi)
            dlr = dlr + dt * zr
            dli = dli + dt * zi
            dar_ref[d] = jnp.where(a_r < -1e-4, dlr, jnp.where(a_r == -1e-4, 0.5 * dlr, 0.0))
            dai_ref[d] = dli
            dldt_ref[d] = jnp.sum(lr * zr + li * zi, axis=-1, keepdims=True) * dt

    a_re, a_im, ldt, bt_re, bt_im = prm[:5]
    mshape = jax.ShapeDtypeStruct(bt_re.shape, F32)
    pshape = jax.ShapeDtypeStruct(a_re.shape, F32)
    return pl.pallas_call(
        body, name="ssm_param_grads", grid=(n_slab,),
        in_specs=[pg_spec, pole, pole, step, mat, mat],
        out_specs=[mat, mat, mat, mat, pole, pole, step, pl.BlockSpec((1, SLAB), lambda j: (0, j))],
        out_shape=[mshape, mshape, mshape, mshape, pshape, pshape, jax.ShapeDtypeStruct(ldt.shape, F32),
                   jax.ShapeDtypeStruct((1, n_slab * SLAB), F32)],
        compiler_params=_cparams(("arbitrary",)),
    )(pg, a_re, a_im, ldt, bt_re, bt_im)


def _peer(k, x, y, c):
    return (1 - x if k & 4 else x, 1 - y if k & 2 else y, 1 - c if k & 1 else c)


def _dev_index(pos):
    return 4 * pos[0] + 2 * pos[1] + pos[2]


def _chip_index(pos):
    return 2 * pos[0] + pos[1]


def _sibling_swap(x, name):
    any_spec = pl.BlockSpec(memory_space=pl.ANY)

    def body(x_ref, out_ref, send_sems, recv_sems):
        x_, y_, c_ = lax.axis_index("x"), lax.axis_index("y"), lax.axis_index("c")
        copies = [pltpu.make_async_remote_copy(
            src_ref=x_ref.at[2 * chip + (1 - c_)], dst_ref=out_ref.at[chip], send_sem=send_sems.at[chip],
            recv_sem=recv_sems.at[chip], device_id=(x_, y_, 1 - c_), device_id_type=pl.DeviceIdType.MESH)
            for chip in range(N_CHIPS)]
        for cp in copies:
            cp.start()
        for cp in copies:
            cp.wait()

    return pl.pallas_call(
        body, name=name, out_shape=jax.ShapeDtypeStruct((N_CHIPS,) + x.shape[1:], x.dtype),
        in_specs=[any_spec], out_specs=any_spec,
        scratch_shapes=[pltpu.SemaphoreType.DMA((N_CHIPS,)), pltpu.SemaphoreType.DMA((N_CHIPS,))],
    )(x)


def _pair_sum(x, got, name):
    n, R, W = got.shape
    tr = _row_tile(R, W, 5 << 20)
    core = lax.axis_index("c").astype(jnp.int32).reshape(1)

    def body(core_ref, a_ref, b_ref, o_ref):
        o_ref[...] = (a_ref[...].astype(F32) + b_ref[...].astype(F32)).astype(BF16)

    spec = pl.BlockSpec((1, tr, W), lambda i, j, c: (i, j, 0))
    grid_spec = pltpu.PrefetchScalarGridSpec(
        num_scalar_prefetch=1, grid=(n, R // tr),
        in_specs=[pl.BlockSpec((1, tr, W), lambda i, j, c: (2 * i + c[0], j, 0)), spec], out_specs=spec)
    return pl.pallas_call(
        body, name=name, grid_spec=grid_spec, out_shape=jax.ShapeDtypeStruct((n, R, W), BF16),
        compiler_params=_cparams(("arbitrary", "arbitrary")),
    )(core, x, got)


def _all_gather(xs, name):
    n = len(xs)
    any_spec = pl.BlockSpec(memory_space=pl.ANY)

    def body(*refs):
        x_refs, out_refs = refs[:n], refs[n:2 * n]
        send_sems, recv_sems, local_sems = refs[2 * n:]
        x, y, c = lax.axis_index("x"), lax.axis_index("y"), lax.axis_index("c")
        me, sibling = (x, y, c), (x, y, 1 - c)
        chips = [(1 - x, y), (x, 1 - y), (1 - x, 1 - y)]

        def copy(a, k, block, to, src=None):
            dst = out_refs[a].at[_dev_index(block)]
            return pltpu.make_async_remote_copy(
                src_ref=dst if src is None else src, dst_ref=dst,
                send_sem=send_sems.at[a, k], recv_sem=recv_sems.at[a, k],
                device_id=to, device_id_type=pl.DeviceIdType.MESH)

        mine = [pltpu.make_async_copy(x_refs[a], out_refs[a].at[_dev_index(me)], local_sems.at[a]) for a in range(n)]
        for cp in mine:
            cp.start()
        first = []
        for a in range(n):
            first.append(copy(a, 0, me, sibling, src=x_refs[a]))
            first += [copy(a, 1 + j, me, (*chip, c), src=x_refs[a]) for j, chip in enumerate(chips)]
        for cp in first:
            cp.start()
        passed = []
        for j, chip in enumerate(chips):
            for a in range(n):
                copy(a, 1 + j, (*chip, c), me).wait_recv()
                fwd = copy(a, 4 + j, (*chip, c), sibling)
                fwd.start()
                passed.append(fwd)
        for a in range(n):
            copy(a, 0, sibling, me).wait_recv()
            for j, chip in enumerate(chips):
                copy(a, 4 + j, (*chip, 1 - c), me).wait_recv()
        for cp in first + passed:
            cp.wait_send()
        for cp in mine:
            cp.wait()

    return pl.pallas_call(
        body, name=name,
        out_shape=[jax.ShapeDtypeStruct((N_DEV,) + v.shape, v.dtype) for v in xs],
        in_specs=[any_spec] * n, out_specs=[any_spec] * n,
        scratch_shapes=[pltpu.SemaphoreType.DMA((n, 7)), pltpu.SemaphoreType.DMA((n, 7)),
                        pltpu.SemaphoreType.DMA((n,))],
    )(*xs)


def _sum_blocks(x, name):
    _, R, W = x.shape

    def body(x_ref, o_ref):
        acc = x_ref[0].astype(F32)
        for d in range(1, N_DEV):
            acc = acc + x_ref[d].astype(F32)
        o_ref[...] = acc

    return pl.pallas_call(body, name=name, out_shape=jax.ShapeDtypeStruct((R, W), F32),
                          compiler_params=pltpu.CompilerParams(vmem_limit_bytes=VMEM_LIMIT))(x)


def _adam_update(w, g, m, v):
    mn = ADAM_B1 * m + (1.0 - ADAM_B1) * g
    vn = ADAM_B2 * v + (1.0 - ADAM_B2) * (g * g)
    m_hat = mn / (1.0 - ADAM_B1 ** ADAM_STEP)
    v_hat = vn / (1.0 - ADAM_B2 ** ADAM_STEP)
    return -ADAM_LR * (m_hat / (jnp.sqrt(v_hat) + ADAM_EPS) + ADAM_WD * w), mn, vn


def _row_tile(R, W, budget):
    padded = -(-W // LANES) * LANES * 4
    if R * padded <= budget:
        return R
    return _pick(R, [t for t in (2048, 1024, 512, 256, 128, 64, 32, 16, 8) if t * padded <= budget])


def _adamw(w, g, m, v, name):
    R, W = w.shape
    tr = _row_tile(R, W, 1 << 20)

    def body(w_ref, g_ref, m_ref, v_ref, d_ref, mo_ref, vo_ref):
        d_ref[...], mo_ref[...], vo_ref[...] = _adam_update(w_ref[...], g_ref[...], m_ref[...], v_ref[...])

    spec = pl.BlockSpec((tr, W), lambda i: (i, 0))
    shp = jax.ShapeDtypeStruct((R, W), F32)
    return pl.pallas_call(
        body, name=name, grid=(R // tr,), in_specs=[spec] * 4, out_specs=[spec] * 3, out_shape=[shp] * 3,
        compiler_params=_cparams(("arbitrary",)),
    )(w, g, m, v)


def _adamw_reduce(w, land, m, v, name):
    R, W = w.shape
    n = land.shape[0]
    tr = _row_tile(R, W, 1 << 20)

    def body(w_ref, l_ref, m_ref, v_ref, g_ref, d_ref, mo_ref, vo_ref):
        g = l_ref[0].astype(F32)
        for d in range(1, n):
            g = g + l_ref[d].astype(F32)
        g_ref[...] = g
        d_ref[...], mo_ref[...], vo_ref[...] = _adam_update(w_ref[...], g, m_ref[...], v_ref[...])

    spec = pl.BlockSpec((tr, W), lambda i: (i, 0))
    lspec = pl.BlockSpec((n, tr, W), lambda i: (0, i, 0))
    shp = jax.ShapeDtypeStruct((R, W), F32)
    return pl.pallas_call(
        body, name=name, grid=(R // tr,), in_specs=[spec, lspec, spec, spec], out_specs=[spec] * 4,
        out_shape=[shp] * 4, compiler_params=_cparams(("arbitrary",)),
    )(w, land, m, v)


def _gelu(v):
    c = math.sqrt(2.0 / math.pi)
    return 0.5 * v * (1.0 + jnp.tanh(c * (v + 0.044715 * v * v * v)))


def _gelu_grad(v):
    c = math.sqrt(2.0 / math.pi)
    t = jnp.tanh(c * (v + 0.044715 * v * v * v))
    return 0.5 * (1.0 + t) + 0.5 * v * (1.0 - t * t) * c * (1.0 + 3.0 * 0.044715 * v * v)


def kernel(x, p, norm_mix, w_in, q_norm, k_norm, ssm_a_re, ssm_a_im, ssm_log_dt, ssm_b_re, ssm_b_im, ssm_c_re, ssm_c_im, ssm_d, w_glu, b_glu, w_out, norm_ple, w_ple_gate, w_ple_proj, norm_final, loss_target, m_norm_mix, m_w_in, m_q_norm, m_k_norm, m_ssm_a_re, m_ssm_a_im, m_ssm_log_dt, m_ssm_b_re, m_ssm_b_im, m_ssm_c_re, m_ssm_c_im, m_ssm_d, m_w_glu, m_b_glu, m_w_out, m_norm_ple, m_w_ple_gate, m_w_ple_proj, m_norm_final, v_norm_mix, v_w_in, v_q_norm, v_k_norm, v_ssm_a_re, v_ssm_a_im, v_ssm_log_dt, v_ssm_b_re, v_ssm_b_im, v_ssm_c_re, v_ssm_c_im, v_ssm_d, v_w_glu, v_b_glu, v_w_out, v_norm_ple, v_w_ple_gate, v_w_ple_proj, v_norm_final):
    L, D = x.shape[1], x.shape[2]
    D_SSM = ssm_d.shape[1]
    G = D_SSM // SSM_H
    n_slab = D_SSM // SLAB
    n_in = w_in.shape[2]
    D_IN = n_in * N_DEV
    n_pp = w_ple_proj.shape[2]
    n_glu = w_glu.shape[2]
    xs = x[0]
    ps = p[0, 0]
    tgt = loss_target[0]

    (win_t3,) = _all_gather([w_in[0].T.astype(BF16)], "gather_w_in")
    win_t = win_t3.reshape(D_IN, D)
    later_weights = _Carry("gather", [w_glu[0].astype(BF16), w_out[0].astype(BF16), w_ple_gate[0].astype(BF16),
                                      w_ple_proj[0].astype(BF16)])

    ssm_prm = (ssm_a_re[0], ssm_a_im[0], ssm_log_dt[0].reshape(2, G, 1),
               ssm_b_re[0].transpose(0, 1, 3, 2), ssm_b_im[0].transpose(0, 1, 3, 2), ssm_c_re[0], ssm_c_im[0])

    cos, sin = _rope_tables(L)
    hn = _norm_in(xs, norm_mix, "norm_mix")
    ZT = 512
    zp_tile = lambda j: jnp.where(j < 2, j, jnp.where(j < D_IN // ZT - 1, j + 1, 2))
    z = _mm(hn, win_t, "nt", "in_proj", out_dtype=BF16, n_tiles=(ZT, zp_tile))
    qr, kr, vb, kt = _qkv_prep(z, cos, sin, q_norm, k_norm)
    o, lse, (wglu3, wout3, wpg3, wpp3) = _attn_fwd(qr, kr, vb, later_weights)
    wout = wout3.reshape(-1, D)
    wpg = wpg3.reshape(-1, D)
    u_off = 2 * D_ATTN
    u_perm = _seg_perm(z[:, u_off:u_off + D_SSM])
    ys_perm, ssm_ent = _ssm_fwd(u_perm, ssm_prm, ssm_d)
    ys = _seg_unperm(ys_perm)

    tm = _pick(L, (256,))

    def gelu_body(y_ref, o_ref):
        o_ref[...] = _gelu(y_ref[...]).astype(BF16)

    (gy,) = _rowcall(gelu_body, "gelu", L, tm, [(ys, _rspec(tm, D_SSM))], [(D_SSM, BF16)])
    glu = _mm(gy, wglu3, "nn", "glu_proj", out_dtype=BF16, bias=b_glu, b_blk=True)

    def mix_body(o_ref, ga_ref, gla_ref, glb_ref, gs_ref, cat_ref):
        ga, gs = _f32(ga_ref), _f32(gs_ref)
        cat_ref[:, :D_ATTN] = (o_ref[...] * ga * _sigmoid(ga)).astype(BF16)
        cat_ref[:, D_ATTN:] = (_f32(gla_ref) * _sigmoid(_f32(glb_ref)) * gs * _sigmoid(gs)).astype(BF16)

    (cat,) = _rowcall(mix_body, "mix", L, tm,
                      [(o, _rspec(tm, D_ATTN)), (z, _rspec(tm, D_ATTN, 1)), (glu, _rspec(tm, D_SSM, 0)),
                       (glu, _rspec(tm, D_SSM, 1)), (z, _rspec(tm, D_SSM, 3))], [(D_ATTN + D_SSM, BF16)])
    h1 = _mm(cat, wout, "nn", "out_proj", add=xs)
    n2 = _norm_in(h1, norm_ple, "norm_ple")
    gpre = _mm(n2, wpg, "nn", "ple_gate", out_dtype=BF16)
    pb = ps.astype(BF16)
    pp = _mm(pb, wpp3, "nn", "ple_proj", out_dtype=BF16, b_blk=True)

    nf = norm_final.reshape(1, D)

    def tail_body(h1_ref, gp_ref, pp_ref, t_ref, g_ref, dh2_ref, dpp_ref, dsg_ref, loss_ref, dg_ref):
        gate = _sigmoid(_f32(gp_ref))
        ppv = _f32(pp_ref)
        h2 = h1_ref[...] + gate * ppv
        r = _rms(h2)
        hh = h2 * r
        err = hh * g_ref[...] - t_ref[...]
        _acc(loss_ref, jnp.broadcast_to(0.5 * jnp.sum(jnp.mean(err * err, axis=-1, keepdims=True)), loss_ref.shape))
        dy = err * (1.0 / D)
        _acc(dg_ref, _colsum(dy * hh))
        dh2 = _rms_bwd(dy, hh, r, g_ref[...])
        dh2_ref[...] = dh2
        dpp_ref[...] = (dh2 * gate).astype(BF16)
        dsg_ref[...] = (dh2 * ppv * gate * (1.0 - gate)).astype(BF16)

    dh2, dpp, dsg, loss_acc, d_nf = _rowcall(
        tail_body, "tail", L, tm,
        [(h1, _rspec(tm, D)), (gpre, _rspec(tm, D)), (pp, _rspec(tm, D)), (tgt, _rspec(tm, D)), (nf, _fspec(nf.shape))],
        [(D, F32), (D, BF16), (D, BF16)], [(1, LANES), (1, D)])

    g_wpp3 = _mm(pb, dpp, "tn", "d_ple_proj", out_dtype=BF16, out_blk=n_pp)
    g_wpg = _mm(n2, dsg, "tn", "d_ple_gate", out_dtype=BF16)
    dn2 = _mm(dsg, wpg, "nt", "d_norm_ple_in", out_dtype=BF16)

    def ple_bwd_body(h1_ref, dn_ref, dh2_ref, g_ref, dh1_ref, dh1b_ref, dg_ref):
        h1v = h1_ref[...]
        r = _rms(h1v)
        hh = h1v * r
        dn = _f32(dn_ref)
        _acc(dg_ref, _colsum(dn * hh))
        dh1 = dh2_ref[...] + _rms_bwd(dn, hh, r, g_ref[...])
        dh1_ref[...] = dh1
        dh1b_ref[...] = dh1.astype(BF16)

    dh1, dh1b, d_nple = _rowcall(
        ple_bwd_body, "ple_bwd", L, tm,
        [(h1, _rspec(tm, D)), (dn2, _rspec(tm, D)), (dh2, _rspec(tm, D)), (norm_ple, _fspec(norm_ple.shape))],
        [(D, F32), (D, BF16)], [(1, D)])

    dcat = _mm(dh1b, wout, "nt", "d_cat", out_dtype=BF16)
    g_wout = _mm(cat, dh1b, "tn", "d_out_proj", out_dtype=BF16)

    def mix_bwd_body(dca_ref, dcs_ref, o_ref, ga_ref, gla_ref, glb_ref, gs_ref,
                     do_ref, dga_ref, dgs_ref, dglu_ref, db_ref):
        dca, dcs, ga, gs = _f32(dca_ref), _f32(dcs_ref), _f32(ga_ref), _f32(gs_ref)
        sa, ss, sb = _sigmoid(ga), _sigmoid(gs), _sigmoid(_f32(glb_ref))
        gla = _f32(gla_ref)
        do_ref[...] = (dca * ga * sa).astype(BF16)
        dga_ref[...] = (dca * o_ref[...] * sa * (1.0 + ga * (1.0 - sa))).astype(BF16)
        dgs_ref[...] = (dcs * gla * sb * ss * (1.0 + gs * (1.0 - ss))).astype(BF16)
        dy2 = dcs * gs * ss
        da, db = dy2 * sb, dy2 * gla * sb * (1.0 - sb)
        dglu_ref[:, :D_SSM] = da.astype(BF16)
        dglu_ref[:, D_SSM:] = db.astype(BF16)
        _acc(db_ref, jnp.concatenate([_colsum(da), _colsum(db)], axis=-1))

    do, dga, dgs, dglu, g_bglu = _rowcall(
        mix_bwd_body, "mix_bwd", L, tm,
        [(dcat, _rspec(tm, D_ATTN, 0)), (dcat, _rspec(tm, D_SSM, 1)), (o, _rspec(tm, D_ATTN)),
         (z, _rspec(tm, D_ATTN, 1)), (glu, _rspec(tm, D_SSM, 0)), (glu, _rspec(tm, D_SSM, 1)),
         (z, _rspec(tm, D_SSM, 3))],
        [(D_ATTN, BF16), (D_ATTN, BF16), (D_SSM, BF16), (2 * D_SSM, BF16)], [(1, 2 * D_SSM)])

    g_wglu3 = _mm(gy, dglu, "tn", "d_glu_proj", out_dtype=BF16, out_blk=n_glu)
    dys = _mm(dglu, wglu3, "nt", "d_ssm_out", b_blk=True,
              post=(lambda out, y: out * _gelu_grad(y), ys))
    du_perm, pg = _ssm_bwd(u_perm, _seg_perm(dys), ssm_ent, ssm_prm, ssm_d)
    du = _seg_unperm(du_perm)

    pg_send = pg.reshape(N_DEV, (n_slab // N_DEV) * 2 * PG_ROWS, SLAB_S)
    dqs, dkr, dvv, (l_wglu, l_wout, l_wpg, l_wpp, l_pg) = _attn_bwd(
        qr, kr, vb, kt, do, o, lse,
        _Carry("a2a", [g_wglu3, g_wout.reshape(N_DEV, -1, D), g_wpg.reshape(N_DEV, -1, D), g_wpp3, pg_send]))

    scale = HEAD_DIM ** -0.5
    kblk = 4 * D_ATTN // D_KV

    a0, k0, v0, u0, s0 = D_ATTN + 2 * D_KV, D_ATTN, D_ATTN + D_KV, 2 * D_ATTN + 2 * D_KV, 2 * D_ATTN + 2 * D_KV + D_SSM

    def qkv_bwd_body(dq_ref, dk_ref, dv_ref, q_ref, k_ref, cos_ref, sin_ref, qn_ref, kn_ref, dga_ref, du_ref, dgs_ref,
                     dz_ref, dqn_ref, dkn_ref):
        c, s = cos_ref[...], sin_ref[...]
        dz_ref[:, a0:a0 + D_ATTN] = dga_ref[...]
        dz_ref[:, u0:u0 + D_SSM] = du_ref[...].astype(BF16)
        dz_ref[:, s0:s0 + D_SSM] = dgs_ref[...]

        def head(g, xh, w):
            dn = g * c + _partner(g * s)
            r = _rms(xh)
            xhat = xh * r
            return _rms_bwd(dn, xhat, r, w), _colsum(dn * xhat)

        dqn = jnp.zeros((1, HEAD_DIM), F32)
        for h in range(N_HEADS):
            sl = slice(h * HEAD_DIM, (h + 1) * HEAD_DIM)
            dx, dw = head(dq_ref[:, sl] * scale, q_ref[:, sl].astype(F32), qn_ref[...])
            dz_ref[:, sl] = dx.astype(BF16)
            dqn = dqn + dw
        dkn = jnp.zeros((1, HEAD_DIM), F32)
        for h in range(N_KV):
            sl = slice(h * HEAD_DIM, (h + 1) * HEAD_DIM)
            dx, dw = head(dk_ref[:, sl], k_ref[:, sl].astype(F32), kn_ref[...])
            dz_ref[:, k0 + h * HEAD_DIM:k0 + (h + 1) * HEAD_DIM] = dx.astype(BF16)
            dkn = dkn + dw
        dz_ref[:, v0:v0 + D_KV] = dv_ref[...].astype(BF16)
        _acc(dqn_ref, dqn)
        _acc(dkn_ref, dkn)

    dz, g_qn, g_kn = _rowcall(
        qkv_bwd_body, "qkv_bwd", L, tm,
        [(dqs, _rspec(tm, D_ATTN)), (dkr, _rspec(tm, D_KV)), (dvv, _rspec(tm, D_KV)),
         (z, _rspec(tm, D_ATTN, 0)), (z, _rspec(tm, D_KV, kblk)), (cos, _rspec(tm, HEAD_DIM)),
         (sin, _rspec(tm, HEAD_DIM)), (q_norm, _fspec(q_norm.shape)), (k_norm, _fspec(k_norm.shape)),
         (dga, _rspec(tm, D_ATTN)), (du, _rspec(tm, D_SSM)), (dgs, _rspec(tm, D_SSM))],
        [(D_IN, BF16)], [(1, HEAD_DIM), (1, HEAD_DIM)])

    g_win_t = _mm(dz, hn, "tn", "d_in_proj", out_dtype=BF16)
    g_win8 = g_win_t.reshape(N_DEV, n_in, D)
    from_sibling = _sibling_swap(g_win8, "swap_d_w_in")
    pair = _pair_sum(g_win8, from_sibling, "pair_sum_d_w_in")
    dhn, (l_win_t,) = _mm(dz, win_t, "nn", "d_norm_mix_in", out_dtype=BF16,
                          carry=_Carry("a2a_chips", [pair]))

    def in_bwd_body(x_ref, dn_ref, dh1_ref, g_ref, dx_ref, dg_ref):
        xv = x_ref[...]
        r = _rms(xv)
        hh = xv * r
        dn = _f32(dn_ref)
        _acc(dg_ref, _colsum(dn * hh))
        dx_ref[...] = dh1_ref[...] + _rms_bwd(dn, hh, r, g_ref[...])

    grad_x, g_nmix = _rowcall(
        in_bwd_body, "in_bwd", L, tm,
        [(xs, _rspec(tm, D)), (dhn, _rspec(tm, D)), (dh1, _rspec(tm, D)), (norm_mix, _fspec(norm_mix.shape))],
        [(D, F32)], [(1, D)])

    tiny_parts = [g_nmix, g_bglu, d_nple, d_nf, g_qn, g_kn, loss_acc[:, :1]]
    tiny_flat = jnp.concatenate([t.reshape(-1) for t in tiny_parts])
    tiny_rows = -(-tiny_flat.shape[0] // (8 * LANES)) * 8
    tiny = jnp.pad(tiny_flat, (0, tiny_rows * LANES - tiny_flat.shape[0])).reshape(tiny_rows, LANES)
    pg_sum = _sum_blocks(l_pg, "sum_ssm_grads")
    pg_all, tiny_all = _all_gather([pg_sum, tiny], "gather_small_grads")
    (g_bt_re, g_bt_im, g_c_re, g_c_im, g_a_re, g_a_im, g_ldt, g_skip) = _ssm_param_grads(
        pg_all.reshape(n_slab, 2, PG_ROWS, SLAB_S), ssm_prm)
    tiny_sum = _sum_blocks(tiny_all, "sum_tiny_grads").reshape(-1)
    tiny_grads, off = [], 0
    for t in tiny_parts:
        tiny_grads.append(tiny_sum[off:off + t.size].reshape(t.shape))
        off += t.size
    r_nmix, r_bglu, r_nple, r_nf, r_qn, r_kn, loss = tiny_grads
    loss = loss.reshape(())

    grads, deltas, new_ms, new_vs = {}, {}, {}, {}
    outs = _adamw_reduce(w_in[0].T, l_win_t, m_w_in[0].T, v_w_in[0].T, "adamw_w_in")
    grads["w_in"], deltas["w_in"], new_ms["w_in"], new_vs["w_in"] = [t.T[None] for t in outs]
    big = [("w_glu", w_glu, l_wglu, m_w_glu, v_w_glu),
           ("w_out", w_out, l_wout, m_w_out, v_w_out), ("w_ple_gate", w_ple_gate, l_wpg, m_w_ple_gate, v_w_ple_gate),
           ("w_ple_proj", w_ple_proj, l_wpp, m_w_ple_proj, v_w_ple_proj)]
    for name, w, ld, m, v in big:
        shp = w.shape
        outs = _adamw_reduce(w[0], ld, m[0], v[0], "adamw_" + name)
        grads[name], deltas[name], new_ms[name], new_vs[name] = [t.reshape(shp) for t in outs]
    bt2 = (2 * G * SSM_H, SSM_P)
    for name, w, g, m, v in (("ssm_b_re", ssm_b_re, g_bt_re, m_ssm_b_re, v_ssm_b_re),
                             ("ssm_b_im", ssm_b_im, g_bt_im, m_ssm_b_im, v_ssm_b_im)):
        to2 = lambda t: t[0].transpose(0, 1, 3, 2).reshape(bt2)
        back = lambda t: t.reshape(2, G, SSM_H, SSM_P).transpose(0, 1, 3, 2)[None]
        outs = _adamw(to2(w), g.reshape(bt2), to2(m), to2(v), "adamw_" + name)
        grads[name] = back(g)
        deltas[name], new_ms[name], new_vs[name] = [back(t) for t in outs]
    small = [("norm_mix", norm_mix, r_nmix, m_norm_mix, v_norm_mix, (1, D)),
             ("q_norm", q_norm, r_qn, m_q_norm, v_q_norm, (1, HEAD_DIM)),
             ("k_norm", k_norm, r_kn, m_k_norm, v_k_norm, (1, HEAD_DIM)),
             ("ssm_a_re", ssm_a_re, g_a_re, m_ssm_a_re, v_ssm_a_re, (2 * G, SSM_P)),
             ("ssm_a_im", ssm_a_im, g_a_im, m_ssm_a_im, v_ssm_a_im, (2 * G, SSM_P)),
             ("ssm_log_dt", ssm_log_dt, g_ldt, m_ssm_log_dt, v_ssm_log_dt, (2, G)),
             ("ssm_c_re", ssm_c_re, g_c_re, m_ssm_c_re, v_ssm_c_re, (2 * G * SSM_H, SSM_P)),
             ("ssm_c_im", ssm_c_im, g_c_im, m_ssm_c_im, v_ssm_c_im, (2 * G * SSM_H, SSM_P)),
             ("ssm_d", ssm_d, g_skip, m_ssm_d, v_ssm_d, (1, D_SSM)),
             ("b_glu", b_glu, r_bglu, m_b_glu, v_b_glu, (1, 2 * D_SSM)),
             ("norm_ple", norm_ple, r_nple, m_norm_ple, v_norm_ple, (1, D)),
             ("norm_final", norm_final, r_nf, m_norm_final, v_norm_final, (1, D))]
    for name, w, g, m, v, s2 in small:
        shp = w.shape
        outs = _adamw(w.reshape(s2), g.reshape(s2), m.reshape(s2), v.reshape(s2), "adamw_" + name)
        grads[name] = g.reshape(shp)
        deltas[name], new_ms[name], new_vs[name] = [t.reshape(shp) for t in outs]

    order = ["norm_mix", "w_in", "q_norm", "k_norm", "ssm_a_re", "ssm_a_im", "ssm_log_dt", "ssm_b_re", "ssm_b_im",
             "ssm_c_re", "ssm_c_im", "ssm_d", "w_glu", "b_glu", "w_out", "norm_ple", "w_ple_gate", "w_ple_proj",
             "norm_final"]
    return (loss, grad_x[None], *[grads[k] for k in order], *[deltas[k] for k in order],
            *[new_ms[k] for k in order], *[new_vs[k] for k in order])
```

```python
import functools
import math

import numpy as np
import jax
import jax.numpy as jnp
from jax import lax
from jax.experimental import pallas as pl
from jax.experimental.pallas import tpu as pltpu

F32 = jnp.float32
BF16 = jnp.bfloat16

N_DEV = 8
N_CHIPS = 4
EPS = 1e-6
GRID_W = 64
ROPE_THETA = 10000.0
HEAD_DIM = 128
N_HEADS = 8
N_KV = 2
REP = N_HEADS // N_KV
D_ATTN = N_HEADS * HEAD_DIM
D_KV = N_KV * HEAD_DIM
SSM_H = 16
SSM_P = 64
SLAB = 128
SLAB_G = SLAB // SSM_H
SLAB_S = SLAB_G * SSM_P
SEG = 8
CHAINS = 2
SCAN_UNROLL = 8
LANES = 128
PG_ROWS = 72
VMEM_LIMIT = 48 << 20

ADAM_LR = 0.001
ADAM_B1 = 0.9
ADAM_B2 = 0.999
ADAM_EPS = 1e-08
ADAM_WD = 0.01
ADAM_STEP = 10


def _pick(n, cands):
    for c in cands:
        if n % c == 0:
            return c
    return n


def _cparams(sem, vmem=VMEM_LIMIT):
    return pltpu.CompilerParams(dimension_semantics=sem, vmem_limit_bytes=vmem)


class _Carry:
    def __init__(self, kind, xs):
        self.kind, self.xs, self.n = kind, list(xs), len(xs)
        self.ks = (2, 4, 6) if kind == "a2a_chips" else tuple(range(1, N_DEV))
        self.index = _chip_index if kind == "a2a_chips" else _dev_index
        lead = (N_DEV,) if kind == "gather" else ()
        self.out_shape = [jax.ShapeDtypeStruct(lead + v.shape, v.dtype) for v in xs]
        self.specs = [pl.BlockSpec(memory_space=pl.ANY)] * self.n
        self.scratch = [pltpu.SemaphoreType.DMA((self.n, len(self.ks))), pltpu.SemaphoreType.DMA((self.n, len(self.ks))),
                        pltpu.SemaphoreType.DMA((self.n,))]

    def _copies(self, x_refs, out_refs, sems):
        send_sems, recv_sems, local_sems = sems
        x, y, c = lax.axis_index("x"), lax.axis_index("y"), lax.axis_index("c")
        me = self.index((x, y, c))
        mine, sends, arrivals = [], [], []
        for a in range(self.n):
            src_mine = x_refs[a] if self.kind == "gather" else x_refs[a].at[me]
            mine.append(pltpu.make_async_copy(src_mine, out_refs[a].at[me], local_sems.at[a]))
            for s, k in enumerate(self.ks):
                peer = _peer(k, x, y, c)
                src = x_refs[a] if self.kind == "gather" else x_refs[a].at[self.index(peer)]
                sends.append(pltpu.make_async_remote_copy(
                    src_ref=src, dst_ref=out_refs[a].at[me], send_sem=send_sems.at[a, s],
                    recv_sem=recv_sems.at[a, s], device_id=peer, device_id_type=pl.DeviceIdType.MESH))
                land = out_refs[a].at[self.index(peer)]
                arrivals.append(pltpu.make_async_remote_copy(
                    src_ref=land, dst_ref=land, send_sem=send_sems.at[a, s],
                    recv_sem=recv_sems.at[a, s], device_id=peer, device_id_type=pl.DeviceIdType.MESH))
        return mine, sends, arrivals

    def start(self, x_refs, out_refs, sems):
        mine, sends, _ = self._copies(x_refs, out_refs, sems)
        for cp in mine + sends:
            cp.start()

    def wait(self, x_refs, out_refs, sems):
        mine, sends, arrivals = self._copies(x_refs, out_refs, sems)
        for cp in arrivals:
            cp.wait_recv()
        for cp in sends:
            cp.wait_send()
        for cp in mine:
            cp.wait()


def _grid_edges(grid):
    first = functools.reduce(lambda p, q: p & q, [pl.program_id(d) == 0 for d in range(len(grid))])
    last = functools.reduce(lambda p, q: p & q, [pl.program_id(d) == g - 1 for d, g in enumerate(grid)])
    return first, last


def _mm(a, b, mode, name, out_dtype=F32, add=None, bias=None, a_blk=False, b_blk=False, out_blk=0, carry=None,
        n_tiles=None, post=None):
    w = b.shape[2] if b_blk else out_blk
    if mode == "nn":
        M, K = a.shape
        N = b.shape[0] * w if b_blk else b.shape[1]
    elif mode == "nt":
        M = a.shape[1] if a_blk else a.shape[0]
        N = b.shape[1] if b_blk else b.shape[0]
        K = b.shape[0] * w if b_blk else b.shape[1]
    else:
        K, M = a.shape
        N = b.shape[0] * w if b_blk else b.shape[1]
    tm = _pick(M, (1024, 768, 512, 256))
    tn = _pick(N, (1024, 768, 512, 256))
    if mode == "tn" and N <= 2048:
        tn = N
    tk = K if (mode != "tn" and K <= 2048) else _pick(K, (1024, 768, 512, 256))
    if mode == "nn" and K > 2048 and N <= 2048:
        tn, tk = N, _pick(K, (1536, 1024, 768, 512, 256))
    perm = lambda j: j
    if n_tiles:
        tn, perm = n_tiles
        tm = _pick(M, (2048, 1024, 512, 256))
    if mode == "nt" and b_blk:
        tk = w
    elif b_blk or out_blk:
        tn = w
    nk = K // tk
    grid = (M // tm, N // tn, nk)
    if mode == "nn":
        a_spec = pl.BlockSpec((tm, tk), lambda i, j, k: (i, k))
        b_spec = (pl.BlockSpec((1, tk, tn), lambda i, j, k: (j, k, 0)) if b_blk
                  else pl.BlockSpec((tk, tn), lambda i, j, k: (k, j)))
        dims = (((1,), (0,)), ((), ()))
    elif mode == "nt":
        a_spec = (pl.BlockSpec((1, tm, tk), lambda i, j, k: (k, i, 0)) if a_blk
                  else pl.BlockSpec((tm, tk), lambda i, j, k: (i, k)))
        b_spec = (pl.BlockSpec((1, tn, tk), lambda i, j, k: (k, j, 0)) if b_blk
                  else pl.BlockSpec((tn, tk), lambda i, j, k: (perm(j), k)))
        dims = (((1,), (1,)), ((), ()))
    else:
        a_spec = pl.BlockSpec((tk, tm), lambda i, j, k: (k, i))
        b_spec = (pl.BlockSpec((1, tk, tn), lambda i, j, k: (j, k, 0)) if b_blk
                  else pl.BlockSpec((tk, tn), lambda i, j, k: (k, j)))
        dims = (((0,), (0,)), ((), ()))
    if out_blk:
        out_spec = pl.BlockSpec((1, tm, tn), lambda i, j, k: (j, i, 0))
        out_shape = jax.ShapeDtypeStruct((N // tn, M, tn), out_dtype)
    else:
        out_spec = pl.BlockSpec((tm, tn), lambda i, j, k: (i, j))
        out_shape = jax.ShapeDtypeStruct((M, N), out_dtype)
    extras, extra_specs, combine = [], [], []
    if add is not None:
        extras.append(add)
        extra_specs.append(pl.BlockSpec((tm, tn), lambda i, j, k: (i, j)))
        combine.append(lambda out, t: out + t)
    if bias is not None:
        extras.append(bias)
        extra_specs.append(pl.BlockSpec((1, tn), lambda i, j, k: (0, j)))
        combine.append(lambda out, t: out + t)
    if post is not None:
        extras.append(post[1])
        extra_specs.append(pl.BlockSpec((tm, tn), lambda i, j, k: (i, j)))
        combine.append(post[0])

    n_ex = len(extras)
    nc = carry.n if carry else 0

    def body(a_ref, b_ref, *rest):
        ex_refs, cx = rest[:n_ex], rest[n_ex:n_ex + nc]
        o_ref, cout = rest[n_ex + nc], rest[n_ex + nc + 1:n_ex + 2 * nc + 1]
        tail = rest[n_ex + 2 * nc + 1:]
        sems = tail[:3] if carry else ()
        first, last = _grid_edges(grid)
        if carry:
            @pl.when(first)
            def _():
                carry.start(cx, cout, sems)

        def product():
            av = a_ref[0] if a_blk else a_ref[...]
            bv = b_ref[0] if b_blk else b_ref[...]
            return lax.dot_general(av, bv, dims, preferred_element_type=F32)

        def finish(out):
            for r, fn in zip(ex_refs, combine):
                out = fn(out, r[...])
            if out_blk:
                o_ref[0] = out.astype(out_dtype)
            else:
                o_ref[...] = out.astype(out_dtype)

        if nk == 1:
            finish(product())
        else:
            acc_ref = tail[-1]
            k = pl.program_id(2)

            @pl.when(k == 0)
            def _():
                acc_ref[...] = jnp.zeros_like(acc_ref)

            acc_ref[...] += product()

            @pl.when(k == nk - 1)
            def _():
                finish(acc_ref[...])

        if carry:
            @pl.when(last)
            def _():
                carry.wait(cx, cout, sems)

    scratch = (carry.scratch if carry else []) + ([pltpu.VMEM((tm, tn), F32)] if nk > 1 else [])
    outs = pl.pallas_call(
        body, name=name, grid=grid,
        in_specs=[a_spec, b_spec] + extra_specs + (carry.specs if carry else []),
        out_specs=[out_spec] + (carry.specs if carry else []),
        out_shape=[out_shape] + (carry.out_shape if carry else []),
        scratch_shapes=scratch,
        compiler_params=_cparams(("arbitrary", "arbitrary", "arbitrary")),
    )(a, b, *extras, *(carry.xs if carry else []))
    return (outs[0], outs[1:]) if carry else outs[0]


def _rspec(tm, w, cb=0):
    return pl.BlockSpec((tm, w), lambda i: (i, cb))


def _fspec(shape):
    nd = len(shape)
    return pl.BlockSpec(shape, lambda i: (0,) * nd)


def _rowcall(body, name, L, tm, ins, row_outs, acc_outs=()):
    out_shape = [jax.ShapeDtypeStruct((L, w), dt) for w, dt in row_outs]
    out_shape += [jax.ShapeDtypeStruct(s, F32) for s in acc_outs]
    out_specs = [_rspec(tm, w) for w, _ in row_outs] + [_fspec(s) for s in acc_outs]
    return pl.pallas_call(
        body, name=name, grid=(L // tm,),
        in_specs=[s for _, s in ins], out_specs=out_specs, out_shape=out_shape,
        compiler_params=_cparams(("arbitrary",)),
    )(*[a for a, _ in ins])


def _acc(ref, val):
    @pl.when(pl.program_id(0) == 0)
    def _():
        ref[...] = jnp.zeros_like(ref)
    ref[...] += val


def _colsum(v):
    return jnp.sum(v, axis=0, keepdims=True)


def _rms(xv):
    return lax.rsqrt(jnp.mean(xv * xv, axis=-1, keepdims=True) + EPS)


def _rms_bwd(dn, xhat, r, g):
    dng = dn * g
    return r * (dng - xhat * jnp.mean(dng * xhat, axis=-1, keepdims=True))


def _sigmoid(v):
    return jax.nn.sigmoid(v)


def _f32(ref):
    return ref[...].astype(F32)


def _partner(v):
    w = v.shape[-1]
    lane = lax.broadcasted_iota(jnp.int32, v.shape, v.ndim - 1)
    first_half = (lane % 64) < 32
    return jnp.where(first_half, pltpu.roll(v, w - 32, axis=v.ndim - 1), pltpu.roll(v, 32, axis=v.ndim - 1))


def _norm_in(x, g, name):
    L, D = x.shape
    tm = _pick(L, (512, 256))

    def body(x_ref, g_ref, o_ref):
        xv = x_ref[...]
        o_ref[...] = (xv * _rms(xv) * g_ref[...]).astype(BF16)

    return _rowcall(body, name, L, tm, [(x, _rspec(tm, D)), (g, _fspec(g.shape))], [(D, BF16)])[0]


def _rope_tables(L):
    t = np.arange(L)
    rows = (t // GRID_W).astype(np.float32)
    cols = (t % GRID_W).astype(np.float32)
    n_freq = HEAD_DIM // 4
    inv_freq = np.float32(ROPE_THETA) ** (-np.arange(n_freq, dtype=np.float32) / np.float32(n_freq))
    ar = (rows[:, None] * inv_freq[None, :]).astype(np.float32).astype(np.float64)
    ac = (cols[:, None] * inv_freq[None, :]).astype(np.float32).astype(np.float64)
    cos = np.concatenate([np.cos(ar), np.cos(ar), np.cos(ac), np.cos(ac)], axis=-1).astype(np.float32)
    sin = np.concatenate([-np.sin(ar), np.sin(ar), -np.sin(ac), np.sin(ac)], axis=-1).astype(np.float32)
    return jnp.asarray(cos), jnp.asarray(sin)


def _qkv_prep(z, cos, sin, qn, kn):
    L = z.shape[0]
    tm = _pick(L, (512, 256))
    scale = HEAD_DIM ** -0.5
    kblk = 4 * D_ATTN // D_KV

    def body(q_ref, k_ref, v_ref, cos_ref, sin_ref, qn_ref, kn_ref, qo_ref, ko_ref, vo_ref, kt_ref):
        c, s = cos_ref[...], sin_ref[...]

        def head(xh, w):
            n = xh * _rms(xh) * w
            return n * c + _partner(n) * s

        for h in range(N_HEADS):
            sl = slice(h * HEAD_DIM, (h + 1) * HEAD_DIM)
            qo_ref[:, sl] = (head(q_ref[:, sl].astype(F32), qn_ref[...]) * scale).astype(BF16)
        for h in range(N_KV):
            sl = slice(h * HEAD_DIM, (h + 1) * HEAD_DIM)
            kr = head(k_ref[:, sl].astype(F32), kn_ref[...])
            ko_ref[:, sl] = kr.astype(BF16)
            kt_ref[sl, :] = kr.T.astype(BF16)
        vo_ref[...] = v_ref[...].astype(BF16)

    return pl.pallas_call(
        body, name="qkv_prep", grid=(L // tm,),
        in_specs=[_rspec(tm, D_ATTN, 0), _rspec(tm, D_KV, kblk), _rspec(tm, D_KV, kblk + 1),
                  _rspec(tm, HEAD_DIM), _rspec(tm, HEAD_DIM), _fspec(qn.shape), _fspec(kn.shape)],
        out_specs=[_rspec(tm, D_ATTN), _rspec(tm, D_KV), _rspec(tm, D_KV),
                   pl.BlockSpec((D_KV, tm), lambda i: (0, i))],
        out_shape=[jax.ShapeDtypeStruct((L, D_ATTN), BF16), jax.ShapeDtypeStruct((L, D_KV), BF16),
                   jax.ShapeDtypeStruct((L, D_KV), BF16), jax.ShapeDtypeStruct((D_KV, L), BF16)],
        compiler_params=_cparams(("arbitrary",)),
    )(z, z, z, cos, sin, qn, kn)


def _col_to_row(col):
    n = col.shape[0]
    eye = lax.broadcasted_iota(jnp.int32, (n, n), 0) == lax.broadcasted_iota(jnp.int32, (n, n), 1)
    return jnp.sum(jnp.where(eye, col, 0.0), axis=0, keepdims=True)


def _attn_fwd(q, k, v, carry=None):
    L = q.shape[0]
    tq = _pick(L, (256, 128))
    grid = (N_HEADS, L // tq)
    nc = carry.n if carry else 0

    def body(q_ref, k_ref, v_ref, *rest):
        cx, (o_ref, lse_ref) = rest[:nc], rest[nc:nc + 2]
        cout, sems = rest[nc + 2:2 * nc + 2], rest[2 * nc + 2:]
        first, last = _grid_edges(grid)
        if carry:
            @pl.when(first)
            def _():
                carry.start(cx, cout, sems)

        s = lax.dot_general(q_ref[...], k_ref[...], (((1,), (1,)), ((), ())), preferred_element_type=F32)
        m = jnp.max(s, axis=-1, keepdims=True)
        e = jnp.exp(s - m)
        l = jnp.sum(e, axis=-1, keepdims=True)
        o_ref[...] = jnp.dot(e.astype(BF16), v_ref[...], preferred_element_type=F32) / l
        lse_ref[0] = _col_to_row(m + jnp.log(l))

        if carry:
            @pl.when(last)
            def _():
                carry.wait(cx, cout, sems)

    outs = pl.pallas_call(
        body, name="attn_fwd", grid=grid,
        in_specs=[pl.BlockSpec((tq, HEAD_DIM), lambda h, i: (i, h)),
                  pl.BlockSpec((L, HEAD_DIM), lambda h, i: (0, h // REP)),
                  pl.BlockSpec((L, HEAD_DIM), lambda h, i: (0, h // REP))] + (carry.specs if carry else []),
        out_specs=[pl.BlockSpec((tq, HEAD_DIM), lambda h, i: (i, h)),
                   pl.BlockSpec((1, 1, tq), lambda h, i: (h, 0, i))] + (carry.specs if carry else []),
        out_shape=[jax.ShapeDtypeStruct((L, D_ATTN), F32), jax.ShapeDtypeStruct((N_HEADS, 1, L), F32)]
        + (carry.out_shape if carry else []),
        scratch_shapes=carry.scratch if carry else [],
        compiler_params=_cparams(("arbitrary", "arbitrary")),
    )(q, k, v, *(carry.xs if carry else []))
    return outs[0], outs[1], outs[2:]


def _attn_bwd(q, k, v, kt, do, o, lse, carry=None):
    L = q.shape[0]
    tq = _pick(L, (256, 128))
    kc = _pick(L, (512, 256, 128))
    nt = (((1,), (1,)), ((), ()))
    grid = (N_KV, REP, L // tq)
    nc = carry.n if carry else 0

    def body(q_ref, do_ref, o_ref, lse_ref, k_ref, v_ref, kt_ref, *rest):
        cx, (dq_ref, dk_ref, dv_ref) = rest[:nc], rest[nc:nc + 3]
        cout, sems = rest[nc + 3:2 * nc + 3], rest[2 * nc + 3:]
        first, last = _grid_edges(grid)
        if carry:
            @pl.when(first)
            def _():
                carry.start(cx, cout, sems)

        @pl.when((pl.program_id(1) == 0) & (pl.program_id(2) == 0))
        def _():
            dk_ref[...] = jnp.zeros_like(dk_ref)
            dv_ref[...] = jnp.zeros_like(dv_ref)

        qv, dov = q_ref[...], do_ref[...]
        lse_row = lse_ref[0]
        delta = _col_to_row(jnp.sum(dov.astype(F32) * o_ref[...], axis=-1, keepdims=True))
        dqt = jnp.zeros((HEAD_DIM, tq), F32)
        for c in range(L // kc):
            sl = slice(c * kc, (c + 1) * kc)
            st = lax.dot_general(k_ref[sl, :], qv, nt, preferred_element_type=F32)
            pt = jnp.exp(st - lse_row)
            dpt = lax.dot_general(v_ref[sl, :], dov, nt, preferred_element_type=F32)
            dst = (pt * (dpt - delta)).astype(BF16)
            dv_ref[sl, :] += jnp.dot(pt.astype(BF16), dov, preferred_element_type=F32)
            dk_ref[sl, :] += jnp.dot(dst, qv, preferred_element_type=F32)
            dqt = dqt + jnp.dot(kt_ref[:, sl], dst, preferred_element_type=F32)
        dq_ref[...] = dqt.T

        if carry:
            @pl.when(last)
            def _():
                carry.wait(cx, cout, sems)

    head = lambda g, r, i: (i, g * REP + r)
    outs = pl.pallas_call(
        body, name="attn_bwd", grid=grid,
        in_specs=[pl.BlockSpec((tq, HEAD_DIM), head), pl.BlockSpec((tq, HEAD_DIM), head),
                  pl.BlockSpec((tq, HEAD_DIM), head),
                  pl.BlockSpec((1, 1, tq), lambda g, r, i: (g * REP + r, 0, i)),
                  pl.BlockSpec((L, HEAD_DIM), lambda g, r, i: (0, g)),
                  pl.BlockSpec((L, HEAD_DIM), lambda g, r, i: (0, g)),
                  pl.BlockSpec((HEAD_DIM, L), lambda g, r, i: (g, 0))] + (carry.specs if carry else []),
        out_specs=[pl.BlockSpec((tq, HEAD_DIM), head),
                   pl.BlockSpec((L, HEAD_DIM), lambda g, r, i: (0, g)),
                   pl.BlockSpec((L, HEAD_DIM), lambda g, r, i: (0, g))] + (carry.specs if carry else []),
        out_shape=[jax.ShapeDtypeStruct((L, D_ATTN), F32), jax.ShapeDtypeStruct((L, D_KV), F32),
                   jax.ShapeDtypeStruct((L, D_KV), F32)] + (carry.out_shape if carry else []),
        scratch_shapes=carry.scratch if carry else [],
        compiler_params=_cparams(("arbitrary", "arbitrary", "arbitrary")),
    )(q, do, o, lse, k, v, kt, *(carry.xs if carry else []))
    return outs[0], outs[1], outs[2], outs[3:]


def _seg_perm(a):
    L, C = a.shape
    return a.reshape(SEG, L // SEG, C).transpose(1, 0, 2).reshape(L, C)


def _seg_unperm(a):
    L, C = a.shape
    return a.reshape(L // SEG, SEG, C).transpose(1, 0, 2).reshape(L, C)


def _cmul(ar, ai, br, bi):
    return ar * br - ai * bi, ar * bi + ai * br


def _rows8(rr):
    if isinstance(rr, int):
        return pl.ds(rr * SEG, SEG)
    return pl.ds(pl.multiple_of(rr * SEG, SEG), SEG)


def _seg_scan(xr_ref, xi_ref, ar, ai, reverse, n_rows, visit=None, visit_init=(), entering=None):
    shape = ar.shape
    zero = jnp.zeros(shape, F32)
    rc = n_rows // CHAINS

    def index(q):
        return (n_rows - 1 - q) if reverse else q

    if entering is None:
        def ends(q, carry):
            out = []
            for j in range(CHAINS):
                sl = _rows8(index(j * rc + q))
                pr, pi = _cmul(ar, ai, carry[2 * j], carry[2 * j + 1])
                out += [pr + xr_ref[sl, :], pi + xi_ref[sl, :]]
            return tuple(out)

        def ends_block(qb, carry):
            for t in range(SCAN_UNROLL):
                carry = ends(qb * SCAN_UNROLL + t, carry)
            return carry

        e = lax.fori_loop(0, rc // SCAN_UNROLL, ends_block, (zero,) * (2 * CHAINS))

        pr, pi = ar, ai
        for _ in range(int(math.log2(rc))):
            pr, pi = _cmul(pr, pi, pr, pi)
        sub = lax.broadcasted_iota(jnp.int32, shape, 0)
        shift = (SEG - 1) if reverse else 1
        edge = (SEG - 1) if reverse else 0
        entering = [(zero, zero)] * CHAINS
        for _ in range(SEG):
            tr, ti = _cmul(pr, pi, *entering[CHAINS - 1])
            cur = (jnp.where(sub == edge, 0.0, pltpu.roll(tr + e[2 * CHAINS - 2], shift, axis=0)),
                   jnp.where(sub == edge, 0.0, pltpu.roll(ti + e[2 * CHAINS - 1], shift, axis=0)))
            entering = [cur]
            for j in range(1, CHAINS):
                tr, ti = _cmul(pr, pi, *cur)
                cur = (tr + e[2 * j - 2], ti + e[2 * j - 1])
                entering.append(cur)

    def step(q, carry, last):
        out, acc = [], carry[2 * CHAINS:]
        for j in range(CHAINS):
            rr = index(j * rc + q)
            sl = _rows8(rr)
            pr, pi = _cmul(ar, ai, carry[2 * j], carry[2 * j + 1])
            nr, ni = pr + xr_ref[sl, :], pi + xi_ref[sl, :]
            xr_ref[sl, :] = nr
            xi_ref[sl, :] = ni
            if visit:
                acc = visit(rr, nr, ni, acc, last and j == CHAINS - 1)
            out += [nr, ni]
        return (*out, *acc)

    def step_block(qb, carry):
        for t in range(SCAN_UNROLL):
            carry = step(qb * SCAN_UNROLL + t, carry, False)
        return carry

    start = tuple(v for pair in entering for v in pair)
    n_blocks = (rc - 1) // SCAN_UNROLL
    carry = lax.fori_loop(0, n_blocks, step_block, (*start, *visit_init))
    for q in range(n_blocks * SCAN_UNROLL, rc - 1):
        carry = step(q, carry, False)
    carry = step(rc - 1, carry, True)
    return entering, carry[2 * CHAINS:]


def _discretise(a_re, a_im, ldt):
    lr = jnp.minimum(a_re, -1e-4)
    li = a_im
    dt = jnp.exp(ldt)
    mag = jnp.exp(lr * dt)
    lbr = mag * jnp.cos(li * dt)
    lbi = mag * jnp.sin(li * dt)
    den = lr * lr + li * li
    nr = lbr - 1.0
    fr = (nr * lr + lbi * li) / den
    fi = (lbi * lr - nr * li) / den
    return lr, li, dt, lbr, lbi, fr, fi


def _lane_row(v):
    return jnp.concatenate([v[g:g + 1, :] for g in range(v.shape[0])], axis=1)


def _ssm_fill_maps(d, prm, tmp_ref, maps):
    a_re_ref, a_im_ref, ldt_ref, bt_re_ref, bt_im_ref, c_re_ref, c_im_ref = prm
    _, _, _, lbr, lbi, fr, fi = _discretise(a_re_ref[d], a_im_ref[d], ldt_ref[d])

    def fill(dst, piece):
        tmp_ref[...] = jnp.zeros_like(tmp_ref)
        for g in range(SLAB_G):
            tmp_ref[g * SSM_H:(g + 1) * SSM_H, g * SSM_P:(g + 1) * SSM_P] = piece(g)
        dst[...] = tmp_ref[...].astype(BF16)

    wbr, wbi, wcr, wci = maps
    fill(wbr, lambda g: fr[g:g + 1] * bt_re_ref[d, g] - fi[g:g + 1] * bt_im_ref[d, g])
    fill(wbi, lambda g: fr[g:g + 1] * bt_im_ref[d, g] + fi[g:g + 1] * bt_re_ref[d, g])
    fill(wcr, lambda g: c_re_ref[d, g])
    fill(wci, lambda g: c_im_ref[d, g])
    return _lane_row(lbr), _lane_row(lbi)


def _ssm_param_specs():
    pole = pl.BlockSpec((2, SLAB_G, SSM_P), lambda j: (0, j, 0))
    step = pl.BlockSpec((2, SLAB_G, 1), lambda j: (0, j, 0))
    mat = pl.BlockSpec((2, SLAB_G, SSM_H, SSM_P), lambda j: (0, j, 0, 0))
    return [pole, pole, step, mat, mat, mat, mat]


_MAP_SCRATCH = [pltpu.VMEM((SLAB, SLAB_S), F32)] + [pltpu.VMEM((SLAB, SLAB_S), BF16)] * 4
_ENT_SPEC = pl.BlockSpec((1, 2, 2 * CHAINS, SEG, SLAB_S), lambda j: (j, 0, 0, 0, 0))
_NT = (((1,), (1,)), ((), ()))


def _ssm_fwd(u, prm, dskip):
    L, C = u.shape
    n_rows = L // SEG
    tc = _pick(L, (2048, 1024, 512, 256))
    u_spec = pl.BlockSpec((L, SLAB), lambda j: (0, j))
    d_spec = pl.BlockSpec((1, SLAB), lambda j: (0, j))

    def body(u_ref, *rest):
        prm_refs, d_ref, y_ref, ent_ref = rest[:7], rest[7], rest[8], rest[9]
        tmp_ref, maps, xr_ref, xi_ref = rest[10], rest[11:15], rest[15], rest[16]
        wbr, wbi, wcr, wci = maps
        y_ref[...] = u_ref[...] * d_ref[...]
        for d in range(2):
            lam_r, lam_i = _ssm_fill_maps(d, prm_refs, tmp_ref, maps)

            def inp(c, _):
                sl = pl.ds(pl.multiple_of(c * tc, tc), tc)
                ub = u_ref[sl, :].astype(BF16)
                xr_ref[sl, :] = jnp.dot(ub, wbr[...], preferred_element_type=F32)
                xi_ref[sl, :] = jnp.dot(ub, wbi[...], preferred_element_type=F32)
                return 0

            lax.fori_loop(0, L // tc, inp, 0)
            ar = jnp.broadcast_to(lam_r, (SEG, SLAB_S))
            ai = jnp.broadcast_to(lam_i, (SEG, SLAB_S))
            entering, _ = _seg_scan(xr_ref, xi_ref, ar, ai, d == 1, n_rows)
            for j, (er, ei) in enumerate(entering):
                ent_ref[0, d, 2 * j] = er
                ent_ref[0, d, 2 * j + 1] = ei

            def outp(c, _):
                sl = pl.ds(pl.multiple_of(c * tc, tc), tc)
                y_ref[sl, :] += (
                    lax.dot_general(xr_ref[sl, :].astype(BF16), wcr[...], _NT, preferred_element_type=F32)
                    - lax.dot_general(xi_ref[sl, :].astype(BF16), wci[...], _NT, preferred_element_type=F32))
                return 0

            lax.fori_loop(0, L // tc, outp, 0)

    return pl.pallas_call(
        body, name="ssm_fwd", grid=(C // SLAB,),
        in_specs=[u_spec] + _ssm_param_specs() + [d_spec],
        out_specs=[u_spec, _ENT_SPEC],
        out_shape=[jax.ShapeDtypeStruct((L, C), F32),
                   jax.ShapeDtypeStruct((C // SLAB, 2, 2 * CHAINS, SEG, SLAB_S), F32)],
        scratch_shapes=_MAP_SCRATCH + [pltpu.VMEM((L, SLAB_S), F32)] * 2,
        compiler_params=_cparams(("arbitrary",)),
    )(u, *prm, dskip)


def _ssm_bwd(u, dy, ent, prm, dskip):
    L, C = u.shape
    n_rows = L // SEG
    n_slab = C // SLAB
    tc = _pick(L, (2048, 1024, 512, 256))
    u_spec = pl.BlockSpec((L, SLAB), lambda j: (0, j))
    d_spec = pl.BlockSpec((1, SLAB), lambda j: (0, j))
    pg_spec = pl.BlockSpec((1, 2, PG_ROWS, SLAB_S), lambda j: (j, 0, 0, 0))

    def body(u_ref, dy_ref, ent_ref, *rest):
        prm_refs, d_ref, du_ref, pg_ref = rest[:7], rest[7], rest[8], rest[9]
        tmp_ref, maps, acc_ref = rest[10], rest[11:15], rest[15]
        xr_ref, xi_ref, gr_ref, gi_ref = rest[16:20]
        wbr, wbi, wcr, wci = maps
        du_ref[...] = dy_ref[...] * d_ref[...]
        pg_ref[...] = jnp.zeros_like(pg_ref)
        pg_ref[0, 0, 66:67, 0:SLAB] = _colsum(dy_ref[...] * u_ref[...])
        for d in range(2):
            lam_r, lam_i = _ssm_fill_maps(d, prm_refs, tmp_ref, maps)

            def inp(c, _):
                sl = pl.ds(pl.multiple_of(c * tc, tc), tc)
                ub = u_ref[sl, :].astype(BF16)
                dyb = dy_ref[sl, :].astype(BF16)
                xr_ref[sl, :] = jnp.dot(ub, wbr[...], preferred_element_type=F32)
                xi_ref[sl, :] = jnp.dot(ub, wbi[...], preferred_element_type=F32)
                gr_ref[sl, :] = jnp.dot(dyb, wcr[...], preferred_element_type=F32)
                gi_ref[sl, :] = -jnp.dot(dyb, wci[...], preferred_element_type=F32)
                return 0

            lax.fori_loop(0, L // tc, inp, 0)
            ar = jnp.broadcast_to(lam_r, (SEG, SLAB_S))
            ai = jnp.broadcast_to(lam_i, (SEG, SLAB_S))
            entering = [(ent_ref[0, d, 2 * j], ent_ref[0, d, 2 * j + 1]) for j in range(CHAINS)]
            _seg_scan(xr_ref, xi_ref, ar, ai, d == 1, n_rows, entering=entering)

            def pole(rr, lr, li, acc, last):
                if last:
                    pr, pi = entering[0]
                else:
                    nb = _rows8(rr + 1 if d == 1 else rr - 1)
                    pr, pi = xr_ref[nb, :], xi_ref[nb, :]
                return acc[0] + lr * pr + li * pi, acc[1] + li * pr - lr * pi

            zero = jnp.zeros((SEG, SLAB_S), F32)
            _, (accr, acci) = _seg_scan(gr_ref, gi_ref, ar, -ai, d == 0, n_rows, pole, (zero, zero))
            pg_ref[0, d, 64:65, :] = _colsum(accr)
            pg_ref[0, d, 65:66, :] = _colsum(acci)

            acc_ref[...] = jnp.zeros_like(acc_ref)

            def outp(c, _):
                sl = pl.ds(pl.multiple_of(c * tc, tc), tc)
                lrb, lib = gr_ref[sl, :].astype(BF16), gi_ref[sl, :].astype(BF16)
                du_ref[sl, :] += (lax.dot_general(lrb, wbr[...], _NT, preferred_element_type=F32)
                                  + lax.dot_general(lib, wbi[...], _NT, preferred_element_type=F32))
                ut = u_ref[sl, :].astype(F32).T.astype(BF16)
                dyt = dy_ref[sl, :].T.astype(BF16)
                acc_ref[0] += jnp.dot(ut, lrb, preferred_element_type=F32)
                acc_ref[1] += jnp.dot(ut, lib, preferred_element_type=F32)
                acc_ref[2] += jnp.dot(dyt, xr_ref[sl, :].astype(BF16), preferred_element_type=F32)
                acc_ref[3] -= jnp.dot(dyt, xi_ref[sl, :].astype(BF16), preferred_element_type=F32)
                return 0

            lax.fori_loop(0, L // tc, outp, 0)
            for m in range(4):
                for g in range(SLAB_G):
                    lanes = slice(g * SSM_P, (g + 1) * SSM_P)
                    pg_ref[0, d, m * SSM_H:(m + 1) * SSM_H, lanes] = acc_ref[m, g * SSM_H:(g + 1) * SSM_H, lanes]

    return pl.pallas_call(
        body, name="ssm_bwd", grid=(n_slab,),
        in_specs=[u_spec, u_spec, _ENT_SPEC] + _ssm_param_specs() + [d_spec],
        out_specs=[u_spec, pg_spec],
        out_shape=[jax.ShapeDtypeStruct((L, C), F32), jax.ShapeDtypeStruct((n_slab, 2, PG_ROWS, SLAB_S), F32)],
        scratch_shapes=_MAP_SCRATCH + [pltpu.VMEM((4, SLAB, SLAB_S), F32)] + [pltpu.VMEM((L, SLAB_S), F32)] * 4,
        compiler_params=_cparams(("arbitrary",), 60 << 20),
    )(u, dy, ent, *prm, dskip)


def _ssm_param_grads(pg, prm):
    n_slab = pg.shape[0]
    G = n_slab * SLAB_G
    pg_spec = pl.BlockSpec((1, 2, PG_ROWS, SLAB_S), lambda j: (j, 0, 0, 0))
    pole, _, step, mat = _ssm_param_specs()[:4]

    def body(pg_ref, a_re_ref, a_im_ref, ldt_ref, bt_re_ref, bt_im_ref,
             dbr_ref, dbi_ref, dcr_ref, dci_ref, dar_ref, dai_ref, dldt_ref, dd_ref):
        dd_ref[...] = pg_ref[0, 0, 66:67, 0:SLAB]
        for d in range(2):
            a_r = a_re_ref[d]
            lr, li, dt, lbr, lbi, f_r, f_i = _discretise(a_r, a_im_ref[d], ldt_ref[d])
            gfr_rows, gfi_rows, glr_rows, gli_rows = [], [], [], []
            for g in range(SLAB_G):
                lanes = slice(g * SSM_P, (g + 1) * SSM_P)
                gbr, gbi = pg_ref[0, d, 0:SSM_H, lanes], pg_ref[0, d, SSM_H:2 * SSM_H, lanes]
                b_r, b_i = bt_re_ref[d, g], bt_im_ref[d, g]
                fr, fi = f_r[g:g + 1], f_i[g:g + 1]
                dbr_ref[d, g] = fr * gbr + fi * gbi
                dbi_ref[d, g] = fr * gbi - fi * gbr
                gfr_rows.append(_colsum(gbr * b_r + gbi * b_i))
                gfi_rows.append(_colsum(gbi * b_r - gbr * b_i))
                dcr_ref[d, g] = pg_ref[0, d, 2 * SSM_H:3 * SSM_H, lanes]
                dci_ref[d, g] = pg_ref[0, d, 3 * SSM_H:4 * SSM_H, lanes]
                glr_rows.append(pg_ref[0, d, 64:65, lanes])
                gli_rows.append(pg_ref[0, d, 65:66, lanes])
            gfr, gfi = jnp.concatenate(gfr_rows, axis=0), jnp.concatenate(gfi_rows, axis=0)
            glr, gli = jnp.concatenate(glr_rows, axis=0), jnp.concatenate(gli_rows, axis=0)
            den = lr * lr + li * li
            ir, ii = lr / den, -li / den
            tr, ti = _cmul(ir, -ii, gfr, gfi)
            glbr, glbi = glr + tr, gli + ti
            qr, qi = _cmul(f_r, f_i, ir, ii)
            dlr, dli = _cmul(-qr, qi, gfr, gfi)
            zr, zi = _cmul(lbr, -lbi, glbr, glbi)
            dlr = dlr + dt * zr
            dli = dli + dt * zi
            dar_ref[d] = jnp.where(a_r < -1e-4, dlr, jnp.where(a_r == -1e-4, 0.5 * dlr, 0.0))
            dai_ref[d] = dli
            dldt_ref[d] = jnp.sum(lr * zr + li * zi, axis=-1, keepdims=True) * dt

    a_re, a_im, ldt, bt_re, bt_im = prm[:5]
    mshape = jax.ShapeDtypeStruct(bt_re.shape, F32)
    pshape = jax.ShapeDtypeStruct(a_re.shape, F32)
    return pl.pallas_call(
        body, name="ssm_param_grads", grid=(n_slab,),
        in_specs=[pg_spec, pole, pole, step, mat, mat],
        out_specs=[mat, mat, mat, mat, pole, pole, step, pl.BlockSpec((1, SLAB), lambda j: (0, j))],
        out_shape=[mshape, mshape, mshape, mshape, pshape, pshape, jax.ShapeDtypeStruct(ldt.shape, F32),
                   jax.ShapeDtypeStruct((1, n_slab * SLAB), F32)],
        compiler_params=_cparams(("arbitrary",)),
    )(pg, a_re, a_im, ldt, bt_re, bt_im)


def _peer(k, x, y, c):
    return (1 - x if k & 4 else x, 1 - y if k & 2 else y, 1 - c if k & 1 else c)


def _dev_index(pos):
    return 4 * pos[0] + 2 * pos[1] + pos[2]


def _chip_index(pos):
    return 2 * pos[0] + pos[1]


def _sibling_swap(x, name):
    any_spec = pl.BlockSpec(memory_space=pl.ANY)

    def body(x_ref, out_ref, send_sems, recv_sems):
        x_, y_, c_ = lax.axis_index("x"), lax.axis_index("y"), lax.axis_index("c")
        copies = [pltpu.make_async_remote_copy(
            src_ref=x_ref.at[2 * chip + (1 - c_)], dst_ref=out_ref.at[chip], send_sem=send_sems.at[chip],
            recv_sem=recv_sems.at[chip], device_id=(x_, y_, 1 - c_), device_id_type=pl.DeviceIdType.MESH)
            for chip in range(N_CHIPS)]
        for cp in copies:
            cp.start()
        for cp in copies:
            cp.wait()

    return pl.pallas_call(
        body, name=name, out_shape=jax.ShapeDtypeStruct((N_CHIPS,) + x.shape[1:], x.dtype),
        in_specs=[any_spec], out_specs=any_spec,
        scratch_shapes=[pltpu.SemaphoreType.DMA((N_CHIPS,)), pltpu.SemaphoreType.DMA((N_CHIPS,))],
    )(x)


def _pair_sum(x, got, name):
    n, R, W = got.shape
    tr = _row_tile(R, W, 5 << 20)
    core = lax.axis_index("c").astype(jnp.int32).reshape(1)

    def body(core_ref, a_ref, b_ref, o_ref):
        o_ref[...] = (a_ref[...].astype(F32) + b_ref[...].astype(F32)).astype(BF16)

    spec = pl.BlockSpec((1, tr, W), lambda i, j, c: (i, j, 0))
    grid_spec = pltpu.PrefetchScalarGridSpec(
        num_scalar_prefetch=1, grid=(n, R // tr),
        in_specs=[pl.BlockSpec((1, tr, W), lambda i, j, c: (2 * i + c[0], j, 0)), spec], out_specs=spec)
    return pl.pallas_call(
        body, name=name, grid_spec=grid_spec, out_shape=jax.ShapeDtypeStruct((n, R, W), BF16),
        compiler_params=_cparams(("arbitrary", "arbitrary")),
    )(core, x, got)


def _all_gather(xs, name):
    n = len(xs)
    any_spec = pl.BlockSpec(memory_space=pl.ANY)

    def body(*refs):
        x_refs, out_refs = refs[:n], refs[n:2 * n]
        send_sems, recv_sems, local_sems = refs[2 * n:]
        x, y, c = lax.axis_index("x"), lax.axis_index("y"), lax.axis_index("c")
        me, sibling = (x, y, c), (x, y, 1 - c)
        chips = [(1 - x, y), (x, 1 - y), (1 - x, 1 - y)]

        def copy(a, k, block, to, src=None):
            dst = out_refs[a].at[_dev_index(block)]
            return pltpu.make_async_remote_copy(
                src_ref=dst if src is None else src, dst_ref=dst,
                send_sem=send_sems.at[a, k], recv_sem=recv_sems.at[a, k],
                device_id=to, device_id_type=pl.DeviceIdType.MESH)

        mine = [pltpu.make_async_copy(x_refs[a], out_refs[a].at[_dev_index(me)], local_sems.at[a]) for a in range(n)]
        for cp in mine:
            cp.start()
        first = []
        for a in range(n):
            first.append(copy(a, 0, me, sibling, src=x_refs[a]))
            first += [copy(a, 1 + j, me, (*chip, c), src=x_refs[a]) for j, chip in enumerate(chips)]
        for cp in first:
            cp.start()
        passed = []
        for j, chip in enumerate(chips):
            for a in range(n):
                copy(a, 1 + j, (*chip, c), me).wait_recv()
                fwd = copy(a, 4 + j, (*chip, c), sibling)
                fwd.start()
                passed.append(fwd)
        for a in range(n):
            copy(a, 0, sibling, me).wait_recv()
            for j, chip in enumerate(chips):
                copy(a, 4 + j, (*chip, 1 - c), me).wait_recv()
        for cp in first + passed:
            cp.wait_send()
        for cp in mine:
            cp.wait()

    return pl.pallas_call(
        body, name=name,
        out_shape=[jax.ShapeDtypeStruct((N_DEV,) + v.shape, v.dtype) for v in xs],
        in_specs=[any_spec] * n, out_specs=[any_spec] * n,
        scratch_shapes=[pltpu.SemaphoreType.DMA((n, 7)), pltpu.SemaphoreType.DMA((n, 7)),
                        pltpu.SemaphoreType.DMA((n,))],
    )(*xs)


def _sum_blocks(x, name):
    _, R, W = x.shape

    def body(x_ref, o_ref):
        acc = x_ref[0].astype(F32)
        for d in range(1, N_DEV):
            acc = acc + x_ref[d].astype(F32)
        o_ref[...] = acc

    return pl.pallas_call(body, name=name, out_shape=jax.ShapeDtypeStruct((R, W), F32),
                          compiler_params=pltpu.CompilerParams(vmem_limit_bytes=VMEM_LIMIT))(x)


def _adam_update(w, g, m, v):
    mn = ADAM_B1 * m + (1.0 - ADAM_B1) * g
    vn = ADAM_B2 * v + (1.0 - ADAM_B2) * (g * g)
    m_hat = mn / (1.0 - ADAM_B1 ** ADAM_STEP)
    v_hat = vn / (1.0 - ADAM_B2 ** ADAM_STEP)
    return -ADAM_LR * (m_hat / (jnp.sqrt(v_hat) + ADAM_EPS) + ADAM_WD * w), mn, vn


def _row_tile(R, W, budget):
    padded = -(-W // LANES) * LANES * 4
    if R * padded <= budget:
        return R
    return _pick(R, [t for t in (2048, 1024, 512, 256, 128, 64, 32, 16, 8) if t * padded <= budget])


def _adamw(w, g, m, v, name):
    R, W = w.shape
    tr = _row_tile(R, W, 1 << 20)

    def body(w_ref, g_ref, m_ref, v_ref, d_ref, mo_ref, vo_ref):
        d_ref[...], mo_ref[...], vo_ref[...] = _adam_update(w_ref[...], g_ref[...], m_ref[...], v_ref[...])

    spec = pl.BlockSpec((tr, W), lambda i: (i, 0))
    shp = jax.ShapeDtypeStruct((R, W), F32)
    return pl.pallas_call(
        body, name=name, grid=(R // tr,), in_specs=[spec] * 4, out_specs=[spec] * 3, out_shape=[shp] * 3,
        compiler_params=_cparams(("arbitrary",)),
    )(w, g, m, v)


def _adamw_group(ws, gs, ms, vs, name):
    n = len(ws)

    def body(*refs):
        ins, outs = refs[:4 * n], refs[4 * n:]
        for i in range(n):
            w_ref, g_ref, m_ref, v_ref = ins[i], ins[n + i], ins[2 * n + i], ins[3 * n + i]
            outs[i][...], outs[n + i][...], outs[2 * n + i][...] = _adam_update(
                w_ref[...], g_ref[...], m_ref[...], v_ref[...])

    shapes = [jax.ShapeDtypeStruct(w.shape, F32) for w in ws]
    outs = pl.pallas_call(body, name=name, out_shape=shapes * 3)(*ws, *gs, *ms, *vs)
    return outs[:n], outs[n:2 * n], outs[2 * n:]


def _adamw_reduce(w, land, m, v, name):
    R, W = w.shape
    n = land.shape[0]
    tr = _row_tile(R, W, 1 << 20)

    def body(w_ref, l_ref, m_ref, v_ref, g_ref, d_ref, mo_ref, vo_ref):
        g = l_ref[0].astype(F32)
        for d in range(1, n):
            g = g + l_ref[d].astype(F32)
        g_ref[...] = g
        d_ref[...], mo_ref[...], vo_ref[...] = _adam_update(w_ref[...], g, m_ref[...], v_ref[...])

    spec = pl.BlockSpec((tr, W), lambda i: (i, 0))
    lspec = pl.BlockSpec((n, tr, W), lambda i: (0, i, 0))
    shp = jax.ShapeDtypeStruct((R, W), F32)
    return pl.pallas_call(
        body, name=name, grid=(R // tr,), in_specs=[spec, lspec, spec, spec], out_specs=[spec] * 4,
        out_shape=[shp] * 4, compiler_params=_cparams(("arbitrary",)),
    )(w, land, m, v)


def _gelu(v):
    c = math.sqrt(2.0 / math.pi)
    return 0.5 * v * (1.0 + jnp.tanh(c * (v + 0.044715 * v * v * v)))


def _gelu_grad(v):
    c = math.sqrt(2.0 / math.pi)
    t = jnp.tanh(c * (v + 0.044715 * v * v * v))
    return 0.5 * (1.0 + t) + 0.5 * v * (1.0 - t * t) * c * (1.0 + 3.0 * 0.044715 * v * v)


def kernel(x, p, norm_mix, w_in, q_norm, k_norm, ssm_a_re, ssm_a_im, ssm_log_dt, ssm_b_re, ssm_b_im, ssm_c_re, ssm_c_im, ssm_d, w_glu, b_glu, w_out, norm_ple, w_ple_gate, w_ple_proj, norm_final, loss_target, m_norm_mix, m_w_in, m_q_norm, m_k_norm, m_ssm_a_re, m_ssm_a_im, m_ssm_log_dt, m_ssm_b_re, m_ssm_b_im, m_ssm_c_re, m_ssm_c_im, m_ssm_d, m_w_glu, m_b_glu, m_w_out, m_norm_ple, m_w_ple_gate, m_w_ple_proj, m_norm_final, v_norm_mix, v_w_in, v_q_norm, v_k_norm, v_ssm_a_re, v_ssm_a_im, v_ssm_log_dt, v_ssm_b_re, v_ssm_b_im, v_ssm_c_re, v_ssm_c_im, v_ssm_d, v_w_glu, v_b_glu, v_w_out, v_norm_ple, v_w_ple_gate, v_w_ple_proj, v_norm_final):
    L, D = x.shape[1], x.shape[2]
    D_SSM = ssm_d.shape[1]
    G = D_SSM // SSM_H
    n_slab = D_SSM // SLAB
    n_in = w_in.shape[2]
    D_IN = n_in * N_DEV
    n_pp = w_ple_proj.shape[2]
    n_glu = w_glu.shape[2]
    xs = x[0]
    ps = p[0, 0]
    tgt = loss_target[0]

    (win_t3,) = _all_gather([w_in[0].T.astype(BF16)], "gather_w_in")
    win_t = win_t3.reshape(D_IN, D)
    later_weights = _Carry("gather", [w_glu[0].astype(BF16), w_out[0].astype(BF16), w_ple_gate[0].astype(BF16),
                                      w_ple_proj[0].astype(BF16)])

    ssm_prm = (ssm_a_re[0], ssm_a_im[0], ssm_log_dt[0].reshape(2, G, 1),
               ssm_b_re[0].transpose(0, 1, 3, 2), ssm_b_im[0].transpose(0, 1, 3, 2), ssm_c_re[0], ssm_c_im[0])

    cos, sin = _rope_tables(L)
    hn = _norm_in(xs, norm_mix, "norm_mix")
    ZT = 512
    zp_tile = lambda j: jnp.where(j < 2, j, jnp.where(j < D_IN // ZT - 1, j + 1, 2))
    z = _mm(hn, win_t, "nt", "in_proj", out_dtype=BF16, n_tiles=(ZT, zp_tile))
    qr, kr, vb, kt = _qkv_prep(z, cos, sin, q_norm, k_norm)
    o, lse, (wglu3, wout3, wpg3, wpp3) = _attn_fwd(qr, kr, vb, later_weights)
    wout = wout3.reshape(-1, D)
    wpg = wpg3.reshape(-1, D)
    u_off = 2 * D_ATTN
    u_perm = _seg_perm(z[:, u_off:u_off + D_SSM])
    ys_perm, ssm_ent = _ssm_fwd(u_perm, ssm_prm, ssm_d)
    ys = _seg_unperm(ys_perm)

    tm = _pick(L, (256,))

    def gelu_body(y_ref, o_ref):
        o_ref[...] = _gelu(y_ref[...]).astype(BF16)

    (gy,) = _rowcall(gelu_body, "gelu", L, tm, [(ys, _rspec(tm, D_SSM))], [(D_SSM, BF16)])
    glu = _mm(gy, wglu3, "nn", "glu_proj", out_dtype=BF16, bias=b_glu, b_blk=True)

    def mix_body(o_ref, ga_ref, gla_ref, glb_ref, gs_ref, cat_ref):
        ga, gs = _f32(ga_ref), _f32(gs_ref)
        cat_ref[:, :D_ATTN] = (o_ref[...] * ga * _sigmoid(ga)).astype(BF16)
        cat_ref[:, D_ATTN:] = (_f32(gla_ref) * _sigmoid(_f32(glb_ref)) * gs * _sigmoid(gs)).astype(BF16)

    (cat,) = _rowcall(mix_body, "mix", L, tm,
                      [(o, _rspec(tm, D_ATTN)), (z, _rspec(tm, D_ATTN, 1)), (glu, _rspec(tm, D_SSM, 0)),
                       (glu, _rspec(tm, D_SSM, 1)), (z, _rspec(tm, D_SSM, 3))], [(D_ATTN + D_SSM, BF16)])
    h1 = _mm(cat, wout, "nn", "out_proj", add=xs)
    n2 = _norm_in(h1, norm_ple, "norm_ple")
    gpre = _mm(n2, wpg, "nn", "ple_gate", out_dtype=BF16)
    pb = ps.astype(BF16)
    pp = _mm(pb, wpp3, "nn", "ple_proj", out_dtype=BF16, b_blk=True)

    nf = norm_final.reshape(1, D)

    def tail_body(h1_ref, gp_ref, pp_ref, t_ref, g_ref, dh2_ref, dpp_ref, dsg_ref, loss_ref, dg_ref):
        gate = _sigmoid(_f32(gp_ref))
        ppv = _f32(pp_ref)
        h2 = h1_ref[...] + gate * ppv
        r = _rms(h2)
        hh = h2 * r
        err = hh * g_ref[...] - t_ref[...]
        _acc(loss_ref, jnp.broadcast_to(0.5 * jnp.sum(jnp.mean(err * err, axis=-1, keepdims=True)), loss_ref.shape))
        dy = err * (1.0 / D)
        _acc(dg_ref, _colsum(dy * hh))
        dh2 = _rms_bwd(dy, hh, r, g_ref[...])
        dh2_ref[...] = dh2
        dpp_ref[...] = (dh2 * gate).astype(BF16)
        dsg_ref[...] = (dh2 * ppv * gate * (1.0 - gate)).astype(BF16)

    dh2, dpp, dsg, loss_acc, d_nf = _rowcall(
        tail_body, "tail", L, tm,
        [(h1, _rspec(tm, D)), (gpre, _rspec(tm, D)), (pp, _rspec(tm, D)), (tgt, _rspec(tm, D)), (nf, _fspec(nf.shape))],
        [(D, F32), (D, BF16), (D, BF16)], [(1, LANES), (1, D)])

    g_wpp3 = _mm(pb, dpp, "tn", "d_ple_proj", out_dtype=BF16, out_blk=n_pp)
    g_wpg = _mm(n2, dsg, "tn", "d_ple_gate", out_dtype=BF16)
    dn2 = _mm(dsg, wpg, "nt", "d_norm_ple_in", out_dtype=BF16)

    def ple_bwd_body(h1_ref, dn_ref, dh2_ref, g_ref, dh1_ref, dh1b_ref, dg_ref):
        h1v = h1_ref[...]
        r = _rms(h1v)
        hh = h1v * r
        dn = _f32(dn_ref)
        _acc(dg_ref, _colsum(dn * hh))
        dh1 = dh2_ref[...] + _rms_bwd(dn, hh, r, g_ref[...])
        dh1_ref[...] = dh1
        dh1b_ref[...] = dh1.astype(BF16)

    dh1, dh1b, d_nple = _rowcall(
        ple_bwd_body, "ple_bwd", L, tm,
        [(h1, _rspec(tm, D)), (dn2, _rspec(tm, D)), (dh2, _rspec(tm, D)), (norm_ple, _fspec(norm_ple.shape))],
        [(D, F32), (D, BF16)], [(1, D)])

    dcat = _mm(dh1b, wout, "nt", "d_cat", out_dtype=BF16)
    g_wout = _mm(cat, dh1b, "tn", "d_out_proj", out_dtype=BF16)

    def mix_bwd_body(dca_ref, dcs_ref, o_ref, ga_ref, gla_ref, glb_ref, gs_ref,
                     do_ref, dga_ref, dgs_ref, dglu_ref, db_ref):
        dca, dcs, ga, gs = _f32(dca_ref), _f32(dcs_ref), _f32(ga_ref), _f32(gs_ref)
        sa, ss, sb = _sigmoid(ga), _sigmoid(gs), _sigmoid(_f32(glb_ref))
        gla = _f32(gla_ref)
        do_ref[...] = (dca * ga * sa).astype(BF16)
        dga_ref[...] = (dca * o_ref[...] * sa * (1.0 + ga * (1.0 - sa))).astype(BF16)
        dgs_ref[...] = (dcs * gla * sb * ss * (1.0 + gs * (1.0 - ss))).astype(BF16)
        dy2 = dcs * gs * ss
        da, db = dy2 * sb, dy2 * gla * sb * (1.0 - sb)
        dglu_ref[:, :D_SSM] = da.astype(BF16)
        dglu_ref[:, D_SSM:] = db.astype(BF16)
        _acc(db_ref, jnp.concatenate([_colsum(da), _colsum(db)], axis=-1))

    do, dga, dgs, dglu, g_bglu = _rowcall(
        mix_bwd_body, "mix_bwd", L, tm,
        [(dcat, _rspec(tm, D_ATTN, 0)), (dcat, _rspec(tm, D_SSM, 1)), (o, _rspec(tm, D_ATTN)),
         (z, _rspec(tm, D_ATTN, 1)), (glu, _rspec(tm, D_SSM, 0)), (glu, _rspec(tm, D_SSM, 1)),
         (z, _rspec(tm, D_SSM, 3))],
        [(D_ATTN, BF16), (D_ATTN, BF16), (D_SSM, BF16), (2 * D_SSM, BF16)], [(1, 2 * D_SSM)])

    g_wglu3 = _mm(gy, dglu, "tn", "d_glu_proj", out_dtype=BF16, out_blk=n_glu)
    dys = _mm(dglu, wglu3, "nt", "d_ssm_out", b_blk=True,
              post=(lambda out, y: out * _gelu_grad(y), ys))
    du_perm, pg = _ssm_bwd(u_perm, _seg_perm(dys), ssm_ent, ssm_prm, ssm_d)
    du = _seg_unperm(du_perm)

    pg_send = pg.reshape(N_DEV, (n_slab // N_DEV) * 2 * PG_ROWS, SLAB_S)
    dqs, dkr, dvv, (l_wglu, l_wout, l_wpg, l_wpp, l_pg) = _attn_bwd(
        qr, kr, vb, kt, do, o, lse,
        _Carry("a2a", [g_wglu3, g_wout.reshape(N_DEV, -1, D), g_wpg.reshape(N_DEV, -1, D), g_wpp3, pg_send]))

    scale = HEAD_DIM ** -0.5
    kblk = 4 * D_ATTN // D_KV

    a0, k0, v0, u0, s0 = D_ATTN + 2 * D_KV, D_ATTN, D_ATTN + D_KV, 2 * D_ATTN + 2 * D_KV, 2 * D_ATTN + 2 * D_KV + D_SSM

    def qkv_bwd_body(dq_ref, dk_ref, dv_ref, q_ref, k_ref, cos_ref, sin_ref, qn_ref, kn_ref, dga_ref, du_ref, dgs_ref,
                     dz_ref, dqn_ref, dkn_ref):
        c, s = cos_ref[...], sin_ref[...]
        dz_ref[:, a0:a0 + D_ATTN] = dga_ref[...]
        dz_ref[:, u0:u0 + D_SSM] = du_ref[...].astype(BF16)
        dz_ref[:, s0:s0 + D_SSM] = dgs_ref[...]

        def head(g, xh, w):
            dn = g * c + _partner(g * s)
            r = _rms(xh)
            xhat = xh * r
            return _rms_bwd(dn, xhat, r, w), _colsum(dn * xhat)

        dqn = jnp.zeros((1, HEAD_DIM), F32)
        for h in range(N_HEADS):
            sl = slice(h * HEAD_DIM, (h + 1) * HEAD_DIM)
            dx, dw = head(dq_ref[:, sl] * scale, q_ref[:, sl].astype(F32), qn_ref[...])
            dz_ref[:, sl] = dx.astype(BF16)
            dqn = dqn + dw
        dkn = jnp.zeros((1, HEAD_DIM), F32)
        for h in range(N_KV):
            sl = slice(h * HEAD_DIM, (h + 1) * HEAD_DIM)
            dx, dw = head(dk_ref[:, sl], k_ref[:, sl].astype(F32), kn_ref[...])
            dz_ref[:, k0 + h * HEAD_DIM:k0 + (h + 1) * HEAD_DIM] = dx.astype(BF16)
            dkn = dkn + dw
        dz_ref[:, v0:v0 + D_KV] = dv_ref[...].astype(BF16)
        _acc(dqn_ref, dqn)
        _acc(dkn_ref, dkn)

    dz, g_qn, g_kn = _rowcall(
        qkv_bwd_body, "qkv_bwd", L, tm,
        [(dqs, _rspec(tm, D_ATTN)), (dkr, _rspec(tm, D_KV)), (dvv, _rspec(tm, D_KV)),
         (z, _rspec(tm, D_ATTN, 0)), (z, _rspec(tm, D_KV, kblk)), (cos, _rspec(tm, HEAD_DIM)),
         (sin, _rspec(tm, HEAD_DIM)), (q_norm, _fspec(q_norm.shape)), (k_norm, _fspec(k_norm.shape)),
         (dga, _rspec(tm, D_ATTN)), (du, _rspec(tm, D_SSM)), (dgs, _rspec(tm, D_SSM))],
        [(D_IN, BF16)], [(1, HEAD_DIM), (1, HEAD_DIM)])

    g_win_t = _mm(dz, hn, "tn", "d_in_proj", out_dtype=BF16)
    g_win8 = g_win_t.reshape(N_DEV, n_in, D)
    from_sibling = _sibling_swap(g_win8, "swap_d_w_in")
    pair = _pair_sum(g_win8, from_sibling, "pair_sum_d_w_in")
    dhn, (l_win_t,) = _mm(dz, win_t, "nn", "d_norm_mix_in", out_dtype=BF16,
                          carry=_Carry("a2a_chips", [pair]))

    def in_bwd_body(x_ref, dn_ref, dh1_ref, g_ref, dx_ref, dg_ref):
        xv = x_ref[...]
        r = _rms(xv)
        hh = xv * r
        dn = _f32(dn_ref)
        _acc(dg_ref, _colsum(dn * hh))
        dx_ref[...] = dh1_ref[...] + _rms_bwd(dn, hh, r, g_ref[...])

    grad_x, g_nmix = _rowcall(
        in_bwd_body, "in_bwd", L, tm,
        [(xs, _rspec(tm, D)), (dhn, _rspec(tm, D)), (dh1, _rspec(tm, D)), (norm_mix, _fspec(norm_mix.shape))],
        [(D, F32)], [(1, D)])

    tiny_parts = [g_nmix, g_bglu, d_nple, d_nf, g_qn, g_kn, loss_acc[:, :1]]
    tiny_flat = jnp.concatenate([t.reshape(-1) for t in tiny_parts])
    tiny_rows = -(-tiny_flat.shape[0] // (8 * LANES)) * 8
    tiny = jnp.pad(tiny_flat, (0, tiny_rows * LANES - tiny_flat.shape[0])).reshape(tiny_rows, LANES)
    pg_sum = _sum_blocks(l_pg, "sum_ssm_grads")
    pg_all, tiny_all = _all_gather([pg_sum, tiny], "gather_small_grads")
    (g_bt_re, g_bt_im, g_c_re, g_c_im, g_a_re, g_a_im, g_ldt, g_skip) = _ssm_param_grads(
        pg_all.reshape(n_slab, 2, PG_ROWS, SLAB_S), ssm_prm)
    tiny_sum = _sum_blocks(tiny_all, "sum_tiny_grads").reshape(-1)
    tiny_grads, off = [], 0
    for t in tiny_parts:
        tiny_grads.append(tiny_sum[off:off + t.size].reshape(t.shape))
        off += t.size
    r_nmix, r_bglu, r_nple, r_nf, r_qn, r_kn, loss = tiny_grads
    loss = loss.reshape(())

    grads, deltas, new_ms, new_vs = {}, {}, {}, {}
    outs = _adamw_reduce(w_in[0].T, l_win_t, m_w_in[0].T, v_w_in[0].T, "adamw_w_in")
    grads["w_in"], deltas["w_in"], new_ms["w_in"], new_vs["w_in"] = [t.T[None] for t in outs]
    big = [("w_glu", w_glu, l_wglu, m_w_glu, v_w_glu),
           ("w_out", w_out, l_wout, m_w_out, v_w_out), ("w_ple_gate", w_ple_gate, l_wpg, m_w_ple_gate, v_w_ple_gate),
           ("w_ple_proj", w_ple_proj, l_wpp, m_w_ple_proj, v_w_ple_proj)]
    for name, w, ld, m, v in big:
        shp = w.shape
        outs = _adamw_reduce(w[0], ld, m[0], v[0], "adamw_" + name)
        grads[name], deltas[name], new_ms[name], new_vs[name] = [t.reshape(shp) for t in outs]
    bt2 = (2 * G * SSM_H, SSM_P)
    for name, w, g, m, v in (("ssm_b_re", ssm_b_re, g_bt_re, m_ssm_b_re, v_ssm_b_re),
                             ("ssm_b_im", ssm_b_im, g_bt_im, m_ssm_b_im, v_ssm_b_im)):
        to2 = lambda t: t[0].transpose(0, 1, 3, 2).reshape(bt2)
        back = lambda t: t.reshape(2, G, SSM_H, SSM_P).transpose(0, 1, 3, 2)[None]
        outs = _adamw(to2(w), g.reshape(bt2), to2(m), to2(v), "adamw_" + name)
        grads[name] = back(g)
        deltas[name], new_ms[name], new_vs[name] = [back(t) for t in outs]
    small = [("norm_mix", norm_mix, r_nmix, m_norm_mix, v_norm_mix, (1, D)),
             ("q_norm", q_norm, r_qn, m_q_norm, v_q_norm, (1, HEAD_DIM)),
             ("k_norm", k_norm, r_kn, m_k_norm, v_k_norm, (1, HEAD_DIM)),
             ("ssm_a_re", ssm_a_re, g_a_re, m_ssm_a_re, v_ssm_a_re, (2 * G, SSM_P)),
             ("ssm_a_im", ssm_a_im, g_a_im, m_ssm_a_im, v_ssm_a_im, (2 * G, SSM_P)),
             ("ssm_log_dt", ssm_log_dt, g_ldt, m_ssm_log_dt, v_ssm_log_dt, (2, G)),
             ("ssm_c_re", ssm_c_re, g_c_re, m_ssm_c_re, v_ssm_c_re, (2 * G * SSM_H, SSM_P)),
             ("ssm_c_im", ssm_c_im, g_c_im, m_ssm_c_im, v_ssm_c_im, (2 * G * SSM_H, SSM_P)),
             ("ssm_d", ssm_d, g_skip, m_ssm_d, v_ssm_d, (1, D_SSM)),
             ("b_glu", b_glu, r_bglu, m_b_glu, v_b_glu, (1, 2 * D_SSM)),
             ("norm_ple", norm_ple, r_nple, m_norm_ple, v_norm_ple, (1, D)),
             ("norm_final", norm_final, r_nf, m_norm_final, v_norm_final, (1, D))]
    group = [it for it in small if it[5][0] * it[5][1] <= (1 << 14)]
    grouped = {it[0] for it in group}
    for name, w, g, m, v, s2 in small:
        if name in grouped:
            continue
        shp = w.shape
        outs = _adamw(w.reshape(s2), g.reshape(s2), m.reshape(s2), v.reshape(s2), "adamw_" + name)
        grads[name] = g.reshape(shp)
        deltas[name], new_ms[name], new_vs[name] = [t.reshape(shp) for t in outs]
    ds, mns, vns = _adamw_group(*[[it[i].reshape(it[5]) for it in group] for i in (1, 2, 3, 4)], "adamw_tiny")
    for (name, w, g, _, _, _), d_, m_, v_ in zip(group, ds, mns, vns):
        shp = w.shape
        grads[name] = g.reshape(shp)
        deltas[name], new_ms[name], new_vs[name] = d_.reshape(shp), m_.reshape(shp), v_.reshape(shp)

    order = ["norm_mix", "w_in", "q_norm", "k_norm", "ssm_a_re", "ssm_a_im", "ssm_log_dt", "ssm_b_re", "ssm_b_im",
             "ssm_c_re", "ssm_c_im", "ssm_d", "w_glu", "b_glu", "w_out", "norm_ple", "w_ple_gate", "w_ple_proj",
             "norm_final"]
    return (loss, grad_x[None], *[grads[k] for k in order], *[deltas[k] for k in order],
            *[new_ms[k] for k in order], *[new_vs[k] for k in order])
```

```python
import functools
import math

import numpy as np
import jax
import jax.numpy as jnp
from jax import lax
from jax.experimental import pallas as pl
from jax.experimental.pallas import tpu as pltpu

F32 = jnp.float32
BF16 = jnp.bfloat16

N_DEV = 8
N_CHIPS = 4
EPS = 1e-6
GRID_W = 64
ROPE_THETA = 10000.0
HEAD_DIM = 128
N_HEADS = 8
N_KV = 2
REP = N_HEADS // N_KV
D_ATTN = N_HEADS * HEAD_DIM
D_KV = N_KV * HEAD_DIM
SSM_H = 16
SSM_P = 64
SLAB = 128
SLAB_G = SLAB // SSM_H
SLAB_S = SLAB_G * SSM_P
SEG = 8
CHAINS = 2
SCAN_UNROLL = 8
LANES = 128
PG_ROWS = 72
VMEM_LIMIT = 48 << 20

ADAM_LR = 0.001
ADAM_B1 = 0.9
ADAM_B2 = 0.999
ADAM_EPS = 1e-08
ADAM_WD = 0.01
ADAM_STEP = 10


def _pick(n, cands):
    for c in cands:
        if n % c == 0:
            return c
    return n


def _cparams(sem, vmem=VMEM_LIMIT):
    return pltpu.CompilerParams(dimension_semantics=sem, vmem_limit_bytes=vmem)


class _Carry:
    def __init__(self, kind, xs):
        self.kind, self.xs, self.n = kind, list(xs), len(xs)
        self.ks = (2, 4, 6) if kind == "a2a_chips" else tuple(range(1, N_DEV))
        self.index = _chip_index if kind == "a2a_chips" else _dev_index
        lead = (N_DEV,) if kind == "gather" else ()
        self.out_shape = [jax.ShapeDtypeStruct(lead + v.shape, v.dtype) for v in xs]
        self.specs = [pl.BlockSpec(memory_space=pl.ANY)] * self.n
        self.scratch = [pltpu.SemaphoreType.DMA((self.n, len(self.ks))), pltpu.SemaphoreType.DMA((self.n, len(self.ks))),
                        pltpu.SemaphoreType.DMA((self.n,))]

    def _copies(self, x_refs, out_refs, sems):
        send_sems, recv_sems, local_sems = sems
        x, y, c = lax.axis_index("x"), lax.axis_index("y"), lax.axis_index("c")
        me = self.index((x, y, c))
        mine, sends, arrivals = [], [], []
        for a in range(self.n):
            src_mine = x_refs[a] if self.kind == "gather" else x_refs[a].at[me]
            mine.append(pltpu.make_async_copy(src_mine, out_refs[a].at[me], local_sems.at[a]))
            for s, k in enumerate(self.ks):
                peer = _peer(k, x, y, c)
                src = x_refs[a] if self.kind == "gather" else x_refs[a].at[self.index(peer)]
                sends.append(pltpu.make_async_remote_copy(
                    src_ref=src, dst_ref=out_refs[a].at[me], send_sem=send_sems.at[a, s],
                    recv_sem=recv_sems.at[a, s], device_id=peer, device_id_type=pl.DeviceIdType.MESH))
                land = out_refs[a].at[self.index(peer)]
                arrivals.append(pltpu.make_async_remote_copy(
                    src_ref=land, dst_ref=land, send_sem=send_sems.at[a, s],
                    recv_sem=recv_sems.at[a, s], device_id=peer, device_id_type=pl.DeviceIdType.MESH))
        return mine, sends, arrivals

    def start(self, x_refs, out_refs, sems):
        mine, sends, _ = self._copies(x_refs, out_refs, sems)
        for cp in mine + sends:
            cp.start()

    def wait(self, x_refs, out_refs, sems):
        mine, sends, arrivals = self._copies(x_refs, out_refs, sems)
        for cp in arrivals:
            cp.wait_recv()
        for cp in sends:
            cp.wait_send()
        for cp in mine:
            cp.wait()


def _grid_edges(grid):
    first = functools.reduce(lambda p, q: p & q, [pl.program_id(d) == 0 for d in range(len(grid))])
    last = functools.reduce(lambda p, q: p & q, [pl.program_id(d) == g - 1 for d, g in enumerate(grid)])
    return first, last


def _mm(a, b, mode, name, out_dtype=F32, add=None, bias=None, a_blk=False, b_blk=False, out_blk=0, carry=None,
        n_tiles=None, post=None):
    w = b.shape[2] if b_blk else out_blk
    if mode == "nn":
        M, K = a.shape
        N = b.shape[0] * w if b_blk else b.shape[1]
    elif mode == "nt":
        M = a.shape[1] if a_blk else a.shape[0]
        N = b.shape[1] if b_blk else b.shape[0]
        K = b.shape[0] * w if b_blk else b.shape[1]
    else:
        K, M = a.shape
        N = b.shape[0] * w if b_blk else b.shape[1]
    tm = _pick(M, (1024, 768, 512, 256))
    tn = _pick(N, (1024, 768, 512, 256))
    if mode == "tn" and N <= 2048:
        tn = N
    tk = K if (mode != "tn" and K <= 2048) else _pick(K, (1024, 768, 512, 256))
    if mode == "nn" and K > 2048 and N <= 2048:
        tn, tk = N, _pick(K, (1536, 1024, 768, 512, 256))
    perm = lambda j: j
    if n_tiles:
        tn, perm = n_tiles
        tm = _pick(M, (2048, 1024, 512, 256))
    if mode == "nt" and b_blk:
        tk = w
    elif b_blk or out_blk:
        tn = w
    nk = K // tk
    grid = (M // tm, N // tn, nk)
    if mode == "nn":
        a_spec = pl.BlockSpec((tm, tk), lambda i, j, k: (i, k))
        b_spec = (pl.BlockSpec((1, tk, tn), lambda i, j, k: (j, k, 0)) if b_blk
                  else pl.BlockSpec((tk, tn), lambda i, j, k: (k, j)))
        dims = (((1,), (0,)), ((), ()))
    elif mode == "nt":
        a_spec = (pl.BlockSpec((1, tm, tk), lambda i, j, k: (k, i, 0)) if a_blk
                  else pl.BlockSpec((tm, tk), lambda i, j, k: (i, k)))
        b_spec = (pl.BlockSpec((1, tn, tk), lambda i, j, k: (k, j, 0)) if b_blk
                  else pl.BlockSpec((tn, tk), lambda i, j, k: (perm(j), k)))
        dims = (((1,), (1,)), ((), ()))
    else:
        a_spec = pl.BlockSpec((tk, tm), lambda i, j, k: (k, i))
        b_spec = (pl.BlockSpec((1, tk, tn), lambda i, j, k: (j, k, 0)) if b_blk
                  else pl.BlockSpec((tk, tn), lambda i, j, k: (k, j)))
        dims = (((0,), (0,)), ((), ()))
    if out_blk:
        out_spec = pl.BlockSpec((1, tm, tn), lambda i, j, k: (j, i, 0))
        out_shape = jax.ShapeDtypeStruct((N // tn, M, tn), out_dtype)
    else:
        out_spec = pl.BlockSpec((tm, tn), lambda i, j, k: (i, j))
        out_shape = jax.ShapeDtypeStruct((M, N), out_dtype)
    extras, extra_specs, combine = [], [], []
    if add is not None:
        extras.append(add)
        extra_specs.append(pl.BlockSpec((tm, tn), lambda i, j, k: (i, j)))
        combine.append(lambda out, t: out + t)
    if bias is not None:
        extras.append(bias)
        extra_specs.append(pl.BlockSpec((1, tn), lambda i, j, k: (0, j)))
        combine.append(lambda out, t: out + t)
    if post is not None:
        extras.append(post[1])
        extra_specs.append(pl.BlockSpec((tm, tn), lambda i, j, k: (i, j)))
        combine.append(post[0])

    n_ex = len(extras)
    nc = carry.n if carry else 0

    def body(a_ref, b_ref, *rest):
        ex_refs, cx = rest[:n_ex], rest[n_ex:n_ex + nc]
        o_ref, cout = rest[n_ex + nc], rest[n_ex + nc + 1:n_ex + 2 * nc + 1]
        tail = rest[n_ex + 2 * nc + 1:]
        sems = tail[:3] if carry else ()
        first, last = _grid_edges(grid)
        if carry:
            @pl.when(first)
            def _():
                carry.start(cx, cout, sems)

        def product():
            av = a_ref[0] if a_blk else a_ref[...]
            bv = b_ref[0] if b_blk else b_ref[...]
            return lax.dot_general(av, bv, dims, preferred_element_type=F32)

        def finish(out):
            for r, fn in zip(ex_refs, combine):
                out = fn(out, r[...])
            if out_blk:
                o_ref[0] = out.astype(out_dtype)
            else:
                o_ref[...] = out.astype(out_dtype)

        if nk == 1:
            finish(product())
        else:
            acc_ref = tail[-1]
            k = pl.program_id(2)

            @pl.when(k == 0)
            def _():
                acc_ref[...] = jnp.zeros_like(acc_ref)

            acc_ref[...] += product()

            @pl.when(k == nk - 1)
            def _():
                finish(acc_ref[...])

        if carry:
            @pl.when(last)
            def _():
                carry.wait(cx, cout, sems)

    scratch = (carry.scratch if carry else []) + ([pltpu.VMEM((tm, tn), F32)] if nk > 1 else [])
    outs = pl.pallas_call(
        body, name=name, grid=grid,
        in_specs=[a_spec, b_spec] + extra_specs + (carry.specs if carry else []),
        out_specs=[out_spec] + (carry.specs if carry else []),
        out_shape=[out_shape] + (carry.out_shape if carry else []),
        scratch_shapes=scratch,
        compiler_params=_cparams(("arbitrary", "arbitrary", "arbitrary")),
    )(a, b, *extras, *(carry.xs if carry else []))
    return (outs[0], outs[1:]) if carry else outs[0]


def _rspec(tm, w, cb=0):
    return pl.BlockSpec((tm, w), lambda i: (i, cb))


def _fspec(shape):
    nd = len(shape)
    return pl.BlockSpec(shape, lambda i: (0,) * nd)


def _rowcall(body, name, L, tm, ins, row_outs, acc_outs=()):
    out_shape = [jax.ShapeDtypeStruct((L, w), dt) for w, dt in row_outs]
    out_shape += [jax.ShapeDtypeStruct(s, F32) for s in acc_outs]
    out_specs = [_rspec(tm, w) for w, _ in row_outs] + [_fspec(s) for s in acc_outs]
    return pl.pallas_call(
        body, name=name, grid=(L // tm,),
        in_specs=[s for _, s in ins], out_specs=out_specs, out_shape=out_shape,
        compiler_params=_cparams(("arbitrary",)),
    )(*[a for a, _ in ins])


def _mm_rows(a, b, mode, name, post, row_ins, full_ins, row_outs, acc_outs=(), vmem=VMEM_LIMIT):
    M, K = a.shape
    N = b.shape[1] if mode == "nn" else b.shape[0]
    tm = _pick(M, (256,))
    dims = (((1,), (0,)), ((), ())) if mode == "nn" else (((1,), (1,)), ((), ()))
    n_in, n_row = len(row_ins) + len(full_ins), len(row_outs)

    def body(a_ref, b_ref, *rest):
        ins, outs = rest[:n_in], rest[n_in:]
        prod = lax.dot_general(a_ref[...], b_ref[...], dims, preferred_element_type=F32)
        res = post(prod, *ins)
        for o_ref, val in zip(outs[:n_row], res[:n_row]):
            o_ref[...] = val.astype(o_ref.dtype)
        for acc_ref, val in zip(outs[n_row:], res[n_row:]):
            _acc(acc_ref, val)

    out_shape = [jax.ShapeDtypeStruct((M, w), dt) for w, dt in row_outs]
    out_shape += [jax.ShapeDtypeStruct(s, F32) for s in acc_outs]
    return pl.pallas_call(
        body, name=name, grid=(M // tm,),
        in_specs=[_rspec(tm, K), _fspec(b.shape)] + [_rspec(tm, r.shape[1]) for r in row_ins]
        + [_fspec(f.shape) for f in full_ins],
        out_specs=[_rspec(tm, w) for w, _ in row_outs] + [_fspec(s) for s in acc_outs],
        out_shape=out_shape, compiler_params=_cparams(("arbitrary",), vmem),
    )(a, b, *row_ins, *full_ins)


def _acc(ref, val):
    @pl.when(pl.program_id(0) == 0)
    def _():
        ref[...] = jnp.zeros_like(ref)
    ref[...] += val


def _colsum(v):
    return jnp.sum(v, axis=0, keepdims=True)


def _rms(xv):
    return lax.rsqrt(jnp.mean(xv * xv, axis=-1, keepdims=True) + EPS)


def _rms_bwd(dn, xhat, r, g):
    dng = dn * g
    return r * (dng - xhat * jnp.mean(dng * xhat, axis=-1, keepdims=True))


def _sigmoid(v):
    return jax.nn.sigmoid(v)


def _f32(ref):
    return ref[...].astype(F32)


def _partner(v):
    w = v.shape[-1]
    lane = lax.broadcasted_iota(jnp.int32, v.shape, v.ndim - 1)
    first_half = (lane % 64) < 32
    return jnp.where(first_half, pltpu.roll(v, w - 32, axis=v.ndim - 1), pltpu.roll(v, 32, axis=v.ndim - 1))


def _norm_in(x, g, name):
    L, D = x.shape
    tm = _pick(L, (512, 256))

    def body(x_ref, g_ref, o_ref):
        xv = x_ref[...]
        o_ref[...] = (xv * _rms(xv) * g_ref[...]).astype(BF16)

    return _rowcall(body, name, L, tm, [(x, _rspec(tm, D)), (g, _fspec(g.shape))], [(D, BF16)])[0]


def _rope_tables(L):
    t = np.arange(L)
    rows = (t // GRID_W).astype(np.float32)
    cols = (t % GRID_W).astype(np.float32)
    n_freq = HEAD_DIM // 4
    inv_freq = np.float32(ROPE_THETA) ** (-np.arange(n_freq, dtype=np.float32) / np.float32(n_freq))
    ar = (rows[:, None] * inv_freq[None, :]).astype(np.float32).astype(np.float64)
    ac = (cols[:, None] * inv_freq[None, :]).astype(np.float32).astype(np.float64)
    cos = np.concatenate([np.cos(ar), np.cos(ar), np.cos(ac), np.cos(ac)], axis=-1).astype(np.float32)
    sin = np.concatenate([-np.sin(ar), np.sin(ar), -np.sin(ac), np.sin(ac)], axis=-1).astype(np.float32)
    return jnp.asarray(cos), jnp.asarray(sin)


def _qkv_prep(z, cos, sin, qn, kn):
    L = z.shape[0]
    tm = _pick(L, (512, 256))
    scale = HEAD_DIM ** -0.5
    kblk = 4 * D_ATTN // D_KV

    def body(q_ref, k_ref, v_ref, cos_ref, sin_ref, qn_ref, kn_ref, qo_ref, ko_ref, vo_ref, kt_ref):
        c, s = cos_ref[...], sin_ref[...]

        def head(xh, w):
            n = xh * _rms(xh) * w
            return n * c + _partner(n) * s

        for h in range(N_HEADS):
            sl = slice(h * HEAD_DIM, (h + 1) * HEAD_DIM)
            qo_ref[:, sl] = (head(q_ref[:, sl].astype(F32), qn_ref[...]) * scale).astype(BF16)
        for h in range(N_KV):
            sl = slice(h * HEAD_DIM, (h + 1) * HEAD_DIM)
            kr = head(k_ref[:, sl].astype(F32), kn_ref[...])
            ko_ref[:, sl] = kr.astype(BF16)
            kt_ref[sl, :] = kr.T.astype(BF16)
        vo_ref[...] = v_ref[...].astype(BF16)

    return pl.pallas_call(
        body, name="qkv_prep", grid=(L // tm,),
        in_specs=[_rspec(tm, D_ATTN, 0), _rspec(tm, D_KV, kblk), _rspec(tm, D_KV, kblk + 1),
                  _rspec(tm, HEAD_DIM), _rspec(tm, HEAD_DIM), _fspec(qn.shape), _fspec(kn.shape)],
        out_specs=[_rspec(tm, D_ATTN), _rspec(tm, D_KV), _rspec(tm, D_KV),
                   pl.BlockSpec((D_KV, tm), lambda i: (0, i))],
        out_shape=[jax.ShapeDtypeStruct((L, D_ATTN), BF16), jax.ShapeDtypeStruct((L, D_KV), BF16),
                   jax.ShapeDtypeStruct((L, D_KV), BF16), jax.ShapeDtypeStruct((D_KV, L), BF16)],
        compiler_params=_cparams(("arbitrary",)),
    )(z, z, z, cos, sin, qn, kn)


def _col_to_row(col):
    n = col.shape[0]
    eye = lax.broadcasted_iota(jnp.int32, (n, n), 0) == lax.broadcasted_iota(jnp.int32, (n, n), 1)
    return jnp.sum(jnp.where(eye, col, 0.0), axis=0, keepdims=True)


def _attn_fwd(q, k, v, carry=None):
    L = q.shape[0]
    tq = _pick(L, (256, 128))
    grid = (N_HEADS, L // tq)
    nc = carry.n if carry else 0

    def body(q_ref, k_ref, v_ref, *rest):
        cx, (o_ref, lse_ref) = rest[:nc], rest[nc:nc + 2]
        cout, sems = rest[nc + 2:2 * nc + 2], rest[2 * nc + 2:]
        first, last = _grid_edges(grid)
        if carry:
            @pl.when(first)
            def _():
                carry.start(cx, cout, sems)

        s = lax.dot_general(q_ref[...], k_ref[...], (((1,), (1,)), ((), ())), preferred_element_type=F32)
        m = jnp.max(s, axis=-1, keepdims=True)
        e = jnp.exp(s - m)
        l = jnp.sum(e, axis=-1, keepdims=True)
        o_ref[...] = jnp.dot(e.astype(BF16), v_ref[...], preferred_element_type=F32) / l
        lse_ref[0] = _col_to_row(m + jnp.log(l))

        if carry:
            @pl.when(last)
            def _():
                carry.wait(cx, cout, sems)

    outs = pl.pallas_call(
        body, name="attn_fwd", grid=grid,
        in_specs=[pl.BlockSpec((tq, HEAD_DIM), lambda h, i: (i, h)),
                  pl.BlockSpec((L, HEAD_DIM), lambda h, i: (0, h // REP)),
                  pl.BlockSpec((L, HEAD_DIM), lambda h, i: (0, h // REP))] + (carry.specs if carry else []),
        out_specs=[pl.BlockSpec((tq, HEAD_DIM), lambda h, i: (i, h)),
                   pl.BlockSpec((1, 1, tq), lambda h, i: (h, 0, i))] + (carry.specs if carry else []),
        out_shape=[jax.ShapeDtypeStruct((L, D_ATTN), F32), jax.ShapeDtypeStruct((N_HEADS, 1, L), F32)]
        + (carry.out_shape if carry else []),
        scratch_shapes=carry.scratch if carry else [],
        compiler_params=_cparams(("arbitrary", "arbitrary")),
    )(q, k, v, *(carry.xs if carry else []))
    return outs[0], outs[1], outs[2:]


def _attn_bwd(q, k, v, kt, do, o, lse, carry=None):
    L = q.shape[0]
    tq = _pick(L, (256, 128))
    kc = _pick(L, (512, 256, 128))
    nt = (((1,), (1,)), ((), ()))
    grid = (N_KV, REP, L // tq)
    nc = carry.n if carry else 0

    def body(q_ref, do_ref, o_ref, lse_ref, k_ref, v_ref, kt_ref, *rest):
        cx, (dq_ref, dk_ref, dv_ref) = rest[:nc], rest[nc:nc + 3]
        cout, sems = rest[nc + 3:2 * nc + 3], rest[2 * nc + 3:]
        first, last = _grid_edges(grid)
        if carry:
            @pl.when(first)
            def _():
                carry.start(cx, cout, sems)

        @pl.when((pl.program_id(1) == 0) & (pl.program_id(2) == 0))
        def _():
            dk_ref[...] = jnp.zeros_like(dk_ref)
            dv_ref[...] = jnp.zeros_like(dv_ref)

        qv, dov = q_ref[...], do_ref[...]
        lse_row = lse_ref[0]
        delta = _col_to_row(jnp.sum(dov.astype(F32) * o_ref[...], axis=-1, keepdims=True))
        dqt = jnp.zeros((HEAD_DIM, tq), F32)
        for c in range(L // kc):
            sl = slice(c * kc, (c + 1) * kc)
            st = lax.dot_general(k_ref[sl, :], qv, nt, preferred_element_type=F32)
            pt = jnp.exp(st - lse_row)
            dpt = lax.dot_general(v_ref[sl, :], dov, nt, preferred_element_type=F32)
            dst = (pt * (dpt - delta)).astype(BF16)
            dv_ref[sl, :] += jnp.dot(pt.astype(BF16), dov, preferred_element_type=F32)
            dk_ref[sl, :] += jnp.dot(dst, qv, preferred_element_type=F32)
            dqt = dqt + jnp.dot(kt_ref[:, sl], dst, preferred_element_type=F32)
        dq_ref[...] = dqt.T

        if carry:
            @pl.when(last)
            def _():
                carry.wait(cx, cout, sems)

    head = lambda g, r, i: (i, g * REP + r)
    outs = pl.pallas_call(
        body, name="attn_bwd", grid=grid,
        in_specs=[pl.BlockSpec((tq, HEAD_DIM), head), pl.BlockSpec((tq, HEAD_DIM), head),
                  pl.BlockSpec((tq, HEAD_DIM), head),
                  pl.BlockSpec((1, 1, tq), lambda g, r, i: (g * REP + r, 0, i)),
                  pl.BlockSpec((L, HEAD_DIM), lambda g, r, i: (0, g)),
                  pl.BlockSpec((L, HEAD_DIM), lambda g, r, i: (0, g)),
                  pl.BlockSpec((HEAD_DIM, L), lambda g, r, i: (g, 0))] + (carry.specs if carry else []),
        out_specs=[pl.BlockSpec((tq, HEAD_DIM), head),
                   pl.BlockSpec((L, HEAD_DIM), lambda g, r, i: (0, g)),
                   pl.BlockSpec((L, HEAD_DIM), lambda g, r, i: (0, g))] + (carry.specs if carry else []),
        out_shape=[jax.ShapeDtypeStruct((L, D_ATTN), F32), jax.ShapeDtypeStruct((L, D_KV), F32),
                   jax.ShapeDtypeStruct((L, D_KV), F32)] + (carry.out_shape if carry else []),
        scratch_shapes=carry.scratch if carry else [],
        compiler_params=_cparams(("arbitrary", "arbitrary", "arbitrary")),
    )(q, do, o, lse, k, v, kt, *(carry.xs if carry else []))
    return outs[0], outs[1], outs[2], outs[3:]


def _seg_perm(a):
    L, C = a.shape
    return a.reshape(SEG, L // SEG, C).transpose(1, 0, 2).reshape(L, C)


def _seg_unperm(a):
    L, C = a.shape
    return a.reshape(L // SEG, SEG, C).transpose(1, 0, 2).reshape(L, C)


def _cmul(ar, ai, br, bi):
    return ar * br - ai * bi, ar * bi + ai * br


def _rows8(rr):
    if isinstance(rr, int):
        return pl.ds(rr * SEG, SEG)
    return pl.ds(pl.multiple_of(rr * SEG, SEG), SEG)


def _seg_scan(xr_ref, xi_ref, ar, ai, reverse, n_rows, visit=None, visit_init=(), entering=None):
    shape = ar.shape
    zero = jnp.zeros(shape, F32)
    rc = n_rows // CHAINS

    def index(q):
        return (n_rows - 1 - q) if reverse else q

    if entering is None:
        def ends(q, carry):
            out = []
            for j in range(CHAINS):
                sl = _rows8(index(j * rc + q))
                pr, pi = _cmul(ar, ai, carry[2 * j], carry[2 * j + 1])
                out += [pr + xr_ref[sl, :], pi + xi_ref[sl, :]]
            return tuple(out)

        def ends_block(qb, carry):
            for t in range(SCAN_UNROLL):
                carry = ends(qb * SCAN_UNROLL + t, carry)
            return carry

        e = lax.fori_loop(0, rc // SCAN_UNROLL, ends_block, (zero,) * (2 * CHAINS))

        pr, pi = ar, ai
        for _ in range(int(math.log2(rc))):
            pr, pi = _cmul(pr, pi, pr, pi)
        sub = lax.broadcasted_iota(jnp.int32, shape, 0)
        shift = (SEG - 1) if reverse else 1
        edge = (SEG - 1) if reverse else 0
        entering = [(zero, zero)] * CHAINS
        for _ in range(SEG):
            tr, ti = _cmul(pr, pi, *entering[CHAINS - 1])
            cur = (jnp.where(sub == edge, 0.0, pltpu.roll(tr + e[2 * CHAINS - 2], shift, axis=0)),
                   jnp.where(sub == edge, 0.0, pltpu.roll(ti + e[2 * CHAINS - 1], shift, axis=0)))
            entering = [cur]
            for j in range(1, CHAINS):
                tr, ti = _cmul(pr, pi, *cur)
                cur = (tr + e[2 * j - 2], ti + e[2 * j - 1])
                entering.append(cur)

    def step(q, carry, last):
        out, acc = [], carry[2 * CHAINS:]
        for j in range(CHAINS):
            rr = index(j * rc + q)
            sl = _rows8(rr)
            pr, pi = _cmul(ar, ai, carry[2 * j], carry[2 * j + 1])
            nr, ni = pr + xr_ref[sl, :], pi + xi_ref[sl, :]
            xr_ref[sl, :] = nr
            xi_ref[sl, :] = ni
            if visit:
                acc = visit(rr, nr, ni, acc, last and j == CHAINS - 1)
            out += [nr, ni]
        return (*out, *acc)

    def step_block(qb, carry):
        for t in range(SCAN_UNROLL):
            carry = step(qb * SCAN_UNROLL + t, carry, False)
        return carry

    start = tuple(v for pair in entering for v in pair)
    n_blocks = (rc - 1) // SCAN_UNROLL
    carry = lax.fori_loop(0, n_blocks, step_block, (*start, *visit_init))
    for q in range(n_blocks * SCAN_UNROLL, rc - 1):
        carry = step(q, carry, False)
    carry = step(rc - 1, carry, True)
    return entering, carry[2 * CHAINS:]


def _discretise(a_re, a_im, ldt):
    lr = jnp.minimum(a_re, -1e-4)
    li = a_im
    dt = jnp.exp(ldt)
    mag = jnp.exp(lr * dt)
    lbr = mag * jnp.cos(li * dt)
    lbi = mag * jnp.sin(li * dt)
    den = lr * lr + li * li
    nr = lbr - 1.0
    fr = (nr * lr + lbi * li) / den
    fi = (lbi * lr - nr * li) / den
    return lr, li, dt, lbr, lbi, fr, fi


def _lane_row(v):
    return jnp.concatenate([v[g:g + 1, :] for g in range(v.shape[0])], axis=1)


def _ssm_fill_maps(d, prm, tmp_ref, maps):
    a_re_ref, a_im_ref, ldt_ref, bt_re_ref, bt_im_ref, c_re_ref, c_im_ref = prm
    _, _, _, lbr, lbi, fr, fi = _discretise(a_re_ref[d], a_im_ref[d], ldt_ref[d])

    def fill(dst, piece):
        tmp_ref[...] = jnp.zeros_like(tmp_ref)
        for g in range(SLAB_G):
            tmp_ref[g * SSM_H:(g + 1) * SSM_H, g * SSM_P:(g + 1) * SSM_P] = piece(g)
        dst[...] = tmp_ref[...].astype(BF16)

    wbr, wbi, wcr, wci = maps
    fill(wbr, lambda g: fr[g:g + 1] * bt_re_ref[d, g] - fi[g:g + 1] * bt_im_ref[d, g])
    fill(wbi, lambda g: fr[g:g + 1] * bt_im_ref[d, g] + fi[g:g + 1] * bt_re_ref[d, g])
    fill(wcr, lambda g: c_re_ref[d, g])
    fill(wci, lambda g: c_im_ref[d, g])
    return _lane_row(lbr), _lane_row(lbi)


def _ssm_param_specs():
    pole = pl.BlockSpec((2, SLAB_G, SSM_P), lambda j: (0, j, 0))
    step = pl.BlockSpec((2, SLAB_G, 1), lambda j: (0, j, 0))
    mat = pl.BlockSpec((2, SLAB_G, SSM_H, SSM_P), lambda j: (0, j, 0, 0))
    return [pole, pole, step, mat, mat, mat, mat]


_MAP_SCRATCH = [pltpu.VMEM((SLAB, SLAB_S), F32)] + [pltpu.VMEM((SLAB, SLAB_S), BF16)] * 4
_ENT_SPEC = pl.BlockSpec((1, 2, 2 * CHAINS, SEG, SLAB_S), lambda j: (j, 0, 0, 0, 0))
_NT = (((1,), (1,)), ((), ()))


def _ssm_fwd(u, prm, dskip):
    L, C = u.shape
    n_rows = L // SEG
    tc = _pick(L, (2048, 1024, 512, 256))
    u_spec = pl.BlockSpec((L, SLAB), lambda j: (0, j))
    d_spec = pl.BlockSpec((1, SLAB), lambda j: (0, j))

    def body(u_ref, *rest):
        prm_refs, d_ref, y_ref, ent_ref = rest[:7], rest[7], rest[8], rest[9]
        tmp_ref, maps, xr_ref, xi_ref = rest[10], rest[11:15], rest[15], rest[16]
        wbr, wbi, wcr, wci = maps
        y_ref[...] = u_ref[...] * d_ref[...]
        for d in range(2):
            lam_r, lam_i = _ssm_fill_maps(d, prm_refs, tmp_ref, maps)

            def inp(c, _):
                sl = pl.ds(pl.multiple_of(c * tc, tc), tc)
                ub = u_ref[sl, :].astype(BF16)
                xr_ref[sl, :] = jnp.dot(ub, wbr[...], preferred_element_type=F32)
                xi_ref[sl, :] = jnp.dot(ub, wbi[...], preferred_element_type=F32)
                return 0

            lax.fori_loop(0, L // tc, inp, 0)
            ar = jnp.broadcast_to(lam_r, (SEG, SLAB_S))
            ai = jnp.broadcast_to(lam_i, (SEG, SLAB_S))
            entering, _ = _seg_scan(xr_ref, xi_ref, ar, ai, d == 1, n_rows)
            for j, (er, ei) in enumerate(entering):
                ent_ref[0, d, 2 * j] = er
                ent_ref[0, d, 2 * j + 1] = ei

            def outp(c, _):
                sl = pl.ds(pl.multiple_of(c * tc, tc), tc)
                y_ref[sl, :] += (
                    lax.dot_general(xr_ref[sl, :].astype(BF16), wcr[...], _NT, preferred_element_type=F32)
                    - lax.dot_general(xi_ref[sl, :].astype(BF16), wci[...], _NT, preferred_element_type=F32))
                return 0

            lax.fori_loop(0, L // tc, outp, 0)

    return pl.pallas_call(
        body, name="ssm_fwd", grid=(C // SLAB,),
        in_specs=[u_spec] + _ssm_param_specs() + [d_spec],
        out_specs=[u_spec, _ENT_SPEC],
        out_shape=[jax.ShapeDtypeStruct((L, C), F32),
                   jax.ShapeDtypeStruct((C // SLAB, 2, 2 * CHAINS, SEG, SLAB_S), F32)],
        scratch_shapes=_MAP_SCRATCH + [pltpu.VMEM((L, SLAB_S), F32)] * 2,
        compiler_params=_cparams(("arbitrary",)),
    )(u, *prm, dskip)


def _ssm_bwd(u, dy, ent, prm, dskip):
    L, C = u.shape
    n_rows = L // SEG
    n_slab = C // SLAB
    tc = _pick(L, (2048, 1024, 512, 256))
    u_spec = pl.BlockSpec((L, SLAB), lambda j: (0, j))
    d_spec = pl.BlockSpec((1, SLAB), lambda j: (0, j))
    pg_spec = pl.BlockSpec((1, 2, PG_ROWS, SLAB_S), lambda j: (j, 0, 0, 0))

    def body(u_ref, dy_ref, ent_ref, *rest):
        prm_refs, d_ref, du_ref, pg_ref = rest[:7], rest[7], rest[8], rest[9]
        tmp_ref, maps, acc_ref = rest[10], rest[11:15], rest[15]
        xr_ref, xi_ref, gr_ref, gi_ref = rest[16:20]
        wbr, wbi, wcr, wci = maps
        du_ref[...] = dy_ref[...] * d_ref[...]
        pg_ref[...] = jnp.zeros_like(pg_ref)
        pg_ref[0, 0, 66:67, 0:SLAB] = _colsum(dy_ref[...] * u_ref[...])
        for d in range(2):
            lam_r, lam_i = _ssm_fill_maps(d, prm_refs, tmp_ref, maps)

            def inp(c, _):
                sl = pl.ds(pl.multiple_of(c * tc, tc), tc)
                ub = u_ref[sl, :].astype(BF16)
                dyb = dy_ref[sl, :].astype(BF16)
                xr_ref[sl, :] = jnp.dot(ub, wbr[...], preferred_element_type=F32)
                xi_ref[sl, :] = jnp.dot(ub, wbi[...], preferred_element_type=F32)
                gr_ref[sl, :] = jnp.dot(dyb, wcr[...], preferred_element_type=F32)
                gi_ref[sl, :] = -jnp.dot(dyb, wci[...], preferred_element_type=F32)
                return 0

            lax.fori_loop(0, L // tc, inp, 0)
            ar = jnp.broadcast_to(lam_r, (SEG, SLAB_S))
            ai = jnp.broadcast_to(lam_i, (SEG, SLAB_S))
            entering = [(ent_ref[0, d, 2 * j], ent_ref[0, d, 2 * j + 1]) for j in range(CHAINS)]
            _seg_scan(xr_ref, xi_ref, ar, ai, d == 1, n_rows, entering=entering)

            def pole(rr, lr, li, acc, last):
                if last:
                    pr, pi = entering[0]
                else:
                    nb = _rows8(rr + 1 if d == 1 else rr - 1)
                    pr, pi = xr_ref[nb, :], xi_ref[nb, :]
                return acc[0] + lr * pr + li * pi, acc[1] + li * pr - lr * pi

            zero = jnp.zeros((SEG, SLAB_S), F32)
            _, (accr, acci) = _seg_scan(gr_ref, gi_ref, ar, -ai, d == 0, n_rows, pole, (zero, zero))
            pg_ref[0, d, 64:65, :] = _colsum(accr)
            pg_ref[0, d, 65:66, :] = _colsum(acci)

            acc_ref[...] = jnp.zeros_like(acc_ref)

            def outp(c, _):
                sl = pl.ds(pl.multiple_of(c * tc, tc), tc)
                lrb, lib = gr_ref[sl, :].astype(BF16), gi_ref[sl, :].astype(BF16)
                du_ref[sl, :] += (lax.dot_general(lrb, wbr[...], _NT, preferred_element_type=F32)
                                  + lax.dot_general(lib, wbi[...], _NT, preferred_element_type=F32))
                ut = u_ref[sl, :].astype(F32).T.astype(BF16)
                dyt = dy_ref[sl, :].T.astype(BF16)
                acc_ref[0] += jnp.dot(ut, lrb, preferred_element_type=F32)
                acc_ref[1] += jnp.dot(ut, lib, preferred_element_type=F32)
                acc_ref[2] += jnp.dot(dyt, xr_ref[sl, :].astype(BF16), preferred_element_type=F32)
                acc_ref[3] -= jnp.dot(dyt, xi_ref[sl, :].astype(BF16), preferred_element_type=F32)
                return 0

            lax.fori_loop(0, L // tc, outp, 0)
            for m in range(4):
                for g in range(SLAB_G):
                    lanes = slice(g * SSM_P, (g + 1) * SSM_P)
                    pg_ref[0, d, m * SSM_H:(m + 1) * SSM_H, lanes] = acc_ref[m, g * SSM_H:(g + 1) * SSM_H, lanes]

    return pl.pallas_call(
        body, name="ssm_bwd", grid=(n_slab,),
        in_specs=[u_spec, u_spec, _ENT_SPEC] + _ssm_param_specs() + [d_spec],
        out_specs=[u_spec, pg_spec],
        out_shape=[jax.ShapeDtypeStruct((L, C), F32), jax.ShapeDtypeStruct((n_slab, 2, PG_ROWS, SLAB_S), F32)],
        scratch_shapes=_MAP_SCRATCH + [pltpu.VMEM((4, SLAB, SLAB_S), F32)] + [pltpu.VMEM((L, SLAB_S), F32)] * 4,
        compiler_params=_cparams(("arbitrary",), 60 << 20),
    )(u, dy, ent, *prm, dskip)


def _ssm_param_grads(pg, prm):
    n_slab = pg.shape[0]
    G = n_slab * SLAB_G
    pg_spec = pl.BlockSpec((1, 2, PG_ROWS, SLAB_S), lambda j: (j, 0, 0, 0))
    pole, _, step, mat = _ssm_param_specs()[:4]

    def body(pg_ref, a_re_ref, a_im_ref, ldt_ref, bt_re_ref, bt_im_ref,
             dbr_ref, dbi_ref, dcr_ref, dci_ref, dar_ref, dai_ref, dldt_ref, dd_ref):
        dd_ref[...] = pg_ref[0, 0, 66:67, 0:SLAB]
        for d in range(2):
            a_r = a_re_ref[d]
            lr, li, dt, lbr, lbi, f_r, f_i = _discretise(a_r, a_im_ref[d], ldt_ref[d])
            gfr_rows, gfi_rows, glr_rows, gli_rows = [], [], [], []
            for g in range(SLAB_G):
                lanes = slice(g * SSM_P, (g + 1) * SSM_P)
                gbr, gbi = pg_ref[0, d, 0:SSM_H, lanes], pg_ref[0, d, SSM_H:2 * SSM_H, lanes]
                b_r, b_i = bt_re_ref[d, g], bt_im_ref[d, g]
                fr, fi = f_r[g:g + 1], f_i[g:g + 1]
                dbr_ref[d, g] = fr * gbr + fi * gbi
                dbi_ref[d, g] = fr * gbi - fi * gbr
                gfr_rows.append(_colsum(gbr * b_r + gbi * b_i))
                gfi_rows.append(_colsum(gbi * b_r - gbr * b_i))
                dcr_ref[d, g] = pg_ref[0, d, 2 * SSM_H:3 * SSM_H, lanes]
                dci_ref[d, g] = pg_ref[0, d, 3 * SSM_H:4 * SSM_H, lanes]
                glr_rows.append(pg_ref[0, d, 64:65, lanes])
                gli_rows.append(pg_ref[0, d, 65:66, lanes])
            gfr, gfi = jnp.concatenate(gfr_rows, axis=0), jnp.concatenate(gfi_rows, axis=0)
            glr, gli = jnp.concatenate(glr_rows, axis=0), jnp.concatenate(gli_rows, axis=0)
            den = lr * lr + li * li
            ir, ii = lr / den, -li / den
            tr, ti = _cmul(ir, -ii, gfr, gfi)
            glbr, glbi = glr + tr, gli + ti
            qr, qi = _cmul(f_r, f_i, ir, ii)
            dlr, dli = _cmul(-qr, qi, gfr, gfi)
            zr, zi = _cmul(lbr, -lbi, glbr, glbi)
            dlr = dlr + dt * zr
            dli = dli + dt * zi
            dar_ref[d] = jnp.where(a_r < -1e-4, dlr, jnp.where(a_r == -1e-4, 0.5 * dlr, 0.0))
            dai_ref[d] = dli
            dldt_ref[d] = jnp.sum(lr * zr + li * zi, axis=-1, keepdims=True) * dt

    a_re, a_im, ldt, bt_re, bt_im = prm[:5]
    mshape = jax.ShapeDtypeStruct(bt_re.shape, F32)
    pshape = jax.ShapeDtypeStruct(a_re.shape, F32)
    return pl.pallas_call(
        body, name="ssm_param_grads", grid=(n_slab,),
        in_specs=[pg_spec, pole, pole, step, mat, mat],
        out_specs=[mat, mat, mat, mat, pole, pole, step, pl.BlockSpec((1, SLAB), lambda j: (0, j))],
        out_shape=[mshape, mshape, mshape, mshape, pshape, pshape, jax.ShapeDtypeStruct(ldt.shape, F32),
                   jax.ShapeDtypeStruct((1, n_slab * SLAB), F32)],
        compiler_params=_cparams(("arbitrary",)),
    )(pg, a_re, a_im, ldt, bt_re, bt_im)


def _peer(k, x, y, c):
    return (1 - x if k & 4 else x, 1 - y if k & 2 else y, 1 - c if k & 1 else c)


def _dev_index(pos):
    return 4 * pos[0] + 2 * pos[1] + pos[2]


def _chip_index(pos):
    return 2 * pos[0] + pos[1]


def _sibling_swap(x, name):
    any_spec = pl.BlockSpec(memory_space=pl.ANY)

    def body(x_ref, out_ref, send_sems, recv_sems):
        x_, y_, c_ = lax.axis_index("x"), lax.axis_index("y"), lax.axis_index("c")
        copies = [pltpu.make_async_remote_copy(
            src_ref=x_ref.at[2 * chip + (1 - c_)], dst_ref=out_ref.at[chip], send_sem=send_sems.at[chip],
            recv_sem=recv_sems.at[chip], device_id=(x_, y_, 1 - c_), device_id_type=pl.DeviceIdType.MESH)
            for chip in range(N_CHIPS)]
        for cp in copies:
            cp.start()
        for cp in copies:
            cp.wait()

    return pl.pallas_call(
        body, name=name, out_shape=jax.ShapeDtypeStruct((N_CHIPS,) + x.shape[1:], x.dtype),
        in_specs=[any_spec], out_specs=any_spec,
        scratch_shapes=[pltpu.SemaphoreType.DMA((N_CHIPS,)), pltpu.SemaphoreType.DMA((N_CHIPS,))],
    )(x)


def _pair_sum(x, got, name):
    n, R, W = got.shape
    tr = _row_tile(R, W, 5 << 20)
    core = lax.axis_index("c").astype(jnp.int32).reshape(1)

    def body(core_ref, a_ref, b_ref, o_ref):
        o_ref[...] = (a_ref[...].astype(F32) + b_ref[...].astype(F32)).astype(BF16)

    spec = pl.BlockSpec((1, tr, W), lambda i, j, c: (i, j, 0))
    grid_spec = pltpu.PrefetchScalarGridSpec(
        num_scalar_prefetch=1, grid=(n, R // tr),
        in_specs=[pl.BlockSpec((1, tr, W), lambda i, j, c: (2 * i + c[0], j, 0)), spec], out_specs=spec)
    return pl.pallas_call(
        body, name=name, grid_spec=grid_spec, out_shape=jax.ShapeDtypeStruct((n, R, W), BF16),
        compiler_params=_cparams(("arbitrary", "arbitrary")),
    )(core, x, got)


def _all_gather(xs, name):
    n = len(xs)
    any_spec = pl.BlockSpec(memory_space=pl.ANY)

    def body(*refs):
        x_refs, out_refs = refs[:n], refs[n:2 * n]
        send_sems, recv_sems, local_sems = refs[2 * n:]
        x, y, c = lax.axis_index("x"), lax.axis_index("y"), lax.axis_index("c")
        me, sibling = (x, y, c), (x, y, 1 - c)
        chips = [(1 - x, y), (x, 1 - y), (1 - x, 1 - y)]

        def copy(a, k, block, to, src=None):
            dst = out_refs[a].at[_dev_index(block)]
            return pltpu.make_async_remote_copy(
                src_ref=dst if src is None else src, dst_ref=dst,
                send_sem=send_sems.at[a, k], recv_sem=recv_sems.at[a, k],
                device_id=to, device_id_type=pl.DeviceIdType.MESH)

        mine = [pltpu.make_async_copy(x_refs[a], out_refs[a].at[_dev_index(me)], local_sems.at[a]) for a in range(n)]
        for cp in mine:
            cp.start()
        first = []
        for a in range(n):
            first.append(copy(a, 0, me, sibling, src=x_refs[a]))
            first += [copy(a, 1 + j, me, (*chip, c), src=x_refs[a]) for j, chip in enumerate(chips)]
        for cp in first:
            cp.start()
        passed = []
        for j, chip in enumerate(chips):
            for a in range(n):
                copy(a, 1 + j, (*chip, c), me).wait_recv()
                fwd = copy(a, 4 + j, (*chip, c), sibling)
                fwd.start()
                passed.append(fwd)
        for a in range(n):
            copy(a, 0, sibling, me).wait_recv()
            for j, chip in enumerate(chips):
                copy(a, 4 + j, (*chip, 1 - c), me).wait_recv()
        for cp in first + passed:
            cp.wait_send()
        for cp in mine:
            cp.wait()

    return pl.pallas_call(
        body, name=name,
        out_shape=[jax.ShapeDtypeStruct((N_DEV,) + v.shape, v.dtype) for v in xs],
        in_specs=[any_spec] * n, out_specs=[any_spec] * n,
        scratch_shapes=[pltpu.SemaphoreType.DMA((n, 7)), pltpu.SemaphoreType.DMA((n, 7)),
                        pltpu.SemaphoreType.DMA((n,))],
    )(*xs)


def _sum_blocks(x, name):
    _, R, W = x.shape

    def body(x_ref, o_ref):
        acc = x_ref[0].astype(F32)
        for d in range(1, N_DEV):
            acc = acc + x_ref[d].astype(F32)
        o_ref[...] = acc

    return pl.pallas_call(body, name=name, out_shape=jax.ShapeDtypeStruct((R, W), F32),
                          compiler_params=pltpu.CompilerParams(vmem_limit_bytes=VMEM_LIMIT))(x)


def _adam_update(w, g, m, v):
    mn = ADAM_B1 * m + (1.0 - ADAM_B1) * g
    vn = ADAM_B2 * v + (1.0 - ADAM_B2) * (g * g)
    m_hat = mn / (1.0 - ADAM_B1 ** ADAM_STEP)
    v_hat = vn / (1.0 - ADAM_B2 ** ADAM_STEP)
    return -ADAM_LR * (m_hat / (jnp.sqrt(v_hat) + ADAM_EPS) + ADAM_WD * w), mn, vn


def _row_tile(R, W, budget):
    padded = -(-W // LANES) * LANES * 4
    if R * padded <= budget:
        return R
    return _pick(R, [t for t in (2048, 1024, 512, 256, 128, 64, 32, 16, 8) if t * padded <= budget])


def _adamw(w, g, m, v, name):
    R, W = w.shape
    tr = _row_tile(R, W, 1 << 20)

    def body(w_ref, g_ref, m_ref, v_ref, d_ref, mo_ref, vo_ref):
        d_ref[...], mo_ref[...], vo_ref[...] = _adam_update(w_ref[...], g_ref[...], m_ref[...], v_ref[...])

    spec = pl.BlockSpec((tr, W), lambda i: (i, 0))
    shp = jax.ShapeDtypeStruct((R, W), F32)
    return pl.pallas_call(
        body, name=name, grid=(R // tr,), in_specs=[spec] * 4, out_specs=[spec] * 3, out_shape=[shp] * 3,
        compiler_params=_cparams(("arbitrary",)),
    )(w, g, m, v)


def _adamw_group(ws, gs, ms, vs, name):
    n = len(ws)

    def body(*refs):
        ins, outs = refs[:4 * n], refs[4 * n:]
        for i in range(n):
            w_ref, g_ref, m_ref, v_ref = ins[i], ins[n + i], ins[2 * n + i], ins[3 * n + i]
            outs[i][...], outs[n + i][...], outs[2 * n + i][...] = _adam_update(
                w_ref[...], g_ref[...], m_ref[...], v_ref[...])

    shapes = [jax.ShapeDtypeStruct(w.shape, F32) for w in ws]
    outs = pl.pallas_call(body, name=name, out_shape=shapes * 3)(*ws, *gs, *ms, *vs)
    return outs[:n], outs[n:2 * n], outs[2 * n:]


def _adamw_reduce(w, land, m, v, name):
    R, W = w.shape
    n = land.shape[0]
    tr = _row_tile(R, W, 1 << 20)

    def body(w_ref, l_ref, m_ref, v_ref, g_ref, d_ref, mo_ref, vo_ref):
        g = l_ref[0].astype(F32)
        for d in range(1, n):
            g = g + l_ref[d].astype(F32)
        g_ref[...] = g
        d_ref[...], mo_ref[...], vo_ref[...] = _adam_update(w_ref[...], g, m_ref[...], v_ref[...])

    spec = pl.BlockSpec((tr, W), lambda i: (i, 0))
    lspec = pl.BlockSpec((n, tr, W), lambda i: (0, i, 0))
    shp = jax.ShapeDtypeStruct((R, W), F32)
    return pl.pallas_call(
        body, name=name, grid=(R // tr,), in_specs=[spec, lspec, spec, spec], out_specs=[spec] * 4,
        out_shape=[shp] * 4, compiler_params=_cparams(("arbitrary",)),
    )(w, land, m, v)


def _gelu(v):
    c = math.sqrt(2.0 / math.pi)
    return 0.5 * v * (1.0 + jnp.tanh(c * (v + 0.044715 * v * v * v)))


def _gelu_grad(v):
    c = math.sqrt(2.0 / math.pi)
    t = jnp.tanh(c * (v + 0.044715 * v * v * v))
    return 0.5 * (1.0 + t) + 0.5 * v * (1.0 - t * t) * c * (1.0 + 3.0 * 0.044715 * v * v)


def kernel(x, p, norm_mix, w_in, q_norm, k_norm, ssm_a_re, ssm_a_im, ssm_log_dt, ssm_b_re, ssm_b_im, ssm_c_re, ssm_c_im, ssm_d, w_glu, b_glu, w_out, norm_ple, w_ple_gate, w_ple_proj, norm_final, loss_target, m_norm_mix, m_w_in, m_q_norm, m_k_norm, m_ssm_a_re, m_ssm_a_im, m_ssm_log_dt, m_ssm_b_re, m_ssm_b_im, m_ssm_c_re, m_ssm_c_im, m_ssm_d, m_w_glu, m_b_glu, m_w_out, m_norm_ple, m_w_ple_gate, m_w_ple_proj, m_norm_final, v_norm_mix, v_w_in, v_q_norm, v_k_norm, v_ssm_a_re, v_ssm_a_im, v_ssm_log_dt, v_ssm_b_re, v_ssm_b_im, v_ssm_c_re, v_ssm_c_im, v_ssm_d, v_w_glu, v_b_glu, v_w_out, v_norm_ple, v_w_ple_gate, v_w_ple_proj, v_norm_final):
    L, D = x.shape[1], x.shape[2]
    D_SSM = ssm_d.shape[1]
    G = D_SSM // SSM_H
    n_slab = D_SSM // SLAB
    n_in = w_in.shape[2]
    D_IN = n_in * N_DEV
    n_pp = w_ple_proj.shape[2]
    n_glu = w_glu.shape[2]
    xs = x[0]
    ps = p[0, 0]
    tgt = loss_target[0]

    (win_t3,) = _all_gather([w_in[0].T.astype(BF16)], "gather_w_in")
    win_t = win_t3.reshape(D_IN, D)
    later_weights = _Carry("gather", [w_glu[0].astype(BF16), w_out[0].astype(BF16), w_ple_gate[0].astype(BF16),
                                      w_ple_proj[0].astype(BF16)])

    ssm_prm = (ssm_a_re[0], ssm_a_im[0], ssm_log_dt[0].reshape(2, G, 1),
               ssm_b_re[0].transpose(0, 1, 3, 2), ssm_b_im[0].transpose(0, 1, 3, 2), ssm_c_re[0], ssm_c_im[0])

    cos, sin = _rope_tables(L)
    hn = _norm_in(xs, norm_mix, "norm_mix")
    ZT = 512
    zp_tile = lambda j: jnp.where(j < 2, j, jnp.where(j < D_IN // ZT - 1, j + 1, 2))
    z = _mm(hn, win_t, "nt", "in_proj", out_dtype=BF16, n_tiles=(ZT, zp_tile))
    qr, kr, vb, kt = _qkv_prep(z, cos, sin, q_norm, k_norm)
    o, lse, (wglu3, wout3, wpg3, wpp3) = _attn_fwd(qr, kr, vb, later_weights)
    wout = wout3.reshape(-1, D)
    wpg = wpg3.reshape(-1, D)
    u_off = 2 * D_ATTN
    u_perm = _seg_perm(z[:, u_off:u_off + D_SSM])
    ys_perm, ssm_ent = _ssm_fwd(u_perm, ssm_prm, ssm_d)
    ys = _seg_unperm(ys_perm)

    tm = _pick(L, (256,))

    def gelu_body(y_ref, o_ref):
        o_ref[...] = _gelu(y_ref[...]).astype(BF16)

    (gy,) = _rowcall(gelu_body, "gelu", L, tm, [(ys, _rspec(tm, D_SSM))], [(D_SSM, BF16)])
    glu = _mm(gy, wglu3, "nn", "glu_proj", out_dtype=BF16, bias=b_glu, b_blk=True)

    def mix_body(o_ref, ga_ref, gla_ref, glb_ref, gs_ref, cat_ref):
        ga, gs = _f32(ga_ref), _f32(gs_ref)
        cat_ref[:, :D_ATTN] = (o_ref[...] * ga * _sigmoid(ga)).astype(BF16)
        cat_ref[:, D_ATTN:] = (_f32(gla_ref) * _sigmoid(_f32(glb_ref)) * gs * _sigmoid(gs)).astype(BF16)

    (cat,) = _rowcall(mix_body, "mix", L, tm,
                      [(o, _rspec(tm, D_ATTN)), (z, _rspec(tm, D_ATTN, 1)), (glu, _rspec(tm, D_SSM, 0)),
                       (glu, _rspec(tm, D_SSM, 1)), (z, _rspec(tm, D_SSM, 3))], [(D_ATTN + D_SSM, BF16)])
    def out_post(prod, x_ref, g_ref):
        h1v = prod + x_ref[...]
        return h1v, h1v * _rms(h1v) * g_ref[...]

    h1, n2 = _mm_rows(cat, wout, "nn", "out_proj", out_post, [xs], [norm_ple], [(D, F32), (D, BF16)])
    pb = ps.astype(BF16)
    pp = _mm(pb, wpp3, "nn", "ple_proj", out_dtype=BF16, b_blk=True)

    nf = norm_final.reshape(1, D)

    def tail_post(gp, h1_ref, pp_ref, t_ref, g_ref):
        gate = _sigmoid(gp)
        ppv = _f32(pp_ref)
        h2 = h1_ref[...] + gate * ppv
        r = _rms(h2)
        hh = h2 * r
        err = hh * g_ref[...] - t_ref[...]
        loss_part = jnp.broadcast_to(0.5 * jnp.sum(jnp.mean(err * err, axis=-1, keepdims=True)), (1, LANES))
        dy = err * (1.0 / D)
        dh2 = _rms_bwd(dy, hh, r, g_ref[...])
        return dh2, dh2 * gate, dh2 * ppv * gate * (1.0 - gate), loss_part, _colsum(dy * hh)

    dh2, dpp, dsg, loss_acc, d_nf = _mm_rows(n2, wpg, "nn", "tail", tail_post, [h1, pp, tgt], [nf],
                                             [(D, F32), (D, BF16), (D, BF16)], [(1, LANES), (1, D)], vmem=58 << 20)

    g_wpp3 = _mm(pb, dpp, "tn", "d_ple_proj", out_dtype=BF16, out_blk=n_pp)
    g_wpg = _mm(n2, dsg, "tn", "d_ple_gate", out_dtype=BF16)
    def ple_bwd_post(dn, h1_ref, dh2_ref, g_ref):
        h1v = h1_ref[...]
        r = _rms(h1v)
        hh = h1v * r
        dh1 = dh2_ref[...] + _rms_bwd(dn, hh, r, g_ref[...])
        return dh1, dh1, _colsum(dn * hh)

    dh1, dh1b, d_nple = _mm_rows(dsg, wpg, "nt", "ple_bwd", ple_bwd_post, [h1, dh2], [norm_ple],
                                 [(D, F32), (D, BF16)], [(1, D)])

    dcat = _mm(dh1b, wout, "nt", "d_cat", out_dtype=BF16)
    g_wout = _mm(cat, dh1b, "tn", "d_out_proj", out_dtype=BF16)

    def mix_bwd_body(dca_ref, dcs_ref, o_ref, ga_ref, gla_ref, glb_ref, gs_ref,
                     do_ref, dga_ref, dgs_ref, dglu_ref, db_ref):
        dca, dcs, ga, gs = _f32(dca_ref), _f32(dcs_ref), _f32(ga_ref), _f32(gs_ref)
        sa, ss, sb = _sigmoid(ga), _sigmoid(gs), _sigmoid(_f32(glb_ref))
        gla = _f32(gla_ref)
        do_ref[...] = (dca * ga * sa).astype(BF16)
        dga_ref[...] = (dca * o_ref[...] * sa * (1.0 + ga * (1.0 - sa))).astype(BF16)
        dgs_ref[...] = (dcs * gla * sb * ss * (1.0 + gs * (1.0 - ss))).astype(BF16)
        dy2 = dcs * gs * ss
        da, db = dy2 * sb, dy2 * gla * sb * (1.0 - sb)
        dglu_ref[:, :D_SSM] = da.astype(BF16)
        dglu_ref[:, D_SSM:] = db.astype(BF16)
        _acc(db_ref, jnp.concatenate([_colsum(da), _colsum(db)], axis=-1))

    do, dga, dgs, dglu, g_bglu = _rowcall(
        mix_bwd_body, "mix_bwd", L, tm,
        [(dcat, _rspec(tm, D_ATTN, 0)), (dcat, _rspec(tm, D_SSM, 1)), (o, _rspec(tm, D_ATTN)),
         (z, _rspec(tm, D_ATTN, 1)), (glu, _rspec(tm, D_SSM, 0)), (glu, _rspec(tm, D_SSM, 1)),
         (z, _rspec(tm, D_SSM, 3))],
        [(D_ATTN, BF16), (D_ATTN, BF16), (D_SSM, BF16), (2 * D_SSM, BF16)], [(1, 2 * D_SSM)])

    g_wglu3 = _mm(gy, dglu, "tn", "d_glu_proj", out_dtype=BF16, out_blk=n_glu)
    dys = _mm(dglu, wglu3, "nt", "d_ssm_out", b_blk=True,
              post=(lambda out, y: out * _gelu_grad(y), ys))
    du_perm, pg = _ssm_bwd(u_perm, _seg_perm(dys), ssm_ent, ssm_prm, ssm_d)
    du = _seg_unperm(du_perm)

    pg_send = pg.reshape(N_DEV, (n_slab // N_DEV) * 2 * PG_ROWS, SLAB_S)
    dqs, dkr, dvv, (l_wglu, l_wout, l_wpg, l_wpp, l_pg) = _attn_bwd(
        qr, kr, vb, kt, do, o, lse,
        _Carry("a2a", [g_wglu3, g_wout.reshape(N_DEV, -1, D), g_wpg.reshape(N_DEV, -1, D), g_wpp3, pg_send]))

    scale = HEAD_DIM ** -0.5
    kblk = 4 * D_ATTN // D_KV

    a0, k0, v0, u0, s0 = D_ATTN + 2 * D_KV, D_ATTN, D_ATTN + D_KV, 2 * D_ATTN + 2 * D_KV, 2 * D_ATTN + 2 * D_KV + D_SSM

    def qkv_bwd_body(dq_ref, dk_ref, dv_ref, q_ref, k_ref, cos_ref, sin_ref, qn_ref, kn_ref, dga_ref, du_ref, dgs_ref,
                     dz_ref, dqn_ref, dkn_ref):
        c, s = cos_ref[...], sin_ref[...]
        dz_ref[:, a0:a0 + D_ATTN] = dga_ref[...]
        dz_ref[:, u0:u0 + D_SSM] = du_ref[...].astype(BF16)
        dz_ref[:, s0:s0 + D_SSM] = dgs_ref[...]

        def head(g, xh, w):
            dn = g * c + _partner(g * s)
            r = _rms(xh)
            xhat = xh * r
            return _rms_bwd(dn, xhat, r, w), _colsum(dn * xhat)

        dqn = jnp.zeros((1, HEAD_DIM), F32)
        for h in range(N_HEADS):
            sl = slice(h * HEAD_DIM, (h + 1) * HEAD_DIM)
            dx, dw = head(dq_ref[:, sl] * scale, q_ref[:, sl].astype(F32), qn_ref[...])
            dz_ref[:, sl] = dx.astype(BF16)
            dqn = dqn + dw
        dkn = jnp.zeros((1, HEAD_DIM), F32)
        for h in range(N_KV):
            sl = slice(h * HEAD_DIM, (h + 1) * HEAD_DIM)
            dx, dw = head(dk_ref[:, sl], k_ref[:, sl].astype(F32), kn_ref[...])
            dz_ref[:, k0 + h * HEAD_DIM:k0 + (h + 1) * HEAD_DIM] = dx.astype(BF16)
            dkn = dkn + dw
        dz_ref[:, v0:v0 + D_KV] = dv_ref[...].astype(BF16)
        _acc(dqn_ref, dqn)
        _acc(dkn_ref, dkn)

    dz, g_qn, g_kn = _rowcall(
        qkv_bwd_body, "qkv_bwd", L, tm,
        [(dqs, _rspec(tm, D_ATTN)), (dkr, _rspec(tm, D_KV)), (dvv, _rspec(tm, D_KV)),
         (z, _rspec(tm, D_ATTN, 0)), (z, _rspec(tm, D_KV, kblk)), (cos, _rspec(tm, HEAD_DIM)),
         (sin, _rspec(tm, HEAD_DIM)), (q_norm, _fspec(q_norm.shape)), (k_norm, _fspec(k_norm.shape)),
         (dga, _rspec(tm, D_ATTN)), (du, _rspec(tm, D_SSM)), (dgs, _rspec(tm, D_SSM))],
        [(D_IN, BF16)], [(1, HEAD_DIM), (1, HEAD_DIM)])

    g_win_t = _mm(dz, hn, "tn", "d_in_proj", out_dtype=BF16)
    g_win8 = g_win_t.reshape(N_DEV, n_in, D)
    from_sibling = _sibling_swap(g_win8, "swap_d_w_in")
    pair = _pair_sum(g_win8, from_sibling, "pair_sum_d_w_in")
    dhn, (l_win_t,) = _mm(dz, win_t, "nn", "d_norm_mix_in", out_dtype=BF16,
                          carry=_Carry("a2a_chips", [pair]))

    def in_bwd_body(x_ref, dn_ref, dh1_ref, g_ref, dx_ref, dg_ref):
        xv = x_ref[...]
        r = _rms(xv)
        hh = xv * r
        dn = _f32(dn_ref)
        _acc(dg_ref, _colsum(dn * hh))
        dx_ref[...] = dh1_ref[...] + _rms_bwd(dn, hh, r, g_ref[...])

    grad_x, g_nmix = _rowcall(
        in_bwd_body, "in_bwd", L, tm,
        [(xs, _rspec(tm, D)), (dhn, _rspec(tm, D)), (dh1, _rspec(tm, D)), (norm_mix, _fspec(norm_mix.shape))],
        [(D, F32)], [(1, D)])

    tiny_parts = [g_nmix, g_bglu, d_nple, d_nf, g_qn, g_kn, loss_acc[:, :1]]
    tiny_flat = jnp.concatenate([t.reshape(-1) for t in tiny_parts])
    tiny_rows = -(-tiny_flat.shape[0] // (8 * LANES)) * 8
    tiny = jnp.pad(tiny_flat, (0, tiny_rows * LANES - tiny_flat.shape[0])).reshape(tiny_rows, LANES)
    pg_sum = _sum_blocks(l_pg, "sum_ssm_grads")
    pg_all, tiny_all = _all_gather([pg_sum, tiny], "gather_small_grads")
    (g_bt_re, g_bt_im, g_c_re, g_c_im, g_a_re, g_a_im, g_ldt, g_skip) = _ssm_param_grads(
        pg_all.reshape(n_slab, 2, PG_ROWS, SLAB_S), ssm_prm)
    tiny_sum = _sum_blocks(tiny_all, "sum_tiny_grads").reshape(-1)
    tiny_grads, off = [], 0
    for t in tiny_parts:
        tiny_grads.append(tiny_sum[off:off + t.size].reshape(t.shape))
        off += t.size
    r_nmix, r_bglu, r_nple, r_nf, r_qn, r_kn, loss = tiny_grads
    loss = loss.reshape(())

    grads, deltas, new_ms, new_vs = {}, {}, {}, {}
    outs = _adamw_reduce(w_in[0].T, l_win_t, m_w_in[0].T, v_w_in[0].T, "adamw_w_in")
    grads["w_in"], deltas["w_in"], new_ms["w_in"], new_vs["w_in"] = [t.T[None] for t in outs]
    big = [("w_glu", w_glu, l_wglu, m_w_glu, v_w_glu),
           ("w_out", w_out, l_wout, m_w_out, v_w_out), ("w_ple_gate", w_ple_gate, l_wpg, m_w_ple_gate, v_w_ple_gate),
           ("w_ple_proj", w_ple_proj, l_wpp, m_w_ple_proj, v_w_ple_proj)]
    for name, w, ld, m, v in big:
        shp = w.shape
        outs = _adamw_reduce(w[0], ld, m[0], v[0], "adamw_" + name)
        grads[name], deltas[name], new_ms[name], new_vs[name] = [t.reshape(shp) for t in outs]
    bt2 = (2 * G * SSM_H, SSM_P)
    for name, w, g, m, v in (("ssm_b_re", ssm_b_re, g_bt_re, m_ssm_b_re, v_ssm_b_re),
                             ("ssm_b_im", ssm_b_im, g_bt_im, m_ssm_b_im, v_ssm_b_im)):
        to2 = lambda t: t[0].transpose(0, 1, 3, 2).reshape(bt2)
        back = lambda t: t.reshape(2, G, SSM_H, SSM_P).transpose(0, 1, 3, 2)[None]
        outs = _adamw(to2(w), g.reshape(bt2), to2(m), to2(v), "adamw_" + name)
        grads[name] = back(g)
        deltas[name], new_ms[name], new_vs[name] = [back(t) for t in outs]
    small = [("norm_mix", norm_mix, r_nmix, m_norm_mix, v_norm_mix, (1, D)),
             ("q_norm", q_norm, r_qn, m_q_norm, v_q_norm, (1, HEAD_DIM)),
             ("k_norm", k_norm, r_kn, m_k_norm, v_k_norm, (1, HEAD_DIM)),
             ("ssm_a_re", ssm_a_re, g_a_re, m_ssm_a_re, v_ssm_a_re, (2 * G, SSM_P)),
             ("ssm_a_im", ssm_a_im, g_a_im, m_ssm_a_im, v_ssm_a_im, (2 * G, SSM_P)),
             ("ssm_log_dt", ssm_log_dt, g_ldt, m_ssm_log_dt, v_ssm_log_dt, (2, G)),
             ("ssm_c_re", ssm_c_re, g_c_re, m_ssm_c_re, v_ssm_c_re, (2 * G * SSM_H, SSM_P)),
             ("ssm_c_im", ssm_c_im, g_c_im, m_ssm_c_im, v_ssm_c_im, (2 * G * SSM_H, SSM_P)),
             ("ssm_d", ssm_d, g_skip, m_ssm_d, v_ssm_d, (1, D_SSM)),
             ("b_glu", b_glu, r_bglu, m_b_glu, v_b_glu, (1, 2 * D_SSM)),
             ("norm_ple", norm_ple, r_nple, m_norm_ple, v_norm_ple, (1, D)),
             ("norm_final", norm_final, r_nf, m_norm_final, v_norm_final, (1, D))]
    group = [it for it in small if it[5][0] * it[5][1] <= (1 << 14)]
    grouped = {it[0] for it in group}
    for name, w, g, m, v, s2 in small:
        if name in grouped:
            continue
        shp = w.shape
        outs = _adamw(w.reshape(s2), g.reshape(s2), m.reshape(s2), v.reshape(s2), "adamw_" + name)
        grads[name] = g.reshape(shp)
        deltas[name], new_ms[name], new_vs[name] = [t.reshape(shp) for t in outs]
    ds, mns, vns = _adamw_group(*[[it[i].reshape(it[5]) for it in group] for i in (1, 2, 3, 4)], "adamw_tiny")
    for (name, w, g, _, _, _), d_, m_, v_ in zip(group, ds, mns, vns):
        shp = w.shape
        grads[name] = g.reshape(shp)
        deltas[name], new_ms[name], new_vs[name] = d_.reshape(shp), m_.reshape(shp), v_.reshape(shp)

    order = ["norm_mix", "w_in", "q_norm", "k_norm", "ssm_a_re", "ssm_a_im", "ssm_log_dt", "ssm_b_re", "ssm_b_im",
             "ssm_c_re", "ssm_c_im", "ssm_d", "w_glu", "b_glu", "w_out", "norm_ple", "w_ple_gate", "w_ple_proj",
             "norm_final"]
    return (loss, grad_x[None], *[grads[k] for k in order], *[deltas[k] for k in order],
            *[new_ms[k] for k in order], *[new_vs[k] for k in order])
```

```python
import functools
import math

import numpy as np
import jax
import jax.numpy as jnp
from jax import lax
from jax.experimental import pallas as pl
from jax.experimental.pallas import tpu as pltpu

F32 = jnp.float32
BF16 = jnp.bfloat16

N_DEV = 8
N_CHIPS = 4
EPS = 1e-6
GRID_W = 64
ROPE_THETA = 10000.0
HEAD_DIM = 128
N_HEADS = 8
N_KV = 2
REP = N_HEADS // N_KV
D_ATTN = N_HEADS * HEAD_DIM
D_KV = N_KV * HEAD_DIM
SSM_H = 16
SSM_P = 64
SLAB = 128
SLAB_G = SLAB // SSM_H
SLAB_S = SLAB_G * SSM_P
SEG = 8
CHAINS = 2
SCAN_UNROLL = 8
LANES = 128
PG_ROWS = 72
VMEM_LIMIT = 48 << 20

ADAM_LR = 0.001
ADAM_B1 = 0.9
ADAM_B2 = 0.999
ADAM_EPS = 1e-08
ADAM_WD = 0.01
ADAM_STEP = 10


def _pick(n, cands):
    for c in cands:
        if n % c == 0:
            return c
    return n


def _cparams(sem, vmem=VMEM_LIMIT):
    return pltpu.CompilerParams(dimension_semantics=sem, vmem_limit_bytes=vmem)


class _Carry:
    def __init__(self, kind, xs):
        self.kind, self.xs, self.n = kind, list(xs), len(xs)
        self.ks = (2, 4, 6) if kind == "a2a_chips" else tuple(range(1, N_DEV))
        self.index = _chip_index if kind == "a2a_chips" else _dev_index
        lead = (N_DEV,) if kind == "gather" else ()
        self.out_shape = [jax.ShapeDtypeStruct(lead + v.shape, v.dtype) for v in xs]
        self.specs = [pl.BlockSpec(memory_space=pl.ANY)] * self.n
        self.scratch = [pltpu.SemaphoreType.DMA((self.n, len(self.ks))), pltpu.SemaphoreType.DMA((self.n, len(self.ks))),
                        pltpu.SemaphoreType.DMA((self.n,))]

    def _copies(self, x_refs, out_refs, sems):
        send_sems, recv_sems, local_sems = sems
        x, y, c = lax.axis_index("x"), lax.axis_index("y"), lax.axis_index("c")
        me = self.index((x, y, c))
        mine, sends, arrivals = [], [], []
        for a in range(self.n):
            src_mine = x_refs[a] if self.kind == "gather" else x_refs[a].at[me]
            mine.append(pltpu.make_async_copy(src_mine, out_refs[a].at[me], local_sems.at[a]))
            for s, k in enumerate(self.ks):
                peer = _peer(k, x, y, c)
                src = x_refs[a] if self.kind == "gather" else x_refs[a].at[self.index(peer)]
                sends.append(pltpu.make_async_remote_copy(
                    src_ref=src, dst_ref=out_refs[a].at[me], send_sem=send_sems.at[a, s],
                    recv_sem=recv_sems.at[a, s], device_id=peer, device_id_type=pl.DeviceIdType.MESH))
                land = out_refs[a].at[self.index(peer)]
                arrivals.append(pltpu.make_async_remote_copy(
                    src_ref=land, dst_ref=land, send_sem=send_sems.at[a, s],
                    recv_sem=recv_sems.at[a, s], device_id=peer, device_id_type=pl.DeviceIdType.MESH))
        return mine, sends, arrivals

    def start(self, x_refs, out_refs, sems):
        mine, sends, _ = self._copies(x_refs, out_refs, sems)
        for cp in mine + sends:
            cp.start()

    def wait(self, x_refs, out_refs, sems):
        mine, sends, arrivals = self._copies(x_refs, out_refs, sems)
        for cp in arrivals:
            cp.wait_recv()
        for cp in sends:
            cp.wait_send()
        for cp in mine:
            cp.wait()


def _grid_edges(grid):
    first = functools.reduce(lambda p, q: p & q, [pl.program_id(d) == 0 for d in range(len(grid))])
    last = functools.reduce(lambda p, q: p & q, [pl.program_id(d) == g - 1 for d, g in enumerate(grid)])
    return first, last


def _mm(a, b, mode, name, out_dtype=F32, add=None, bias=None, a_blk=False, b_blk=False, out_blk=0, carry=None,
        n_tiles=None, post=None):
    w = b.shape[2] if b_blk else out_blk
    if mode == "nn":
        M, K = a.shape
        N = b.shape[0] * w if b_blk else b.shape[1]
    elif mode == "nt":
        M = a.shape[1] if a_blk else a.shape[0]
        N = b.shape[1] if b_blk else b.shape[0]
        K = b.shape[0] * w if b_blk else b.shape[1]
    else:
        K, M = a.shape
        N = b.shape[0] * w if b_blk else b.shape[1]
    tm = _pick(M, (1024, 768, 512, 256))
    tn = _pick(N, (1024, 768, 512, 256))
    if mode == "tn" and N <= 2048:
        tn = N
    tk = K if (mode != "tn" and K <= 2048) else _pick(K, (1024, 768, 512, 256))
    if mode == "nn" and K > 2048 and N <= 2048:
        tn, tk = N, _pick(K, (1536, 1024, 768, 512, 256))
    perm = lambda j: j
    if n_tiles:
        tn, perm = n_tiles
        tm = _pick(M, (2048, 1024, 512, 256))
    if mode == "nt" and b_blk:
        tk = w
    elif b_blk or out_blk:
        tn = w
    nk = K // tk
    grid = (M // tm, N // tn, nk)
    if mode == "nn":
        a_spec = pl.BlockSpec((tm, tk), lambda i, j, k: (i, k))
        b_spec = (pl.BlockSpec((1, tk, tn), lambda i, j, k: (j, k, 0)) if b_blk
                  else pl.BlockSpec((tk, tn), lambda i, j, k: (k, j)))
        dims = (((1,), (0,)), ((), ()))
    elif mode == "nt":
        a_spec = (pl.BlockSpec((1, tm, tk), lambda i, j, k: (k, i, 0)) if a_blk
                  else pl.BlockSpec((tm, tk), lambda i, j, k: (i, k)))
        b_spec = (pl.BlockSpec((1, tn, tk), lambda i, j, k: (k, j, 0)) if b_blk
                  else pl.BlockSpec((tn, tk), lambda i, j, k: (perm(j), k)))
        dims = (((1,), (1,)), ((), ()))
    else:
        a_spec = pl.BlockSpec((tk, tm), lambda i, j, k: (k, i))
        b_spec = (pl.BlockSpec((1, tk, tn), lambda i, j, k: (j, k, 0)) if b_blk
                  else pl.BlockSpec((tk, tn), lambda i, j, k: (k, j)))
        dims = (((0,), (0,)), ((), ()))
    if out_blk:
        out_spec = pl.BlockSpec((1, tm, tn), lambda i, j, k: (j, i, 0))
        out_shape = jax.ShapeDtypeStruct((N // tn, M, tn), out_dtype)
    else:
        out_spec = pl.BlockSpec((tm, tn), lambda i, j, k: (i, j))
        out_shape = jax.ShapeDtypeStruct((M, N), out_dtype)
    extras, extra_specs, combine = [], [], []
    if add is not None:
        extras.append(add)
        extra_specs.append(pl.BlockSpec((tm, tn), lambda i, j, k: (i, j)))
        combine.append(lambda out, t: out + t)
    if bias is not None:
        extras.append(bias)
        extra_specs.append(pl.BlockSpec((1, tn), lambda i, j, k: (0, j)))
        combine.append(lambda out, t: out + t)
    if post is not None:
        extras.append(post[1])
        extra_specs.append(pl.BlockSpec((tm, tn), lambda i, j, k: (i, j)))
        combine.append(post[0])

    n_ex = len(extras)
    nc = carry.n if carry else 0

    def body(a_ref, b_ref, *rest):
        ex_refs, cx = rest[:n_ex], rest[n_ex:n_ex + nc]
        o_ref, cout = rest[n_ex + nc], rest[n_ex + nc + 1:n_ex + 2 * nc + 1]
        tail = rest[n_ex + 2 * nc + 1:]
        sems = tail[:3] if carry else ()
        first, last = _grid_edges(grid)
        if carry:
            @pl.when(first)
            def _():
                carry.start(cx, cout, sems)

        def product():
            av = a_ref[0] if a_blk else a_ref[...]
            bv = b_ref[0] if b_blk else b_ref[...]
            return lax.dot_general(av, bv, dims, preferred_element_type=F32)

        def finish(out):
            for r, fn in zip(ex_refs, combine):
                out = fn(out, r[...])
            if out_blk:
                o_ref[0] = out.astype(out_dtype)
            else:
                o_ref[...] = out.astype(out_dtype)

        if nk == 1:
            finish(product())
        else:
            acc_ref = tail[-1]
            k = pl.program_id(2)

            @pl.when(k == 0)
            def _():
                acc_ref[...] = jnp.zeros_like(acc_ref)

            acc_ref[...] += product()

            @pl.when(k == nk - 1)
            def _():
                finish(acc_ref[...])

        if carry:
            @pl.when(last)
            def _():
                carry.wait(cx, cout, sems)

    scratch = (carry.scratch if carry else []) + ([pltpu.VMEM((tm, tn), F32)] if nk > 1 else [])
    outs = pl.pallas_call(
        body, name=name, grid=grid,
        in_specs=[a_spec, b_spec] + extra_specs + (carry.specs if carry else []),
        out_specs=[out_spec] + (carry.specs if carry else []),
        out_shape=[out_shape] + (carry.out_shape if carry else []),
        scratch_shapes=scratch,
        compiler_params=_cparams(("arbitrary", "arbitrary", "arbitrary")),
    )(a, b, *extras, *(carry.xs if carry else []))
    return (outs[0], outs[1:]) if carry else outs[0]


def _rspec(tm, w, cb=0):
    return pl.BlockSpec((tm, w), lambda i: (i, cb))


def _fspec(shape):
    nd = len(shape)
    return pl.BlockSpec(shape, lambda i: (0,) * nd)


def _rowcall(body, name, L, tm, ins, row_outs, acc_outs=()):
    out_shape = [jax.ShapeDtypeStruct((L, w), dt) for w, dt in row_outs]
    out_shape += [jax.ShapeDtypeStruct(s, F32) for s in acc_outs]
    out_specs = [_rspec(tm, w) for w, _ in row_outs] + [_fspec(s) for s in acc_outs]
    return pl.pallas_call(
        body, name=name, grid=(L // tm,),
        in_specs=[s for _, s in ins], out_specs=out_specs, out_shape=out_shape,
        compiler_params=_cparams(("arbitrary",)),
    )(*[a for a, _ in ins])


def _mm_rows(a, b, mode, name, post, row_ins, full_ins, row_outs, acc_outs=(), vmem=VMEM_LIMIT):
    M, K = a.shape
    N = b.shape[1] if mode == "nn" else b.shape[0]
    tm = _pick(M, (256,))
    dims = (((1,), (0,)), ((), ())) if mode == "nn" else (((1,), (1,)), ((), ()))
    n_in, n_row = len(row_ins) + len(full_ins), len(row_outs)

    def body(a_ref, b_ref, *rest):
        ins, outs = rest[:n_in], rest[n_in:]
        prod = lax.dot_general(a_ref[...], b_ref[...], dims, preferred_element_type=F32)
        res = post(prod, *ins)
        for o_ref, val in zip(outs[:n_row], res[:n_row]):
            o_ref[...] = val.astype(o_ref.dtype)
        for acc_ref, val in zip(outs[n_row:], res[n_row:]):
            _acc(acc_ref, val)

    out_shape = [jax.ShapeDtypeStruct((M, w), dt) for w, dt in row_outs]
    out_shape += [jax.ShapeDtypeStruct(s, F32) for s in acc_outs]
    row_ins = [r if isinstance(r, tuple) else (r, r.shape[1], 0) for r in row_ins]
    return pl.pallas_call(
        body, name=name, grid=(M // tm,),
        in_specs=[_rspec(tm, K), _fspec(b.shape)] + [_rspec(tm, w, cb) for _, w, cb in row_ins]
        + [_fspec(f.shape) for f in full_ins],
        out_specs=[_rspec(tm, w) for w, _ in row_outs] + [_fspec(s) for s in acc_outs],
        out_shape=out_shape, compiler_params=_cparams(("arbitrary",), vmem),
    )(a, b, *[r for r, _, _ in row_ins], *full_ins)


def _acc(ref, val):
    @pl.when(pl.program_id(0) == 0)
    def _():
        ref[...] = jnp.zeros_like(ref)
    ref[...] += val


def _colsum(v):
    return jnp.sum(v, axis=0, keepdims=True)


def _rms(xv):
    return lax.rsqrt(jnp.mean(xv * xv, axis=-1, keepdims=True) + EPS)


def _rms_bwd(dn, xhat, r, g):
    dng = dn * g
    return r * (dng - xhat * jnp.mean(dng * xhat, axis=-1, keepdims=True))


def _sigmoid(v):
    return jax.nn.sigmoid(v)


def _f32(ref):
    return ref[...].astype(F32)


def _partner(v):
    w = v.shape[-1]
    lane = lax.broadcasted_iota(jnp.int32, v.shape, v.ndim - 1)
    first_half = (lane % 64) < 32
    return jnp.where(first_half, pltpu.roll(v, w - 32, axis=v.ndim - 1), pltpu.roll(v, 32, axis=v.ndim - 1))


def _norm_in(x, g, name):
    L, D = x.shape
    tm = _pick(L, (512, 256))

    def body(x_ref, g_ref, o_ref):
        xv = x_ref[...]
        o_ref[...] = (xv * _rms(xv) * g_ref[...]).astype(BF16)

    return _rowcall(body, name, L, tm, [(x, _rspec(tm, D)), (g, _fspec(g.shape))], [(D, BF16)])[0]


def _rope_tables(L):
    t = np.arange(L)
    rows = (t // GRID_W).astype(np.float32)
    cols = (t % GRID_W).astype(np.float32)
    n_freq = HEAD_DIM // 4
    inv_freq = np.float32(ROPE_THETA) ** (-np.arange(n_freq, dtype=np.float32) / np.float32(n_freq))
    ar = (rows[:, None] * inv_freq[None, :]).astype(np.float32).astype(np.float64)
    ac = (cols[:, None] * inv_freq[None, :]).astype(np.float32).astype(np.float64)
    cos = np.concatenate([np.cos(ar), np.cos(ar), np.cos(ac), np.cos(ac)], axis=-1).astype(np.float32)
    sin = np.concatenate([-np.sin(ar), np.sin(ar), -np.sin(ac), np.sin(ac)], axis=-1).astype(np.float32)
    return jnp.asarray(cos), jnp.asarray(sin)


def _qkv_prep(z, cos, sin, qn, kn):
    L = z.shape[0]
    tm = _pick(L, (512, 256))
    scale = HEAD_DIM ** -0.5
    kblk = 4 * D_ATTN // D_KV

    def body(q_ref, k_ref, v_ref, cos_ref, sin_ref, qn_ref, kn_ref, qo_ref, ko_ref, vo_ref, kt_ref):
        c, s = cos_ref[...], sin_ref[...]

        def head(xh, w):
            n = xh * _rms(xh) * w
            return n * c + _partner(n) * s

        for h in range(N_HEADS):
            sl = slice(h * HEAD_DIM, (h + 1) * HEAD_DIM)
            qo_ref[:, sl] = (head(q_ref[:, sl].astype(F32), qn_ref[...]) * scale).astype(BF16)
        for h in range(N_KV):
            sl = slice(h * HEAD_DIM, (h + 1) * HEAD_DIM)
            kr = head(k_ref[:, sl].astype(F32), kn_ref[...])
            ko_ref[:, sl] = kr.astype(BF16)
            kt_ref[sl, :] = kr.T.astype(BF16)
        vo_ref[...] = v_ref[...].astype(BF16)

    return pl.pallas_call(
        body, name="qkv_prep", grid=(L // tm,),
        in_specs=[_rspec(tm, D_ATTN, 0), _rspec(tm, D_KV, kblk), _rspec(tm, D_KV, kblk + 1),
                  _rspec(tm, HEAD_DIM), _rspec(tm, HEAD_DIM), _fspec(qn.shape), _fspec(kn.shape)],
        out_specs=[_rspec(tm, D_ATTN), _rspec(tm, D_KV), _rspec(tm, D_KV),
                   pl.BlockSpec((D_KV, tm), lambda i: (0, i))],
        out_shape=[jax.ShapeDtypeStruct((L, D_ATTN), BF16), jax.ShapeDtypeStruct((L, D_KV), BF16),
                   jax.ShapeDtypeStruct((L, D_KV), BF16), jax.ShapeDtypeStruct((D_KV, L), BF16)],
        compiler_params=_cparams(("arbitrary",)),
    )(z, z, z, cos, sin, qn, kn)


def _col_to_row(col):
    n = col.shape[0]
    eye = lax.broadcasted_iota(jnp.int32, (n, n), 0) == lax.broadcasted_iota(jnp.int32, (n, n), 1)
    return jnp.sum(jnp.where(eye, col, 0.0), axis=0, keepdims=True)


def _attn_fwd(q, k, v, carry=None):
    L = q.shape[0]
    tq = _pick(L, (256, 128))
    grid = (N_HEADS, L // tq)
    nc = carry.n if carry else 0

    def body(q_ref, k_ref, v_ref, *rest):
        cx, (o_ref, lse_ref) = rest[:nc], rest[nc:nc + 2]
        cout, sems = rest[nc + 2:2 * nc + 2], rest[2 * nc + 2:]
        first, last = _grid_edges(grid)
        if carry:
            @pl.when(first)
            def _():
                carry.start(cx, cout, sems)

        s = lax.dot_general(q_ref[...], k_ref[...], (((1,), (1,)), ((), ())), preferred_element_type=F32)
        m = jnp.max(s, axis=-1, keepdims=True)
        e = jnp.exp(s - m)
        l = jnp.sum(e, axis=-1, keepdims=True)
        o_ref[...] = jnp.dot(e.astype(BF16), v_ref[...], preferred_element_type=F32) / l
        lse_ref[0] = _col_to_row(m + jnp.log(l))

        if carry:
            @pl.when(last)
            def _():
                carry.wait(cx, cout, sems)

    outs = pl.pallas_call(
        body, name="attn_fwd", grid=grid,
        in_specs=[pl.BlockSpec((tq, HEAD_DIM), lambda h, i: (i, h)),
                  pl.BlockSpec((L, HEAD_DIM), lambda h, i: (0, h // REP)),
                  pl.BlockSpec((L, HEAD_DIM), lambda h, i: (0, h // REP))] + (carry.specs if carry else []),
        out_specs=[pl.BlockSpec((tq, HEAD_DIM), lambda h, i: (i, h)),
                   pl.BlockSpec((1, 1, tq), lambda h, i: (h, 0, i))] + (carry.specs if carry else []),
        out_shape=[jax.ShapeDtypeStruct((L, D_ATTN), F32), jax.ShapeDtypeStruct((N_HEADS, 1, L), F32)]
        + (carry.out_shape if carry else []),
        scratch_shapes=carry.scratch if carry else [],
        compiler_params=_cparams(("arbitrary", "arbitrary")),
    )(q, k, v, *(carry.xs if carry else []))
    return outs[0], outs[1], outs[2:]


def _attn_bwd(q, k, v, kt, do, o, lse, carry=None):
    L = q.shape[0]
    tq = _pick(L, (256, 128))
    kc = _pick(L, (512, 256, 128))
    nt = (((1,), (1,)), ((), ()))
    grid = (N_KV, REP, L // tq)
    nc = carry.n if carry else 0

    def body(q_ref, do_ref, o_ref, lse_ref, k_ref, v_ref, kt_ref, *rest):
        cx, (dq_ref, dk_ref, dv_ref) = rest[:nc], rest[nc:nc + 3]
        cout, sems = rest[nc + 3:2 * nc + 3], rest[2 * nc + 3:]
        first, last = _grid_edges(grid)
        if carry:
            @pl.when(first)
            def _():
                carry.start(cx, cout, sems)

        @pl.when((pl.program_id(1) == 0) & (pl.program_id(2) == 0))
        def _():
            dk_ref[...] = jnp.zeros_like(dk_ref)
            dv_ref[...] = jnp.zeros_like(dv_ref)

        qv, dov = q_ref[...], do_ref[...]
        lse_row = lse_ref[0]
        delta = _col_to_row(jnp.sum(dov.astype(F32) * o_ref[...], axis=-1, keepdims=True))
        dqt = jnp.zeros((HEAD_DIM, tq), F32)
        for c in range(L // kc):
            sl = slice(c * kc, (c + 1) * kc)
            st = lax.dot_general(k_ref[sl, :], qv, nt, preferred_element_type=F32)
            pt = jnp.exp(st - lse_row)
            dpt = lax.dot_general(v_ref[sl, :], dov, nt, preferred_element_type=F32)
            dst = (pt * (dpt - delta)).astype(BF16)
            dv_ref[sl, :] += jnp.dot(pt.astype(BF16), dov, preferred_element_type=F32)
            dk_ref[sl, :] += jnp.dot(dst, qv, preferred_element_type=F32)
            dqt = dqt + jnp.dot(kt_ref[:, sl], dst, preferred_element_type=F32)
        dq_ref[...] = dqt.T

        if carry:
            @pl.when(last)
            def _():
                carry.wait(cx, cout, sems)

    head = lambda g, r, i: (i, g * REP + r)
    outs = pl.pallas_call(
        body, name="attn_bwd", grid=grid,
        in_specs=[pl.BlockSpec((tq, HEAD_DIM), head), pl.BlockSpec((tq, HEAD_DIM), head),
                  pl.BlockSpec((tq, HEAD_DIM), head),
                  pl.BlockSpec((1, 1, tq), lambda g, r, i: (g * REP + r, 0, i)),
                  pl.BlockSpec((L, HEAD_DIM), lambda g, r, i: (0, g)),
                  pl.BlockSpec((L, HEAD_DIM), lambda g, r, i: (0, g)),
                  pl.BlockSpec((HEAD_DIM, L), lambda g, r, i: (g, 0))] + (carry.specs if carry else []),
        out_specs=[pl.BlockSpec((tq, HEAD_DIM), head),
                   pl.BlockSpec((L, HEAD_DIM), lambda g, r, i: (0, g)),
                   pl.BlockSpec((L, HEAD_DIM), lambda g, r, i: (0, g))] + (carry.specs if carry else []),
        out_shape=[jax.ShapeDtypeStruct((L, D_ATTN), F32), jax.ShapeDtypeStruct((L, D_KV), F32),
                   jax.ShapeDtypeStruct((L, D_KV), F32)] + (carry.out_shape if carry else []),
        scratch_shapes=carry.scratch if carry else [],
        compiler_params=_cparams(("arbitrary", "arbitrary", "arbitrary")),
    )(q, do, o, lse, k, v, kt, *(carry.xs if carry else []))
    return outs[0], outs[1], outs[2], outs[3:]


def _seg_perm(a):
    L, C = a.shape
    return a.reshape(SEG, L // SEG, C).transpose(1, 0, 2).reshape(L, C)


def _seg_unperm(a):
    L, C = a.shape
    return a.reshape(L // SEG, SEG, C).transpose(1, 0, 2).reshape(L, C)


def _cmul(ar, ai, br, bi):
    return ar * br - ai * bi, ar * bi + ai * br


def _rows8(rr):
    if isinstance(rr, int):
        return pl.ds(rr * SEG, SEG)
    return pl.ds(pl.multiple_of(rr * SEG, SEG), SEG)


def _seg_scan(xr_ref, xi_ref, ar, ai, reverse, n_rows, visit=None, visit_init=(), entering=None):
    shape = ar.shape
    zero = jnp.zeros(shape, F32)
    rc = n_rows // CHAINS

    def index(q):
        return (n_rows - 1 - q) if reverse else q

    if entering is None:
        def ends(q, carry):
            out = []
            for j in range(CHAINS):
                sl = _rows8(index(j * rc + q))
                pr, pi = _cmul(ar, ai, carry[2 * j], carry[2 * j + 1])
                out += [pr + xr_ref[sl, :], pi + xi_ref[sl, :]]
            return tuple(out)

        def ends_block(qb, carry):
            for t in range(SCAN_UNROLL):
                carry = ends(qb * SCAN_UNROLL + t, carry)
            return carry

        e = lax.fori_loop(0, rc // SCAN_UNROLL, ends_block, (zero,) * (2 * CHAINS))

        pr, pi = ar, ai
        for _ in range(int(math.log2(rc))):
            pr, pi = _cmul(pr, pi, pr, pi)
        sub = lax.broadcasted_iota(jnp.int32, shape, 0)
        shift = (SEG - 1) if reverse else 1
        edge = (SEG - 1) if reverse else 0
        entering = [(zero, zero)] * CHAINS
        for _ in range(SEG):
            tr, ti = _cmul(pr, pi, *entering[CHAINS - 1])
            cur = (jnp.where(sub == edge, 0.0, pltpu.roll(tr + e[2 * CHAINS - 2], shift, axis=0)),
                   jnp.where(sub == edge, 0.0, pltpu.roll(ti + e[2 * CHAINS - 1], shift, axis=0)))
            entering = [cur]
            for j in range(1, CHAINS):
                tr, ti = _cmul(pr, pi, *cur)
                cur = (tr + e[2 * j - 2], ti + e[2 * j - 1])
                entering.append(cur)

    def step(q, carry, last):
        out, acc = [], carry[2 * CHAINS:]
        for j in range(CHAINS):
            rr = index(j * rc + q)
            sl = _rows8(rr)
            pr, pi = _cmul(ar, ai, carry[2 * j], carry[2 * j + 1])
            nr, ni = pr + xr_ref[sl, :], pi + xi_ref[sl, :]
            xr_ref[sl, :] = nr
            xi_ref[sl, :] = ni
            if visit:
                acc = visit(rr, nr, ni, acc, last and j == CHAINS - 1)
            out += [nr, ni]
        return (*out, *acc)

    def step_block(qb, carry):
        for t in range(SCAN_UNROLL):
            carry = step(qb * SCAN_UNROLL + t, carry, False)
        return carry

    start = tuple(v for pair in entering for v in pair)
    n_blocks = (rc - 1) // SCAN_UNROLL
    carry = lax.fori_loop(0, n_blocks, step_block, (*start, *visit_init))
    for q in range(n_blocks * SCAN_UNROLL, rc - 1):
        carry = step(q, carry, False)
    carry = step(rc - 1, carry, True)
    return entering, carry[2 * CHAINS:]


def _discretise(a_re, a_im, ldt):
    lr = jnp.minimum(a_re, -1e-4)
    li = a_im
    dt = jnp.exp(ldt)
    mag = jnp.exp(lr * dt)
    lbr = mag * jnp.cos(li * dt)
    lbi = mag * jnp.sin(li * dt)
    den = lr * lr + li * li
    nr = lbr - 1.0
    fr = (nr * lr + lbi * li) / den
    fi = (lbi * lr - nr * li) / den
    return lr, li, dt, lbr, lbi, fr, fi


def _lane_row(v):
    return jnp.concatenate([v[g:g + 1, :] for g in range(v.shape[0])], axis=1)


def _ssm_fill_maps(d, prm, tmp_ref, maps):
    a_re_ref, a_im_ref, ldt_ref, bt_re_ref, bt_im_ref, c_re_ref, c_im_ref = prm
    _, _, _, lbr, lbi, fr, fi = _discretise(a_re_ref[d], a_im_ref[d], ldt_ref[d])

    def fill(dst, piece):
        tmp_ref[...] = jnp.zeros_like(tmp_ref)
        for g in range(SLAB_G):
            tmp_ref[g * SSM_H:(g + 1) * SSM_H, g * SSM_P:(g + 1) * SSM_P] = piece(g)
        dst[...] = tmp_ref[...].astype(BF16)

    wbr, wbi, wcr, wci = maps
    fill(wbr, lambda g: fr[g:g + 1] * bt_re_ref[d, g] - fi[g:g + 1] * bt_im_ref[d, g])
    fill(wbi, lambda g: fr[g:g + 1] * bt_im_ref[d, g] + fi[g:g + 1] * bt_re_ref[d, g])
    fill(wcr, lambda g: c_re_ref[d, g])
    fill(wci, lambda g: c_im_ref[d, g])
    return _lane_row(lbr), _lane_row(lbi)


def _ssm_param_specs():
    pole = pl.BlockSpec((2, SLAB_G, SSM_P), lambda j: (0, j, 0))
    step = pl.BlockSpec((2, SLAB_G, 1), lambda j: (0, j, 0))
    mat = pl.BlockSpec((2, SLAB_G, SSM_H, SSM_P), lambda j: (0, j, 0, 0))
    return [pole, pole, step, mat, mat, mat, mat]


_MAP_SCRATCH = [pltpu.VMEM((SLAB, SLAB_S), F32)] + [pltpu.VMEM((SLAB, SLAB_S), BF16)] * 4
_ENT_SPEC = pl.BlockSpec((1, 2, 2 * CHAINS, SEG, SLAB_S), lambda j: (j, 0, 0, 0, 0))
_NT = (((1,), (1,)), ((), ()))


def _ssm_fwd(u, prm, dskip):
    L, C = u.shape
    n_rows = L // SEG
    tc = _pick(L, (2048, 1024, 512, 256))
    u_spec = pl.BlockSpec((L, SLAB), lambda j: (0, j))
    d_spec = pl.BlockSpec((1, SLAB), lambda j: (0, j))

    def body(u_ref, *rest):
        prm_refs, d_ref, y_ref, ent_ref = rest[:7], rest[7], rest[8], rest[9]
        tmp_ref, maps, xr_ref, xi_ref = rest[10], rest[11:15], rest[15], rest[16]
        wbr, wbi, wcr, wci = maps
        y_ref[...] = u_ref[...] * d_ref[...]
        for d in range(2):
            lam_r, lam_i = _ssm_fill_maps(d, prm_refs, tmp_ref, maps)

            def inp(c, _):
                sl = pl.ds(pl.multiple_of(c * tc, tc), tc)
                ub = u_ref[sl, :].astype(BF16)
                xr_ref[sl, :] = jnp.dot(ub, wbr[...], preferred_element_type=F32)
                xi_ref[sl, :] = jnp.dot(ub, wbi[...], preferred_element_type=F32)
                return 0

            lax.fori_loop(0, L // tc, inp, 0)
            ar = jnp.broadcast_to(lam_r, (SEG, SLAB_S))
            ai = jnp.broadcast_to(lam_i, (SEG, SLAB_S))
            entering, _ = _seg_scan(xr_ref, xi_ref, ar, ai, d == 1, n_rows)
            for j, (er, ei) in enumerate(entering):
                ent_ref[0, d, 2 * j] = er
                ent_ref[0, d, 2 * j + 1] = ei

            def outp(c, _):
                sl = pl.ds(pl.multiple_of(c * tc, tc), tc)
                y_ref[sl, :] += (
                    lax.dot_general(xr_ref[sl, :].astype(BF16), wcr[...], _NT, preferred_element_type=F32)
                    - lax.dot_general(xi_ref[sl, :].astype(BF16), wci[...], _NT, preferred_element_type=F32))
                return 0

            lax.fori_loop(0, L // tc, outp, 0)

    return pl.pallas_call(
        body, name="ssm_fwd", grid=(C // SLAB,),
        in_specs=[u_spec] + _ssm_param_specs() + [d_spec],
        out_specs=[u_spec, _ENT_SPEC],
        out_shape=[jax.ShapeDtypeStruct((L, C), F32),
                   jax.ShapeDtypeStruct((C // SLAB, 2, 2 * CHAINS, SEG, SLAB_S), F32)],
        scratch_shapes=_MAP_SCRATCH + [pltpu.VMEM((L, SLAB_S), F32)] * 2,
        compiler_params=_cparams(("arbitrary",)),
    )(u, *prm, dskip)


def _ssm_bwd(u, dy, ent, prm, dskip):
    L, C = u.shape
    n_rows = L // SEG
    n_slab = C // SLAB
    tc = _pick(L, (2048, 1024, 512, 256))
    u_spec = pl.BlockSpec((L, SLAB), lambda j: (0, j))
    d_spec = pl.BlockSpec((1, SLAB), lambda j: (0, j))
    pg_spec = pl.BlockSpec((1, 2, PG_ROWS, SLAB_S), lambda j: (j, 0, 0, 0))

    def body(u_ref, dy_ref, ent_ref, *rest):
        prm_refs, d_ref, du_ref, pg_ref = rest[:7], rest[7], rest[8], rest[9]
        tmp_ref, maps, acc_ref = rest[10], rest[11:15], rest[15]
        xr_ref, xi_ref, gr_ref, gi_ref = rest[16:20]
        wbr, wbi, wcr, wci = maps
        du_ref[...] = dy_ref[...] * d_ref[...]
        pg_ref[...] = jnp.zeros_like(pg_ref)
        pg_ref[0, 0, 66:67, 0:SLAB] = _colsum(dy_ref[...] * u_ref[...])
        for d in range(2):
            lam_r, lam_i = _ssm_fill_maps(d, prm_refs, tmp_ref, maps)

            def inp(c, _):
                sl = pl.ds(pl.multiple_of(c * tc, tc), tc)
                ub = u_ref[sl, :].astype(BF16)
                dyb = dy_ref[sl, :].astype(BF16)
                xr_ref[sl, :] = jnp.dot(ub, wbr[...], preferred_element_type=F32)
                xi_ref[sl, :] = jnp.dot(ub, wbi[...], preferred_element_type=F32)
                gr_ref[sl, :] = jnp.dot(dyb, wcr[...], preferred_element_type=F32)
                gi_ref[sl, :] = -jnp.dot(dyb, wci[...], preferred_element_type=F32)
                return 0

            lax.fori_loop(0, L // tc, inp, 0)
            ar = jnp.broadcast_to(lam_r, (SEG, SLAB_S))
            ai = jnp.broadcast_to(lam_i, (SEG, SLAB_S))
            entering = [(ent_ref[0, d, 2 * j], ent_ref[0, d, 2 * j + 1]) for j in range(CHAINS)]
            _seg_scan(xr_ref, xi_ref, ar, ai, d == 1, n_rows, entering=entering)

            def pole(rr, lr, li, acc, last):
                if last:
                    pr, pi = entering[0]
                else:
                    nb = _rows8(rr + 1 if d == 1 else rr - 1)
                    pr, pi = xr_ref[nb, :], xi_ref[nb, :]
                return acc[0] + lr * pr + li * pi, acc[1] + li * pr - lr * pi

            zero = jnp.zeros((SEG, SLAB_S), F32)
            _, (accr, acci) = _seg_scan(gr_ref, gi_ref, ar, -ai, d == 0, n_rows, pole, (zero, zero))
            pg_ref[0, d, 64:65, :] = _colsum(accr)
            pg_ref[0, d, 65:66, :] = _colsum(acci)

            acc_ref[...] = jnp.zeros_like(acc_ref)

            def outp(c, _):
                sl = pl.ds(pl.multiple_of(c * tc, tc), tc)
                lrb, lib = gr_ref[sl, :].astype(BF16), gi_ref[sl, :].astype(BF16)
                du_ref[sl, :] += (lax.dot_general(lrb, wbr[...], _NT, preferred_element_type=F32)
                                  + lax.dot_general(lib, wbi[...], _NT, preferred_element_type=F32))
                ut = u_ref[sl, :].astype(F32).T.astype(BF16)
                dyt = dy_ref[sl, :].T.astype(BF16)
                acc_ref[0] += jnp.dot(ut, lrb, preferred_element_type=F32)
                acc_ref[1] += jnp.dot(ut, lib, preferred_element_type=F32)
                acc_ref[2] += jnp.dot(dyt, xr_ref[sl, :].astype(BF16), preferred_element_type=F32)
                acc_ref[3] -= jnp.dot(dyt, xi_ref[sl, :].astype(BF16), preferred_element_type=F32)
                return 0

            lax.fori_loop(0, L // tc, outp, 0)
            for m in range(4):
                for g in range(SLAB_G):
                    lanes = slice(g * SSM_P, (g + 1) * SSM_P)
                    pg_ref[0, d, m * SSM_H:(m + 1) * SSM_H, lanes] = acc_ref[m, g * SSM_H:(g + 1) * SSM_H, lanes]

    return pl.pallas_call(
        body, name="ssm_bwd", grid=(n_slab,),
        in_specs=[u_spec, u_spec, _ENT_SPEC] + _ssm_param_specs() + [d_spec],
        out_specs=[u_spec, pg_spec],
        out_shape=[jax.ShapeDtypeStruct((L, C), F32), jax.ShapeDtypeStruct((n_slab, 2, PG_ROWS, SLAB_S), F32)],
        scratch_shapes=_MAP_SCRATCH + [pltpu.VMEM((4, SLAB, SLAB_S), F32)] + [pltpu.VMEM((L, SLAB_S), F32)] * 4,
        compiler_params=_cparams(("arbitrary",), 60 << 20),
    )(u, dy, ent, *prm, dskip)


def _ssm_param_grads(pg, prm):
    n_slab = pg.shape[0]
    G = n_slab * SLAB_G
    pg_spec = pl.BlockSpec((1, 2, PG_ROWS, SLAB_S), lambda j: (j, 0, 0, 0))
    pole, _, step, mat = _ssm_param_specs()[:4]

    def body(pg_ref, a_re_ref, a_im_ref, ldt_ref, bt_re_ref, bt_im_ref,
             dbr_ref, dbi_ref, dcr_ref, dci_ref, dar_ref, dai_ref, dldt_ref, dd_ref):
        dd_ref[...] = pg_ref[0, 0, 66:67, 0:SLAB]
        for d in range(2):
            a_r = a_re_ref[d]
            lr, li, dt, lbr, lbi, f_r, f_i = _discretise(a_r, a_im_ref[d], ldt_ref[d])
            gfr_rows, gfi_rows, glr_rows, gli_rows = [], [], [], []
            for g in range(SLAB_G):
                lanes = slice(g * SSM_P, (g + 1) * SSM_P)
                gbr, gbi = pg_ref[0, d, 0:SSM_H, lanes], pg_ref[0, d, SSM_H:2 * SSM_H, lanes]
                b_r, b_i = bt_re_ref[d, g], bt_im_ref[d, g]
                fr, fi = f_r[g:g + 1], f_i[g:g + 1]
                dbr_ref[d, g] = fr * gbr + fi * gbi
                dbi_ref[d, g] = fr * gbi - fi * gbr
                gfr_rows.append(_colsum(gbr * b_r + gbi * b_i))
                gfi_rows.append(_colsum(gbi * b_r - gbr * b_i))
                dcr_ref[d, g] = pg_ref[0, d, 2 * SSM_H:3 * SSM_H, lanes]
                dci_ref[d, g] = pg_ref[0, d, 3 * SSM_H:4 * SSM_H, lanes]
                glr_rows.append(pg_ref[0, d, 64:65, lanes])
                gli_rows.append(pg_ref[0, d, 65:66, lanes])
            gfr, gfi = jnp.concatenate(gfr_rows, axis=0), jnp.concatenate(gfi_rows, axis=0)
            glr, gli = jnp.concatenate(glr_rows, axis=0), jnp.concatenate(gli_rows, axis=0)
            den = lr * lr + li * li
            ir, ii = lr / den, -li / den
            tr, ti = _cmul(ir, -ii, gfr, gfi)
            glbr, glbi = glr + tr, gli + ti
            qr, qi = _cmul(f_r, f_i, ir, ii)
            dlr, dli = _cmul(-qr, qi, gfr, gfi)
            zr, zi = _cmul(lbr, -lbi, glbr, glbi)
            dlr = dlr + dt * zr
            dli = dli + dt * zi
            dar_ref[d] = jnp.where(a_r < -1e-4, dlr, jnp.where(a_r == -1e-4, 0.5 * dlr, 0.0))
            dai_ref[d] = dli
            dldt_ref[d] = jnp.sum(lr * zr + li * zi, axis=-1, keepdims=True) * dt

    a_re, a_im, ldt, bt_re, bt_im = prm[:5]
    mshape = jax.ShapeDtypeStruct(bt_re.shape, F32)
    pshape = jax.ShapeDtypeStruct(a_re.shape, F32)
    return pl.pallas_call(
        body, name="ssm_param_grads", grid=(n_slab,),
        in_specs=[pg_spec, pole, pole, step, mat, mat],
        out_specs=[mat, mat, mat, mat, pole, pole, step, pl.BlockSpec((1, SLAB), lambda j: (0, j))],
        out_shape=[mshape, mshape, mshape, mshape, pshape, pshape, jax.ShapeDtypeStruct(ldt.shape, F32),
                   jax.ShapeDtypeStruct((1, n_slab * SLAB), F32)],
        compiler_params=_cparams(("arbitrary",)),
    )(pg, a_re, a_im, ldt, bt_re, bt_im)


def _peer(k, x, y, c):
    return (1 - x if k & 4 else x, 1 - y if k & 2 else y, 1 - c if k & 1 else c)


def _dev_index(pos):
    return 4 * pos[0] + 2 * pos[1] + pos[2]


def _chip_index(pos):
    return 2 * pos[0] + pos[1]


def _sibling_swap(x, name):
    any_spec = pl.BlockSpec(memory_space=pl.ANY)

    def body(x_ref, out_ref, send_sems, recv_sems):
        x_, y_, c_ = lax.axis_index("x"), lax.axis_index("y"), lax.axis_index("c")
        copies = [pltpu.make_async_remote_copy(
            src_ref=x_ref.at[2 * chip + (1 - c_)], dst_ref=out_ref.at[chip], send_sem=send_sems.at[chip],
            recv_sem=recv_sems.at[chip], device_id=(x_, y_, 1 - c_), device_id_type=pl.DeviceIdType.MESH)
            for chip in range(N_CHIPS)]
        for cp in copies:
            cp.start()
        for cp in copies:
            cp.wait()

    return pl.pallas_call(
        body, name=name, out_shape=jax.ShapeDtypeStruct((N_CHIPS,) + x.shape[1:], x.dtype),
        in_specs=[any_spec], out_specs=any_spec,
        scratch_shapes=[pltpu.SemaphoreType.DMA((N_CHIPS,)), pltpu.SemaphoreType.DMA((N_CHIPS,))],
    )(x)


def _pair_sum(x, got, name):
    n, R, W = got.shape
    tr = _row_tile(R, W, 5 << 20)
    core = lax.axis_index("c").astype(jnp.int32).reshape(1)

    def body(core_ref, a_ref, b_ref, o_ref):
        o_ref[...] = (a_ref[...].astype(F32) + b_ref[...].astype(F32)).astype(BF16)

    spec = pl.BlockSpec((1, tr, W), lambda i, j, c: (i, j, 0))
    grid_spec = pltpu.PrefetchScalarGridSpec(
        num_scalar_prefetch=1, grid=(n, R // tr),
        in_specs=[pl.BlockSpec((1, tr, W), lambda i, j, c: (2 * i + c[0], j, 0)), spec], out_specs=spec)
    return pl.pallas_call(
        body, name=name, grid_spec=grid_spec, out_shape=jax.ShapeDtypeStruct((n, R, W), BF16),
        compiler_params=_cparams(("arbitrary", "arbitrary")),
    )(core, x, got)


def _all_gather(xs, name):
    n = len(xs)
    any_spec = pl.BlockSpec(memory_space=pl.ANY)

    def body(*refs):
        x_refs, out_refs = refs[:n], refs[n:2 * n]
        send_sems, recv_sems, local_sems = refs[2 * n:]
        x, y, c = lax.axis_index("x"), lax.axis_index("y"), lax.axis_index("c")
        me, sibling = (x, y, c), (x, y, 1 - c)
        chips = [(1 - x, y), (x, 1 - y), (1 - x, 1 - y)]

        def copy(a, k, block, to, src=None):
            dst = out_refs[a].at[_dev_index(block)]
            return pltpu.make_async_remote_copy(
                src_ref=dst if src is None else src, dst_ref=dst,
                send_sem=send_sems.at[a, k], recv_sem=recv_sems.at[a, k],
                device_id=to, device_id_type=pl.DeviceIdType.MESH)

        mine = [pltpu.make_async_copy(x_refs[a], out_refs[a].at[_dev_index(me)], local_sems.at[a]) for a in range(n)]
        for cp in mine:
            cp.start()
        first = []
        for a in range(n):
            first.append(copy(a, 0, me, sibling, src=x_refs[a]))
            first += [copy(a, 1 + j, me, (*chip, c), src=x_refs[a]) for j, chip in enumerate(chips)]
        for cp in first:
            cp.start()
        passed = []
        for j, chip in enumerate(chips):
            for a in range(n):
                copy(a, 1 + j, (*chip, c), me).wait_recv()
                fwd = copy(a, 4 + j, (*chip, c), sibling)
                fwd.start()
                passed.append(fwd)
        for a in range(n):
            copy(a, 0, sibling, me).wait_recv()
            for j, chip in enumerate(chips):
                copy(a, 4 + j, (*chip, 1 - c), me).wait_recv()
        for cp in first + passed:
            cp.wait_send()
        for cp in mine:
            cp.wait()

    return pl.pallas_call(
        body, name=name,
        out_shape=[jax.ShapeDtypeStruct((N_DEV,) + v.shape, v.dtype) for v in xs],
        in_specs=[any_spec] * n, out_specs=[any_spec] * n,
        scratch_shapes=[pltpu.SemaphoreType.DMA((n, 7)), pltpu.SemaphoreType.DMA((n, 7)),
                        pltpu.SemaphoreType.DMA((n,))],
    )(*xs)


def _sum_blocks(x, name):
    _, R, W = x.shape

    def body(x_ref, o_ref):
        acc = x_ref[0].astype(F32)
        for d in range(1, N_DEV):
            acc = acc + x_ref[d].astype(F32)
        o_ref[...] = acc

    return pl.pallas_call(body, name=name, out_shape=jax.ShapeDtypeStruct((R, W), F32),
                          compiler_params=pltpu.CompilerParams(vmem_limit_bytes=VMEM_LIMIT))(x)


def _adam_update(w, g, m, v):
    mn = ADAM_B1 * m + (1.0 - ADAM_B1) * g
    vn = ADAM_B2 * v + (1.0 - ADAM_B2) * (g * g)
    m_hat = mn / (1.0 - ADAM_B1 ** ADAM_STEP)
    v_hat = vn / (1.0 - ADAM_B2 ** ADAM_STEP)
    return -ADAM_LR * (m_hat / (jnp.sqrt(v_hat) + ADAM_EPS) + ADAM_WD * w), mn, vn


def _row_tile(R, W, budget):
    padded = -(-W // LANES) * LANES * 4
    if R * padded <= budget:
        return R
    return _pick(R, [t for t in (2048, 1024, 512, 256, 128, 64, 32, 16, 8) if t * padded <= budget])


def _adamw(w, g, m, v, name):
    R, W = w.shape
    tr = _row_tile(R, W, 1 << 20)

    def body(w_ref, g_ref, m_ref, v_ref, d_ref, mo_ref, vo_ref):
        d_ref[...], mo_ref[...], vo_ref[...] = _adam_update(w_ref[...], g_ref[...], m_ref[...], v_ref[...])

    spec = pl.BlockSpec((tr, W), lambda i: (i, 0))
    shp = jax.ShapeDtypeStruct((R, W), F32)
    return pl.pallas_call(
        body, name=name, grid=(R // tr,), in_specs=[spec] * 4, out_specs=[spec] * 3, out_shape=[shp] * 3,
        compiler_params=_cparams(("arbitrary",)),
    )(w, g, m, v)


def _adamw_group(ws, gs, ms, vs, name):
    n = len(ws)

    def body(*refs):
        ins, outs = refs[:4 * n], refs[4 * n:]
        for i in range(n):
            w_ref, g_ref, m_ref, v_ref = ins[i], ins[n + i], ins[2 * n + i], ins[3 * n + i]
            outs[i][...], outs[n + i][...], outs[2 * n + i][...] = _adam_update(
                w_ref[...], g_ref[...], m_ref[...], v_ref[...])

    shapes = [jax.ShapeDtypeStruct(w.shape, F32) for w in ws]
    outs = pl.pallas_call(body, name=name, out_shape=shapes * 3)(*ws, *gs, *ms, *vs)
    return outs[:n], outs[n:2 * n], outs[2 * n:]


def _adamw_reduce(w, land, m, v, name):
    R, W = w.shape
    n = land.shape[0]
    tr = _row_tile(R, W, 1 << 20)

    def body(w_ref, l_ref, m_ref, v_ref, g_ref, d_ref, mo_ref, vo_ref):
        g = l_ref[0].astype(F32)
        for d in range(1, n):
            g = g + l_ref[d].astype(F32)
        g_ref[...] = g
        d_ref[...], mo_ref[...], vo_ref[...] = _adam_update(w_ref[...], g, m_ref[...], v_ref[...])

    spec = pl.BlockSpec((tr, W), lambda i: (i, 0))
    lspec = pl.BlockSpec((n, tr, W), lambda i: (0, i, 0))
    shp = jax.ShapeDtypeStruct((R, W), F32)
    return pl.pallas_call(
        body, name=name, grid=(R // tr,), in_specs=[spec, lspec, spec, spec], out_specs=[spec] * 4,
        out_shape=[shp] * 4, compiler_params=_cparams(("arbitrary",)),
    )(w, land, m, v)


def _gelu(v):
    c = math.sqrt(2.0 / math.pi)
    return 0.5 * v * (1.0 + jnp.tanh(c * (v + 0.044715 * v * v * v)))


def _gelu_grad(v):
    c = math.sqrt(2.0 / math.pi)
    t = jnp.tanh(c * (v + 0.044715 * v * v * v))
    return 0.5 * (1.0 + t) + 0.5 * v * (1.0 - t * t) * c * (1.0 + 3.0 * 0.044715 * v * v)


def kernel(x, p, norm_mix, w_in, q_norm, k_norm, ssm_a_re, ssm_a_im, ssm_log_dt, ssm_b_re, ssm_b_im, ssm_c_re, ssm_c_im, ssm_d, w_glu, b_glu, w_out, norm_ple, w_ple_gate, w_ple_proj, norm_final, loss_target, m_norm_mix, m_w_in, m_q_norm, m_k_norm, m_ssm_a_re, m_ssm_a_im, m_ssm_log_dt, m_ssm_b_re, m_ssm_b_im, m_ssm_c_re, m_ssm_c_im, m_ssm_d, m_w_glu, m_b_glu, m_w_out, m_norm_ple, m_w_ple_gate, m_w_ple_proj, m_norm_final, v_norm_mix, v_w_in, v_q_norm, v_k_norm, v_ssm_a_re, v_ssm_a_im, v_ssm_log_dt, v_ssm_b_re, v_ssm_b_im, v_ssm_c_re, v_ssm_c_im, v_ssm_d, v_w_glu, v_b_glu, v_w_out, v_norm_ple, v_w_ple_gate, v_w_ple_proj, v_norm_final):
    L, D = x.shape[1], x.shape[2]
    D_SSM = ssm_d.shape[1]
    G = D_SSM // SSM_H
    n_slab = D_SSM // SLAB
    n_in = w_in.shape[2]
    D_IN = n_in * N_DEV
    n_pp = w_ple_proj.shape[2]
    n_glu = w_glu.shape[2]
    xs = x[0]
    ps = p[0, 0]
    tgt = loss_target[0]

    (win_t3,) = _all_gather([w_in[0].T.astype(BF16)], "gather_w_in")
    win_t = win_t3.reshape(D_IN, D)
    later_weights = _Carry("gather", [w_glu[0].astype(BF16), w_out[0].astype(BF16), w_ple_gate[0].astype(BF16),
                                      w_ple_proj[0].astype(BF16)])

    ssm_prm = (ssm_a_re[0], ssm_a_im[0], ssm_log_dt[0].reshape(2, G, 1),
               ssm_b_re[0].transpose(0, 1, 3, 2), ssm_b_im[0].transpose(0, 1, 3, 2), ssm_c_re[0], ssm_c_im[0])

    cos, sin = _rope_tables(L)
    hn = _norm_in(xs, norm_mix, "norm_mix")
    ZT = 512
    zp_tile = lambda j: jnp.where(j < 2, j, jnp.where(j < D_IN // ZT - 1, j + 1, 2))
    z = _mm(hn, win_t, "nt", "in_proj", out_dtype=BF16, n_tiles=(ZT, zp_tile))
    qr, kr, vb, kt = _qkv_prep(z, cos, sin, q_norm, k_norm)
    o, lse, (wglu3, wout3, wpg3, wpp3) = _attn_fwd(qr, kr, vb, later_weights)
    wout = wout3.reshape(-1, D)
    wpg = wpg3.reshape(-1, D)
    u_off = 2 * D_ATTN
    u_perm = _seg_perm(z[:, u_off:u_off + D_SSM])
    ys_perm, ssm_ent = _ssm_fwd(u_perm, ssm_prm, ssm_d)
    ys = _seg_unperm(ys_perm)

    tm = _pick(L, (256,))

    def gelu_body(y_ref, o_ref):
        o_ref[...] = _gelu(y_ref[...]).astype(BF16)

    (gy,) = _rowcall(gelu_body, "gelu", L, tm, [(ys, _rspec(tm, D_SSM))], [(D_SSM, BF16)])
    wglu = wglu3.transpose(1, 0, 2).reshape(D_SSM, 2 * D_SSM)

    def mix_post(prod, o_ref, ga_ref, gs_ref, b_ref):
        glu_b = (prod + b_ref[...]).astype(BF16)
        gluv = glu_b.astype(F32)
        ga, gs = _f32(ga_ref), _f32(gs_ref)
        y_attn = o_ref[...] * ga * _sigmoid(ga)
        y_ssm = gluv[:, :D_SSM] * _sigmoid(gluv[:, D_SSM:]) * gs * _sigmoid(gs)
        return glu_b, jnp.concatenate([y_attn, y_ssm], axis=-1)

    glu, cat = _mm_rows(gy, wglu, "nn", "mix", mix_post, [o, (z, D_ATTN, 1), (z, D_SSM, 3)], [b_glu],
                        [(2 * D_SSM, BF16), (D_ATTN + D_SSM, BF16)])

    def out_post(prod, x_ref, g_ref):
        h1v = prod + x_ref[...]
        return h1v, h1v * _rms(h1v) * g_ref[...]

    h1, n2 = _mm_rows(cat, wout, "nn", "out_proj", out_post, [xs], [norm_ple], [(D, F32), (D, BF16)])
    pb = ps.astype(BF16)
    pp = _mm(pb, wpp3, "nn", "ple_proj", out_dtype=BF16, b_blk=True)

    nf = norm_final.reshape(1, D)

    def tail_post(gp, h1_ref, pp_ref, t_ref, g_ref):
        gate = _sigmoid(gp)
        ppv = _f32(pp_ref)
        h2 = h1_ref[...] + gate * ppv
        r = _rms(h2)
        hh = h2 * r
        err = hh * g_ref[...] - t_ref[...]
        loss_part = jnp.broadcast_to(0.5 * jnp.sum(jnp.mean(err * err, axis=-1, keepdims=True)), (1, LANES))
        dy = err * (1.0 / D)
        dh2 = _rms_bwd(dy, hh, r, g_ref[...])
        return dh2, dh2 * gate, dh2 * ppv * gate * (1.0 - gate), loss_part, _colsum(dy * hh)

    dh2, dpp, dsg, loss_acc, d_nf = _mm_rows(n2, wpg, "nn", "tail", tail_post, [h1, pp, tgt], [nf],
                                             [(D, F32), (D, BF16), (D, BF16)], [(1, LANES), (1, D)], vmem=58 << 20)

    g_wpp3 = _mm(pb, dpp, "tn", "d_ple_proj", out_dtype=BF16, out_blk=n_pp)
    g_wpg = _mm(n2, dsg, "tn", "d_ple_gate", out_dtype=BF16)
    def ple_bwd_post(dn, h1_ref, dh2_ref, g_ref):
        h1v = h1_ref[...]
        r = _rms(h1v)
        hh = h1v * r
        dh1 = dh2_ref[...] + _rms_bwd(dn, hh, r, g_ref[...])
        return dh1, dh1, _colsum(dn * hh)

    dh1, dh1b, d_nple = _mm_rows(dsg, wpg, "nt", "ple_bwd", ple_bwd_post, [h1, dh2], [norm_ple],
                                 [(D, F32), (D, BF16)], [(1, D)])

    g_wout = _mm(cat, dh1b, "tn", "d_out_proj", out_dtype=BF16)

    def mix_bwd_post(dcat, o_ref, ga_ref, glu_ref, gs_ref):
        dca, dcs = dcat[:, :D_ATTN], dcat[:, D_ATTN:]
        ga, gs = _f32(ga_ref), _f32(gs_ref)
        gla, glb = glu_ref[:, :D_SSM].astype(F32), glu_ref[:, D_SSM:].astype(F32)
        sa, ss, sb = _sigmoid(ga), _sigmoid(gs), _sigmoid(glb)
        do = dca * ga * sa
        dga = dca * o_ref[...] * sa * (1.0 + ga * (1.0 - sa))
        dgs = dcs * gla * sb * ss * (1.0 + gs * (1.0 - ss))
        dy2 = dcs * gs * ss
        da, db = dy2 * sb, dy2 * gla * sb * (1.0 - sb)
        return (do, dga, dgs, jnp.concatenate([da, db], axis=-1),
                jnp.concatenate([_colsum(da), _colsum(db)], axis=-1))

    do, dga, dgs, dglu, g_bglu = _mm_rows(
        dh1b, wout, "nt", "mix_bwd", mix_bwd_post, [o, (z, D_ATTN, 1), glu, (z, D_SSM, 3)], [],
        [(D_ATTN, BF16), (D_ATTN, BF16), (D_SSM, BF16), (2 * D_SSM, BF16)], [(1, 2 * D_SSM)])

    g_wglu3 = _mm(gy, dglu, "tn", "d_glu_proj", out_dtype=BF16, out_blk=n_glu)
    dys = _mm(dglu, wglu3, "nt", "d_ssm_out", b_blk=True,
              post=(lambda out, y: out * _gelu_grad(y), ys))
    du_perm, pg = _ssm_bwd(u_perm, _seg_perm(dys), ssm_ent, ssm_prm, ssm_d)
    du = _seg_unperm(du_perm)

    pg_send = pg.reshape(N_DEV, (n_slab // N_DEV) * 2 * PG_ROWS, SLAB_S)
    dqs, dkr, dvv, (l_wglu, l_wout, l_wpg, l_wpp, l_pg) = _attn_bwd(
        qr, kr, vb, kt, do, o, lse,
        _Carry("a2a", [g_wglu3, g_wout.reshape(N_DEV, -1, D), g_wpg.reshape(N_DEV, -1, D), g_wpp3, pg_send]))

    scale = HEAD_DIM ** -0.5
    kblk = 4 * D_ATTN // D_KV

    a0, k0, v0, u0, s0 = D_ATTN + 2 * D_KV, D_ATTN, D_ATTN + D_KV, 2 * D_ATTN + 2 * D_KV, 2 * D_ATTN + 2 * D_KV + D_SSM

    def qkv_bwd_body(dq_ref, dk_ref, dv_ref, q_ref, k_ref, cos_ref, sin_ref, qn_ref, kn_ref, dga_ref, du_ref, dgs_ref,
                     dz_ref, dqn_ref, dkn_ref):
        c, s = cos_ref[...], sin_ref[...]
        dz_ref[:, a0:a0 + D_ATTN] = dga_ref[...]
        dz_ref[:, u0:u0 + D_SSM] = du_ref[...].astype(BF16)
        dz_ref[:, s0:s0 + D_SSM] = dgs_ref[...]

        def head(g, xh, w):
            dn = g * c + _partner(g * s)
            r = _rms(xh)
            xhat = xh * r
            return _rms_bwd(dn, xhat, r, w), _colsum(dn * xhat)

        dqn = jnp.zeros((1, HEAD_DIM), F32)
        for h in range(N_HEADS):
            sl = slice(h * HEAD_DIM, (h + 1) * HEAD_DIM)
            dx, dw = head(dq_ref[:, sl] * scale, q_ref[:, sl].astype(F32), qn_ref[...])
            dz_ref[:, sl] = dx.astype(BF16)
            dqn = dqn + dw
        dkn = jnp.zeros((1, HEAD_DIM), F32)
        for h in range(N_KV):
            sl = slice(h * HEAD_DIM, (h + 1) * HEAD_DIM)
            dx, dw = head(dk_ref[:, sl], k_ref[:, sl].astype(F32), kn_ref[...])
            dz_ref[:, k0 + h * HEAD_DIM:k0 + (h + 1) * HEAD_DIM] = dx.astype(BF16)
            dkn = dkn + dw
        dz_ref[:, v0:v0 + D_KV] = dv_ref[...].astype(BF16)
        _acc(dqn_ref, dqn)
        _acc(dkn_ref, dkn)

    dz, g_qn, g_kn = _rowcall(
        qkv_bwd_body, "qkv_bwd", L, tm,
        [(dqs, _rspec(tm, D_ATTN)), (dkr, _rspec(tm, D_KV)), (dvv, _rspec(tm, D_KV)),
         (z, _rspec(tm, D_ATTN, 0)), (z, _rspec(tm, D_KV, kblk)), (cos, _rspec(tm, HEAD_DIM)),
         (sin, _rspec(tm, HEAD_DIM)), (q_norm, _fspec(q_norm.shape)), (k_norm, _fspec(k_norm.shape)),
         (dga, _rspec(tm, D_ATTN)), (du, _rspec(tm, D_SSM)), (dgs, _rspec(tm, D_SSM))],
        [(D_IN, BF16)], [(1, HEAD_DIM), (1, HEAD_DIM)])

    g_win_t = _mm(dz, hn, "tn", "d_in_proj", out_dtype=BF16)
    g_win8 = g_win_t.reshape(N_DEV, n_in, D)
    from_sibling = _sibling_swap(g_win8, "swap_d_w_in")
    pair = _pair_sum(g_win8, from_sibling, "pair_sum_d_w_in")
    dhn, (l_win_t,) = _mm(dz, win_t, "nn", "d_norm_mix_in", out_dtype=BF16,
                          carry=_Carry("a2a_chips", [pair]))

    def in_bwd_body(x_ref, dn_ref, dh1_ref, g_ref, dx_ref, dg_ref):
        xv = x_ref[...]
        r = _rms(xv)
        hh = xv * r
        dn = _f32(dn_ref)
        _acc(dg_ref, _colsum(dn * hh))
        dx_ref[...] = dh1_ref[...] + _rms_bwd(dn, hh, r, g_ref[...])

    grad_x, g_nmix = _rowcall(
        in_bwd_body, "in_bwd", L, tm,
        [(xs, _rspec(tm, D)), (dhn, _rspec(tm, D)), (dh1, _rspec(tm, D)), (norm_mix, _fspec(norm_mix.shape))],
        [(D, F32)], [(1, D)])

    tiny_parts = [g_nmix, g_bglu, d_nple, d_nf, g_qn, g_kn, loss_acc[:, :1]]
    tiny_flat = jnp.concatenate([t.reshape(-1) for t in tiny_parts])
    tiny_rows = -(-tiny_flat.shape[0] // (8 * LANES)) * 8
    tiny = jnp.pad(tiny_flat, (0, tiny_rows * LANES - tiny_flat.shape[0])).reshape(tiny_rows, LANES)
    pg_sum = _sum_blocks(l_pg, "sum_ssm_grads")
    pg_all, tiny_all = _all_gather([pg_sum, tiny], "gather_small_grads")
    (g_bt_re, g_bt_im, g_c_re, g_c_im, g_a_re, g_a_im, g_ldt, g_skip) = _ssm_param_grads(
        pg_all.reshape(n_slab, 2, PG_ROWS, SLAB_S), ssm_prm)
    tiny_sum = _sum_blocks(tiny_all, "sum_tiny_grads").reshape(-1)
    tiny_grads, off = [], 0
    for t in tiny_parts:
        tiny_grads.append(tiny_sum[off:off + t.size].reshape(t.shape))
        off += t.size
    r_nmix, r_bglu, r_nple, r_nf, r_qn, r_kn, loss = tiny_grads
    loss = loss.reshape(())

    grads, deltas, new_ms, new_vs = {}, {}, {}, {}
    outs = _adamw_reduce(w_in[0].T, l_win_t, m_w_in[0].T, v_w_in[0].T, "adamw_w_in")
    grads["w_in"], deltas["w_in"], new_ms["w_in"], new_vs["w_in"] = [t.T[None] for t in outs]
    big = [("w_glu", w_glu, l_wglu, m_w_glu, v_w_glu),
           ("w_out", w_out, l_wout, m_w_out, v_w_out), ("w_ple_gate", w_ple_gate, l_wpg, m_w_ple_gate, v_w_ple_gate),
           ("w_ple_proj", w_ple_proj, l_wpp, m_w_ple_proj, v_w_ple_proj)]
    for name, w, ld, m, v in big:
        shp = w.shape
        outs = _adamw_reduce(w[0], ld, m[0], v[0], "adamw_" + name)
        grads[name], deltas[name], new_ms[name], new_vs[name] = [t.reshape(shp) for t in outs]
    bt2 = (2 * G * SSM_H, SSM_P)
    for name, w, g, m, v in (("ssm_b_re", ssm_b_re, g_bt_re, m_ssm_b_re, v_ssm_b_re),
                             ("ssm_b_im", ssm_b_im, g_bt_im, m_ssm_b_im, v_ssm_b_im)):
        to2 = lambda t: t[0].transpose(0, 1, 3, 2).reshape(bt2)
        back = lambda t: t.reshape(2, G, SSM_H, SSM_P).transpose(0, 1, 3, 2)[None]
        outs = _adamw(to2(w), g.reshape(bt2), to2(m), to2(v), "adamw_" + name)
        grads[name] = back(g)
        deltas[name], new_ms[name], new_vs[name] = [back(t) for t in outs]
    small = [("norm_mix", norm_mix, r_nmix, m_norm_mix, v_norm_mix, (1, D)),
             ("q_norm", q_norm, r_qn, m_q_norm, v_q_norm, (1, HEAD_DIM)),
             ("k_norm", k_norm, r_kn, m_k_norm, v_k_norm, (1, HEAD_DIM)),
             ("ssm_a_re", ssm_a_re, g_a_re, m_ssm_a_re, v_ssm_a_re, (2 * G, SSM_P)),
             ("ssm_a_im", ssm_a_im, g_a_im, m_ssm_a_im, v_ssm_a_im, (2 * G, SSM_P)),
             ("ssm_log_dt", ssm_log_dt, g_ldt, m_ssm_log_dt, v_ssm_log_dt, (2, G)),
             ("ssm_c_re", ssm_c_re, g_c_re, m_ssm_c_re, v_ssm_c_re, (2 * G * SSM_H, SSM_P)),
             ("ssm_c_im", ssm_c_im, g_c_im, m_ssm_c_im, v_ssm_c_im, (2 * G * SSM_H, SSM_P)),
             ("ssm_d", ssm_d, g_skip, m_ssm_d, v_ssm_d, (1, D_SSM)),
             ("b_glu", b_glu, r_bglu, m_b_glu, v_b_glu, (1, 2 * D_SSM)),
             ("norm_ple", norm_ple, r_nple, m_norm_ple, v_norm_ple, (1, D)),
             ("norm_final", norm_final, r_nf, m_norm_final, v_norm_final, (1, D))]
    group = [it for it in small if it[5][0] * it[5][1] <= (1 << 14)]
    grouped = {it[0] for it in group}
    for name, w, g, m, v, s2 in small:
        if name in grouped:
            continue
        shp = w.shape
        outs = _adamw(w.reshape(s2), g.reshape(s2), m.reshape(s2), v.reshape(s2), "adamw_" + name)
        grads[name] = g.reshape(shp)
        deltas[name], new_ms[name], new_vs[name] = [t.reshape(shp) for t in outs]
    ds, mns, vns = _adamw_group(*[[it[i].reshape(it[5]) for it in group] for i in (1, 2, 3, 4)], "adamw_tiny")
    for (name, w, g, _, _, _), d_, m_, v_ in zip(group, ds, mns, vns):
        shp = w.shape
        grads[name] = g.reshape(shp)
        deltas[name], new_ms[name], new_vs[name] = d_.reshape(shp), m_.reshape(shp), v_.reshape(shp)

    order = ["norm_mix", "w_in", "q_norm", "k_norm", "ssm_a_re", "ssm_a_im", "ssm_log_dt", "ssm_b_re", "ssm_b_im",
             "ssm_c_re", "ssm_c_im", "ssm_d", "w_glu", "b_glu", "w_out", "norm_ple", "w_ple_gate", "w_ple_proj",
             "norm_final"]
    return (loss, grad_x[None], *[grads[k] for k in order], *[deltas[k] for k in order],
            *[new_ms[k] for k in order], *[new_vs[k] for k in order])
```

```python
import functools
import math

import numpy as np
import jax
import jax.numpy as jnp
from jax import lax
from jax.experimental import pallas as pl
from jax.experimental.pallas import tpu as pltpu

F32 = jnp.float32
BF16 = jnp.bfloat16

N_DEV = 8
N_CHIPS = 4
EPS = 1e-6
GRID_W = 64
ROPE_THETA = 10000.0
HEAD_DIM = 128
N_HEADS = 8
N_KV = 2
REP = N_HEADS // N_KV
D_ATTN = N_HEADS * HEAD_DIM
D_KV = N_KV * HEAD_DIM
SSM_H = 16
SSM_P = 64
SLAB = 128
SLAB_G = SLAB // SSM_H
SLAB_S = SLAB_G * SSM_P
SEG = 8
CHAINS = 2
SCAN_UNROLL = 8
LANES = 128
PG_ROWS = 72
VMEM_LIMIT = 48 << 20

ADAM_LR = 0.001
ADAM_B1 = 0.9
ADAM_B2 = 0.999
ADAM_EPS = 1e-08
ADAM_WD = 0.01
ADAM_STEP = 10


def _pick(n, cands):
    for c in cands:
        if n % c == 0:
            return c
    return n


def _cparams(sem, vmem=VMEM_LIMIT):
    return pltpu.CompilerParams(dimension_semantics=sem, vmem_limit_bytes=vmem)


class _Carry:
    def __init__(self, kind, xs):
        self.kind, self.xs, self.n = kind, list(xs), len(xs)
        self.ks = (2, 4, 6) if kind == "a2a_chips" else tuple(range(1, N_DEV))
        self.index = _chip_index if kind == "a2a_chips" else _dev_index
        lead = (N_DEV,) if kind == "gather" else ()
        self.out_shape = [jax.ShapeDtypeStruct(lead + v.shape, v.dtype) for v in xs]
        self.specs = [pl.BlockSpec(memory_space=pl.ANY)] * self.n
        self.scratch = [pltpu.SemaphoreType.DMA((self.n, len(self.ks))), pltpu.SemaphoreType.DMA((self.n, len(self.ks))),
                        pltpu.SemaphoreType.DMA((self.n,))]

    def _copies(self, x_refs, out_refs, sems):
        send_sems, recv_sems, local_sems = sems
        x, y, c = lax.axis_index("x"), lax.axis_index("y"), lax.axis_index("c")
        me = self.index((x, y, c))
        mine, sends, arrivals = [], [], []
        for a in range(self.n):
            src_mine = x_refs[a] if self.kind == "gather" else x_refs[a].at[me]
            mine.append(pltpu.make_async_copy(src_mine, out_refs[a].at[me], local_sems.at[a]))
            for s, k in enumerate(self.ks):
                peer = _peer(k, x, y, c)
                src = x_refs[a] if self.kind == "gather" else x_refs[a].at[self.index(peer)]
                sends.append(pltpu.make_async_remote_copy(
                    src_ref=src, dst_ref=out_refs[a].at[me], send_sem=send_sems.at[a, s],
                    recv_sem=recv_sems.at[a, s], device_id=peer, device_id_type=pl.DeviceIdType.MESH))
                land = out_refs[a].at[self.index(peer)]
                arrivals.append(pltpu.make_async_remote_copy(
                    src_ref=land, dst_ref=land, send_sem=send_sems.at[a, s],
                    recv_sem=recv_sems.at[a, s], device_id=peer, device_id_type=pl.DeviceIdType.MESH))
        return mine, sends, arrivals

    def start(self, x_refs, out_refs, sems):
        mine, sends, _ = self._copies(x_refs, out_refs, sems)
        for cp in mine + sends:
            cp.start()

    def wait(self, x_refs, out_refs, sems):
        mine, sends, arrivals = self._copies(x_refs, out_refs, sems)
        for cp in arrivals:
            cp.wait_recv()
        for cp in sends:
            cp.wait_send()
        for cp in mine:
            cp.wait()


def _grid_edges(grid):
    first = functools.reduce(lambda p, q: p & q, [pl.program_id(d) == 0 for d in range(len(grid))])
    last = functools.reduce(lambda p, q: p & q, [pl.program_id(d) == g - 1 for d, g in enumerate(grid)])
    return first, last


def _mm(a, b, mode, name, out_dtype=F32, add=None, bias=None, a_blk=False, b_blk=False, out_blk=0, carry=None,
        n_tiles=None, post=None):
    w = b.shape[2] if b_blk else out_blk
    if mode == "nn":
        M, K = a.shape
        N = b.shape[0] * w if b_blk else b.shape[1]
    elif mode == "nt":
        M = a.shape[1] if a_blk else a.shape[0]
        N = b.shape[1] if b_blk else b.shape[0]
        K = b.shape[0] * w if b_blk else b.shape[1]
    else:
        K, M = a.shape
        N = b.shape[0] * w if b_blk else b.shape[1]
    tm = _pick(M, (1024, 768, 512, 256))
    tn = _pick(N, (1024, 768, 512, 256))
    if mode == "tn" and N <= 2048:
        tn = N
    tk = K if (mode != "tn" and K <= 2048) else _pick(K, (1024, 768, 512, 256))
    if mode == "nn" and K > 2048 and N <= 2048:
        tn, tk = N, _pick(K, (1536, 1024, 768, 512, 256))
    perm = lambda j: j
    if n_tiles:
        tn, perm = n_tiles
        tm = _pick(M, (2048, 1024, 512, 256))
    if mode == "nt" and b_blk:
        tk = w
    elif b_blk or out_blk:
        tn = w
    nk = K // tk
    grid = (M // tm, N // tn, nk)
    if mode == "nn":
        a_spec = pl.BlockSpec((tm, tk), lambda i, j, k: (i, k))
        b_spec = (pl.BlockSpec((1, tk, tn), lambda i, j, k: (j, k, 0)) if b_blk
                  else pl.BlockSpec((tk, tn), lambda i, j, k: (k, j)))
        dims = (((1,), (0,)), ((), ()))
    elif mode == "nt":
        a_spec = (pl.BlockSpec((1, tm, tk), lambda i, j, k: (k, i, 0)) if a_blk
                  else pl.BlockSpec((tm, tk), lambda i, j, k: (i, k)))
        b_spec = (pl.BlockSpec((1, tn, tk), lambda i, j, k: (k, j, 0)) if b_blk
                  else pl.BlockSpec((tn, tk), lambda i, j, k: (perm(j), k)))
        dims = (((1,), (1,)), ((), ()))
    else:
        a_spec = pl.BlockSpec((tk, tm), lambda i, j, k: (k, i))
        b_spec = (pl.BlockSpec((1, tk, tn), lambda i, j, k: (j, k, 0)) if b_blk
                  else pl.BlockSpec((tk, tn), lambda i, j, k: (k, j)))
        dims = (((0,), (0,)), ((), ()))
    if out_blk:
        out_spec = pl.BlockSpec((1, tm, tn), lambda i, j, k: (j, i, 0))
        out_shape = jax.ShapeDtypeStruct((N // tn, M, tn), out_dtype)
    else:
        out_spec = pl.BlockSpec((tm, tn), lambda i, j, k: (i, j))
        out_shape = jax.ShapeDtypeStruct((M, N), out_dtype)
    extras, extra_specs, combine = [], [], []
    if add is not None:
        extras.append(add)
        extra_specs.append(pl.BlockSpec((tm, tn), lambda i, j, k: (i, j)))
        combine.append(lambda out, t: out + t)
    if bias is not None:
        extras.append(bias)
        extra_specs.append(pl.BlockSpec((1, tn), lambda i, j, k: (0, j)))
        combine.append(lambda out, t: out + t)
    if post is not None:
        extras.append(post[1])
        extra_specs.append(pl.BlockSpec((tm, tn), lambda i, j, k: (i, j)))
        combine.append(post[0])

    n_ex = len(extras)
    nc = carry.n if carry else 0

    def body(a_ref, b_ref, *rest):
        ex_refs, cx = rest[:n_ex], rest[n_ex:n_ex + nc]
        o_ref, cout = rest[n_ex + nc], rest[n_ex + nc + 1:n_ex + 2 * nc + 1]
        tail = rest[n_ex + 2 * nc + 1:]
        sems = tail[:3] if carry else ()
        first, last = _grid_edges(grid)
        if carry:
            @pl.when(first)
            def _():
                carry.start(cx, cout, sems)

        def product():
            av = a_ref[0] if a_blk else a_ref[...]
            bv = b_ref[0] if b_blk else b_ref[...]
            return lax.dot_general(av, bv, dims, preferred_element_type=F32)

        def finish(out):
            for r, fn in zip(ex_refs, combine):
                out = fn(out, r[...])
            if out_blk:
                o_ref[0] = out.astype(out_dtype)
            else:
                o_ref[...] = out.astype(out_dtype)

        if nk == 1:
            finish(product())
        else:
            acc_ref = tail[-1]
            k = pl.program_id(2)

            @pl.when(k == 0)
            def _():
                acc_ref[...] = jnp.zeros_like(acc_ref)

            acc_ref[...] += product()

            @pl.when(k == nk - 1)
            def _():
                finish(acc_ref[...])

        if carry:
            @pl.when(last)
            def _():
                carry.wait(cx, cout, sems)

    scratch = (carry.scratch if carry else []) + ([pltpu.VMEM((tm, tn), F32)] if nk > 1 else [])
    outs = pl.pallas_call(
        body, name=name, grid=grid,
        in_specs=[a_spec, b_spec] + extra_specs + (carry.specs if carry else []),
        out_specs=[out_spec] + (carry.specs if carry else []),
        out_shape=[out_shape] + (carry.out_shape if carry else []),
        scratch_shapes=scratch,
        compiler_params=_cparams(("arbitrary", "arbitrary", "arbitrary")),
    )(a, b, *extras, *(carry.xs if carry else []))
    return (outs[0], outs[1:]) if carry else outs[0]


def _rspec(tm, w, cb=0):
    return pl.BlockSpec((tm, w), lambda i: (i, cb))


def _fspec(shape):
    nd = len(shape)
    return pl.BlockSpec(shape, lambda i: (0,) * nd)


def _rowcall(body, name, L, tm, ins, row_outs, acc_outs=()):
    out_shape = [jax.ShapeDtypeStruct((L, w), dt) for w, dt in row_outs]
    out_shape += [jax.ShapeDtypeStruct(s, F32) for s in acc_outs]
    out_specs = [_rspec(tm, w) for w, _ in row_outs] + [_fspec(s) for s in acc_outs]
    return pl.pallas_call(
        body, name=name, grid=(L // tm,),
        in_specs=[s for _, s in ins], out_specs=out_specs, out_shape=out_shape,
        compiler_params=_cparams(("arbitrary",)),
    )(*[a for a, _ in ins])


def _mm_rows(a, b, mode, name, post, row_ins, full_ins, row_outs, acc_outs=(), vmem=VMEM_LIMIT, pre=None):
    M, K = a.shape
    N = b.shape[1] if mode == "nn" else b.shape[0]
    tm = _pick(M, (256,))
    dims = (((1,), (0,)), ((), ())) if mode == "nn" else (((1,), (1,)), ((), ()))
    n_in, n_row = len(row_ins) + len(full_ins), len(row_outs)

    def body(a_ref, b_ref, *rest):
        ins, outs = rest[:n_in], rest[n_in:]
        av = pre(a_ref[...]) if pre else a_ref[...]
        prod = lax.dot_general(av, b_ref[...], dims, preferred_element_type=F32)
        res = post(prod, av, *ins) if pre else post(prod, *ins)
        for o_ref, val in zip(outs[:n_row], res[:n_row]):
            o_ref[...] = val.astype(o_ref.dtype)
        for acc_ref, val in zip(outs[n_row:], res[n_row:]):
            _acc(acc_ref, val)

    out_shape = [jax.ShapeDtypeStruct((M, w), dt) for w, dt in row_outs]
    out_shape += [jax.ShapeDtypeStruct(s, F32) for s in acc_outs]
    row_ins = [r if isinstance(r, tuple) else (r, r.shape[1], 0) for r in row_ins]
    return pl.pallas_call(
        body, name=name, grid=(M // tm,),
        in_specs=[_rspec(tm, K), _fspec(b.shape)] + [_rspec(tm, w, cb) for _, w, cb in row_ins]
        + [_fspec(f.shape) for f in full_ins],
        out_specs=[_rspec(tm, w) for w, _ in row_outs] + [_fspec(s) for s in acc_outs],
        out_shape=out_shape, compiler_params=_cparams(("arbitrary",), vmem),
    )(a, b, *[r for r, _, _ in row_ins], *full_ins)


def _acc(ref, val):
    @pl.when(pl.program_id(0) == 0)
    def _():
        ref[...] = jnp.zeros_like(ref)
    ref[...] += val


def _colsum(v):
    return jnp.sum(v, axis=0, keepdims=True)


def _rms(xv):
    return lax.rsqrt(jnp.mean(xv * xv, axis=-1, keepdims=True) + EPS)


def _rms_bwd(dn, xhat, r, g):
    dng = dn * g
    return r * (dng - xhat * jnp.mean(dng * xhat, axis=-1, keepdims=True))


def _sigmoid(v):
    return jax.nn.sigmoid(v)


def _f32(ref):
    return ref[...].astype(F32)


def _partner(v):
    w = v.shape[-1]
    lane = lax.broadcasted_iota(jnp.int32, v.shape, v.ndim - 1)
    first_half = (lane % 64) < 32
    return jnp.where(first_half, pltpu.roll(v, w - 32, axis=v.ndim - 1), pltpu.roll(v, 32, axis=v.ndim - 1))


def _norm_in(x, g, name):
    L, D = x.shape
    tm = _pick(L, (512, 256))

    def body(x_ref, g_ref, o_ref):
        xv = x_ref[...]
        o_ref[...] = (xv * _rms(xv) * g_ref[...]).astype(BF16)

    return _rowcall(body, name, L, tm, [(x, _rspec(tm, D)), (g, _fspec(g.shape))], [(D, BF16)])[0]


def _rope_tables(L):
    t = np.arange(L)
    rows = (t // GRID_W).astype(np.float32)
    cols = (t % GRID_W).astype(np.float32)
    n_freq = HEAD_DIM // 4
    inv_freq = np.float32(ROPE_THETA) ** (-np.arange(n_freq, dtype=np.float32) / np.float32(n_freq))
    ar = (rows[:, None] * inv_freq[None, :]).astype(np.float32).astype(np.float64)
    ac = (cols[:, None] * inv_freq[None, :]).astype(np.float32).astype(np.float64)
    cos = np.concatenate([np.cos(ar), np.cos(ar), np.cos(ac), np.cos(ac)], axis=-1).astype(np.float32)
    sin = np.concatenate([-np.sin(ar), np.sin(ar), -np.sin(ac), np.sin(ac)], axis=-1).astype(np.float32)
    return jnp.asarray(cos), jnp.asarray(sin)


def _qkv_prep(z, cos, sin, qn, kn):
    L = z.shape[0]
    tm = _pick(L, (512, 256))
    scale = HEAD_DIM ** -0.5
    kblk = 4 * D_ATTN // D_KV

    def body(q_ref, k_ref, v_ref, cos_ref, sin_ref, qn_ref, kn_ref, qo_ref, ko_ref, vo_ref, kt_ref):
        c, s = cos_ref[...], sin_ref[...]

        def head(xh, w):
            n = xh * _rms(xh) * w
            return n * c + _partner(n) * s

        for h in range(N_HEADS):
            sl = slice(h * HEAD_DIM, (h + 1) * HEAD_DIM)
            qo_ref[:, sl] = (head(q_ref[:, sl].astype(F32), qn_ref[...]) * scale).astype(BF16)
        for h in range(N_KV):
            sl = slice(h * HEAD_DIM, (h + 1) * HEAD_DIM)
            kr = head(k_ref[:, sl].astype(F32), kn_ref[...])
            ko_ref[:, sl] = kr.astype(BF16)
            kt_ref[sl, :] = kr.T.astype(BF16)
        vo_ref[...] = v_ref[...].astype(BF16)

    return pl.pallas_call(
        body, name="qkv_prep", grid=(L // tm,),
        in_specs=[_rspec(tm, D_ATTN, 0), _rspec(tm, D_KV, kblk), _rspec(tm, D_KV, kblk + 1),
                  _rspec(tm, HEAD_DIM), _rspec(tm, HEAD_DIM), _fspec(qn.shape), _fspec(kn.shape)],
        out_specs=[_rspec(tm, D_ATTN), _rspec(tm, D_KV), _rspec(tm, D_KV),
                   pl.BlockSpec((D_KV, tm), lambda i: (0, i))],
        out_shape=[jax.ShapeDtypeStruct((L, D_ATTN), BF16), jax.ShapeDtypeStruct((L, D_KV), BF16),
                   jax.ShapeDtypeStruct((L, D_KV), BF16), jax.ShapeDtypeStruct((D_KV, L), BF16)],
        compiler_params=_cparams(("arbitrary",)),
    )(z, z, z, cos, sin, qn, kn)


def _col_to_row(col):
    n = col.shape[0]
    eye = lax.broadcasted_iota(jnp.int32, (n, n), 0) == lax.broadcasted_iota(jnp.int32, (n, n), 1)
    return jnp.sum(jnp.where(eye, col, 0.0), axis=0, keepdims=True)


def _attn_fwd(q, k, v, carry=None):
    L = q.shape[0]
    tq = _pick(L, (256, 128))
    grid = (N_HEADS, L // tq)
    nc = carry.n if carry else 0

    def body(q_ref, k_ref, v_ref, *rest):
        cx, (o_ref, lse_ref) = rest[:nc], rest[nc:nc + 2]
        cout, sems = rest[nc + 2:2 * nc + 2], rest[2 * nc + 2:]
        first, last = _grid_edges(grid)
        if carry:
            @pl.when(first)
            def _():
                carry.start(cx, cout, sems)

        s = lax.dot_general(q_ref[...], k_ref[...], (((1,), (1,)), ((), ())), preferred_element_type=F32)
        m = jnp.max(s, axis=-1, keepdims=True)
        e = jnp.exp(s - m)
        l = jnp.sum(e, axis=-1, keepdims=True)
        o_ref[...] = jnp.dot(e.astype(BF16), v_ref[...], preferred_element_type=F32) / l
        lse_ref[0] = _col_to_row(m + jnp.log(l))

        if carry:
            @pl.when(last)
            def _():
                carry.wait(cx, cout, sems)

    outs = pl.pallas_call(
        body, name="attn_fwd", grid=grid,
        in_specs=[pl.BlockSpec((tq, HEAD_DIM), lambda h, i: (i, h)),
                  pl.BlockSpec((L, HEAD_DIM), lambda h, i: (0, h // REP)),
                  pl.BlockSpec((L, HEAD_DIM), lambda h, i: (0, h // REP))] + (carry.specs if carry else []),
        out_specs=[pl.BlockSpec((tq, HEAD_DIM), lambda h, i: (i, h)),
                   pl.BlockSpec((1, 1, tq), lambda h, i: (h, 0, i))] + (carry.specs if carry else []),
        out_shape=[jax.ShapeDtypeStruct((L, D_ATTN), F32), jax.ShapeDtypeStruct((N_HEADS, 1, L), F32)]
        + (carry.out_shape if carry else []),
        scratch_shapes=carry.scratch if carry else [],
        compiler_params=_cparams(("arbitrary", "arbitrary")),
    )(q, k, v, *(carry.xs if carry else []))
    return outs[0], outs[1], outs[2:]


def _attn_bwd(q, k, v, kt, do, o, lse, carry=None):
    L = q.shape[0]
    tq = _pick(L, (256, 128))
    kc = _pick(L, (512, 256, 128))
    nt = (((1,), (1,)), ((), ()))
    grid = (N_KV, REP, L // tq)
    nc = carry.n if carry else 0

    def body(q_ref, do_ref, o_ref, lse_ref, k_ref, v_ref, kt_ref, *rest):
        cx, (dq_ref, dk_ref, dv_ref) = rest[:nc], rest[nc:nc + 3]
        cout, sems = rest[nc + 3:2 * nc + 3], rest[2 * nc + 3:]
        first, last = _grid_edges(grid)
        if carry:
            @pl.when(first)
            def _():
                carry.start(cx, cout, sems)

        @pl.when((pl.program_id(1) == 0) & (pl.program_id(2) == 0))
        def _():
            dk_ref[...] = jnp.zeros_like(dk_ref)
            dv_ref[...] = jnp.zeros_like(dv_ref)

        qv, dov = q_ref[...], do_ref[...]
        lse_row = lse_ref[0]
        delta = _col_to_row(jnp.sum(dov.astype(F32) * o_ref[...], axis=-1, keepdims=True))
        dqt = jnp.zeros((HEAD_DIM, tq), F32)
        for c in range(L // kc):
            sl = slice(c * kc, (c + 1) * kc)
            st = lax.dot_general(k_ref[sl, :], qv, nt, preferred_element_type=F32)
            pt = jnp.exp(st - lse_row)
            dpt = lax.dot_general(v_ref[sl, :], dov, nt, preferred_element_type=F32)
            dst = (pt * (dpt - delta)).astype(BF16)
            dv_ref[sl, :] += jnp.dot(pt.astype(BF16), dov, preferred_element_type=F32)
            dk_ref[sl, :] += jnp.dot(dst, qv, preferred_element_type=F32)
            dqt = dqt + jnp.dot(kt_ref[:, sl], dst, preferred_element_type=F32)
        dq_ref[...] = dqt.T

        if carry:
            @pl.when(last)
            def _():
                carry.wait(cx, cout, sems)

    head = lambda g, r, i: (i, g * REP + r)
    outs = pl.pallas_call(
        body, name="attn_bwd", grid=grid,
        in_specs=[pl.BlockSpec((tq, HEAD_DIM), head), pl.BlockSpec((tq, HEAD_DIM), head),
                  pl.BlockSpec((tq, HEAD_DIM), head),
                  pl.BlockSpec((1, 1, tq), lambda g, r, i: (g * REP + r, 0, i)),
                  pl.BlockSpec((L, HEAD_DIM), lambda g, r, i: (0, g)),
                  pl.BlockSpec((L, HEAD_DIM), lambda g, r, i: (0, g)),
                  pl.BlockSpec((HEAD_DIM, L), lambda g, r, i: (g, 0))] + (carry.specs if carry else []),
        out_specs=[pl.BlockSpec((tq, HEAD_DIM), head),
                   pl.BlockSpec((L, HEAD_DIM), lambda g, r, i: (0, g)),
                   pl.BlockSpec((L, HEAD_DIM), lambda g, r, i: (0, g))] + (carry.specs if carry else []),
        out_shape=[jax.ShapeDtypeStruct((L, D_ATTN), F32), jax.ShapeDtypeStruct((L, D_KV), F32),
                   jax.ShapeDtypeStruct((L, D_KV), F32)] + (carry.out_shape if carry else []),
        scratch_shapes=carry.scratch if carry else [],
        compiler_params=_cparams(("arbitrary", "arbitrary", "arbitrary")),
    )(q, do, o, lse, k, v, kt, *(carry.xs if carry else []))
    return outs[0], outs[1], outs[2], outs[3:]


def _seg_perm(a):
    L, C = a.shape
    return a.reshape(SEG, L // SEG, C).transpose(1, 0, 2).reshape(L, C)


def _seg_unperm(a):
    L, C = a.shape
    return a.reshape(L // SEG, SEG, C).transpose(1, 0, 2).reshape(L, C)


def _cmul(ar, ai, br, bi):
    return ar * br - ai * bi, ar * bi + ai * br


def _rows8(rr):
    if isinstance(rr, int):
        return pl.ds(rr * SEG, SEG)
    return pl.ds(pl.multiple_of(rr * SEG, SEG), SEG)


def _seg_scan(xr_ref, xi_ref, ar, ai, reverse, n_rows, visit=None, visit_init=(), entering=None):
    shape = ar.shape
    zero = jnp.zeros(shape, F32)
    rc = n_rows // CHAINS

    def index(q):
        return (n_rows - 1 - q) if reverse else q

    if entering is None:
        def ends(q, carry):
            out = []
            for j in range(CHAINS):
                sl = _rows8(index(j * rc + q))
                pr, pi = _cmul(ar, ai, carry[2 * j], carry[2 * j + 1])
                out += [pr + xr_ref[sl, :], pi + xi_ref[sl, :]]
            return tuple(out)

        def ends_block(qb, carry):
            for t in range(SCAN_UNROLL):
                carry = ends(qb * SCAN_UNROLL + t, carry)
            return carry

        e = lax.fori_loop(0, rc // SCAN_UNROLL, ends_block, (zero,) * (2 * CHAINS))

        pr, pi = ar, ai
        for _ in range(int(math.log2(rc))):
            pr, pi = _cmul(pr, pi, pr, pi)
        sub = lax.broadcasted_iota(jnp.int32, shape, 0)
        shift = (SEG - 1) if reverse else 1
        edge = (SEG - 1) if reverse else 0
        entering = [(zero, zero)] * CHAINS
        for _ in range(SEG):
            tr, ti = _cmul(pr, pi, *entering[CHAINS - 1])
            cur = (jnp.where(sub == edge, 0.0, pltpu.roll(tr + e[2 * CHAINS - 2], shift, axis=0)),
                   jnp.where(sub == edge, 0.0, pltpu.roll(ti + e[2 * CHAINS - 1], shift, axis=0)))
            entering = [cur]
            for j in range(1, CHAINS):
                tr, ti = _cmul(pr, pi, *cur)
                cur = (tr + e[2 * j - 2], ti + e[2 * j - 1])
                entering.append(cur)

    def step(q, carry, last):
        out, acc = [], carry[2 * CHAINS:]
        for j in range(CHAINS):
            rr = index(j * rc + q)
            sl = _rows8(rr)
            pr, pi = _cmul(ar, ai, carry[2 * j], carry[2 * j + 1])
            nr, ni = pr + xr_ref[sl, :], pi + xi_ref[sl, :]
            xr_ref[sl, :] = nr
            xi_ref[sl, :] = ni
            if visit:
                acc = visit(rr, nr, ni, acc, last and j == CHAINS - 1)
            out += [nr, ni]
        return (*out, *acc)

    def step_block(qb, carry):
        for t in range(SCAN_UNROLL):
            carry = step(qb * SCAN_UNROLL + t, carry, False)
        return carry

    start = tuple(v for pair in entering for v in pair)
    n_blocks = (rc - 1) // SCAN_UNROLL
    carry = lax.fori_loop(0, n_blocks, step_block, (*start, *visit_init))
    for q in range(n_blocks * SCAN_UNROLL, rc - 1):
        carry = step(q, carry, False)
    carry = step(rc - 1, carry, True)
    return entering, carry[2 * CHAINS:]


def _discretise(a_re, a_im, ldt):
    lr = jnp.minimum(a_re, -1e-4)
    li = a_im
    dt = jnp.exp(ldt)
    mag = jnp.exp(lr * dt)
    lbr = mag * jnp.cos(li * dt)
    lbi = mag * jnp.sin(li * dt)
    den = lr * lr + li * li
    nr = lbr - 1.0
    fr = (nr * lr + lbi * li) / den
    fi = (lbi * lr - nr * li) / den
    return lr, li, dt, lbr, lbi, fr, fi


def _lane_row(v):
    return jnp.concatenate([v[g:g + 1, :] for g in range(v.shape[0])], axis=1)


def _ssm_fill_maps(d, prm, tmp_ref, maps):
    a_re_ref, a_im_ref, ldt_ref, bt_re_ref, bt_im_ref, c_re_ref, c_im_ref = prm
    _, _, _, lbr, lbi, fr, fi = _discretise(a_re_ref[d], a_im_ref[d], ldt_ref[d])

    def fill(dst, piece):
        tmp_ref[...] = jnp.zeros_like(tmp_ref)
        for g in range(SLAB_G):
            tmp_ref[g * SSM_H:(g + 1) * SSM_H, g * SSM_P:(g + 1) * SSM_P] = piece(g)
        dst[...] = tmp_ref[...].astype(BF16)

    wbr, wbi, wcr, wci = maps
    fill(wbr, lambda g: fr[g:g + 1] * bt_re_ref[d, g] - fi[g:g + 1] * bt_im_ref[d, g])
    fill(wbi, lambda g: fr[g:g + 1] * bt_im_ref[d, g] + fi[g:g + 1] * bt_re_ref[d, g])
    fill(wcr, lambda g: c_re_ref[d, g])
    fill(wci, lambda g: c_im_ref[d, g])
    return _lane_row(lbr), _lane_row(lbi)


def _ssm_param_specs():
    pole = pl.BlockSpec((2, SLAB_G, SSM_P), lambda j: (0, j, 0))
    step = pl.BlockSpec((2, SLAB_G, 1), lambda j: (0, j, 0))
    mat = pl.BlockSpec((2, SLAB_G, SSM_H, SSM_P), lambda j: (0, j, 0, 0))
    return [pole, pole, step, mat, mat, mat, mat]


_MAP_SCRATCH = [pltpu.VMEM((SLAB, SLAB_S), F32)] + [pltpu.VMEM((SLAB, SLAB_S), BF16)] * 4
_ENT_SPEC = pl.BlockSpec((1, 2, 2 * CHAINS, SEG, SLAB_S), lambda j: (j, 0, 0, 0, 0))
_NT = (((1,), (1,)), ((), ()))


def _ssm_fwd(u, prm, dskip):
    L, C = u.shape
    n_rows = L // SEG
    tc = _pick(L, (2048, 1024, 512, 256))
    u_spec = pl.BlockSpec((L, SLAB), lambda j: (0, j))
    d_spec = pl.BlockSpec((1, SLAB), lambda j: (0, j))

    def body(u_ref, *rest):
        prm_refs, d_ref, y_ref, ent_ref = rest[:7], rest[7], rest[8], rest[9]
        tmp_ref, maps, xr_ref, xi_ref = rest[10], rest[11:15], rest[15], rest[16]
        wbr, wbi, wcr, wci = maps
        y_ref[...] = u_ref[...] * d_ref[...]
        for d in range(2):
            lam_r, lam_i = _ssm_fill_maps(d, prm_refs, tmp_ref, maps)

            def inp(c, _):
                sl = pl.ds(pl.multiple_of(c * tc, tc), tc)
                ub = u_ref[sl, :].astype(BF16)
                xr_ref[sl, :] = jnp.dot(ub, wbr[...], preferred_element_type=F32)
                xi_ref[sl, :] = jnp.dot(ub, wbi[...], preferred_element_type=F32)
                return 0

            lax.fori_loop(0, L // tc, inp, 0)
            ar = jnp.broadcast_to(lam_r, (SEG, SLAB_S))
            ai = jnp.broadcast_to(lam_i, (SEG, SLAB_S))
            entering, _ = _seg_scan(xr_ref, xi_ref, ar, ai, d == 1, n_rows)
            for j, (er, ei) in enumerate(entering):
                ent_ref[0, d, 2 * j] = er
                ent_ref[0, d, 2 * j + 1] = ei

            def outp(c, _):
                sl = pl.ds(pl.multiple_of(c * tc, tc), tc)
                y_ref[sl, :] += (
                    lax.dot_general(xr_ref[sl, :].astype(BF16), wcr[...], _NT, preferred_element_type=F32)
                    - lax.dot_general(xi_ref[sl, :].astype(BF16), wci[...], _NT, preferred_element_type=F32))
                return 0

            lax.fori_loop(0, L // tc, outp, 0)

    return pl.pallas_call(
        body, name="ssm_fwd", grid=(C // SLAB,),
        in_specs=[u_spec] + _ssm_param_specs() + [d_spec],
        out_specs=[u_spec, _ENT_SPEC],
        out_shape=[jax.ShapeDtypeStruct((L, C), F32),
                   jax.ShapeDtypeStruct((C // SLAB, 2, 2 * CHAINS, SEG, SLAB_S), F32)],
        scratch_shapes=_MAP_SCRATCH + [pltpu.VMEM((L, SLAB_S), F32)] * 2,
        compiler_params=_cparams(("arbitrary",)),
    )(u, *prm, dskip)


def _ssm_bwd(u, dy, ent, prm, dskip):
    L, C = u.shape
    n_rows = L // SEG
    n_slab = C // SLAB
    tc = _pick(L, (2048, 1024, 512, 256))
    u_spec = pl.BlockSpec((L, SLAB), lambda j: (0, j))
    d_spec = pl.BlockSpec((1, SLAB), lambda j: (0, j))
    pg_spec = pl.BlockSpec((1, 2, PG_ROWS, SLAB_S), lambda j: (j, 0, 0, 0))

    def body(u_ref, dy_ref, ent_ref, *rest):
        prm_refs, d_ref, du_ref, pg_ref = rest[:7], rest[7], rest[8], rest[9]
        tmp_ref, maps, acc_ref = rest[10], rest[11:15], rest[15]
        xr_ref, xi_ref, gr_ref, gi_ref = rest[16:20]
        wbr, wbi, wcr, wci = maps
        du_ref[...] = dy_ref[...] * d_ref[...]
        pg_ref[...] = jnp.zeros_like(pg_ref)
        pg_ref[0, 0, 66:67, 0:SLAB] = _colsum(dy_ref[...] * u_ref[...])
        for d in range(2):
            lam_r, lam_i = _ssm_fill_maps(d, prm_refs, tmp_ref, maps)

            def inp(c, _):
                sl = pl.ds(pl.multiple_of(c * tc, tc), tc)
                ub = u_ref[sl, :].astype(BF16)
                dyb = dy_ref[sl, :].astype(BF16)
                xr_ref[sl, :] = jnp.dot(ub, wbr[...], preferred_element_type=F32)
                xi_ref[sl, :] = jnp.dot(ub, wbi[...], preferred_element_type=F32)
                gr_ref[sl, :] = jnp.dot(dyb, wcr[...], preferred_element_type=F32)
                gi_ref[sl, :] = -jnp.dot(dyb, wci[...], preferred_element_type=F32)
                return 0

            lax.fori_loop(0, L // tc, inp, 0)
            ar = jnp.broadcast_to(lam_r, (SEG, SLAB_S))
            ai = jnp.broadcast_to(lam_i, (SEG, SLAB_S))
            entering = [(ent_ref[0, d, 2 * j], ent_ref[0, d, 2 * j + 1]) for j in range(CHAINS)]
            _seg_scan(xr_ref, xi_ref, ar, ai, d == 1, n_rows, entering=entering)

            def pole(rr, lr, li, acc, last):
                if last:
                    pr, pi = entering[0]
                else:
                    nb = _rows8(rr + 1 if d == 1 else rr - 1)
                    pr, pi = xr_ref[nb, :], xi_ref[nb, :]
                return acc[0] + lr * pr + li * pi, acc[1] + li * pr - lr * pi

            zero = jnp.zeros((SEG, SLAB_S), F32)
            _, (accr, acci) = _seg_scan(gr_ref, gi_ref, ar, -ai, d == 0, n_rows, pole, (zero, zero))
            pg_ref[0, d, 64:65, :] = _colsum(accr)
            pg_ref[0, d, 65:66, :] = _colsum(acci)

            acc_ref[...] = jnp.zeros_like(acc_ref)

            def outp(c, _):
                sl = pl.ds(pl.multiple_of(c * tc, tc), tc)
                lrb, lib = gr_ref[sl, :].astype(BF16), gi_ref[sl, :].astype(BF16)
                du_ref[sl, :] += (lax.dot_general(lrb, wbr[...], _NT, preferred_element_type=F32)
                                  + lax.dot_general(lib, wbi[...], _NT, preferred_element_type=F32))
                ut = u_ref[sl, :].astype(F32).T.astype(BF16)
                dyt = dy_ref[sl, :].T.astype(BF16)
                acc_ref[0] += jnp.dot(ut, lrb, preferred_element_type=F32)
                acc_ref[1] += jnp.dot(ut, lib, preferred_element_type=F32)
                acc_ref[2] += jnp.dot(dyt, xr_ref[sl, :].astype(BF16), preferred_element_type=F32)
                acc_ref[3] -= jnp.dot(dyt, xi_ref[sl, :].astype(BF16), preferred_element_type=F32)
                return 0

            lax.fori_loop(0, L // tc, outp, 0)
            for m in range(4):
                for g in range(SLAB_G):
                    lanes = slice(g * SSM_P, (g + 1) * SSM_P)
                    pg_ref[0, d, m * SSM_H:(m + 1) * SSM_H, lanes] = acc_ref[m, g * SSM_H:(g + 1) * SSM_H, lanes]

    return pl.pallas_call(
        body, name="ssm_bwd", grid=(n_slab,),
        in_specs=[u_spec, u_spec, _ENT_SPEC] + _ssm_param_specs() + [d_spec],
        out_specs=[u_spec, pg_spec],
        out_shape=[jax.ShapeDtypeStruct((L, C), F32), jax.ShapeDtypeStruct((n_slab, 2, PG_ROWS, SLAB_S), F32)],
        scratch_shapes=_MAP_SCRATCH + [pltpu.VMEM((4, SLAB, SLAB_S), F32)] + [pltpu.VMEM((L, SLAB_S), F32)] * 4,
        compiler_params=_cparams(("arbitrary",), 60 << 20),
    )(u, dy, ent, *prm, dskip)


def _ssm_param_grads(pg, prm):
    n_slab = pg.shape[0]
    G = n_slab * SLAB_G
    pg_spec = pl.BlockSpec((1, 2, PG_ROWS, SLAB_S), lambda j: (j, 0, 0, 0))
    pole, _, step, mat = _ssm_param_specs()[:4]

    def body(pg_ref, a_re_ref, a_im_ref, ldt_ref, bt_re_ref, bt_im_ref,
             dbr_ref, dbi_ref, dcr_ref, dci_ref, dar_ref, dai_ref, dldt_ref, dd_ref):
        dd_ref[...] = pg_ref[0, 0, 66:67, 0:SLAB]
        for d in range(2):
            a_r = a_re_ref[d]
            lr, li, dt, lbr, lbi, f_r, f_i = _discretise(a_r, a_im_ref[d], ldt_ref[d])
            gfr_rows, gfi_rows, glr_rows, gli_rows = [], [], [], []
            for g in range(SLAB_G):
                lanes = slice(g * SSM_P, (g + 1) * SSM_P)
                gbr, gbi = pg_ref[0, d, 0:SSM_H, lanes], pg_ref[0, d, SSM_H:2 * SSM_H, lanes]
                b_r, b_i = bt_re_ref[d, g], bt_im_ref[d, g]
                fr, fi = f_r[g:g + 1], f_i[g:g + 1]
                dbr_ref[d, g] = fr * gbr + fi * gbi
                dbi_ref[d, g] = fr * gbi - fi * gbr
                gfr_rows.append(_colsum(gbr * b_r + gbi * b_i))
                gfi_rows.append(_colsum(gbi * b_r - gbr * b_i))
                dcr_ref[d, g] = pg_ref[0, d, 2 * SSM_H:3 * SSM_H, lanes]
                dci_ref[d, g] = pg_ref[0, d, 3 * SSM_H:4 * SSM_H, lanes]
                glr_rows.append(pg_ref[0, d, 64:65, lanes])
                gli_rows.append(pg_ref[0, d, 65:66, lanes])
            gfr, gfi = jnp.concatenate(gfr_rows, axis=0), jnp.concatenate(gfi_rows, axis=0)
            glr, gli = jnp.concatenate(glr_rows, axis=0), jnp.concatenate(gli_rows, axis=0)
            den = lr * lr + li * li
            ir, ii = lr / den, -li / den
            tr, ti = _cmul(ir, -ii, gfr, gfi)
            glbr, glbi = glr + tr, gli + ti
            qr, qi = _cmul(f_r, f_i, ir, ii)
            dlr, dli = _cmul(-qr, qi, gfr, gfi)
            zr, zi = _cmul(lbr, -lbi, glbr, glbi)
            dlr = dlr + dt * zr
            dli = dli + dt * zi
            dar_ref[d] = jnp.where(a_r < -1e-4, dlr, jnp.where(a_r == -1e-4, 0.5 * dlr, 0.0))
            dai_ref[d] = dli
            dldt_ref[d] = jnp.sum(lr * zr + li * zi, axis=-1, keepdims=True) * dt

    a_re, a_im, ldt, bt_re, bt_im = prm[:5]
    mshape = jax.ShapeDtypeStruct(bt_re.shape, F32)
    pshape = jax.ShapeDtypeStruct(a_re.shape, F32)
    return pl.pallas_call(
        body, name="ssm_param_grads", grid=(n_slab,),
        in_specs=[pg_spec, pole, pole, step, mat, mat],
        out_specs=[mat, mat, mat, mat, pole, pole, step, pl.BlockSpec((1, SLAB), lambda j: (0, j))],
        out_shape=[mshape, mshape, mshape, mshape, pshape, pshape, jax.ShapeDtypeStruct(ldt.shape, F32),
                   jax.ShapeDtypeStruct((1, n_slab * SLAB), F32)],
        compiler_params=_cparams(("arbitrary",)),
    )(pg, a_re, a_im, ldt, bt_re, bt_im)


def _peer(k, x, y, c):
    return (1 - x if k & 4 else x, 1 - y if k & 2 else y, 1 - c if k & 1 else c)


def _dev_index(pos):
    return 4 * pos[0] + 2 * pos[1] + pos[2]


def _chip_index(pos):
    return 2 * pos[0] + pos[1]


def _sibling_swap(x, name):
    any_spec = pl.BlockSpec(memory_space=pl.ANY)

    def body(x_ref, out_ref, send_sems, recv_sems):
        x_, y_, c_ = lax.axis_index("x"), lax.axis_index("y"), lax.axis_index("c")
        copies = [pltpu.make_async_remote_copy(
            src_ref=x_ref.at[2 * chip + (1 - c_)], dst_ref=out_ref.at[chip], send_sem=send_sems.at[chip],
            recv_sem=recv_sems.at[chip], device_id=(x_, y_, 1 - c_), device_id_type=pl.DeviceIdType.MESH)
            for chip in range(N_CHIPS)]
        for cp in copies:
            cp.start()
        for cp in copies:
            cp.wait()

    return pl.pallas_call(
        body, name=name, out_shape=jax.ShapeDtypeStruct((N_CHIPS,) + x.shape[1:], x.dtype),
        in_specs=[any_spec], out_specs=any_spec,
        scratch_shapes=[pltpu.SemaphoreType.DMA((N_CHIPS,)), pltpu.SemaphoreType.DMA((N_CHIPS,))],
    )(x)


def _pair_sum(x, got, name):
    n, R, W = got.shape
    tr = _row_tile(R, W, 5 << 20)
    core = lax.axis_index("c").astype(jnp.int32).reshape(1)

    def body(core_ref, a_ref, b_ref, o_ref):
        o_ref[...] = (a_ref[...].astype(F32) + b_ref[...].astype(F32)).astype(BF16)

    spec = pl.BlockSpec((1, tr, W), lambda i, j, c: (i, j, 0))
    grid_spec = pltpu.PrefetchScalarGridSpec(
        num_scalar_prefetch=1, grid=(n, R // tr),
        in_specs=[pl.BlockSpec((1, tr, W), lambda i, j, c: (2 * i + c[0], j, 0)), spec], out_specs=spec)
    return pl.pallas_call(
        body, name=name, grid_spec=grid_spec, out_shape=jax.ShapeDtypeStruct((n, R, W), BF16),
        compiler_params=_cparams(("arbitrary", "arbitrary")),
    )(core, x, got)


def _all_gather(xs, name):
    n = len(xs)
    any_spec = pl.BlockSpec(memory_space=pl.ANY)

    def body(*refs):
        x_refs, out_refs = refs[:n], refs[n:2 * n]
        send_sems, recv_sems, local_sems = refs[2 * n:]
        x, y, c = lax.axis_index("x"), lax.axis_index("y"), lax.axis_index("c")
        me, sibling = (x, y, c), (x, y, 1 - c)
        chips = [(1 - x, y), (x, 1 - y), (1 - x, 1 - y)]

        def copy(a, k, block, to, src=None):
            dst = out_refs[a].at[_dev_index(block)]
            return pltpu.make_async_remote_copy(
                src_ref=dst if src is None else src, dst_ref=dst,
                send_sem=send_sems.at[a, k], recv_sem=recv_sems.at[a, k],
                device_id=to, device_id_type=pl.DeviceIdType.MESH)

        mine = [pltpu.make_async_copy(x_refs[a], out_refs[a].at[_dev_index(me)], local_sems.at[a]) for a in range(n)]
        for cp in mine:
            cp.start()
        first = []
        for a in range(n):
            first.append(copy(a, 0, me, sibling, src=x_refs[a]))
            first += [copy(a, 1 + j, me, (*chip, c), src=x_refs[a]) for j, chip in enumerate(chips)]
        for cp in first:
            cp.start()
        passed = []
        for j, chip in enumerate(chips):
            for a in range(n):
                copy(a, 1 + j, (*chip, c), me).wait_recv()
                fwd = copy(a, 4 + j, (*chip, c), sibling)
                fwd.start()
                passed.append(fwd)
        for a in range(n):
            copy(a, 0, sibling, me).wait_recv()
            for j, chip in enumerate(chips):
                copy(a, 4 + j, (*chip, 1 - c), me).wait_recv()
        for cp in first + passed:
            cp.wait_send()
        for cp in mine:
            cp.wait()

    return pl.pallas_call(
        body, name=name,
        out_shape=[jax.ShapeDtypeStruct((N_DEV,) + v.shape, v.dtype) for v in xs],
        in_specs=[any_spec] * n, out_specs=[any_spec] * n,
        scratch_shapes=[pltpu.SemaphoreType.DMA((n, 7)), pltpu.SemaphoreType.DMA((n, 7)),
                        pltpu.SemaphoreType.DMA((n,))],
    )(*xs)


def _sum_blocks(x, name):
    _, R, W = x.shape

    def body(x_ref, o_ref):
        acc = x_ref[0].astype(F32)
        for d in range(1, N_DEV):
            acc = acc + x_ref[d].astype(F32)
        o_ref[...] = acc

    return pl.pallas_call(body, name=name, out_shape=jax.ShapeDtypeStruct((R, W), F32),
                          compiler_params=pltpu.CompilerParams(vmem_limit_bytes=VMEM_LIMIT))(x)


def _adam_update(w, g, m, v):
    mn = ADAM_B1 * m + (1.0 - ADAM_B1) * g
    vn = ADAM_B2 * v + (1.0 - ADAM_B2) * (g * g)
    m_hat = mn / (1.0 - ADAM_B1 ** ADAM_STEP)
    v_hat = vn / (1.0 - ADAM_B2 ** ADAM_STEP)
    return -ADAM_LR * (m_hat / (jnp.sqrt(v_hat) + ADAM_EPS) + ADAM_WD * w), mn, vn


def _row_tile(R, W, budget):
    padded = -(-W // LANES) * LANES * 4
    if R * padded <= budget:
        return R
    return _pick(R, [t for t in (2048, 1024, 512, 256, 128, 64, 32, 16, 8) if t * padded <= budget])


def _adamw(w, g, m, v, name):
    R, W = w.shape
    tr = _row_tile(R, W, 1 << 20)

    def body(w_ref, g_ref, m_ref, v_ref, d_ref, mo_ref, vo_ref):
        d_ref[...], mo_ref[...], vo_ref[...] = _adam_update(w_ref[...], g_ref[...], m_ref[...], v_ref[...])

    spec = pl.BlockSpec((tr, W), lambda i: (i, 0))
    shp = jax.ShapeDtypeStruct((R, W), F32)
    return pl.pallas_call(
        body, name=name, grid=(R // tr,), in_specs=[spec] * 4, out_specs=[spec] * 3, out_shape=[shp] * 3,
        compiler_params=_cparams(("arbitrary",)),
    )(w, g, m, v)


def _adamw_group(ws, gs, ms, vs, name):
    n = len(ws)

    def body(*refs):
        ins, outs = refs[:4 * n], refs[4 * n:]
        for i in range(n):
            w_ref, g_ref, m_ref, v_ref = ins[i], ins[n + i], ins[2 * n + i], ins[3 * n + i]
            outs[i][...], outs[n + i][...], outs[2 * n + i][...] = _adam_update(
                w_ref[...], g_ref[...], m_ref[...], v_ref[...])

    shapes = [jax.ShapeDtypeStruct(w.shape, F32) for w in ws]
    outs = pl.pallas_call(body, name=name, out_shape=shapes * 3)(*ws, *gs, *ms, *vs)
    return outs[:n], outs[n:2 * n], outs[2 * n:]


def _adamw_reduce(w, land, m, v, name):
    R, W = w.shape
    n = land.shape[0]
    tr = _row_tile(R, W, 1 << 20)

    def body(w_ref, l_ref, m_ref, v_ref, g_ref, d_ref, mo_ref, vo_ref):
        g = l_ref[0].astype(F32)
        for d in range(1, n):
            g = g + l_ref[d].astype(F32)
        g_ref[...] = g
        d_ref[...], mo_ref[...], vo_ref[...] = _adam_update(w_ref[...], g, m_ref[...], v_ref[...])

    spec = pl.BlockSpec((tr, W), lambda i: (i, 0))
    lspec = pl.BlockSpec((n, tr, W), lambda i: (0, i, 0))
    shp = jax.ShapeDtypeStruct((R, W), F32)
    return pl.pallas_call(
        body, name=name, grid=(R // tr,), in_specs=[spec, lspec, spec, spec], out_specs=[spec] * 4,
        out_shape=[shp] * 4, compiler_params=_cparams(("arbitrary",)),
    )(w, land, m, v)


def _gelu(v):
    c = math.sqrt(2.0 / math.pi)
    return 0.5 * v * (1.0 + jnp.tanh(c * (v + 0.044715 * v * v * v)))


def _gelu_grad(v):
    c = math.sqrt(2.0 / math.pi)
    t = jnp.tanh(c * (v + 0.044715 * v * v * v))
    return 0.5 * (1.0 + t) + 0.5 * v * (1.0 - t * t) * c * (1.0 + 3.0 * 0.044715 * v * v)


def kernel(x, p, norm_mix, w_in, q_norm, k_norm, ssm_a_re, ssm_a_im, ssm_log_dt, ssm_b_re, ssm_b_im, ssm_c_re, ssm_c_im, ssm_d, w_glu, b_glu, w_out, norm_ple, w_ple_gate, w_ple_proj, norm_final, loss_target, m_norm_mix, m_w_in, m_q_norm, m_k_norm, m_ssm_a_re, m_ssm_a_im, m_ssm_log_dt, m_ssm_b_re, m_ssm_b_im, m_ssm_c_re, m_ssm_c_im, m_ssm_d, m_w_glu, m_b_glu, m_w_out, m_norm_ple, m_w_ple_gate, m_w_ple_proj, m_norm_final, v_norm_mix, v_w_in, v_q_norm, v_k_norm, v_ssm_a_re, v_ssm_a_im, v_ssm_log_dt, v_ssm_b_re, v_ssm_b_im, v_ssm_c_re, v_ssm_c_im, v_ssm_d, v_w_glu, v_b_glu, v_w_out, v_norm_ple, v_w_ple_gate, v_w_ple_proj, v_norm_final):
    L, D = x.shape[1], x.shape[2]
    D_SSM = ssm_d.shape[1]
    G = D_SSM // SSM_H
    n_slab = D_SSM // SLAB
    n_in = w_in.shape[2]
    D_IN = n_in * N_DEV
    n_pp = w_ple_proj.shape[2]
    n_glu = w_glu.shape[2]
    xs = x[0]
    ps = p[0, 0]
    tgt = loss_target[0]

    (win_t3,) = _all_gather([w_in[0].T.astype(BF16)], "gather_w_in")
    win_t = win_t3.reshape(D_IN, D)
    later_weights = _Carry("gather", [w_glu[0].astype(BF16), w_out[0].astype(BF16), w_ple_gate[0].astype(BF16),
                                      w_ple_proj[0].astype(BF16)])

    ssm_prm = (ssm_a_re[0], ssm_a_im[0], ssm_log_dt[0].reshape(2, G, 1),
               ssm_b_re[0].transpose(0, 1, 3, 2), ssm_b_im[0].transpose(0, 1, 3, 2), ssm_c_re[0], ssm_c_im[0])

    cos, sin = _rope_tables(L)
    hn = _norm_in(xs, norm_mix, "norm_mix")
    ZT = 512
    zp_tile = lambda j: jnp.where(j < 2, j, jnp.where(j < D_IN // ZT - 1, j + 1, 2))
    z = _mm(hn, win_t, "nt", "in_proj", out_dtype=BF16, n_tiles=(ZT, zp_tile))
    qr, kr, vb, kt = _qkv_prep(z, cos, sin, q_norm, k_norm)
    o, lse, (wglu3, wout3, wpg3, wpp3) = _attn_fwd(qr, kr, vb, later_weights)
    wout = wout3.reshape(-1, D)
    wpg = wpg3.reshape(-1, D)
    u_off = 2 * D_ATTN
    u_perm = _seg_perm(z[:, u_off:u_off + D_SSM])
    ys_perm, ssm_ent = _ssm_fwd(u_perm, ssm_prm, ssm_d)
    ys = _seg_unperm(ys_perm)

    tm = _pick(L, (256,))
    wglu = wglu3.transpose(1, 0, 2).reshape(D_SSM, 2 * D_SSM)

    def mix_post(prod, gy_tile, o_ref, ga_ref, gs_ref, b_ref):
        glu_b = (prod + b_ref[...]).astype(BF16)
        gluv = glu_b.astype(F32)
        ga, gs = _f32(ga_ref), _f32(gs_ref)
        y_attn = o_ref[...] * ga * _sigmoid(ga)
        y_ssm = gluv[:, :D_SSM] * _sigmoid(gluv[:, D_SSM:]) * gs * _sigmoid(gs)
        return gy_tile, glu_b, jnp.concatenate([y_attn, y_ssm], axis=-1)

    gy, glu, cat = _mm_rows(ys, wglu, "nn", "mix", mix_post, [o, (z, D_ATTN, 1), (z, D_SSM, 3)], [b_glu],
                            [(D_SSM, BF16), (2 * D_SSM, BF16), (D_ATTN + D_SSM, BF16)],
                            pre=lambda y: _gelu(y).astype(BF16))

    def out_post(prod, x_ref, g_ref):
        h1v = prod + x_ref[...]
        return h1v, h1v * _rms(h1v) * g_ref[...]

    h1, n2 = _mm_rows(cat, wout, "nn", "out_proj", out_post, [xs], [norm_ple], [(D, F32), (D, BF16)])
    pb = ps.astype(BF16)
    pp = _mm(pb, wpp3, "nn", "ple_proj", out_dtype=BF16, b_blk=True)

    nf = norm_final.reshape(1, D)

    def tail_post(gp, h1_ref, pp_ref, t_ref, g_ref):
        gate = _sigmoid(gp)
        ppv = _f32(pp_ref)
        h2 = h1_ref[...] + gate * ppv
        r = _rms(h2)
        hh = h2 * r
        err = hh * g_ref[...] - t_ref[...]
        loss_part = jnp.broadcast_to(0.5 * jnp.sum(jnp.mean(err * err, axis=-1, keepdims=True)), (1, LANES))
        dy = err * (1.0 / D)
        dh2 = _rms_bwd(dy, hh, r, g_ref[...])
        return dh2, dh2 * gate, dh2 * ppv * gate * (1.0 - gate), loss_part, _colsum(dy * hh)

    dh2, dpp, dsg, loss_acc, d_nf = _mm_rows(n2, wpg, "nn", "tail", tail_post, [h1, pp, tgt], [nf],
                                             [(D, F32), (D, BF16), (D, BF16)], [(1, LANES), (1, D)], vmem=58 << 20)

    g_wpp3 = _mm(pb, dpp, "tn", "d_ple_proj", out_dtype=BF16, out_blk=n_pp)
    g_wpg = _mm(n2, dsg, "tn", "d_ple_gate", out_dtype=BF16)
    def ple_bwd_post(dn, h1_ref, dh2_ref, g_ref):
        h1v = h1_ref[...]
        r = _rms(h1v)
        hh = h1v * r
        dh1 = dh2_ref[...] + _rms_bwd(dn, hh, r, g_ref[...])
        return dh1, dh1, _colsum(dn * hh)

    dh1, dh1b, d_nple = _mm_rows(dsg, wpg, "nt", "ple_bwd", ple_bwd_post, [h1, dh2], [norm_ple],
                                 [(D, F32), (D, BF16)], [(1, D)])

    g_wout = _mm(cat, dh1b, "tn", "d_out_proj", out_dtype=BF16)

    def mix_bwd_post(dcat, o_ref, ga_ref, glu_ref, gs_ref):
        dca, dcs = dcat[:, :D_ATTN], dcat[:, D_ATTN:]
        ga, gs = _f32(ga_ref), _f32(gs_ref)
        gla, glb = glu_ref[:, :D_SSM].astype(F32), glu_ref[:, D_SSM:].astype(F32)
        sa, ss, sb = _sigmoid(ga), _sigmoid(gs), _sigmoid(glb)
        do = dca * ga * sa
        dga = dca * o_ref[...] * sa * (1.0 + ga * (1.0 - sa))
        dgs = dcs * gla * sb * ss * (1.0 + gs * (1.0 - ss))
        dy2 = dcs * gs * ss
        da, db = dy2 * sb, dy2 * gla * sb * (1.0 - sb)
        return (do, dga, dgs, jnp.concatenate([da, db], axis=-1),
                jnp.concatenate([_colsum(da), _colsum(db)], axis=-1))

    do, dga, dgs, dglu, g_bglu = _mm_rows(
        dh1b, wout, "nt", "mix_bwd", mix_bwd_post, [o, (z, D_ATTN, 1), glu, (z, D_SSM, 3)], [],
        [(D_ATTN, BF16), (D_ATTN, BF16), (D_SSM, BF16), (2 * D_SSM, BF16)], [(1, 2 * D_SSM)])

    g_wglu3 = _mm(gy, dglu, "tn", "d_glu_proj", out_dtype=BF16, out_blk=n_glu)
    dys = _mm(dglu, wglu3, "nt", "d_ssm_out", b_blk=True,
              post=(lambda out, y: out * _gelu_grad(y), ys))
    du_perm, pg = _ssm_bwd(u_perm, _seg_perm(dys), ssm_ent, ssm_prm, ssm_d)
    du = _seg_unperm(du_perm)

    pg_send = pg.reshape(N_DEV, (n_slab // N_DEV) * 2 * PG_ROWS, SLAB_S)
    dqs, dkr, dvv, (l_wglu, l_wout, l_wpg, l_wpp, l_pg) = _attn_bwd(
        qr, kr, vb, kt, do, o, lse,
        _Carry("a2a", [g_wglu3, g_wout.reshape(N_DEV, -1, D), g_wpg.reshape(N_DEV, -1, D), g_wpp3, pg_send]))

    scale = HEAD_DIM ** -0.5
    kblk = 4 * D_ATTN // D_KV

    a0, k0, v0, u0, s0 = D_ATTN + 2 * D_KV, D_ATTN, D_ATTN + D_KV, 2 * D_ATTN + 2 * D_KV, 2 * D_ATTN + 2 * D_KV + D_SSM

    def qkv_bwd_body(dq_ref, dk_ref, dv_ref, q_ref, k_ref, cos_ref, sin_ref, qn_ref, kn_ref, dga_ref, du_ref, dgs_ref,
                     dz_ref, dqn_ref, dkn_ref):
        c, s = cos_ref[...], sin_ref[...]
        dz_ref[:, a0:a0 + D_ATTN] = dga_ref[...]
        dz_ref[:, u0:u0 + D_SSM] = du_ref[...].astype(BF16)
        dz_ref[:, s0:s0 + D_SSM] = dgs_ref[...]

        def head(g, xh, w):
            dn = g * c + _partner(g * s)
            r = _rms(xh)
            xhat = xh * r
            return _rms_bwd(dn, xhat, r, w), _colsum(dn * xhat)

        dqn = jnp.zeros((1, HEAD_DIM), F32)
        for h in range(N_HEADS):
            sl = slice(h * HEAD_DIM, (h + 1) * HEAD_DIM)
            dx, dw = head(dq_ref[:, sl] * scale, q_ref[:, sl].astype(F32), qn_ref[...])
            dz_ref[:, sl] = dx.astype(BF16)
            dqn = dqn + dw
        dkn = jnp.zeros((1, HEAD_DIM), F32)
        for h in range(N_KV):
            sl = slice(h * HEAD_DIM, (h + 1) * HEAD_DIM)
            dx, dw = head(dk_ref[:, sl], k_ref[:, sl].astype(F32), kn_ref[...])
            dz_ref[:, k0 + h * HEAD_DIM:k0 + (h + 1) * HEAD_DIM] = dx.astype(BF16)
            dkn = dkn + dw
        dz_ref[:, v0:v0 + D_KV] = dv_ref[...].astype(BF16)
        _acc(dqn_ref, dqn)
        _acc(dkn_ref, dkn)

    dz, g_qn, g_kn = _rowcall(
        qkv_bwd_body, "qkv_bwd", L, tm,
        [(dqs, _rspec(tm, D_ATTN)), (dkr, _rspec(tm, D_KV)), (dvv, _rspec(tm, D_KV)),
         (z, _rspec(tm, D_ATTN, 0)), (z, _rspec(tm, D_KV, kblk)), (cos, _rspec(tm, HEAD_DIM)),
         (sin, _rspec(tm, HEAD_DIM)), (q_norm, _fspec(q_norm.shape)), (k_norm, _fspec(k_norm.shape)),
         (dga, _rspec(tm, D_ATTN)), (du, _rspec(tm, D_SSM)), (dgs, _rspec(tm, D_SSM))],
        [(D_IN, BF16)], [(1, HEAD_DIM), (1, HEAD_DIM)])

    g_win_t = _mm(dz, hn, "tn", "d_in_proj", out_dtype=BF16)
    g_win8 = g_win_t.reshape(N_DEV, n_in, D)
    from_sibling = _sibling_swap(g_win8, "swap_d_w_in")
    pair = _pair_sum(g_win8, from_sibling, "pair_sum_d_w_in")
    dhn, (l_win_t,) = _mm(dz, win_t, "nn", "d_norm_mix_in", out_dtype=BF16,
                          carry=_Carry("a2a_chips", [pair]))

    def in_bwd_body(x_ref, dn_ref, dh1_ref, g_ref, dx_ref, dg_ref):
        xv = x_ref[...]
        r = _rms(xv)
        hh = xv * r
        dn = _f32(dn_ref)
        _acc(dg_ref, _colsum(dn * hh))
        dx_ref[...] = dh1_ref[...] + _rms_bwd(dn, hh, r, g_ref[...])

    grad_x, g_nmix = _rowcall(
        in_bwd_body, "in_bwd", L, tm,
        [(xs, _rspec(tm, D)), (dhn, _rspec(tm, D)), (dh1, _rspec(tm, D)), (norm_mix, _fspec(norm_mix.shape))],
        [(D, F32)], [(1, D)])

    tiny_parts = [g_nmix, g_bglu, d_nple, d_nf, g_qn, g_kn, loss_acc[:, :1]]
    tiny_flat = jnp.concatenate([t.reshape(-1) for t in tiny_parts])
    tiny_rows = -(-tiny_flat.shape[0] // (8 * LANES)) * 8
    tiny = jnp.pad(tiny_flat, (0, tiny_rows * LANES - tiny_flat.shape[0])).reshape(tiny_rows, LANES)
    pg_sum = _sum_blocks(l_pg, "sum_ssm_grads")
    pg_all, tiny_all = _all_gather([pg_sum, tiny], "gather_small_grads")
    (g_bt_re, g_bt_im, g_c_re, g_c_im, g_a_re, g_a_im, g_ldt, g_skip) = _ssm_param_grads(
        pg_all.reshape(n_slab, 2, PG_ROWS, SLAB_S), ssm_prm)
    tiny_sum = _sum_blocks(tiny_all, "sum_tiny_grads").reshape(-1)
    tiny_grads, off = [], 0
    for t in tiny_parts:
        tiny_grads.append(tiny_sum[off:off + t.size].reshape(t.shape))
        off += t.size
    r_nmix, r_bglu, r_nple, r_nf, r_qn, r_kn, loss = tiny_grads
    loss = loss.reshape(())

    grads, deltas, new_ms, new_vs = {}, {}, {}, {}
    outs = _adamw_reduce(w_in[0].T, l_win_t, m_w_in[0].T, v_w_in[0].T, "adamw_w_in")
    grads["w_in"], deltas["w_in"], new_ms["w_in"], new_vs["w_in"] = [t.T[None] for t in outs]
    big = [("w_glu", w_glu, l_wglu, m_w_glu, v_w_glu),
           ("w_out", w_out, l_wout, m_w_out, v_w_out), ("w_ple_gate", w_ple_gate, l_wpg, m_w_ple_gate, v_w_ple_gate),
           ("w_ple_proj", w_ple_proj, l_wpp, m_w_ple_proj, v_w_ple_proj)]
    for name, w, ld, m, v in big:
        shp = w.shape
        outs = _adamw_reduce(w[0], ld, m[0], v[0], "adamw_" + name)
        grads[name], deltas[name], new_ms[name], new_vs[name] = [t.reshape(shp) for t in outs]
    bt2 = (2 * G * SSM_H, SSM_P)
    for name, w, g, m, v in (("ssm_b_re", ssm_b_re, g_bt_re, m_ssm_b_re, v_ssm_b_re),
                             ("ssm_b_im", ssm_b_im, g_bt_im, m_ssm_b_im, v_ssm_b_im)):
        to2 = lambda t: t[0].transpose(0, 1, 3, 2).reshape(bt2)
        back = lambda t: t.reshape(2, G, SSM_H, SSM_P).transpose(0, 1, 3, 2)[None]
        outs = _adamw(to2(w), g.reshape(bt2), to2(m), to2(v), "adamw_" + name)
        grads[name] = back(g)
        deltas[name], new_ms[name], new_vs[name] = [back(t) for t in outs]
    small = [("norm_mix", norm_mix, r_nmix, m_norm_mix, v_norm_mix, (1, D)),
             ("q_norm", q_norm, r_qn, m_q_norm, v_q_norm, (1, HEAD_DIM)),
             ("k_norm", k_norm, r_kn, m_k_norm, v_k_norm, (1, HEAD_DIM)),
             ("ssm_a_re", ssm_a_re, g_a_re, m_ssm_a_re, v_ssm_a_re, (2 * G, SSM_P)),
             ("ssm_a_im", ssm_a_im, g_a_im, m_ssm_a_im, v_ssm_a_im, (2 * G, SSM_P)),
             ("ssm_log_dt", ssm_log_dt, g_ldt, m_ssm_log_dt, v_ssm_log_dt, (2, G)),
             ("ssm_c_re", ssm_c_re, g_c_re, m_ssm_c_re, v_ssm_c_re, (2 * G * SSM_H, SSM_P)),
             ("ssm_c_im", ssm_c_im, g_c_im, m_ssm_c_im, v_ssm_c_im, (2 * G * SSM_H, SSM_P)),
             ("ssm_d", ssm_d, g_skip, m_ssm_d, v_ssm_d, (1, D_SSM)),
             ("b_glu", b_glu, r_bglu, m_b_glu, v_b_glu, (1, 2 * D_SSM)),
             ("norm_ple", norm_ple, r_nple, m_norm_ple, v_norm_ple, (1, D)),
             ("norm_final", norm_final, r_nf, m_norm_final, v_norm_final, (1, D))]
    group = [it for it in small if it[5][0] * it[5][1] <= (1 << 14)]
    grouped = {it[0] for it in group}
    for name, w, g, m, v, s2 in small:
        if name in grouped:
            continue
        shp = w.shape
        outs = _adamw(w.reshape(s2), g.reshape(s2), m.reshape(s2), v.reshape(s2), "adamw_" + name)
        grads[name] = g.reshape(shp)
        deltas[name], new_ms[name], new_vs[name] = [t.reshape(shp) for t in outs]
    ds, mns, vns = _adamw_group(*[[it[i].reshape(it[5]) for it in group] for i in (1, 2, 3, 4)], "adamw_tiny")
    for (name, w, g, _, _, _), d_, m_, v_ in zip(group, ds, mns, vns):
        shp = w.shape
        grads[name] = g.reshape(shp)
        deltas[name], new_ms[name], new_vs[name] = d_.reshape(shp), m_.reshape(shp), v_.reshape(shp)

    order = ["norm_mix", "w_in", "q_norm", "k_norm", "ssm_a_re", "ssm_a_im", "ssm_log_dt", "ssm_b_re", "ssm_b_im",
             "ssm_c_re", "ssm_c_im", "ssm_d", "w_glu", "b_glu", "w_out", "norm_ple", "w_ple_gate", "w_ple_proj",
             "norm_final"]
    return (loss, grad_x[None], *[grads[k] for k in order], *[deltas[k] for k in order],
            *[new_ms[k] for k in order], *[new_vs[k] for k in order])
```

```python
import functools
import math

import numpy as np
import jax
import jax.numpy as jnp
from jax import lax
from jax.experimental import pallas as pl
from jax.experimental.pallas import tpu as pltpu

F32 = jnp.float32
BF16 = jnp.bfloat16

N_DEV = 8
N_CHIPS = 4
EPS = 1e-6
GRID_W = 64
ROPE_THETA = 10000.0
HEAD_DIM = 128
N_HEADS = 8
N_KV = 2
REP = N_HEADS // N_KV
D_ATTN = N_HEADS * HEAD_DIM
D_KV = N_KV * HEAD_DIM
SSM_H = 16
SSM_P = 64
SLAB = 128
SLAB_G = SLAB // SSM_H
SLAB_S = SLAB_G * SSM_P
SEG = 8
CHAINS = 2
SCAN_UNROLL = 8
LANES = 128
PG_ROWS = 72
VMEM_LIMIT = 48 << 20

ADAM_LR = 0.001
ADAM_B1 = 0.9
ADAM_B2 = 0.999
ADAM_EPS = 1e-08
ADAM_WD = 0.01
ADAM_STEP = 10


def _pick(n, cands):
    for c in cands:
        if n % c == 0:
            return c
    return n


def _cparams(sem, vmem=VMEM_LIMIT):
    return pltpu.CompilerParams(dimension_semantics=sem, vmem_limit_bytes=vmem)


class _Carry:
    def __init__(self, kind, xs):
        self.kind, self.xs, self.n = kind, list(xs), len(xs)
        self.ks = (2, 4, 6) if kind == "a2a_chips" else tuple(range(1, N_DEV))
        self.index = _chip_index if kind == "a2a_chips" else _dev_index
        lead = (N_DEV,) if kind == "gather" else ()
        self.out_shape = [jax.ShapeDtypeStruct(lead + v.shape, v.dtype) for v in xs]
        self.specs = [pl.BlockSpec(memory_space=pl.ANY)] * self.n
        self.scratch = [pltpu.SemaphoreType.DMA((self.n, len(self.ks))), pltpu.SemaphoreType.DMA((self.n, len(self.ks))),
                        pltpu.SemaphoreType.DMA((self.n,))]

    def _copies(self, x_refs, out_refs, sems):
        send_sems, recv_sems, local_sems = sems
        x, y, c = lax.axis_index("x"), lax.axis_index("y"), lax.axis_index("c")
        me = self.index((x, y, c))
        mine, sends, arrivals = [], [], []
        for a in range(self.n):
            src_mine = x_refs[a] if self.kind == "gather" else x_refs[a].at[me]
            mine.append(pltpu.make_async_copy(src_mine, out_refs[a].at[me], local_sems.at[a]))
            for s, k in enumerate(self.ks):
                peer = _peer(k, x, y, c)
                src = x_refs[a] if self.kind == "gather" else x_refs[a].at[self.index(peer)]
                sends.append(pltpu.make_async_remote_copy(
                    src_ref=src, dst_ref=out_refs[a].at[me], send_sem=send_sems.at[a, s],
                    recv_sem=recv_sems.at[a, s], device_id=peer, device_id_type=pl.DeviceIdType.MESH))
                land = out_refs[a].at[self.index(peer)]
                arrivals.append(pltpu.make_async_remote_copy(
                    src_ref=land, dst_ref=land, send_sem=send_sems.at[a, s],
                    recv_sem=recv_sems.at[a, s], device_id=peer, device_id_type=pl.DeviceIdType.MESH))
        return mine, sends, arrivals

    def start(self, x_refs, out_refs, sems):
        mine, sends, _ = self._copies(x_refs, out_refs, sems)
        for cp in mine + sends:
            cp.start()

    def wait(self, x_refs, out_refs, sems):
        mine, sends, arrivals = self._copies(x_refs, out_refs, sems)
        for cp in arrivals:
            cp.wait_recv()
        for cp in sends:
            cp.wait_send()
        for cp in mine:
            cp.wait()


def _grid_edges(grid):
    first = functools.reduce(lambda p, q: p & q, [pl.program_id(d) == 0 for d in range(len(grid))])
    last = functools.reduce(lambda p, q: p & q, [pl.program_id(d) == g - 1 for d, g in enumerate(grid)])
    return first, last


def _mm(a, b, mode, name, out_dtype=F32, add=None, bias=None, a_blk=False, b_blk=False, out_blk=0, carry=None,
        n_tiles=None, post=None):
    w = b.shape[2] if b_blk else out_blk
    if mode == "nn":
        M, K = a.shape
        N = b.shape[0] * w if b_blk else b.shape[1]
    elif mode == "nt":
        M = a.shape[1] if a_blk else a.shape[0]
        N = b.shape[1] if b_blk else b.shape[0]
        K = b.shape[0] * w if b_blk else b.shape[1]
    else:
        K, M = a.shape
        N = b.shape[0] * w if b_blk else b.shape[1]
    tm = _pick(M, (1024, 768, 512, 256))
    tn = _pick(N, (1024, 768, 512, 256))
    if mode == "tn" and N <= 2048:
        tn = N
    tk = K if (mode != "tn" and K <= 2048) else _pick(K, (2048, 1024, 768, 512, 256) if mode == "tn"
                                                       else (1024, 768, 512, 256))
    if mode == "nn" and K > 2048 and N <= 2048:
        tn, tk = N, _pick(K, (1536, 1024, 768, 512, 256))
    perm = lambda j: j
    if n_tiles:
        tn, perm = n_tiles
        tm = _pick(M, (2048, 1024, 512, 256))
    if mode == "nt" and b_blk:
        tk = w
    elif b_blk or out_blk:
        tn = w
    nk = K // tk
    grid = (M // tm, N // tn, nk)
    if mode == "nn":
        a_spec = pl.BlockSpec((tm, tk), lambda i, j, k: (i, k))
        b_spec = (pl.BlockSpec((1, tk, tn), lambda i, j, k: (j, k, 0)) if b_blk
                  else pl.BlockSpec((tk, tn), lambda i, j, k: (k, j)))
        dims = (((1,), (0,)), ((), ()))
    elif mode == "nt":
        a_spec = (pl.BlockSpec((1, tm, tk), lambda i, j, k: (k, i, 0)) if a_blk
                  else pl.BlockSpec((tm, tk), lambda i, j, k: (i, k)))
        b_spec = (pl.BlockSpec((1, tn, tk), lambda i, j, k: (k, j, 0)) if b_blk
                  else pl.BlockSpec((tn, tk), lambda i, j, k: (perm(j), k)))
        dims = (((1,), (1,)), ((), ()))
    else:
        a_spec = pl.BlockSpec((tk, tm), lambda i, j, k: (k, i))
        b_spec = (pl.BlockSpec((1, tk, tn), lambda i, j, k: (j, k, 0)) if b_blk
                  else pl.BlockSpec((tk, tn), lambda i, j, k: (k, j)))
        dims = (((0,), (0,)), ((), ()))
    if out_blk:
        out_spec = pl.BlockSpec((1, tm, tn), lambda i, j, k: (j, i, 0))
        out_shape = jax.ShapeDtypeStruct((N // tn, M, tn), out_dtype)
    else:
        out_spec = pl.BlockSpec((tm, tn), lambda i, j, k: (i, j))
        out_shape = jax.ShapeDtypeStruct((M, N), out_dtype)
    extras, extra_specs, combine = [], [], []
    if add is not None:
        extras.append(add)
        extra_specs.append(pl.BlockSpec((tm, tn), lambda i, j, k: (i, j)))
        combine.append(lambda out, t: out + t)
    if bias is not None:
        extras.append(bias)
        extra_specs.append(pl.BlockSpec((1, tn), lambda i, j, k: (0, j)))
        combine.append(lambda out, t: out + t)
    if post is not None:
        extras.append(post[1])
        extra_specs.append(pl.BlockSpec((tm, tn), lambda i, j, k: (i, j)))
        combine.append(post[0])

    n_ex = len(extras)
    nc = carry.n if carry else 0

    def body(a_ref, b_ref, *rest):
        ex_refs, cx = rest[:n_ex], rest[n_ex:n_ex + nc]
        o_ref, cout = rest[n_ex + nc], rest[n_ex + nc + 1:n_ex + 2 * nc + 1]
        tail = rest[n_ex + 2 * nc + 1:]
        sems = tail[:3] if carry else ()
        first, last = _grid_edges(grid)
        if carry:
            @pl.when(first)
            def _():
                carry.start(cx, cout, sems)

        def product():
            av = a_ref[0] if a_blk else a_ref[...]
            bv = b_ref[0] if b_blk else b_ref[...]
            return lax.dot_general(av, bv, dims, preferred_element_type=F32)

        def finish(out):
            for r, fn in zip(ex_refs, combine):
                out = fn(out, r[...])
            if out_blk:
                o_ref[0] = out.astype(out_dtype)
            else:
                o_ref[...] = out.astype(out_dtype)

        if nk == 1:
            finish(product())
        else:
            acc_ref = tail[-1]
            k = pl.program_id(2)

            @pl.when(k == 0)
            def _():
                acc_ref[...] = jnp.zeros_like(acc_ref)

            acc_ref[...] += product()

            @pl.when(k == nk - 1)
            def _():
                finish(acc_ref[...])

        if carry:
            @pl.when(last)
            def _():
                carry.wait(cx, cout, sems)

    scratch = (carry.scratch if carry else []) + ([pltpu.VMEM((tm, tn), F32)] if nk > 1 else [])
    outs = pl.pallas_call(
        body, name=name, grid=grid,
        in_specs=[a_spec, b_spec] + extra_specs + (carry.specs if carry else []),
        out_specs=[out_spec] + (carry.specs if carry else []),
        out_shape=[out_shape] + (carry.out_shape if carry else []),
        scratch_shapes=scratch,
        compiler_params=_cparams(("arbitrary", "arbitrary", "arbitrary")),
    )(a, b, *extras, *(carry.xs if carry else []))
    return (outs[0], outs[1:]) if carry else outs[0]


def _rspec(tm, w, cb=0):
    return pl.BlockSpec((tm, w), lambda i: (i, cb))


def _fspec(shape):
    nd = len(shape)
    return pl.BlockSpec(shape, lambda i: (0,) * nd)


def _rowcall(body, name, L, tm, ins, row_outs, acc_outs=()):
    out_shape = [jax.ShapeDtypeStruct((L, w), dt) for w, dt in row_outs]
    out_shape += [jax.ShapeDtypeStruct(s, F32) for s in acc_outs]
    out_specs = [_rspec(tm, w) for w, _ in row_outs] + [_fspec(s) for s in acc_outs]
    return pl.pallas_call(
        body, name=name, grid=(L // tm,),
        in_specs=[s for _, s in ins], out_specs=out_specs, out_shape=out_shape,
        compiler_params=_cparams(("arbitrary",)),
    )(*[a for a, _ in ins])


def _mm_rows(a, b, mode, name, post, row_ins, full_ins, row_outs, acc_outs=(), vmem=VMEM_LIMIT, pre=None):
    M, K = a.shape
    N = b.shape[1] if mode == "nn" else b.shape[0]
    tm = _pick(M, (256,))
    dims = (((1,), (0,)), ((), ())) if mode == "nn" else (((1,), (1,)), ((), ()))
    n_in, n_row = len(row_ins) + len(full_ins), len(row_outs)

    def body(a_ref, b_ref, *rest):
        ins, outs = rest[:n_in], rest[n_in:]
        av = pre(a_ref[...]) if pre else a_ref[...]
        prod = lax.dot_general(av, b_ref[...], dims, preferred_element_type=F32)
        res = post(prod, av, *ins) if pre else post(prod, *ins)
        for o_ref, val in zip(outs[:n_row], res[:n_row]):
            o_ref[...] = val.astype(o_ref.dtype)
        for acc_ref, val in zip(outs[n_row:], res[n_row:]):
            _acc(acc_ref, val)

    out_shape = [jax.ShapeDtypeStruct((M, w), dt) for w, dt in row_outs]
    out_shape += [jax.ShapeDtypeStruct(s, F32) for s in acc_outs]
    row_ins = [r if isinstance(r, tuple) else (r, r.shape[1], 0) for r in row_ins]
    return pl.pallas_call(
        body, name=name, grid=(M // tm,),
        in_specs=[_rspec(tm, K), _fspec(b.shape)] + [_rspec(tm, w, cb) for _, w, cb in row_ins]
        + [_fspec(f.shape) for f in full_ins],
        out_specs=[_rspec(tm, w) for w, _ in row_outs] + [_fspec(s) for s in acc_outs],
        out_shape=out_shape, compiler_params=_cparams(("arbitrary",), vmem),
    )(a, b, *[r for r, _, _ in row_ins], *full_ins)


def _acc(ref, val):
    @pl.when(pl.program_id(0) == 0)
    def _():
        ref[...] = jnp.zeros_like(ref)
    ref[...] += val


def _colsum(v):
    return jnp.sum(v, axis=0, keepdims=True)


def _rms(xv):
    return lax.rsqrt(jnp.mean(xv * xv, axis=-1, keepdims=True) + EPS)


def _rms_bwd(dn, xhat, r, g):
    dng = dn * g
    return r * (dng - xhat * jnp.mean(dng * xhat, axis=-1, keepdims=True))


def _sigmoid(v):
    return jax.nn.sigmoid(v)


def _f32(ref):
    return ref[...].astype(F32)


def _partner(v):
    w = v.shape[-1]
    lane = lax.broadcasted_iota(jnp.int32, v.shape, v.ndim - 1)
    first_half = (lane % 64) < 32
    return jnp.where(first_half, pltpu.roll(v, w - 32, axis=v.ndim - 1), pltpu.roll(v, 32, axis=v.ndim - 1))


def _norm_in(x, g, name):
    L, D = x.shape
    tm = _pick(L, (512, 256))

    def body(x_ref, g_ref, o_ref):
        xv = x_ref[...]
        o_ref[...] = (xv * _rms(xv) * g_ref[...]).astype(BF16)

    return _rowcall(body, name, L, tm, [(x, _rspec(tm, D)), (g, _fspec(g.shape))], [(D, BF16)])[0]


def _rope_tables(L):
    t = np.arange(L)
    rows = (t // GRID_W).astype(np.float32)
    cols = (t % GRID_W).astype(np.float32)
    n_freq = HEAD_DIM // 4
    inv_freq = np.float32(ROPE_THETA) ** (-np.arange(n_freq, dtype=np.float32) / np.float32(n_freq))
    ar = (rows[:, None] * inv_freq[None, :]).astype(np.float32).astype(np.float64)
    ac = (cols[:, None] * inv_freq[None, :]).astype(np.float32).astype(np.float64)
    cos = np.concatenate([np.cos(ar), np.cos(ar), np.cos(ac), np.cos(ac)], axis=-1).astype(np.float32)
    sin = np.concatenate([-np.sin(ar), np.sin(ar), -np.sin(ac), np.sin(ac)], axis=-1).astype(np.float32)
    return jnp.asarray(cos), jnp.asarray(sin)


def _qkv_prep(z, cos, sin, qn, kn):
    L = z.shape[0]
    tm = _pick(L, (512, 256))
    scale = HEAD_DIM ** -0.5
    kblk = 4 * D_ATTN // D_KV

    def body(q_ref, k_ref, v_ref, cos_ref, sin_ref, qn_ref, kn_ref, qo_ref, ko_ref, vo_ref, kt_ref):
        c, s = cos_ref[...], sin_ref[...]

        def head(xh, w):
            n = xh * _rms(xh) * w
            return n * c + _partner(n) * s

        for h in range(N_HEADS):
            sl = slice(h * HEAD_DIM, (h + 1) * HEAD_DIM)
            qo_ref[:, sl] = (head(q_ref[:, sl].astype(F32), qn_ref[...]) * scale).astype(BF16)
        for h in range(N_KV):
            sl = slice(h * HEAD_DIM, (h + 1) * HEAD_DIM)
            kr = head(k_ref[:, sl].astype(F32), kn_ref[...])
            ko_ref[:, sl] = kr.astype(BF16)
            kt_ref[sl, :] = kr.T.astype(BF16)
        vo_ref[...] = v_ref[...].astype(BF16)

    return pl.pallas_call(
        body, name="qkv_prep", grid=(L // tm,),
        in_specs=[_rspec(tm, D_ATTN, 0), _rspec(tm, D_KV, kblk), _rspec(tm, D_KV, kblk + 1),
                  _rspec(tm, HEAD_DIM), _rspec(tm, HEAD_DIM), _fspec(qn.shape), _fspec(kn.shape)],
        out_specs=[_rspec(tm, D_ATTN), _rspec(tm, D_KV), _rspec(tm, D_KV),
                   pl.BlockSpec((D_KV, tm), lambda i: (0, i))],
        out_shape=[jax.ShapeDtypeStruct((L, D_ATTN), BF16), jax.ShapeDtypeStruct((L, D_KV), BF16),
                   jax.ShapeDtypeStruct((L, D_KV), BF16), jax.ShapeDtypeStruct((D_KV, L), BF16)],
        compiler_params=_cparams(("arbitrary",)),
    )(z, z, z, cos, sin, qn, kn)


def _col_to_row(col):
    n = col.shape[0]
    eye = lax.broadcasted_iota(jnp.int32, (n, n), 0) == lax.broadcasted_iota(jnp.int32, (n, n), 1)
    return jnp.sum(jnp.where(eye, col, 0.0), axis=0, keepdims=True)


def _attn_fwd(q, k, v, carry=None):
    L = q.shape[0]
    tq = _pick(L, (256, 128))
    grid = (N_HEADS, L // tq)
    nc = carry.n if carry else 0

    def body(q_ref, k_ref, v_ref, *rest):
        cx, (o_ref, lse_ref) = rest[:nc], rest[nc:nc + 2]
        cout, sems = rest[nc + 2:2 * nc + 2], rest[2 * nc + 2:]
        first, last = _grid_edges(grid)
        if carry:
            @pl.when(first)
            def _():
                carry.start(cx, cout, sems)

        s = lax.dot_general(q_ref[...], k_ref[...], (((1,), (1,)), ((), ())), preferred_element_type=F32)
        m = jnp.max(s, axis=-1, keepdims=True)
        e = jnp.exp(s - m)
        l = jnp.sum(e, axis=-1, keepdims=True)
        o_ref[...] = (jnp.dot(e.astype(BF16), v_ref[...], preferred_element_type=F32) / l).astype(BF16)
        lse_ref[0] = _col_to_row(m + jnp.log(l))

        if carry:
            @pl.when(last)
            def _():
                carry.wait(cx, cout, sems)

    outs = pl.pallas_call(
        body, name="attn_fwd", grid=grid,
        in_specs=[pl.BlockSpec((tq, HEAD_DIM), lambda h, i: (i, h)),
                  pl.BlockSpec((L, HEAD_DIM), lambda h, i: (0, h // REP)),
                  pl.BlockSpec((L, HEAD_DIM), lambda h, i: (0, h // REP))] + (carry.specs if carry else []),
        out_specs=[pl.BlockSpec((tq, HEAD_DIM), lambda h, i: (i, h)),
                   pl.BlockSpec((1, 1, tq), lambda h, i: (h, 0, i))] + (carry.specs if carry else []),
        out_shape=[jax.ShapeDtypeStruct((L, D_ATTN), BF16), jax.ShapeDtypeStruct((N_HEADS, 1, L), F32)]
        + (carry.out_shape if carry else []),
        scratch_shapes=carry.scratch if carry else [],
        compiler_params=_cparams(("arbitrary", "arbitrary")),
    )(q, k, v, *(carry.xs if carry else []))
    return outs[0], outs[1], outs[2:]


def _attn_bwd(q, k, v, kt, do, o, lse, carry=None):
    L = q.shape[0]
    tq = _pick(L, (256, 128))
    kc = _pick(L, (512, 256, 128))
    nt = (((1,), (1,)), ((), ()))
    grid = (N_KV, REP, L // tq)
    nc = carry.n if carry else 0

    def body(q_ref, do_ref, o_ref, lse_ref, k_ref, v_ref, kt_ref, *rest):
        cx, (dq_ref, dk_ref, dv_ref) = rest[:nc], rest[nc:nc + 3]
        cout, sems = rest[nc + 3:2 * nc + 3], rest[2 * nc + 3:]
        first, last = _grid_edges(grid)
        if carry:
            @pl.when(first)
            def _():
                carry.start(cx, cout, sems)

        @pl.when((pl.program_id(1) == 0) & (pl.program_id(2) == 0))
        def _():
            dk_ref[...] = jnp.zeros_like(dk_ref)
            dv_ref[...] = jnp.zeros_like(dv_ref)

        qv, dov = q_ref[...], do_ref[...]
        lse_row = lse_ref[0]
        delta = _col_to_row(jnp.sum(dov.astype(F32) * _f32(o_ref), axis=-1, keepdims=True))
        dqt = jnp.zeros((HEAD_DIM, tq), F32)
        for c in range(L // kc):
            sl = slice(c * kc, (c + 1) * kc)
            st = lax.dot_general(k_ref[sl, :], qv, nt, preferred_element_type=F32)
            pt = jnp.exp(st - lse_row)
            dpt = lax.dot_general(v_ref[sl, :], dov, nt, preferred_element_type=F32)
            dst = (pt * (dpt - delta)).astype(BF16)
            dv_ref[sl, :] += jnp.dot(pt.astype(BF16), dov, preferred_element_type=F32)
            dk_ref[sl, :] += jnp.dot(dst, qv, preferred_element_type=F32)
            dqt = dqt + jnp.dot(kt_ref[:, sl], dst, preferred_element_type=F32)
        dq_ref[...] = dqt.T

        if carry:
            @pl.when(last)
            def _():
                carry.wait(cx, cout, sems)

    head = lambda g, r, i: (i, g * REP + r)
    outs = pl.pallas_call(
        body, name="attn_bwd", grid=grid,
        in_specs=[pl.BlockSpec((tq, HEAD_DIM), head), pl.BlockSpec((tq, HEAD_DIM), head),
                  pl.BlockSpec((tq, HEAD_DIM), head),
                  pl.BlockSpec((1, 1, tq), lambda g, r, i: (g * REP + r, 0, i)),
                  pl.BlockSpec((L, HEAD_DIM), lambda g, r, i: (0, g)),
                  pl.BlockSpec((L, HEAD_DIM), lambda g, r, i: (0, g)),
                  pl.BlockSpec((HEAD_DIM, L), lambda g, r, i: (g, 0))] + (carry.specs if carry else []),
        out_specs=[pl.BlockSpec((tq, HEAD_DIM), head),
                   pl.BlockSpec((L, HEAD_DIM), lambda g, r, i: (0, g)),
                   pl.BlockSpec((L, HEAD_DIM), lambda g, r, i: (0, g))] + (carry.specs if carry else []),
        out_shape=[jax.ShapeDtypeStruct((L, D_ATTN), F32), jax.ShapeDtypeStruct((L, D_KV), F32),
                   jax.ShapeDtypeStruct((L, D_KV), F32)] + (carry.out_shape if carry else []),
        scratch_shapes=carry.scratch if carry else [],
        compiler_params=_cparams(("arbitrary", "arbitrary", "arbitrary")),
    )(q, do, o, lse, k, v, kt, *(carry.xs if carry else []))
    return outs[0], outs[1], outs[2], outs[3:]


def _seg_perm(a):
    L, C = a.shape
    return a.reshape(SEG, L // SEG, C).transpose(1, 0, 2).reshape(L, C)


def _seg_unperm(a):
    L, C = a.shape
    return a.reshape(L // SEG, SEG, C).transpose(1, 0, 2).reshape(L, C)


def _cmul(ar, ai, br, bi):
    return ar * br - ai * bi, ar * bi + ai * br


def _rows8(rr):
    if isinstance(rr, int):
        return pl.ds(rr * SEG, SEG)
    return pl.ds(pl.multiple_of(rr * SEG, SEG), SEG)


def _seg_scan(xr_ref, xi_ref, ar, ai, reverse, n_rows, visit=None, visit_init=(), entering=None):
    shape = ar.shape
    zero = jnp.zeros(shape, F32)
    rc = n_rows // CHAINS

    def index(q):
        return (n_rows - 1 - q) if reverse else q

    if entering is None:
        def ends(q, carry):
            out = []
            for j in range(CHAINS):
                sl = _rows8(index(j * rc + q))
                pr, pi = _cmul(ar, ai, carry[2 * j], carry[2 * j + 1])
                out += [pr + xr_ref[sl, :], pi + xi_ref[sl, :]]
            return tuple(out)

        def ends_block(qb, carry):
            for t in range(SCAN_UNROLL):
                carry = ends(qb * SCAN_UNROLL + t, carry)
            return carry

        e = lax.fori_loop(0, rc // SCAN_UNROLL, ends_block, (zero,) * (2 * CHAINS))

        pr, pi = ar, ai
        for _ in range(int(math.log2(rc))):
            pr, pi = _cmul(pr, pi, pr, pi)
        sub = lax.broadcasted_iota(jnp.int32, shape, 0)
        shift = (SEG - 1) if reverse else 1
        edge = (SEG - 1) if reverse else 0
        entering = [(zero, zero)] * CHAINS
        for _ in range(SEG):
            tr, ti = _cmul(pr, pi, *entering[CHAINS - 1])
            cur = (jnp.where(sub == edge, 0.0, pltpu.roll(tr + e[2 * CHAINS - 2], shift, axis=0)),
                   jnp.where(sub == edge, 0.0, pltpu.roll(ti + e[2 * CHAINS - 1], shift, axis=0)))
            entering = [cur]
            for j in range(1, CHAINS):
                tr, ti = _cmul(pr, pi, *cur)
                cur = (tr + e[2 * j - 2], ti + e[2 * j - 1])
                entering.append(cur)

    def step(q, carry, last):
        out, acc = [], carry[2 * CHAINS:]
        for j in range(CHAINS):
            rr = index(j * rc + q)
            sl = _rows8(rr)
            pr, pi = _cmul(ar, ai, carry[2 * j], carry[2 * j + 1])
            nr, ni = pr + xr_ref[sl, :], pi + xi_ref[sl, :]
            xr_ref[sl, :] = nr
            xi_ref[sl, :] = ni
            if visit:
                acc = visit(rr, nr, ni, acc, last and j == CHAINS - 1)
            out += [nr, ni]
        return (*out, *acc)

    def step_block(qb, carry):
        for t in range(SCAN_UNROLL):
            carry = step(qb * SCAN_UNROLL + t, carry, False)
        return carry

    start = tuple(v for pair in entering for v in pair)
    n_blocks = (rc - 1) // SCAN_UNROLL
    carry = lax.fori_loop(0, n_blocks, step_block, (*start, *visit_init))
    for q in range(n_blocks * SCAN_UNROLL, rc - 1):
        carry = step(q, carry, False)
    carry = step(rc - 1, carry, True)
    return entering, carry[2 * CHAINS:]


def _discretise(a_re, a_im, ldt):
    lr = jnp.minimum(a_re, -1e-4)
    li = a_im
    dt = jnp.exp(ldt)
    mag = jnp.exp(lr * dt)
    lbr = mag * jnp.cos(li * dt)
    lbi = mag * jnp.sin(li * dt)
    den = lr * lr + li * li
    nr = lbr - 1.0
    fr = (nr * lr + lbi * li) / den
    fi = (lbi * lr - nr * li) / den
    return lr, li, dt, lbr, lbi, fr, fi


def _lane_row(v):
    return jnp.concatenate([v[g:g + 1, :] for g in range(v.shape[0])], axis=1)


def _ssm_fill_maps(d, prm, tmp_ref, maps):
    a_re_ref, a_im_ref, ldt_ref, bt_re_ref, bt_im_ref, c_re_ref, c_im_ref = prm
    _, _, _, lbr, lbi, fr, fi = _discretise(a_re_ref[d], a_im_ref[d], ldt_ref[d])

    def fill(dst, piece):
        tmp_ref[...] = jnp.zeros_like(tmp_ref)
        for g in range(SLAB_G):
            tmp_ref[g * SSM_H:(g + 1) * SSM_H, g * SSM_P:(g + 1) * SSM_P] = piece(g)
        dst[...] = tmp_ref[...].astype(BF16)

    wbr, wbi, wcr, wci = maps
    fill(wbr, lambda g: fr[g:g + 1] * bt_re_ref[d, g] - fi[g:g + 1] * bt_im_ref[d, g])
    fill(wbi, lambda g: fr[g:g + 1] * bt_im_ref[d, g] + fi[g:g + 1] * bt_re_ref[d, g])
    fill(wcr, lambda g: c_re_ref[d, g])
    fill(wci, lambda g: c_im_ref[d, g])
    return _lane_row(lbr), _lane_row(lbi)


def _ssm_param_specs():
    pole = pl.BlockSpec((2, SLAB_G, SSM_P), lambda j: (0, j, 0))
    step = pl.BlockSpec((2, SLAB_G, 1), lambda j: (0, j, 0))
    mat = pl.BlockSpec((2, SLAB_G, SSM_H, SSM_P), lambda j: (0, j, 0, 0))
    return [pole, pole, step, mat, mat, mat, mat]


_MAP_SCRATCH = [pltpu.VMEM((SLAB, SLAB_S), F32)] + [pltpu.VMEM((SLAB, SLAB_S), BF16)] * 4
_ENT_SPEC = pl.BlockSpec((1, 2, 2 * CHAINS, SEG, SLAB_S), lambda j: (j, 0, 0, 0, 0))
_NT = (((1,), (1,)), ((), ()))


def _ssm_fwd(u, prm, dskip):
    L, C = u.shape
    n_rows = L // SEG
    tc = _pick(L, (2048, 1024, 512, 256))
    u_spec = pl.BlockSpec((L, SLAB), lambda j: (0, j))
    d_spec = pl.BlockSpec((1, SLAB), lambda j: (0, j))

    def body(u_ref, *rest):
        prm_refs, d_ref, y_ref, ent_ref = rest[:7], rest[7], rest[8], rest[9]
        tmp_ref, maps, xr_ref, xi_ref = rest[10], rest[11:15], rest[15], rest[16]
        wbr, wbi, wcr, wci = maps
        y_ref[...] = u_ref[...] * d_ref[...]
        for d in range(2):
            lam_r, lam_i = _ssm_fill_maps(d, prm_refs, tmp_ref, maps)

            def inp(c, _):
                sl = pl.ds(pl.multiple_of(c * tc, tc), tc)
                ub = u_ref[sl, :].astype(BF16)
                xr_ref[sl, :] = jnp.dot(ub, wbr[...], preferred_element_type=F32)
                xi_ref[sl, :] = jnp.dot(ub, wbi[...], preferred_element_type=F32)
                return 0

            lax.fori_loop(0, L // tc, inp, 0)
            ar = jnp.broadcast_to(lam_r, (SEG, SLAB_S))
            ai = jnp.broadcast_to(lam_i, (SEG, SLAB_S))
            entering, _ = _seg_scan(xr_ref, xi_ref, ar, ai, d == 1, n_rows)
            for j, (er, ei) in enumerate(entering):
                ent_ref[0, d, 2 * j] = er
                ent_ref[0, d, 2 * j + 1] = ei

            def outp(c, _):
                sl = pl.ds(pl.multiple_of(c * tc, tc), tc)
                y_ref[sl, :] += (
                    lax.dot_general(xr_ref[sl, :].astype(BF16), wcr[...], _NT, preferred_element_type=F32)
                    - lax.dot_general(xi_ref[sl, :].astype(BF16), wci[...], _NT, preferred_element_type=F32))
                return 0

            lax.fori_loop(0, L // tc, outp, 0)

    return pl.pallas_call(
        body, name="ssm_fwd", grid=(C // SLAB,),
        in_specs=[u_spec] + _ssm_param_specs() + [d_spec],
        out_specs=[u_spec, _ENT_SPEC],
        out_shape=[jax.ShapeDtypeStruct((L, C), F32),
                   jax.ShapeDtypeStruct((C // SLAB, 2, 2 * CHAINS, SEG, SLAB_S), F32)],
        scratch_shapes=_MAP_SCRATCH + [pltpu.VMEM((L, SLAB_S), F32)] * 2,
        compiler_params=_cparams(("arbitrary",)),
    )(u, *prm, dskip)


def _ssm_bwd(u, dy, ent, prm, dskip):
    L, C = u.shape
    n_rows = L // SEG
    n_slab = C // SLAB
    tc = _pick(L, (2048, 1024, 512, 256))
    u_spec = pl.BlockSpec((L, SLAB), lambda j: (0, j))
    d_spec = pl.BlockSpec((1, SLAB), lambda j: (0, j))
    pg_spec = pl.BlockSpec((1, 2, PG_ROWS, SLAB_S), lambda j: (j, 0, 0, 0))

    def body(u_ref, dy_ref, ent_ref, *rest):
        prm_refs, d_ref, du_ref, pg_ref = rest[:7], rest[7], rest[8], rest[9]
        tmp_ref, maps, acc_ref = rest[10], rest[11:15], rest[15]
        xr_ref, xi_ref, gr_ref, gi_ref = rest[16:20]
        wbr, wbi, wcr, wci = maps
        du_ref[...] = dy_ref[...] * d_ref[...]
        pg_ref[...] = jnp.zeros_like(pg_ref)
        pg_ref[0, 0, 66:67, 0:SLAB] = _colsum(dy_ref[...] * u_ref[...])
        for d in range(2):
            lam_r, lam_i = _ssm_fill_maps(d, prm_refs, tmp_ref, maps)

            def inp(c, _):
                sl = pl.ds(pl.multiple_of(c * tc, tc), tc)
                ub = u_ref[sl, :].astype(BF16)
                dyb = dy_ref[sl, :].astype(BF16)
                xr_ref[sl, :] = jnp.dot(ub, wbr[...], preferred_element_type=F32)
                xi_ref[sl, :] = jnp.dot(ub, wbi[...], preferred_element_type=F32)
                gr_ref[sl, :] = jnp.dot(dyb, wcr[...], preferred_element_type=F32)
                gi_ref[sl, :] = -jnp.dot(dyb, wci[...], preferred_element_type=F32)
                return 0

            lax.fori_loop(0, L // tc, inp, 0)
            ar = jnp.broadcast_to(lam_r, (SEG, SLAB_S))
            ai = jnp.broadcast_to(lam_i, (SEG, SLAB_S))
            entering = [(ent_ref[0, d, 2 * j], ent_ref[0, d, 2 * j + 1]) for j in range(CHAINS)]
            _seg_scan(xr_ref, xi_ref, ar, ai, d == 1, n_rows, entering=entering)

            def pole(rr, lr, li, acc, last):
                if last:
                    pr, pi = entering[0]
                else:
                    nb = _rows8(rr + 1 if d == 1 else rr - 1)
                    pr, pi = xr_ref[nb, :], xi_ref[nb, :]
                return acc[0] + lr * pr + li * pi, acc[1] + li * pr - lr * pi

            zero = jnp.zeros((SEG, SLAB_S), F32)
            _, (accr, acci) = _seg_scan(gr_ref, gi_ref, ar, -ai, d == 0, n_rows, pole, (zero, zero))
            pg_ref[0, d, 64:65, :] = _colsum(accr)
            pg_ref[0, d, 65:66, :] = _colsum(acci)

            acc_ref[...] = jnp.zeros_like(acc_ref)

            def outp(c, _):
                sl = pl.ds(pl.multiple_of(c * tc, tc), tc)
                lrb, lib = gr_ref[sl, :].astype(BF16), gi_ref[sl, :].astype(BF16)
                du_ref[sl, :] += (lax.dot_general(lrb, wbr[...], _NT, preferred_element_type=F32)
                                  + lax.dot_general(lib, wbi[...], _NT, preferred_element_type=F32))
                ut = u_ref[sl, :].astype(F32).T.astype(BF16)
                dyt = dy_ref[sl, :].T.astype(BF16)
                acc_ref[0] += jnp.dot(ut, lrb, preferred_element_type=F32)
                acc_ref[1] += jnp.dot(ut, lib, preferred_element_type=F32)
                acc_ref[2] += jnp.dot(dyt, xr_ref[sl, :].astype(BF16), preferred_element_type=F32)
                acc_ref[3] -= jnp.dot(dyt, xi_ref[sl, :].astype(BF16), preferred_element_type=F32)
                return 0

            lax.fori_loop(0, L // tc, outp, 0)
            for m in range(4):
                for g in range(SLAB_G):
                    lanes = slice(g * SSM_P, (g + 1) * SSM_P)
                    pg_ref[0, d, m * SSM_H:(m + 1) * SSM_H, lanes] = acc_ref[m, g * SSM_H:(g + 1) * SSM_H, lanes]

    return pl.pallas_call(
        body, name="ssm_bwd", grid=(n_slab,),
        in_specs=[u_spec, u_spec, _ENT_SPEC] + _ssm_param_specs() + [d_spec],
        out_specs=[u_spec, pg_spec],
        out_shape=[jax.ShapeDtypeStruct((L, C), F32), jax.ShapeDtypeStruct((n_slab, 2, PG_ROWS, SLAB_S), F32)],
        scratch_shapes=_MAP_SCRATCH + [pltpu.VMEM((4, SLAB, SLAB_S), F32)] + [pltpu.VMEM((L, SLAB_S), F32)] * 4,
        compiler_params=_cparams(("arbitrary",), 60 << 20),
    )(u, dy, ent, *prm, dskip)


def _ssm_param_grads(pg, prm):
    n_slab = pg.shape[0]
    G = n_slab * SLAB_G
    pg_spec = pl.BlockSpec((1, 2, PG_ROWS, SLAB_S), lambda j: (j, 0, 0, 0))
    pole, _, step, mat = _ssm_param_specs()[:4]

    def body(pg_ref, a_re_ref, a_im_ref, ldt_ref, bt_re_ref, bt_im_ref,
             dbr_ref, dbi_ref, dcr_ref, dci_ref, dar_ref, dai_ref, dldt_ref, dd_ref):
        dd_ref[...] = pg_ref[0, 0, 66:67, 0:SLAB]
        for d in range(2):
            a_r = a_re_ref[d]
            lr, li, dt, lbr, lbi, f_r, f_i = _discretise(a_r, a_im_ref[d], ldt_ref[d])
            gfr_rows, gfi_rows, glr_rows, gli_rows = [], [], [], []
            for g in range(SLAB_G):
                lanes = slice(g * SSM_P, (g + 1) * SSM_P)
                gbr, gbi = pg_ref[0, d, 0:SSM_H, lanes], pg_ref[0, d, SSM_H:2 * SSM_H, lanes]
                b_r, b_i = bt_re_ref[d, g], bt_im_ref[d, g]
                fr, fi = f_r[g:g + 1], f_i[g:g + 1]
                dbr_ref[d, g] = fr * gbr + fi * gbi
                dbi_ref[d, g] = fr * gbi - fi * gbr
                gfr_rows.append(_colsum(gbr * b_r + gbi * b_i))
                gfi_rows.append(_colsum(gbi * b_r - gbr * b_i))
                dcr_ref[d, g] = pg_ref[0, d, 2 * SSM_H:3 * SSM_H, lanes]
                dci_ref[d, g] = pg_ref[0, d, 3 * SSM_H:4 * SSM_H, lanes]
                glr_rows.append(pg_ref[0, d, 64:65, lanes])
                gli_rows.append(pg_ref[0, d, 65:66, lanes])
            gfr, gfi = jnp.concatenate(gfr_rows, axis=0), jnp.concatenate(gfi_rows, axis=0)
            glr, gli = jnp.concatenate(glr_rows, axis=0), jnp.concatenate(gli_rows, axis=0)
            den = lr * lr + li * li
            ir, ii = lr / den, -li / den
            tr, ti = _cmul(ir, -ii, gfr, gfi)
            glbr, glbi = glr + tr, gli + ti
            qr, qi = _cmul(f_r, f_i, ir, ii)
            dlr, dli = _cmul(-qr, qi, gfr, gfi)
            zr, zi = _cmul(lbr, -lbi, glbr, glbi)
            dlr = dlr + dt * zr
            dli = dli + dt * zi
            dar_ref[d] = jnp.where(a_r < -1e-4, dlr, jnp.where(a_r == -1e-4, 0.5 * dlr, 0.0))
            dai_ref[d] = dli
            dldt_ref[d] = jnp.sum(lr * zr + li * zi, axis=-1, keepdims=True) * dt

    a_re, a_im, ldt, bt_re, bt_im = prm[:5]
    mshape = jax.ShapeDtypeStruct(bt_re.shape, F32)
    pshape = jax.ShapeDtypeStruct(a_re.shape, F32)
    return pl.pallas_call(
        body, name="ssm_param_grads", grid=(n_slab,),
        in_specs=[pg_spec, pole, pole, step, mat, mat],
        out_specs=[mat, mat, mat, mat, pole, pole, step, pl.BlockSpec((1, SLAB), lambda j: (0, j))],
        out_shape=[mshape, mshape, mshape, mshape, pshape, pshape, jax.ShapeDtypeStruct(ldt.shape, F32),
                   jax.ShapeDtypeStruct((1, n_slab * SLAB), F32)],
        compiler_params=_cparams(("arbitrary",)),
    )(pg, a_re, a_im, ldt, bt_re, bt_im)


def _peer(k, x, y, c):
    return (1 - x if k & 4 else x, 1 - y if k & 2 else y, 1 - c if k & 1 else c)


def _dev_index(pos):
    return 4 * pos[0] + 2 * pos[1] + pos[2]


def _chip_index(pos):
    return 2 * pos[0] + pos[1]


def _sibling_swap(x, name):
    any_spec = pl.BlockSpec(memory_space=pl.ANY)

    def body(x_ref, out_ref, send_sems, recv_sems):
        x_, y_, c_ = lax.axis_index("x"), lax.axis_index("y"), lax.axis_index("c")
        copies = [pltpu.make_async_remote_copy(
            src_ref=x_ref.at[2 * chip + (1 - c_)], dst_ref=out_ref.at[chip], send_sem=send_sems.at[chip],
            recv_sem=recv_sems.at[chip], device_id=(x_, y_, 1 - c_), device_id_type=pl.DeviceIdType.MESH)
            for chip in range(N_CHIPS)]
        for cp in copies:
            cp.start()
        for cp in copies:
            cp.wait()

    return pl.pallas_call(
        body, name=name, out_shape=jax.ShapeDtypeStruct((N_CHIPS,) + x.shape[1:], x.dtype),
        in_specs=[any_spec], out_specs=any_spec,
        scratch_shapes=[pltpu.SemaphoreType.DMA((N_CHIPS,)), pltpu.SemaphoreType.DMA((N_CHIPS,))],
    )(x)


def _pair_sum(x, got, name):
    n, R, W = got.shape
    tr = _row_tile(R, W, 5 << 20)
    core = lax.axis_index("c").astype(jnp.int32).reshape(1)

    def body(core_ref, a_ref, b_ref, o_ref):
        o_ref[...] = (a_ref[...].astype(F32) + b_ref[...].astype(F32)).astype(BF16)

    spec = pl.BlockSpec((1, tr, W), lambda i, j, c: (i, j, 0))
    grid_spec = pltpu.PrefetchScalarGridSpec(
        num_scalar_prefetch=1, grid=(n, R // tr),
        in_specs=[pl.BlockSpec((1, tr, W), lambda i, j, c: (2 * i + c[0], j, 0)), spec], out_specs=spec)
    return pl.pallas_call(
        body, name=name, grid_spec=grid_spec, out_shape=jax.ShapeDtypeStruct((n, R, W), BF16),
        compiler_params=_cparams(("arbitrary", "arbitrary")),
    )(core, x, got)


def _all_gather(xs, name):
    n = len(xs)
    any_spec = pl.BlockSpec(memory_space=pl.ANY)

    def body(*refs):
        x_refs, out_refs = refs[:n], refs[n:2 * n]
        send_sems, recv_sems, local_sems = refs[2 * n:]
        x, y, c = lax.axis_index("x"), lax.axis_index("y"), lax.axis_index("c")
        me, sibling = (x, y, c), (x, y, 1 - c)
        chips = [(1 - x, y), (x, 1 - y), (1 - x, 1 - y)]

        def copy(a, k, block, to, src=None):
            dst = out_refs[a].at[_dev_index(block)]
            return pltpu.make_async_remote_copy(
                src_ref=dst if src is None else src, dst_ref=dst,
                send_sem=send_sems.at[a, k], recv_sem=recv_sems.at[a, k],
                device_id=to, device_id_type=pl.DeviceIdType.MESH)

        mine = [pltpu.make_async_copy(x_refs[a], out_refs[a].at[_dev_index(me)], local_sems.at[a]) for a in range(n)]
        for cp in mine:
            cp.start()
        first = []
        for a in range(n):
            first.append(copy(a, 0, me, sibling, src=x_refs[a]))
            first += [copy(a, 1 + j, me, (*chip, c), src=x_refs[a]) for j, chip in enumerate(chips)]
        for cp in first:
            cp.start()
        passed = []
        for j, chip in enumerate(chips):
            for a in range(n):
                copy(a, 1 + j, (*chip, c), me).wait_recv()
                fwd = copy(a, 4 + j, (*chip, c), sibling)
                fwd.start()
                passed.append(fwd)
        for a in range(n):
            copy(a, 0, sibling, me).wait_recv()
            for j, chip in enumerate(chips):
                copy(a, 4 + j, (*chip, 1 - c), me).wait_recv()
        for cp in first + passed:
            cp.wait_send()
        for cp in mine:
            cp.wait()

    return pl.pallas_call(
        body, name=name,
        out_shape=[jax.ShapeDtypeStruct((N_DEV,) + v.shape, v.dtype) for v in xs],
        in_specs=[any_spec] * n, out_specs=[any_spec] * n,
        scratch_shapes=[pltpu.SemaphoreType.DMA((n, 7)), pltpu.SemaphoreType.DMA((n, 7)),
                        pltpu.SemaphoreType.DMA((n,))],
    )(*xs)


def _sum_blocks(x, name):
    _, R, W = x.shape

    def body(x_ref, o_ref):
        acc = x_ref[0].astype(F32)
        for d in range(1, N_DEV):
            acc = acc + x_ref[d].astype(F32)
        o_ref[...] = acc

    return pl.pallas_call(body, name=name, out_shape=jax.ShapeDtypeStruct((R, W), F32),
                          compiler_params=pltpu.CompilerParams(vmem_limit_bytes=VMEM_LIMIT))(x)


def _adam_update(w, g, m, v):
    mn = ADAM_B1 * m + (1.0 - ADAM_B1) * g
    vn = ADAM_B2 * v + (1.0 - ADAM_B2) * (g * g)
    m_hat = mn / (1.0 - ADAM_B1 ** ADAM_STEP)
    v_hat = vn / (1.0 - ADAM_B2 ** ADAM_STEP)
    return -ADAM_LR * (m_hat / (jnp.sqrt(v_hat) + ADAM_EPS) + ADAM_WD * w), mn, vn


def _row_tile(R, W, budget):
    padded = -(-W // LANES) * LANES * 4
    if R * padded <= budget:
        return R
    return _pick(R, [t for t in (2048, 1024, 512, 256, 128, 64, 32, 16, 8) if t * padded <= budget])


def _adamw(w, g, m, v, name):
    R, W = w.shape
    tr = _row_tile(R, W, 1 << 20)

    def body(w_ref, g_ref, m_ref, v_ref, d_ref, mo_ref, vo_ref):
        d_ref[...], mo_ref[...], vo_ref[...] = _adam_update(w_ref[...], g_ref[...], m_ref[...], v_ref[...])

    spec = pl.BlockSpec((tr, W), lambda i: (i, 0))
    shp = jax.ShapeDtypeStruct((R, W), F32)
    return pl.pallas_call(
        body, name=name, grid=(R // tr,), in_specs=[spec] * 4, out_specs=[spec] * 3, out_shape=[shp] * 3,
        compiler_params=_cparams(("arbitrary",)),
    )(w, g, m, v)


def _adamw_group(ws, gs, ms, vs, name):
    n = len(ws)

    def body(*refs):
        ins, outs = refs[:4 * n], refs[4 * n:]
        for i in range(n):
            w_ref, g_ref, m_ref, v_ref = ins[i], ins[n + i], ins[2 * n + i], ins[3 * n + i]
            outs[i][...], outs[n + i][...], outs[2 * n + i][...] = _adam_update(
                w_ref[...], g_ref[...], m_ref[...], v_ref[...])

    shapes = [jax.ShapeDtypeStruct(w.shape, F32) for w in ws]
    outs = pl.pallas_call(body, name=name, out_shape=shapes * 3)(*ws, *gs, *ms, *vs)
    return outs[:n], outs[n:2 * n], outs[2 * n:]


def _adamw_reduce(w, land, m, v, name):
    R, W = w.shape
    n = land.shape[0]
    tr = _row_tile(R, W, 1 << 20)

    def body(w_ref, l_ref, m_ref, v_ref, g_ref, d_ref, mo_ref, vo_ref):
        g = l_ref[0].astype(F32)
        for d in range(1, n):
            g = g + l_ref[d].astype(F32)
        g_ref[...] = g
        d_ref[...], mo_ref[...], vo_ref[...] = _adam_update(w_ref[...], g, m_ref[...], v_ref[...])

    spec = pl.BlockSpec((tr, W), lambda i: (i, 0))
    lspec = pl.BlockSpec((n, tr, W), lambda i: (0, i, 0))
    shp = jax.ShapeDtypeStruct((R, W), F32)
    return pl.pallas_call(
        body, name=name, grid=(R // tr,), in_specs=[spec, lspec, spec, spec], out_specs=[spec] * 4,
        out_shape=[shp] * 4, compiler_params=_cparams(("arbitrary",)),
    )(w, land, m, v)


def _gelu(v):
    c = math.sqrt(2.0 / math.pi)
    return 0.5 * v * (1.0 + jnp.tanh(c * (v + 0.044715 * v * v * v)))


def _gelu_grad(v):
    c = math.sqrt(2.0 / math.pi)
    t = jnp.tanh(c * (v + 0.044715 * v * v * v))
    return 0.5 * (1.0 + t) + 0.5 * v * (1.0 - t * t) * c * (1.0 + 3.0 * 0.044715 * v * v)


def kernel(x, p, norm_mix, w_in, q_norm, k_norm, ssm_a_re, ssm_a_im, ssm_log_dt, ssm_b_re, ssm_b_im, ssm_c_re, ssm_c_im, ssm_d, w_glu, b_glu, w_out, norm_ple, w_ple_gate, w_ple_proj, norm_final, loss_target, m_norm_mix, m_w_in, m_q_norm, m_k_norm, m_ssm_a_re, m_ssm_a_im, m_ssm_log_dt, m_ssm_b_re, m_ssm_b_im, m_ssm_c_re, m_ssm_c_im, m_ssm_d, m_w_glu, m_b_glu, m_w_out, m_norm_ple, m_w_ple_gate, m_w_ple_proj, m_norm_final, v_norm_mix, v_w_in, v_q_norm, v_k_norm, v_ssm_a_re, v_ssm_a_im, v_ssm_log_dt, v_ssm_b_re, v_ssm_b_im, v_ssm_c_re, v_ssm_c_im, v_ssm_d, v_w_glu, v_b_glu, v_w_out, v_norm_ple, v_w_ple_gate, v_w_ple_proj, v_norm_final):
    L, D = x.shape[1], x.shape[2]
    D_SSM = ssm_d.shape[1]
    G = D_SSM // SSM_H
    n_slab = D_SSM // SLAB
    n_in = w_in.shape[2]
    D_IN = n_in * N_DEV
    n_pp = w_ple_proj.shape[2]
    n_glu = w_glu.shape[2]
    xs = x[0]
    ps = p[0, 0]
    tgt = loss_target[0]

    (win_t3,) = _all_gather([w_in[0].T.astype(BF16)], "gather_w_in")
    win_t = win_t3.reshape(D_IN, D)
    later_weights = _Carry("gather", [w_glu[0].astype(BF16), w_out[0].astype(BF16), w_ple_gate[0].astype(BF16),
                                      w_ple_proj[0].astype(BF16)])

    ssm_prm = (ssm_a_re[0], ssm_a_im[0], ssm_log_dt[0].reshape(2, G, 1),
               ssm_b_re[0].transpose(0, 1, 3, 2), ssm_b_im[0].transpose(0, 1, 3, 2), ssm_c_re[0], ssm_c_im[0])

    cos, sin = _rope_tables(L)
    hn = _norm_in(xs, norm_mix, "norm_mix")
    ZT = 512
    zp_tile = lambda j: jnp.where(j < 2, j, jnp.where(j < D_IN // ZT - 1, j + 1, 2))
    z = _mm(hn, win_t, "nt", "in_proj", out_dtype=BF16, n_tiles=(ZT, zp_tile))
    qr, kr, vb, kt = _qkv_prep(z, cos, sin, q_norm, k_norm)
    o, lse, (wglu3, wout3, wpg3, wpp3) = _attn_fwd(qr, kr, vb, later_weights)
    wout = wout3.reshape(-1, D)
    wpg = wpg3.reshape(-1, D)
    u_off = 2 * D_ATTN
    u_perm = _seg_perm(z[:, u_off:u_off + D_SSM])
    ys_perm, ssm_ent = _ssm_fwd(u_perm, ssm_prm, ssm_d)
    ys = _seg_unperm(ys_perm)

    tm = _pick(L, (256,))
    wglu = wglu3.transpose(1, 0, 2).reshape(D_SSM, 2 * D_SSM)

    def mix_post(prod, gy_tile, o_ref, ga_ref, gs_ref, b_ref):
        glu_b = (prod + b_ref[...]).astype(BF16)
        gluv = glu_b.astype(F32)
        ga, gs = _f32(ga_ref), _f32(gs_ref)
        y_attn = _f32(o_ref) * ga * _sigmoid(ga)
        y_ssm = gluv[:, :D_SSM] * _sigmoid(gluv[:, D_SSM:]) * gs * _sigmoid(gs)
        return gy_tile, glu_b, jnp.concatenate([y_attn, y_ssm], axis=-1)

    gy, glu, cat = _mm_rows(ys, wglu, "nn", "mix", mix_post, [o, (z, D_ATTN, 1), (z, D_SSM, 3)], [b_glu],
                            [(D_SSM, BF16), (2 * D_SSM, BF16), (D_ATTN + D_SSM, BF16)],
                            pre=lambda y: _gelu(y).astype(BF16))

    def out_post(prod, x_ref, g_ref):
        h1v = prod + x_ref[...]
        return h1v, h1v * _rms(h1v) * g_ref[...]

    h1, n2 = _mm_rows(cat, wout, "nn", "out_proj", out_post, [xs], [norm_ple], [(D, F32), (D, BF16)])
    pb = ps.astype(BF16)
    pp = _mm(pb, wpp3, "nn", "ple_proj", out_dtype=BF16, b_blk=True)

    nf = norm_final.reshape(1, D)

    def tail_post(gp, h1_ref, pp_ref, t_ref, g_ref):
        gate = _sigmoid(gp)
        ppv = _f32(pp_ref)
        h2 = h1_ref[...] + gate * ppv
        r = _rms(h2)
        hh = h2 * r
        err = hh * g_ref[...] - t_ref[...]
        loss_part = jnp.broadcast_to(0.5 * jnp.sum(jnp.mean(err * err, axis=-1, keepdims=True)), (1, LANES))
        dy = err * (1.0 / D)
        dh2 = _rms_bwd(dy, hh, r, g_ref[...])
        return dh2, dh2 * gate, dh2 * ppv * gate * (1.0 - gate), loss_part, _colsum(dy * hh)

    dh2, dpp, dsg, loss_acc, d_nf = _mm_rows(n2, wpg, "nn", "tail", tail_post, [h1, pp, tgt], [nf],
                                             [(D, F32), (D, BF16), (D, BF16)], [(1, LANES), (1, D)], vmem=58 << 20)

    g_wpp3 = _mm(pb, dpp, "tn", "d_ple_proj", out_dtype=BF16, out_blk=n_pp)
    g_wpg = _mm(n2, dsg, "tn", "d_ple_gate", out_dtype=BF16)
    def ple_bwd_post(dn, h1_ref, dh2_ref, g_ref):
        h1v = h1_ref[...]
        r = _rms(h1v)
        hh = h1v * r
        dh1 = dh2_ref[...] + _rms_bwd(dn, hh, r, g_ref[...])
        return dh1, dh1, _colsum(dn * hh)

    dh1, dh1b, d_nple = _mm_rows(dsg, wpg, "nt", "ple_bwd", ple_bwd_post, [h1, dh2], [norm_ple],
                                 [(D, F32), (D, BF16)], [(1, D)])

    g_wout = _mm(cat, dh1b, "tn", "d_out_proj", out_dtype=BF16)

    def mix_bwd_post(dcat, o_ref, ga_ref, glu_ref, gs_ref):
        dca, dcs = dcat[:, :D_ATTN], dcat[:, D_ATTN:]
        ga, gs = _f32(ga_ref), _f32(gs_ref)
        gla, glb = glu_ref[:, :D_SSM].astype(F32), glu_ref[:, D_SSM:].astype(F32)
        sa, ss, sb = _sigmoid(ga), _sigmoid(gs), _sigmoid(glb)
        do = dca * ga * sa
        dga = dca * _f32(o_ref) * sa * (1.0 + ga * (1.0 - sa))
        dgs = dcs * gla * sb * ss * (1.0 + gs * (1.0 - ss))
        dy2 = dcs * gs * ss
        da, db = dy2 * sb, dy2 * gla * sb * (1.0 - sb)
        return (do, dga, dgs, jnp.concatenate([da, db], axis=-1),
                jnp.concatenate([_colsum(da), _colsum(db)], axis=-1))

    do, dga, dgs, dglu, g_bglu = _mm_rows(
        dh1b, wout, "nt", "mix_bwd", mix_bwd_post, [o, (z, D_ATTN, 1), glu, (z, D_SSM, 3)], [],
        [(D_ATTN, BF16), (D_ATTN, BF16), (D_SSM, BF16), (2 * D_SSM, BF16)], [(1, 2 * D_SSM)])

    g_wglu3 = _mm(gy, dglu, "tn", "d_glu_proj", out_dtype=BF16, out_blk=n_glu)
    dys = _mm(dglu, wglu3, "nt", "d_ssm_out", b_blk=True,
              post=(lambda out, y: out * _gelu_grad(y), ys))
    du_perm, pg = _ssm_bwd(u_perm, _seg_perm(dys), ssm_ent, ssm_prm, ssm_d)
    du = _seg_unperm(du_perm)

    pg_send = pg.reshape(N_DEV, (n_slab // N_DEV) * 2 * PG_ROWS, SLAB_S)
    dqs, dkr, dvv, (l_wglu, l_wout, l_wpg, l_wpp, l_pg) = _attn_bwd(
        qr, kr, vb, kt, do, o, lse,
        _Carry("a2a", [g_wglu3, g_wout.reshape(N_DEV, -1, D), g_wpg.reshape(N_DEV, -1, D), g_wpp3, pg_send]))

    scale = HEAD_DIM ** -0.5
    kblk = 4 * D_ATTN // D_KV

    a0, k0, v0, u0, s0 = D_ATTN + 2 * D_KV, D_ATTN, D_ATTN + D_KV, 2 * D_ATTN + 2 * D_KV, 2 * D_ATTN + 2 * D_KV + D_SSM

    def qkv_bwd_body(dq_ref, dk_ref, dv_ref, q_ref, k_ref, cos_ref, sin_ref, qn_ref, kn_ref, dga_ref, du_ref, dgs_ref,
                     dz_ref, dqn_ref, dkn_ref):
        c, s = cos_ref[...], sin_ref[...]
        dz_ref[:, a0:a0 + D_ATTN] = dga_ref[...]
        dz_ref[:, u0:u0 + D_SSM] = du_ref[...].astype(BF16)
        dz_ref[:, s0:s0 + D_SSM] = dgs_ref[...]

        def head(g, xh, w):
            dn = g * c + _partner(g * s)
            r = _rms(xh)
            xhat = xh * r
            return _rms_bwd(dn, xhat, r, w), _colsum(dn * xhat)

        dqn = jnp.zeros((1, HEAD_DIM), F32)
        for h in range(N_HEADS):
            sl = slice(h * HEAD_DIM, (h + 1) * HEAD_DIM)
            dx, dw = head(dq_ref[:, sl] * scale, q_ref[:, sl].astype(F32), qn_ref[...])
            dz_ref[:, sl] = dx.astype(BF16)
            dqn = dqn + dw
        dkn = jnp.zeros((1, HEAD_DIM), F32)
        for h in range(N_KV):
            sl = slice(h * HEAD_DIM, (h + 1) * HEAD_DIM)
            dx, dw = head(dk_ref[:, sl], k_ref[:, sl].astype(F32), kn_ref[...])
            dz_ref[:, k0 + h * HEAD_DIM:k0 + (h + 1) * HEAD_DIM] = dx.astype(BF16)
            dkn = dkn + dw
        dz_ref[:, v0:v0 + D_KV] = dv_ref[...].astype(BF16)
        _acc(dqn_ref, dqn)
        _acc(dkn_ref, dkn)

    dz, g_qn, g_kn = _rowcall(
        qkv_bwd_body, "qkv_bwd", L, tm,
        [(dqs, _rspec(tm, D_ATTN)), (dkr, _rspec(tm, D_KV)), (dvv, _rspec(tm, D_KV)),
         (z, _rspec(tm, D_ATTN, 0)), (z, _rspec(tm, D_KV, kblk)), (cos, _rspec(tm, HEAD_DIM)),
         (sin, _rspec(tm, HEAD_DIM)), (q_norm, _fspec(q_norm.shape)), (k_norm, _fspec(k_norm.shape)),
         (dga, _rspec(tm, D_ATTN)), (du, _rspec(tm, D_SSM)), (dgs, _rspec(tm, D_SSM))],
        [(D_IN, BF16)], [(1, HEAD_DIM), (1, HEAD_DIM)])

    g_win_t = _mm(dz, hn, "tn", "d_in_proj", out_dtype=BF16)
    g_win8 = g_win_t.reshape(N_DEV, n_in, D)
    from_sibling = _sibling_swap(g_win8, "swap_d_w_in")
    pair = _pair_sum(g_win8, from_sibling, "pair_sum_d_w_in")
    dhn, (l_win_t,) = _mm(dz, win_t, "nn", "d_norm_mix_in", out_dtype=BF16,
                          carry=_Carry("a2a_chips", [pair]))

    def in_bwd_body(x_ref, dn_ref, dh1_ref, g_ref, dx_ref, dg_ref):
        xv = x_ref[...]
        r = _rms(xv)
        hh = xv * r
        dn = _f32(dn_ref)
        _acc(dg_ref, _colsum(dn * hh))
        dx_ref[...] = dh1_ref[...] + _rms_bwd(dn, hh, r, g_ref[...])

    grad_x, g_nmix = _rowcall(
        in_bwd_body, "in_bwd", L, tm,
        [(xs, _rspec(tm, D)), (dhn, _rspec(tm, D)), (dh1, _rspec(tm, D)), (norm_mix, _fspec(norm_mix.shape))],
        [(D, F32)], [(1, D)])

    tiny_parts = [g_nmix, g_bglu, d_nple, d_nf, g_qn, g_kn, loss_acc[:, :1]]
    tiny_flat = jnp.concatenate([t.reshape(-1) for t in tiny_parts])
    tiny_rows = -(-tiny_flat.shape[0] // (8 * LANES)) * 8
    tiny = jnp.pad(tiny_flat, (0, tiny_rows * LANES - tiny_flat.shape[0])).reshape(tiny_rows, LANES)
    pg_sum = _sum_blocks(l_pg, "sum_ssm_grads")
    pg_all, tiny_all = _all_gather([pg_sum, tiny], "gather_small_grads")
    (g_bt_re, g_bt_im, g_c_re, g_c_im, g_a_re, g_a_im, g_ldt, g_skip) = _ssm_param_grads(
        pg_all.reshape(n_slab, 2, PG_ROWS, SLAB_S), ssm_prm)
    tiny_sum = _sum_blocks(tiny_all, "sum_tiny_grads").reshape(-1)
    tiny_grads, off = [], 0
    for t in tiny_parts:
        tiny_grads.append(tiny_sum[off:off + t.size].reshape(t.shape))
        off += t.size
    r_nmix, r_bglu, r_nple, r_nf, r_qn, r_kn, loss = tiny_grads
    loss = loss.reshape(())

    grads, deltas, new_ms, new_vs = {}, {}, {}, {}
    outs = _adamw_reduce(w_in[0].T, l_win_t, m_w_in[0].T, v_w_in[0].T, "adamw_w_in")
    grads["w_in"], deltas["w_in"], new_ms["w_in"], new_vs["w_in"] = [t.T[None] for t in outs]
    big = [("w_glu", w_glu, l_wglu, m_w_glu, v_w_glu),
           ("w_out", w_out, l_wout, m_w_out, v_w_out), ("w_ple_gate", w_ple_gate, l_wpg, m_w_ple_gate, v_w_ple_gate),
           ("w_ple_proj", w_ple_proj, l_wpp, m_w_ple_proj, v_w_ple_proj)]
    for name, w, ld, m, v in big:
        shp = w.shape
        outs = _adamw_reduce(w[0], ld, m[0], v[0], "adamw_" + name)
        grads[name], deltas[name], new_ms[name], new_vs[name] = [t.reshape(shp) for t in outs]
    bt2 = (2 * G * SSM_H, SSM_P)
    for name, w, g, m, v in (("ssm_b_re", ssm_b_re, g_bt_re, m_ssm_b_re, v_ssm_b_re),
                             ("ssm_b_im", ssm_b_im, g_bt_im, m_ssm_b_im, v_ssm_b_im)):
        to2 = lambda t: t[0].transpose(0, 1, 3, 2).reshape(bt2)
        back = lambda t: t.reshape(2, G, SSM_H, SSM_P).transpose(0, 1, 3, 2)[None]
        outs = _adamw(to2(w), g.reshape(bt2), to2(m), to2(v), "adamw_" + name)
        grads[name] = back(g)
        deltas[name], new_ms[name], new_vs[name] = [back(t) for t in outs]
    small = [("norm_mix", norm_mix, r_nmix, m_norm_mix, v_norm_mix, (1, D)),
             ("q_norm", q_norm, r_qn, m_q_norm, v_q_norm, (1, HEAD_DIM)),
             ("k_norm", k_norm, r_kn, m_k_norm, v_k_norm, (1, HEAD_DIM)),
             ("ssm_a_re", ssm_a_re, g_a_re, m_ssm_a_re, v_ssm_a_re, (2 * G, SSM_P)),
             ("ssm_a_im", ssm_a_im, g_a_im, m_ssm_a_im, v_ssm_a_im, (2 * G, SSM_P)),
             ("ssm_log_dt", ssm_log_dt, g_ldt, m_ssm_log_dt, v_ssm_log_dt, (2, G)),
             ("ssm_c_re", ssm_c_re, g_c_re, m_ssm_c_re, v_ssm_c_re, (2 * G * SSM_H, SSM_P)),
             ("ssm_c_im", ssm_c_im, g_c_im, m_ssm_c_im, v_ssm_c_im, (2 * G * SSM_H, SSM_P)),
             ("ssm_d", ssm_d, g_skip, m_ssm_d, v_ssm_d, (1, D_SSM)),
             ("b_glu", b_glu, r_bglu, m_b_glu, v_b_glu, (1, 2 * D_SSM)),
             ("norm_ple", norm_ple, r_nple, m_norm_ple, v_norm_ple, (1, D)),
             ("norm_final", norm_final, r_nf, m_norm_final, v_norm_final, (1, D))]
    group = [it for it in small if it[5][0] * it[5][1] <= (1 << 14)]
    grouped = {it[0] for it in group}
    for name, w, g, m, v, s2 in small:
        if name in grouped:
            continue
        shp = w.shape
        outs = _adamw(w.reshape(s2), g.reshape(s2), m.reshape(s2), v.reshape(s2), "adamw_" + name)
        grads[name] = g.reshape(shp)
        deltas[name], new_ms[name], new_vs[name] = [t.reshape(shp) for t in outs]
    ds, mns, vns = _adamw_group(*[[it[i].reshape(it[5]) for it in group] for i in (1, 2, 3, 4)], "adamw_tiny")
    for (name, w, g, _, _, _), d_, m_, v_ in zip(group, ds, mns, vns):
        shp = w.shape
        grads[name] = g.reshape(shp)
        deltas[name], new_ms[name], new_vs[name] = d_.reshape(shp), m_.reshape(shp), v_.reshape(shp)

    order = ["norm_mix", "w_in", "q_norm", "k_norm", "ssm_a_re", "ssm_a_im", "ssm_log_dt", "ssm_b_re", "ssm_b_im",
             "ssm_c_re", "ssm_c_im", "ssm_d", "w_glu", "b_glu", "w_out", "norm_ple", "w_ple_gate", "w_ple_proj",
             "norm_final"]
    return (loss, grad_x[None], *[grads[k] for k in order], *[deltas[k] for k in order],
            *[new_ms[k] for k in order], *[new_vs[k] for k in order])
```

```python
import functools
import math

import numpy as np
import jax
import jax.numpy as jnp
from jax import lax
from jax.experimental import pallas as pl
from jax.experimental.pallas import tpu as pltpu

F32 = jnp.float32
BF16 = jnp.bfloat16

N_DEV = 8
N_CHIPS = 4
EPS = 1e-6
GRID_W = 64
ROPE_THETA = 10000.0
HEAD_DIM = 128
N_HEADS = 8
N_KV = 2
REP = N_HEADS // N_KV
D_ATTN = N_HEADS * HEAD_DIM
D_KV = N_KV * HEAD_DIM
SSM_H = 16
SSM_P = 64
SLAB = 128
SLAB_G = SLAB // SSM_H
SLAB_S = SLAB_G * SSM_P
SEG = 8
CHAINS = 2
SCAN_UNROLL = 8
LANES = 128
PG_ROWS = 72
VMEM_LIMIT = 48 << 20

ADAM_LR = 0.001
ADAM_B1 = 0.9
ADAM_B2 = 0.999
ADAM_EPS = 1e-08
ADAM_WD = 0.01
ADAM_STEP = 10


def _pick(n, cands):
    for c in cands:
        if n % c == 0:
            return c
    return n


def _cparams(sem, vmem=VMEM_LIMIT):
    return pltpu.CompilerParams(dimension_semantics=sem, vmem_limit_bytes=vmem)


class _Carry:
    def __init__(self, kind, xs):
        self.kind, self.xs, self.n = kind, list(xs), len(xs)
        self.ks = (2, 4, 6) if kind == "a2a_chips" else tuple(range(1, N_DEV))
        self.index = _chip_index if kind == "a2a_chips" else _dev_index
        lead = (N_DEV,) if kind == "gather" else ()
        self.out_shape = [jax.ShapeDtypeStruct(lead + v.shape, v.dtype) for v in xs]
        self.specs = [pl.BlockSpec(memory_space=pl.ANY)] * self.n
        self.scratch = [pltpu.SemaphoreType.DMA((self.n, len(self.ks))), pltpu.SemaphoreType.DMA((self.n, len(self.ks))),
                        pltpu.SemaphoreType.DMA((self.n,))]

    def _copies(self, x_refs, out_refs, sems):
        send_sems, recv_sems, local_sems = sems
        x, y, c = lax.axis_index("x"), lax.axis_index("y"), lax.axis_index("c")
        me = self.index((x, y, c))
        mine, sends, arrivals = [], [], []
        for a in range(self.n):
            src_mine = x_refs[a] if self.kind == "gather" else x_refs[a].at[me]
            mine.append(pltpu.make_async_copy(src_mine, out_refs[a].at[me], local_sems.at[a]))
            for s, k in enumerate(self.ks):
                peer = _peer(k, x, y, c)
                src = x_refs[a] if self.kind == "gather" else x_refs[a].at[self.index(peer)]
                sends.append(pltpu.make_async_remote_copy(
                    src_ref=src, dst_ref=out_refs[a].at[me], send_sem=send_sems.at[a, s],
                    recv_sem=recv_sems.at[a, s], device_id=peer, device_id_type=pl.DeviceIdType.MESH))
                land = out_refs[a].at[self.index(peer)]
                arrivals.append(pltpu.make_async_remote_copy(
                    src_ref=land, dst_ref=land, send_sem=send_sems.at[a, s],
                    recv_sem=recv_sems.at[a, s], device_id=peer, device_id_type=pl.DeviceIdType.MESH))
        return mine, sends, arrivals

    def start(self, x_refs, out_refs, sems):
        mine, sends, _ = self._copies(x_refs, out_refs, sems)
        for cp in mine + sends:
            cp.start()

    def wait(self, x_refs, out_refs, sems):
        mine, sends, arrivals = self._copies(x_refs, out_refs, sems)
        for cp in arrivals:
            cp.wait_recv()
        for cp in sends:
            cp.wait_send()
        for cp in mine:
            cp.wait()


def _grid_edges(grid):
    first = functools.reduce(lambda p, q: p & q, [pl.program_id(d) == 0 for d in range(len(grid))])
    last = functools.reduce(lambda p, q: p & q, [pl.program_id(d) == g - 1 for d, g in enumerate(grid)])
    return first, last


def _mm(a, b, mode, name, out_dtype=F32, add=None, bias=None, a_blk=False, b_blk=False, out_blk=0, carry=None,
        n_tiles=None, post=None):
    w = b.shape[2] if b_blk else out_blk
    if mode == "nn":
        M, K = a.shape
        N = b.shape[0] * w if b_blk else b.shape[1]
    elif mode == "nt":
        M = a.shape[1] if a_blk else a.shape[0]
        N = b.shape[1] if b_blk else b.shape[0]
        K = b.shape[0] * w if b_blk else b.shape[1]
    else:
        K, M = a.shape
        N = b.shape[0] * w if b_blk else b.shape[1]
    tm = _pick(M, (1024, 768, 512, 256))
    tn = _pick(N, (1024, 768, 512, 256))
    if mode == "tn" and N <= 2048:
        tn = N
    tk = K if (mode != "tn" and K <= 2048) else _pick(K, (2048, 1024, 768, 512, 256) if mode == "tn"
                                                       else (1024, 768, 512, 256))
    if mode == "nn" and K > 2048 and N <= 2048:
        tn, tk = N, _pick(K, (1536, 1024, 768, 512, 256))
    perm = lambda j: j
    if n_tiles:
        tn, perm = n_tiles
        tm = _pick(M, (2048, 1024, 512, 256))
    if mode == "nt" and b_blk:
        tk = w
    elif b_blk or out_blk:
        tn = w
    if mode == "tn" and 4 * K * (tm + tn) <= (24 << 20):
        tk = K
    nk = K // tk
    grid = (M // tm, N // tn, nk)
    if mode == "nn":
        a_spec = pl.BlockSpec((tm, tk), lambda i, j, k: (i, k))
        b_spec = (pl.BlockSpec((1, tk, tn), lambda i, j, k: (j, k, 0)) if b_blk
                  else pl.BlockSpec((tk, tn), lambda i, j, k: (k, j)))
        dims = (((1,), (0,)), ((), ()))
    elif mode == "nt":
        a_spec = (pl.BlockSpec((1, tm, tk), lambda i, j, k: (k, i, 0)) if a_blk
                  else pl.BlockSpec((tm, tk), lambda i, j, k: (i, k)))
        b_spec = (pl.BlockSpec((1, tn, tk), lambda i, j, k: (k, j, 0)) if b_blk
                  else pl.BlockSpec((tn, tk), lambda i, j, k: (perm(j), k)))
        dims = (((1,), (1,)), ((), ()))
    else:
        a_spec = pl.BlockSpec((tk, tm), lambda i, j, k: (k, i))
        b_spec = (pl.BlockSpec((1, tk, tn), lambda i, j, k: (j, k, 0)) if b_blk
                  else pl.BlockSpec((tk, tn), lambda i, j, k: (k, j)))
        dims = (((0,), (0,)), ((), ()))
    if out_blk:
        out_spec = pl.BlockSpec((1, tm, tn), lambda i, j, k: (j, i, 0))
        out_shape = jax.ShapeDtypeStruct((N // tn, M, tn), out_dtype)
    else:
        out_spec = pl.BlockSpec((tm, tn), lambda i, j, k: (i, j))
        out_shape = jax.ShapeDtypeStruct((M, N), out_dtype)
    extras, extra_specs, combine = [], [], []
    if add is not None:
        extras.append(add)
        extra_specs.append(pl.BlockSpec((tm, tn), lambda i, j, k: (i, j)))
        combine.append(lambda out, t: out + t)
    if bias is not None:
        extras.append(bias)
        extra_specs.append(pl.BlockSpec((1, tn), lambda i, j, k: (0, j)))
        combine.append(lambda out, t: out + t)
    if post is not None:
        extras.append(post[1])
        extra_specs.append(pl.BlockSpec((tm, tn), lambda i, j, k: (i, j)))
        combine.append(post[0])

    n_ex = len(extras)
    nc = carry.n if carry else 0

    def body(a_ref, b_ref, *rest):
        ex_refs, cx = rest[:n_ex], rest[n_ex:n_ex + nc]
        o_ref, cout = rest[n_ex + nc], rest[n_ex + nc + 1:n_ex + 2 * nc + 1]
        tail = rest[n_ex + 2 * nc + 1:]
        sems = tail[:3] if carry else ()
        first, last = _grid_edges(grid)
        if carry:
            @pl.when(first)
            def _():
                carry.start(cx, cout, sems)

        def product():
            av = a_ref[0] if a_blk else a_ref[...]
            bv = b_ref[0] if b_blk else b_ref[...]
            return lax.dot_general(av, bv, dims, preferred_element_type=F32)

        def finish(out):
            for r, fn in zip(ex_refs, combine):
                out = fn(out, r[...])
            if out_blk:
                o_ref[0] = out.astype(out_dtype)
            else:
                o_ref[...] = out.astype(out_dtype)

        if nk == 1:
            finish(product())
        else:
            acc_ref = tail[-1]
            k = pl.program_id(2)

            @pl.when(k == 0)
            def _():
                acc_ref[...] = jnp.zeros_like(acc_ref)

            acc_ref[...] += product()

            @pl.when(k == nk - 1)
            def _():
                finish(acc_ref[...])

        if carry:
            @pl.when(last)
            def _():
                carry.wait(cx, cout, sems)

    scratch = (carry.scratch if carry else []) + ([pltpu.VMEM((tm, tn), F32)] if nk > 1 else [])
    outs = pl.pallas_call(
        body, name=name, grid=grid,
        in_specs=[a_spec, b_spec] + extra_specs + (carry.specs if carry else []),
        out_specs=[out_spec] + (carry.specs if carry else []),
        out_shape=[out_shape] + (carry.out_shape if carry else []),
        scratch_shapes=scratch,
        compiler_params=_cparams(("arbitrary", "arbitrary", "arbitrary")),
    )(a, b, *extras, *(carry.xs if carry else []))
    return (outs[0], outs[1:]) if carry else outs[0]


def _rspec(tm, w, cb=0):
    return pl.BlockSpec((tm, w), lambda i: (i, cb))


def _fspec(shape):
    nd = len(shape)
    return pl.BlockSpec(shape, lambda i: (0,) * nd)


def _rowcall(body, name, L, tm, ins, row_outs, acc_outs=()):
    out_shape = [jax.ShapeDtypeStruct((L, w), dt) for w, dt in row_outs]
    out_shape += [jax.ShapeDtypeStruct(s, F32) for s in acc_outs]
    out_specs = [_rspec(tm, w) for w, _ in row_outs] + [_fspec(s) for s in acc_outs]
    return pl.pallas_call(
        body, name=name, grid=(L // tm,),
        in_specs=[s for _, s in ins], out_specs=out_specs, out_shape=out_shape,
        compiler_params=_cparams(("arbitrary",)),
    )(*[a for a, _ in ins])


def _mm_rows(a, b, mode, name, post, row_ins, full_ins, row_outs, acc_outs=(), vmem=VMEM_LIMIT, pre=None):
    M, K = a.shape
    N = b.shape[1] if mode == "nn" else b.shape[0]
    tm = _pick(M, (256,))
    dims = (((1,), (0,)), ((), ())) if mode == "nn" else (((1,), (1,)), ((), ()))
    n_in, n_row = len(row_ins) + len(full_ins), len(row_outs)

    def body(a_ref, b_ref, *rest):
        ins, outs = rest[:n_in], rest[n_in:]
        av = pre(a_ref[...]) if pre else a_ref[...]
        prod = lax.dot_general(av, b_ref[...], dims, preferred_element_type=F32)
        res = post(prod, av, *ins) if pre else post(prod, *ins)
        for o_ref, val in zip(outs[:n_row], res[:n_row]):
            o_ref[...] = val.astype(o_ref.dtype)
        for acc_ref, val in zip(outs[n_row:], res[n_row:]):
            _acc(acc_ref, val)

    out_shape = [jax.ShapeDtypeStruct((M, w), dt) for w, dt in row_outs]
    out_shape += [jax.ShapeDtypeStruct(s, F32) for s in acc_outs]
    row_ins = [r if isinstance(r, tuple) else (r, r.shape[1], 0) for r in row_ins]
    return pl.pallas_call(
        body, name=name, grid=(M // tm,),
        in_specs=[_rspec(tm, K), _fspec(b.shape)] + [_rspec(tm, w, cb) for _, w, cb in row_ins]
        + [_fspec(f.shape) for f in full_ins],
        out_specs=[_rspec(tm, w) for w, _ in row_outs] + [_fspec(s) for s in acc_outs],
        out_shape=out_shape, compiler_params=_cparams(("arbitrary",), vmem),
    )(a, b, *[r for r, _, _ in row_ins], *full_ins)


def _acc(ref, val):
    @pl.when(pl.program_id(0) == 0)
    def _():
        ref[...] = jnp.zeros_like(ref)
    ref[...] += val


def _colsum(v):
    return jnp.sum(v, axis=0, keepdims=True)


def _rms(xv):
    return lax.rsqrt(jnp.mean(xv * xv, axis=-1, keepdims=True) + EPS)


def _rms_bwd(dn, xhat, r, g):
    dng = dn * g
    return r * (dng - xhat * jnp.mean(dng * xhat, axis=-1, keepdims=True))


def _sigmoid(v):
    return jax.nn.sigmoid(v)


def _f32(ref):
    return ref[...].astype(F32)


def _partner(v):
    w = v.shape[-1]
    lane = lax.broadcasted_iota(jnp.int32, v.shape, v.ndim - 1)
    first_half = (lane % 64) < 32
    return jnp.where(first_half, pltpu.roll(v, w - 32, axis=v.ndim - 1), pltpu.roll(v, 32, axis=v.ndim - 1))


def _norm_in(x, g, name):
    L, D = x.shape
    tm = _pick(L, (512, 256))

    def body(x_ref, g_ref, o_ref):
        xv = x_ref[...]
        o_ref[...] = (xv * _rms(xv) * g_ref[...]).astype(BF16)

    return _rowcall(body, name, L, tm, [(x, _rspec(tm, D)), (g, _fspec(g.shape))], [(D, BF16)])[0]


def _rope_tables(L):
    t = np.arange(L)
    rows = (t // GRID_W).astype(np.float32)
    cols = (t % GRID_W).astype(np.float32)
    n_freq = HEAD_DIM // 4
    inv_freq = np.float32(ROPE_THETA) ** (-np.arange(n_freq, dtype=np.float32) / np.float32(n_freq))
    ar = (rows[:, None] * inv_freq[None, :]).astype(np.float32).astype(np.float64)
    ac = (cols[:, None] * inv_freq[None, :]).astype(np.float32).astype(np.float64)
    cos = np.concatenate([np.cos(ar), np.cos(ar), np.cos(ac), np.cos(ac)], axis=-1).astype(np.float32)
    sin = np.concatenate([-np.sin(ar), np.sin(ar), -np.sin(ac), np.sin(ac)], axis=-1).astype(np.float32)
    return jnp.asarray(cos), jnp.asarray(sin)


def _qkv_prep(z, cos, sin, qn, kn):
    L = z.shape[0]
    tm = _pick(L, (512, 256))
    scale = HEAD_DIM ** -0.5
    kblk = 4 * D_ATTN // D_KV

    def body(q_ref, k_ref, v_ref, cos_ref, sin_ref, qn_ref, kn_ref, qo_ref, ko_ref, vo_ref, kt_ref):
        c, s = cos_ref[...], sin_ref[...]

        def head(xh, w):
            n = xh * _rms(xh) * w
            return n * c + _partner(n) * s

        for h in range(N_HEADS):
            sl = slice(h * HEAD_DIM, (h + 1) * HEAD_DIM)
            qo_ref[:, sl] = (head(q_ref[:, sl].astype(F32), qn_ref[...]) * scale).astype(BF16)
        for h in range(N_KV):
            sl = slice(h * HEAD_DIM, (h + 1) * HEAD_DIM)
            kr = head(k_ref[:, sl].astype(F32), kn_ref[...])
            ko_ref[:, sl] = kr.astype(BF16)
            kt_ref[sl, :] = kr.T.astype(BF16)
        vo_ref[...] = v_ref[...].astype(BF16)

    return pl.pallas_call(
        body, name="qkv_prep", grid=(L // tm,),
        in_specs=[_rspec(tm, D_ATTN, 0), _rspec(tm, D_KV, kblk), _rspec(tm, D_KV, kblk + 1),
                  _rspec(tm, HEAD_DIM), _rspec(tm, HEAD_DIM), _fspec(qn.shape), _fspec(kn.shape)],
        out_specs=[_rspec(tm, D_ATTN), _rspec(tm, D_KV), _rspec(tm, D_KV),
                   pl.BlockSpec((D_KV, tm), lambda i: (0, i))],
        out_shape=[jax.ShapeDtypeStruct((L, D_ATTN), BF16), jax.ShapeDtypeStruct((L, D_KV), BF16),
                   jax.ShapeDtypeStruct((L, D_KV), BF16), jax.ShapeDtypeStruct((D_KV, L), BF16)],
        compiler_params=_cparams(("arbitrary",)),
    )(z, z, z, cos, sin, qn, kn)


def _col_to_row(col):
    n = col.shape[0]
    eye = lax.broadcasted_iota(jnp.int32, (n, n), 0) == lax.broadcasted_iota(jnp.int32, (n, n), 1)
    return jnp.sum(jnp.where(eye, col, 0.0), axis=0, keepdims=True)


def _attn_fwd(q, k, v, carry=None):
    L = q.shape[0]
    tq = _pick(L, (256, 128))
    grid = (N_HEADS, L // tq)
    nc = carry.n if carry else 0

    def body(q_ref, k_ref, v_ref, *rest):
        cx, (o_ref, lse_ref) = rest[:nc], rest[nc:nc + 2]
        cout, sems = rest[nc + 2:2 * nc + 2], rest[2 * nc + 2:]
        first, last = _grid_edges(grid)
        if carry:
            @pl.when(first)
            def _():
                carry.start(cx, cout, sems)

        s = lax.dot_general(q_ref[...], k_ref[...], (((1,), (1,)), ((), ())), preferred_element_type=F32)
        m = jnp.max(s, axis=-1, keepdims=True)
        e = jnp.exp(s - m)
        l = jnp.sum(e, axis=-1, keepdims=True)
        o_ref[...] = (jnp.dot(e.astype(BF16), v_ref[...], preferred_element_type=F32) / l).astype(BF16)
        lse_ref[0] = _col_to_row(m + jnp.log(l))

        if carry:
            @pl.when(last)
            def _():
                carry.wait(cx, cout, sems)

    outs = pl.pallas_call(
        body, name="attn_fwd", grid=grid,
        in_specs=[pl.BlockSpec((tq, HEAD_DIM), lambda h, i: (i, h)),
                  pl.BlockSpec((L, HEAD_DIM), lambda h, i: (0, h // REP)),
                  pl.BlockSpec((L, HEAD_DIM), lambda h, i: (0, h // REP))] + (carry.specs if carry else []),
        out_specs=[pl.BlockSpec((tq, HEAD_DIM), lambda h, i: (i, h)),
                   pl.BlockSpec((1, 1, tq), lambda h, i: (h, 0, i))] + (carry.specs if carry else []),
        out_shape=[jax.ShapeDtypeStruct((L, D_ATTN), BF16), jax.ShapeDtypeStruct((N_HEADS, 1, L), F32)]
        + (carry.out_shape if carry else []),
        scratch_shapes=carry.scratch if carry else [],
        compiler_params=_cparams(("arbitrary", "arbitrary")),
    )(q, k, v, *(carry.xs if carry else []))
    return outs[0], outs[1], outs[2:]


def _attn_bwd(q, k, v, kt, do, o, lse, carry=None):
    L = q.shape[0]
    tq = _pick(L, (256, 128))
    kc = _pick(L, (512, 256, 128))
    nt = (((1,), (1,)), ((), ()))
    grid = (N_KV, REP, L // tq)
    nc = carry.n if carry else 0

    def body(q_ref, do_ref, o_ref, lse_ref, k_ref, v_ref, kt_ref, *rest):
        cx, (dq_ref, dk_ref, dv_ref) = rest[:nc], rest[nc:nc + 3]
        cout, sems = rest[nc + 3:2 * nc + 3], rest[2 * nc + 3:]
        first, last = _grid_edges(grid)
        if carry:
            @pl.when(first)
            def _():
                carry.start(cx, cout, sems)

        @pl.when((pl.program_id(1) == 0) & (pl.program_id(2) == 0))
        def _():
            dk_ref[...] = jnp.zeros_like(dk_ref)
            dv_ref[...] = jnp.zeros_like(dv_ref)

        qv, dov = q_ref[...], do_ref[...]
        lse_row = lse_ref[0]
        delta = _col_to_row(jnp.sum(dov.astype(F32) * _f32(o_ref), axis=-1, keepdims=True))
        dqt = jnp.zeros((HEAD_DIM, tq), F32)
        for c in range(L // kc):
            sl = slice(c * kc, (c + 1) * kc)
            st = lax.dot_general(k_ref[sl, :], qv, nt, preferred_element_type=F32)
            pt = jnp.exp(st - lse_row)
            dpt = lax.dot_general(v_ref[sl, :], dov, nt, preferred_element_type=F32)
            dst = (pt * (dpt - delta)).astype(BF16)
            dv_ref[sl, :] += jnp.dot(pt.astype(BF16), dov, preferred_element_type=F32)
            dk_ref[sl, :] += jnp.dot(dst, qv, preferred_element_type=F32)
            dqt = dqt + jnp.dot(kt_ref[:, sl], dst, preferred_element_type=F32)
        dq_ref[...] = dqt.T

        if carry:
            @pl.when(last)
            def _():
                carry.wait(cx, cout, sems)

    head = lambda g, r, i: (i, g * REP + r)
    outs = pl.pallas_call(
        body, name="attn_bwd", grid=grid,
        in_specs=[pl.BlockSpec((tq, HEAD_DIM), head), pl.BlockSpec((tq, HEAD_DIM), head),
                  pl.BlockSpec((tq, HEAD_DIM), head),
                  pl.BlockSpec((1, 1, tq), lambda g, r, i: (g * REP + r, 0, i)),
                  pl.BlockSpec((L, HEAD_DIM), lambda g, r, i: (0, g)),
                  pl.BlockSpec((L, HEAD_DIM), lambda g, r, i: (0, g)),
                  pl.BlockSpec((HEAD_DIM, L), lambda g, r, i: (g, 0))] + (carry.specs if carry else []),
        out_specs=[pl.BlockSpec((tq, HEAD_DIM), head),
                   pl.BlockSpec((L, HEAD_DIM), lambda g, r, i: (0, g)),
                   pl.BlockSpec((L, HEAD_DIM), lambda g, r, i: (0, g))] + (carry.specs if carry else []),
        out_shape=[jax.ShapeDtypeStruct((L, D_ATTN), F32), jax.ShapeDtypeStruct((L, D_KV), F32),
                   jax.ShapeDtypeStruct((L, D_KV), F32)] + (carry.out_shape if carry else []),
        scratch_shapes=carry.scratch if carry else [],
        compiler_params=_cparams(("arbitrary", "arbitrary", "arbitrary")),
    )(q, do, o, lse, k, v, kt, *(carry.xs if carry else []))
    return outs[0], outs[1], outs[2], outs[3:]


def _seg_perm(a):
    L, C = a.shape
    return a.reshape(SEG, L // SEG, C).transpose(1, 0, 2).reshape(L, C)


def _seg_unperm(a):
    L, C = a.shape
    return a.reshape(L // SEG, SEG, C).transpose(1, 0, 2).reshape(L, C)


def _cmul(ar, ai, br, bi):
    return ar * br - ai * bi, ar * bi + ai * br


def _rows8(rr):
    if isinstance(rr, int):
        return pl.ds(rr * SEG, SEG)
    return pl.ds(pl.multiple_of(rr * SEG, SEG), SEG)


def _seg_scan(xr_ref, xi_ref, ar, ai, reverse, n_rows, visit=None, visit_init=(), entering=None):
    shape = ar.shape
    zero = jnp.zeros(shape, F32)
    rc = n_rows // CHAINS

    def index(q):
        return (n_rows - 1 - q) if reverse else q

    if entering is None:
        def ends(q, carry):
            out = []
            for j in range(CHAINS):
                sl = _rows8(index(j * rc + q))
                pr, pi = _cmul(ar, ai, carry[2 * j], carry[2 * j + 1])
                out += [pr + xr_ref[sl, :], pi + xi_ref[sl, :]]
            return tuple(out)

        def ends_block(qb, carry):
            for t in range(SCAN_UNROLL):
                carry = ends(qb * SCAN_UNROLL + t, carry)
            return carry

        e = lax.fori_loop(0, rc // SCAN_UNROLL, ends_block, (zero,) * (2 * CHAINS))

        pr, pi = ar, ai
        for _ in range(int(math.log2(rc))):
            pr, pi = _cmul(pr, pi, pr, pi)
        sub = lax.broadcasted_iota(jnp.int32, shape, 0)
        shift = (SEG - 1) if reverse else 1
        edge = (SEG - 1) if reverse else 0
        entering = [(zero, zero)] * CHAINS
        for _ in range(SEG):
            tr, ti = _cmul(pr, pi, *entering[CHAINS - 1])
            cur = (jnp.where(sub == edge, 0.0, pltpu.roll(tr + e[2 * CHAINS - 2], shift, axis=0)),
                   jnp.where(sub == edge, 0.0, pltpu.roll(ti + e[2 * CHAINS - 1], shift, axis=0)))
            entering = [cur]
            for j in range(1, CHAINS):
                tr, ti = _cmul(pr, pi, *cur)
                cur = (tr + e[2 * j - 2], ti + e[2 * j - 1])
                entering.append(cur)

    def step(q, carry, last):
        out, acc = [], carry[2 * CHAINS:]
        for j in range(CHAINS):
            rr = index(j * rc + q)
            sl = _rows8(rr)
            pr, pi = _cmul(ar, ai, carry[2 * j], carry[2 * j + 1])
            nr, ni = pr + xr_ref[sl, :], pi + xi_ref[sl, :]
            xr_ref[sl, :] = nr
            xi_ref[sl, :] = ni
            if visit:
                acc = visit(rr, nr, ni, acc, last and j == CHAINS - 1)
            out += [nr, ni]
        return (*out, *acc)

    def step_block(qb, carry):
        for t in range(SCAN_UNROLL):
            carry = step(qb * SCAN_UNROLL + t, carry, False)
        return carry

    start = tuple(v for pair in entering for v in pair)
    n_blocks = (rc - 1) // SCAN_UNROLL
    carry = lax.fori_loop(0, n_blocks, step_block, (*start, *visit_init))
    for q in range(n_blocks * SCAN_UNROLL, rc - 1):
        carry = step(q, carry, False)
    carry = step(rc - 1, carry, True)
    return entering, carry[2 * CHAINS:]


def _discretise(a_re, a_im, ldt):
    lr = jnp.minimum(a_re, -1e-4)
    li = a_im
    dt = jnp.exp(ldt)
    mag = jnp.exp(lr * dt)
    lbr = mag * jnp.cos(li * dt)
    lbi = mag * jnp.sin(li * dt)
    den = lr * lr + li * li
    nr = lbr - 1.0
    fr = (nr * lr + lbi * li) / den
    fi = (lbi * lr - nr * li) / den
    return lr, li, dt, lbr, lbi, fr, fi


def _lane_row(v):
    return jnp.concatenate([v[g:g + 1, :] for g in range(v.shape[0])], axis=1)


def _ssm_fill_maps(d, prm, tmp_ref, maps):
    a_re_ref, a_im_ref, ldt_ref, bt_re_ref, bt_im_ref, c_re_ref, c_im_ref = prm
    _, _, _, lbr, lbi, fr, fi = _discretise(a_re_ref[d], a_im_ref[d], ldt_ref[d])

    def fill(dst, piece):
        tmp_ref[...] = jnp.zeros_like(tmp_ref)
        for g in range(SLAB_G):
            tmp_ref[g * SSM_H:(g + 1) * SSM_H, g * SSM_P:(g + 1) * SSM_P] = piece(g)
        dst[...] = tmp_ref[...].astype(BF16)

    wbr, wbi, wcr, wci = maps
    fill(wbr, lambda g: fr[g:g + 1] * bt_re_ref[d, g] - fi[g:g + 1] * bt_im_ref[d, g])
    fill(wbi, lambda g: fr[g:g + 1] * bt_im_ref[d, g] + fi[g:g + 1] * bt_re_ref[d, g])
    fill(wcr, lambda g: c_re_ref[d, g])
    fill(wci, lambda g: c_im_ref[d, g])
    return _lane_row(lbr), _lane_row(lbi)


def _ssm_param_specs():
    pole = pl.BlockSpec((2, SLAB_G, SSM_P), lambda j: (0, j, 0))
    step = pl.BlockSpec((2, SLAB_G, 1), lambda j: (0, j, 0))
    mat = pl.BlockSpec((2, SLAB_G, SSM_H, SSM_P), lambda j: (0, j, 0, 0))
    return [pole, pole, step, mat, mat, mat, mat]


_MAP_SCRATCH = [pltpu.VMEM((SLAB, SLAB_S), F32)] + [pltpu.VMEM((SLAB, SLAB_S), BF16)] * 4
_ENT_SPEC = pl.BlockSpec((1, 2, 2 * CHAINS, SEG, SLAB_S), lambda j: (j, 0, 0, 0, 0))
_NT = (((1,), (1,)), ((), ()))


def _ssm_fwd(u, prm, dskip):
    L, C = u.shape
    n_rows = L // SEG
    tc = _pick(L, (2048, 1024, 512, 256))
    u_spec = pl.BlockSpec((L, SLAB), lambda j: (0, j))
    d_spec = pl.BlockSpec((1, SLAB), lambda j: (0, j))

    def body(u_ref, *rest):
        prm_refs, d_ref, y_ref, ent_ref = rest[:7], rest[7], rest[8], rest[9]
        tmp_ref, maps, xr_ref, xi_ref = rest[10], rest[11:15], rest[15], rest[16]
        wbr, wbi, wcr, wci = maps
        y_ref[...] = u_ref[...] * d_ref[...]
        for d in range(2):
            lam_r, lam_i = _ssm_fill_maps(d, prm_refs, tmp_ref, maps)

            def inp(c, _):
                sl = pl.ds(pl.multiple_of(c * tc, tc), tc)
                ub = u_ref[sl, :].astype(BF16)
                xr_ref[sl, :] = jnp.dot(ub, wbr[...], preferred_element_type=F32)
                xi_ref[sl, :] = jnp.dot(ub, wbi[...], preferred_element_type=F32)
                return 0

            lax.fori_loop(0, L // tc, inp, 0)
            ar = jnp.broadcast_to(lam_r, (SEG, SLAB_S))
            ai = jnp.broadcast_to(lam_i, (SEG, SLAB_S))
            entering, _ = _seg_scan(xr_ref, xi_ref, ar, ai, d == 1, n_rows)
            for j, (er, ei) in enumerate(entering):
                ent_ref[0, d, 2 * j] = er
                ent_ref[0, d, 2 * j + 1] = ei

            def outp(c, _):
                sl = pl.ds(pl.multiple_of(c * tc, tc), tc)
                y_ref[sl, :] += (
                    lax.dot_general(xr_ref[sl, :].astype(BF16), wcr[...], _NT, preferred_element_type=F32)
                    - lax.dot_general(xi_ref[sl, :].astype(BF16), wci[...], _NT, preferred_element_type=F32))
                return 0

            lax.fori_loop(0, L // tc, outp, 0)

    return pl.pallas_call(
        body, name="ssm_fwd", grid=(C // SLAB,),
        in_specs=[u_spec] + _ssm_param_specs() + [d_spec],
        out_specs=[u_spec, _ENT_SPEC],
        out_shape=[jax.ShapeDtypeStruct((L, C), F32),
                   jax.ShapeDtypeStruct((C // SLAB, 2, 2 * CHAINS, SEG, SLAB_S), F32)],
        scratch_shapes=_MAP_SCRATCH + [pltpu.VMEM((L, SLAB_S), F32)] * 2,
        compiler_params=_cparams(("arbitrary",)),
    )(u, *prm, dskip)


def _ssm_bwd(u, dy, ent, prm, dskip):
    L, C = u.shape
    n_rows = L // SEG
    n_slab = C // SLAB
    tc = _pick(L, (2048, 1024, 512, 256))
    u_spec = pl.BlockSpec((L, SLAB), lambda j: (0, j))
    d_spec = pl.BlockSpec((1, SLAB), lambda j: (0, j))
    pg_spec = pl.BlockSpec((1, 2, PG_ROWS, SLAB_S), lambda j: (j, 0, 0, 0))

    def body(u_ref, dy_ref, ent_ref, *rest):
        prm_refs, d_ref, du_ref, pg_ref = rest[:7], rest[7], rest[8], rest[9]
        tmp_ref, maps, acc_ref = rest[10], rest[11:15], rest[15]
        xr_ref, xi_ref, gr_ref, gi_ref = rest[16:20]
        wbr, wbi, wcr, wci = maps
        du_ref[...] = dy_ref[...] * d_ref[...]
        pg_ref[...] = jnp.zeros_like(pg_ref)
        pg_ref[0, 0, 66:67, 0:SLAB] = _colsum(dy_ref[...] * u_ref[...])
        for d in range(2):
            lam_r, lam_i = _ssm_fill_maps(d, prm_refs, tmp_ref, maps)

            def inp(c, _):
                sl = pl.ds(pl.multiple_of(c * tc, tc), tc)
                ub = u_ref[sl, :].astype(BF16)
                dyb = dy_ref[sl, :].astype(BF16)
                xr_ref[sl, :] = jnp.dot(ub, wbr[...], preferred_element_type=F32)
                xi_ref[sl, :] = jnp.dot(ub, wbi[...], preferred_element_type=F32)
                gr_ref[sl, :] = jnp.dot(dyb, wcr[...], preferred_element_type=F32)
                gi_ref[sl, :] = -jnp.dot(dyb, wci[...], preferred_element_type=F32)
                return 0

            lax.fori_loop(0, L // tc, inp, 0)
            ar = jnp.broadcast_to(lam_r, (SEG, SLAB_S))
            ai = jnp.broadcast_to(lam_i, (SEG, SLAB_S))
            entering = [(ent_ref[0, d, 2 * j], ent_ref[0, d, 2 * j + 1]) for j in range(CHAINS)]
            _seg_scan(xr_ref, xi_ref, ar, ai, d == 1, n_rows, entering=entering)

            def pole(rr, lr, li, acc, last):
                if last:
                    pr, pi = entering[0]
                else:
                    nb = _rows8(rr + 1 if d == 1 else rr - 1)
                    pr, pi = xr_ref[nb, :], xi_ref[nb, :]
                return acc[0] + lr * pr + li * pi, acc[1] + li * pr - lr * pi

            zero = jnp.zeros((SEG, SLAB_S), F32)
            _, (accr, acci) = _seg_scan(gr_ref, gi_ref, ar, -ai, d == 0, n_rows, pole, (zero, zero))
            pg_ref[0, d, 64:65, :] = _colsum(accr)
            pg_ref[0, d, 65:66, :] = _colsum(acci)

            acc_ref[...] = jnp.zeros_like(acc_ref)

            def outp(c, _):
                sl = pl.ds(pl.multiple_of(c * tc, tc), tc)
                lrb, lib = gr_ref[sl, :].astype(BF16), gi_ref[sl, :].astype(BF16)
                du_ref[sl, :] += (lax.dot_general(lrb, wbr[...], _NT, preferred_element_type=F32)
                                  + lax.dot_general(lib, wbi[...], _NT, preferred_element_type=F32))
                ut = u_ref[sl, :].astype(F32).T.astype(BF16)
                dyt = dy_ref[sl, :].T.astype(BF16)
                acc_ref[0] += jnp.dot(ut, lrb, preferred_element_type=F32)
                acc_ref[1] += jnp.dot(ut, lib, preferred_element_type=F32)
                acc_ref[2] += jnp.dot(dyt, xr_ref[sl, :].astype(BF16), preferred_element_type=F32)
                acc_ref[3] -= jnp.dot(dyt, xi_ref[sl, :].astype(BF16), preferred_element_type=F32)
                return 0

            lax.fori_loop(0, L // tc, outp, 0)
            for m in range(4):
                for g in range(SLAB_G):
                    lanes = slice(g * SSM_P, (g + 1) * SSM_P)
                    pg_ref[0, d, m * SSM_H:(m + 1) * SSM_H, lanes] = acc_ref[m, g * SSM_H:(g + 1) * SSM_H, lanes]

    return pl.pallas_call(
        body, name="ssm_bwd", grid=(n_slab,),
        in_specs=[u_spec, u_spec, _ENT_SPEC] + _ssm_param_specs() + [d_spec],
        out_specs=[u_spec, pg_spec],
        out_shape=[jax.ShapeDtypeStruct((L, C), F32), jax.ShapeDtypeStruct((n_slab, 2, PG_ROWS, SLAB_S), F32)],
        scratch_shapes=_MAP_SCRATCH + [pltpu.VMEM((4, SLAB, SLAB_S), F32)] + [pltpu.VMEM((L, SLAB_S), F32)] * 4,
        compiler_params=_cparams(("arbitrary",), 60 << 20),
    )(u, dy, ent, *prm, dskip)


def _ssm_param_grads(pg, prm):
    n_slab = pg.shape[0]
    G = n_slab * SLAB_G
    pg_spec = pl.BlockSpec((1, 2, PG_ROWS, SLAB_S), lambda j: (j, 0, 0, 0))
    pole, _, step, mat = _ssm_param_specs()[:4]

    def body(pg_ref, a_re_ref, a_im_ref, ldt_ref, bt_re_ref, bt_im_ref,
             dbr_ref, dbi_ref, dcr_ref, dci_ref, dar_ref, dai_ref, dldt_ref, dd_ref):
        dd_ref[...] = pg_ref[0, 0, 66:67, 0:SLAB]
        for d in range(2):
            a_r = a_re_ref[d]
            lr, li, dt, lbr, lbi, f_r, f_i = _discretise(a_r, a_im_ref[d], ldt_ref[d])
            gfr_rows, gfi_rows, glr_rows, gli_rows = [], [], [], []
            for g in range(SLAB_G):
                lanes = slice(g * SSM_P, (g + 1) * SSM_P)
                gbr, gbi = pg_ref[0, d, 0:SSM_H, lanes], pg_ref[0, d, SSM_H:2 * SSM_H, lanes]
                b_r, b_i = bt_re_ref[d, g], bt_im_ref[d, g]
                fr, fi = f_r[g:g + 1], f_i[g:g + 1]
                dbr_ref[d, g] = fr * gbr + fi * gbi
                dbi_ref[d, g] = fr * gbi - fi * gbr
                gfr_rows.append(_colsum(gbr * b_r + gbi * b_i))
                gfi_rows.append(_colsum(gbi * b_r - gbr * b_i))
                dcr_ref[d, g] = pg_ref[0, d, 2 * SSM_H:3 * SSM_H, lanes]
                dci_ref[d, g] = pg_ref[0, d, 3 * SSM_H:4 * SSM_H, lanes]
                glr_rows.append(pg_ref[0, d, 64:65, lanes])
                gli_rows.append(pg_ref[0, d, 65:66, lanes])
            gfr, gfi = jnp.concatenate(gfr_rows, axis=0), jnp.concatenate(gfi_rows, axis=0)
            glr, gli = jnp.concatenate(glr_rows, axis=0), jnp.concatenate(gli_rows, axis=0)
            den = lr * lr + li * li
            ir, ii = lr / den, -li / den
            tr, ti = _cmul(ir, -ii, gfr, gfi)
            glbr, glbi = glr + tr, gli + ti
            qr, qi = _cmul(f_r, f_i, ir, ii)
            dlr, dli = _cmul(-qr, qi, gfr, gfi)
            zr, zi = _cmul(lbr, -lbi, glbr, glbi)
            dlr = dlr + dt * zr
            dli = dli + dt * zi
            dar_ref[d] = jnp.where(a_r < -1e-4, dlr, jnp.where(a_r == -1e-4, 0.5 * dlr, 0.0))
            dai_ref[d] = dli
            dldt_ref[d] = jnp.sum(lr * zr + li * zi, axis=-1, keepdims=True) * dt

    a_re, a_im, ldt, bt_re, bt_im = prm[:5]
    mshape = jax.ShapeDtypeStruct(bt_re.shape, F32)
    pshape = jax.ShapeDtypeStruct(a_re.shape, F32)
    return pl.pallas_call(
        body, name="ssm_param_grads", grid=(n_slab,),
        in_specs=[pg_spec, pole, pole, step, mat, mat],
        out_specs=[mat, mat, mat, mat, pole, pole, step, pl.BlockSpec((1, SLAB), lambda j: (0, j))],
        out_shape=[mshape, mshape, mshape, mshape, pshape, pshape, jax.ShapeDtypeStruct(ldt.shape, F32),
                   jax.ShapeDtypeStruct((1, n_slab * SLAB), F32)],
        compiler_params=_cparams(("arbitrary",)),
    )(pg, a_re, a_im, ldt, bt_re, bt_im)


def _peer(k, x, y, c):
    return (1 - x if k & 4 else x, 1 - y if k & 2 else y, 1 - c if k & 1 else c)


def _dev_index(pos):
    return 4 * pos[0] + 2 * pos[1] + pos[2]


def _chip_index(pos):
    return 2 * pos[0] + pos[1]


def _sibling_swap(x, name):
    any_spec = pl.BlockSpec(memory_space=pl.ANY)

    def body(x_ref, out_ref, send_sems, recv_sems):
        x_, y_, c_ = lax.axis_index("x"), lax.axis_index("y"), lax.axis_index("c")
        copies = [pltpu.make_async_remote_copy(
            src_ref=x_ref.at[2 * chip + (1 - c_)], dst_ref=out_ref.at[chip], send_sem=send_sems.at[chip],
            recv_sem=recv_sems.at[chip], device_id=(x_, y_, 1 - c_), device_id_type=pl.DeviceIdType.MESH)
            for chip in range(N_CHIPS)]
        for cp in copies:
            cp.start()
        for cp in copies:
            cp.wait()

    return pl.pallas_call(
        body, name=name, out_shape=jax.ShapeDtypeStruct((N_CHIPS,) + x.shape[1:], x.dtype),
        in_specs=[any_spec], out_specs=any_spec,
        scratch_shapes=[pltpu.SemaphoreType.DMA((N_CHIPS,)), pltpu.SemaphoreType.DMA((N_CHIPS,))],
    )(x)


def _pair_sum(x, got, name):
    n, R, W = got.shape
    tr = _row_tile(R, W, 5 << 20)
    core = lax.axis_index("c").astype(jnp.int32).reshape(1)

    def body(core_ref, a_ref, b_ref, o_ref):
        o_ref[...] = (a_ref[...].astype(F32) + b_ref[...].astype(F32)).astype(BF16)

    spec = pl.BlockSpec((1, tr, W), lambda i, j, c: (i, j, 0))
    grid_spec = pltpu.PrefetchScalarGridSpec(
        num_scalar_prefetch=1, grid=(n, R // tr),
        in_specs=[pl.BlockSpec((1, tr, W), lambda i, j, c: (2 * i + c[0], j, 0)), spec], out_specs=spec)
    return pl.pallas_call(
        body, name=name, grid_spec=grid_spec, out_shape=jax.ShapeDtypeStruct((n, R, W), BF16),
        compiler_params=_cparams(("arbitrary", "arbitrary")),
    )(core, x, got)


def _all_gather(xs, name):
    n = len(xs)
    any_spec = pl.BlockSpec(memory_space=pl.ANY)

    def body(*refs):
        x_refs, out_refs = refs[:n], refs[n:2 * n]
        send_sems, recv_sems, local_sems = refs[2 * n:]
        x, y, c = lax.axis_index("x"), lax.axis_index("y"), lax.axis_index("c")
        me, sibling = (x, y, c), (x, y, 1 - c)
        chips = [(1 - x, y), (x, 1 - y), (1 - x, 1 - y)]

        def copy(a, k, block, to, src=None):
            dst = out_refs[a].at[_dev_index(block)]
            return pltpu.make_async_remote_copy(
                src_ref=dst if src is None else src, dst_ref=dst,
                send_sem=send_sems.at[a, k], recv_sem=recv_sems.at[a, k],
                device_id=to, device_id_type=pl.DeviceIdType.MESH)

        mine = [pltpu.make_async_copy(x_refs[a], out_refs[a].at[_dev_index(me)], local_sems.at[a]) for a in range(n)]
        for cp in mine:
            cp.start()
        first = []
        for a in range(n):
            first.append(copy(a, 0, me, sibling, src=x_refs[a]))
            first += [copy(a, 1 + j, me, (*chip, c), src=x_refs[a]) for j, chip in enumerate(chips)]
        for cp in first:
            cp.start()
        passed = []
        for j, chip in enumerate(chips):
            for a in range(n):
                copy(a, 1 + j, (*chip, c), me).wait_recv()
                fwd = copy(a, 4 + j, (*chip, c), sibling)
                fwd.start()
                passed.append(fwd)
        for a in range(n):
            copy(a, 0, sibling, me).wait_recv()
            for j, chip in enumerate(chips):
                copy(a, 4 + j, (*chip, 1 - c), me).wait_recv()
        for cp in first + passed:
            cp.wait_send()
        for cp in mine:
            cp.wait()

    return pl.pallas_call(
        body, name=name,
        out_shape=[jax.ShapeDtypeStruct((N_DEV,) + v.shape, v.dtype) for v in xs],
        in_specs=[any_spec] * n, out_specs=[any_spec] * n,
        scratch_shapes=[pltpu.SemaphoreType.DMA((n, 7)), pltpu.SemaphoreType.DMA((n, 7)),
                        pltpu.SemaphoreType.DMA((n,))],
    )(*xs)


def _sum_blocks(x, name):
    _, R, W = x.shape

    def body(x_ref, o_ref):
        acc = x_ref[0].astype(F32)
        for d in range(1, N_DEV):
            acc = acc + x_ref[d].astype(F32)
        o_ref[...] = acc

    return pl.pallas_call(body, name=name, out_shape=jax.ShapeDtypeStruct((R, W), F32),
                          compiler_params=pltpu.CompilerParams(vmem_limit_bytes=VMEM_LIMIT))(x)


def _adam_update(w, g, m, v):
    mn = ADAM_B1 * m + (1.0 - ADAM_B1) * g
    vn = ADAM_B2 * v + (1.0 - ADAM_B2) * (g * g)
    m_hat = mn / (1.0 - ADAM_B1 ** ADAM_STEP)
    v_hat = vn / (1.0 - ADAM_B2 ** ADAM_STEP)
    return -ADAM_LR * (m_hat / (jnp.sqrt(v_hat) + ADAM_EPS) + ADAM_WD * w), mn, vn


def _row_tile(R, W, budget):
    padded = -(-W // LANES) * LANES * 4
    if R * padded <= budget:
        return R
    return _pick(R, [t for t in (2048, 1024, 512, 256, 128, 64, 32, 16, 8) if t * padded <= budget])


def _adamw(w, g, m, v, name):
    R, W = w.shape
    tr = _row_tile(R, W, 1 << 20)

    def body(w_ref, g_ref, m_ref, v_ref, d_ref, mo_ref, vo_ref):
        d_ref[...], mo_ref[...], vo_ref[...] = _adam_update(w_ref[...], g_ref[...], m_ref[...], v_ref[...])

    spec = pl.BlockSpec((tr, W), lambda i: (i, 0))
    shp = jax.ShapeDtypeStruct((R, W), F32)
    return pl.pallas_call(
        body, name=name, grid=(R // tr,), in_specs=[spec] * 4, out_specs=[spec] * 3, out_shape=[shp] * 3,
        compiler_params=_cparams(("arbitrary",)),
    )(w, g, m, v)


def _adamw_group(ws, gs, ms, vs, name):
    n = len(ws)

    def body(*refs):
        ins, outs = refs[:4 * n], refs[4 * n:]
        for i in range(n):
            w_ref, g_ref, m_ref, v_ref = ins[i], ins[n + i], ins[2 * n + i], ins[3 * n + i]
            outs[i][...], outs[n + i][...], outs[2 * n + i][...] = _adam_update(
                w_ref[...], g_ref[...], m_ref[...], v_ref[...])

    shapes = [jax.ShapeDtypeStruct(w.shape, F32) for w in ws]
    outs = pl.pallas_call(body, name=name, out_shape=shapes * 3)(*ws, *gs, *ms, *vs)
    return outs[:n], outs[n:2 * n], outs[2 * n:]


def _adamw_reduce(w, land, m, v, name):
    R, W = w.shape
    n = land.shape[0]
    tr = _row_tile(R, W, 1 << 20)

    def body(w_ref, l_ref, m_ref, v_ref, g_ref, d_ref, mo_ref, vo_ref):
        g = l_ref[0].astype(F32)
        for d in range(1, n):
            g = g + l_ref[d].astype(F32)
        g_ref[...] = g
        d_ref[...], mo_ref[...], vo_ref[...] = _adam_update(w_ref[...], g, m_ref[...], v_ref[...])

    spec = pl.BlockSpec((tr, W), lambda i: (i, 0))
    lspec = pl.BlockSpec((n, tr, W), lambda i: (0, i, 0))
    shp = jax.ShapeDtypeStruct((R, W), F32)
    return pl.pallas_call(
        body, name=name, grid=(R // tr,), in_specs=[spec, lspec, spec, spec], out_specs=[spec] * 4,
        out_shape=[shp] * 4, compiler_params=_cparams(("arbitrary",)),
    )(w, land, m, v)


def _gelu(v):
    c = math.sqrt(2.0 / math.pi)
    return 0.5 * v * (1.0 + jnp.tanh(c * (v + 0.044715 * v * v * v)))


def _gelu_grad(v):
    c = math.sqrt(2.0 / math.pi)
    t = jnp.tanh(c * (v + 0.044715 * v * v * v))
    return 0.5 * (1.0 + t) + 0.5 * v * (1.0 - t * t) * c * (1.0 + 3.0 * 0.044715 * v * v)


def kernel(x, p, norm_mix, w_in, q_norm, k_norm, ssm_a_re, ssm_a_im, ssm_log_dt, ssm_b_re, ssm_b_im, ssm_c_re, ssm_c_im, ssm_d, w_glu, b_glu, w_out, norm_ple, w_ple_gate, w_ple_proj, norm_final, loss_target, m_norm_mix, m_w_in, m_q_norm, m_k_norm, m_ssm_a_re, m_ssm_a_im, m_ssm_log_dt, m_ssm_b_re, m_ssm_b_im, m_ssm_c_re, m_ssm_c_im, m_ssm_d, m_w_glu, m_b_glu, m_w_out, m_norm_ple, m_w_ple_gate, m_w_ple_proj, m_norm_final, v_norm_mix, v_w_in, v_q_norm, v_k_norm, v_ssm_a_re, v_ssm_a_im, v_ssm_log_dt, v_ssm_b_re, v_ssm_b_im, v_ssm_c_re, v_ssm_c_im, v_ssm_d, v_w_glu, v_b_glu, v_w_out, v_norm_ple, v_w_ple_gate, v_w_ple_proj, v_norm_final):
    L, D = x.shape[1], x.shape[2]
    D_SSM = ssm_d.shape[1]
    G = D_SSM // SSM_H
    n_slab = D_SSM // SLAB
    n_in = w_in.shape[2]
    D_IN = n_in * N_DEV
    n_pp = w_ple_proj.shape[2]
    n_glu = w_glu.shape[2]
    xs = x[0]
    ps = p[0, 0]
    tgt = loss_target[0]

    (win_t3,) = _all_gather([w_in[0].T.astype(BF16)], "gather_w_in")
    win_t = win_t3.reshape(D_IN, D)
    later_weights = _Carry("gather", [w_glu[0].astype(BF16), w_out[0].astype(BF16), w_ple_gate[0].astype(BF16),
                                      w_ple_proj[0].astype(BF16)])

    ssm_prm = (ssm_a_re[0], ssm_a_im[0], ssm_log_dt[0].reshape(2, G, 1),
               ssm_b_re[0].transpose(0, 1, 3, 2), ssm_b_im[0].transpose(0, 1, 3, 2), ssm_c_re[0], ssm_c_im[0])

    cos, sin = _rope_tables(L)
    hn = _norm_in(xs, norm_mix, "norm_mix")
    ZT = 512
    zp_tile = lambda j: jnp.where(j < 2, j, jnp.where(j < D_IN // ZT - 1, j + 1, 2))
    z = _mm(hn, win_t, "nt", "in_proj", out_dtype=BF16, n_tiles=(ZT, zp_tile))
    qr, kr, vb, kt = _qkv_prep(z, cos, sin, q_norm, k_norm)
    o, lse, (wglu3, wout3, wpg3, wpp3) = _attn_fwd(qr, kr, vb, later_weights)
    wout = wout3.reshape(-1, D)
    wpg = wpg3.reshape(-1, D)
    u_off = 2 * D_ATTN
    u_perm = _seg_perm(z[:, u_off:u_off + D_SSM])
    ys_perm, ssm_ent = _ssm_fwd(u_perm, ssm_prm, ssm_d)
    ys = _seg_unperm(ys_perm)

    tm = _pick(L, (256,))
    wglu = wglu3.transpose(1, 0, 2).reshape(D_SSM, 2 * D_SSM)

    def mix_post(prod, gy_tile, o_ref, ga_ref, gs_ref, b_ref):
        glu_b = (prod + b_ref[...]).astype(BF16)
        gluv = glu_b.astype(F32)
        ga, gs = _f32(ga_ref), _f32(gs_ref)
        y_attn = _f32(o_ref) * ga * _sigmoid(ga)
        y_ssm = gluv[:, :D_SSM] * _sigmoid(gluv[:, D_SSM:]) * gs * _sigmoid(gs)
        return gy_tile, glu_b, jnp.concatenate([y_attn, y_ssm], axis=-1)

    gy, glu, cat = _mm_rows(ys, wglu, "nn", "mix", mix_post, [o, (z, D_ATTN, 1), (z, D_SSM, 3)], [b_glu],
                            [(D_SSM, BF16), (2 * D_SSM, BF16), (D_ATTN + D_SSM, BF16)],
                            pre=lambda y: _gelu(y).astype(BF16))

    def out_post(prod, x_ref, g_ref):
        h1v = prod + x_ref[...]
        return h1v, h1v * _rms(h1v) * g_ref[...]

    h1, n2 = _mm_rows(cat, wout, "nn", "out_proj", out_post, [xs], [norm_ple], [(D, F32), (D, BF16)])
    pb = ps.astype(BF16)
    pp = _mm(pb, wpp3, "nn", "ple_proj", out_dtype=BF16, b_blk=True)

    nf = norm_final.reshape(1, D)

    def tail_post(gp, h1_ref, pp_ref, t_ref, g_ref):
        gate = _sigmoid(gp)
        ppv = _f32(pp_ref)
        h2 = h1_ref[...] + gate * ppv
        r = _rms(h2)
        hh = h2 * r
        err = hh * g_ref[...] - t_ref[...]
        loss_part = jnp.broadcast_to(0.5 * jnp.sum(jnp.mean(err * err, axis=-1, keepdims=True)), (1, LANES))
        dy = err * (1.0 / D)
        dh2 = _rms_bwd(dy, hh, r, g_ref[...])
        return dh2, dh2 * gate, dh2 * ppv * gate * (1.0 - gate), loss_part, _colsum(dy * hh)

    dh2, dpp, dsg, loss_acc, d_nf = _mm_rows(n2, wpg, "nn", "tail", tail_post, [h1, pp, tgt], [nf],
                                             [(D, F32), (D, BF16), (D, BF16)], [(1, LANES), (1, D)], vmem=58 << 20)

    g_wpp3 = _mm(pb, dpp, "tn", "d_ple_proj", out_dtype=BF16, out_blk=n_pp)
    g_wpg = _mm(n2, dsg, "tn", "d_ple_gate", out_dtype=BF16)
    def ple_bwd_post(dn, h1_ref, dh2_ref, g_ref):
        h1v = h1_ref[...]
        r = _rms(h1v)
        hh = h1v * r
        dh1 = dh2_ref[...] + _rms_bwd(dn, hh, r, g_ref[...])
        return dh1, dh1, _colsum(dn * hh)

    dh1, dh1b, d_nple = _mm_rows(dsg, wpg, "nt", "ple_bwd", ple_bwd_post, [h1, dh2], [norm_ple],
                                 [(D, F32), (D, BF16)], [(1, D)])

    g_wout = _mm(cat, dh1b, "tn", "d_out_proj", out_dtype=BF16)

    def mix_bwd_post(dcat, o_ref, ga_ref, glu_ref, gs_ref):
        dca, dcs = dcat[:, :D_ATTN], dcat[:, D_ATTN:]
        ga, gs = _f32(ga_ref), _f32(gs_ref)
        gla, glb = glu_ref[:, :D_SSM].astype(F32), glu_ref[:, D_SSM:].astype(F32)
        sa, ss, sb = _sigmoid(ga), _sigmoid(gs), _sigmoid(glb)
        do = dca * ga * sa
        dga = dca * _f32(o_ref) * sa * (1.0 + ga * (1.0 - sa))
        dgs = dcs * gla * sb * ss * (1.0 + gs * (1.0 - ss))
        dy2 = dcs * gs * ss
        da, db = dy2 * sb, dy2 * gla * sb * (1.0 - sb)
        return (do, dga, dgs, jnp.concatenate([da, db], axis=-1),
                jnp.concatenate([_colsum(da), _colsum(db)], axis=-1))

    do, dga, dgs, dglu, g_bglu = _mm_rows(
        dh1b, wout, "nt", "mix_bwd", mix_bwd_post, [o, (z, D_ATTN, 1), glu, (z, D_SSM, 3)], [],
        [(D_ATTN, BF16), (D_ATTN, BF16), (D_SSM, BF16), (2 * D_SSM, BF16)], [(1, 2 * D_SSM)])

    g_wglu3 = _mm(gy, dglu, "tn", "d_glu_proj", out_dtype=BF16, out_blk=n_glu)
    dys = _mm(dglu, wglu3, "nt", "d_ssm_out", b_blk=True,
              post=(lambda out, y: out * _gelu_grad(y), ys))
    du_perm, pg = _ssm_bwd(u_perm, _seg_perm(dys), ssm_ent, ssm_prm, ssm_d)
    du = _seg_unperm(du_perm)

    pg_send = pg.reshape(N_DEV, (n_slab // N_DEV) * 2 * PG_ROWS, SLAB_S)
    dqs, dkr, dvv, (l_wglu, l_wout, l_wpg, l_wpp, l_pg) = _attn_bwd(
        qr, kr, vb, kt, do, o, lse,
        _Carry("a2a", [g_wglu3, g_wout.reshape(N_DEV, -1, D), g_wpg.reshape(N_DEV, -1, D), g_wpp3, pg_send]))

    scale = HEAD_DIM ** -0.5
    kblk = 4 * D_ATTN // D_KV

    a0, k0, v0, u0, s0 = D_ATTN + 2 * D_KV, D_ATTN, D_ATTN + D_KV, 2 * D_ATTN + 2 * D_KV, 2 * D_ATTN + 2 * D_KV + D_SSM

    def qkv_bwd_body(dq_ref, dk_ref, dv_ref, q_ref, k_ref, cos_ref, sin_ref, qn_ref, kn_ref, dga_ref, du_ref, dgs_ref,
                     dz_ref, dqn_ref, dkn_ref):
        c, s = cos_ref[...], sin_ref[...]
        dz_ref[:, a0:a0 + D_ATTN] = dga_ref[...]
        dz_ref[:, u0:u0 + D_SSM] = du_ref[...].astype(BF16)
        dz_ref[:, s0:s0 + D_SSM] = dgs_ref[...]

        def head(g, xh, w):
            dn = g * c + _partner(g * s)
            r = _rms(xh)
            xhat = xh * r
            return _rms_bwd(dn, xhat, r, w), _colsum(dn * xhat)

        dqn = jnp.zeros((1, HEAD_DIM), F32)
        for h in range(N_HEADS):
            sl = slice(h * HEAD_DIM, (h + 1) * HEAD_DIM)
            dx, dw = head(dq_ref[:, sl] * scale, q_ref[:, sl].astype(F32), qn_ref[...])
            dz_ref[:, sl] = dx.astype(BF16)
            dqn = dqn + dw
        dkn = jnp.zeros((1, HEAD_DIM), F32)
        for h in range(N_KV):
            sl = slice(h * HEAD_DIM, (h + 1) * HEAD_DIM)
            dx, dw = head(dk_ref[:, sl], k_ref[:, sl].astype(F32), kn_ref[...])
            dz_ref[:, k0 + h * HEAD_DIM:k0 + (h + 1) * HEAD_DIM] = dx.astype(BF16)
            dkn = dkn + dw
        dz_ref[:, v0:v0 + D_KV] = dv_ref[...].astype(BF16)
        _acc(dqn_ref, dqn)
        _acc(dkn_ref, dkn)

    dz, g_qn, g_kn = _rowcall(
        qkv_bwd_body, "qkv_bwd", L, tm,
        [(dqs, _rspec(tm, D_ATTN)), (dkr, _rspec(tm, D_KV)), (dvv, _rspec(tm, D_KV)),
         (z, _rspec(tm, D_ATTN, 0)), (z, _rspec(tm, D_KV, kblk)), (cos, _rspec(tm, HEAD_DIM)),
         (sin, _rspec(tm, HEAD_DIM)), (q_norm, _fspec(q_norm.shape)), (k_norm, _fspec(k_norm.shape)),
         (dga, _rspec(tm, D_ATTN)), (du, _rspec(tm, D_SSM)), (dgs, _rspec(tm, D_SSM))],
        [(D_IN, BF16)], [(1, HEAD_DIM), (1, HEAD_DIM)])

    g_win_t = _mm(dz, hn, "tn", "d_in_proj", out_dtype=BF16)
    g_win8 = g_win_t.reshape(N_DEV, n_in, D)
    from_sibling = _sibling_swap(g_win8, "swap_d_w_in")
    pair = _pair_sum(g_win8, from_sibling, "pair_sum_d_w_in")
    dhn, (l_win_t,) = _mm(dz, win_t, "nn", "d_norm_mix_in", out_dtype=BF16,
                          carry=_Carry("a2a_chips", [pair]))

    def in_bwd_body(x_ref, dn_ref, dh1_ref, g_ref, dx_ref, dg_ref):
        xv = x_ref[...]
        r = _rms(xv)
        hh = xv * r
        dn = _f32(dn_ref)
        _acc(dg_ref, _colsum(dn * hh))
        dx_ref[...] = dh1_ref[...] + _rms_bwd(dn, hh, r, g_ref[...])

    grad_x, g_nmix = _rowcall(
        in_bwd_body, "in_bwd", L, tm,
        [(xs, _rspec(tm, D)), (dhn, _rspec(tm, D)), (dh1, _rspec(tm, D)), (norm_mix, _fspec(norm_mix.shape))],
        [(D, F32)], [(1, D)])

    tiny_parts = [g_nmix, g_bglu, d_nple, d_nf, g_qn, g_kn, loss_acc[:, :1]]
    tiny_flat = jnp.concatenate([t.reshape(-1) for t in tiny_parts])
    tiny_rows = -(-tiny_flat.shape[0] // (8 * LANES)) * 8
    tiny = jnp.pad(tiny_flat, (0, tiny_rows * LANES - tiny_flat.shape[0])).reshape(tiny_rows, LANES)
    pg_sum = _sum_blocks(l_pg, "sum_ssm_grads")
    pg_all, tiny_all = _all_gather([pg_sum, tiny], "gather_small_grads")
    (g_bt_re, g_bt_im, g_c_re, g_c_im, g_a_re, g_a_im, g_ldt, g_skip) = _ssm_param_grads(
        pg_all.reshape(n_slab, 2, PG_ROWS, SLAB_S), ssm_prm)
    tiny_sum = _sum_blocks(tiny_all, "sum_tiny_grads").reshape(-1)
    tiny_grads, off = [], 0
    for t in tiny_parts:
        tiny_grads.append(tiny_sum[off:off + t.size].reshape(t.shape))
        off += t.size
    r_nmix, r_bglu, r_nple, r_nf, r_qn, r_kn, loss = tiny_grads
    loss = loss.reshape(())

    grads, deltas, new_ms, new_vs = {}, {}, {}, {}
    outs = _adamw_reduce(w_in[0].T, l_win_t, m_w_in[0].T, v_w_in[0].T, "adamw_w_in")
    grads["w_in"], deltas["w_in"], new_ms["w_in"], new_vs["w_in"] = [t.T[None] for t in outs]
    big = [("w_glu", w_glu, l_wglu, m_w_glu, v_w_glu),
           ("w_out", w_out, l_wout, m_w_out, v_w_out), ("w_ple_gate", w_ple_gate, l_wpg, m_w_ple_gate, v_w_ple_gate),
           ("w_ple_proj", w_ple_proj, l_wpp, m_w_ple_proj, v_w_ple_proj)]
    for name, w, ld, m, v in big:
        shp = w.shape
        outs = _adamw_reduce(w[0], ld, m[0], v[0], "adamw_" + name)
        grads[name], deltas[name], new_ms[name], new_vs[name] = [t.reshape(shp) for t in outs]
    bt2 = (2 * G * SSM_H, SSM_P)
    for name, w, g, m, v in (("ssm_b_re", ssm_b_re, g_bt_re, m_ssm_b_re, v_ssm_b_re),
                             ("ssm_b_im", ssm_b_im, g_bt_im, m_ssm_b_im, v_ssm_b_im)):
        to2 = lambda t: t[0].transpose(0, 1, 3, 2).reshape(bt2)
        back = lambda t: t.reshape(2, G, SSM_H, SSM_P).transpose(0, 1, 3, 2)[None]
        outs = _adamw(to2(w), g.reshape(bt2), to2(m), to2(v), "adamw_" + name)
        grads[name] = back(g)
        deltas[name], new_ms[name], new_vs[name] = [back(t) for t in outs]
    small = [("norm_mix", norm_mix, r_nmix, m_norm_mix, v_norm_mix, (1, D)),
             ("q_norm", q_norm, r_qn, m_q_norm, v_q_norm, (1, HEAD_DIM)),
             ("k_norm", k_norm, r_kn, m_k_norm, v_k_norm, (1, HEAD_DIM)),
             ("ssm_a_re", ssm_a_re, g_a_re, m_ssm_a_re, v_ssm_a_re, (2 * G, SSM_P)),
             ("ssm_a_im", ssm_a_im, g_a_im, m_ssm_a_im, v_ssm_a_im, (2 * G, SSM_P)),
             ("ssm_log_dt", ssm_log_dt, g_ldt, m_ssm_log_dt, v_ssm_log_dt, (2, G)),
             ("ssm_c_re", ssm_c_re, g_c_re, m_ssm_c_re, v_ssm_c_re, (2 * G * SSM_H, SSM_P)),
             ("ssm_c_im", ssm_c_im, g_c_im, m_ssm_c_im, v_ssm_c_im, (2 * G * SSM_H, SSM_P)),
             ("ssm_d", ssm_d, g_skip, m_ssm_d, v_ssm_d, (1, D_SSM)),
             ("b_glu", b_glu, r_bglu, m_b_glu, v_b_glu, (1, 2 * D_SSM)),
             ("norm_ple", norm_ple, r_nple, m_norm_ple, v_norm_ple, (1, D)),
             ("norm_final", norm_final, r_nf, m_norm_final, v_norm_final, (1, D))]
    group = [it for it in small if it[5][0] * it[5][1] <= (1 << 14)]
    grouped = {it[0] for it in group}
    for name, w, g, m, v, s2 in small:
        if name in grouped:
            continue
        shp = w.shape
        outs = _adamw(w.reshape(s2), g.reshape(s2), m.reshape(s2), v.reshape(s2), "adamw_" + name)
        grads[name] = g.reshape(shp)
        deltas[name], new_ms[name], new_vs[name] = [t.reshape(shp) for t in outs]
    ds, mns, vns = _adamw_group(*[[it[i].reshape(it[5]) for it in group] for i in (1, 2, 3, 4)], "adamw_tiny")
    for (name, w, g, _, _, _), d_, m_, v_ in zip(group, ds, mns, vns):
        shp = w.shape
        grads[name] = g.reshape(shp)
        deltas[name], new_ms[name], new_vs[name] = d_.reshape(shp), m_.reshape(shp), v_.reshape(shp)

    order = ["norm_mix", "w_in", "q_norm", "k_norm", "ssm_a_re", "ssm_a_im", "ssm_log_dt", "ssm_b_re", "ssm_b_im",
             "ssm_c_re", "ssm_c_im", "ssm_d", "w_glu", "b_glu", "w_out", "norm_ple", "w_ple_gate", "w_ple_proj",
             "norm_final"]
    return (loss, grad_x[None], *[grads[k] for k in order], *[deltas[k] for k in order],
            *[new_ms[k] for k in order], *[new_vs[k] for k in order])
```

```python
import functools
import math

import numpy as np
import jax
import jax.numpy as jnp
from jax import lax
from jax.experimental import pallas as pl
from jax.experimental.pallas import tpu as pltpu

F32 = jnp.float32
BF16 = jnp.bfloat16

N_DEV = 8
N_CHIPS = 4
EPS = 1e-6
GRID_W = 64
ROPE_THETA = 10000.0
HEAD_DIM = 128
N_HEADS = 8
N_KV = 2
REP = N_HEADS // N_KV
D_ATTN = N_HEADS * HEAD_DIM
D_KV = N_KV * HEAD_DIM
SSM_H = 16
SSM_P = 64
SLAB = 128
SLAB_G = SLAB // SSM_H
SLAB_S = SLAB_G * SSM_P
SEG = 8
CHAINS = 2
SCAN_UNROLL = 8
LANES = 128
PG_ROWS = 72
VMEM_LIMIT = 48 << 20

ADAM_LR = 0.001
ADAM_B1 = 0.9
ADAM_B2 = 0.999
ADAM_EPS = 1e-08
ADAM_WD = 0.01
ADAM_STEP = 10


def _pick(n, cands):
    for c in cands:
        if n % c == 0:
            return c
    return n


def _cparams(sem, vmem=VMEM_LIMIT):
    return pltpu.CompilerParams(dimension_semantics=sem, vmem_limit_bytes=vmem)


class _Carry:
    def __init__(self, kind, xs):
        self.kind, self.xs, self.n = kind, list(xs), len(xs)
        self.ks = (2, 4, 6) if kind == "a2a_chips" else tuple(range(1, N_DEV))
        self.index = _chip_index if kind == "a2a_chips" else _dev_index
        lead = (N_DEV,) if kind == "gather" else ()
        self.out_shape = [jax.ShapeDtypeStruct(lead + v.shape, v.dtype) for v in xs]
        self.specs = [pl.BlockSpec(memory_space=pl.ANY)] * self.n
        self.scratch = [pltpu.SemaphoreType.DMA((self.n, len(self.ks))), pltpu.SemaphoreType.DMA((self.n, len(self.ks))),
                        pltpu.SemaphoreType.DMA((self.n,))]

    def _copies(self, x_refs, out_refs, sems):
        send_sems, recv_sems, local_sems = sems
        x, y, c = lax.axis_index("x"), lax.axis_index("y"), lax.axis_index("c")
        me = self.index((x, y, c))
        mine, sends, arrivals = [], [], []
        for a in range(self.n):
            src_mine = x_refs[a] if self.kind == "gather" else x_refs[a].at[me]
            mine.append(pltpu.make_async_copy(src_mine, out_refs[a].at[me], local_sems.at[a]))
            for s, k in enumerate(self.ks):
                peer = _peer(k, x, y, c)
                src = x_refs[a] if self.kind == "gather" else x_refs[a].at[self.index(peer)]
                sends.append(pltpu.make_async_remote_copy(
                    src_ref=src, dst_ref=out_refs[a].at[me], send_sem=send_sems.at[a, s],
                    recv_sem=recv_sems.at[a, s], device_id=peer, device_id_type=pl.DeviceIdType.MESH))
                land = out_refs[a].at[self.index(peer)]
                arrivals.append(pltpu.make_async_remote_copy(
                    src_ref=land, dst_ref=land, send_sem=send_sems.at[a, s],
                    recv_sem=recv_sems.at[a, s], device_id=peer, device_id_type=pl.DeviceIdType.MESH))
        return mine, sends, arrivals

    def start(self, x_refs, out_refs, sems):
        mine, sends, _ = self._copies(x_refs, out_refs, sems)
        for cp in mine + sends:
            cp.start()

    def wait(self, x_refs, out_refs, sems):
        mine, sends, arrivals = self._copies(x_refs, out_refs, sems)
        for cp in arrivals:
            cp.wait_recv()
        for cp in sends:
            cp.wait_send()
        for cp in mine:
            cp.wait()


def _grid_edges(grid):
    first = functools.reduce(lambda p, q: p & q, [pl.program_id(d) == 0 for d in range(len(grid))])
    last = functools.reduce(lambda p, q: p & q, [pl.program_id(d) == g - 1 for d, g in enumerate(grid)])
    return first, last


def _mm(a, b, mode, name, out_dtype=F32, add=None, bias=None, a_blk=False, b_blk=False, out_blk=0, carry=None,
        n_tiles=None, post=None):
    w = b.shape[2] if b_blk else out_blk
    if mode == "nn":
        M, K = a.shape
        N = b.shape[0] * w if b_blk else b.shape[1]
    elif mode == "nt":
        M = a.shape[1] if a_blk else a.shape[0]
        N = b.shape[1] if b_blk else b.shape[0]
        K = b.shape[0] * w if b_blk else b.shape[1]
    else:
        K, M = a.shape
        N = b.shape[0] * w if b_blk else b.shape[1]
    tm = _pick(M, (1024, 768, 512, 256))
    tn = _pick(N, (1024, 768, 512, 256))
    if mode == "tn" and N <= 2048:
        tn = N
    tk = K if (mode != "tn" and K <= 2048) else _pick(K, (2048, 1024, 768, 512, 256) if mode == "tn"
                                                       else (1024, 768, 512, 256))
    if mode == "nn" and K > 2048 and N <= 2048:
        tn, tk = N, _pick(K, (1536, 1024, 768, 512, 256))
    perm = lambda j: j
    if n_tiles:
        tn, perm = n_tiles
        tm = _pick(M, (2048, 1024, 512, 256))
    if mode == "nt" and b_blk:
        tk = w
    elif b_blk or out_blk:
        tn = w
    if mode == "tn" and 4 * K * (tm + tn) <= (24 << 20):
        tk = K
    nk = K // tk
    grid = (M // tm, N // tn, nk)
    if mode == "nn":
        a_spec = pl.BlockSpec((tm, tk), lambda i, j, k: (i, k))
        b_spec = (pl.BlockSpec((1, tk, tn), lambda i, j, k: (j, k, 0)) if b_blk
                  else pl.BlockSpec((tk, tn), lambda i, j, k: (k, j)))
        dims = (((1,), (0,)), ((), ()))
    elif mode == "nt":
        a_spec = (pl.BlockSpec((1, tm, tk), lambda i, j, k: (k, i, 0)) if a_blk
                  else pl.BlockSpec((tm, tk), lambda i, j, k: (i, k)))
        b_spec = (pl.BlockSpec((1, tn, tk), lambda i, j, k: (k, j, 0)) if b_blk
                  else pl.BlockSpec((tn, tk), lambda i, j, k: (perm(j), k)))
        dims = (((1,), (1,)), ((), ()))
    else:
        a_spec = pl.BlockSpec((tk, tm), lambda i, j, k: (k, i))
        b_spec = (pl.BlockSpec((1, tk, tn), lambda i, j, k: (j, k, 0)) if b_blk
                  else pl.BlockSpec((tk, tn), lambda i, j, k: (k, j)))
        dims = (((0,), (0,)), ((), ()))
    if out_blk:
        out_spec = pl.BlockSpec((1, tm, tn), lambda i, j, k: (j, i, 0))
        out_shape = jax.ShapeDtypeStruct((N // tn, M, tn), out_dtype)
    else:
        out_spec = pl.BlockSpec((tm, tn), lambda i, j, k: (i, j))
        out_shape = jax.ShapeDtypeStruct((M, N), out_dtype)
    extras, extra_specs, combine = [], [], []
    if add is not None:
        extras.append(add)
        extra_specs.append(pl.BlockSpec((tm, tn), lambda i, j, k: (i, j)))
        combine.append(lambda out, t: out + t)
    if bias is not None:
        extras.append(bias)
        extra_specs.append(pl.BlockSpec((1, tn), lambda i, j, k: (0, j)))
        combine.append(lambda out, t: out + t)
    if post is not None:
        extras.append(post[1])
        extra_specs.append(pl.BlockSpec((tm, tn), lambda i, j, k: (i, j)))
        combine.append(post[0])

    n_ex = len(extras)
    nc = carry.n if carry else 0

    def body(a_ref, b_ref, *rest):
        ex_refs, cx = rest[:n_ex], rest[n_ex:n_ex + nc]
        o_ref, cout = rest[n_ex + nc], rest[n_ex + nc + 1:n_ex + 2 * nc + 1]
        tail = rest[n_ex + 2 * nc + 1:]
        sems = tail[:3] if carry else ()
        first, last = _grid_edges(grid)
        if carry:
            @pl.when(first)
            def _():
                carry.start(cx, cout, sems)

        def product():
            av = a_ref[0] if a_blk else a_ref[...]
            bv = b_ref[0] if b_blk else b_ref[...]
            return lax.dot_general(av, bv, dims, preferred_element_type=F32)

        def finish(out):
            for r, fn in zip(ex_refs, combine):
                out = fn(out, r[...])
            if out_blk:
                o_ref[0] = out.astype(out_dtype)
            else:
                o_ref[...] = out.astype(out_dtype)

        if nk == 1:
            finish(product())
        else:
            acc_ref = tail[-1]
            k = pl.program_id(2)

            @pl.when(k == 0)
            def _():
                acc_ref[...] = jnp.zeros_like(acc_ref)

            acc_ref[...] += product()

            @pl.when(k == nk - 1)
            def _():
                finish(acc_ref[...])

        if carry:
            @pl.when(last)
            def _():
                carry.wait(cx, cout, sems)

    scratch = (carry.scratch if carry else []) + ([pltpu.VMEM((tm, tn), F32)] if nk > 1 else [])
    outs = pl.pallas_call(
        body, name=name, grid=grid,
        in_specs=[a_spec, b_spec] + extra_specs + (carry.specs if carry else []),
        out_specs=[out_spec] + (carry.specs if carry else []),
        out_shape=[out_shape] + (carry.out_shape if carry else []),
        scratch_shapes=scratch,
        compiler_params=_cparams(("arbitrary", "arbitrary", "arbitrary")),
    )(a, b, *extras, *(carry.xs if carry else []))
    return (outs[0], outs[1:]) if carry else outs[0]


def _rspec(tm, w, cb=0):
    return pl.BlockSpec((tm, w), lambda i: (i, cb))


def _fspec(shape):
    nd = len(shape)
    return pl.BlockSpec(shape, lambda i: (0,) * nd)


def _rowcall(body, name, L, tm, ins, row_outs, acc_outs=()):
    out_shape = [jax.ShapeDtypeStruct((L, w), dt) for w, dt in row_outs]
    out_shape += [jax.ShapeDtypeStruct(s, F32) for s in acc_outs]
    out_specs = [_rspec(tm, w) for w, _ in row_outs] + [_fspec(s) for s in acc_outs]
    return pl.pallas_call(
        body, name=name, grid=(L // tm,),
        in_specs=[s for _, s in ins], out_specs=out_specs, out_shape=out_shape,
        compiler_params=_cparams(("arbitrary",)),
    )(*[a for a, _ in ins])


def _mm_rows(a, b, mode, name, post, row_ins, full_ins, row_outs, acc_outs=(), vmem=VMEM_LIMIT, pre=None, tm=256):
    M, K = a.shape
    N = b.shape[1] if mode == "nn" else b.shape[0]
    tm = _pick(M, (tm,))
    dims = (((1,), (0,)), ((), ())) if mode == "nn" else (((1,), (1,)), ((), ()))
    b_spec = pl.BlockSpec(b.shape, lambda i: (0, 0), pipeline_mode=pl.Buffered(1))
    n_in, n_row = len(row_ins) + len(full_ins), len(row_outs)

    def body(a_ref, b_ref, *rest):
        ins, outs = rest[:n_in], rest[n_in:]
        av = pre(a_ref[...]) if pre else a_ref[...]
        prod = lax.dot_general(av, b_ref[...], dims, preferred_element_type=F32)
        res = post(prod, av, *ins) if pre else post(prod, *ins)
        for o_ref, val in zip(outs[:n_row], res[:n_row]):
            o_ref[...] = val.astype(o_ref.dtype)
        for acc_ref, val in zip(outs[n_row:], res[n_row:]):
            _acc(acc_ref, val)

    out_shape = [jax.ShapeDtypeStruct((M, w), dt) for w, dt in row_outs]
    out_shape += [jax.ShapeDtypeStruct(s, F32) for s in acc_outs]
    row_ins = [r if isinstance(r, tuple) else (r, r.shape[1], 0) for r in row_ins]
    return pl.pallas_call(
        body, name=name, grid=(M // tm,),
        in_specs=[_rspec(tm, K), b_spec] + [_rspec(tm, w, cb) for _, w, cb in row_ins]
        + [_fspec(f.shape) for f in full_ins],
        out_specs=[_rspec(tm, w) for w, _ in row_outs] + [_fspec(s) for s in acc_outs],
        out_shape=out_shape, compiler_params=_cparams(("arbitrary",), vmem),
    )(a, b, *[r for r, _, _ in row_ins], *full_ins)


def _acc(ref, val):
    @pl.when(pl.program_id(0) == 0)
    def _():
        ref[...] = jnp.zeros_like(ref)
    ref[...] += val


def _colsum(v):
    return jnp.sum(v, axis=0, keepdims=True)


def _rms(xv):
    return lax.rsqrt(jnp.mean(xv * xv, axis=-1, keepdims=True) + EPS)


def _rms_bwd(dn, xhat, r, g):
    dng = dn * g
    return r * (dng - xhat * jnp.mean(dng * xhat, axis=-1, keepdims=True))


def _sigmoid(v):
    return jax.nn.sigmoid(v)


def _f32(ref):
    return ref[...].astype(F32)


def _partner(v):
    w = v.shape[-1]
    lane = lax.broadcasted_iota(jnp.int32, v.shape, v.ndim - 1)
    first_half = (lane % 64) < 32
    return jnp.where(first_half, pltpu.roll(v, w - 32, axis=v.ndim - 1), pltpu.roll(v, 32, axis=v.ndim - 1))


def _norm_in(x, g, name):
    L, D = x.shape
    tm = _pick(L, (512, 256))

    def body(x_ref, g_ref, o_ref):
        xv = x_ref[...]
        o_ref[...] = (xv * _rms(xv) * g_ref[...]).astype(BF16)

    return _rowcall(body, name, L, tm, [(x, _rspec(tm, D)), (g, _fspec(g.shape))], [(D, BF16)])[0]


def _rope_tables(L):
    t = np.arange(L)
    rows = (t // GRID_W).astype(np.float32)
    cols = (t % GRID_W).astype(np.float32)
    n_freq = HEAD_DIM // 4
    inv_freq = np.float32(ROPE_THETA) ** (-np.arange(n_freq, dtype=np.float32) / np.float32(n_freq))
    ar = (rows[:, None] * inv_freq[None, :]).astype(np.float32).astype(np.float64)
    ac = (cols[:, None] * inv_freq[None, :]).astype(np.float32).astype(np.float64)
    cos = np.concatenate([np.cos(ar), np.cos(ar), np.cos(ac), np.cos(ac)], axis=-1).astype(np.float32)
    sin = np.concatenate([-np.sin(ar), np.sin(ar), -np.sin(ac), np.sin(ac)], axis=-1).astype(np.float32)
    return jnp.asarray(cos), jnp.asarray(sin)


def _qkv_prep(z, cos, sin, qn, kn):
    L = z.shape[0]
    tm = _pick(L, (512, 256))
    scale = HEAD_DIM ** -0.5
    kblk = 4 * D_ATTN // D_KV

    def body(q_ref, k_ref, v_ref, cos_ref, sin_ref, qn_ref, kn_ref, qo_ref, ko_ref, vo_ref, kt_ref):
        c, s = cos_ref[...], sin_ref[...]

        def head(xh, w):
            n = xh * _rms(xh) * w
            return n * c + _partner(n) * s

        for h in range(N_HEADS):
            sl = slice(h * HEAD_DIM, (h + 1) * HEAD_DIM)
            qo_ref[:, sl] = (head(q_ref[:, sl].astype(F32), qn_ref[...]) * scale).astype(BF16)
        for h in range(N_KV):
            sl = slice(h * HEAD_DIM, (h + 1) * HEAD_DIM)
            kr = head(k_ref[:, sl].astype(F32), kn_ref[...])
            ko_ref[:, sl] = kr.astype(BF16)
            kt_ref[sl, :] = kr.T.astype(BF16)
        vo_ref[...] = v_ref[...].astype(BF16)

    return pl.pallas_call(
        body, name="qkv_prep", grid=(L // tm,),
        in_specs=[_rspec(tm, D_ATTN, 0), _rspec(tm, D_KV, kblk), _rspec(tm, D_KV, kblk + 1),
                  _rspec(tm, HEAD_DIM), _rspec(tm, HEAD_DIM), _fspec(qn.shape), _fspec(kn.shape)],
        out_specs=[_rspec(tm, D_ATTN), _rspec(tm, D_KV), _rspec(tm, D_KV),
                   pl.BlockSpec((D_KV, tm), lambda i: (0, i))],
        out_shape=[jax.ShapeDtypeStruct((L, D_ATTN), BF16), jax.ShapeDtypeStruct((L, D_KV), BF16),
                   jax.ShapeDtypeStruct((L, D_KV), BF16), jax.ShapeDtypeStruct((D_KV, L), BF16)],
        compiler_params=_cparams(("arbitrary",)),
    )(z, z, z, cos, sin, qn, kn)


def _col_to_row(col):
    n = col.shape[0]
    eye = lax.broadcasted_iota(jnp.int32, (n, n), 0) == lax.broadcasted_iota(jnp.int32, (n, n), 1)
    return jnp.sum(jnp.where(eye, col, 0.0), axis=0, keepdims=True)


def _attn_fwd(q, k, v, carry=None):
    L = q.shape[0]
    tq = _pick(L, (256, 128))
    grid = (N_HEADS, L // tq)
    nc = carry.n if carry else 0

    def body(q_ref, k_ref, v_ref, *rest):
        cx, (o_ref, lse_ref) = rest[:nc], rest[nc:nc + 2]
        cout, sems = rest[nc + 2:2 * nc + 2], rest[2 * nc + 2:]
        first, last = _grid_edges(grid)
        if carry:
            @pl.when(first)
            def _():
                carry.start(cx, cout, sems)

        s = lax.dot_general(q_ref[...], k_ref[...], (((1,), (1,)), ((), ())), preferred_element_type=F32)
        m = jnp.max(s, axis=-1, keepdims=True)
        e = jnp.exp(s - m)
        l = jnp.sum(e, axis=-1, keepdims=True)
        o_ref[...] = (jnp.dot(e.astype(BF16), v_ref[...], preferred_element_type=F32) / l).astype(BF16)
        lse_ref[0] = _col_to_row(m + jnp.log(l))

        if carry:
            @pl.when(last)
            def _():
                carry.wait(cx, cout, sems)

    outs = pl.pallas_call(
        body, name="attn_fwd", grid=grid,
        in_specs=[pl.BlockSpec((tq, HEAD_DIM), lambda h, i: (i, h)),
                  pl.BlockSpec((L, HEAD_DIM), lambda h, i: (0, h // REP)),
                  pl.BlockSpec((L, HEAD_DIM), lambda h, i: (0, h // REP))] + (carry.specs if carry else []),
        out_specs=[pl.BlockSpec((tq, HEAD_DIM), lambda h, i: (i, h)),
                   pl.BlockSpec((1, 1, tq), lambda h, i: (h, 0, i))] + (carry.specs if carry else []),
        out_shape=[jax.ShapeDtypeStruct((L, D_ATTN), BF16), jax.ShapeDtypeStruct((N_HEADS, 1, L), F32)]
        + (carry.out_shape if carry else []),
        scratch_shapes=carry.scratch if carry else [],
        compiler_params=_cparams(("arbitrary", "arbitrary")),
    )(q, k, v, *(carry.xs if carry else []))
    return outs[0], outs[1], outs[2:]


def _attn_bwd(q, k, v, kt, do, o, lse, carry=None):
    L = q.shape[0]
    tq = _pick(L, (256, 128))
    kc = _pick(L, (512, 256, 128))
    nt = (((1,), (1,)), ((), ()))
    grid = (N_KV, REP, L // tq)
    nc = carry.n if carry else 0

    def body(q_ref, do_ref, o_ref, lse_ref, k_ref, v_ref, kt_ref, *rest):
        cx, (dq_ref, dk_ref, dv_ref) = rest[:nc], rest[nc:nc + 3]
        cout, sems = rest[nc + 3:2 * nc + 3], rest[2 * nc + 3:]
        first, last = _grid_edges(grid)
        if carry:
            @pl.when(first)
            def _():
                carry.start(cx, cout, sems)

        @pl.when((pl.program_id(1) == 0) & (pl.program_id(2) == 0))
        def _():
            dk_ref[...] = jnp.zeros_like(dk_ref)
            dv_ref[...] = jnp.zeros_like(dv_ref)

        qv, dov = q_ref[...], do_ref[...]
        lse_row = lse_ref[0]
        delta = _col_to_row(jnp.sum(dov.astype(F32) * _f32(o_ref), axis=-1, keepdims=True))
        dqt = jnp.zeros((HEAD_DIM, tq), F32)
        for c in range(L // kc):
            sl = slice(c * kc, (c + 1) * kc)
            st = lax.dot_general(k_ref[sl, :], qv, nt, preferred_element_type=F32)
            pt = jnp.exp(st - lse_row)
            dpt = lax.dot_general(v_ref[sl, :], dov, nt, preferred_element_type=F32)
            dst = (pt * (dpt - delta)).astype(BF16)
            dv_ref[sl, :] += jnp.dot(pt.astype(BF16), dov, preferred_element_type=F32)
            dk_ref[sl, :] += jnp.dot(dst, qv, preferred_element_type=F32)
            dqt = dqt + jnp.dot(kt_ref[:, sl], dst, preferred_element_type=F32)
        dq_ref[...] = dqt.T

        if carry:
            @pl.when(last)
            def _():
                carry.wait(cx, cout, sems)

    head = lambda g, r, i: (i, g * REP + r)
    outs = pl.pallas_call(
        body, name="attn_bwd", grid=grid,
        in_specs=[pl.BlockSpec((tq, HEAD_DIM), head), pl.BlockSpec((tq, HEAD_DIM), head),
                  pl.BlockSpec((tq, HEAD_DIM), head),
                  pl.BlockSpec((1, 1, tq), lambda g, r, i: (g * REP + r, 0, i)),
                  pl.BlockSpec((L, HEAD_DIM), lambda g, r, i: (0, g)),
                  pl.BlockSpec((L, HEAD_DIM), lambda g, r, i: (0, g)),
                  pl.BlockSpec((HEAD_DIM, L), lambda g, r, i: (g, 0))] + (carry.specs if carry else []),
        out_specs=[pl.BlockSpec((tq, HEAD_DIM), head),
                   pl.BlockSpec((L, HEAD_DIM), lambda g, r, i: (0, g)),
                   pl.BlockSpec((L, HEAD_DIM), lambda g, r, i: (0, g))] + (carry.specs if carry else []),
        out_shape=[jax.ShapeDtypeStruct((L, D_ATTN), F32), jax.ShapeDtypeStruct((L, D_KV), F32),
                   jax.ShapeDtypeStruct((L, D_KV), F32)] + (carry.out_shape if carry else []),
        scratch_shapes=carry.scratch if carry else [],
        compiler_params=_cparams(("arbitrary", "arbitrary", "arbitrary")),
    )(q, do, o, lse, k, v, kt, *(carry.xs if carry else []))
    return outs[0], outs[1], outs[2], outs[3:]


def _seg_perm(a):
    L, C = a.shape
    return a.reshape(SEG, L // SEG, C).transpose(1, 0, 2).reshape(L, C)


def _seg_unperm(a):
    L, C = a.shape
    return a.reshape(L // SEG, SEG, C).transpose(1, 0, 2).reshape(L, C)


def _cmul(ar, ai, br, bi):
    return ar * br - ai * bi, ar * bi + ai * br


def _rows8(rr):
    if isinstance(rr, int):
        return pl.ds(rr * SEG, SEG)
    return pl.ds(pl.multiple_of(rr * SEG, SEG), SEG)


def _seg_scan(xr_ref, xi_ref, ar, ai, reverse, n_rows, visit=None, visit_init=(), entering=None):
    shape = ar.shape
    zero = jnp.zeros(shape, F32)
    rc = n_rows // CHAINS

    def index(q):
        return (n_rows - 1 - q) if reverse else q

    if entering is None:
        def ends(q, carry):
            out = []
            for j in range(CHAINS):
                sl = _rows8(index(j * rc + q))
                pr, pi = _cmul(ar, ai, carry[2 * j], carry[2 * j + 1])
                out += [pr + xr_ref[sl, :], pi + xi_ref[sl, :]]
            return tuple(out)

        def ends_block(qb, carry):
            for t in range(SCAN_UNROLL):
                carry = ends(qb * SCAN_UNROLL + t, carry)
            return carry

        e = lax.fori_loop(0, rc // SCAN_UNROLL, ends_block, (zero,) * (2 * CHAINS))

        pr, pi = ar, ai
        for _ in range(int(math.log2(rc))):
            pr, pi = _cmul(pr, pi, pr, pi)
        sub = lax.broadcasted_iota(jnp.int32, shape, 0)
        shift = (SEG - 1) if reverse else 1
        edge = (SEG - 1) if reverse else 0
        entering = [(zero, zero)] * CHAINS
        for _ in range(SEG):
            tr, ti = _cmul(pr, pi, *entering[CHAINS - 1])
            cur = (jnp.where(sub == edge, 0.0, pltpu.roll(tr + e[2 * CHAINS - 2], shift, axis=0)),
                   jnp.where(sub == edge, 0.0, pltpu.roll(ti + e[2 * CHAINS - 1], shift, axis=0)))
            entering = [cur]
            for j in range(1, CHAINS):
                tr, ti = _cmul(pr, pi, *cur)
                cur = (tr + e[2 * j - 2], ti + e[2 * j - 1])
                entering.append(cur)

    def step(q, carry, last):
        out, acc = [], carry[2 * CHAINS:]
        for j in range(CHAINS):
            rr = index(j * rc + q)
            sl = _rows8(rr)
            pr, pi = _cmul(ar, ai, carry[2 * j], carry[2 * j + 1])
            nr, ni = pr + xr_ref[sl, :], pi + xi_ref[sl, :]
            xr_ref[sl, :] = nr
            xi_ref[sl, :] = ni
            if visit:
                acc = visit(rr, nr, ni, acc, last and j == CHAINS - 1)
            out += [nr, ni]
        return (*out, *acc)

    def step_block(qb, carry):
        for t in range(SCAN_UNROLL):
            carry = step(qb * SCAN_UNROLL + t, carry, False)
        return carry

    start = tuple(v for pair in entering for v in pair)
    n_blocks = (rc - 1) // SCAN_UNROLL
    carry = lax.fori_loop(0, n_blocks, step_block, (*start, *visit_init))
    for q in range(n_blocks * SCAN_UNROLL, rc - 1):
        carry = step(q, carry, False)
    carry = step(rc - 1, carry, True)
    return entering, carry[2 * CHAINS:]


def _discretise(a_re, a_im, ldt):
    lr = jnp.minimum(a_re, -1e-4)
    li = a_im
    dt = jnp.exp(ldt)
    mag = jnp.exp(lr * dt)
    lbr = mag * jnp.cos(li * dt)
    lbi = mag * jnp.sin(li * dt)
    den = lr * lr + li * li
    nr = lbr - 1.0
    fr = (nr * lr + lbi * li) / den
    fi = (lbi * lr - nr * li) / den
    return lr, li, dt, lbr, lbi, fr, fi


def _lane_row(v):
    return jnp.concatenate([v[g:g + 1, :] for g in range(v.shape[0])], axis=1)


def _ssm_fill_maps(d, prm, tmp_ref, maps):
    a_re_ref, a_im_ref, ldt_ref, bt_re_ref, bt_im_ref, c_re_ref, c_im_ref = prm
    _, _, _, lbr, lbi, fr, fi = _discretise(a_re_ref[d], a_im_ref[d], ldt_ref[d])

    def fill(dst, piece):
        tmp_ref[...] = jnp.zeros_like(tmp_ref)
        for g in range(SLAB_G):
            tmp_ref[g * SSM_H:(g + 1) * SSM_H, g * SSM_P:(g + 1) * SSM_P] = piece(g)
        dst[...] = tmp_ref[...].astype(BF16)

    wbr, wbi, wcr, wci = maps
    fill(wbr, lambda g: fr[g:g + 1] * bt_re_ref[d, g] - fi[g:g + 1] * bt_im_ref[d, g])
    fill(wbi, lambda g: fr[g:g + 1] * bt_im_ref[d, g] + fi[g:g + 1] * bt_re_ref[d, g])
    fill(wcr, lambda g: c_re_ref[d, g])
    fill(wci, lambda g: c_im_ref[d, g])
    return _lane_row(lbr), _lane_row(lbi)


def _ssm_param_specs():
    pole = pl.BlockSpec((2, SLAB_G, SSM_P), lambda j: (0, j, 0))
    step = pl.BlockSpec((2, SLAB_G, 1), lambda j: (0, j, 0))
    mat = pl.BlockSpec((2, SLAB_G, SSM_H, SSM_P), lambda j: (0, j, 0, 0))
    return [pole, pole, step, mat, mat, mat, mat]


_MAP_SCRATCH = [pltpu.VMEM((SLAB, SLAB_S), F32)] + [pltpu.VMEM((SLAB, SLAB_S), BF16)] * 4
_ENT_SPEC = pl.BlockSpec((1, 2, 2 * CHAINS, SEG, SLAB_S), lambda j: (j, 0, 0, 0, 0))
_NT = (((1,), (1,)), ((), ()))


def _ssm_fwd(u, prm, dskip):
    L, C = u.shape
    n_rows = L // SEG
    tc = _pick(L, (2048, 1024, 512, 256))
    u_spec = pl.BlockSpec((L, SLAB), lambda j: (0, j))
    d_spec = pl.BlockSpec((1, SLAB), lambda j: (0, j))

    def body(u_ref, *rest):
        prm_refs, d_ref, y_ref, ent_ref = rest[:7], rest[7], rest[8], rest[9]
        tmp_ref, maps, xr_ref, xi_ref = rest[10], rest[11:15], rest[15], rest[16]
        wbr, wbi, wcr, wci = maps
        y_ref[...] = u_ref[...] * d_ref[...]
        for d in range(2):
            lam_r, lam_i = _ssm_fill_maps(d, prm_refs, tmp_ref, maps)

            def inp(c, _):
                sl = pl.ds(pl.multiple_of(c * tc, tc), tc)
                ub = u_ref[sl, :].astype(BF16)
                xr_ref[sl, :] = jnp.dot(ub, wbr[...], preferred_element_type=F32)
                xi_ref[sl, :] = jnp.dot(ub, wbi[...], preferred_element_type=F32)
                return 0

            lax.fori_loop(0, L // tc, inp, 0)
            ar = jnp.broadcast_to(lam_r, (SEG, SLAB_S))
            ai = jnp.broadcast_to(lam_i, (SEG, SLAB_S))
            entering, _ = _seg_scan(xr_ref, xi_ref, ar, ai, d == 1, n_rows)
            for j, (er, ei) in enumerate(entering):
                ent_ref[0, d, 2 * j] = er
                ent_ref[0, d, 2 * j + 1] = ei

            def outp(c, _):
                sl = pl.ds(pl.multiple_of(c * tc, tc), tc)
                y_ref[sl, :] += (
                    lax.dot_general(xr_ref[sl, :].astype(BF16), wcr[...], _NT, preferred_element_type=F32)
                    - lax.dot_general(xi_ref[sl, :].astype(BF16), wci[...], _NT, preferred_element_type=F32))
                return 0

            lax.fori_loop(0, L // tc, outp, 0)

    return pl.pallas_call(
        body, name="ssm_fwd", grid=(C // SLAB,),
        in_specs=[u_spec] + _ssm_param_specs() + [d_spec],
        out_specs=[u_spec, _ENT_SPEC],
        out_shape=[jax.ShapeDtypeStruct((L, C), F32),
                   jax.ShapeDtypeStruct((C // SLAB, 2, 2 * CHAINS, SEG, SLAB_S), F32)],
        scratch_shapes=_MAP_SCRATCH + [pltpu.VMEM((L, SLAB_S), F32)] * 2,
        compiler_params=_cparams(("arbitrary",)),
    )(u, *prm, dskip)


def _ssm_bwd(u, dy, ent, prm, dskip):
    L, C = u.shape
    n_rows = L // SEG
    n_slab = C // SLAB
    tc = _pick(L, (2048, 1024, 512, 256))
    u_spec = pl.BlockSpec((L, SLAB), lambda j: (0, j))
    d_spec = pl.BlockSpec((1, SLAB), lambda j: (0, j))
    pg_spec = pl.BlockSpec((1, 2, PG_ROWS, SLAB_S), lambda j: (j, 0, 0, 0))

    def body(u_ref, dy_ref, ent_ref, *rest):
        prm_refs, d_ref, du_ref, pg_ref = rest[:7], rest[7], rest[8], rest[9]
        tmp_ref, maps, acc_ref = rest[10], rest[11:15], rest[15]
        xr_ref, xi_ref, gr_ref, gi_ref = rest[16:20]
        wbr, wbi, wcr, wci = maps
        du_ref[...] = dy_ref[...] * d_ref[...]
        pg_ref[...] = jnp.zeros_like(pg_ref)
        pg_ref[0, 0, 66:67, 0:SLAB] = _colsum(dy_ref[...] * u_ref[...])
        for d in range(2):
            lam_r, lam_i = _ssm_fill_maps(d, prm_refs, tmp_ref, maps)

            def inp(c, _):
                sl = pl.ds(pl.multiple_of(c * tc, tc), tc)
                ub = u_ref[sl, :].astype(BF16)
                dyb = dy_ref[sl, :].astype(BF16)
                xr_ref[sl, :] = jnp.dot(ub, wbr[...], preferred_element_type=F32)
                xi_ref[sl, :] = jnp.dot(ub, wbi[...], preferred_element_type=F32)
                gr_ref[sl, :] = jnp.dot(dyb, wcr[...], preferred_element_type=F32)
                gi_ref[sl, :] = -jnp.dot(dyb, wci[...], preferred_element_type=F32)
                return 0

            lax.fori_loop(0, L // tc, inp, 0)
            ar = jnp.broadcast_to(lam_r, (SEG, SLAB_S))
            ai = jnp.broadcast_to(lam_i, (SEG, SLAB_S))
            entering = [(ent_ref[0, d, 2 * j], ent_ref[0, d, 2 * j + 1]) for j in range(CHAINS)]
            _seg_scan(xr_ref, xi_ref, ar, ai, d == 1, n_rows, entering=entering)

            def pole(rr, lr, li, acc, last):
                if last:
                    pr, pi = entering[0]
                else:
                    nb = _rows8(rr + 1 if d == 1 else rr - 1)
                    pr, pi = xr_ref[nb, :], xi_ref[nb, :]
                return acc[0] + lr * pr + li * pi, acc[1] + li * pr - lr * pi

            zero = jnp.zeros((SEG, SLAB_S), F32)
            _, (accr, acci) = _seg_scan(gr_ref, gi_ref, ar, -ai, d == 0, n_rows, pole, (zero, zero))
            pg_ref[0, d, 64:65, :] = _colsum(accr)
            pg_ref[0, d, 65:66, :] = _colsum(acci)

            acc_ref[...] = jnp.zeros_like(acc_ref)

            def outp(c, _):
                sl = pl.ds(pl.multiple_of(c * tc, tc), tc)
                lrb, lib = gr_ref[sl, :].astype(BF16), gi_ref[sl, :].astype(BF16)
                du_ref[sl, :] += (lax.dot_general(lrb, wbr[...], _NT, preferred_element_type=F32)
                                  + lax.dot_general(lib, wbi[...], _NT, preferred_element_type=F32))
                ut = u_ref[sl, :].astype(F32).T.astype(BF16)
                dyt = dy_ref[sl, :].T.astype(BF16)
                acc_ref[0] += jnp.dot(ut, lrb, preferred_element_type=F32)
                acc_ref[1] += jnp.dot(ut, lib, preferred_element_type=F32)
                acc_ref[2] += jnp.dot(dyt, xr_ref[sl, :].astype(BF16), preferred_element_type=F32)
                acc_ref[3] -= jnp.dot(dyt, xi_ref[sl, :].astype(BF16), preferred_element_type=F32)
                return 0

            lax.fori_loop(0, L // tc, outp, 0)
            for m in range(4):
                for g in range(SLAB_G):
                    lanes = slice(g * SSM_P, (g + 1) * SSM_P)
                    pg_ref[0, d, m * SSM_H:(m + 1) * SSM_H, lanes] = acc_ref[m, g * SSM_H:(g + 1) * SSM_H, lanes]

    return pl.pallas_call(
        body, name="ssm_bwd", grid=(n_slab,),
        in_specs=[u_spec, u_spec, _ENT_SPEC] + _ssm_param_specs() + [d_spec],
        out_specs=[u_spec, pg_spec],
        out_shape=[jax.ShapeDtypeStruct((L, C), F32), jax.ShapeDtypeStruct((n_slab, 2, PG_ROWS, SLAB_S), F32)],
        scratch_shapes=_MAP_SCRATCH + [pltpu.VMEM((4, SLAB, SLAB_S), F32)] + [pltpu.VMEM((L, SLAB_S), F32)] * 4,
        compiler_params=_cparams(("arbitrary",), 60 << 20),
    )(u, dy, ent, *prm, dskip)


def _ssm_param_grads(pg, prm):
    n_slab = pg.shape[0]
    G = n_slab * SLAB_G
    pg_spec = pl.BlockSpec((1, 2, PG_ROWS, SLAB_S), lambda j: (j, 0, 0, 0))
    pole, _, step, mat = _ssm_param_specs()[:4]

    def body(pg_ref, a_re_ref, a_im_ref, ldt_ref, bt_re_ref, bt_im_ref,
             dbr_ref, dbi_ref, dcr_ref, dci_ref, dar_ref, dai_ref, dldt_ref, dd_ref):
        dd_ref[...] = pg_ref[0, 0, 66:67, 0:SLAB]
        for d in range(2):
            a_r = a_re_ref[d]
            lr, li, dt, lbr, lbi, f_r, f_i = _discretise(a_r, a_im_ref[d], ldt_ref[d])
            gfr_rows, gfi_rows, glr_rows, gli_rows = [], [], [], []
            for g in range(SLAB_G):
                lanes = slice(g * SSM_P, (g + 1) * SSM_P)
                gbr, gbi = pg_ref[0, d, 0:SSM_H, lanes], pg_ref[0, d, SSM_H:2 * SSM_H, lanes]
                b_r, b_i = bt_re_ref[d, g], bt_im_ref[d, g]
                fr, fi = f_r[g:g + 1], f_i[g:g + 1]
                dbr_ref[d, g] = fr * gbr + fi * gbi
                dbi_ref[d, g] = fr * gbi - fi * gbr
                gfr_rows.append(_colsum(gbr * b_r + gbi * b_i))
                gfi_rows.append(_colsum(gbi * b_r - gbr * b_i))
                dcr_ref[d, g] = pg_ref[0, d, 2 * SSM_H:3 * SSM_H, lanes]
                dci_ref[d, g] = pg_ref[0, d, 3 * SSM_H:4 * SSM_H, lanes]
                glr_rows.append(pg_ref[0, d, 64:65, lanes])
                gli_rows.append(pg_ref[0, d, 65:66, lanes])
            gfr, gfi = jnp.concatenate(gfr_rows, axis=0), jnp.concatenate(gfi_rows, axis=0)
            glr, gli = jnp.concatenate(glr_rows, axis=0), jnp.concatenate(gli_rows, axis=0)
            den = lr * lr + li * li
            ir, ii = lr / den, -li / den
            tr, ti = _cmul(ir, -ii, gfr, gfi)
            glbr, glbi = glr + tr, gli + ti
            qr, qi = _cmul(f_r, f_i, ir, ii)
            dlr, dli = _cmul(-qr, qi, gfr, gfi)
            zr, zi = _cmul(lbr, -lbi, glbr, glbi)
            dlr = dlr + dt * zr
            dli = dli + dt * zi
            dar_ref[d] = jnp.where(a_r < -1e-4, dlr, jnp.where(a_r == -1e-4, 0.5 * dlr, 0.0))
            dai_ref[d] = dli
            dldt_ref[d] = jnp.sum(lr * zr + li * zi, axis=-1, keepdims=True) * dt

    a_re, a_im, ldt, bt_re, bt_im = prm[:5]
    mshape = jax.ShapeDtypeStruct(bt_re.shape, F32)
    pshape = jax.ShapeDtypeStruct(a_re.shape, F32)
    return pl.pallas_call(
        body, name="ssm_param_grads", grid=(n_slab,),
        in_specs=[pg_spec, pole, pole, step, mat, mat],
        out_specs=[mat, mat, mat, mat, pole, pole, step, pl.BlockSpec((1, SLAB), lambda j: (0, j))],
        out_shape=[mshape, mshape, mshape, mshape, pshape, pshape, jax.ShapeDtypeStruct(ldt.shape, F32),
                   jax.ShapeDtypeStruct((1, n_slab * SLAB), F32)],
        compiler_params=_cparams(("arbitrary",)),
    )(pg, a_re, a_im, ldt, bt_re, bt_im)


def _peer(k, x, y, c):
    return (1 - x if k & 4 else x, 1 - y if k & 2 else y, 1 - c if k & 1 else c)


def _dev_index(pos):
    return 4 * pos[0] + 2 * pos[1] + pos[2]


def _chip_index(pos):
    return 2 * pos[0] + pos[1]


def _sibling_swap(x, name):
    any_spec = pl.BlockSpec(memory_space=pl.ANY)

    def body(x_ref, out_ref, send_sems, recv_sems):
        x_, y_, c_ = lax.axis_index("x"), lax.axis_index("y"), lax.axis_index("c")
        copies = [pltpu.make_async_remote_copy(
            src_ref=x_ref.at[2 * chip + (1 - c_)], dst_ref=out_ref.at[chip], send_sem=send_sems.at[chip],
            recv_sem=recv_sems.at[chip], device_id=(x_, y_, 1 - c_), device_id_type=pl.DeviceIdType.MESH)
            for chip in range(N_CHIPS)]
        for cp in copies:
            cp.start()
        for cp in copies:
            cp.wait()

    return pl.pallas_call(
        body, name=name, out_shape=jax.ShapeDtypeStruct((N_CHIPS,) + x.shape[1:], x.dtype),
        in_specs=[any_spec], out_specs=any_spec,
        scratch_shapes=[pltpu.SemaphoreType.DMA((N_CHIPS,)), pltpu.SemaphoreType.DMA((N_CHIPS,))],
    )(x)


def _pair_sum(x, got, name):
    n, R, W = got.shape
    tr = _row_tile(R, W, 5 << 20)
    core = lax.axis_index("c").astype(jnp.int32).reshape(1)

    def body(core_ref, a_ref, b_ref, o_ref):
        o_ref[...] = (a_ref[...].astype(F32) + b_ref[...].astype(F32)).astype(BF16)

    spec = pl.BlockSpec((1, tr, W), lambda i, j, c: (i, j, 0))
    grid_spec = pltpu.PrefetchScalarGridSpec(
        num_scalar_prefetch=1, grid=(n, R // tr),
        in_specs=[pl.BlockSpec((1, tr, W), lambda i, j, c: (2 * i + c[0], j, 0)), spec], out_specs=spec)
    return pl.pallas_call(
        body, name=name, grid_spec=grid_spec, out_shape=jax.ShapeDtypeStruct((n, R, W), BF16),
        compiler_params=_cparams(("arbitrary", "arbitrary")),
    )(core, x, got)


def _all_gather(xs, name):
    n = len(xs)
    any_spec = pl.BlockSpec(memory_space=pl.ANY)

    def body(*refs):
        x_refs, out_refs = refs[:n], refs[n:2 * n]
        send_sems, recv_sems, local_sems = refs[2 * n:]
        x, y, c = lax.axis_index("x"), lax.axis_index("y"), lax.axis_index("c")
        me, sibling = (x, y, c), (x, y, 1 - c)
        chips = [(1 - x, y), (x, 1 - y), (1 - x, 1 - y)]

        def copy(a, k, block, to, src=None):
            dst = out_refs[a].at[_dev_index(block)]
            return pltpu.make_async_remote_copy(
                src_ref=dst if src is None else src, dst_ref=dst,
                send_sem=send_sems.at[a, k], recv_sem=recv_sems.at[a, k],
                device_id=to, device_id_type=pl.DeviceIdType.MESH)

        mine = [pltpu.make_async_copy(x_refs[a], out_refs[a].at[_dev_index(me)], local_sems.at[a]) for a in range(n)]
        for cp in mine:
            cp.start()
        first = []
        for a in range(n):
            first.append(copy(a, 0, me, sibling, src=x_refs[a]))
            first += [copy(a, 1 + j, me, (*chip, c), src=x_refs[a]) for j, chip in enumerate(chips)]
        for cp in first:
            cp.start()
        passed = []
        for j, chip in enumerate(chips):
            for a in range(n):
                copy(a, 1 + j, (*chip, c), me).wait_recv()
                fwd = copy(a, 4 + j, (*chip, c), sibling)
                fwd.start()
                passed.append(fwd)
        for a in range(n):
            copy(a, 0, sibling, me).wait_recv()
            for j, chip in enumerate(chips):
                copy(a, 4 + j, (*chip, 1 - c), me).wait_recv()
        for cp in first + passed:
            cp.wait_send()
        for cp in mine:
            cp.wait()

    return pl.pallas_call(
        body, name=name,
        out_shape=[jax.ShapeDtypeStruct((N_DEV,) + v.shape, v.dtype) for v in xs],
        in_specs=[any_spec] * n, out_specs=[any_spec] * n,
        scratch_shapes=[pltpu.SemaphoreType.DMA((n, 7)), pltpu.SemaphoreType.DMA((n, 7)),
                        pltpu.SemaphoreType.DMA((n,))],
    )(*xs)


def _sum_blocks(x, name):
    _, R, W = x.shape

    def body(x_ref, o_ref):
        acc = x_ref[0].astype(F32)
        for d in range(1, N_DEV):
            acc = acc + x_ref[d].astype(F32)
        o_ref[...] = acc

    return pl.pallas_call(body, name=name, out_shape=jax.ShapeDtypeStruct((R, W), F32),
                          compiler_params=pltpu.CompilerParams(vmem_limit_bytes=VMEM_LIMIT))(x)


def _adam_update(w, g, m, v):
    mn = ADAM_B1 * m + (1.0 - ADAM_B1) * g
    vn = ADAM_B2 * v + (1.0 - ADAM_B2) * (g * g)
    m_hat = mn / (1.0 - ADAM_B1 ** ADAM_STEP)
    v_hat = vn / (1.0 - ADAM_B2 ** ADAM_STEP)
    return -ADAM_LR * (m_hat / (jnp.sqrt(v_hat) + ADAM_EPS) + ADAM_WD * w), mn, vn


def _row_tile(R, W, budget):
    padded = -(-W // LANES) * LANES * 4
    if R * padded <= budget:
        return R
    return _pick(R, [t for t in (2048, 1024, 512, 256, 128, 64, 32, 16, 8) if t * padded <= budget])


def _adamw(w, g, m, v, name):
    R, W = w.shape
    tr = _row_tile(R, W, 1 << 20)

    def body(w_ref, g_ref, m_ref, v_ref, d_ref, mo_ref, vo_ref):
        d_ref[...], mo_ref[...], vo_ref[...] = _adam_update(w_ref[...], g_ref[...], m_ref[...], v_ref[...])

    spec = pl.BlockSpec((tr, W), lambda i: (i, 0))
    shp = jax.ShapeDtypeStruct((R, W), F32)
    return pl.pallas_call(
        body, name=name, grid=(R // tr,), in_specs=[spec] * 4, out_specs=[spec] * 3, out_shape=[shp] * 3,
        compiler_params=_cparams(("arbitrary",)),
    )(w, g, m, v)


def _adamw_group(ws, gs, ms, vs, name):
    n = len(ws)

    def body(*refs):
        ins, outs = refs[:4 * n], refs[4 * n:]
        for i in range(n):
            w_ref, g_ref, m_ref, v_ref = ins[i], ins[n + i], ins[2 * n + i], ins[3 * n + i]
            outs[i][...], outs[n + i][...], outs[2 * n + i][...] = _adam_update(
                w_ref[...], g_ref[...], m_ref[...], v_ref[...])

    shapes = [jax.ShapeDtypeStruct(w.shape, F32) for w in ws]
    outs = pl.pallas_call(body, name=name, out_shape=shapes * 3)(*ws, *gs, *ms, *vs)
    return outs[:n], outs[n:2 * n], outs[2 * n:]


def _adamw_reduce(w, land, m, v, name):
    R, W = w.shape
    n = land.shape[0]
    tr = _row_tile(R, W, 1 << 20)

    def body(w_ref, l_ref, m_ref, v_ref, g_ref, d_ref, mo_ref, vo_ref):
        g = l_ref[0].astype(F32)
        for d in range(1, n):
            g = g + l_ref[d].astype(F32)
        g_ref[...] = g
        d_ref[...], mo_ref[...], vo_ref[...] = _adam_update(w_ref[...], g, m_ref[...], v_ref[...])

    spec = pl.BlockSpec((tr, W), lambda i: (i, 0))
    lspec = pl.BlockSpec((n, tr, W), lambda i: (0, i, 0))
    shp = jax.ShapeDtypeStruct((R, W), F32)
    return pl.pallas_call(
        body, name=name, grid=(R // tr,), in_specs=[spec, lspec, spec, spec], out_specs=[spec] * 4,
        out_shape=[shp] * 4, compiler_params=_cparams(("arbitrary",)),
    )(w, land, m, v)


def _gelu(v):
    c = math.sqrt(2.0 / math.pi)
    return 0.5 * v * (1.0 + jnp.tanh(c * (v + 0.044715 * v * v * v)))


def _gelu_grad(v):
    c = math.sqrt(2.0 / math.pi)
    t = jnp.tanh(c * (v + 0.044715 * v * v * v))
    return 0.5 * (1.0 + t) + 0.5 * v * (1.0 - t * t) * c * (1.0 + 3.0 * 0.044715 * v * v)


def kernel(x, p, norm_mix, w_in, q_norm, k_norm, ssm_a_re, ssm_a_im, ssm_log_dt, ssm_b_re, ssm_b_im, ssm_c_re, ssm_c_im, ssm_d, w_glu, b_glu, w_out, norm_ple, w_ple_gate, w_ple_proj, norm_final, loss_target, m_norm_mix, m_w_in, m_q_norm, m_k_norm, m_ssm_a_re, m_ssm_a_im, m_ssm_log_dt, m_ssm_b_re, m_ssm_b_im, m_ssm_c_re, m_ssm_c_im, m_ssm_d, m_w_glu, m_b_glu, m_w_out, m_norm_ple, m_w_ple_gate, m_w_ple_proj, m_norm_final, v_norm_mix, v_w_in, v_q_norm, v_k_norm, v_ssm_a_re, v_ssm_a_im, v_ssm_log_dt, v_ssm_b_re, v_ssm_b_im, v_ssm_c_re, v_ssm_c_im, v_ssm_d, v_w_glu, v_b_glu, v_w_out, v_norm_ple, v_w_ple_gate, v_w_ple_proj, v_norm_final):
    L, D = x.shape[1], x.shape[2]
    D_SSM = ssm_d.shape[1]
    G = D_SSM // SSM_H
    n_slab = D_SSM // SLAB
    n_in = w_in.shape[2]
    D_IN = n_in * N_DEV
    n_pp = w_ple_proj.shape[2]
    n_glu = w_glu.shape[2]
    xs = x[0]
    ps = p[0, 0]
    tgt = loss_target[0]

    (win_t3,) = _all_gather([w_in[0].T.astype(BF16)], "gather_w_in")
    win_t = win_t3.reshape(D_IN, D)
    later_weights = _Carry("gather", [w_glu[0].astype(BF16), w_out[0].astype(BF16), w_ple_gate[0].astype(BF16),
                                      w_ple_proj[0].astype(BF16)])

    ssm_prm = (ssm_a_re[0], ssm_a_im[0], ssm_log_dt[0].reshape(2, G, 1),
               ssm_b_re[0].transpose(0, 1, 3, 2), ssm_b_im[0].transpose(0, 1, 3, 2), ssm_c_re[0], ssm_c_im[0])

    cos, sin = _rope_tables(L)
    hn = _norm_in(xs, norm_mix, "norm_mix")
    ZT = 512
    zp_tile = lambda j: jnp.where(j < 2, j, jnp.where(j < D_IN // ZT - 1, j + 1, 2))
    z = _mm(hn, win_t, "nt", "in_proj", out_dtype=BF16, n_tiles=(ZT, zp_tile))
    qr, kr, vb, kt = _qkv_prep(z, cos, sin, q_norm, k_norm)
    o, lse, (wglu3, wout3, wpg3, wpp3) = _attn_fwd(qr, kr, vb, later_weights)
    wout = wout3.reshape(-1, D)
    wpg = wpg3.reshape(-1, D)
    u_off = 2 * D_ATTN
    u_perm = _seg_perm(z[:, u_off:u_off + D_SSM])
    ys_perm, ssm_ent = _ssm_fwd(u_perm, ssm_prm, ssm_d)
    ys = _seg_unperm(ys_perm)

    tm = _pick(L, (256,))
    wglu = wglu3.transpose(1, 0, 2).reshape(D_SSM, 2 * D_SSM)

    def mix_post(prod, gy_tile, o_ref, ga_ref, gs_ref, b_ref):
        glu_b = (prod + b_ref[...]).astype(BF16)
        gluv = glu_b.astype(F32)
        ga, gs = _f32(ga_ref), _f32(gs_ref)
        y_attn = _f32(o_ref) * ga * _sigmoid(ga)
        y_ssm = gluv[:, :D_SSM] * _sigmoid(gluv[:, D_SSM:]) * gs * _sigmoid(gs)
        return gy_tile, glu_b, jnp.concatenate([y_attn, y_ssm], axis=-1)

    gy, glu, cat = _mm_rows(ys, wglu, "nn", "mix", mix_post, [o, (z, D_ATTN, 1), (z, D_SSM, 3)], [b_glu],
                            [(D_SSM, BF16), (2 * D_SSM, BF16), (D_ATTN + D_SSM, BF16)],
                            pre=lambda y: _gelu(y).astype(BF16), tm=512)

    def out_post(prod, x_ref, g_ref):
        h1v = prod + x_ref[...]
        return h1v, h1v * _rms(h1v) * g_ref[...]

    h1, n2 = _mm_rows(cat, wout, "nn", "out_proj", out_post, [xs], [norm_ple], [(D, F32), (D, BF16)], tm=512)
    pb = ps.astype(BF16)
    pp = _mm(pb, wpp3, "nn", "ple_proj", out_dtype=BF16, b_blk=True)

    nf = norm_final.reshape(1, D)

    def tail_post(gp, h1_ref, pp_ref, t_ref, g_ref):
        gate = _sigmoid(gp)
        ppv = _f32(pp_ref)
        h2 = h1_ref[...] + gate * ppv
        r = _rms(h2)
        hh = h2 * r
        err = hh * g_ref[...] - t_ref[...]
        loss_part = jnp.broadcast_to(0.5 * jnp.sum(jnp.mean(err * err, axis=-1, keepdims=True)), (1, LANES))
        dy = err * (1.0 / D)
        dh2 = _rms_bwd(dy, hh, r, g_ref[...])
        return dh2, dh2 * gate, dh2 * ppv * gate * (1.0 - gate), loss_part, _colsum(dy * hh)

    dh2, dpp, dsg, loss_acc, d_nf = _mm_rows(n2, wpg, "nn", "tail", tail_post, [h1, pp, tgt], [nf],
                                             [(D, F32), (D, BF16), (D, BF16)], [(1, LANES), (1, D)], vmem=58 << 20)

    g_wpp3 = _mm(pb, dpp, "tn", "d_ple_proj", out_dtype=BF16, out_blk=n_pp)
    g_wpg = _mm(n2, dsg, "tn", "d_ple_gate", out_dtype=BF16)
    def ple_bwd_post(dn, h1_ref, dh2_ref, g_ref):
        h1v = h1_ref[...]
        r = _rms(h1v)
        hh = h1v * r
        dh1 = dh2_ref[...] + _rms_bwd(dn, hh, r, g_ref[...])
        return dh1, dh1, _colsum(dn * hh)

    dh1, dh1b, d_nple = _mm_rows(dsg, wpg, "nt", "ple_bwd", ple_bwd_post, [h1, dh2], [norm_ple],
                                 [(D, F32), (D, BF16)], [(1, D)])

    g_wout = _mm(cat, dh1b, "tn", "d_out_proj", out_dtype=BF16)

    def mix_bwd_post(dcat, o_ref, ga_ref, glu_ref, gs_ref):
        dca, dcs = dcat[:, :D_ATTN], dcat[:, D_ATTN:]
        ga, gs = _f32(ga_ref), _f32(gs_ref)
        gla, glb = glu_ref[:, :D_SSM].astype(F32), glu_ref[:, D_SSM:].astype(F32)
        sa, ss, sb = _sigmoid(ga), _sigmoid(gs), _sigmoid(glb)
        do = dca * ga * sa
        dga = dca * _f32(o_ref) * sa * (1.0 + ga * (1.0 - sa))
        dgs = dcs * gla * sb * ss * (1.0 + gs * (1.0 - ss))
        dy2 = dcs * gs * ss
        da, db = dy2 * sb, dy2 * gla * sb * (1.0 - sb)
        return (do, dga, dgs, jnp.concatenate([da, db], axis=-1),
                jnp.concatenate([_colsum(da), _colsum(db)], axis=-1))

    do, dga, dgs, dglu, g_bglu = _mm_rows(
        dh1b, wout, "nt", "mix_bwd", mix_bwd_post, [o, (z, D_ATTN, 1), glu, (z, D_SSM, 3)], [],
        [(D_ATTN, BF16), (D_ATTN, BF16), (D_SSM, BF16), (2 * D_SSM, BF16)], [(1, 2 * D_SSM)])

    g_wglu3 = _mm(gy, dglu, "tn", "d_glu_proj", out_dtype=BF16, out_blk=n_glu)
    dys = _mm(dglu, wglu3, "nt", "d_ssm_out", b_blk=True,
              post=(lambda out, y: out * _gelu_grad(y), ys))
    du_perm, pg = _ssm_bwd(u_perm, _seg_perm(dys), ssm_ent, ssm_prm, ssm_d)
    du = _seg_unperm(du_perm)

    pg_send = pg.reshape(N_DEV, (n_slab // N_DEV) * 2 * PG_ROWS, SLAB_S)
    dqs, dkr, dvv, (l_wglu, l_wout, l_wpg, l_wpp, l_pg) = _attn_bwd(
        qr, kr, vb, kt, do, o, lse,
        _Carry("a2a", [g_wglu3, g_wout.reshape(N_DEV, -1, D), g_wpg.reshape(N_DEV, -1, D), g_wpp3, pg_send]))

    scale = HEAD_DIM ** -0.5
    kblk = 4 * D_ATTN // D_KV

    a0, k0, v0, u0, s0 = D_ATTN + 2 * D_KV, D_ATTN, D_ATTN + D_KV, 2 * D_ATTN + 2 * D_KV, 2 * D_ATTN + 2 * D_KV + D_SSM

    def qkv_bwd_body(dq_ref, dk_ref, dv_ref, q_ref, k_ref, cos_ref, sin_ref, qn_ref, kn_ref, dga_ref, du_ref, dgs_ref,
                     dz_ref, dqn_ref, dkn_ref):
        c, s = cos_ref[...], sin_ref[...]
        dz_ref[:, a0:a0 + D_ATTN] = dga_ref[...]
        dz_ref[:, u0:u0 + D_SSM] = du_ref[...].astype(BF16)
        dz_ref[:, s0:s0 + D_SSM] = dgs_ref[...]

        def head(g, xh, w):
            dn = g * c + _partner(g * s)
            r = _rms(xh)
            xhat = xh * r
            return _rms_bwd(dn, xhat, r, w), _colsum(dn * xhat)

        dqn = jnp.zeros((1, HEAD_DIM), F32)
        for h in range(N_HEADS):
            sl = slice(h * HEAD_DIM, (h + 1) * HEAD_DIM)
            dx, dw = head(dq_ref[:, sl] * scale, q_ref[:, sl].astype(F32), qn_ref[...])
            dz_ref[:, sl] = dx.astype(BF16)
            dqn = dqn + dw
        dkn = jnp.zeros((1, HEAD_DIM), F32)
        for h in range(N_KV):
            sl = slice(h * HEAD_DIM, (h + 1) * HEAD_DIM)
            dx, dw = head(dk_ref[:, sl], k_ref[:, sl].astype(F32), kn_ref[...])
            dz_ref[:, k0 + h * HEAD_DIM:k0 + (h + 1) * HEAD_DIM] = dx.astype(BF16)
            dkn = dkn + dw
        dz_ref[:, v0:v0 + D_KV] = dv_ref[...].astype(BF16)
        _acc(dqn_ref, dqn)
        _acc(dkn_ref, dkn)

    dz, g_qn, g_kn = _rowcall(
        qkv_bwd_body, "qkv_bwd", L, tm,
        [(dqs, _rspec(tm, D_ATTN)), (dkr, _rspec(tm, D_KV)), (dvv, _rspec(tm, D_KV)),
         (z, _rspec(tm, D_ATTN, 0)), (z, _rspec(tm, D_KV, kblk)), (cos, _rspec(tm, HEAD_DIM)),
         (sin, _rspec(tm, HEAD_DIM)), (q_norm, _fspec(q_norm.shape)), (k_norm, _fspec(k_norm.shape)),
         (dga, _rspec(tm, D_ATTN)), (du, _rspec(tm, D_SSM)), (dgs, _rspec(tm, D_SSM))],
        [(D_IN, BF16)], [(1, HEAD_DIM), (1, HEAD_DIM)])

    g_win_t = _mm(dz, hn, "tn", "d_in_proj", out_dtype=BF16)
    g_win8 = g_win_t.reshape(N_DEV, n_in, D)
    from_sibling = _sibling_swap(g_win8, "swap_d_w_in")
    pair = _pair_sum(g_win8, from_sibling, "pair_sum_d_w_in")
    dhn, (l_win_t,) = _mm(dz, win_t, "nn", "d_norm_mix_in", out_dtype=BF16,
                          carry=_Carry("a2a_chips", [pair]))

    def in_bwd_body(x_ref, dn_ref, dh1_ref, g_ref, dx_ref, dg_ref):
        xv = x_ref[...]
        r = _rms(xv)
        hh = xv * r
        dn = _f32(dn_ref)
        _acc(dg_ref, _colsum(dn * hh))
        dx_ref[...] = dh1_ref[...] + _rms_bwd(dn, hh, r, g_ref[...])

    grad_x, g_nmix = _rowcall(
        in_bwd_body, "in_bwd", L, tm,
        [(xs, _rspec(tm, D)), (dhn, _rspec(tm, D)), (dh1, _rspec(tm, D)), (norm_mix, _fspec(norm_mix.shape))],
        [(D, F32)], [(1, D)])

    tiny_parts = [g_nmix, g_bglu, d_nple, d_nf, g_qn, g_kn, loss_acc[:, :1]]
    tiny_flat = jnp.concatenate([t.reshape(-1) for t in tiny_parts])
    tiny_rows = -(-tiny_flat.shape[0] // (8 * LANES)) * 8
    tiny = jnp.pad(tiny_flat, (0, tiny_rows * LANES - tiny_flat.shape[0])).reshape(tiny_rows, LANES)
    pg_sum = _sum_blocks(l_pg, "sum_ssm_grads")
    pg_all, tiny_all = _all_gather([pg_sum, tiny], "gather_small_grads")
    (g_bt_re, g_bt_im, g_c_re, g_c_im, g_a_re, g_a_im, g_ldt, g_skip) = _ssm_param_grads(
        pg_all.reshape(n_slab, 2, PG_ROWS, SLAB_S), ssm_prm)
    tiny_sum = _sum_blocks(tiny_all, "sum_tiny_grads").reshape(-1)
    tiny_grads, off = [], 0
    for t in tiny_parts:
        tiny_grads.append(tiny_sum[off:off + t.size].reshape(t.shape))
        off += t.size
    r_nmix, r_bglu, r_nple, r_nf, r_qn, r_kn, loss = tiny_grads
    loss = loss.reshape(())

    grads, deltas, new_ms, new_vs = {}, {}, {}, {}
    outs = _adamw_reduce(w_in[0].T, l_win_t, m_w_in[0].T, v_w_in[0].T, "adamw_w_in")
    grads["w_in"], deltas["w_in"], new_ms["w_in"], new_vs["w_in"] = [t.T[None] for t in outs]
    big = [("w_glu", w_glu, l_wglu, m_w_glu, v_w_glu),
           ("w_out", w_out, l_wout, m_w_out, v_w_out), ("w_ple_gate", w_ple_gate, l_wpg, m_w_ple_gate, v_w_ple_gate),
           ("w_ple_proj", w_ple_proj, l_wpp, m_w_ple_proj, v_w_ple_proj)]
    for name, w, ld, m, v in big:
        shp = w.shape
        outs = _adamw_reduce(w[0], ld, m[0], v[0], "adamw_" + name)
        grads[name], deltas[name], new_ms[name], new_vs[name] = [t.reshape(shp) for t in outs]
    bt2 = (2 * G * SSM_H, SSM_P)
    for name, w, g, m, v in (("ssm_b_re", ssm_b_re, g_bt_re, m_ssm_b_re, v_ssm_b_re),
                             ("ssm_b_im", ssm_b_im, g_bt_im, m_ssm_b_im, v_ssm_b_im)):
        to2 = lambda t: t[0].transpose(0, 1, 3, 2).reshape(bt2)
        back = lambda t: t.reshape(2, G, SSM_H, SSM_P).transpose(0, 1, 3, 2)[None]
        outs = _adamw(to2(w), g.reshape(bt2), to2(m), to2(v), "adamw_" + name)
        grads[name] = back(g)
        deltas[name], new_ms[name], new_vs[name] = [back(t) for t in outs]
    small = [("norm_mix", norm_mix, r_nmix, m_norm_mix, v_norm_mix, (1, D)),
             ("q_norm", q_norm, r_qn, m_q_norm, v_q_norm, (1, HEAD_DIM)),
             ("k_norm", k_norm, r_kn, m_k_norm, v_k_norm, (1, HEAD_DIM)),
             ("ssm_a_re", ssm_a_re, g_a_re, m_ssm_a_re, v_ssm_a_re, (2 * G, SSM_P)),
             ("ssm_a_im", ssm_a_im, g_a_im, m_ssm_a_im, v_ssm_a_im, (2 * G, SSM_P)),
             ("ssm_log_dt", ssm_log_dt, g_ldt, m_ssm_log_dt, v_ssm_log_dt, (2, G)),
             ("ssm_c_re", ssm_c_re, g_c_re, m_ssm_c_re, v_ssm_c_re, (2 * G * SSM_H, SSM_P)),
             ("ssm_c_im", ssm_c_im, g_c_im, m_ssm_c_im, v_ssm_c_im, (2 * G * SSM_H, SSM_P)),
             ("ssm_d", ssm_d, g_skip, m_ssm_d, v_ssm_d, (1, D_SSM)),
             ("b_glu", b_glu, r_bglu, m_b_glu, v_b_glu, (1, 2 * D_SSM)),
             ("norm_ple", norm_ple, r_nple, m_norm_ple, v_norm_ple, (1, D)),
             ("norm_final", norm_final, r_nf, m_norm_final, v_norm_final, (1, D))]
    group = [it for it in small if it[5][0] * it[5][1] <= (1 << 14)]
    grouped = {it[0] for it in group}
    for name, w, g, m, v, s2 in small:
        if name in grouped:
            continue
        shp = w.shape
        outs = _adamw(w.reshape(s2), g.reshape(s2), m.reshape(s2), v.reshape(s2), "adamw_" + name)
        grads[name] = g.reshape(shp)
        deltas[name], new_ms[name], new_vs[name] = [t.reshape(shp) for t in outs]
    ds, mns, vns = _adamw_group(*[[it[i].reshape(it[5]) for it in group] for i in (1, 2, 3, 4)], "adamw_tiny")
    for (name, w, g, _, _, _), d_, m_, v_ in zip(group, ds, mns, vns):
        shp = w.shape
        grads[name] = g.reshape(shp)
        deltas[name], new_ms[name], new_vs[name] = d_.reshape(shp), m_.reshape(shp), v_.reshape(shp)

    order = ["norm_mix", "w_in", "q_norm", "k_norm", "ssm_a_re", "ssm_a_im", "ssm_log_dt", "ssm_b_re", "ssm_b_im",
             "ssm_c_re", "ssm_c_im", "ssm_d", "w_glu", "b_glu", "w_out", "norm_ple", "w_ple_gate", "w_ple_proj",
             "norm_final"]
    return (loss, grad_x[None], *[grads[k] for k in order], *[deltas[k] for k in order],
            *[new_ms[k] for k in order], *[new_vs[k] for k in order])
```

```python
import functools
import math

import numpy as np
import jax
import jax.numpy as jnp
from jax import lax
from jax.experimental import pallas as pl
from jax.experimental.pallas import tpu as pltpu

F32 = jnp.float32
BF16 = jnp.bfloat16

N_DEV = 8
N_CHIPS = 4
EPS = 1e-6
GRID_W = 64
ROPE_THETA = 10000.0
HEAD_DIM = 128
N_HEADS = 8
N_KV = 2
REP = N_HEADS // N_KV
D_ATTN = N_HEADS * HEAD_DIM
D_KV = N_KV * HEAD_DIM
SSM_H = 16
SSM_P = 64
SLAB = 128
SLAB_G = SLAB // SSM_H
SLAB_S = SLAB_G * SSM_P
SEG = 8
CHAINS = 2
SCAN_UNROLL = 8
LANES = 128
PG_ROWS = 72
VMEM_LIMIT = 48 << 20

ADAM_LR = 0.001
ADAM_B1 = 0.9
ADAM_B2 = 0.999
ADAM_EPS = 1e-08
ADAM_WD = 0.01
ADAM_STEP = 10


def _pick(n, cands):
    for c in cands:
        if n % c == 0:
            return c
    return n


def _cparams(sem, vmem=VMEM_LIMIT):
    return pltpu.CompilerParams(dimension_semantics=sem, vmem_limit_bytes=vmem)


class _Carry:
    def __init__(self, kind, xs):
        self.kind, self.xs, self.n = kind, list(xs), len(xs)
        self.ks = (2, 4, 6) if kind == "a2a_chips" else tuple(range(1, N_DEV))
        self.index = _chip_index if kind == "a2a_chips" else _dev_index
        lead = (N_DEV,) if kind == "gather" else ()
        self.out_shape = [jax.ShapeDtypeStruct(lead + v.shape, v.dtype) for v in xs]
        self.specs = [pl.BlockSpec(memory_space=pl.ANY)] * self.n
        self.scratch = [pltpu.SemaphoreType.DMA((self.n, len(self.ks))), pltpu.SemaphoreType.DMA((self.n, len(self.ks))),
                        pltpu.SemaphoreType.DMA((self.n,))]

    def _copies(self, x_refs, out_refs, sems):
        send_sems, recv_sems, local_sems = sems
        x, y, c = lax.axis_index("x"), lax.axis_index("y"), lax.axis_index("c")
        me = self.index((x, y, c))
        mine, sends, arrivals = [], [], []
        for a in range(self.n):
            src_mine = x_refs[a] if self.kind == "gather" else x_refs[a].at[me]
            mine.append(pltpu.make_async_copy(src_mine, out_refs[a].at[me], local_sems.at[a]))
            for s, k in enumerate(self.ks):
                peer = _peer(k, x, y, c)
                src = x_refs[a] if self.kind == "gather" else x_refs[a].at[self.index(peer)]
                sends.append(pltpu.make_async_remote_copy(
                    src_ref=src, dst_ref=out_refs[a].at[me], send_sem=send_sems.at[a, s],
                    recv_sem=recv_sems.at[a, s], device_id=peer, device_id_type=pl.DeviceIdType.MESH))
                land = out_refs[a].at[self.index(peer)]
                arrivals.append(pltpu.make_async_remote_copy(
                    src_ref=land, dst_ref=land, send_sem=send_sems.at[a, s],
                    recv_sem=recv_sems.at[a, s], device_id=peer, device_id_type=pl.DeviceIdType.MESH))
        return mine, sends, arrivals

    def start(self, x_refs, out_refs, sems):
        mine, sends, _ = self._copies(x_refs, out_refs, sems)
        for cp in mine + sends:
            cp.start()

    def wait(self, x_refs, out_refs, sems):
        mine, sends, arrivals = self._copies(x_refs, out_refs, sems)
        for cp in arrivals:
            cp.wait_recv()
        for cp in sends:
            cp.wait_send()
        for cp in mine:
            cp.wait()


def _grid_edges(grid):
    first = functools.reduce(lambda p, q: p & q, [pl.program_id(d) == 0 for d in range(len(grid))])
    last = functools.reduce(lambda p, q: p & q, [pl.program_id(d) == g - 1 for d, g in enumerate(grid)])
    return first, last


def _mm(a, b, mode, name, out_dtype=F32, add=None, bias=None, a_blk=False, b_blk=False, out_blk=0, carry=None,
        n_tiles=None, post=None):
    w = b.shape[2] if b_blk else out_blk
    if mode == "nn":
        M, K = a.shape
        N = b.shape[0] * w if b_blk else b.shape[1]
    elif mode == "nt":
        M = a.shape[1] if a_blk else a.shape[0]
        N = b.shape[1] if b_blk else b.shape[0]
        K = b.shape[0] * w if b_blk else b.shape[1]
    else:
        K, M = a.shape
        N = b.shape[0] * w if b_blk else b.shape[1]
    tm = _pick(M, (1024, 768, 512, 256))
    tn = _pick(N, (1024, 768, 512, 256))
    if mode == "tn" and N <= 2048:
        tn = N
    tk = K if (mode != "tn" and K <= 2048) else _pick(K, (2048, 1024, 768, 512, 256) if mode == "tn"
                                                       else (1024, 768, 512, 256))
    if mode == "nn" and K > 2048 and N <= 2048:
        tn, tk = N, _pick(K, (1536, 1024, 768, 512, 256))
    perm = lambda j: j
    if n_tiles:
        tn, perm = n_tiles
        tm = _pick(M, (2048, 1024, 512, 256))
    if mode == "nt" and b_blk:
        tk = w
    elif b_blk or out_blk:
        tn = w
    if mode == "tn" and 4 * K * (tm + tn) <= (24 << 20):
        tk = K
    nk = K // tk
    grid = (M // tm, N // tn, nk)
    if mode == "nn":
        a_spec = pl.BlockSpec((tm, tk), lambda i, j, k: (i, k))
        b_spec = (pl.BlockSpec((1, tk, tn), lambda i, j, k: (j, k, 0)) if b_blk
                  else pl.BlockSpec((tk, tn), lambda i, j, k: (k, j)))
        dims = (((1,), (0,)), ((), ()))
    elif mode == "nt":
        a_spec = (pl.BlockSpec((1, tm, tk), lambda i, j, k: (k, i, 0)) if a_blk
                  else pl.BlockSpec((tm, tk), lambda i, j, k: (i, k)))
        b_spec = (pl.BlockSpec((1, tn, tk), lambda i, j, k: (k, j, 0)) if b_blk
                  else pl.BlockSpec((tn, tk), lambda i, j, k: (perm(j), k)))
        dims = (((1,), (1,)), ((), ()))
    else:
        a_spec = pl.BlockSpec((tk, tm), lambda i, j, k: (k, i))
        b_spec = (pl.BlockSpec((1, tk, tn), lambda i, j, k: (j, k, 0)) if b_blk
                  else pl.BlockSpec((tk, tn), lambda i, j, k: (k, j)))
        dims = (((0,), (0,)), ((), ()))
    if out_blk:
        out_spec = pl.BlockSpec((1, tm, tn), lambda i, j, k: (j, i, 0))
        out_shape = jax.ShapeDtypeStruct((N // tn, M, tn), out_dtype)
    else:
        out_spec = pl.BlockSpec((tm, tn), lambda i, j, k: (i, j))
        out_shape = jax.ShapeDtypeStruct((M, N), out_dtype)
    extras, extra_specs, combine = [], [], []
    if add is not None:
        extras.append(add)
        extra_specs.append(pl.BlockSpec((tm, tn), lambda i, j, k: (i, j)))
        combine.append(lambda out, t: out + t)
    if bias is not None:
        extras.append(bias)
        extra_specs.append(pl.BlockSpec((1, tn), lambda i, j, k: (0, j)))
        combine.append(lambda out, t: out + t)
    if post is not None:
        extras.append(post[1])
        extra_specs.append(pl.BlockSpec((tm, tn), lambda i, j, k: (i, j)))
        combine.append(post[0])

    n_ex = len(extras)
    nc = carry.n if carry else 0

    def body(a_ref, b_ref, *rest):
        ex_refs, cx = rest[:n_ex], rest[n_ex:n_ex + nc]
        o_ref, cout = rest[n_ex + nc], rest[n_ex + nc + 1:n_ex + 2 * nc + 1]
        tail = rest[n_ex + 2 * nc + 1:]
        sems = tail[:3] if carry else ()
        first, last = _grid_edges(grid)
        if carry:
            @pl.when(first)
            def _():
                carry.start(cx, cout, sems)

        def product():
            av = a_ref[0] if a_blk else a_ref[...]
            bv = b_ref[0] if b_blk else b_ref[...]
            return lax.dot_general(av, bv, dims, preferred_element_type=F32)

        def finish(out):
            for r, fn in zip(ex_refs, combine):
                out = fn(out, r[...])
            if out_blk:
                o_ref[0] = out.astype(out_dtype)
            else:
                o_ref[...] = out.astype(out_dtype)

        if nk == 1:
            finish(product())
        else:
            acc_ref = tail[-1]
            k = pl.program_id(2)

            @pl.when(k == 0)
            def _():
                acc_ref[...] = jnp.zeros_like(acc_ref)

            acc_ref[...] += product()

            @pl.when(k == nk - 1)
            def _():
                finish(acc_ref[...])

        if carry:
            @pl.when(last)
            def _():
                carry.wait(cx, cout, sems)

    scratch = (carry.scratch if carry else []) + ([pltpu.VMEM((tm, tn), F32)] if nk > 1 else [])
    outs = pl.pallas_call(
        body, name=name, grid=grid,
        in_specs=[a_spec, b_spec] + extra_specs + (carry.specs if carry else []),
        out_specs=[out_spec] + (carry.specs if carry else []),
        out_shape=[out_shape] + (carry.out_shape if carry else []),
        scratch_shapes=scratch,
        compiler_params=_cparams(("arbitrary", "arbitrary", "arbitrary")),
    )(a, b, *extras, *(carry.xs if carry else []))
    return (outs[0], outs[1:]) if carry else outs[0]


def _rspec(tm, w, cb=0):
    return pl.BlockSpec((tm, w), lambda i: (i, cb))


def _fspec(shape):
    nd = len(shape)
    return pl.BlockSpec(shape, lambda i: (0,) * nd)


def _rowcall(body, name, L, tm, ins, row_outs, acc_outs=()):
    out_shape = [jax.ShapeDtypeStruct((L, w), dt) for w, dt in row_outs]
    out_shape += [jax.ShapeDtypeStruct(s, F32) for s in acc_outs]
    out_specs = [_rspec(tm, w) for w, _ in row_outs] + [_fspec(s) for s in acc_outs]
    return pl.pallas_call(
        body, name=name, grid=(L // tm,),
        in_specs=[s for _, s in ins], out_specs=out_specs, out_shape=out_shape,
        compiler_params=_cparams(("arbitrary",)),
    )(*[a for a, _ in ins])


def _mm_rows(a, b, mode, name, post, row_ins, full_ins, row_outs, acc_outs=(), vmem=VMEM_LIMIT, pre=None, tm=256):
    M, K = a.shape
    N = b.shape[1] if mode == "nn" else b.shape[0]
    tm = _pick(M, (tm,))
    dims = (((1,), (0,)), ((), ())) if mode == "nn" else (((1,), (1,)), ((), ()))
    b_spec = pl.BlockSpec(b.shape, lambda i: (0, 0), pipeline_mode=pl.Buffered(1))
    n_in, n_row = len(row_ins) + len(full_ins), len(row_outs)

    def body(a_ref, b_ref, *rest):
        ins, outs = rest[:n_in], rest[n_in:]
        av = pre(a_ref[...]) if pre else a_ref[...]
        prod = lax.dot_general(av, b_ref[...], dims, preferred_element_type=F32)
        res = post(prod, av, *ins) if pre else post(prod, *ins)
        for o_ref, val in zip(outs[:n_row], res[:n_row]):
            o_ref[...] = val.astype(o_ref.dtype)
        for acc_ref, val in zip(outs[n_row:], res[n_row:]):
            _acc(acc_ref, val)

    out_shape = [jax.ShapeDtypeStruct((M, w), dt) for w, dt in row_outs]
    out_shape += [jax.ShapeDtypeStruct(s, F32) for s in acc_outs]
    row_ins = [r if isinstance(r, tuple) else (r, r.shape[1], 0) for r in row_ins]
    return pl.pallas_call(
        body, name=name, grid=(M // tm,),
        in_specs=[_rspec(tm, K), b_spec] + [_rspec(tm, w, cb) for _, w, cb in row_ins]
        + [_fspec(f.shape) for f in full_ins],
        out_specs=[_rspec(tm, w) for w, _ in row_outs] + [_fspec(s) for s in acc_outs],
        out_shape=out_shape, compiler_params=_cparams(("arbitrary",), vmem),
    )(a, b, *[r for r, _, _ in row_ins], *full_ins)


def _acc(ref, val):
    @pl.when(pl.program_id(0) == 0)
    def _():
        ref[...] = jnp.zeros_like(ref)
    ref[...] += val


def _colsum(v):
    return jnp.sum(v, axis=0, keepdims=True)


def _rms(xv):
    return lax.rsqrt(jnp.mean(xv * xv, axis=-1, keepdims=True) + EPS)


def _rms_bwd(dn, xhat, r, g):
    dng = dn * g
    return r * (dng - xhat * jnp.mean(dng * xhat, axis=-1, keepdims=True))


def _sigmoid(v):
    return jax.nn.sigmoid(v)


def _f32(ref):
    return ref[...].astype(F32)


def _partner(v):
    w = v.shape[-1]
    lane = lax.broadcasted_iota(jnp.int32, v.shape, v.ndim - 1)
    first_half = (lane % 64) < 32
    return jnp.where(first_half, pltpu.roll(v, w - 32, axis=v.ndim - 1), pltpu.roll(v, 32, axis=v.ndim - 1))


def _norm_in(x, g, name):
    L, D = x.shape
    tm = _pick(L, (512, 256))

    def body(x_ref, g_ref, o_ref):
        xv = x_ref[...]
        o_ref[...] = (xv * _rms(xv) * g_ref[...]).astype(BF16)

    return _rowcall(body, name, L, tm, [(x, _rspec(tm, D)), (g, _fspec(g.shape))], [(D, BF16)])[0]


def _rope_tables(L):
    t = np.arange(L)
    rows = (t // GRID_W).astype(np.float32)
    cols = (t % GRID_W).astype(np.float32)
    n_freq = HEAD_DIM // 4
    inv_freq = np.float32(ROPE_THETA) ** (-np.arange(n_freq, dtype=np.float32) / np.float32(n_freq))
    ar = (rows[:, None] * inv_freq[None, :]).astype(np.float32).astype(np.float64)
    ac = (cols[:, None] * inv_freq[None, :]).astype(np.float32).astype(np.float64)
    cos = np.concatenate([np.cos(ar), np.cos(ar), np.cos(ac), np.cos(ac)], axis=-1).astype(np.float32)
    sin = np.concatenate([-np.sin(ar), np.sin(ar), -np.sin(ac), np.sin(ac)], axis=-1).astype(np.float32)
    return jnp.asarray(cos), jnp.asarray(sin)


def _qkv_prep(z, cos, sin, qn, kn):
    L = z.shape[0]
    tm = _pick(L, (512, 256))
    scale = HEAD_DIM ** -0.5
    kblk = 4 * D_ATTN // D_KV

    def body(q_ref, k_ref, v_ref, cos_ref, sin_ref, qn_ref, kn_ref, qo_ref, ko_ref, vo_ref, kt_ref):
        c, s = cos_ref[...], sin_ref[...]

        def head(xh, w):
            n = xh * _rms(xh) * w
            return n * c + _partner(n) * s

        for h in range(N_HEADS):
            sl = slice(h * HEAD_DIM, (h + 1) * HEAD_DIM)
            qo_ref[:, sl] = (head(q_ref[:, sl].astype(F32), qn_ref[...]) * scale).astype(BF16)
        for h in range(N_KV):
            sl = slice(h * HEAD_DIM, (h + 1) * HEAD_DIM)
            kr = head(k_ref[:, sl].astype(F32), kn_ref[...])
            ko_ref[:, sl] = kr.astype(BF16)
            kt_ref[sl, :] = kr.T.astype(BF16)
        vo_ref[...] = v_ref[...].astype(BF16)

    return pl.pallas_call(
        body, name="qkv_prep", grid=(L // tm,),
        in_specs=[_rspec(tm, D_ATTN, 0), _rspec(tm, D_KV, kblk), _rspec(tm, D_KV, kblk + 1),
                  _rspec(tm, HEAD_DIM), _rspec(tm, HEAD_DIM), _fspec(qn.shape), _fspec(kn.shape)],
        out_specs=[_rspec(tm, D_ATTN), _rspec(tm, D_KV), _rspec(tm, D_KV),
                   pl.BlockSpec((D_KV, tm), lambda i: (0, i))],
        out_shape=[jax.ShapeDtypeStruct((L, D_ATTN), BF16), jax.ShapeDtypeStruct((L, D_KV), BF16),
                   jax.ShapeDtypeStruct((L, D_KV), BF16), jax.ShapeDtypeStruct((D_KV, L), BF16)],
        compiler_params=_cparams(("arbitrary",)),
    )(z, z, z, cos, sin, qn, kn)


def _col_to_row(col):
    n = col.shape[0]
    eye = lax.broadcasted_iota(jnp.int32, (n, n), 0) == lax.broadcasted_iota(jnp.int32, (n, n), 1)
    return jnp.sum(jnp.where(eye, col, 0.0), axis=0, keepdims=True)


def _attn_fwd(q, k, v, carry=None):
    L = q.shape[0]
    tq = _pick(L, (256, 128))
    grid = (N_HEADS, L // tq)
    nc = carry.n if carry else 0

    def body(q_ref, k_ref, v_ref, *rest):
        cx, (o_ref, lse_ref) = rest[:nc], rest[nc:nc + 2]
        cout, sems = rest[nc + 2:2 * nc + 2], rest[2 * nc + 2:]
        first, last = _grid_edges(grid)
        if carry:
            @pl.when(first)
            def _():
                carry.start(cx, cout, sems)

        s = lax.dot_general(q_ref[...], k_ref[...], (((1,), (1,)), ((), ())), preferred_element_type=F32)
        m = jnp.max(s, axis=-1, keepdims=True)
        e = jnp.exp(s - m)
        l = jnp.sum(e, axis=-1, keepdims=True)
        o_ref[...] = (jnp.dot(e.astype(BF16), v_ref[...], preferred_element_type=F32) / l).astype(BF16)
        lse_ref[0] = _col_to_row(m + jnp.log(l))

        if carry:
            @pl.when(last)
            def _():
                carry.wait(cx, cout, sems)

    outs = pl.pallas_call(
        body, name="attn_fwd", grid=grid,
        in_specs=[pl.BlockSpec((tq, HEAD_DIM), lambda h, i: (i, h)),
                  pl.BlockSpec((L, HEAD_DIM), lambda h, i: (0, h // REP)),
                  pl.BlockSpec((L, HEAD_DIM), lambda h, i: (0, h // REP))] + (carry.specs if carry else []),
        out_specs=[pl.BlockSpec((tq, HEAD_DIM), lambda h, i: (i, h)),
                   pl.BlockSpec((1, 1, tq), lambda h, i: (h, 0, i))] + (carry.specs if carry else []),
        out_shape=[jax.ShapeDtypeStruct((L, D_ATTN), BF16), jax.ShapeDtypeStruct((N_HEADS, 1, L), F32)]
        + (carry.out_shape if carry else []),
        scratch_shapes=carry.scratch if carry else [],
        compiler_params=_cparams(("arbitrary", "arbitrary")),
    )(q, k, v, *(carry.xs if carry else []))
    return outs[0], outs[1], outs[2:]


def _attn_bwd(q, k, v, kt, do, o, lse, carry=None):
    L = q.shape[0]
    tq = _pick(L, (256, 128))
    kc = _pick(L, (512, 256, 128))
    nt = (((1,), (1,)), ((), ()))
    grid = (N_KV, REP, L // tq)
    nc = carry.n if carry else 0

    def body(q_ref, do_ref, o_ref, lse_ref, k_ref, v_ref, kt_ref, *rest):
        cx, (dq_ref, dk_ref, dv_ref) = rest[:nc], rest[nc:nc + 3]
        cout, sems = rest[nc + 3:2 * nc + 3], rest[2 * nc + 3:]
        first, last = _grid_edges(grid)
        if carry:
            @pl.when(first)
            def _():
                carry.start(cx, cout, sems)

        @pl.when((pl.program_id(1) == 0) & (pl.program_id(2) == 0))
        def _():
            dk_ref[...] = jnp.zeros_like(dk_ref)
            dv_ref[...] = jnp.zeros_like(dv_ref)

        qv, dov = q_ref[...], do_ref[...]
        lse_row = lse_ref[0]
        delta = _col_to_row(jnp.sum(dov.astype(F32) * _f32(o_ref), axis=-1, keepdims=True))
        dqt = jnp.zeros((HEAD_DIM, tq), F32)
        for c in range(L // kc):
            sl = slice(c * kc, (c + 1) * kc)
            st = lax.dot_general(k_ref[sl, :], qv, nt, preferred_element_type=F32)
            pt = jnp.exp(st - lse_row)
            dpt = lax.dot_general(v_ref[sl, :], dov, nt, preferred_element_type=F32)
            dst = (pt * (dpt - delta)).astype(BF16)
            dv_ref[sl, :] += jnp.dot(pt.astype(BF16), dov, preferred_element_type=F32)
            dk_ref[sl, :] += jnp.dot(dst, qv, preferred_element_type=F32)
            dqt = dqt + jnp.dot(kt_ref[:, sl], dst, preferred_element_type=F32)
        dq_ref[...] = dqt.T

        if carry:
            @pl.when(last)
            def _():
                carry.wait(cx, cout, sems)

    head = lambda g, r, i: (i, g * REP + r)
    outs = pl.pallas_call(
        body, name="attn_bwd", grid=grid,
        in_specs=[pl.BlockSpec((tq, HEAD_DIM), head), pl.BlockSpec((tq, HEAD_DIM), head),
                  pl.BlockSpec((tq, HEAD_DIM), head),
                  pl.BlockSpec((1, 1, tq), lambda g, r, i: (g * REP + r, 0, i)),
                  pl.BlockSpec((L, HEAD_DIM), lambda g, r, i: (0, g)),
                  pl.BlockSpec((L, HEAD_DIM), lambda g, r, i: (0, g)),
                  pl.BlockSpec((HEAD_DIM, L), lambda g, r, i: (g, 0))] + (carry.specs if carry else []),
        out_specs=[pl.BlockSpec((tq, HEAD_DIM), head),
                   pl.BlockSpec((L, HEAD_DIM), lambda g, r, i: (0, g)),
                   pl.BlockSpec((L, HEAD_DIM), lambda g, r, i: (0, g))] + (carry.specs if carry else []),
        out_shape=[jax.ShapeDtypeStruct((L, D_ATTN), F32), jax.ShapeDtypeStruct((L, D_KV), F32),
                   jax.ShapeDtypeStruct((L, D_KV), F32)] + (carry.out_shape if carry else []),
        scratch_shapes=carry.scratch if carry else [],
        compiler_params=_cparams(("arbitrary", "arbitrary", "arbitrary")),
    )(q, do, o, lse, k, v, kt, *(carry.xs if carry else []))
    return outs[0], outs[1], outs[2], outs[3:]


def _seg_perm(a):
    L, C = a.shape
    return a.reshape(SEG, L // SEG, C).transpose(1, 0, 2).reshape(L, C)


def _seg_unperm(a):
    L, C = a.shape
    return a.reshape(L // SEG, SEG, C).transpose(1, 0, 2).reshape(L, C)


def _cmul(ar, ai, br, bi):
    return ar * br - ai * bi, ar * bi + ai * br


def _rows8(rr):
    if isinstance(rr, int):
        return pl.ds(rr * SEG, SEG)
    return pl.ds(pl.multiple_of(rr * SEG, SEG), SEG)


def _seg_scan(xr_ref, xi_ref, ar, ai, reverse, n_rows, visit=None, visit_init=(), entering=None):
    shape = ar.shape
    zero = jnp.zeros(shape, F32)
    rc = n_rows // CHAINS

    def index(q):
        return (n_rows - 1 - q) if reverse else q

    if entering is None:
        def ends(q, carry):
            out = []
            for j in range(CHAINS):
                sl = _rows8(index(j * rc + q))
                pr, pi = _cmul(ar, ai, carry[2 * j], carry[2 * j + 1])
                out += [pr + xr_ref[sl, :], pi + xi_ref[sl, :]]
            return tuple(out)

        def ends_block(qb, carry):
            for t in range(SCAN_UNROLL):
                carry = ends(qb * SCAN_UNROLL + t, carry)
            return carry

        e = lax.fori_loop(0, rc // SCAN_UNROLL, ends_block, (zero,) * (2 * CHAINS))

        pr, pi = ar, ai
        for _ in range(int(math.log2(rc))):
            pr, pi = _cmul(pr, pi, pr, pi)
        sub = lax.broadcasted_iota(jnp.int32, shape, 0)
        shift = (SEG - 1) if reverse else 1
        edge = (SEG - 1) if reverse else 0
        entering = [(zero, zero)] * CHAINS
        for _ in range(SEG):
            tr, ti = _cmul(pr, pi, *entering[CHAINS - 1])
            cur = (jnp.where(sub == edge, 0.0, pltpu.roll(tr + e[2 * CHAINS - 2], shift, axis=0)),
                   jnp.where(sub == edge, 0.0, pltpu.roll(ti + e[2 * CHAINS - 1], shift, axis=0)))
            entering = [cur]
            for j in range(1, CHAINS):
                tr, ti = _cmul(pr, pi, *cur)
                cur = (tr + e[2 * j - 2], ti + e[2 * j - 1])
                entering.append(cur)

    def step(q, carry, last):
        out, acc = [], carry[2 * CHAINS:]
        for j in range(CHAINS):
            rr = index(j * rc + q)
            sl = _rows8(rr)
            pr, pi = _cmul(ar, ai, carry[2 * j], carry[2 * j + 1])
            nr, ni = pr + xr_ref[sl, :], pi + xi_ref[sl, :]
            xr_ref[sl, :] = nr
            xi_ref[sl, :] = ni
            if visit:
                acc = visit(rr, nr, ni, acc, last and j == CHAINS - 1)
            out += [nr, ni]
        return (*out, *acc)

    def step_block(qb, carry):
        for t in range(SCAN_UNROLL):
            carry = step(qb * SCAN_UNROLL + t, carry, False)
        return carry

    start = tuple(v for pair in entering for v in pair)
    n_blocks = (rc - 1) // SCAN_UNROLL
    carry = lax.fori_loop(0, n_blocks, step_block, (*start, *visit_init))
    for q in range(n_blocks * SCAN_UNROLL, rc - 1):
        carry = step(q, carry, False)
    carry = step(rc - 1, carry, True)
    return entering, carry[2 * CHAINS:]


def _discretise(a_re, a_im, ldt):
    lr = jnp.minimum(a_re, -1e-4)
    li = a_im
    dt = jnp.exp(ldt)
    mag = jnp.exp(lr * dt)
    lbr = mag * jnp.cos(li * dt)
    lbi = mag * jnp.sin(li * dt)
    den = lr * lr + li * li
    nr = lbr - 1.0
    fr = (nr * lr + lbi * li) / den
    fi = (lbi * lr - nr * li) / den
    return lr, li, dt, lbr, lbi, fr, fi


def _lane_row(v):
    return jnp.concatenate([v[g:g + 1, :] for g in range(v.shape[0])], axis=1)


def _ssm_fill_maps(d, prm, tmp_ref, maps):
    a_re_ref, a_im_ref, ldt_ref, bt_re_ref, bt_im_ref, c_re_ref, c_im_ref = prm
    _, _, _, lbr, lbi, fr, fi = _discretise(a_re_ref[d], a_im_ref[d], ldt_ref[d])

    def fill(dst, piece):
        tmp_ref[...] = jnp.zeros_like(tmp_ref)
        for g in range(SLAB_G):
            tmp_ref[g * SSM_H:(g + 1) * SSM_H, g * SSM_P:(g + 1) * SSM_P] = piece(g)
        dst[...] = tmp_ref[...].astype(BF16)

    wbr, wbi, wcr, wci = maps
    fill(wbr, lambda g: fr[g:g + 1] * bt_re_ref[d, g] - fi[g:g + 1] * bt_im_ref[d, g])
    fill(wbi, lambda g: fr[g:g + 1] * bt_im_ref[d, g] + fi[g:g + 1] * bt_re_ref[d, g])
    fill(wcr, lambda g: c_re_ref[d, g])
    fill(wci, lambda g: c_im_ref[d, g])
    return _lane_row(lbr), _lane_row(lbi)


def _ssm_param_specs():
    pole = pl.BlockSpec((2, SLAB_G, SSM_P), lambda j: (0, j, 0))
    step = pl.BlockSpec((2, SLAB_G, 1), lambda j: (0, j, 0))
    mat = pl.BlockSpec((2, SLAB_G, SSM_H, SSM_P), lambda j: (0, j, 0, 0))
    return [pole, pole, step, mat, mat, mat, mat]


_MAP_SCRATCH = [pltpu.VMEM((SLAB, SLAB_S), F32)] + [pltpu.VMEM((SLAB, SLAB_S), BF16)] * 4
_ENT_SPEC = pl.BlockSpec((1, 2, 2 * CHAINS, SEG, SLAB_S), lambda j: (j, 0, 0, 0, 0))
_NT = (((1,), (1,)), ((), ()))


def _ssm_fwd(u, prm, dskip):
    L, C = u.shape
    n_rows = L // SEG
    tc = _pick(L, (2048, 1024, 512, 256))
    u_spec = pl.BlockSpec((L, SLAB), lambda j: (0, j))
    d_spec = pl.BlockSpec((1, SLAB), lambda j: (0, j))

    def body(u_ref, *rest):
        prm_refs, d_ref, y_ref, ent_ref = rest[:7], rest[7], rest[8], rest[9]
        tmp_ref, maps, xr_ref, xi_ref = rest[10], rest[11:15], rest[15], rest[16]
        wbr, wbi, wcr, wci = maps
        y_ref[...] = u_ref[...] * d_ref[...]
        for d in range(2):
            lam_r, lam_i = _ssm_fill_maps(d, prm_refs, tmp_ref, maps)

            def inp(c, _):
                sl = pl.ds(pl.multiple_of(c * tc, tc), tc)
                ub = u_ref[sl, :].astype(BF16)
                xr_ref[sl, :] = jnp.dot(ub, wbr[...], preferred_element_type=F32)
                xi_ref[sl, :] = jnp.dot(ub, wbi[...], preferred_element_type=F32)
                return 0

            lax.fori_loop(0, L // tc, inp, 0)
            ar = jnp.broadcast_to(lam_r, (SEG, SLAB_S))
            ai = jnp.broadcast_to(lam_i, (SEG, SLAB_S))
            entering, _ = _seg_scan(xr_ref, xi_ref, ar, ai, d == 1, n_rows)
            for j, (er, ei) in enumerate(entering):
                ent_ref[0, d, 2 * j] = er
                ent_ref[0, d, 2 * j + 1] = ei

            def outp(c, _):
                sl = pl.ds(pl.multiple_of(c * tc, tc), tc)
                y_ref[sl, :] += (
                    lax.dot_general(xr_ref[sl, :].astype(BF16), wcr[...], _NT, preferred_element_type=F32)
                    - lax.dot_general(xi_ref[sl, :].astype(BF16), wci[...], _NT, preferred_element_type=F32))
                return 0

            lax.fori_loop(0, L // tc, outp, 0)

    return pl.pallas_call(
        body, name="ssm_fwd", grid=(C // SLAB,),
        in_specs=[u_spec] + _ssm_param_specs() + [d_spec],
        out_specs=[u_spec, _ENT_SPEC],
        out_shape=[jax.ShapeDtypeStruct((L, C), F32),
                   jax.ShapeDtypeStruct((C // SLAB, 2, 2 * CHAINS, SEG, SLAB_S), F32)],
        scratch_shapes=_MAP_SCRATCH + [pltpu.VMEM((L, SLAB_S), F32)] * 2,
        compiler_params=_cparams(("arbitrary",)),
    )(u, *prm, dskip)


def _ssm_bwd(u, dy, ent, prm, dskip):
    L, C = u.shape
    n_rows = L // SEG
    n_slab = C // SLAB
    tc = _pick(L, (2048, 1024, 512, 256))
    u_spec = pl.BlockSpec((L, SLAB), lambda j: (0, j))
    d_spec = pl.BlockSpec((1, SLAB), lambda j: (0, j))
    pg_spec = pl.BlockSpec((1, 2, PG_ROWS, SLAB_S), lambda j: (j, 0, 0, 0))

    def body(u_ref, dy_ref, ent_ref, *rest):
        prm_refs, d_ref, du_ref, pg_ref = rest[:7], rest[7], rest[8], rest[9]
        tmp_ref, maps, acc_ref = rest[10], rest[11:15], rest[15]
        xr_ref, xi_ref, gr_ref, gi_ref = rest[16:20]
        wbr, wbi, wcr, wci = maps
        du_ref[...] = dy_ref[...] * d_ref[...]
        pg_ref[...] = jnp.zeros_like(pg_ref)
        pg_ref[0, 0, 66:67, 0:SLAB] = _colsum(dy_ref[...] * u_ref[...])
        for d in range(2):
            lam_r, lam_i = _ssm_fill_maps(d, prm_refs, tmp_ref, maps)

            def inp(c, _):
                sl = pl.ds(pl.multiple_of(c * tc, tc), tc)
                ub = u_ref[sl, :].astype(BF16)
                dyb = dy_ref[sl, :].astype(BF16)
                xr_ref[sl, :] = jnp.dot(ub, wbr[...], preferred_element_type=F32)
                xi_ref[sl, :] = jnp.dot(ub, wbi[...], preferred_element_type=F32)
                gr_ref[sl, :] = jnp.dot(dyb, wcr[...], preferred_element_type=F32)
                gi_ref[sl, :] = -jnp.dot(dyb, wci[...], preferred_element_type=F32)
                return 0

            lax.fori_loop(0, L // tc, inp, 0)
            ar = jnp.broadcast_to(lam_r, (SEG, SLAB_S))
            ai = jnp.broadcast_to(lam_i, (SEG, SLAB_S))
            entering = [(ent_ref[0, d, 2 * j], ent_ref[0, d, 2 * j + 1]) for j in range(CHAINS)]
            _seg_scan(xr_ref, xi_ref, ar, ai, d == 1, n_rows, entering=entering)

            def pole(rr, lr, li, acc, last):
                if last:
                    pr, pi = entering[0]
                else:
                    nb = _rows8(rr + 1 if d == 1 else rr - 1)
                    pr, pi = xr_ref[nb, :], xi_ref[nb, :]
                return acc[0] + lr * pr + li * pi, acc[1] + li * pr - lr * pi

            zero = jnp.zeros((SEG, SLAB_S), F32)
            _, (accr, acci) = _seg_scan(gr_ref, gi_ref, ar, -ai, d == 0, n_rows, pole, (zero, zero))
            pg_ref[0, d, 64:65, :] = _colsum(accr)
            pg_ref[0, d, 65:66, :] = _colsum(acci)

            acc_ref[...] = jnp.zeros_like(acc_ref)

            def outp(c, _):
                sl = pl.ds(pl.multiple_of(c * tc, tc), tc)
                lrb, lib = gr_ref[sl, :].astype(BF16), gi_ref[sl, :].astype(BF16)
                du_ref[sl, :] += (lax.dot_general(lrb, wbr[...], _NT, preferred_element_type=F32)
                                  + lax.dot_general(lib, wbi[...], _NT, preferred_element_type=F32))
                ut = u_ref[sl, :].astype(F32).T.astype(BF16)
                dyt = dy_ref[sl, :].T.astype(BF16)
                acc_ref[0] += jnp.dot(ut, lrb, preferred_element_type=F32)
                acc_ref[1] += jnp.dot(ut, lib, preferred_element_type=F32)
                acc_ref[2] += jnp.dot(dyt, xr_ref[sl, :].astype(BF16), preferred_element_type=F32)
                acc_ref[3] -= jnp.dot(dyt, xi_ref[sl, :].astype(BF16), preferred_element_type=F32)
                return 0

            lax.fori_loop(0, L // tc, outp, 0)
            for m in range(4):
                for g in range(SLAB_G):
                    lanes = slice(g * SSM_P, (g + 1) * SSM_P)
                    pg_ref[0, d, m * SSM_H:(m + 1) * SSM_H, lanes] = acc_ref[m, g * SSM_H:(g + 1) * SSM_H, lanes]

    return pl.pallas_call(
        body, name="ssm_bwd", grid=(n_slab,),
        in_specs=[u_spec, u_spec, _ENT_SPEC] + _ssm_param_specs() + [d_spec],
        out_specs=[u_spec, pg_spec],
        out_shape=[jax.ShapeDtypeStruct((L, C), F32), jax.ShapeDtypeStruct((n_slab, 2, PG_ROWS, SLAB_S), F32)],
        scratch_shapes=_MAP_SCRATCH + [pltpu.VMEM((4, SLAB, SLAB_S), F32)] + [pltpu.VMEM((L, SLAB_S), F32)] * 4,
        compiler_params=_cparams(("arbitrary",), 60 << 20),
    )(u, dy, ent, *prm, dskip)


def _ssm_param_grads(pg, prm):
    n_slab = pg.shape[0]
    G = n_slab * SLAB_G
    pg_spec = pl.BlockSpec((1, 2, PG_ROWS, SLAB_S), lambda j: (j, 0, 0, 0))
    pole, _, step, mat = _ssm_param_specs()[:4]

    def body(pg_ref, a_re_ref, a_im_ref, ldt_ref, bt_re_ref, bt_im_ref,
             dbr_ref, dbi_ref, dcr_ref, dci_ref, dar_ref, dai_ref, dldt_ref, dd_ref):
        dd_ref[...] = pg_ref[0, 0, 66:67, 0:SLAB]
        for d in range(2):
            a_r = a_re_ref[d]
            lr, li, dt, lbr, lbi, f_r, f_i = _discretise(a_r, a_im_ref[d], ldt_ref[d])
            gfr_rows, gfi_rows, glr_rows, gli_rows = [], [], [], []
            for g in range(SLAB_G):
                lanes = slice(g * SSM_P, (g + 1) * SSM_P)
                gbr, gbi = pg_ref[0, d, 0:SSM_H, lanes], pg_ref[0, d, SSM_H:2 * SSM_H, lanes]
                b_r, b_i = bt_re_ref[d, g], bt_im_ref[d, g]
                fr, fi = f_r[g:g + 1], f_i[g:g + 1]
                dbr_ref[d, g] = fr * gbr + fi * gbi
                dbi_ref[d, g] = fr * gbi - fi * gbr
                gfr_rows.append(_colsum(gbr * b_r + gbi * b_i))
                gfi_rows.append(_colsum(gbi * b_r - gbr * b_i))
                dcr_ref[d, g] = pg_ref[0, d, 2 * SSM_H:3 * SSM_H, lanes]
                dci_ref[d, g] = pg_ref[0, d, 3 * SSM_H:4 * SSM_H, lanes]
                glr_rows.append(pg_ref[0, d, 64:65, lanes])
                gli_rows.append(pg_ref[0, d, 65:66, lanes])
            gfr, gfi = jnp.concatenate(gfr_rows, axis=0), jnp.concatenate(gfi_rows, axis=0)
            glr, gli = jnp.concatenate(glr_rows, axis=0), jnp.concatenate(gli_rows, axis=0)
            den = lr * lr + li * li
            ir, ii = lr / den, -li / den
            tr, ti = _cmul(ir, -ii, gfr, gfi)
            glbr, glbi = glr + tr, gli + ti
            qr, qi = _cmul(f_r, f_i, ir, ii)
            dlr, dli = _cmul(-qr, qi, gfr, gfi)
            zr, zi = _cmul(lbr, -lbi, glbr, glbi)
            dlr = dlr + dt * zr
            dli = dli + dt * zi
            dar_ref[d] = jnp.where(a_r < -1e-4, dlr, jnp.where(a_r == -1e-4, 0.5 * dlr, 0.0))
            dai_ref[d] = dli
            dldt_ref[d] = jnp.sum(lr * zr + li * zi, axis=-1, keepdims=True) * dt

    a_re, a_im, ldt, bt_re, bt_im = prm[:5]
    mshape = jax.ShapeDtypeStruct(bt_re.shape, F32)
    pshape = jax.ShapeDtypeStruct(a_re.shape, F32)
    return pl.pallas_call(
        body, name="ssm_param_grads", grid=(n_slab,),
        in_specs=[pg_spec, pole, pole, step, mat, mat],
        out_specs=[mat, mat, mat, mat, pole, pole, step, pl.BlockSpec((1, SLAB), lambda j: (0, j))],
        out_shape=[mshape, mshape, mshape, mshape, pshape, pshape, jax.ShapeDtypeStruct(ldt.shape, F32),
                   jax.ShapeDtypeStruct((1, n_slab * SLAB), F32)],
        compiler_params=_cparams(("arbitrary",)),
    )(pg, a_re, a_im, ldt, bt_re, bt_im)


def _peer(k, x, y, c):
    return (1 - x if k & 4 else x, 1 - y if k & 2 else y, 1 - c if k & 1 else c)


def _dev_index(pos):
    return 4 * pos[0] + 2 * pos[1] + pos[2]


def _chip_index(pos):
    return 2 * pos[0] + pos[1]


def _sibling_swap(x, name):
    any_spec = pl.BlockSpec(memory_space=pl.ANY)

    def body(x_ref, out_ref, send_sems, recv_sems):
        x_, y_, c_ = lax.axis_index("x"), lax.axis_index("y"), lax.axis_index("c")
        copies = [pltpu.make_async_remote_copy(
            src_ref=x_ref.at[2 * chip + (1 - c_)], dst_ref=out_ref.at[chip], send_sem=send_sems.at[chip],
            recv_sem=recv_sems.at[chip], device_id=(x_, y_, 1 - c_), device_id_type=pl.DeviceIdType.MESH)
            for chip in range(N_CHIPS)]
        for cp in copies:
            cp.start()
        for cp in copies:
            cp.wait()

    return pl.pallas_call(
        body, name=name, out_shape=jax.ShapeDtypeStruct((N_CHIPS,) + x.shape[1:], x.dtype),
        in_specs=[any_spec], out_specs=any_spec,
        scratch_shapes=[pltpu.SemaphoreType.DMA((N_CHIPS,)), pltpu.SemaphoreType.DMA((N_CHIPS,))],
    )(x)


def _pair_sum(x, got, name):
    n, R, W = got.shape
    tr = _row_tile(R, W, 5 << 20)
    core = lax.axis_index("c").astype(jnp.int32).reshape(1)

    def body(core_ref, a_ref, b_ref, o_ref):
        o_ref[...] = (a_ref[...].astype(F32) + b_ref[...].astype(F32)).astype(BF16)

    spec = pl.BlockSpec((1, tr, W), lambda i, j, c: (i, j, 0))
    grid_spec = pltpu.PrefetchScalarGridSpec(
        num_scalar_prefetch=1, grid=(n, R // tr),
        in_specs=[pl.BlockSpec((1, tr, W), lambda i, j, c: (2 * i + c[0], j, 0)), spec], out_specs=spec)
    return pl.pallas_call(
        body, name=name, grid_spec=grid_spec, out_shape=jax.ShapeDtypeStruct((n, R, W), BF16),
        compiler_params=_cparams(("arbitrary", "arbitrary")),
    )(core, x, got)


def _all_gather(xs, name):
    n = len(xs)
    any_spec = pl.BlockSpec(memory_space=pl.ANY)

    def body(*refs):
        x_refs, out_refs = refs[:n], refs[n:2 * n]
        send_sems, recv_sems, local_sems = refs[2 * n:]
        x, y, c = lax.axis_index("x"), lax.axis_index("y"), lax.axis_index("c")
        me, sibling = (x, y, c), (x, y, 1 - c)
        chips = [(1 - x, y), (x, 1 - y), (1 - x, 1 - y)]

        def copy(a, k, block, to, src=None):
            dst = out_refs[a].at[_dev_index(block)]
            return pltpu.make_async_remote_copy(
                src_ref=dst if src is None else src, dst_ref=dst,
                send_sem=send_sems.at[a, k], recv_sem=recv_sems.at[a, k],
                device_id=to, device_id_type=pl.DeviceIdType.MESH)

        mine = [pltpu.make_async_copy(x_refs[a], out_refs[a].at[_dev_index(me)], local_sems.at[a]) for a in range(n)]
        for cp in mine:
            cp.start()
        first = []
        for a in range(n):
            first.append(copy(a, 0, me, sibling, src=x_refs[a]))
            first += [copy(a, 1 + j, me, (*chip, c), src=x_refs[a]) for j, chip in enumerate(chips)]
        for cp in first:
            cp.start()
        passed = []
        for j, chip in enumerate(chips):
            for a in range(n):
                copy(a, 1 + j, (*chip, c), me).wait_recv()
                fwd = copy(a, 4 + j, (*chip, c), sibling)
                fwd.start()
                passed.append(fwd)
        for a in range(n):
            copy(a, 0, sibling, me).wait_recv()
            for j, chip in enumerate(chips):
                copy(a, 4 + j, (*chip, 1 - c), me).wait_recv()
        for cp in first + passed:
            cp.wait_send()
        for cp in mine:
            cp.wait()

    return pl.pallas_call(
        body, name=name,
        out_shape=[jax.ShapeDtypeStruct((N_DEV,) + v.shape, v.dtype) for v in xs],
        in_specs=[any_spec] * n, out_specs=[any_spec] * n,
        scratch_shapes=[pltpu.SemaphoreType.DMA((n, 7)), pltpu.SemaphoreType.DMA((n, 7)),
                        pltpu.SemaphoreType.DMA((n,))],
    )(*xs)


def _sum_blocks(x, name):
    _, R, W = x.shape

    def body(x_ref, o_ref):
        acc = x_ref[0].astype(F32)
        for d in range(1, N_DEV):
            acc = acc + x_ref[d].astype(F32)
        o_ref[...] = acc

    return pl.pallas_call(body, name=name, out_shape=jax.ShapeDtypeStruct((R, W), F32),
                          compiler_params=pltpu.CompilerParams(vmem_limit_bytes=VMEM_LIMIT))(x)


def _adam_update(w, g, m, v):
    mn = ADAM_B1 * m + (1.0 - ADAM_B1) * g
    vn = ADAM_B2 * v + (1.0 - ADAM_B2) * (g * g)
    m_hat = mn / (1.0 - ADAM_B1 ** ADAM_STEP)
    v_hat = vn / (1.0 - ADAM_B2 ** ADAM_STEP)
    return -ADAM_LR * (m_hat / (jnp.sqrt(v_hat) + ADAM_EPS) + ADAM_WD * w), mn, vn


def _row_tile(R, W, budget):
    padded = -(-W // LANES) * LANES * 4
    if R * padded <= budget:
        return R
    return _pick(R, [t for t in (2048, 1024, 512, 256, 128, 64, 32, 16, 8) if t * padded <= budget])


def _adamw(w, g, m, v, name):
    R, W = w.shape
    tr = _row_tile(R, W, 1 << 20)

    def body(w_ref, g_ref, m_ref, v_ref, d_ref, mo_ref, vo_ref):
        d_ref[...], mo_ref[...], vo_ref[...] = _adam_update(w_ref[...], g_ref[...], m_ref[...], v_ref[...])

    spec = pl.BlockSpec((tr, W), lambda i: (i, 0))
    shp = jax.ShapeDtypeStruct((R, W), F32)
    return pl.pallas_call(
        body, name=name, grid=(R // tr,), in_specs=[spec] * 4, out_specs=[spec] * 3, out_shape=[shp] * 3,
        compiler_params=_cparams(("arbitrary",)),
    )(w, g, m, v)


def _adamw_group(ws, gs, ms, vs, name):
    n = len(ws)

    def body(*refs):
        ins, outs = refs[:4 * n], refs[4 * n:]
        for i in range(n):
            w_ref, g_ref, m_ref, v_ref = ins[i], ins[n + i], ins[2 * n + i], ins[3 * n + i]
            outs[i][...], outs[n + i][...], outs[2 * n + i][...] = _adam_update(
                w_ref[...], g_ref[...], m_ref[...], v_ref[...])

    shapes = [jax.ShapeDtypeStruct(w.shape, F32) for w in ws]
    outs = pl.pallas_call(body, name=name, out_shape=shapes * 3)(*ws, *gs, *ms, *vs)
    return outs[:n], outs[n:2 * n], outs[2 * n:]


def _adamw_reduce(w, land, m, v, name):
    R, W = w.shape
    n = land.shape[0]
    tr = _row_tile(R, W, 1 << 20)

    def body(w_ref, l_ref, m_ref, v_ref, g_ref, d_ref, mo_ref, vo_ref):
        g = l_ref[0].astype(F32)
        for d in range(1, n):
            g = g + l_ref[d].astype(F32)
        g_ref[...] = g
        d_ref[...], mo_ref[...], vo_ref[...] = _adam_update(w_ref[...], g, m_ref[...], v_ref[...])

    spec = pl.BlockSpec((tr, W), lambda i: (i, 0))
    lspec = pl.BlockSpec((n, tr, W), lambda i: (0, i, 0))
    shp = jax.ShapeDtypeStruct((R, W), F32)
    return pl.pallas_call(
        body, name=name, grid=(R // tr,), in_specs=[spec, lspec, spec, spec], out_specs=[spec] * 4,
        out_shape=[shp] * 4, compiler_params=_cparams(("arbitrary",)),
    )(w, land, m, v)


def _gelu(v):
    c = math.sqrt(2.0 / math.pi)
    return 0.5 * v * (1.0 + jnp.tanh(c * (v + 0.044715 * v * v * v)))


def _gelu_grad(v):
    c = math.sqrt(2.0 / math.pi)
    t = jnp.tanh(c * (v + 0.044715 * v * v * v))
    return 0.5 * (1.0 + t) + 0.5 * v * (1.0 - t * t) * c * (1.0 + 3.0 * 0.044715 * v * v)


def kernel(x, p, norm_mix, w_in, q_norm, k_norm, ssm_a_re, ssm_a_im, ssm_log_dt, ssm_b_re, ssm_b_im, ssm_c_re, ssm_c_im, ssm_d, w_glu, b_glu, w_out, norm_ple, w_ple_gate, w_ple_proj, norm_final, loss_target, m_norm_mix, m_w_in, m_q_norm, m_k_norm, m_ssm_a_re, m_ssm_a_im, m_ssm_log_dt, m_ssm_b_re, m_ssm_b_im, m_ssm_c_re, m_ssm_c_im, m_ssm_d, m_w_glu, m_b_glu, m_w_out, m_norm_ple, m_w_ple_gate, m_w_ple_proj, m_norm_final, v_norm_mix, v_w_in, v_q_norm, v_k_norm, v_ssm_a_re, v_ssm_a_im, v_ssm_log_dt, v_ssm_b_re, v_ssm_b_im, v_ssm_c_re, v_ssm_c_im, v_ssm_d, v_w_glu, v_b_glu, v_w_out, v_norm_ple, v_w_ple_gate, v_w_ple_proj, v_norm_final):
    L, D = x.shape[1], x.shape[2]
    D_SSM = ssm_d.shape[1]
    G = D_SSM // SSM_H
    n_slab = D_SSM // SLAB
    n_in = w_in.shape[2]
    D_IN = n_in * N_DEV
    n_pp = w_ple_proj.shape[2]
    n_glu = w_glu.shape[2]
    xs = x[0]
    ps = p[0, 0]
    tgt = loss_target[0]

    (win_t3,) = _all_gather([w_in[0].T.astype(BF16)], "gather_w_in")
    win_t = win_t3.reshape(D_IN, D)
    later_weights = _Carry("gather", [w_glu[0].astype(BF16), w_out[0].astype(BF16), w_ple_gate[0].astype(BF16),
                                      w_ple_proj[0].astype(BF16)])

    ssm_prm = (ssm_a_re[0], ssm_a_im[0], ssm_log_dt[0].reshape(2, G, 1),
               ssm_b_re[0].transpose(0, 1, 3, 2), ssm_b_im[0].transpose(0, 1, 3, 2), ssm_c_re[0], ssm_c_im[0])

    cos, sin = _rope_tables(L)
    hn = _norm_in(xs, norm_mix, "norm_mix")
    ZT = 512
    zp_tile = lambda j: jnp.where(j < 2, j, jnp.where(j < D_IN // ZT - 1, j + 1, 2))
    z = _mm(hn, win_t, "nt", "in_proj", out_dtype=BF16, n_tiles=(ZT, zp_tile))
    qr, kr, vb, kt = _qkv_prep(z, cos, sin, q_norm, k_norm)
    o, lse, (wglu3, wout3, wpg3, wpp3) = _attn_fwd(qr, kr, vb, later_weights)
    wout = wout3.reshape(-1, D)
    wpg = wpg3.reshape(-1, D)
    u_off = 2 * D_ATTN
    u_perm = _seg_perm(z[:, u_off:u_off + D_SSM])
    ys_perm, ssm_ent = _ssm_fwd(u_perm, ssm_prm, ssm_d)
    ys = _seg_unperm(ys_perm)

    tm = _pick(L, (256,))
    wglu = wglu3.transpose(1, 0, 2).reshape(D_SSM, 2 * D_SSM)

    def mix_post(prod, gy_tile, o_ref, ga_ref, gs_ref, b_ref):
        glu_b = (prod + b_ref[...]).astype(BF16)
        gluv = glu_b.astype(F32)
        ga, gs = _f32(ga_ref), _f32(gs_ref)
        y_attn = _f32(o_ref) * ga * _sigmoid(ga)
        y_ssm = gluv[:, :D_SSM] * _sigmoid(gluv[:, D_SSM:]) * gs * _sigmoid(gs)
        return gy_tile, glu_b, jnp.concatenate([y_attn, y_ssm], axis=-1)

    gy, glu, cat = _mm_rows(ys, wglu, "nn", "mix", mix_post, [o, (z, D_ATTN, 1), (z, D_SSM, 3)], [b_glu],
                            [(D_SSM, BF16), (2 * D_SSM, BF16), (D_ATTN + D_SSM, BF16)],
                            pre=lambda y: _gelu(y).astype(BF16), tm=512)

    def out_post(prod, x_ref, g_ref):
        h1v = prod + x_ref[...]
        return h1v, h1v * _rms(h1v) * g_ref[...]

    h1, n2 = _mm_rows(cat, wout, "nn", "out_proj", out_post, [xs], [norm_ple], [(D, F32), (D, BF16)], tm=512)
    pb = ps.astype(BF16)
    pp = _mm(pb, wpp3, "nn", "ple_proj", out_dtype=BF16, b_blk=True)

    nf = norm_final.reshape(1, D)

    def tail_post(gp, h1_ref, pp_ref, t_ref, g_ref):
        gate = _sigmoid(gp)
        ppv = _f32(pp_ref)
        h2 = h1_ref[...] + gate * ppv
        r = _rms(h2)
        hh = h2 * r
        err = hh * g_ref[...] - t_ref[...]
        loss_part = jnp.broadcast_to(0.5 * jnp.sum(jnp.mean(err * err, axis=-1, keepdims=True)), (1, LANES))
        dy = err * (1.0 / D)
        dh2 = _rms_bwd(dy, hh, r, g_ref[...])
        return dh2, dh2 * gate, dh2 * ppv * gate * (1.0 - gate), loss_part, _colsum(dy * hh)

    dh2, dpp, dsg, loss_acc, d_nf = _mm_rows(n2, wpg, "nn", "tail", tail_post, [h1, pp, tgt], [nf],
                                             [(D, F32), (D, BF16), (D, BF16)], [(1, LANES), (1, D)], vmem=58 << 20)

    g_wpp3 = _mm(pb, dpp, "tn", "d_ple_proj", out_dtype=BF16, out_blk=n_pp)
    g_wpg = _mm(n2, dsg, "tn", "d_ple_gate", out_dtype=BF16)
    def ple_bwd_post(dn, h1_ref, dh2_ref, g_ref):
        h1v = h1_ref[...]
        r = _rms(h1v)
        hh = h1v * r
        dh1 = dh2_ref[...] + _rms_bwd(dn, hh, r, g_ref[...])
        return dh1, dh1, _colsum(dn * hh)

    dh1, dh1b, d_nple = _mm_rows(dsg, wpg, "nt", "ple_bwd", ple_bwd_post, [h1, dh2], [norm_ple],
                                 [(D, F32), (D, BF16)], [(1, D)], tm=512, vmem=58 << 20)

    g_wout = _mm(cat, dh1b, "tn", "d_out_proj", out_dtype=BF16)

    def mix_bwd_post(dcat, o_ref, ga_ref, glu_ref, gs_ref):
        dca, dcs = dcat[:, :D_ATTN], dcat[:, D_ATTN:]
        ga, gs = _f32(ga_ref), _f32(gs_ref)
        gla, glb = glu_ref[:, :D_SSM].astype(F32), glu_ref[:, D_SSM:].astype(F32)
        sa, ss, sb = _sigmoid(ga), _sigmoid(gs), _sigmoid(glb)
        do = dca * ga * sa
        dga = dca * _f32(o_ref) * sa * (1.0 + ga * (1.0 - sa))
        dgs = dcs * gla * sb * ss * (1.0 + gs * (1.0 - ss))
        dy2 = dcs * gs * ss
        da, db = dy2 * sb, dy2 * gla * sb * (1.0 - sb)
        return (do, dga, dgs, jnp.concatenate([da, db], axis=-1),
                jnp.concatenate([_colsum(da), _colsum(db)], axis=-1))

    do, dga, dgs, dglu, g_bglu = _mm_rows(
        dh1b, wout, "nt", "mix_bwd", mix_bwd_post, [o, (z, D_ATTN, 1), glu, (z, D_SSM, 3)], [],
        [(D_ATTN, BF16), (D_ATTN, BF16), (D_SSM, BF16), (2 * D_SSM, BF16)], [(1, 2 * D_SSM)])

    g_wglu3 = _mm(gy, dglu, "tn", "d_glu_proj", out_dtype=BF16, out_blk=n_glu)
    dys = _mm(dglu, wglu3, "nt", "d_ssm_out", b_blk=True,
              post=(lambda out, y: out * _gelu_grad(y), ys))
    du_perm, pg = _ssm_bwd(u_perm, _seg_perm(dys), ssm_ent, ssm_prm, ssm_d)
    du = _seg_unperm(du_perm)

    pg_send = pg.reshape(N_DEV, (n_slab // N_DEV) * 2 * PG_ROWS, SLAB_S)
    dqs, dkr, dvv, (l_wglu, l_wout, l_wpg, l_wpp, l_pg) = _attn_bwd(
        qr, kr, vb, kt, do, o, lse,
        _Carry("a2a", [g_wglu3, g_wout.reshape(N_DEV, -1, D), g_wpg.reshape(N_DEV, -1, D), g_wpp3, pg_send]))

    scale = HEAD_DIM ** -0.5
    kblk = 4 * D_ATTN // D_KV

    a0, k0, v0, u0, s0 = D_ATTN + 2 * D_KV, D_ATTN, D_ATTN + D_KV, 2 * D_ATTN + 2 * D_KV, 2 * D_ATTN + 2 * D_KV + D_SSM

    def qkv_bwd_body(dq_ref, dk_ref, dv_ref, q_ref, k_ref, cos_ref, sin_ref, qn_ref, kn_ref, dga_ref, du_ref, dgs_ref,
                     dz_ref, dqn_ref, dkn_ref):
        c, s = cos_ref[...], sin_ref[...]
        dz_ref[:, a0:a0 + D_ATTN] = dga_ref[...]
        dz_ref[:, u0:u0 + D_SSM] = du_ref[...].astype(BF16)
        dz_ref[:, s0:s0 + D_SSM] = dgs_ref[...]

        def head(g, xh, w):
            dn = g * c + _partner(g * s)
            r = _rms(xh)
            xhat = xh * r
            return _rms_bwd(dn, xhat, r, w), _colsum(dn * xhat)

        dqn = jnp.zeros((1, HEAD_DIM), F32)
        for h in range(N_HEADS):
            sl = slice(h * HEAD_DIM, (h + 1) * HEAD_DIM)
            dx, dw = head(dq_ref[:, sl] * scale, q_ref[:, sl].astype(F32), qn_ref[...])
            dz_ref[:, sl] = dx.astype(BF16)
            dqn = dqn + dw
        dkn = jnp.zeros((1, HEAD_DIM), F32)
        for h in range(N_KV):
            sl = slice(h * HEAD_DIM, (h + 1) * HEAD_DIM)
            dx, dw = head(dk_ref[:, sl], k_ref[:, sl].astype(F32), kn_ref[...])
            dz_ref[:, k0 + h * HEAD_DIM:k0 + (h + 1) * HEAD_DIM] = dx.astype(BF16)
            dkn = dkn + dw
        dz_ref[:, v0:v0 + D_KV] = dv_ref[...].astype(BF16)
        _acc(dqn_ref, dqn)
        _acc(dkn_ref, dkn)

    dz, g_qn, g_kn = _rowcall(
        qkv_bwd_body, "qkv_bwd", L, tm,
        [(dqs, _rspec(tm, D_ATTN)), (dkr, _rspec(tm, D_KV)), (dvv, _rspec(tm, D_KV)),
         (z, _rspec(tm, D_ATTN, 0)), (z, _rspec(tm, D_KV, kblk)), (cos, _rspec(tm, HEAD_DIM)),
         (sin, _rspec(tm, HEAD_DIM)), (q_norm, _fspec(q_norm.shape)), (k_norm, _fspec(k_norm.shape)),
         (dga, _rspec(tm, D_ATTN)), (du, _rspec(tm, D_SSM)), (dgs, _rspec(tm, D_SSM))],
        [(D_IN, BF16)], [(1, HEAD_DIM), (1, HEAD_DIM)])

    g_win_t = _mm(dz, hn, "tn", "d_in_proj", out_dtype=BF16)
    g_win8 = g_win_t.reshape(N_DEV, n_in, D)
    from_sibling = _sibling_swap(g_win8, "swap_d_w_in")
    pair = _pair_sum(g_win8, from_sibling, "pair_sum_d_w_in")
    dhn, (l_win_t,) = _mm(dz, win_t, "nn", "d_norm_mix_in", out_dtype=BF16,
                          carry=_Carry("a2a_chips", [pair]))

    def in_bwd_body(x_ref, dn_ref, dh1_ref, g_ref, dx_ref, dg_ref):
        xv = x_ref[...]
        r = _rms(xv)
        hh = xv * r
        dn = _f32(dn_ref)
        _acc(dg_ref, _colsum(dn * hh))
        dx_ref[...] = dh1_ref[...] + _rms_bwd(dn, hh, r, g_ref[...])

    grad_x, g_nmix = _rowcall(
        in_bwd_body, "in_bwd", L, tm,
        [(xs, _rspec(tm, D)), (dhn, _rspec(tm, D)), (dh1, _rspec(tm, D)), (norm_mix, _fspec(norm_mix.shape))],
        [(D, F32)], [(1, D)])

    tiny_parts = [g_nmix, g_bglu, d_nple, d_nf, g_qn, g_kn, loss_acc[:, :1]]
    tiny_flat = jnp.concatenate([t.reshape(-1) for t in tiny_parts])
    tiny_rows = -(-tiny_flat.shape[0] // (8 * LANES)) * 8
    tiny = jnp.pad(tiny_flat, (0, tiny_rows * LANES - tiny_flat.shape[0])).reshape(tiny_rows, LANES)
    pg_sum = _sum_blocks(l_pg, "sum_ssm_grads")
    pg_all, tiny_all = _all_gather([pg_sum, tiny], "gather_small_grads")
    (g_bt_re, g_bt_im, g_c_re, g_c_im, g_a_re, g_a_im, g_ldt, g_skip) = _ssm_param_grads(
        pg_all.reshape(n_slab, 2, PG_ROWS, SLAB_S), ssm_prm)
    tiny_sum = _sum_blocks(tiny_all, "sum_tiny_grads").reshape(-1)
    tiny_grads, off = [], 0
    for t in tiny_parts:
        tiny_grads.append(tiny_sum[off:off + t.size].reshape(t.shape))
        off += t.size
    r_nmix, r_bglu, r_nple, r_nf, r_qn, r_kn, loss = tiny_grads
    loss = loss.reshape(())

    grads, deltas, new_ms, new_vs = {}, {}, {}, {}
    outs = _adamw_reduce(w_in[0].T, l_win_t, m_w_in[0].T, v_w_in[0].T, "adamw_w_in")
    grads["w_in"], deltas["w_in"], new_ms["w_in"], new_vs["w_in"] = [t.T[None] for t in outs]
    big = [("w_glu", w_glu, l_wglu, m_w_glu, v_w_glu),
           ("w_out", w_out, l_wout, m_w_out, v_w_out), ("w_ple_gate", w_ple_gate, l_wpg, m_w_ple_gate, v_w_ple_gate),
           ("w_ple_proj", w_ple_proj, l_wpp, m_w_ple_proj, v_w_ple_proj)]
    for name, w, ld, m, v in big:
        shp = w.shape
        outs = _adamw_reduce(w[0], ld, m[0], v[0], "adamw_" + name)
        grads[name], deltas[name], new_ms[name], new_vs[name] = [t.reshape(shp) for t in outs]
    bt2 = (2 * G * SSM_H, SSM_P)
    for name, w, g, m, v in (("ssm_b_re", ssm_b_re, g_bt_re, m_ssm_b_re, v_ssm_b_re),
                             ("ssm_b_im", ssm_b_im, g_bt_im, m_ssm_b_im, v_ssm_b_im)):
        to2 = lambda t: t[0].transpose(0, 1, 3, 2).reshape(bt2)
        back = lambda t: t.reshape(2, G, SSM_H, SSM_P).transpose(0, 1, 3, 2)[None]
        outs = _adamw(to2(w), g.reshape(bt2), to2(m), to2(v), "adamw_" + name)
        grads[name] = back(g)
        deltas[name], new_ms[name], new_vs[name] = [back(t) for t in outs]
    small = [("norm_mix", norm_mix, r_nmix, m_norm_mix, v_norm_mix, (1, D)),
             ("q_norm", q_norm, r_qn, m_q_norm, v_q_norm, (1, HEAD_DIM)),
             ("k_norm", k_norm, r_kn, m_k_norm, v_k_norm, (1, HEAD_DIM)),
             ("ssm_a_re", ssm_a_re, g_a_re, m_ssm_a_re, v_ssm_a_re, (2 * G, SSM_P)),
             ("ssm_a_im", ssm_a_im, g_a_im, m_ssm_a_im, v_ssm_a_im, (2 * G, SSM_P)),
             ("ssm_log_dt", ssm_log_dt, g_ldt, m_ssm_log_dt, v_ssm_log_dt, (2, G)),
             ("ssm_c_re", ssm_c_re, g_c_re, m_ssm_c_re, v_ssm_c_re, (2 * G * SSM_H, SSM_P)),
             ("ssm_c_im", ssm_c_im, g_c_im, m_ssm_c_im, v_ssm_c_im, (2 * G * SSM_H, SSM_P)),
             ("ssm_d", ssm_d, g_skip, m_ssm_d, v_ssm_d, (1, D_SSM)),
             ("b_glu", b_glu, r_bglu, m_b_glu, v_b_glu, (1, 2 * D_SSM)),
             ("norm_ple", norm_ple, r_nple, m_norm_ple, v_norm_ple, (1, D)),
             ("norm_final", norm_final, r_nf, m_norm_final, v_norm_final, (1, D))]
    group = [it for it in small if it[5][0] * it[5][1] <= (1 << 14)]
    grouped = {it[0] for it in group}
    for name, w, g, m, v, s2 in small:
        if name in grouped:
            continue
        shp = w.shape
        outs = _adamw(w.reshape(s2), g.reshape(s2), m.reshape(s2), v.reshape(s2), "adamw_" + name)
        grads[name] = g.reshape(shp)
        deltas[name], new_ms[name], new_vs[name] = [t.reshape(shp) for t in outs]
    ds, mns, vns = _adamw_group(*[[it[i].reshape(it[5]) for it in group] for i in (1, 2, 3, 4)], "adamw_tiny")
    for (name, w, g, _, _, _), d_, m_, v_ in zip(group, ds, mns, vns):
        shp = w.shape
        grads[name] = g.reshape(shp)
        deltas[name], new_ms[name], new_vs[name] = d_.reshape(shp), m_.reshape(shp), v_.reshape(shp)

    order = ["norm_mix", "w_in", "q_norm", "k_norm", "ssm_a_re", "ssm_a_im", "ssm_log_dt", "ssm_b_re", "ssm_b_im",
             "ssm_c_re", "ssm_c_im", "ssm_d", "w_glu", "b_glu", "w_out", "norm_ple", "w_ple_gate", "w_ple_proj",
             "norm_final"]
    return (loss, grad_x[None], *[grads[k] for k in order], *[deltas[k] for k in order],
            *[new_ms[k] for k in order], *[new_vs[k] for k in order])
```
